```python
import math
import jax, jax.numpy as jnp
from jax import lax
import numpy as np

D_MODEL = 2048
BATCH = 2
SEQ = 16384
DEPTH = 2

GRID_W = 64
CTX_LEN = 256
HEAD_DIM = 128
A_HEADS = D_MODEL // (2 * HEAD_DIM)
A_HALF = HEAD_DIM // 2
B_HEADS = D_MODEL // (2 * HEAD_DIM)
B_KV_HEADS = 2
B_GROUP = B_HEADS // B_KV_HEADS
WINDOW = 128
BLOCK = 128
ROPE_BASE = 10000.0
D_MIX = (A_HEADS + B_HEADS) * HEAD_DIM
Q_COLS = (A_HEADS + B_HEADS) * HEAD_DIM
KV_COLS = 2 * A_HEADS * HEAD_DIM + 2 * B_KV_HEADS * HEAD_DIM
KV_SPLITS = [A_HEADS * HEAD_DIM, 2 * A_HEADS * HEAD_DIM, 2 * A_HEADS * HEAD_DIM + B_KV_HEADS * HEAD_DIM]
IN_ATTN = Q_COLS + KV_COLS
HY_ORDER = 2
HY_EMB = 33
HY_FILTER_W = 64
HY_SHORT = 3
HY_DECAY_TARGET = 1e-2
HY_MAX_DECAY_PCT = 0.3
HY_MIN_DECAY_PCT = 1.5
D_FF = 5632
FFN_CONV = 3
EPS = 1e-6
NEG = -1e30

kernel_name = 'hybrid_diffattn_swa_hyena_dit'


def _rms(x, g):
    xf = x.astype(jnp.float32)
    y = xf * lax.rsqrt(jnp.mean(xf * xf, axis=-1, keepdims=True) + EPS)
    return (y * g.astype(jnp.float32)).astype(x.dtype)


def _dwconv(z, w, b):
    K = w.shape[0]
    pad = (K - 1) // 2
    L = z.shape[1]
    zp = jnp.pad(z, ((0, 0), (pad, pad), (0, 0)))
    out = zp[:, 0:L] * w[0] + b
    for k in range(1, K):
        out = out + zp[:, k:k + L] * w[k]
    return out


def _axial_rope_tables(L, dim):
    rows = L // GRID_W
    row = jnp.repeat(jnp.arange(rows, dtype=jnp.float32), GRID_W)
    col = jnp.tile(jnp.arange(GRID_W, dtype=jnp.float32), rows)
    n_f = dim // 4
    inv = ROPE_BASE ** (-jnp.arange(n_f, dtype=jnp.float32) / n_f)
    ang = jnp.stack([row[:, None] * inv, col[:, None] * inv], axis=1)
    return jnp.cos(ang), jnp.sin(ang)


def _apply_rope(x, cos, sin):
    shp = x.shape
    n_f = shp[-1] // 4
    xr = x.astype(jnp.float32).reshape(shp[:-1] + (2, 2, n_f))
    x1, x2 = xr[..., 0, :], xr[..., 1, :]
    bshape = (shp[1],) + (1,) * (x.ndim - 3) + (2, n_f)
    c = cos.reshape(bshape)
    s = sin.reshape(bshape)
    out = jnp.stack([x1 * c - x2 * s, x2 * c + x1 * s], axis=-2)
    return out.reshape(shp).astype(x.dtype)


def _q_heads(t):
    B_, n = t.shape[:2]
    qa, qb = jnp.split(t, [A_HEADS * HEAD_DIM], axis=-1)
    return (qa.reshape(B_, n, A_HEADS, 2, A_HALF),
            qb.reshape(B_, n, B_KV_HEADS, B_GROUP, HEAD_DIM))


def _kv_heads(t):
    B_, n = t.shape[:2]
    ka, va, kb, vb = jnp.split(t, KV_SPLITS, axis=-1)
    return (ka.reshape(B_, n, A_HEADS, 2, A_HALF), va.reshape(B_, n, A_HEADS, HEAD_DIM),
            kb.reshape(B_, n, B_KV_HEADS, HEAD_DIM), vb.reshape(B_, n, B_KV_HEADS, HEAD_DIM))


def _diff_softmax(q, k, v, lam):
    s = jnp.einsum('bqhmd,bkhmd->bhmqk', q, k, preferred_element_type=jnp.float32)
    p = jax.nn.softmax(s, axis=-1)
    w = p[:, :, 0] - lam * p[:, :, 1]
    return jnp.einsum('bhqk,bkhd->bqhd', w.astype(v.dtype), v)


def _diff_attn_blocks(q, k_all, v_all, lam):
    B_, L = q.shape[:2]
    nb = L // BLOCK
    qblocks = jnp.moveaxis(q.reshape((B_, nb, BLOCK) + q.shape[2:]), 1, 0)
    out = lax.map(lambda qi: _diff_softmax(qi, k_all, v_all, lam), qblocks)
    return jnp.moveaxis(out, 0, 1).reshape(B_, L, A_HEADS, HEAD_DIM)


def _sink_softmax(q, kc, vc, sink):
    s = jnp.einsum('bqgrd,bcgd->bgrqc', q, kc, preferred_element_type=jnp.float32)
    s_sink = jnp.broadcast_to(sink.astype(jnp.float32)[None, :, :, None, None], s.shape[:-1] + (1,))
    p = jax.nn.softmax(jnp.concatenate([s, s_sink], axis=-1), axis=-1)[..., :kc.shape[1]]
    return jnp.einsum('bgrqc,bcgd->bqgrd', p.astype(vc.dtype), vc)


def _window_attn_blocks(q, k, v, kc, vc, sink):
    B_, L = q.shape[:2]
    nb = L // BLOCK
    C = kc.shape[1]
    pad = ((0, 0), (BLOCK, BLOCK), (0, 0), (0, 0))
    kp = jnp.pad(k, pad)
    vp = jnp.pad(v, pad)
    qblocks = jnp.moveaxis(q.reshape((B_, nb, BLOCK) + q.shape[2:]), 1, 0)
    offs_q = jnp.arange(BLOCK)
    offs_k = jnp.arange(3 * BLOCK) - BLOCK

    def one(args):
        i, qi = args
        start = i * BLOCK
        kb = lax.dynamic_slice_in_dim(kp, start, 3 * BLOCK, axis=1)
        vb = lax.dynamic_slice_in_dim(vp, start, 3 * BLOCK, axis=1)
        qpos = start + offs_q
        kpos = start + offs_k
        valid = ((jnp.abs(kpos[None, :] - qpos[:, None]) <= WINDOW)
                 & (kpos >= 0)[None, :] & (kpos < L)[None, :])
        s_lat = jnp.where(valid, jnp.einsum('bqgrd,bkgd->bgrqk', qi, kb,
                                            preferred_element_type=jnp.float32), NEG)
        s_ctx = jnp.einsum('bqgrd,bcgd->bgrqc', qi, kc, preferred_element_type=jnp.float32)
        s_sink = jnp.broadcast_to(sink.astype(jnp.float32)[None, :, :, None, None], s_ctx.shape[:-1] + (1,))
        prob = jax.nn.softmax(jnp.concatenate([s_lat, s_ctx, s_sink], axis=-1), axis=-1)
        p_lat = prob[..., :3 * BLOCK].astype(v.dtype)
        p_ctx = prob[..., 3 * BLOCK:3 * BLOCK + C].astype(v.dtype)
        return (jnp.einsum('bgrqk,bkgd->bqgrd', p_lat, vb)
                + jnp.einsum('bgrqc,bcgd->bqgrd', p_ctx, vc))

    out = lax.map(one, (jnp.arange(nb), qblocks))
    return jnp.moveaxis(out, 0, 1).reshape(B_, L, B_HEADS, HEAD_DIM)


def _attn_mixer(h, hc, ctx_out, lam_init, w_in, w_out, a_q_g, a_k_g, lq1, lk1, lq2, lk2,
                a_sub_g, b_q_g, b_k_g, b_sink):
    B_, L, _ = h.shape
    f32 = jnp.float32
    lam = (jnp.exp(jnp.sum(lq1.astype(f32) * lk1.astype(f32)))
           - jnp.exp(jnp.sum(lq2.astype(f32) * lk2.astype(f32))) + lam_init)
    sink = b_sink.reshape(B_KV_HEADS, B_GROUP)
    cos_a, sin_a = _axial_rope_tables(L, A_HALF)
    cos_b, sin_b = _axial_rope_tables(L, HEAD_DIM)
    p = h @ w_in
    qa, qb = _q_heads(p[..., :Q_COLS])
    ka, va, kb, vb = _kv_heads(p[..., Q_COLS:])
    qa = _apply_rope(_rms(qa, a_q_g), cos_a, sin_a) * A_HALF ** -0.5
    ka = _apply_rope(_rms(ka, a_k_g), cos_a, sin_a)
    qb = _apply_rope(_rms(qb, b_q_g), cos_b, sin_b) * HEAD_DIM ** -0.5
    kb = _apply_rope(_rms(kb, b_k_g), cos_b, sin_b)
    pc = hc @ (w_in if ctx_out else w_in[:, Q_COLS:])
    kac, vac, kbc, vbc = _kv_heads(pc[..., -KV_COLS:])
    kac = _rms(kac, a_k_g)
    kbc = _rms(kbc, b_k_g)
    k_all = jnp.concatenate([kac, ka], axis=1)
    v_all = jnp.concatenate([vac, va], axis=1)
    oa = _rms(_diff_attn_blocks(qa, k_all, v_all, lam), a_sub_g) * (1.0 - lam_init)
    ob = _window_attn_blocks(qb, kb, vb, kbc, vbc, sink)
    y = jnp.concatenate([oa.reshape(B_, L, -1), ob.reshape(B_, L, -1)], axis=-1) @ w_out
    yc = None
    if ctx_out:
        C = hc.shape[1]
        qac, qbc = _q_heads(pc[..., :Q_COLS])
        qac = _rms(qac, a_q_g) * A_HALF ** -0.5
        qbc = _rms(qbc, b_q_g) * HEAD_DIM ** -0.5
        oac = _rms(_diff_softmax(qac, kac, vac, lam), a_sub_g) * (1.0 - lam_init)
        obc = _sink_softmax(qbc, kbc, vbc, sink)
        yc = jnp.concatenate([oac.reshape(B_, C, -1), obc.reshape(B_, C, -1)], axis=-1) @ w_out
    return y, yc


def _hyena_filters(L, f_w0, f_b0, f_w1, f_b1, f_w2, f_b2, f_freq, f_w3):
    f32 = jnp.float32
    t = jnp.linspace(0.0, 1.0, L, dtype=f32)[:, None]
    bands = (HY_EMB - 1) // 2
    w = 2.0 * math.pi * jnp.arange(L, dtype=f32)[:, None] / L
    f = jnp.linspace(1e-4, bands - 1, bands, dtype=f32)[None, :]
    z = jnp.concatenate([t, jnp.cos(f * w), -jnp.sin(f * w)], axis=-1)
    freq = f_freq.astype(f32)
    a = jnp.sin(freq * (z @ f_w0.astype(f32) + f_b0.astype(f32)))
    a = jnp.sin(freq * (a @ f_w1.astype(f32) + f_b1.astype(f32)))
    a = jnp.sin(freq * (a @ f_w2.astype(f32) + f_b2.astype(f32)))
    hf = (a @ f_w3.astype(f32)).reshape(L, HY_ORDER, 2, D_MODEL)
    max_decay = math.log(HY_DECAY_TARGET) / HY_MAX_DECAY_PCT
    min_decay = math.log(HY_DECAY_TARGET) / HY_MIN_DECAY_PCT
    deltas = jnp.linspace(min_decay, max_decay, D_MODEL, dtype=f32)
    decay = jnp.exp(-t * jnp.abs(deltas))
    return hf * decay[:, None, None, :]


def _bidir_long_conv(u, h_fwd, h_bwd, d_skip):
    L, D = u.shape[1], u.shape[2]
    filt = jnp.concatenate([h_fwd, jnp.zeros((1, D), h_fwd.dtype), h_bwd[:0:-1]], axis=0)
    filt = filt / jnp.sum(jnp.abs(filt), axis=0, keepdims=True)
    n = 2 * L
    uf = jnp.fft.rfft(u.astype(jnp.float32), n=n, axis=1)
    ff = jnp.fft.rfft(filt, n=n, axis=0)
    y = jnp.fft.irfft(uf * ff[None], n=n, axis=1)[:, :L]
    return (y + u.astype(jnp.float32) * d_skip.astype(jnp.float32)).astype(u.dtype)


def _hyena(h, w_in, b_in, sconv_w, sconv_b, f_w0, f_b0, f_w1, f_b1, f_w2, f_b2, f_freq, f_w3,
           d_skip, w_out):
    L = h.shape[1]
    z = _dwconv(h @ w_in + b_in, sconv_w, sconv_b)
    v, x1, x2 = jnp.split(z, 3, axis=-1)
    filt = _hyena_filters(L, f_w0, f_b0, f_w1, f_b1, f_w2, f_b2, f_freq, f_w3)
    y = v
    for o, gate in enumerate((x1, x2)):
        y = gate * _bidir_long_conv(y, filt[:, o, 0], filt[:, o, 1], d_skip[o])
    return y @ w_out


def _conv_ffn(h, w_gate, w_val, conv_w, conv_b, w_down):
    g = _dwconv(h @ w_gate, conv_w, conv_b)
    return (jax.nn.gelu(g) * (h @ w_val)) @ w_down


def _ctx_read_later(l):
    return any(j % 2 == 0 for j in range(l + 1, DEPTH))


def setup_inputs(seed: int = 0) -> dict:
    key = jax.random.key(seed)
    keys = iter(jax.random.split(key, 64))

    def nrm(shape, scale):
        return jax.random.normal(next(keys), shape, jnp.float32) * scale

    def gain(shape):
        return 1.0 + nrm(shape, 0.02)

    NA = (DEPTH + 1) // 2
    NH = DEPTH // 2
    D = D_MODEL
    return {
        'x': nrm((BATCH, SEQ, D), 1.0),
        'c': nrm((BATCH, D), 1.0),
        'ctx': nrm((BATCH, CTX_LEN, D), 1.0),
        'c_ctx': nrm((D,), 1.0),
        'ada_w': nrm((DEPTH, D, 6 * D), 0.5 * D ** -0.5),
        'ada_b': nrm((DEPTH, 6 * D), 0.02),
        'norm1_g': gain((DEPTH, D)),
        'norm2_g': gain((DEPTH, D)),
        'attn_w_in': nrm((NA, D, IN_ATTN), D ** -0.5),
        'attn_w_out': nrm((NA, D_MIX, D), D_MIX ** -0.5),
        'a_q_g': gain((NA, A_HALF)),
        'a_k_g': gain((NA, A_HALF)),
        'a_lam_q1': nrm((NA, A_HALF), 0.1),
        'a_lam_k1': nrm((NA, A_HALF), 0.1),
        'a_lam_q2': nrm((NA, A_HALF), 0.1),
        'a_lam_k2': nrm((NA, A_HALF), 0.1),
        'a_sub_g': gain((NA, HEAD_DIM)),
        'b_q_g': gain((NA, HEAD_DIM)),
        'b_k_g': gain((NA, HEAD_DIM)),
        'b_sink': nrm((NA, B_HEADS), 0.5),
        'hy_w_in': nrm((NH, D, 3 * D), D ** -0.5),
        'hy_b_in': nrm((NH, 3 * D), 0.02),
        'hy_sconv_w': nrm((NH, HY_SHORT, 3 * D), HY_SHORT ** -0.5),
        'hy_sconv_b': nrm((NH, 3 * D), 0.02),
        'hy_f_w0': nrm((NH, HY_EMB, HY_FILTER_W), HY_EMB ** -0.5),
        'hy_f_b0': nrm((NH, HY_FILTER_W), 0.02),
        'hy_f_w1': nrm((NH, HY_FILTER_W, HY_FILTER_W), HY_FILTER_W ** -0.5),
        'hy_f_b1': nrm((NH, HY_FILTER_W), 0.02),
        'hy_f_w2': nrm((NH, HY_FILTER_W, HY_FILTER_W), HY_FILTER_W ** -0.5),
        'hy_f_b2': nrm((NH, HY_FILTER_W), 0.02),
        'hy_f_freq': gain((NH, HY_FILTER_W)),
        'hy_f_w3': nrm((NH, HY_FILTER_W, HY_ORDER * 2 * D), HY_FILTER_W ** -0.5),
        'hy_d': nrm((NH, HY_ORDER, D), 0.5),
        'hy_w_out': nrm((NH, D, D), D ** -0.5),
        'ffn_w_gate': nrm((DEPTH, D, D_FF), D ** -0.5),
        'ffn_w_val': nrm((DEPTH, D, D_FF), D ** -0.5),
        'ffn_conv_w': nrm((DEPTH, FFN_CONV, D_FF), FFN_CONV ** -0.5),
        'ffn_conv_b': nrm((DEPTH, D_FF), 0.02),
        'ffn_w_down': nrm((DEPTH, D_FF, D), D_FF ** -0.5),
    }


def reference(x, c, ctx, c_ctx, ada_w, ada_b, norm1_g, norm2_g,
              attn_w_in, attn_w_out, a_q_g, a_k_g, a_lam_q1, a_lam_k1, a_lam_q2, a_lam_k2, a_sub_g,
              b_q_g, b_k_g, b_sink,
              hy_w_in, hy_b_in, hy_sconv_w, hy_sconv_b, hy_f_w0, hy_f_b0, hy_f_w1, hy_f_b1,
              hy_f_w2, hy_f_b2, hy_f_freq, hy_f_w3, hy_d, hy_w_out,
              ffn_w_gate, ffn_w_val, ffn_conv_w, ffn_conv_b, ffn_w_down):
    s_lat = jax.nn.silu(c)
    s_ctx = jax.nn.silu(c_ctx)
    xs, cs = x, ctx
    for l in range(DEPTH):
        ctx_out = _ctx_read_later(l)
        ctx_in = (l % 2 == 0) or ctx_out
        i = l // 2
        m = s_lat @ ada_w[l] + ada_b[l]
        sh1, sc1, g1, sh2, sc2, g2 = (t[:, None, :] for t in jnp.split(m, 6, axis=-1))
        h = _rms(xs, norm1_g[l]) * (1.0 + sc1) + sh1
        hc = None
        if ctx_in:
            mc = s_ctx @ ada_w[l] + ada_b[l]
            csh1, csc1, cg1, csh2, csc2, cg2 = jnp.split(mc, 6)
            hc = _rms(cs, norm1_g[l]) * (1.0 + csc1) + csh1
        if l % 2 == 0:
            y, yc = _attn_mixer(h, hc, ctx_out, 0.8 - 0.6 * math.exp(-0.3 * l),
                                attn_w_in[i], attn_w_out[i], a_q_g[i], a_k_g[i],
                                a_lam_q1[i], a_lam_k1[i], a_lam_q2[i], a_lam_k2[i], a_sub_g[i],
                                b_q_g[i], b_k_g[i], b_sink[i])
        else:
            hy = (hy_w_in[i], hy_b_in[i], hy_sconv_w[i], hy_sconv_b[i], hy_f_w0[i], hy_f_b0[i],
                  hy_f_w1[i], hy_f_b1[i], hy_f_w2[i], hy_f_b2[i], hy_f_freq[i], hy_f_w3[i],
                  hy_d[i], hy_w_out[i])
            y = _hyena(h, *hy)
            yc = _hyena(hc, *hy) if ctx_out else None
        xs = xs + g1 * y
        ffn = (ffn_w_gate[l], ffn_w_val[l], ffn_conv_w[l], ffn_conv_b[l], ffn_w_down[l])
        xs = xs + g2 * _conv_ffn(_rms(xs, norm2_g[l]) * (1.0 + sc2) + sh2, *ffn)
        if ctx_out:
            cs = cs + cg1 * yc
            cs = cs + cg2 * _conv_ffn(_rms(cs, norm2_g[l]) * (1.0 + csc2) + csh2, *ffn)
    return xs
```

```python
import functools
import math

import numpy as np
import jax
import jax.numpy as jnp
from jax import lax
from jax.experimental import pallas as pl
from jax.experimental.pallas import tpu as pltpu

F32 = jnp.float32
BF16 = jnp.bfloat16
HIGHEST = lax.Precision.HIGHEST

HEAD_DIM = 128
A_HALF = HEAD_DIM // 2
B_KV_HEADS = 2
GRID_W = 64
WINDOW = 128
ROPE_BASE = 10000.0
EPS = 1e-6
NEG = -1e30
HY_ORDER = 2
HY_DECAY_TARGET = 1e-2
HY_MAX_DECAY_PCT = 0.3
HY_MIN_DECAY_PCT = 1.5

LANES = 128
BF16_SUBLANES = 16
FFT_N1 = 256
VMEM_LIMIT_BYTES = 56 * 1024 * 1024


def _params(*sem):
    return pltpu.CompilerParams(dimension_semantics=sem, vmem_limit_bytes=VMEM_LIMIT_BYTES)


def _ada_kernel(c_ref, w_ref, b_ref, o_ref):
    c = c_ref[...]
    s = c * (1.0 / (1.0 + jnp.exp(-c)))
    o_ref[0] = jnp.dot(s, w_ref[0], preferred_element_type=F32, precision=HIGHEST) + b_ref[0]


def _ada(cc, ada_w, ada_b):
    depth, d, n = ada_w.shape
    tn = 1024
    return pl.pallas_call(
        _ada_kernel,
        grid=(depth, n // tn),
        in_specs=[pl.BlockSpec((8, d), lambda l, j: (0, 0)),
                  pl.BlockSpec((1, d, tn), lambda l, j: (l, 0, j)),
                  pl.BlockSpec((1, 1, tn), lambda l, j: (l, 0, j))],
        out_specs=pl.BlockSpec((1, 8, tn), lambda l, j: (l, 0, j)),
        out_shape=jax.ShapeDtypeStruct((depth, 8, n), F32),
        compiler_params=_params("parallel", "parallel"),
        name="ada_modulation",
    )(cc, ada_w, ada_b.reshape(depth, 1, n))


def _normmod_kernel(x_ref, g_ref, sc_ref, sh_ref, o_ref):
    x = x_ref[0]
    ms = jnp.mean(x * x, axis=-1, keepdims=True)
    y = x * lax.rsqrt(ms + EPS) * g_ref[...]
    o_ref[0] = (y * (1.0 + sc_ref[0]) + sh_ref[0]).astype(o_ref.dtype)


def _normmod(x, g, sc, sh):
    b, l, d = x.shape
    tm = min(512, l)
    return pl.pallas_call(
        _normmod_kernel,
        grid=(b, l // tm),
        in_specs=[pl.BlockSpec((1, tm, d), lambda bi, i: (bi, i, 0)),
                  pl.BlockSpec((1, d), lambda bi, i: (0, 0)),
                  pl.BlockSpec((1, 1, d), lambda bi, i: (bi, 0, 0)),
                  pl.BlockSpec((1, 1, d), lambda bi, i: (bi, 0, 0))],
        out_specs=pl.BlockSpec((1, tm, d), lambda bi, i: (bi, i, 0)),
        out_shape=jax.ShapeDtypeStruct((b, l, d), BF16),
        compiler_params=_params("parallel", "parallel"),
        name="rmsnorm_modulate",
    )(x, g.reshape(1, d), sc, sh)


def _head_epilogue(z, gain, scale, cos, sin, chunk, rope):
    lane = lax.broadcasted_iota(jnp.int32, z.shape, 1)
    zz = z * z
    s_all = jnp.sum(zz, axis=-1, keepdims=True)
    if chunk == HEAD_DIM:
        ms = s_all * (1.0 / HEAD_DIM)
    else:
        s_lo = jnp.sum(jnp.where(lane < A_HALF, zz, 0.0), axis=-1, keepdims=True)
        ms = jnp.where(lane < A_HALF, s_lo, s_all - s_lo) * (1.0 / A_HALF)
    y = z * lax.rsqrt(ms + EPS) * gain
    if rope:
        nf = chunk // 4
        first = (lane % (2 * nf)) < nf
        partner = jnp.where(first, pltpu.roll(y, HEAD_DIM - nf, 1), pltpu.roll(y, nf, 1))
        y = y * cos + partner * sin
    return y * scale


def _inproj_kernel(h_ref, w_ref, gain_ref, scale_ref, cos_ref, sin_ref, o_ref, *, modes, rope):
    j = pl.program_id(1)
    z = jnp.dot(h_ref[...], w_ref[...], preferred_element_type=F32)
    for lo, hi, chunk in modes:
        @pl.when((j >= lo) & (j < hi))
        def _(chunk=chunk):
            for half in range(2):
                sl = slice(half * HEAD_DIM, (half + 1) * HEAD_DIM)
                zh = z[:, sl]
                if chunk:
                    zh = _head_epilogue(zh, gain_ref[:, sl], scale_ref[:, sl],
                                        cos_ref[0], sin_ref[0], chunk, rope)
                o_ref[:, sl] = zh.astype(o_ref.dtype)


def _inproj(h2d, w, gain, scale, cos_tab, sin_tab, *, modes, type_of_block, col0, ncols, l, rope):
    m, d = h2d.shape
    tn = 2 * HEAD_DIM
    tm = min(512, l)
    pos_blocks = l // tm
    kern = functools.partial(_inproj_kernel, modes=modes, rope=rope)
    return pl.pallas_call(
        kern,
        grid=(m // tm, ncols // tn),
        in_specs=[pl.BlockSpec((tm, d), lambda i, j: (i, 0)),
                  pl.BlockSpec((d, tn), lambda i, j: (0, j + col0)),
                  pl.BlockSpec((1, tn), lambda i, j: (0, j + col0)),
                  pl.BlockSpec((1, tn), lambda i, j: (0, j + col0)),
                  pl.BlockSpec((1, tm, HEAD_DIM), lambda i, j: (type_of_block(j), i % pos_blocks, 0)),
                  pl.BlockSpec((1, tm, HEAD_DIM), lambda i, j: (type_of_block(j), i % pos_blocks, 0))],
        out_specs=pl.BlockSpec((tm, tn), lambda i, j: (i, j)),
        out_shape=jax.ShapeDtypeStruct((m, ncols), BF16),
        compiler_params=_params("parallel", "arbitrary"),
        name="attn_in_projection",
    )(h2d, w, gain, scale, cos_tab, sin_tab)


def _diffattn_kernel(lam_ref, q_ref, kc_ref, vc_ref, k_ref, v_ref, g_ref, o_ref,
                     q2_s, m_s, l_s, acc_s, *, tq, nkv, out_scale):
    kv = pl.program_id(3)

    def process(k, v):
        s = lax.dot_general(q2_s[...], k, (((1,), (1,)), ((), ())), preferred_element_type=F32)
        m_prev = m_s[...]
        m_new = jnp.maximum(m_prev, jnp.max(s, axis=-1, keepdims=True))
        alpha = jnp.exp(m_prev - m_new)
        p = jnp.exp(s - m_new)
        l_s[...] = alpha * l_s[...] + jnp.sum(p, axis=-1, keepdims=True)
        acc_s[...] = alpha * acc_s[...] + jnp.dot(p.astype(BF16), v, preferred_element_type=F32)
        m_s[...] = m_new

    @pl.when(kv == 0)
    def _():
        q = q_ref[0]
        lane = lax.broadcasted_iota(jnp.int32, q.shape, 1)
        zero = jnp.zeros_like(q)
        q2_s[0:tq, :] = jnp.where(lane < A_HALF, q, zero)
        q2_s[tq:2 * tq, :] = jnp.where(lane >= A_HALF, q, zero)
        m_s[...] = jnp.full(m_s.shape, -jnp.inf, F32)
        l_s[...] = jnp.zeros(l_s.shape, F32)
        acc_s[...] = jnp.zeros(acc_s.shape, F32)
        process(kc_ref[0], vc_ref[0])

    process(k_ref[0], v_ref[0])

    @pl.when(kv == nkv - 1)
    def _():
        o = acc_s[...] / l_s[...]
        d = o[0:tq] - lam_ref[0, 0] * o[tq:2 * tq]
        ms = jnp.mean(d * d, axis=-1, keepdims=True)
        o_ref[0] = (d * lax.rsqrt(ms + EPS) * g_ref[...] * out_scale).astype(o_ref.dtype)


def _diffattn(lam, p, pc, sub_g, *, heads, l, c, out_scale):
    b = p.shape[0]
    tq = min(256, l)
    tk = min(512, l)
    nkv = l // tk
    q_blk = 0
    k_blk = 2 * heads
    v_blk = 3 * heads
    kern = functools.partial(_diffattn_kernel, tq=tq, nkv=nkv, out_scale=out_scale)
    return pl.pallas_call(
        kern,
        grid=(b, heads, l // tq, nkv),
        in_specs=[pl.BlockSpec(memory_space=pltpu.SMEM),
                  pl.BlockSpec((1, tq, HEAD_DIM), lambda bi, h, i, kv: (bi, i, q_blk + h)),
                  pl.BlockSpec((1, c, HEAD_DIM), lambda bi, h, i, kv: (bi, 0, h)),
                  pl.BlockSpec((1, c, HEAD_DIM), lambda bi, h, i, kv: (bi, 0, heads + h)),
                  pl.BlockSpec((1, tk, HEAD_DIM), lambda bi, h, i, kv: (bi, kv, k_blk + h)),
                  pl.BlockSpec((1, tk, HEAD_DIM), lambda bi, h, i, kv: (bi, kv, v_blk + h)),
                  pl.BlockSpec((1, HEAD_DIM), lambda bi, h, i, kv: (0, 0))],
        out_specs=pl.BlockSpec((1, tq, HEAD_DIM), lambda bi, h, i, kv: (bi, i, h)),
        out_shape=jax.ShapeDtypeStruct((b, l, heads * HEAD_DIM), BF16),
        scratch_shapes=[pltpu.VMEM((2 * tq, HEAD_DIM), BF16),
                        pltpu.VMEM((2 * tq, 1), F32),
                        pltpu.VMEM((2 * tq, 1), F32),
                        pltpu.VMEM((2 * tq, HEAD_DIM), F32)],
        compiler_params=_params("parallel", "parallel", "parallel", "arbitrary"),
        name="diff_attention",
    )(lam, p, pc, pc, p, p, sub_g.reshape(1, HEAD_DIM))


def _winattn_kernel(sink_ref, q_ref, kp_ref, km_ref, kn_ref, vp_ref, vm_ref, vn_ref, kc_ref, vc_ref,
                    o_ref, *, tq, l, group):
    g = pl.program_id(1)
    i = pl.program_id(2)
    kband = jnp.concatenate([kp_ref[0], km_ref[0], kn_ref[0]], axis=0)
    vband = jnp.concatenate([vp_ref[0], vm_ref[0], vn_ref[0]], axis=0)
    nk = tq + 2 * WINDOW
    qpos = i * tq + lax.broadcasted_iota(jnp.int32, (tq, nk), 0)
    kpos = i * tq - WINDOW + lax.broadcasted_iota(jnp.int32, (tq, nk), 1)
    valid = (jnp.abs(kpos - qpos) <= WINDOW) & (kpos >= 0) & (kpos < l)
    kc = kc_ref[0]
    vc = vc_ref[0]
    dn = (((1,), (1,)), ((), ()))
    for r in range(group):
        sl = slice(r * HEAD_DIM, (r + 1) * HEAD_DIM)
        q = q_ref[0, :, sl]
        s_lat = jnp.where(valid, lax.dot_general(q, kband, dn, preferred_element_type=F32), NEG)
        s_ctx = lax.dot_general(q, kc, dn, preferred_element_type=F32)
        sk = sink_ref[g, r]
        m = jnp.maximum(jnp.maximum(jnp.max(s_lat, axis=-1, keepdims=True),
                                    jnp.max(s_ctx, axis=-1, keepdims=True)), sk)
        p_lat = jnp.exp(s_lat - m)
        p_ctx = jnp.exp(s_ctx - m)
        denom = (jnp.sum(p_lat, axis=-1, keepdims=True) + jnp.sum(p_ctx, axis=-1, keepdims=True)
                 + jnp.exp(sk - m))
        o = (jnp.dot(p_lat.astype(BF16), vband, preferred_element_type=F32)
             + jnp.dot(p_ctx.astype(BF16), vc, preferred_element_type=F32))
        o_ref[0, :, sl] = (o / denom).astype(o_ref.dtype)


def _winattn(sink, p, pc, *, a_heads, b_heads, l, c):
    b = p.shape[0]
    group = b_heads // B_KV_HEADS
    tq = min(256, l)
    wpb = tq // WINDOW
    nwb = l // WINDOW
    gw = group * HEAD_DIM
    q_blk = a_heads * HEAD_DIM // gw
    k_blk = (a_heads + b_heads) + 2 * a_heads
    v_blk = k_blk + B_KV_HEADS
    kc_blk = 2 * a_heads
    vc_blk = kc_blk + B_KV_HEADS
    kern = functools.partial(_winattn_kernel, tq=tq, l=l, group=group)

    def prev_map(col):
        return lambda bi, g, i: (bi, jnp.maximum(i * wpb - 1, 0), col + g)

    def main_map(col):
        return lambda bi, g, i: (bi, i, col + g)

    def next_map(col):
        return lambda bi, g, i: (bi, jnp.minimum((i + 1) * wpb, nwb - 1), col + g)

    return pl.pallas_call(
        kern,
        grid=(b, B_KV_HEADS, l // tq),
        in_specs=[pl.BlockSpec(memory_space=pltpu.SMEM),
                  pl.BlockSpec((1, tq, gw), lambda bi, g, i: (bi, i, q_blk + g)),
                  pl.BlockSpec((1, WINDOW, HEAD_DIM), prev_map(k_blk)),
                  pl.BlockSpec((1, tq, HEAD_DIM), main_map(k_blk)),
                  pl.BlockSpec((1, WINDOW, HEAD_DIM), next_map(k_blk)),
                  pl.BlockSpec((1, WINDOW, HEAD_DIM), prev_map(v_blk)),
                  pl.BlockSpec((1, tq, HEAD_DIM), main_map(v_blk)),
                  pl.BlockSpec((1, WINDOW, HEAD_DIM), next_map(v_blk)),
                  pl.BlockSpec((1, c, HEAD_DIM), lambda bi, g, i: (bi, 0, kc_blk + g)),
                  pl.BlockSpec((1, c, HEAD_DIM), lambda bi, g, i: (bi, 0, vc_blk + g))],
        out_specs=pl.BlockSpec((1, tq, gw), lambda bi, g, i: (bi, i, g)),
        out_shape=jax.ShapeDtypeStruct((b, l, b_heads * HEAD_DIM), BF16),
        compiler_params=_params("parallel", "parallel", "parallel"),
        name="window_attention",
    )(sink, p, p, p, p, p, p, p, pc, pc)


def _mm_resid_kernel(*refs, nparts):
    a_refs = refs[0:nparts]
    w_refs = refs[nparts:2 * nparts]
    res_ref, gate_ref, o_ref = refs[2 * nparts:]
    y = jnp.dot(a_refs[0][0], w_refs[0][...], preferred_element_type=F32)
    for a_ref, w_ref in zip(a_refs[1:], w_refs[1:]):
        y = y + jnp.dot(a_ref[0], w_ref[...], preferred_element_type=F32)
    o_ref[0] = res_ref[0] + gate_ref[0] * y


def _mm_resid(parts, ws, res, gate):
    b, l, n = res.shape
    tm = min(512, l)
    tn = min(512, n)
    nparts = len(parts)
    in_specs = ([pl.BlockSpec((1, tm, a.shape[2]), lambda bi, i, j: (bi, i, 0)) for a in parts]
                + [pl.BlockSpec((w.shape[0], tn), lambda bi, i, j: (0, j)) for w in ws]
                + [pl.BlockSpec((1, tm, tn), lambda bi, i, j: (bi, i, j)),
                   pl.BlockSpec((1, 1, tn), lambda bi, i, j: (bi, 0, j))])
    return pl.pallas_call(
        functools.partial(_mm_resid_kernel, nparts=nparts),
        grid=(b, l // tm, n // tn),
        in_specs=in_specs,
        out_specs=pl.BlockSpec((1, tm, tn), lambda bi, i, j: (bi, i, j)),
        out_shape=jax.ShapeDtypeStruct((b, l, n), F32),
        compiler_params=_params("parallel", "parallel", "arbitrary"),
        name="out_projection_gated_residual",
    )(*parts, *ws, res, gate)


HALO = BF16_SUBLANES


def _fill_rows(hbuf, hp_ref, hm_ref, hn_ref, tm):
    hbuf[0:HALO, :] = hp_ref[0]
    hbuf[HALO:HALO + tm, :] = hm_ref[0]
    hbuf[HALO + tm:2 * HALO + tm, :] = hn_ref[0]


def _conv3(zbuf, cw_ref, cb_ref, tm, first, last):
    @pl.when(first)
    def _():
        zbuf[HALO - 1:HALO, :] = jnp.zeros((1, zbuf.shape[1]), F32)

    @pl.when(last)
    def _():
        zbuf[HALO + tm:HALO + tm + 1, :] = jnp.zeros((1, zbuf.shape[1]), F32)

    return (zbuf[HALO - 1:HALO - 1 + tm, :] * cw_ref[0:1, :] + zbuf[HALO:HALO + tm, :] * cw_ref[1:2, :]
            + zbuf[HALO + 1:HALO + 1 + tm, :] * cw_ref[2:3, :] + cb_ref[...])


def _gelu_tanh(x):
    return 0.5 * x * (1.0 + jnp.tanh(math.sqrt(2.0 / math.pi) * (x + 0.044715 * (x * x * x))))


def _ffn_up_kernel(hp_ref, hm_ref, hn_ref, wg_ref, wv_ref, cw_ref, cb_ref, o_ref, hbuf, zbuf, *, tm, nt):
    i = pl.program_id(1)
    j = pl.program_id(2)

    @pl.when(j == 0)
    def _():
        _fill_rows(hbuf, hp_ref, hm_ref, hn_ref, tm)

    zbuf[...] = jnp.dot(hbuf[...], wg_ref[...], preferred_element_type=F32)
    g = _conv3(zbuf, cw_ref, cb_ref, tm, i == 0, i == nt - 1)
    v = jnp.dot(hbuf[HALO:HALO + tm, :], wv_ref[...], preferred_element_type=F32)
    o_ref[0] = (_gelu_tanh(g) * v).astype(o_ref.dtype)


def _hy_in_kernel(hp_ref, hm_ref, hn_ref, w_ref, b_ref, cw_ref, cb_ref, o_ref, hbuf, zbuf, *, tm, nt):
    i = pl.program_id(1)
    j = pl.program_id(2)

    @pl.when(j == 0)
    def _():
        _fill_rows(hbuf, hp_ref, hm_ref, hn_ref, tm)

    zbuf[...] = jnp.dot(hbuf[...], w_ref[...], preferred_element_type=F32) + b_ref[...]
    o_ref[0] = _conv3(zbuf, cw_ref, cb_ref, tm, i == 0, i == nt - 1).astype(o_ref.dtype)


def _mm_conv(kernel, h, ws, vecs, n, out_dtype, name):
    b, l, k = h.shape
    tm = min(512, l)
    tn = min(512, n)
    nt = l // tm
    hpb = tm // HALO
    nhb = l // HALO
    in_specs = ([pl.BlockSpec((1, HALO, k), lambda bi, i, j: (bi, jnp.maximum(i * hpb - 1, 0), 0)),
                 pl.BlockSpec((1, tm, k), lambda bi, i, j: (bi, i, 0)),
                 pl.BlockSpec((1, HALO, k), lambda bi, i, j: (bi, jnp.minimum((i + 1) * hpb, nhb - 1), 0))]
                + [pl.BlockSpec((k, tn), lambda bi, i, j: (0, j)) for _ in ws]
                + [pl.BlockSpec((v.shape[0], tn), lambda bi, i, j: (0, j)) for v in vecs])
    return pl.pallas_call(
        functools.partial(kernel, tm=tm, nt=nt),
        grid=(b, nt, n // tn),
        in_specs=in_specs,
        out_specs=pl.BlockSpec((1, tm, tn), lambda bi, i, j: (bi, i, j)),
        out_shape=jax.ShapeDtypeStruct((b, l, n), out_dtype),
        scratch_shapes=[pltpu.VMEM((tm + 2 * HALO, k), BF16),
                        pltpu.VMEM((tm + 2 * HALO, tn), F32)],
        compiler_params=_params("parallel", "parallel", "arbitrary"),
        name=name,
    )(h, h, h, *ws, *vecs)


def _filter_kernel(z_ref, w0_ref, b0_ref, w1_ref, b1_ref, w2_ref, b2_ref, fr_ref, w3_ref, dl_ref,
                   o_ref, s_ref, *, tt):
    it = pl.program_id(1)
    z = z_ref[...]
    fr = fr_ref[...]

    def layer(a, w_ref, b_ref):
        return jnp.sin(fr * (jnp.dot(a, w_ref[...], preferred_element_type=F32, precision=HIGHEST)
                             + b_ref[...]))

    a = layer(layer(layer(z, w0_ref, b0_ref), w1_ref, b1_ref), w2_ref, b2_ref)
    decay = jnp.exp(-z[:, 0:1] * dl_ref[...])
    row = it * tt + lax.broadcasted_iota(jnp.int32, decay.shape, 0)

    @pl.when(it == 0)
    def _():
        s_ref[...] = jnp.zeros(s_ref.shape, F32)

    for s in range(2 * HY_ORDER):
        h = jnp.dot(a, w3_ref[s], preferred_element_type=F32, precision=HIGHEST) * decay
        if s % 2 == 1:
            h = jnp.where(row == 0, 0.0, h)
        o_ref[s] = h.astype(o_ref.dtype)
        s_ref[s] += jnp.sum(jnp.abs(h), axis=0, keepdims=True)


def _hyena_filters(l, d, f_w0, f_b0, f_w1, f_b1, f_w2, f_b2, f_freq, f_w3):
    emb, fw = f_w0.shape
    bands = (emb - 1) // 2
    t = np.linspace(0.0, 1.0, l)[:, None]
    w = 2.0 * math.pi * np.arange(l)[:, None] / l
    f = np.linspace(1e-4, bands - 1, bands)[None, :]
    emb_pad = -(-emb // 8) * 8
    z = np.concatenate([t, np.cos(f * w), -np.sin(f * w), np.zeros((l, emb_pad - emb))], axis=-1)
    z = jnp.asarray(z, F32)
    w0 = jnp.concatenate([f_w0, jnp.zeros((emb_pad - emb, fw), F32)], axis=0)
    max_decay = math.log(HY_DECAY_TARGET) / HY_MAX_DECAY_PCT
    min_decay = math.log(HY_DECAY_TARGET) / HY_MIN_DECAY_PCT
    dl = jnp.asarray(np.abs(np.linspace(min_decay, max_decay, d))[None, :], F32)
    nseg = 2 * HY_ORDER
    w3 = jnp.transpose(f_w3.reshape(fw, nseg, d), (1, 0, 2))
    tt = min(256, l)
    td = min(512, d)
    vec = lambda a: a.reshape(1, fw)
    small = lambda shape: pl.BlockSpec(shape, lambda jd, it: (0,) * len(shape))
    return pl.pallas_call(
        functools.partial(_filter_kernel, tt=tt),
        grid=(d // td, l // tt),
        in_specs=[pl.BlockSpec((tt, emb_pad), lambda jd, it: (it, 0)),
                  small((emb_pad, fw)), small((1, fw)), small((fw, fw)), small((1, fw)),
                  small((fw, fw)), small((1, fw)), small((1, fw)),
                  pl.BlockSpec((nseg, fw, td), lambda jd, it: (0, 0, jd)),
                  pl.BlockSpec((1, td), lambda jd, it: (0, jd))],
        out_specs=[pl.BlockSpec((nseg, tt, td), lambda jd, it: (0, it, jd)),
                   pl.BlockSpec((nseg, 1, td), lambda jd, it: (0, 0, jd))],
        out_shape=[jax.ShapeDtypeStruct((nseg, l, d), BF16),
                   jax.ShapeDtypeStruct((nseg, 1, d), F32)],
        compiler_params=_params("parallel", "arbitrary"),
        name="hyena_filters",
    )(z, w0, vec(f_b0), f_w1, vec(f_b1), f_w2, vec(f_b2), vec(f_freq), w3, dl)


def _dft_tables(l):
    n1 = FFT_N1
    n = 2 * l
    n2 = n // n1
    k1 = np.arange(n1)[:, None]
    f1 = np.exp(-2j * np.pi * k1 * np.arange(n1 // 2)[None, :] / n1)
    tw = np.exp(-2j * np.pi * np.arange(n2)[:, None] * np.arange(n1)[None, :] / n)
    f1r, f1i = jnp.asarray(f1.real, F32), jnp.asarray(f1.imag, F32)
    twr, twi = jnp.asarray(tw.real, F32)[:, :, None], jnp.asarray(tw.imag, F32)[:, :, None]
    fr = twr * f1r - twi * f1i
    fi = twr * f1i + twi * f1r
    ta = jnp.concatenate([jnp.concatenate([fr, -fi], axis=2),
                          jnp.concatenate([fi, fr], axis=2)], axis=1)
    fb = np.exp(-2j * np.pi * np.arange(n2)[:, None] * np.arange(n2)[None, :] / n2)
    tb = np.block([[fb.real, -fb.imag], [fb.imag, fb.real]])
    return (ta.astype(BF16), jnp.swapaxes(ta, 1, 2).astype(BF16),
            jnp.asarray(tb, BF16), jnp.asarray(tb.T, BF16))


def _fft_a_kernel(x_ref, t_ref, o_ref):
    x = x_ref[0]
    x = x.reshape(x.shape[0] * x.shape[1], x.shape[2]).astype(BF16)
    o_ref[0] = jnp.dot(t_ref[0], x, preferred_element_type=F32).astype(o_ref.dtype)


def _fft_a(x4, ta, *, n2, d, width, seg):
    s, p, half, _ = x4.shape
    kdim = p * half
    tn = min(2048, d)
    wb, sb, db = width // tn, (seg * d) // tn, d // tn
    return pl.pallas_call(
        _fft_a_kernel,
        grid=(s, n2, db),
        in_specs=[pl.BlockSpec((1, p, half, tn), lambda si, m, j: (si, 0, 0, m * wb + sb + j)),
                  pl.BlockSpec((1, 2 * FFT_N1, kdim), lambda si, m, j: (m, 0, 0))],
        out_specs=pl.BlockSpec((1, 2 * FFT_N1, tn), lambda si, m, j: (si, 0, m * db + j)),
        out_shape=jax.ShapeDtypeStruct((s, 2 * FFT_N1, n2 * d), BF16),
        compiler_params=_params("parallel", "parallel", "parallel"),
        name="dft_stage_a",
    )(x4, ta)


def _fft_bfilt_kernel(af_ref, ab_ref, tb_ref, sc_ref, o_ref, *, n2):
    tb = tb_ref[...]
    hf = jnp.dot(tb, af_ref[0].reshape(2 * n2, -1), preferred_element_type=F32)
    hb = jnp.dot(tb, ab_ref[0].reshape(2 * n2, -1), preferred_element_type=F32)
    sc = sc_ref[0]
    o_ref[0, 0, 0] = (hf[0:n2] + hb[0:n2]) * sc
    o_ref[0, 1, 0] = (hf[n2:2 * n2] - hb[n2:2 * n2]) * sc


def _fft_bfilt(af, tb, scale, *, n2, d):
    tn = min(2048, d)
    blk = (1, 2, 1, n2, tn)
    return pl.pallas_call(
        functools.partial(_fft_bfilt_kernel, n2=n2),
        grid=(HY_ORDER, FFT_N1, d // tn),
        in_specs=[pl.BlockSpec(blk, lambda o, k, j: (2 * o, 0, k, 0, j)),
                  pl.BlockSpec(blk, lambda o, k, j: (2 * o + 1, 0, k, 0, j)),
                  pl.BlockSpec((2 * n2, 2 * n2), lambda o, k, j: (0, 0)),
                  pl.BlockSpec((1, 1, tn), lambda o, k, j: (o, 0, j))],
        out_specs=pl.BlockSpec(blk, lambda o, k, j: (o, 0, k, 0, j)),
        out_shape=jax.ShapeDtypeStruct((HY_ORDER, 2, FFT_N1, n2, d), F32),
        compiler_params=_params("parallel", "parallel", "parallel"),
        name="dft_stage_b_filters",
    )(af, af, tb, scale)


def _fft_b_kernel(a_ref, g_ref, tb_ref, tbt_ref, o_ref, *, n2):
    z = jnp.dot(tb_ref[...], a_ref[...].reshape(2 * n2, -1), preferred_element_type=F32)
    zr, zi = z[0:n2], z[n2:2 * n2]
    gr, gi = g_ref[0, 0, 0], g_ref[0, 1, 0]
    prod = jnp.concatenate([zr * gr - zi * gi, zr * gi + zi * gr], axis=0).astype(BF16)
    y = jnp.dot(tbt_ref[...], prod, preferred_element_type=F32)
    o_ref[...] = y.reshape(o_ref.shape).astype(o_ref.dtype)


def _fft_b(a, g, tb, tbt, order, *, n2, d):
    tn = min(2048, d)
    blk = (2, 1, n2, tn)
    return pl.pallas_call(
        functools.partial(_fft_b_kernel, n2=n2),
        grid=(FFT_N1, d // tn),
        in_specs=[pl.BlockSpec(blk, lambda k, j: (0, k, 0, j)),
                  pl.BlockSpec((1,) + blk, lambda k, j: (order, 0, k, 0, j)),
                  pl.BlockSpec((2 * n2, 2 * n2), lambda k, j: (0, 0)),
                  pl.BlockSpec((2 * n2, 2 * n2), lambda k, j: (0, 0))],
        out_specs=pl.BlockSpec(blk, lambda k, j: (0, k, 0, j)),
        out_shape=jax.ShapeDtypeStruct((2, FFT_N1, n2, d), BF16),
        compiler_params=_params("parallel", "parallel"),
        name="dft_stage_b_spectrum_product",
    )(a, g, tb, tbt)


def _fft_ainv_kernel(y_ref, t_ref, u_ref, gate_ref, d_ref, o_ref):
    x = jnp.dot(t_ref[0], y_ref[...], preferred_element_type=F32)
    x = x.reshape(o_ref.shape)
    o_ref[...] = (gate_ref[...] * (x + d_ref[...] * u_ref[...])).astype(o_ref.dtype)


def _fft_ainv(y2, tat, u3, gate3, dskip, *, n2, d, uw, useg, gw, gseg, out_dtype):
    b, half, _ = u3.shape
    tn = min(1024, d)
    db = d // tn
    ub, usb = uw // tn, (useg * d) // tn
    gb, gsb = gw // tn, (gseg * d) // tn
    return pl.pallas_call(
        _fft_ainv_kernel,
        grid=(n2, db),
        in_specs=[pl.BlockSpec((2 * FFT_N1, tn), lambda m, j: (0, m * db + j)),
                  pl.BlockSpec((1, FFT_N1, 2 * FFT_N1), lambda m, j: (m, 0, 0)),
                  pl.BlockSpec((b, half, tn), lambda m, j: (0, 0, m * ub + usb + j)),
                  pl.BlockSpec((b, half, tn), lambda m, j: (0, 0, m * gb + gsb + j)),
                  pl.BlockSpec((1, tn), lambda m, j: (0, j))],
        out_specs=pl.BlockSpec((b, half, tn), lambda m, j: (0, 0, m * db + j)),
        out_shape=jax.ShapeDtypeStruct((b, half, n2 * d), out_dtype),
        compiler_params=_params("parallel", "parallel"),
        name="dft_stage_a_inverse_gate",
    )(y2, tat, u3, gate3, dskip)


def _hyena_mixer(h, hy, l, d):
    (w_in, b_in, sconv_w, sconv_b, f_w0, f_b0, f_w1, f_b1, f_w2, f_b2, f_freq, f_w3, d_skip, w_out) = hy
    b = h.shape[0]
    assert b == 2, "the two batch rows ride the real / imaginary planes of one complex DFT"
    n1 = FFT_N1
    half = n1 // 2
    n2 = (2 * l) // n1
    assert half * n2 == l
    z3 = _mm_conv(_hy_in_kernel, h, [w_in.astype(BF16)],
                  [b_in.reshape(1, -1), sconv_w, sconv_b.reshape(1, -1)], 3 * d, F32, "hyena_in_proj_conv")
    filt, fsum = _hyena_filters(l, d, f_w0, f_b0, f_w1, f_b1, f_w2, f_b2, f_freq, f_w3)
    ta, tat, tb, tbt = _dft_tables(l)
    nseg = 2 * HY_ORDER
    af = _fft_a(filt.reshape(nseg, 1, half, n2 * d), ta, n2=n2, d=d, width=d, seg=0)
    af = af.reshape(nseg, 2, n1, n2, d)
    norm = (fsum[0::2] + fsum[1::2])
    g = _fft_bfilt(af, tb, 1.0 / (norm * (2 * l)), n2=n2, d=d)
    z3v = z3.reshape(b, half, n2 * 3 * d)
    y = z3v
    yw, yseg = 3 * d, 0
    for o in range(HY_ORDER):
        a = _fft_a(y[None], ta, n2=n2, d=d, width=yw, seg=yseg)
        yb = _fft_b(a.reshape(2, n1, n2, d), g, tb, tbt, o, n2=n2, d=d)
        last = o == HY_ORDER - 1
        y = _fft_ainv(yb.reshape(2 * n1, n2 * d), tat, y, z3v, d_skip[o].reshape(1, d),
                      n2=n2, d=d, uw=yw, useg=yseg, gw=3 * d, gseg=o + 1,
                      out_dtype=BF16 if last else F32)
        yw, yseg = d, 0
    return y.reshape(b, l, d), w_out.astype(BF16)


def _rope_tables(l):
    t = jnp.arange(l)
    row = (t // GRID_W).astype(F32)[:, None]
    col = (t % GRID_W).astype(F32)[:, None]
    cos, sin = [], []
    for dim in (A_HALF, HEAD_DIM):
        nf = dim // 4
        inv = jnp.asarray(ROPE_BASE ** (-np.arange(nf) / nf), F32)[None, :]
        ar, ac = row * inv, col * inv
        c = jnp.concatenate([jnp.cos(ar), jnp.cos(ar), jnp.cos(ac), jnp.cos(ac)], axis=1)
        s = jnp.concatenate([-jnp.sin(ar), jnp.sin(ar), -jnp.sin(ac), jnp.sin(ac)], axis=1)
        reps = HEAD_DIM // dim
        cos.append(jnp.tile(c, (1, reps)))
        sin.append(jnp.tile(s, (1, reps)))
    return jnp.stack(cos), jnp.stack(sin)


def _attn_mixer(h, hc, lam_init, w_in, w_out, a_q_g, a_k_g, lq1, lk1, lq2, lk2, a_sub_g, b_q_g, b_k_g, b_sink):
    b, l, d = h.shape
    c = hc.shape[1]
    a_heads = d // (2 * HEAD_DIM)
    b_heads = d // (2 * HEAD_DIM)
    q_cols = (a_heads + b_heads) * HEAD_DIM
    in_cols = w_in.shape[1]
    lam = (jnp.exp(jnp.sum(lq1 * lk1)) - jnp.exp(jnp.sum(lq2 * lk2)) + lam_init).reshape(1, 1)
    sink = b_sink.reshape(B_KV_HEADS, b_heads // B_KV_HEADS)

    ones = lambda n: jnp.ones((n,), F32)
    tile = lambda v, n: jnp.tile(v, n)
    gain = jnp.concatenate([tile(a_q_g, 2 * a_heads), tile(b_q_g, b_heads), tile(a_k_g, 2 * a_heads),
                            ones(a_heads * HEAD_DIM), tile(b_k_g, B_KV_HEADS),
                            ones(B_KV_HEADS * HEAD_DIM)]).reshape(1, in_cols)
    scale = jnp.concatenate([jnp.full((a_heads * HEAD_DIM,), A_HALF ** -0.5, F32),
                             jnp.full((b_heads * HEAD_DIM,), HEAD_DIM ** -0.5, F32),
                             ones(in_cols - q_cols)]).reshape(1, in_cols)
    e = [0, a_heads // 2, (a_heads + b_heads) // 2, (2 * a_heads + b_heads) // 2,
         (3 * a_heads + b_heads) // 2, (3 * a_heads + b_heads) // 2 + 1, (3 * a_heads + b_heads) // 2 + 2]
    chunks = [A_HALF, HEAD_DIM, A_HALF, 0, HEAD_DIM, 0]
    modes = tuple((e[s], e[s + 1], chunks[s]) for s in range(6))
    cos_tab, sin_tab = _rope_tables(l)

    def type_of(j0):
        def f(j):
            jj = j + j0
            is_a = (jj < e[1]) | ((jj >= e[2]) & (jj < e[3]))
            return jnp.where(is_a, 0, 1)
        return f

    w_bf = w_in.astype(BF16)
    p = _inproj(h.reshape(b * l, d), w_bf, gain, scale, cos_tab, sin_tab, modes=modes,
                type_of_block=type_of(0), col0=0, ncols=in_cols, l=l, rope=True).reshape(b, l, in_cols)
    kv0 = e[2]
    modes_c = tuple((lo - kv0, hi - kv0, ch) for lo, hi, ch in modes[2:])
    pc = _inproj(hc.reshape(b * c, d), w_bf, gain, scale, cos_tab, sin_tab, modes=modes_c,
                 type_of_block=type_of(kv0), col0=kv0, ncols=in_cols - q_cols, l=c, rope=False
                 ).reshape(b, c, in_cols - q_cols)
    oa = _diffattn(lam, p, pc, a_sub_g, heads=a_heads, l=l, c=c, out_scale=1.0 - lam_init)
    ob = _winattn(sink, p, pc, a_heads=a_heads, b_heads=b_heads, l=l, c=c)
    w_out_bf = w_out.astype(BF16)
    na = a_heads * HEAD_DIM
    return [oa, ob], [w_out_bf[:na], w_out_bf[na:]]


def kernel(x, c, ctx, c_ctx, ada_w, ada_b, norm1_g, norm2_g, attn_w_in, attn_w_out, a_q_g, a_k_g, a_lam_q1, a_lam_k1, a_lam_q2, a_lam_k2, a_sub_g, b_q_g, b_k_g, b_sink, hy_w_in, hy_b_in, hy_sconv_w, hy_sconv_b, hy_f_w0, hy_f_b0, hy_f_w1, hy_f_b1, hy_f_w2, hy_f_b2, hy_f_freq, hy_f_w3, hy_d, hy_w_out, ffn_w_gate, ffn_w_val, ffn_conv_w, ffn_conv_b, ffn_w_down):
    b, l, d = x.shape
    depth = ada_w.shape[0]
    cc = jnp.concatenate([c, c_ctx[None, :], jnp.zeros((8 - b - 1, d), F32)], axis=0)
    m = _ada(cc, ada_w, ada_b)
    xs = x
    for layer in range(depth):
        i = layer // 2
        lat = [m[layer, :b, k * d:(k + 1) * d].reshape(b, 1, d) for k in range(6)]
        sh1, sc1, g1, sh2, sc2, g2 = lat
        h = _normmod(xs, norm1_g[layer], sc1, sh1)
        if layer % 2 == 0:
            mc = [jnp.broadcast_to(m[layer, b, k * d:(k + 1) * d].reshape(1, 1, d), (b, 1, d)) for k in range(2)]
            hc = _normmod(ctx, norm1_g[layer], mc[1], mc[0])
            parts, ws = _attn_mixer(h, hc, 0.8 - 0.6 * math.exp(-0.3 * layer), attn_w_in[i], attn_w_out[i],
                                    a_q_g[i], a_k_g[i], a_lam_q1[i], a_lam_k1[i], a_lam_q2[i], a_lam_k2[i],
                                    a_sub_g[i], b_q_g[i], b_k_g[i], b_sink[i])
        else:
            hy = (hy_w_in[i], hy_b_in[i], hy_sconv_w[i], hy_sconv_b[i], hy_f_w0[i], hy_f_b0[i], hy_f_w1[i],
                  hy_f_b1[i], hy_f_w2[i], hy_f_b2[i], hy_f_freq[i], hy_f_w3[i], hy_d[i], hy_w_out[i])
            y, w_o = _hyena_mixer(h, hy, l, d)
            parts, ws = [y], [w_o]
        xs = _mm_resid(parts, ws, xs, g1)
        h2 = _normmod(xs, norm2_g[layer], sc2, sh2)
        d_ff = ffn_w_gate.shape[2]
        hid = _mm_conv(_ffn_up_kernel, h2, [ffn_w_gate[layer].astype(BF16), ffn_w_val[layer].astype(BF16)],
                       [ffn_conv_w[layer], ffn_conv_b[layer].reshape(1, -1)], d_ff, BF16, "ffn_up_conv_glu")
        xs = _mm_resid([hid], [ffn_w_down[layer].astype(BF16)], xs, g2)
    return xs
```

```python
import functools
import math

import numpy as np
import jax
import jax.numpy as jnp
from jax import lax
from jax.experimental import pallas as pl
from jax.experimental.pallas import tpu as pltpu

F32 = jnp.float32
BF16 = jnp.bfloat16
HIGHEST = lax.Precision.HIGHEST

HEAD_DIM = 128
A_HALF = HEAD_DIM // 2
B_KV_HEADS = 2
GRID_W = 64
WINDOW = 128
ROPE_BASE = 10000.0
EPS = 1e-6
NEG = -1e30
HY_ORDER = 2
HY_DECAY_TARGET = 1e-2
HY_MAX_DECAY_PCT = 0.3
HY_MIN_DECAY_PCT = 1.5

LANES = 128
BF16_SUBLANES = 16
FFT_N1 = 256
VMEM_LIMIT_BYTES = 56 * 1024 * 1024


def _params(*sem):
    return pltpu.CompilerParams(dimension_semantics=sem, vmem_limit_bytes=VMEM_LIMIT_BYTES)


def _ada_kernel(c_ref, w_ref, b_ref, o_ref):
    c = c_ref[...]
    s = c * (1.0 / (1.0 + jnp.exp(-c)))
    o_ref[0] = jnp.dot(s, w_ref[0], preferred_element_type=F32, precision=HIGHEST) + b_ref[0]


def _ada(cc, ada_w, ada_b):
    depth, d, n = ada_w.shape
    tn = 1024
    return pl.pallas_call(
        _ada_kernel,
        grid=(depth, n // tn),
        in_specs=[pl.BlockSpec((8, d), lambda l, j: (0, 0)),
                  pl.BlockSpec((1, d, tn), lambda l, j: (l, 0, j)),
                  pl.BlockSpec((1, 1, tn), lambda l, j: (l, 0, j))],
        out_specs=pl.BlockSpec((1, 8, tn), lambda l, j: (l, 0, j)),
        out_shape=jax.ShapeDtypeStruct((depth, 8, n), F32),
        compiler_params=_params("parallel", "parallel"),
        name="ada_modulation",
    )(cc, ada_w, ada_b.reshape(depth, 1, n))


def _normmod_kernel(x_ref, g_ref, sc_ref, sh_ref, o_ref):
    x = x_ref[0]
    ms = jnp.mean(x * x, axis=-1, keepdims=True)
    y = x * lax.rsqrt(ms + EPS) * g_ref[...]
    o_ref[0] = (y * (1.0 + sc_ref[0]) + sh_ref[0]).astype(o_ref.dtype)


def _normmod(x, g, sc, sh):
    b, l, d = x.shape
    tm = min(512, l)
    return pl.pallas_call(
        _normmod_kernel,
        grid=(b, l // tm),
        in_specs=[pl.BlockSpec((1, tm, d), lambda bi, i: (bi, i, 0)),
                  pl.BlockSpec((1, d), lambda bi, i: (0, 0)),
                  pl.BlockSpec((1, 1, d), lambda bi, i: (bi, 0, 0)),
                  pl.BlockSpec((1, 1, d), lambda bi, i: (bi, 0, 0))],
        out_specs=pl.BlockSpec((1, tm, d), lambda bi, i: (bi, i, 0)),
        out_shape=jax.ShapeDtypeStruct((b, l, d), BF16),
        compiler_params=_params("parallel", "parallel"),
        name="rmsnorm_modulate",
    )(x, g.reshape(1, d), sc, sh)


def _head_epilogue(z, gain, scale, cos, sin, chunk, rope):
    lane = lax.broadcasted_iota(jnp.int32, z.shape, 1)
    zz = z * z
    s_all = jnp.sum(zz, axis=-1, keepdims=True)
    if chunk == HEAD_DIM:
        ms = s_all * (1.0 / HEAD_DIM)
    else:
        s_lo = jnp.sum(jnp.where(lane < A_HALF, zz, 0.0), axis=-1, keepdims=True)
        ms = jnp.where(lane < A_HALF, s_lo, s_all - s_lo) * (1.0 / A_HALF)
    y = z * lax.rsqrt(ms + EPS) * gain
    if rope:
        nf = chunk // 4
        first = (lane % (2 * nf)) < nf
        partner = jnp.where(first, pltpu.roll(y, HEAD_DIM - nf, 1), pltpu.roll(y, nf, 1))
        y = y * cos + partner * sin
    return y * scale


def _inproj_kernel(h_ref, w_ref, gain_ref, scale_ref, cos_ref, sin_ref, o_ref, *, modes, rope):
    j = pl.program_id(1)
    z = jnp.dot(h_ref[...], w_ref[...], preferred_element_type=F32)
    for lo, hi, chunk in modes:
        @pl.when((j >= lo) & (j < hi))
        def _(chunk=chunk):
            for half in range(2):
                sl = slice(half * HEAD_DIM, (half + 1) * HEAD_DIM)
                zh = z[:, sl]
                if chunk:
                    zh = _head_epilogue(zh, gain_ref[:, sl], scale_ref[:, sl],
                                        cos_ref[0], sin_ref[0], chunk, rope)
                o_ref[:, sl] = zh.astype(o_ref.dtype)


def _inproj(h2d, w, gain, scale, cos_tab, sin_tab, *, modes, type_of_block, col0, ncols, l, rope):
    m, d = h2d.shape
    tn = 2 * HEAD_DIM
    tm = min(512, l)
    pos_blocks = l // tm
    kern = functools.partial(_inproj_kernel, modes=modes, rope=rope)
    return pl.pallas_call(
        kern,
        grid=(m // tm, ncols // tn),
        in_specs=[pl.BlockSpec((tm, d), lambda i, j: (i, 0)),
                  pl.BlockSpec((d, tn), lambda i, j: (0, j + col0)),
                  pl.BlockSpec((1, tn), lambda i, j: (0, j + col0)),
                  pl.BlockSpec((1, tn), lambda i, j: (0, j + col0)),
                  pl.BlockSpec((1, tm, HEAD_DIM), lambda i, j: (type_of_block(j), i % pos_blocks, 0)),
                  pl.BlockSpec((1, tm, HEAD_DIM), lambda i, j: (type_of_block(j), i % pos_blocks, 0))],
        out_specs=pl.BlockSpec((tm, tn), lambda i, j: (i, j)),
        out_shape=jax.ShapeDtypeStruct((m, ncols), BF16),
        compiler_params=_params("parallel", "arbitrary"),
        name="attn_in_projection",
    )(h2d, w, gain, scale, cos_tab, sin_tab)


def _diffattn_kernel(lam_ref, q_ref, kc_ref, vc_ref, k_ref, v_ref, g_ref, o_ref,
                     q2_s, m_s, l_s, acc_s, *, tq, nkv, out_scale):
    kv = pl.program_id(3)

    def process(k, v):
        s = lax.dot_general(q2_s[...], k, (((1,), (1,)), ((), ())), preferred_element_type=F32)
        m_prev = m_s[...]
        m_new = jnp.maximum(m_prev, jnp.max(s, axis=-1, keepdims=True))
        alpha = jnp.exp2(m_prev - m_new)
        p = jnp.exp2(s - m_new)
        l_s[...] = alpha * l_s[...] + jnp.sum(p, axis=-1, keepdims=True)
        acc_s[...] = alpha * acc_s[...] + jnp.dot(p.astype(BF16), v, preferred_element_type=F32)
        m_s[...] = m_new

    @pl.when(kv == 0)
    def _():
        q = q_ref[0]
        lane = lax.broadcasted_iota(jnp.int32, q.shape, 1)
        zero = jnp.zeros_like(q)
        q2_s[0:tq, :] = jnp.where(lane < A_HALF, q, zero)
        q2_s[tq:2 * tq, :] = jnp.where(lane >= A_HALF, q, zero)
        m_s[...] = jnp.full(m_s.shape, -jnp.inf, F32)
        l_s[...] = jnp.zeros(l_s.shape, F32)
        acc_s[...] = jnp.zeros(acc_s.shape, F32)
        process(kc_ref[0], vc_ref[0])

    process(k_ref[0], v_ref[0])

    @pl.when(kv == nkv - 1)
    def _():
        o = acc_s[...] / l_s[...]
        d = o[0:tq] - lam_ref[0, 0] * o[tq:2 * tq]
        ms = jnp.mean(d * d, axis=-1, keepdims=True)
        o_ref[0] = (d * lax.rsqrt(ms + EPS) * g_ref[...] * out_scale).astype(o_ref.dtype)


def _diffattn_fast_kernel(lam_ref, q_ref, kc_ref, vct_ref, k_ref, vt_ref, g_ref, o_ref,
                          q2_s, l_s, acc_s, *, tq, tk, kc, nkv, out_scale):
    kv = pl.program_id(3)
    dn = (((1,), (1,)), ((), ()))

    def process(k, vt):
        n = k.shape[0]
        s = lax.dot_general(k, q2_s[...], dn, preferred_element_type=F32)
        p = jnp.exp2(s)
        l_s[...] += jnp.sum(p.reshape(n // 8, 8, 2 * tq), axis=0)
        acc_s[...] += jnp.dot(vt, p.astype(BF16), preferred_element_type=F32)

    @pl.when(kv == 0)
    def _():
        q = q_ref[0]
        lane = lax.broadcasted_iota(jnp.int32, q.shape, 1)
        zero = jnp.zeros_like(q)
        q2_s[0:tq, :] = jnp.where(lane < A_HALF, q, zero)
        q2_s[tq:2 * tq, :] = jnp.where(lane >= A_HALF, q, zero)
        l_s[...] = jnp.zeros(l_s.shape, F32)
        acc_s[...] = jnp.zeros(acc_s.shape, F32)
        process(kc_ref[0], vct_ref[0, 0])

    for ci in range(tk // kc):
        process(k_ref[0, ci * kc:(ci + 1) * kc, :], vt_ref[0, 0, :, ci * kc:(ci + 1) * kc])

    @pl.when(kv == nkv - 1)
    def _():
        l = jnp.sum(l_s[...], axis=0, keepdims=True)
        o = acc_s[...] / l
        d = o[:, 0:tq] - lam_ref[0, 0] * o[:, tq:2 * tq]
        ms = jnp.mean(d * d, axis=0, keepdims=True)
        y = d * lax.rsqrt(ms + EPS) * g_ref[...] * out_scale
        o_ref[0] = y.T.astype(o_ref.dtype)


def _diffattn_fast(lam, p, pc, sub_g, *, heads, l, c, out_scale):
    b = p.shape[0]
    tq = min(256, l)
    tk = min(2048, l)
    kc = min(512, tk)
    nkv = l // tk
    k_blk = 2 * heads
    hd = heads * HEAD_DIM
    vt = jnp.transpose(p[:, :, 3 * hd:4 * hd].reshape(b, l, heads, HEAD_DIM), (0, 2, 3, 1))
    vct = jnp.transpose(pc[:, :, hd:2 * hd].reshape(b, c, heads, HEAD_DIM), (0, 2, 3, 1))
    kern = functools.partial(_diffattn_fast_kernel, tq=tq, tk=tk, kc=kc, nkv=nkv, out_scale=out_scale)
    return pl.pallas_call(
        kern,
        grid=(b, heads, l // tq, nkv),
        in_specs=[pl.BlockSpec(memory_space=pltpu.SMEM),
                  pl.BlockSpec((1, tq, HEAD_DIM), lambda bi, h, i, kv: (bi, i, h)),
                  pl.BlockSpec((1, c, HEAD_DIM), lambda bi, h, i, kv: (bi, 0, h)),
                  pl.BlockSpec((1, 1, HEAD_DIM, c), lambda bi, h, i, kv: (bi, h, 0, 0)),
                  pl.BlockSpec((1, tk, HEAD_DIM), lambda bi, h, i, kv: (bi, kv, k_blk + h)),
                  pl.BlockSpec((1, 1, HEAD_DIM, tk), lambda bi, h, i, kv: (bi, h, 0, kv)),
                  pl.BlockSpec((HEAD_DIM, 1), lambda bi, h, i, kv: (0, 0))],
        out_specs=pl.BlockSpec((1, tq, HEAD_DIM), lambda bi, h, i, kv: (bi, i, h)),
        out_shape=jax.ShapeDtypeStruct((b, l, hd), BF16),
        scratch_shapes=[pltpu.VMEM((2 * tq, HEAD_DIM), BF16),
                        pltpu.VMEM((8, 2 * tq), F32),
                        pltpu.VMEM((HEAD_DIM, 2 * tq), F32)],
        compiler_params=_params("parallel", "parallel", "parallel", "arbitrary"),
        name="diff_attention_bounded",
    )(lam, p, pc, vct, p, vt, sub_g.reshape(HEAD_DIM, 1))


SCORE_BOUND_LOG2 = 60.0


def _diffattn(lam, p, pc, sub_g, score_bound, *, heads, l, c, out_scale):
    kw = dict(heads=heads, l=l, c=c, out_scale=out_scale)
    return lax.cond(score_bound < SCORE_BOUND_LOG2,
                    lambda *a: _diffattn_fast(*a, **kw), lambda *a: _diffattn_safe(*a, **kw),
                    lam, p, pc, sub_g)


def _diffattn_safe(lam, p, pc, sub_g, *, heads, l, c, out_scale):
    b = p.shape[0]
    tq = min(256, l)
    tk = min(512, l)
    nkv = l // tk
    q_blk = 0
    k_blk = 2 * heads
    v_blk = 3 * heads
    kern = functools.partial(_diffattn_kernel, tq=tq, nkv=nkv, out_scale=out_scale)
    return pl.pallas_call(
        kern,
        grid=(b, heads, l // tq, nkv),
        in_specs=[pl.BlockSpec(memory_space=pltpu.SMEM),
                  pl.BlockSpec((1, tq, HEAD_DIM), lambda bi, h, i, kv: (bi, i, q_blk + h)),
                  pl.BlockSpec((1, c, HEAD_DIM), lambda bi, h, i, kv: (bi, 0, h)),
                  pl.BlockSpec((1, c, HEAD_DIM), lambda bi, h, i, kv: (bi, 0, heads + h)),
                  pl.BlockSpec((1, tk, HEAD_DIM), lambda bi, h, i, kv: (bi, kv, k_blk + h)),
                  pl.BlockSpec((1, tk, HEAD_DIM), lambda bi, h, i, kv: (bi, kv, v_blk + h)),
                  pl.BlockSpec((1, HEAD_DIM), lambda bi, h, i, kv: (0, 0))],
        out_specs=pl.BlockSpec((1, tq, HEAD_DIM), lambda bi, h, i, kv: (bi, i, h)),
        out_shape=jax.ShapeDtypeStruct((b, l, heads * HEAD_DIM), BF16),
        scratch_shapes=[pltpu.VMEM((2 * tq, HEAD_DIM), BF16),
                        pltpu.VMEM((2 * tq, 1), F32),
                        pltpu.VMEM((2 * tq, 1), F32),
                        pltpu.VMEM((2 * tq, HEAD_DIM), F32)],
        compiler_params=_params("parallel", "parallel", "parallel", "arbitrary"),
        name="diff_attention",
    )(lam, p, pc, pc, p, p, sub_g.reshape(1, HEAD_DIM))


def _winattn_kernel(sink_ref, q_ref, kp_ref, km_ref, kn_ref, vp_ref, vm_ref, vn_ref, kc_ref, vc_ref,
                    o_ref, *, tq, l, group):
    g = pl.program_id(1)
    i = pl.program_id(2)
    kband = jnp.concatenate([kp_ref[0], km_ref[0], kn_ref[0]], axis=0)
    vband = jnp.concatenate([vp_ref[0], vm_ref[0], vn_ref[0]], axis=0)
    nk = tq + 2 * WINDOW
    qpos = i * tq + lax.broadcasted_iota(jnp.int32, (tq, nk), 0)
    kpos = i * tq - WINDOW + lax.broadcasted_iota(jnp.int32, (tq, nk), 1)
    valid = (jnp.abs(kpos - qpos) <= WINDOW) & (kpos >= 0) & (kpos < l)
    kc = kc_ref[0]
    vc = vc_ref[0]
    dn = (((1,), (1,)), ((), ()))
    for r in range(group):
        sl = slice(r * HEAD_DIM, (r + 1) * HEAD_DIM)
        q = q_ref[0, :, sl]
        s_lat = jnp.where(valid, lax.dot_general(q, kband, dn, preferred_element_type=F32), NEG)
        s_ctx = lax.dot_general(q, kc, dn, preferred_element_type=F32)
        sk = sink_ref[g, r]
        m = jnp.maximum(jnp.maximum(jnp.max(s_lat, axis=-1, keepdims=True),
                                    jnp.max(s_ctx, axis=-1, keepdims=True)), sk)
        p_lat = jnp.exp(s_lat - m)
        p_ctx = jnp.exp(s_ctx - m)
        denom = (jnp.sum(p_lat, axis=-1, keepdims=True) + jnp.sum(p_ctx, axis=-1, keepdims=True)
                 + jnp.exp(sk - m))
        o = (jnp.dot(p_lat.astype(BF16), vband, preferred_element_type=F32)
             + jnp.dot(p_ctx.astype(BF16), vc, preferred_element_type=F32))
        o_ref[0, :, sl] = (o / denom).astype(o_ref.dtype)


def _winattn(sink, p, pc, *, a_heads, b_heads, l, c):
    b = p.shape[0]
    group = b_heads // B_KV_HEADS
    tq = min(256, l)
    wpb = tq // WINDOW
    nwb = l // WINDOW
    gw = group * HEAD_DIM
    q_blk = a_heads * HEAD_DIM // gw
    k_blk = (a_heads + b_heads) + 2 * a_heads
    v_blk = k_blk + B_KV_HEADS
    kc_blk = 2 * a_heads
    vc_blk = kc_blk + B_KV_HEADS
    kern = functools.partial(_winattn_kernel, tq=tq, l=l, group=group)

    def prev_map(col):
        return lambda bi, g, i: (bi, jnp.maximum(i * wpb - 1, 0), col + g)

    def main_map(col):
        return lambda bi, g, i: (bi, i, col + g)

    def next_map(col):
        return lambda bi, g, i: (bi, jnp.minimum((i + 1) * wpb, nwb - 1), col + g)

    return pl.pallas_call(
        kern,
        grid=(b, B_KV_HEADS, l // tq),
        in_specs=[pl.BlockSpec(memory_space=pltpu.SMEM),
                  pl.BlockSpec((1, tq, gw), lambda bi, g, i: (bi, i, q_blk + g)),
                  pl.BlockSpec((1, WINDOW, HEAD_DIM), prev_map(k_blk)),
                  pl.BlockSpec((1, tq, HEAD_DIM), main_map(k_blk)),
                  pl.BlockSpec((1, WINDOW, HEAD_DIM), next_map(k_blk)),
                  pl.BlockSpec((1, WINDOW, HEAD_DIM), prev_map(v_blk)),
                  pl.BlockSpec((1, tq, HEAD_DIM), main_map(v_blk)),
                  pl.BlockSpec((1, WINDOW, HEAD_DIM), next_map(v_blk)),
                  pl.BlockSpec((1, c, HEAD_DIM), lambda bi, g, i: (bi, 0, kc_blk + g)),
                  pl.BlockSpec((1, c, HEAD_DIM), lambda bi, g, i: (bi, 0, vc_blk + g))],
        out_specs=pl.BlockSpec((1, tq, gw), lambda bi, g, i: (bi, i, g)),
        out_shape=jax.ShapeDtypeStruct((b, l, b_heads * HEAD_DIM), BF16),
        compiler_params=_params("parallel", "parallel", "parallel"),
        name="window_attention",
    )(sink, p, p, p, p, p, p, p, pc, pc)


def _mm_resid_kernel(*refs, nparts):
    a_refs = refs[0:nparts]
    w_refs = refs[nparts:2 * nparts]
    res_ref, gate_ref, o_ref = refs[2 * nparts:]
    y = jnp.dot(a_refs[0][0], w_refs[0][...], preferred_element_type=F32)
    for a_ref, w_ref in zip(a_refs[1:], w_refs[1:]):
        y = y + jnp.dot(a_ref[0], w_ref[...], preferred_element_type=F32)
    o_ref[0] = res_ref[0] + gate_ref[0] * y


def _mm_resid(parts, ws, res, gate):
    b, l, n = res.shape
    tm = min(512, l)
    tn = min(512, n)
    nparts = len(parts)
    in_specs = ([pl.BlockSpec((1, tm, a.shape[2]), lambda bi, i, j: (bi, i, 0)) for a in parts]
                + [pl.BlockSpec((w.shape[0], tn), lambda bi, i, j: (0, j)) for w in ws]
                + [pl.BlockSpec((1, tm, tn), lambda bi, i, j: (bi, i, j)),
                   pl.BlockSpec((1, 1, tn), lambda bi, i, j: (bi, 0, j))])
    return pl.pallas_call(
        functools.partial(_mm_resid_kernel, nparts=nparts),
        grid=(b, l // tm, n // tn),
        in_specs=in_specs,
        out_specs=pl.BlockSpec((1, tm, tn), lambda bi, i, j: (bi, i, j)),
        out_shape=jax.ShapeDtypeStruct((b, l, n), F32),
        compiler_params=_params("parallel", "parallel", "arbitrary"),
        name="out_projection_gated_residual",
    )(*parts, *ws, res, gate)


HALO = BF16_SUBLANES


def _fill_rows(hbuf, hp_ref, hm_ref, hn_ref, tm):
    hbuf[0:HALO, :] = hp_ref[0]
    hbuf[HALO:HALO + tm, :] = hm_ref[0]
    hbuf[HALO + tm:2 * HALO + tm, :] = hn_ref[0]


def _conv3(zbuf, cw_ref, cb_ref, tm, first, last):
    @pl.when(first)
    def _():
        zbuf[HALO - 1:HALO, :] = jnp.zeros((1, zbuf.shape[1]), F32)

    @pl.when(last)
    def _():
        zbuf[HALO + tm:HALO + tm + 1, :] = jnp.zeros((1, zbuf.shape[1]), F32)

    return (zbuf[HALO - 1:HALO - 1 + tm, :] * cw_ref[0:1, :] + zbuf[HALO:HALO + tm, :] * cw_ref[1:2, :]
            + zbuf[HALO + 1:HALO + 1 + tm, :] * cw_ref[2:3, :] + cb_ref[...])


def _gelu_tanh(x):
    return 0.5 * x * (1.0 + jnp.tanh(math.sqrt(2.0 / math.pi) * (x + 0.044715 * (x * x * x))))


def _ffn_up_kernel(hp_ref, hm_ref, hn_ref, wg_ref, wv_ref, cw_ref, cb_ref, o_ref, hbuf, zbuf, *, tm, nt):
    i = pl.program_id(1)
    j = pl.program_id(2)

    @pl.when(j == 0)
    def _():
        _fill_rows(hbuf, hp_ref, hm_ref, hn_ref, tm)

    zbuf[...] = jnp.dot(hbuf[...], wg_ref[...], preferred_element_type=F32)
    g = _conv3(zbuf, cw_ref, cb_ref, tm, i == 0, i == nt - 1)
    v = jnp.dot(hbuf[HALO:HALO + tm, :], wv_ref[...], preferred_element_type=F32)
    o_ref[0] = (_gelu_tanh(g) * v).astype(o_ref.dtype)


def _hy_in_kernel(hp_ref, hm_ref, hn_ref, w_ref, b_ref, cw_ref, cb_ref, o_ref, hbuf, zbuf, *, tm, nt):
    i = pl.program_id(1)
    j = pl.program_id(2)

    @pl.when(j == 0)
    def _():
        _fill_rows(hbuf, hp_ref, hm_ref, hn_ref, tm)

    zbuf[...] = jnp.dot(hbuf[...], w_ref[...], preferred_element_type=F32) + b_ref[...]
    o_ref[0] = _conv3(zbuf, cw_ref, cb_ref, tm, i == 0, i == nt - 1).astype(o_ref.dtype)


def _mm_conv(kernel, h, ws, vecs, n, out_dtype, name):
    b, l, k = h.shape
    tm = min(512, l)
    tn = min(512, n)
    nt = l // tm
    hpb = tm // HALO
    nhb = l // HALO
    in_specs = ([pl.BlockSpec((1, HALO, k), lambda bi, i, j: (bi, jnp.maximum(i * hpb - 1, 0), 0)),
                 pl.BlockSpec((1, tm, k), lambda bi, i, j: (bi, i, 0)),
                 pl.BlockSpec((1, HALO, k), lambda bi, i, j: (bi, jnp.minimum((i + 1) * hpb, nhb - 1), 0))]
                + [pl.BlockSpec((k, tn), lambda bi, i, j: (0, j)) for _ in ws]
                + [pl.BlockSpec((v.shape[0], tn), lambda bi, i, j: (0, j)) for v in vecs])
    return pl.pallas_call(
        functools.partial(kernel, tm=tm, nt=nt),
        grid=(b, nt, n // tn),
        in_specs=in_specs,
        out_specs=pl.BlockSpec((1, tm, tn), lambda bi, i, j: (bi, i, j)),
        out_shape=jax.ShapeDtypeStruct((b, l, n), out_dtype),
        scratch_shapes=[pltpu.VMEM((tm + 2 * HALO, k), BF16),
                        pltpu.VMEM((tm + 2 * HALO, tn), F32)],
        compiler_params=_params("parallel", "parallel", "arbitrary"),
        name=name,
    )(h, h, h, *ws, *vecs)


def _filter_kernel(z_ref, w0_ref, b0_ref, w1_ref, b1_ref, w2_ref, b2_ref, fr_ref, w3_ref, dl_ref,
                   o_ref, s_ref, *, tt):
    it = pl.program_id(1)
    z = z_ref[...]
    fr = fr_ref[...]

    def layer(a, w_ref, b_ref):
        return jnp.sin(fr * (jnp.dot(a, w_ref[...], preferred_element_type=F32, precision=HIGHEST)
                             + b_ref[...]))

    a = layer(layer(layer(z, w0_ref, b0_ref), w1_ref, b1_ref), w2_ref, b2_ref)
    decay = jnp.exp(-z[:, 0:1] * dl_ref[...])
    row = it * tt + lax.broadcasted_iota(jnp.int32, decay.shape, 0)

    @pl.when(it == 0)
    def _():
        s_ref[...] = jnp.zeros(s_ref.shape, F32)

    for s in range(2 * HY_ORDER):
        h = jnp.dot(a, w3_ref[s], preferred_element_type=F32, precision=HIGHEST) * decay
        if s % 2 == 1:
            h = jnp.where(row == 0, 0.0, h)
        o_ref[s] = h.astype(o_ref.dtype)
        s_ref[s] += jnp.sum(jnp.abs(h), axis=0, keepdims=True)


def _hyena_filters(l, d, f_w0, f_b0, f_w1, f_b1, f_w2, f_b2, f_freq, f_w3):
    emb, fw = f_w0.shape
    bands = (emb - 1) // 2
    t = np.linspace(0.0, 1.0, l)[:, None]
    w = 2.0 * math.pi * np.arange(l)[:, None] / l
    f = np.linspace(1e-4, bands - 1, bands)[None, :]
    emb_pad = -(-emb // 8) * 8
    z = np.concatenate([t, np.cos(f * w), -np.sin(f * w), np.zeros((l, emb_pad - emb))], axis=-1)
    z = jnp.asarray(z, F32)
    w0 = jnp.concatenate([f_w0, jnp.zeros((emb_pad - emb, fw), F32)], axis=0)
    max_decay = math.log(HY_DECAY_TARGET) / HY_MAX_DECAY_PCT
    min_decay = math.log(HY_DECAY_TARGET) / HY_MIN_DECAY_PCT
    dl = jnp.asarray(np.abs(np.linspace(min_decay, max_decay, d))[None, :], F32)
    nseg = 2 * HY_ORDER
    w3 = jnp.transpose(f_w3.reshape(fw, nseg, d), (1, 0, 2))
    tt = min(256, l)
    td = min(512, d)
    vec = lambda a: a.reshape(1, fw)
    small = lambda shape: pl.BlockSpec(shape, lambda jd, it: (0,) * len(shape))
    return pl.pallas_call(
        functools.partial(_filter_kernel, tt=tt),
        grid=(d // td, l // tt),
        in_specs=[pl.BlockSpec((tt, emb_pad), lambda jd, it: (it, 0)),
                  small((emb_pad, fw)), small((1, fw)), small((fw, fw)), small((1, fw)),
                  small((fw, fw)), small((1, fw)), small((1, fw)),
                  pl.BlockSpec((nseg, fw, td), lambda jd, it: (0, 0, jd)),
                  pl.BlockSpec((1, td), lambda jd, it: (0, jd))],
        out_specs=[pl.BlockSpec((nseg, tt, td), lambda jd, it: (0, it, jd)),
                   pl.BlockSpec((nseg, 1, td), lambda jd, it: (0, 0, jd))],
        out_shape=[jax.ShapeDtypeStruct((nseg, l, d), BF16),
                   jax.ShapeDtypeStruct((nseg, 1, d), F32)],
        compiler_params=_params("parallel", "arbitrary"),
        name="hyena_filters",
    )(z, w0, vec(f_b0), f_w1, vec(f_b1), f_w2, vec(f_b2), vec(f_freq), w3, dl)


def _dft_tables(l):
    n1 = FFT_N1
    n = 2 * l
    n2 = n // n1
    k1 = np.arange(n1)[:, None]
    f1 = np.exp(-2j * np.pi * k1 * np.arange(n1 // 2)[None, :] / n1)
    tw = np.exp(-2j * np.pi * np.arange(n2)[:, None] * np.arange(n1)[None, :] / n)
    f1r, f1i = jnp.asarray(f1.real, F32), jnp.asarray(f1.imag, F32)
    twr, twi = jnp.asarray(tw.real, F32)[:, :, None], jnp.asarray(tw.imag, F32)[:, :, None]
    fr = twr * f1r - twi * f1i
    fi = twr * f1i + twi * f1r
    ta = jnp.concatenate([jnp.concatenate([fr, -fi], axis=2),
                          jnp.concatenate([fi, fr], axis=2)], axis=1)
    fb = np.exp(-2j * np.pi * np.arange(n2)[:, None] * np.arange(n2)[None, :] / n2)
    tb = np.block([[fb.real, -fb.imag], [fb.imag, fb.real]])
    return (ta.astype(BF16), jnp.swapaxes(ta, 1, 2).astype(BF16),
            jnp.asarray(tb, BF16), jnp.asarray(tb.T, BF16))


def _fft_a_kernel(x_ref, t_ref, o_ref):
    x = x_ref[0]
    x = x.reshape(x.shape[0] * x.shape[1], x.shape[2]).astype(BF16)
    o_ref[0] = jnp.dot(t_ref[0], x, preferred_element_type=F32).astype(o_ref.dtype)


def _fft_a(x4, ta, *, n2, d, width, seg):
    s, p, half, _ = x4.shape
    kdim = p * half
    tn = min(2048, d)
    wb, sb, db = width // tn, (seg * d) // tn, d // tn
    return pl.pallas_call(
        _fft_a_kernel,
        grid=(s, n2, db),
        in_specs=[pl.BlockSpec((1, p, half, tn), lambda si, m, j: (si, 0, 0, m * wb + sb + j)),
                  pl.BlockSpec((1, 2 * FFT_N1, kdim), lambda si, m, j: (m, 0, 0))],
        out_specs=pl.BlockSpec((1, 2 * FFT_N1, tn), lambda si, m, j: (si, 0, m * db + j)),
        out_shape=jax.ShapeDtypeStruct((s, 2 * FFT_N1, n2 * d), BF16),
        compiler_params=_params("parallel", "parallel", "parallel"),
        name="dft_stage_a",
    )(x4, ta)


def _fft_bfilt_kernel(af_ref, ab_ref, tb_ref, sc_ref, o_ref, *, n2):
    tb = tb_ref[...]
    hf = jnp.dot(tb, af_ref[0].reshape(2 * n2, -1), preferred_element_type=F32)
    hb = jnp.dot(tb, ab_ref[0].reshape(2 * n2, -1), preferred_element_type=F32)
    sc = sc_ref[0]
    o_ref[0, 0, 0] = (hf[0:n2] + hb[0:n2]) * sc
    o_ref[0, 1, 0] = (hf[n2:2 * n2] - hb[n2:2 * n2]) * sc


def _fft_bfilt(af, tb, scale, *, n2, d):
    tn = min(2048, d)
    blk = (1, 2, 1, n2, tn)
    return pl.pallas_call(
        functools.partial(_fft_bfilt_kernel, n2=n2),
        grid=(HY_ORDER, FFT_N1, d // tn),
        in_specs=[pl.BlockSpec(blk, lambda o, k, j: (2 * o, 0, k, 0, j)),
                  pl.BlockSpec(blk, lambda o, k, j: (2 * o + 1, 0, k, 0, j)),
                  pl.BlockSpec((2 * n2, 2 * n2), lambda o, k, j: (0, 0)),
                  pl.BlockSpec((1, 1, tn), lambda o, k, j: (o, 0, j))],
        out_specs=pl.BlockSpec(blk, lambda o, k, j: (o, 0, k, 0, j)),
        out_shape=jax.ShapeDtypeStruct((HY_ORDER, 2, FFT_N1, n2, d), F32),
        compiler_params=_params("parallel", "parallel", "parallel"),
        name="dft_stage_b_filters",
    )(af, af, tb, scale)


def _fft_b_kernel(a_ref, g_ref, tb_ref, tbt_ref, o_ref, *, n2):
    z = jnp.dot(tb_ref[...], a_ref[...].reshape(2 * n2, -1), preferred_element_type=F32)
    zr, zi = z[0:n2], z[n2:2 * n2]
    gr, gi = g_ref[0, 0, 0], g_ref[0, 1, 0]
    prod = jnp.concatenate([zr * gr - zi * gi, zr * gi + zi * gr], axis=0).astype(BF16)
    y = jnp.dot(tbt_ref[...], prod, preferred_element_type=F32)
    o_ref[...] = y.reshape(o_ref.shape).astype(o_ref.dtype)


def _fft_b(a, g, tb, tbt, order, *, n2, d):
    tn = min(2048, d)
    blk = (2, 1, n2, tn)
    return pl.pallas_call(
        functools.partial(_fft_b_kernel, n2=n2),
        grid=(FFT_N1, d // tn),
        in_specs=[pl.BlockSpec(blk, lambda k, j: (0, k, 0, j)),
                  pl.BlockSpec((1,) + blk, lambda k, j: (order, 0, k, 0, j)),
                  pl.BlockSpec((2 * n2, 2 * n2), lambda k, j: (0, 0)),
                  pl.BlockSpec((2 * n2, 2 * n2), lambda k, j: (0, 0))],
        out_specs=pl.BlockSpec(blk, lambda k, j: (0, k, 0, j)),
        out_shape=jax.ShapeDtypeStruct((2, FFT_N1, n2, d), BF16),
        compiler_params=_params("parallel", "parallel"),
        name="dft_stage_b_spectrum_product",
    )(a, g, tb, tbt)


def _fft_ainv_kernel(y_ref, t_ref, u_ref, gate_ref, d_ref, o_ref):
    x = jnp.dot(t_ref[0], y_ref[...], preferred_element_type=F32)
    x = x.reshape(o_ref.shape)
    o_ref[...] = (gate_ref[...] * (x + d_ref[...] * u_ref[...])).astype(o_ref.dtype)


def _fft_ainv(y2, tat, u3, gate3, dskip, *, n2, d, uw, useg, gw, gseg, out_dtype):
    b, half, _ = u3.shape
    tn = min(1024, d)
    db = d // tn
    ub, usb = uw // tn, (useg * d) // tn
    gb, gsb = gw // tn, (gseg * d) // tn
    return pl.pallas_call(
        _fft_ainv_kernel,
        grid=(n2, db),
        in_specs=[pl.BlockSpec((2 * FFT_N1, tn), lambda m, j: (0, m * db + j)),
                  pl.BlockSpec((1, FFT_N1, 2 * FFT_N1), lambda m, j: (m, 0, 0)),
                  pl.BlockSpec((b, half, tn), lambda m, j: (0, 0, m * ub + usb + j)),
                  pl.BlockSpec((b, half, tn), lambda m, j: (0, 0, m * gb + gsb + j)),
                  pl.BlockSpec((1, tn), lambda m, j: (0, j))],
        out_specs=pl.BlockSpec((b, half, tn), lambda m, j: (0, 0, m * db + j)),
        out_shape=jax.ShapeDtypeStruct((b, half, n2 * d), out_dtype),
        compiler_params=_params("parallel", "parallel"),
        name="dft_stage_a_inverse_gate",
    )(y2, tat, u3, gate3, dskip)


def _hyena_mixer(h, hy, l, d):
    (w_in, b_in, sconv_w, sconv_b, f_w0, f_b0, f_w1, f_b1, f_w2, f_b2, f_freq, f_w3, d_skip, w_out) = hy
    b = h.shape[0]
    assert b == 2, "the two batch rows ride the real / imaginary planes of one complex DFT"
    n1 = FFT_N1
    half = n1 // 2
    n2 = (2 * l) // n1
    assert half * n2 == l
    z3 = _mm_conv(_hy_in_kernel, h, [w_in.astype(BF16)],
                  [b_in.reshape(1, -1), sconv_w, sconv_b.reshape(1, -1)], 3 * d, F32, "hyena_in_proj_conv")
    filt, fsum = _hyena_filters(l, d, f_w0, f_b0, f_w1, f_b1, f_w2, f_b2, f_freq, f_w3)
    ta, tat, tb, tbt = _dft_tables(l)
    nseg = 2 * HY_ORDER
    af = _fft_a(filt.reshape(nseg, 1, half, n2 * d), ta, n2=n2, d=d, width=d, seg=0)
    af = af.reshape(nseg, 2, n1, n2, d)
    norm = (fsum[0::2] + fsum[1::2])
    g = _fft_bfilt(af, tb, 1.0 / (norm * (2 * l)), n2=n2, d=d)
    z3v = z3.reshape(b, half, n2 * 3 * d)
    y = z3v
    yw, yseg = 3 * d, 0
    for o in range(HY_ORDER):
        a = _fft_a(y[None], ta, n2=n2, d=d, width=yw, seg=yseg)
        yb = _fft_b(a.reshape(2, n1, n2, d), g, tb, tbt, o, n2=n2, d=d)
        last = o == HY_ORDER - 1
        y = _fft_ainv(yb.reshape(2 * n1, n2 * d), tat, y, z3v, d_skip[o].reshape(1, d),
                      n2=n2, d=d, uw=yw, useg=yseg, gw=3 * d, gseg=o + 1,
                      out_dtype=BF16 if last else F32)
        yw, yseg = d, 0
    return y.reshape(b, l, d), w_out.astype(BF16)


def _rope_tables(l):
    t = jnp.arange(l)
    row = (t // GRID_W).astype(F32)[:, None]
    col = (t % GRID_W).astype(F32)[:, None]
    cos, sin = [], []
    for dim in (A_HALF, HEAD_DIM):
        nf = dim // 4
        inv = jnp.asarray(ROPE_BASE ** (-np.arange(nf) / nf), F32)[None, :]
        ar, ac = row * inv, col * inv
        c = jnp.concatenate([jnp.cos(ar), jnp.cos(ar), jnp.cos(ac), jnp.cos(ac)], axis=1)
        s = jnp.concatenate([-jnp.sin(ar), jnp.sin(ar), -jnp.sin(ac), jnp.sin(ac)], axis=1)
        reps = HEAD_DIM // dim
        cos.append(jnp.tile(c, (1, reps)))
        sin.append(jnp.tile(s, (1, reps)))
    return jnp.stack(cos), jnp.stack(sin)


def _attn_mixer(h, hc, lam_init, w_in, w_out, a_q_g, a_k_g, lq1, lk1, lq2, lk2, a_sub_g, b_q_g, b_k_g, b_sink):
    b, l, d = h.shape
    c = hc.shape[1]
    a_heads = d // (2 * HEAD_DIM)
    b_heads = d // (2 * HEAD_DIM)
    q_cols = (a_heads + b_heads) * HEAD_DIM
    in_cols = w_in.shape[1]
    lam = (jnp.exp(jnp.sum(lq1 * lk1)) - jnp.exp(jnp.sum(lq2 * lk2)) + lam_init).reshape(1, 1)
    sink = b_sink.reshape(B_KV_HEADS, b_heads // B_KV_HEADS)

    ones = lambda n: jnp.ones((n,), F32)
    tile = lambda v, n: jnp.tile(v, n)
    gain = jnp.concatenate([tile(a_q_g, 2 * a_heads), tile(b_q_g, b_heads), tile(a_k_g, 2 * a_heads),
                            ones(a_heads * HEAD_DIM), tile(b_k_g, B_KV_HEADS),
                            ones(B_KV_HEADS * HEAD_DIM)]).reshape(1, in_cols)
    scale = jnp.concatenate([jnp.full((a_heads * HEAD_DIM,), A_HALF ** -0.5 * math.log2(math.e), F32),
                             jnp.full((b_heads * HEAD_DIM,), HEAD_DIM ** -0.5, F32),
                             ones(in_cols - q_cols)]).reshape(1, in_cols)
    e = [0, a_heads // 2, (a_heads + b_heads) // 2, (2 * a_heads + b_heads) // 2,
         (3 * a_heads + b_heads) // 2, (3 * a_heads + b_heads) // 2 + 1, (3 * a_heads + b_heads) // 2 + 2]
    chunks = [A_HALF, HEAD_DIM, A_HALF, 0, HEAD_DIM, 0]
    modes = tuple((e[s], e[s + 1], chunks[s]) for s in range(6))
    cos_tab, sin_tab = _rope_tables(l)

    def type_of(j0):
        def f(j):
            jj = j + j0
            is_a = (jj < e[1]) | ((jj >= e[2]) & (jj < e[3]))
            return jnp.where(is_a, 0, 1)
        return f

    w_bf = w_in.astype(BF16)
    p = _inproj(h.reshape(b * l, d), w_bf, gain, scale, cos_tab, sin_tab, modes=modes,
                type_of_block=type_of(0), col0=0, ncols=in_cols, l=l, rope=True).reshape(b, l, in_cols)
    kv0 = e[2]
    modes_c = tuple((lo - kv0, hi - kv0, ch) for lo, hi, ch in modes[2:])
    pc = _inproj(hc.reshape(b * c, d), w_bf, gain, scale, cos_tab, sin_tab, modes=modes_c,
                 type_of_block=type_of(kv0), col0=kv0, ncols=in_cols - q_cols, l=c, rope=False
                 ).reshape(b, c, in_cols - q_cols)
    score_bound = (A_HALF ** 0.5 * math.log2(math.e)) * jnp.max(jnp.abs(a_q_g)) * jnp.max(jnp.abs(a_k_g))
    oa = _diffattn(lam, p, pc, a_sub_g, score_bound, heads=a_heads, l=l, c=c, out_scale=1.0 - lam_init)
    ob = _winattn(sink, p, pc, a_heads=a_heads, b_heads=b_heads, l=l, c=c)
    w_out_bf = w_out.astype(BF16)
    na = a_heads * HEAD_DIM
    return [oa, ob], [w_out_bf[:na], w_out_bf[na:]]


def kernel(x, c, ctx, c_ctx, ada_w, ada_b, norm1_g, norm2_g, attn_w_in, attn_w_out, a_q_g, a_k_g, a_lam_q1, a_lam_k1, a_lam_q2, a_lam_k2, a_sub_g, b_q_g, b_k_g, b_sink, hy_w_in, hy_b_in, hy_sconv_w, hy_sconv_b, hy_f_w0, hy_f_b0, hy_f_w1, hy_f_b1, hy_f_w2, hy_f_b2, hy_f_freq, hy_f_w3, hy_d, hy_w_out, ffn_w_gate, ffn_w_val, ffn_conv_w, ffn_conv_b, ffn_w_down):
    b, l, d = x.shape
    depth = ada_w.shape[0]
    cc = jnp.concatenate([c, c_ctx[None, :], jnp.zeros((8 - b - 1, d), F32)], axis=0)
    m = _ada(cc, ada_w, ada_b)
    xs = x
    for layer in range(depth):
        i = layer // 2
        lat = [m[layer, :b, k * d:(k + 1) * d].reshape(b, 1, d) for k in range(6)]
        sh1, sc1, g1, sh2, sc2, g2 = lat
        h = _normmod(xs, norm1_g[layer], sc1, sh1)
        if layer % 2 == 0:
            mc = [jnp.broadcast_to(m[layer, b, k * d:(k + 1) * d].reshape(1, 1, d), (b, 1, d)) for k in range(2)]
            hc = _normmod(ctx, norm1_g[layer], mc[1], mc[0])
            parts, ws = _attn_mixer(h, hc, 0.8 - 0.6 * math.exp(-0.3 * layer), attn_w_in[i], attn_w_out[i],
                                    a_q_g[i], a_k_g[i], a_lam_q1[i], a_lam_k1[i], a_lam_q2[i], a_lam_k2[i],
                                    a_sub_g[i], b_q_g[i], b_k_g[i], b_sink[i])
        else:
            hy = (hy_w_in[i], hy_b_in[i], hy_sconv_w[i], hy_sconv_b[i], hy_f_w0[i], hy_f_b0[i], hy_f_w1[i],
                  hy_f_b1[i], hy_f_w2[i], hy_f_b2[i], hy_f_freq[i], hy_f_w3[i], hy_d[i], hy_w_out[i])
            y, w_o = _hyena_mixer(h, hy, l, d)
            parts, ws = [y], [w_o]
        xs = _mm_resid(parts, ws, xs, g1)
        h2 = _normmod(xs, norm2_g[layer], sc2, sh2)
        d_ff = ffn_w_gate.shape[2]
        hid = _mm_conv(_ffn_up_kernel, h2, [ffn_w_gate[layer].astype(BF16), ffn_w_val[layer].astype(BF16)],
                       [ffn_conv_w[layer], ffn_conv_b[layer].reshape(1, -1)], d_ff, BF16, "ffn_up_conv_glu")
        xs = _mm_resid([hid], [ffn_w_down[layer].astype(BF16)], xs, g2)
    return xs
```

```python
import functools
import math

import numpy as np
import jax
import jax.numpy as jnp
from jax import lax
from jax.experimental import pallas as pl
from jax.experimental.pallas import tpu as pltpu

F32 = jnp.float32
BF16 = jnp.bfloat16
HIGHEST = lax.Precision.HIGHEST

HEAD_DIM = 128
A_HALF = HEAD_DIM // 2
B_KV_HEADS = 2
GRID_W = 64
WINDOW = 128
ROPE_BASE = 10000.0
EPS = 1e-6
NEG = -1e30
HY_ORDER = 2
HY_DECAY_TARGET = 1e-2
HY_MAX_DECAY_PCT = 0.3
HY_MIN_DECAY_PCT = 1.5

LANES = 128
BF16_SUBLANES = 16
FFT_N1 = 256
VMEM_LIMIT_BYTES = 56 * 1024 * 1024


def _params(*sem):
    return pltpu.CompilerParams(dimension_semantics=sem, vmem_limit_bytes=VMEM_LIMIT_BYTES)


def _ada_kernel(c_ref, w_ref, b_ref, o_ref):
    c = c_ref[...]
    s = c * (1.0 / (1.0 + jnp.exp(-c)))
    o_ref[0] = jnp.dot(s, w_ref[0], preferred_element_type=F32, precision=HIGHEST) + b_ref[0]


def _ada(cc, ada_w, ada_b):
    depth, d, n = ada_w.shape
    tn = 1024
    return pl.pallas_call(
        _ada_kernel,
        grid=(depth, n // tn),
        in_specs=[pl.BlockSpec((8, d), lambda l, j: (0, 0)),
                  pl.BlockSpec((1, d, tn), lambda l, j: (l, 0, j)),
                  pl.BlockSpec((1, 1, tn), lambda l, j: (l, 0, j))],
        out_specs=pl.BlockSpec((1, 8, tn), lambda l, j: (l, 0, j)),
        out_shape=jax.ShapeDtypeStruct((depth, 8, n), F32),
        compiler_params=_params("parallel", "parallel"),
        name="ada_modulation",
    )(cc, ada_w, ada_b.reshape(depth, 1, n))


def _normmod_kernel(x_ref, g_ref, sc_ref, sh_ref, o_ref):
    x = x_ref[0]
    ms = jnp.mean(x * x, axis=-1, keepdims=True)
    y = x * lax.rsqrt(ms + EPS) * g_ref[...]
    o_ref[0] = (y * (1.0 + sc_ref[0]) + sh_ref[0]).astype(o_ref.dtype)


def _normmod(x, g, sc, sh):
    b, l, d = x.shape
    tm = min(512, l)
    return pl.pallas_call(
        _normmod_kernel,
        grid=(b, l // tm),
        in_specs=[pl.BlockSpec((1, tm, d), lambda bi, i: (bi, i, 0)),
                  pl.BlockSpec((1, d), lambda bi, i: (0, 0)),
                  pl.BlockSpec((1, 1, d), lambda bi, i: (bi, 0, 0)),
                  pl.BlockSpec((1, 1, d), lambda bi, i: (bi, 0, 0))],
        out_specs=pl.BlockSpec((1, tm, d), lambda bi, i: (bi, i, 0)),
        out_shape=jax.ShapeDtypeStruct((b, l, d), BF16),
        compiler_params=_params("parallel", "parallel"),
        name="rmsnorm_modulate",
    )(x, g.reshape(1, d), sc, sh)


def _head_epilogue(z, gain, scale, cos, sin, chunk, rope):
    lane = lax.broadcasted_iota(jnp.int32, z.shape, 1)
    zz = z * z
    s_all = jnp.sum(zz, axis=-1, keepdims=True)
    if chunk == HEAD_DIM:
        ms = s_all * (1.0 / HEAD_DIM)
    else:
        s_lo = jnp.sum(jnp.where(lane < A_HALF, zz, 0.0), axis=-1, keepdims=True)
        ms = jnp.where(lane < A_HALF, s_lo, s_all - s_lo) * (1.0 / A_HALF)
    y = z * lax.rsqrt(ms + EPS) * gain
    if rope:
        nf = chunk // 4
        first = (lane % (2 * nf)) < nf
        partner = jnp.where(first, pltpu.roll(y, HEAD_DIM - nf, 1), pltpu.roll(y, nf, 1))
        y = y * cos + partner * sin
    return y * scale


def _inproj_kernel(h_ref, w_ref, gain_ref, scale_ref, cos_ref, sin_ref, o_ref, *, modes, rope):
    j = pl.program_id(1)
    z = jnp.dot(h_ref[...], w_ref[...], preferred_element_type=F32)
    for lo, hi, chunk in modes:
        @pl.when((j >= lo) & (j < hi))
        def _(chunk=chunk):
            for half in range(2):
                sl = slice(half * HEAD_DIM, (half + 1) * HEAD_DIM)
                zh = z[:, sl]
                if chunk:
                    zh = _head_epilogue(zh, gain_ref[:, sl], scale_ref[:, sl],
                                        cos_ref[0], sin_ref[0], chunk, rope)
                o_ref[:, sl] = zh.astype(o_ref.dtype)


def _inproj(h2d, w, gain, scale, cos_tab, sin_tab, *, modes, type_of_block, col0, ncols, l, rope):
    m, d = h2d.shape
    tn = 2 * HEAD_DIM
    tm = min(512, l)
    pos_blocks = l // tm
    kern = functools.partial(_inproj_kernel, modes=modes, rope=rope)
    return pl.pallas_call(
        kern,
        grid=(m // tm, ncols // tn),
        in_specs=[pl.BlockSpec((tm, d), lambda i, j: (i, 0)),
                  pl.BlockSpec((d, tn), lambda i, j: (0, j + col0)),
                  pl.BlockSpec((1, tn), lambda i, j: (0, j + col0)),
                  pl.BlockSpec((1, tn), lambda i, j: (0, j + col0)),
                  pl.BlockSpec((1, tm, HEAD_DIM), lambda i, j: (type_of_block(j), i % pos_blocks, 0)),
                  pl.BlockSpec((1, tm, HEAD_DIM), lambda i, j: (type_of_block(j), i % pos_blocks, 0))],
        out_specs=pl.BlockSpec((tm, tn), lambda i, j: (i, j)),
        out_shape=jax.ShapeDtypeStruct((m, ncols), BF16),
        compiler_params=_params("parallel", "arbitrary"),
        name="attn_in_projection",
    )(h2d, w, gain, scale, cos_tab, sin_tab)


def _diffattn_kernel(lam_ref, q_ref, kc_ref, vc_ref, k_ref, v_ref, g_ref, o_ref,
                     q2_s, m_s, l_s, acc_s, *, tq, nkv, out_scale):
    kv = pl.program_id(3)

    def process(k, v):
        s = lax.dot_general(q2_s[...], k, (((1,), (1,)), ((), ())), preferred_element_type=F32)
        m_prev = m_s[...]
        m_new = jnp.maximum(m_prev, jnp.max(s, axis=-1, keepdims=True))
        alpha = jnp.exp2(m_prev - m_new)
        p = jnp.exp2(s - m_new)
        l_s[...] = alpha * l_s[...] + jnp.sum(p, axis=-1, keepdims=True)
        acc_s[...] = alpha * acc_s[...] + jnp.dot(p.astype(BF16), v, preferred_element_type=F32)
        m_s[...] = m_new

    @pl.when(kv == 0)
    def _():
        q = q_ref[0]
        lane = lax.broadcasted_iota(jnp.int32, q.shape, 1)
        zero = jnp.zeros_like(q)
        q2_s[0:tq, :] = jnp.where(lane < A_HALF, q, zero)
        q2_s[tq:2 * tq, :] = jnp.where(lane >= A_HALF, q, zero)
        m_s[...] = jnp.full(m_s.shape, -jnp.inf, F32)
        l_s[...] = jnp.zeros(l_s.shape, F32)
        acc_s[...] = jnp.zeros(acc_s.shape, F32)
        process(kc_ref[0], vc_ref[0])

    process(k_ref[0], v_ref[0])

    @pl.when(kv == nkv - 1)
    def _():
        o = acc_s[...] / l_s[...]
        d = o[0:tq] - lam_ref[0, 0] * o[tq:2 * tq]
        ms = jnp.mean(d * d, axis=-1, keepdims=True)
        o_ref[0] = (d * lax.rsqrt(ms + EPS) * g_ref[...] * out_scale).astype(o_ref.dtype)


def _diffattn_fast_kernel(lam_ref, q_ref, kc_ref, vct_ref, k_ref, vt_ref, g_ref, o_ref,
                          q2_s, l_s, acc_s, p_s, *, tq, tk, kc, nkv, out_scale):
    kv = pl.program_id(3)
    dn = (((1,), (1,)), ((), ()))

    def weights(k):
        n = k.shape[0]
        p = jnp.exp2(lax.dot_general(k, q2_s[...], dn, preferred_element_type=F32))
        return p.astype(BF16), jnp.sum(p.reshape(n // 8, 8, 2 * tq), axis=0)

    @pl.when(kv == 0)
    def _():
        q = q_ref[0]
        lane = lax.broadcasted_iota(jnp.int32, q.shape, 1)
        zero = jnp.zeros_like(q)
        q2_s[0:tq, :] = jnp.where(lane < A_HALF, q, zero)
        q2_s[tq:2 * tq, :] = jnp.where(lane >= A_HALF, q, zero)
        p, ls = weights(kc_ref[0])
        l_s[...] = ls
        acc_s[...] = jnp.dot(vct_ref[0, 0], p, preferred_element_type=F32)

    ls = l_s[...]
    for ci in range(tk // kc):
        p, lc = weights(k_ref[0, ci * kc:(ci + 1) * kc, :])
        p_s[ci * kc:(ci + 1) * kc, :] = p
        ls = ls + lc
    l_s[...] = ls
    acc_s[...] += jnp.dot(vt_ref[0, 0], p_s[...], preferred_element_type=F32)

    @pl.when(kv == nkv - 1)
    def _():
        l = jnp.sum(l_s[...], axis=0, keepdims=True)
        o = acc_s[...] / l
        d = o[:, 0:tq] - lam_ref[0, 0] * o[:, tq:2 * tq]
        ms = jnp.mean(d * d, axis=0, keepdims=True)
        y = d * lax.rsqrt(ms + EPS) * g_ref[...] * out_scale
        o_ref[0] = y.T.astype(o_ref.dtype)


def _diffattn_fast(lam, p, pc, sub_g, *, heads, l, c, out_scale):
    b = p.shape[0]
    tq = min(256, l)
    tk = min(2048, l)
    kc = min(512, tk)
    nkv = l // tk
    k_blk = 2 * heads
    hd = heads * HEAD_DIM
    vt = jnp.transpose(p[:, :, 3 * hd:4 * hd].reshape(b, l, heads, HEAD_DIM), (0, 2, 3, 1))
    vct = jnp.transpose(pc[:, :, hd:2 * hd].reshape(b, c, heads, HEAD_DIM), (0, 2, 3, 1))
    kern = functools.partial(_diffattn_fast_kernel, tq=tq, tk=tk, kc=kc, nkv=nkv, out_scale=out_scale)
    return pl.pallas_call(
        kern,
        grid=(b, heads, l // tq, nkv),
        in_specs=[pl.BlockSpec(memory_space=pltpu.SMEM),
                  pl.BlockSpec((1, tq, HEAD_DIM), lambda bi, h, i, kv: (bi, i, h)),
                  pl.BlockSpec((1, c, HEAD_DIM), lambda bi, h, i, kv: (bi, 0, h)),
                  pl.BlockSpec((1, 1, HEAD_DIM, c), lambda bi, h, i, kv: (bi, h, 0, 0)),
                  pl.BlockSpec((1, tk, HEAD_DIM), lambda bi, h, i, kv: (bi, kv, k_blk + h)),
                  pl.BlockSpec((1, 1, HEAD_DIM, tk), lambda bi, h, i, kv: (bi, h, 0, kv)),
                  pl.BlockSpec((HEAD_DIM, 1), lambda bi, h, i, kv: (0, 0))],
        out_specs=pl.BlockSpec((1, tq, HEAD_DIM), lambda bi, h, i, kv: (bi, i, h)),
        out_shape=jax.ShapeDtypeStruct((b, l, hd), BF16),
        scratch_shapes=[pltpu.VMEM((2 * tq, HEAD_DIM), BF16),
                        pltpu.VMEM((8, 2 * tq), F32),
                        pltpu.VMEM((HEAD_DIM, 2 * tq), F32),
                        pltpu.VMEM((tk, 2 * tq), BF16)],
        compiler_params=_params("parallel", "parallel", "parallel", "arbitrary"),
        name="diff_attention_bounded",
    )(lam, p, pc, vct, p, vt, sub_g.reshape(HEAD_DIM, 1))


SCORE_BOUND_LOG2 = 60.0


def _diffattn(lam, p, pc, sub_g, score_bound, *, heads, l, c, out_scale):
    kw = dict(heads=heads, l=l, c=c, out_scale=out_scale)
    return lax.cond(score_bound < SCORE_BOUND_LOG2,
                    lambda *a: _diffattn_fast(*a, **kw), lambda *a: _diffattn_safe(*a, **kw),
                    lam, p, pc, sub_g)


def _diffattn_safe(lam, p, pc, sub_g, *, heads, l, c, out_scale):
    b = p.shape[0]
    tq = min(256, l)
    tk = min(512, l)
    nkv = l // tk
    q_blk = 0
    k_blk = 2 * heads
    v_blk = 3 * heads
    kern = functools.partial(_diffattn_kernel, tq=tq, nkv=nkv, out_scale=out_scale)
    return pl.pallas_call(
        kern,
        grid=(b, heads, l // tq, nkv),
        in_specs=[pl.BlockSpec(memory_space=pltpu.SMEM),
                  pl.BlockSpec((1, tq, HEAD_DIM), lambda bi, h, i, kv: (bi, i, q_blk + h)),
                  pl.BlockSpec((1, c, HEAD_DIM), lambda bi, h, i, kv: (bi, 0, h)),
                  pl.BlockSpec((1, c, HEAD_DIM), lambda bi, h, i, kv: (bi, 0, heads + h)),
                  pl.BlockSpec((1, tk, HEAD_DIM), lambda bi, h, i, kv: (bi, kv, k_blk + h)),
                  pl.BlockSpec((1, tk, HEAD_DIM), lambda bi, h, i, kv: (bi, kv, v_blk + h)),
                  pl.BlockSpec((1, HEAD_DIM), lambda bi, h, i, kv: (0, 0))],
        out_specs=pl.BlockSpec((1, tq, HEAD_DIM), lambda bi, h, i, kv: (bi, i, h)),
        out_shape=jax.ShapeDtypeStruct((b, l, heads * HEAD_DIM), BF16),
        scratch_shapes=[pltpu.VMEM((2 * tq, HEAD_DIM), BF16),
                        pltpu.VMEM((2 * tq, 1), F32),
                        pltpu.VMEM((2 * tq, 1), F32),
                        pltpu.VMEM((2 * tq, HEAD_DIM), F32)],
        compiler_params=_params("parallel", "parallel", "parallel", "arbitrary"),
        name="diff_attention",
    )(lam, p, pc, pc, p, p, sub_g.reshape(1, HEAD_DIM))


def _winattn_kernel(sink_ref, q_ref, kp_ref, km_ref, kn_ref, vp_ref, vm_ref, vn_ref, kc_ref, vc_ref,
                    o_ref, *, tq, l, group):
    g = pl.program_id(1)
    i = pl.program_id(2)
    kband = jnp.concatenate([kp_ref[0], km_ref[0], kn_ref[0]], axis=0)
    vband = jnp.concatenate([vp_ref[0], vm_ref[0], vn_ref[0]], axis=0)
    nk = tq + 2 * WINDOW
    qpos = i * tq + lax.broadcasted_iota(jnp.int32, (tq, nk), 0)
    kpos = i * tq - WINDOW + lax.broadcasted_iota(jnp.int32, (tq, nk), 1)
    valid = (jnp.abs(kpos - qpos) <= WINDOW) & (kpos >= 0) & (kpos < l)
    kc = kc_ref[0]
    vc = vc_ref[0]
    dn = (((1,), (1,)), ((), ()))
    for r in range(group):
        sl = slice(r * HEAD_DIM, (r + 1) * HEAD_DIM)
        q = q_ref[0, :, sl]
        s_lat = jnp.where(valid, lax.dot_general(q, kband, dn, preferred_element_type=F32), NEG)
        s_ctx = lax.dot_general(q, kc, dn, preferred_element_type=F32)
        sk = sink_ref[g, r]
        m = jnp.maximum(jnp.maximum(jnp.max(s_lat, axis=-1, keepdims=True),
                                    jnp.max(s_ctx, axis=-1, keepdims=True)), sk)
        p_lat = jnp.exp(s_lat - m)
        p_ctx = jnp.exp(s_ctx - m)
        denom = (jnp.sum(p_lat, axis=-1, keepdims=True) + jnp.sum(p_ctx, axis=-1, keepdims=True)
                 + jnp.exp(sk - m))
        o = (jnp.dot(p_lat.astype(BF16), vband, preferred_element_type=F32)
             + jnp.dot(p_ctx.astype(BF16), vc, preferred_element_type=F32))
        o_ref[0, :, sl] = (o / denom).astype(o_ref.dtype)


def _winattn(sink, p, pc, *, a_heads, b_heads, l, c):
    b = p.shape[0]
    group = b_heads // B_KV_HEADS
    tq = min(256, l)
    wpb = tq // WINDOW
    nwb = l // WINDOW
    gw = group * HEAD_DIM
    q_blk = a_heads * HEAD_DIM // gw
    k_blk = (a_heads + b_heads) + 2 * a_heads
    v_blk = k_blk + B_KV_HEADS
    kc_blk = 2 * a_heads
    vc_blk = kc_blk + B_KV_HEADS
    kern = functools.partial(_winattn_kernel, tq=tq, l=l, group=group)

    def prev_map(col):
        return lambda bi, g, i: (bi, jnp.maximum(i * wpb - 1, 0), col + g)

    def main_map(col):
        return lambda bi, g, i: (bi, i, col + g)

    def next_map(col):
        return lambda bi, g, i: (bi, jnp.minimum((i + 1) * wpb, nwb - 1), col + g)

    return pl.pallas_call(
        kern,
        grid=(b, B_KV_HEADS, l // tq),
        in_specs=[pl.BlockSpec(memory_space=pltpu.SMEM),
                  pl.BlockSpec((1, tq, gw), lambda bi, g, i: (bi, i, q_blk + g)),
                  pl.BlockSpec((1, WINDOW, HEAD_DIM), prev_map(k_blk)),
                  pl.BlockSpec((1, tq, HEAD_DIM), main_map(k_blk)),
                  pl.BlockSpec((1, WINDOW, HEAD_DIM), next_map(k_blk)),
                  pl.BlockSpec((1, WINDOW, HEAD_DIM), prev_map(v_blk)),
                  pl.BlockSpec((1, tq, HEAD_DIM), main_map(v_blk)),
                  pl.BlockSpec((1, WINDOW, HEAD_DIM), next_map(v_blk)),
                  pl.BlockSpec((1, c, HEAD_DIM), lambda bi, g, i: (bi, 0, kc_blk + g)),
                  pl.BlockSpec((1, c, HEAD_DIM), lambda bi, g, i: (bi, 0, vc_blk + g))],
        out_specs=pl.BlockSpec((1, tq, gw), lambda bi, g, i: (bi, i, g)),
        out_shape=jax.ShapeDtypeStruct((b, l, b_heads * HEAD_DIM), BF16),
        compiler_params=_params("parallel", "parallel", "parallel"),
        name="window_attention",
    )(sink, p, p, p, p, p, p, p, pc, pc)


def _mm_resid_kernel(*refs, nparts):
    a_refs = refs[0:nparts]
    w_refs = refs[nparts:2 * nparts]
    res_ref, gate_ref, o_ref = refs[2 * nparts:]
    y = jnp.dot(a_refs[0][0].astype(BF16), w_refs[0][...], preferred_element_type=F32)
    for a_ref, w_ref in zip(a_refs[1:], w_refs[1:]):
        y = y + jnp.dot(a_ref[0].astype(BF16), w_ref[...], preferred_element_type=F32)
    o_ref[0] = res_ref[0] + gate_ref[0] * y


def _mm_resid(parts, ws, res, gate):
    b, l, n = res.shape
    tm = min(512, l)
    tn = min(512, n)
    nparts = len(parts)
    in_specs = ([pl.BlockSpec((1, tm, a.shape[2]), lambda bi, i, j: (bi, i, 0)) for a in parts]
                + [pl.BlockSpec((w.shape[0], tn), lambda bi, i, j: (0, j)) for w in ws]
                + [pl.BlockSpec((1, tm, tn), lambda bi, i, j: (bi, i, j)),
                   pl.BlockSpec((1, 1, tn), lambda bi, i, j: (bi, 0, j))])
    return pl.pallas_call(
        functools.partial(_mm_resid_kernel, nparts=nparts),
        grid=(b, l // tm, n // tn),
        in_specs=in_specs,
        out_specs=pl.BlockSpec((1, tm, tn), lambda bi, i, j: (bi, i, j)),
        out_shape=jax.ShapeDtypeStruct((b, l, n), F32),
        compiler_params=_params("parallel", "parallel", "arbitrary"),
        name="out_projection_gated_residual",
    )(*parts, *ws, res, gate)


HALO = BF16_SUBLANES


def _fill_rows(hbuf, hp_ref, hm_ref, hn_ref, tm):
    hbuf[0:HALO, :] = hp_ref[0]
    hbuf[HALO:HALO + tm, :] = hm_ref[0]
    hbuf[HALO + tm:2 * HALO + tm, :] = hn_ref[0]


def _conv3(zbuf, cw_ref, cb_ref, tm, first, last):
    @pl.when(first)
    def _():
        zbuf[HALO - 1:HALO, :] = jnp.zeros((1, zbuf.shape[1]), F32)

    @pl.when(last)
    def _():
        zbuf[HALO + tm:HALO + tm + 1, :] = jnp.zeros((1, zbuf.shape[1]), F32)

    return (zbuf[HALO - 1:HALO - 1 + tm, :] * cw_ref[0:1, :] + zbuf[HALO:HALO + tm, :] * cw_ref[1:2, :]
            + zbuf[HALO + 1:HALO + 1 + tm, :] * cw_ref[2:3, :] + cb_ref[...])


def _gelu_tanh(x):
    return 0.5 * x * (1.0 + jnp.tanh(math.sqrt(2.0 / math.pi) * (x + 0.044715 * (x * x * x))))


def _ffn_up_kernel(hp_ref, hm_ref, hn_ref, wg_ref, wv_ref, cw_ref, cb_ref, o_ref, hbuf, zbuf, *, tm, nt):
    i = pl.program_id(1)
    j = pl.program_id(2)

    @pl.when(j == 0)
    def _():
        _fill_rows(hbuf, hp_ref, hm_ref, hn_ref, tm)

    zbuf[...] = jnp.dot(hbuf[...], wg_ref[...], preferred_element_type=F32)
    g = _conv3(zbuf, cw_ref, cb_ref, tm, i == 0, i == nt - 1)
    v = jnp.dot(hbuf[HALO:HALO + tm, :], wv_ref[...], preferred_element_type=F32)
    o_ref[0] = (_gelu_tanh(g) * v).astype(o_ref.dtype)


def _hy_in_kernel(hp_ref, hm_ref, hn_ref, w_ref, b_ref, cw_ref, cb_ref, o_ref, hbuf, zbuf, *, tm, nt):
    i = pl.program_id(1)
    j = pl.program_id(2)

    @pl.when(j == 0)
    def _():
        _fill_rows(hbuf, hp_ref, hm_ref, hn_ref, tm)

    zbuf[...] = jnp.dot(hbuf[...], w_ref[...], preferred_element_type=F32) + b_ref[...]
    o_ref[0] = _conv3(zbuf, cw_ref, cb_ref, tm, i == 0, i == nt - 1).astype(o_ref.dtype)


def _mm_conv(kernel, h, ws, vecs, n, out_dtype, name):
    b, l, k = h.shape
    tm = min(512, l)
    tn = min(512, n)
    nt = l // tm
    hpb = tm // HALO
    nhb = l // HALO
    in_specs = ([pl.BlockSpec((1, HALO, k), lambda bi, i, j: (bi, jnp.maximum(i * hpb - 1, 0), 0)),
                 pl.BlockSpec((1, tm, k), lambda bi, i, j: (bi, i, 0)),
                 pl.BlockSpec((1, HALO, k), lambda bi, i, j: (bi, jnp.minimum((i + 1) * hpb, nhb - 1), 0))]
                + [pl.BlockSpec((k, tn), lambda bi, i, j: (0, j)) for _ in ws]
                + [pl.BlockSpec((v.shape[0], tn), lambda bi, i, j: (0, j)) for v in vecs])
    return pl.pallas_call(
        functools.partial(kernel, tm=tm, nt=nt),
        grid=(b, nt, n // tn),
        in_specs=in_specs,
        out_specs=pl.BlockSpec((1, tm, tn), lambda bi, i, j: (bi, i, j)),
        out_shape=jax.ShapeDtypeStruct((b, l, n), out_dtype),
        scratch_shapes=[pltpu.VMEM((tm + 2 * HALO, k), BF16),
                        pltpu.VMEM((tm + 2 * HALO, tn), F32)],
        compiler_params=_params("parallel", "parallel", "arbitrary"),
        name=name,
    )(h, h, h, *ws, *vecs)


def _filter_kernel(z_ref, w0_ref, b0_ref, w1_ref, b1_ref, w2_ref, b2_ref, fr_ref, w3_ref, dl_ref,
                   o_ref, s_ref, *, tt):
    it = pl.program_id(1)
    z = z_ref[...]
    fr = fr_ref[...]

    def layer(a, w_ref, b_ref):
        return jnp.sin(fr * (jnp.dot(a, w_ref[...], preferred_element_type=F32, precision=HIGHEST)
                             + b_ref[...]))

    a = layer(layer(layer(z, w0_ref, b0_ref), w1_ref, b1_ref), w2_ref, b2_ref)
    decay = jnp.exp(-z[:, 0:1] * dl_ref[...])
    row = it * tt + lax.broadcasted_iota(jnp.int32, decay.shape, 0)

    @pl.when(it == 0)
    def _():
        s_ref[...] = jnp.zeros(s_ref.shape, F32)

    for s in range(2 * HY_ORDER):
        h = jnp.dot(a, w3_ref[s], preferred_element_type=F32, precision=HIGHEST) * decay
        if s % 2 == 1:
            h = jnp.where(row == 0, 0.0, h)
        o_ref[s] = h.astype(o_ref.dtype)
        s_ref[s] += jnp.sum(jnp.abs(h), axis=0, keepdims=True)


def _hyena_filters(l, d, f_w0, f_b0, f_w1, f_b1, f_w2, f_b2, f_freq, f_w3):
    emb, fw = f_w0.shape
    bands = (emb - 1) // 2
    t = np.linspace(0.0, 1.0, l)[:, None]
    w = 2.0 * math.pi * np.arange(l)[:, None] / l
    f = np.linspace(1e-4, bands - 1, bands)[None, :]
    emb_pad = -(-emb // 8) * 8
    z = np.concatenate([t, np.cos(f * w), -np.sin(f * w), np.zeros((l, emb_pad - emb))], axis=-1)
    z = jnp.asarray(z, F32)
    w0 = jnp.concatenate([f_w0, jnp.zeros((emb_pad - emb, fw), F32)], axis=0)
    max_decay = math.log(HY_DECAY_TARGET) / HY_MAX_DECAY_PCT
    min_decay = math.log(HY_DECAY_TARGET) / HY_MIN_DECAY_PCT
    dl = jnp.asarray(np.abs(np.linspace(min_decay, max_decay, d))[None, :], F32)
    nseg = 2 * HY_ORDER
    w3 = jnp.transpose(f_w3.reshape(fw, nseg, d), (1, 0, 2))
    tt = min(256, l)
    td = min(512, d)
    vec = lambda a: a.reshape(1, fw)
    small = lambda shape: pl.BlockSpec(shape, lambda jd, it: (0,) * len(shape))
    return pl.pallas_call(
        functools.partial(_filter_kernel, tt=tt),
        grid=(d // td, l // tt),
        in_specs=[pl.BlockSpec((tt, emb_pad), lambda jd, it: (it, 0)),
                  small((emb_pad, fw)), small((1, fw)), small((fw, fw)), small((1, fw)),
                  small((fw, fw)), small((1, fw)), small((1, fw)),
                  pl.BlockSpec((nseg, fw, td), lambda jd, it: (0, 0, jd)),
                  pl.BlockSpec((1, td), lambda jd, it: (0, jd))],
        out_specs=[pl.BlockSpec((nseg, tt, td), lambda jd, it: (0, it, jd)),
                   pl.BlockSpec((nseg, 1, td), lambda jd, it: (0, 0, jd))],
        out_shape=[jax.ShapeDtypeStruct((nseg, l, d), F32),
                   jax.ShapeDtypeStruct((nseg, 1, d), F32)],
        compiler_params=_params("parallel", "arbitrary"),
        name="hyena_filters",
    )(z, w0, vec(f_b0), f_w1, vec(f_b1), f_w2, vec(f_b2), vec(f_freq), w3, dl)


def _dft_tables(l):
    n1 = FFT_N1
    n = 2 * l
    n2 = n // n1
    k1 = np.arange(n1)[:, None]
    f1 = np.exp(-2j * np.pi * k1 * np.arange(n1 // 2)[None, :] / n1)
    tw = np.exp(-2j * np.pi * np.arange(n2)[:, None] * np.arange(n1)[None, :] / n)
    f1r, f1i = jnp.asarray(f1.real, F32), jnp.asarray(f1.imag, F32)
    twr, twi = jnp.asarray(tw.real, F32)[:, :, None], jnp.asarray(tw.imag, F32)[:, :, None]
    fr = twr * f1r - twi * f1i
    fi = twr * f1i + twi * f1r
    ta = jnp.concatenate([jnp.concatenate([fr, -fi], axis=2),
                          jnp.concatenate([fi, fr], axis=2)], axis=1)
    fb = np.exp(-2j * np.pi * np.arange(n2)[:, None] * np.arange(n2)[None, :] / n2)
    tb = np.block([[fb.real, -fb.imag], [fb.imag, fb.real]])
    return (ta.astype(BF16), jnp.swapaxes(ta, 1, 2).astype(BF16),
            jnp.asarray(tb, BF16), jnp.asarray(tb.T, BF16))


FFT_NB = 8


def _fft_a_kernel(x_ref, t_ref, o_ref):
    planes = x_ref.shape[1]
    for jj in range(FFT_NB):
        x = jnp.concatenate([x_ref[0, c, :, jj, :] for c in range(planes)], axis=0).astype(BF16)
        o_ref[0, :, jj, :] = jnp.dot(t_ref[jj], x, preferred_element_type=F32)


def _fft_a(x5, ta, *, n2, d, seg):
    s, p, half, _, _ = x5.shape
    kdim = p * half
    tn = min(512, d)
    sb, db = (seg * d) // tn, d // tn
    return pl.pallas_call(
        _fft_a_kernel,
        grid=(s, n2 // FFT_NB, db),
        in_specs=[pl.BlockSpec((1, p, half, FFT_NB, tn), lambda si, m, j: (si, 0, 0, m, sb + j)),
                  pl.BlockSpec((FFT_NB, 2 * FFT_N1, kdim), lambda si, m, j: (m, 0, 0))],
        out_specs=pl.BlockSpec((1, 2 * FFT_N1, FFT_NB, tn), lambda si, m, j: (si, 0, m, j)),
        out_shape=jax.ShapeDtypeStruct((s, 2 * FFT_N1, n2, d), F32),
        compiler_params=_params("parallel", "parallel", "parallel"),
        name="dft_stage_a",
    )(x5, ta)


def _fft_bfilt_kernel(af_ref, ab_ref, tb_ref, sc_ref, o_ref, *, n2):
    tb = tb_ref[...]
    hf = jnp.dot(tb, af_ref[0].reshape(2 * n2, -1).astype(BF16), preferred_element_type=F32)
    hb = jnp.dot(tb, ab_ref[0].reshape(2 * n2, -1).astype(BF16), preferred_element_type=F32)
    sc = sc_ref[0]
    o_ref[0, 0, 0] = (hf[0:n2] + hb[0:n2]) * sc
    o_ref[0, 1, 0] = (hf[n2:2 * n2] - hb[n2:2 * n2]) * sc


def _fft_bfilt(af, tb, scale, *, n2, d):
    tn = min(2048, d)
    blk = (1, 2, 1, n2, tn)
    return pl.pallas_call(
        functools.partial(_fft_bfilt_kernel, n2=n2),
        grid=(HY_ORDER, FFT_N1, d // tn),
        in_specs=[pl.BlockSpec(blk, lambda o, k, j: (2 * o, 0, k, 0, j)),
                  pl.BlockSpec(blk, lambda o, k, j: (2 * o + 1, 0, k, 0, j)),
                  pl.BlockSpec((2 * n2, 2 * n2), lambda o, k, j: (0, 0)),
                  pl.BlockSpec((1, 1, tn), lambda o, k, j: (o, 0, j))],
        out_specs=pl.BlockSpec(blk, lambda o, k, j: (o, 0, k, 0, j)),
        out_shape=jax.ShapeDtypeStruct((HY_ORDER, 2, FFT_N1, n2, d), F32),
        compiler_params=_params("parallel", "parallel", "parallel"),
        name="dft_stage_b_filters",
    )(af, af, tb, scale)


def _fft_b_kernel(a_ref, g_ref, tb_ref, tbt_ref, o_ref, *, n2):
    z = jnp.dot(tb_ref[...], a_ref[...].reshape(2 * n2, -1).astype(BF16), preferred_element_type=F32)
    zr, zi = z[0:n2], z[n2:2 * n2]
    gr, gi = g_ref[0, 0, 0], g_ref[0, 1, 0]
    prod = jnp.concatenate([zr * gr - zi * gi, zr * gi + zi * gr], axis=0).astype(BF16)
    y = jnp.dot(tbt_ref[...], prod, preferred_element_type=F32)
    o_ref[...] = y.reshape(o_ref.shape).astype(o_ref.dtype)


def _fft_b(a, g, tb, tbt, order, *, n2, d):
    tn = min(2048, d)
    blk = (2, 1, n2, tn)
    return pl.pallas_call(
        functools.partial(_fft_b_kernel, n2=n2),
        grid=(FFT_N1, d // tn),
        in_specs=[pl.BlockSpec(blk, lambda k, j: (0, k, 0, j)),
                  pl.BlockSpec((1,) + blk, lambda k, j: (order, 0, k, 0, j)),
                  pl.BlockSpec((2 * n2, 2 * n2), lambda k, j: (0, 0)),
                  pl.BlockSpec((2 * n2, 2 * n2), lambda k, j: (0, 0))],
        out_specs=pl.BlockSpec(blk, lambda k, j: (0, k, 0, j)),
        out_shape=jax.ShapeDtypeStruct((2, FFT_N1, n2, d), F32),
        compiler_params=_params("parallel", "parallel"),
        name="dft_stage_b_spectrum_product",
    )(a, g, tb, tbt)


def _fft_ainv_kernel(y_ref, t_ref, u_ref, gate_ref, d_ref, o_ref):
    half = o_ref.shape[1]
    for jj in range(FFT_NB):
        x = jnp.dot(t_ref[jj], y_ref[:, jj, :].astype(BF16), preferred_element_type=F32)
        for c in range(2):
            conv = x[c * half:(c + 1) * half]
            o_ref[c, :, jj, :] = gate_ref[c, :, jj, :] * (conv + d_ref[...] * u_ref[c, :, jj, :])


def _fft_ainv(y3, tat, u4, gate4, dskip, *, n2, d, useg, gseg):
    b, half, _, _ = u4.shape
    tn = min(256, d)
    db = d // tn
    usb, gsb = (useg * d) // tn, (gseg * d) // tn
    return pl.pallas_call(
        _fft_ainv_kernel,
        grid=(n2 // FFT_NB, db),
        in_specs=[pl.BlockSpec((2 * FFT_N1, FFT_NB, tn), lambda m, j: (0, m, j)),
                  pl.BlockSpec((FFT_NB, FFT_N1, 2 * FFT_N1), lambda m, j: (m, 0, 0)),
                  pl.BlockSpec((b, half, FFT_NB, tn), lambda m, j: (0, 0, m, usb + j)),
                  pl.BlockSpec((b, half, FFT_NB, tn), lambda m, j: (0, 0, m, gsb + j)),
                  pl.BlockSpec((1, tn), lambda m, j: (0, j))],
        out_specs=pl.BlockSpec((b, half, FFT_NB, tn), lambda m, j: (0, 0, m, j)),
        out_shape=jax.ShapeDtypeStruct((b, half, n2, d), F32),
        compiler_params=_params("parallel", "parallel"),
        name="dft_stage_a_inverse_gate",
    )(y3, tat, u4, gate4, dskip)


def _hyena_mixer(h, hy, l, d):
    (w_in, b_in, sconv_w, sconv_b, f_w0, f_b0, f_w1, f_b1, f_w2, f_b2, f_freq, f_w3, d_skip, w_out) = hy
    b = h.shape[0]
    assert b == 2, "the two batch rows ride the real / imaginary planes of one complex DFT"
    n1 = FFT_N1
    half = n1 // 2
    n2 = (2 * l) // n1
    assert half * n2 == l
    z3 = _mm_conv(_hy_in_kernel, h, [w_in.astype(BF16)],
                  [b_in.reshape(1, -1), sconv_w, sconv_b.reshape(1, -1)], 3 * d, F32, "hyena_in_proj_conv")
    filt, fsum = _hyena_filters(l, d, f_w0, f_b0, f_w1, f_b1, f_w2, f_b2, f_freq, f_w3)
    ta, tat, tb, tbt = _dft_tables(l)
    nseg = 2 * HY_ORDER
    af = _fft_a(filt.reshape(nseg, 1, half, n2, d), ta, n2=n2, d=d, seg=0)
    norm = (fsum[0::2] + fsum[1::2])
    g = _fft_bfilt(af.reshape(nseg, 2, n1, n2, d), tb, 1.0 / (norm * (2 * l)), n2=n2, d=d)
    z4 = z3.reshape(b, half, n2, 3 * d)
    y, yseg = z4, 0
    for o in range(HY_ORDER):
        a = _fft_a(y[None], ta, n2=n2, d=d, seg=yseg)
        yb = _fft_b(a.reshape(2, n1, n2, d), g, tb, tbt, o, n2=n2, d=d)
        y = _fft_ainv(yb.reshape(2 * n1, n2, d), tat, y, z4, d_skip[o].reshape(1, d),
                      n2=n2, d=d, useg=yseg, gseg=o + 1)
        yseg = 0
    return y.reshape(b, l, d), w_out.astype(BF16)


def _rope_tables(l):
    t = jnp.arange(l)
    row = (t // GRID_W).astype(F32)[:, None]
    col = (t % GRID_W).astype(F32)[:, None]
    cos, sin = [], []
    for dim in (A_HALF, HEAD_DIM):
        nf = dim // 4
        inv = jnp.asarray(ROPE_BASE ** (-np.arange(nf) / nf), F32)[None, :]
        ar, ac = row * inv, col * inv
        c = jnp.concatenate([jnp.cos(ar), jnp.cos(ar), jnp.cos(ac), jnp.cos(ac)], axis=1)
        s = jnp.concatenate([-jnp.sin(ar), jnp.sin(ar), -jnp.sin(ac), jnp.sin(ac)], axis=1)
        reps = HEAD_DIM // dim
        cos.append(jnp.tile(c, (1, reps)))
        sin.append(jnp.tile(s, (1, reps)))
    return jnp.stack(cos), jnp.stack(sin)


def _attn_mixer(h, hc, lam_init, w_in, w_out, a_q_g, a_k_g, lq1, lk1, lq2, lk2, a_sub_g, b_q_g, b_k_g, b_sink):
    b, l, d = h.shape
    c = hc.shape[1]
    a_heads = d // (2 * HEAD_DIM)
    b_heads = d // (2 * HEAD_DIM)
    q_cols = (a_heads + b_heads) * HEAD_DIM
    in_cols = w_in.shape[1]
    lam = (jnp.exp(jnp.sum(lq1 * lk1)) - jnp.exp(jnp.sum(lq2 * lk2)) + lam_init).reshape(1, 1)
    sink = b_sink.reshape(B_KV_HEADS, b_heads // B_KV_HEADS)

    ones = lambda n: jnp.ones((n,), F32)
    tile = lambda v, n: jnp.tile(v, n)
    gain = jnp.concatenate([tile(a_q_g, 2 * a_heads), tile(b_q_g, b_heads), tile(a_k_g, 2 * a_heads),
                            ones(a_heads * HEAD_DIM), tile(b_k_g, B_KV_HEADS),
                            ones(B_KV_HEADS * HEAD_DIM)]).reshape(1, in_cols)
    scale = jnp.concatenate([jnp.full((a_heads * HEAD_DIM,), A_HALF ** -0.5 * math.log2(math.e), F32),
                             jnp.full((b_heads * HEAD_DIM,), HEAD_DIM ** -0.5, F32),
                             ones(in_cols - q_cols)]).reshape(1, in_cols)
    e = [0, a_heads // 2, (a_heads + b_heads) // 2, (2 * a_heads + b_heads) // 2,
         (3 * a_heads + b_heads) // 2, (3 * a_heads + b_heads) // 2 + 1, (3 * a_heads + b_heads) // 2 + 2]
    chunks = [A_HALF, HEAD_DIM, A_HALF, 0, HEAD_DIM, 0]
    modes = tuple((e[s], e[s + 1], chunks[s]) for s in range(6))
    cos_tab, sin_tab = _rope_tables(l)

    def type_of(j0):
        def f(j):
            jj = j + j0
            is_a = (jj < e[1]) | ((jj >= e[2]) & (jj < e[3]))
            return jnp.where(is_a, 0, 1)
        return f

    w_bf = w_in.astype(BF16)
    p = _inproj(h.reshape(b * l, d), w_bf, gain, scale, cos_tab, sin_tab, modes=modes,
                type_of_block=type_of(0), col0=0, ncols=in_cols, l=l, rope=True).reshape(b, l, in_cols)
    kv0 = e[2]
    modes_c = tuple((lo - kv0, hi - kv0, ch) for lo, hi, ch in modes[2:])
    pc = _inproj(hc.reshape(b * c, d), w_bf, gain, scale, cos_tab, sin_tab, modes=modes_c,
                 type_of_block=type_of(kv0), col0=kv0, ncols=in_cols - q_cols, l=c, rope=False
                 ).reshape(b, c, in_cols - q_cols)
    score_bound = (A_HALF ** 0.5 * math.log2(math.e)) * jnp.max(jnp.abs(a_q_g)) * jnp.max(jnp.abs(a_k_g))
    oa = _diffattn(lam, p, pc, a_sub_g, score_bound, heads=a_heads, l=l, c=c, out_scale=1.0 - lam_init)
    ob = _winattn(sink, p, pc, a_heads=a_heads, b_heads=b_heads, l=l, c=c)
    w_out_bf = w_out.astype(BF16)
    na = a_heads * HEAD_DIM
    return [oa, ob], [w_out_bf[:na], w_out_bf[na:]]


def kernel(x, c, ctx, c_ctx, ada_w, ada_b, norm1_g, norm2_g, attn_w_in, attn_w_out, a_q_g, a_k_g, a_lam_q1, a_lam_k1, a_lam_q2, a_lam_k2, a_sub_g, b_q_g, b_k_g, b_sink, hy_w_in, hy_b_in, hy_sconv_w, hy_sconv_b, hy_f_w0, hy_f_b0, hy_f_w1, hy_f_b1, hy_f_w2, hy_f_b2, hy_f_freq, hy_f_w3, hy_d, hy_w_out, ffn_w_gate, ffn_w_val, ffn_conv_w, ffn_conv_b, ffn_w_down):
    b, l, d = x.shape
    depth = ada_w.shape[0]
    cc = jnp.concatenate([c, c_ctx[None, :], jnp.zeros((8 - b - 1, d), F32)], axis=0)
    m = _ada(cc, ada_w, ada_b)
    xs = x
    for layer in range(depth):
        i = layer // 2
        lat = [m[layer, :b, k * d:(k + 1) * d].reshape(b, 1, d) for k in range(6)]
        sh1, sc1, g1, sh2, sc2, g2 = lat
        h = _normmod(xs, norm1_g[layer], sc1, sh1)
        if layer % 2 == 0:
            mc = [jnp.broadcast_to(m[layer, b, k * d:(k + 1) * d].reshape(1, 1, d), (b, 1, d)) for k in range(2)]
            hc = _normmod(ctx, norm1_g[layer], mc[1], mc[0])
            parts, ws = _attn_mixer(h, hc, 0.8 - 0.6 * math.exp(-0.3 * layer), attn_w_in[i], attn_w_out[i],
                                    a_q_g[i], a_k_g[i], a_lam_q1[i], a_lam_k1[i], a_lam_q2[i], a_lam_k2[i],
                                    a_sub_g[i], b_q_g[i], b_k_g[i], b_sink[i])
        else:
            hy = (hy_w_in[i], hy_b_in[i], hy_sconv_w[i], hy_sconv_b[i], hy_f_w0[i], hy_f_b0[i], hy_f_w1[i],
                  hy_f_b1[i], hy_f_w2[i], hy_f_b2[i], hy_f_freq[i], hy_f_w3[i], hy_d[i], hy_w_out[i])
            y, w_o = _hyena_mixer(h, hy, l, d)
            parts, ws = [y], [w_o]
        xs = _mm_resid(parts, ws, xs, g1)
        h2 = _normmod(xs, norm2_g[layer], sc2, sh2)
        d_ff = ffn_w_gate.shape[2]
        hid = _mm_conv(_ffn_up_kernel, h2, [ffn_w_gate[layer].astype(BF16), ffn_w_val[layer].astype(BF16)],
                       [ffn_conv_w[layer], ffn_conv_b[layer].reshape(1, -1)], d_ff, BF16, "ffn_up_conv_glu")
        xs = _mm_resid([hid], [ffn_w_down[layer].astype(BF16)], xs, g2)
    return xs
```

```python
import functools
import math

import numpy as np
import jax
import jax.numpy as jnp
from jax import lax
from jax.experimental import pallas as pl
from jax.experimental.pallas import tpu as pltpu

F32 = jnp.float32
BF16 = jnp.bfloat16
HIGHEST = lax.Precision.HIGHEST

HEAD_DIM = 128
A_HALF = HEAD_DIM // 2
B_KV_HEADS = 2
GRID_W = 64
WINDOW = 128
ROPE_BASE = 10000.0
EPS = 1e-6
NEG = -1e30
HY_ORDER = 2
HY_DECAY_TARGET = 1e-2
HY_MAX_DECAY_PCT = 0.3
HY_MIN_DECAY_PCT = 1.5

LANES = 128
BF16_SUBLANES = 16
FFT_N1 = 256
VMEM_LIMIT_BYTES = 56 * 1024 * 1024


def _params(*sem):
    return pltpu.CompilerParams(dimension_semantics=sem, vmem_limit_bytes=VMEM_LIMIT_BYTES)


def _ada_kernel(c_ref, w_ref, b_ref, o_ref):
    c = c_ref[...]
    s = c * (1.0 / (1.0 + jnp.exp(-c)))
    o_ref[0] = jnp.dot(s, w_ref[0], preferred_element_type=F32, precision=HIGHEST) + b_ref[0]


def _ada(cc, ada_w, ada_b):
    depth, d, n = ada_w.shape
    tn = 1024
    return pl.pallas_call(
        _ada_kernel,
        grid=(depth, n // tn),
        in_specs=[pl.BlockSpec((8, d), lambda l, j: (0, 0)),
                  pl.BlockSpec((1, d, tn), lambda l, j: (l, 0, j)),
                  pl.BlockSpec((1, 1, tn), lambda l, j: (l, 0, j))],
        out_specs=pl.BlockSpec((1, 8, tn), lambda l, j: (l, 0, j)),
        out_shape=jax.ShapeDtypeStruct((depth, 8, n), F32),
        compiler_params=_params("parallel", "parallel"),
        name="ada_modulation",
    )(cc, ada_w, ada_b.reshape(depth, 1, n))


def _normmod_kernel(x_ref, g_ref, sc_ref, sh_ref, o_ref):
    x = x_ref[0]
    ms = jnp.mean(x * x, axis=-1, keepdims=True)
    y = x * lax.rsqrt(ms + EPS) * g_ref[...]
    o_ref[0] = (y * (1.0 + sc_ref[0]) + sh_ref[0]).astype(o_ref.dtype)


def _normmod(x, g, sc, sh):
    b, l, d = x.shape
    tm = min(512, l)
    return pl.pallas_call(
        _normmod_kernel,
        grid=(b, l // tm),
        in_specs=[pl.BlockSpec((1, tm, d), lambda bi, i: (bi, i, 0)),
                  pl.BlockSpec((1, d), lambda bi, i: (0, 0)),
                  pl.BlockSpec((1, 1, d), lambda bi, i: (bi, 0, 0)),
                  pl.BlockSpec((1, 1, d), lambda bi, i: (bi, 0, 0))],
        out_specs=pl.BlockSpec((1, tm, d), lambda bi, i: (bi, i, 0)),
        out_shape=jax.ShapeDtypeStruct((b, l, d), BF16),
        compiler_params=_params("parallel", "parallel"),
        name="rmsnorm_modulate",
    )(x, g.reshape(1, d), sc, sh)


def _head_epilogue(z, gain, scale, cos, sin, chunk, rope):
    lane = lax.broadcasted_iota(jnp.int32, z.shape, 1)
    zz = z * z
    s_all = jnp.sum(zz, axis=-1, keepdims=True)
    if chunk == HEAD_DIM:
        ms = s_all * (1.0 / HEAD_DIM)
    else:
        s_lo = jnp.sum(jnp.where(lane < A_HALF, zz, 0.0), axis=-1, keepdims=True)
        ms = jnp.where(lane < A_HALF, s_lo, s_all - s_lo) * (1.0 / A_HALF)
    y = z * lax.rsqrt(ms + EPS) * gain
    if rope:
        nf = chunk // 4
        first = (lane % (2 * nf)) < nf
        partner = jnp.where(first, pltpu.roll(y, HEAD_DIM - nf, 1), pltpu.roll(y, nf, 1))
        y = y * cos + partner * sin
    return y * scale


def _inproj_kernel(h_ref, w_ref, gain_ref, scale_ref, cos_ref, sin_ref, o_ref, *, modes, rope):
    j = pl.program_id(1)
    z = jnp.dot(h_ref[...], w_ref[...], preferred_element_type=F32)
    for lo, hi, chunk in modes:
        @pl.when((j >= lo) & (j < hi))
        def _(chunk=chunk):
            for half in range(2):
                sl = slice(half * HEAD_DIM, (half + 1) * HEAD_DIM)
                zh = z[:, sl]
                if chunk:
                    zh = _head_epilogue(zh, gain_ref[:, sl], scale_ref[:, sl],
                                        cos_ref[0], sin_ref[0], chunk, rope)
                o_ref[:, sl] = zh.astype(o_ref.dtype)


def _inproj(h2d, w, gain, scale, cos_tab, sin_tab, *, modes, type_of_block, col0, ncols, l, rope):
    m, d = h2d.shape
    tn = 2 * HEAD_DIM
    tm = min(512, l)
    pos_blocks = l // tm
    kern = functools.partial(_inproj_kernel, modes=modes, rope=rope)
    return pl.pallas_call(
        kern,
        grid=(m // tm, ncols // tn),
        in_specs=[pl.BlockSpec((tm, d), lambda i, j: (i, 0)),
                  pl.BlockSpec((d, tn), lambda i, j: (0, j + col0)),
                  pl.BlockSpec((1, tn), lambda i, j: (0, j + col0)),
                  pl.BlockSpec((1, tn), lambda i, j: (0, j + col0)),
                  pl.BlockSpec((1, tm, HEAD_DIM), lambda i, j: (type_of_block(j), i % pos_blocks, 0)),
                  pl.BlockSpec((1, tm, HEAD_DIM), lambda i, j: (type_of_block(j), i % pos_blocks, 0))],
        out_specs=pl.BlockSpec((tm, tn), lambda i, j: (i, j)),
        out_shape=jax.ShapeDtypeStruct((m, ncols), BF16),
        compiler_params=_params("parallel", "arbitrary"),
        name="attn_in_projection",
    )(h2d, w, gain, scale, cos_tab, sin_tab)


def _diffattn_kernel(lam_ref, q_ref, kc_ref, vc_ref, k_ref, v_ref, g_ref, o_ref,
                     q2_s, m_s, l_s, acc_s, *, tq, nkv, out_scale):
    kv = pl.program_id(3)

    def process(k, v):
        s = lax.dot_general(q2_s[...], k, (((1,), (1,)), ((), ())), preferred_element_type=F32)
        m_prev = m_s[...]
        m_new = jnp.maximum(m_prev, jnp.max(s, axis=-1, keepdims=True))
        alpha = jnp.exp2(m_prev - m_new)
        p = jnp.exp2(s - m_new)
        l_s[...] = alpha * l_s[...] + jnp.sum(p, axis=-1, keepdims=True)
        acc_s[...] = alpha * acc_s[...] + jnp.dot(p.astype(BF16), v, preferred_element_type=F32)
        m_s[...] = m_new

    @pl.when(kv == 0)
    def _():
        q = q_ref[0]
        lane = lax.broadcasted_iota(jnp.int32, q.shape, 1)
        zero = jnp.zeros_like(q)
        q2_s[0:tq, :] = jnp.where(lane < A_HALF, q, zero)
        q2_s[tq:2 * tq, :] = jnp.where(lane >= A_HALF, q, zero)
        m_s[...] = jnp.full(m_s.shape, -jnp.inf, F32)
        l_s[...] = jnp.zeros(l_s.shape, F32)
        acc_s[...] = jnp.zeros(acc_s.shape, F32)
        process(kc_ref[0], vc_ref[0])

    process(k_ref[0], v_ref[0])

    @pl.when(kv == nkv - 1)
    def _():
        o = acc_s[...] / l_s[...]
        d = o[0:tq] - lam_ref[0, 0] * o[tq:2 * tq]
        ms = jnp.mean(d * d, axis=-1, keepdims=True)
        o_ref[0] = (d * lax.rsqrt(ms + EPS) * g_ref[...] * out_scale).astype(o_ref.dtype)


def _diffattn_fast_kernel(lam_ref, q_ref, kc_ref, vct_ref, k_ref, vt_ref, g_ref, o_ref,
                          q2_s, l_s, acc_s, p_s, *, tq, tk, kc, nkv, out_scale):
    kv = pl.program_id(3)
    dn = (((1,), (1,)), ((), ()))

    def weights(k):
        n = k.shape[0]
        p = jnp.exp2(lax.dot_general(k, q2_s[...], dn, preferred_element_type=F32))
        return p.astype(BF16), jnp.sum(p.reshape(n // 8, 8, 2 * tq), axis=0)

    @pl.when(kv == 0)
    def _():
        q = q_ref[0]
        lane = lax.broadcasted_iota(jnp.int32, q.shape, 1)
        zero = jnp.zeros_like(q)
        q2_s[0:tq, :] = jnp.where(lane < A_HALF, q, zero)
        q2_s[tq:2 * tq, :] = jnp.where(lane >= A_HALF, q, zero)
        p, ls = weights(kc_ref[0])
        l_s[...] = ls
        acc_s[...] = jnp.dot(vct_ref[0, 0], p, preferred_element_type=F32)

    ls = l_s[...]
    for ci in range(tk // kc):
        p, lc = weights(k_ref[0, ci * kc:(ci + 1) * kc, :])
        p_s[ci * kc:(ci + 1) * kc, :] = p
        ls = ls + lc
    l_s[...] = ls
    acc_s[...] += jnp.dot(vt_ref[0, 0], p_s[...], preferred_element_type=F32)

    @pl.when(kv == nkv - 1)
    def _():
        l = jnp.sum(l_s[...], axis=0, keepdims=True)
        o = acc_s[...] / l
        d = o[:, 0:tq] - lam_ref[0, 0] * o[:, tq:2 * tq]
        ms = jnp.mean(d * d, axis=0, keepdims=True)
        y = d * lax.rsqrt(ms + EPS) * g_ref[...] * out_scale
        o_ref[0] = y.T.astype(o_ref.dtype)


def _diffattn_fast(lam, p, pc, sub_g, *, heads, l, c, out_scale):
    b = p.shape[0]
    tq = min(512, l)
    tk = min(4096, l)
    kc = min(512, tk)
    nkv = l // tk
    k_blk = 2 * heads
    hd = heads * HEAD_DIM
    vt = jnp.transpose(p[:, :, 3 * hd:4 * hd].reshape(b, l, heads, HEAD_DIM), (0, 2, 3, 1))
    vct = jnp.transpose(pc[:, :, hd:2 * hd].reshape(b, c, heads, HEAD_DIM), (0, 2, 3, 1))
    kern = functools.partial(_diffattn_fast_kernel, tq=tq, tk=tk, kc=kc, nkv=nkv, out_scale=out_scale)
    return pl.pallas_call(
        kern,
        grid=(b, heads, l // tq, nkv),
        in_specs=[pl.BlockSpec(memory_space=pltpu.SMEM),
                  pl.BlockSpec((1, tq, HEAD_DIM), lambda bi, h, i, kv: (bi, i, h)),
                  pl.BlockSpec((1, c, HEAD_DIM), lambda bi, h, i, kv: (bi, 0, h)),
                  pl.BlockSpec((1, 1, HEAD_DIM, c), lambda bi, h, i, kv: (bi, h, 0, 0)),
                  pl.BlockSpec((1, tk, HEAD_DIM), lambda bi, h, i, kv: (bi, kv, k_blk + h)),
                  pl.BlockSpec((1, 1, HEAD_DIM, tk), lambda bi, h, i, kv: (bi, h, 0, kv)),
                  pl.BlockSpec((HEAD_DIM, 1), lambda bi, h, i, kv: (0, 0))],
        out_specs=pl.BlockSpec((1, tq, HEAD_DIM), lambda bi, h, i, kv: (bi, i, h)),
        out_shape=jax.ShapeDtypeStruct((b, l, hd), BF16),
        scratch_shapes=[pltpu.VMEM((2 * tq, HEAD_DIM), BF16),
                        pltpu.VMEM((8, 2 * tq), F32),
                        pltpu.VMEM((HEAD_DIM, 2 * tq), F32),
                        pltpu.VMEM((tk, 2 * tq), BF16)],
        compiler_params=_params("parallel", "parallel", "parallel", "arbitrary"),
        name="diff_attention_bounded",
    )(lam, p, pc, vct, p, vt, sub_g.reshape(HEAD_DIM, 1))


SCORE_BOUND_LOG2 = 60.0


def _diffattn(lam, p, pc, sub_g, score_bound, *, heads, l, c, out_scale):
    kw = dict(heads=heads, l=l, c=c, out_scale=out_scale)
    return lax.cond(score_bound < SCORE_BOUND_LOG2,
                    lambda *a: _diffattn_fast(*a, **kw), lambda *a: _diffattn_safe(*a, **kw),
                    lam, p, pc, sub_g)


def _diffattn_safe(lam, p, pc, sub_g, *, heads, l, c, out_scale):
    b = p.shape[0]
    tq = min(256, l)
    tk = min(512, l)
    nkv = l // tk
    q_blk = 0
    k_blk = 2 * heads
    v_blk = 3 * heads
    kern = functools.partial(_diffattn_kernel, tq=tq, nkv=nkv, out_scale=out_scale)
    return pl.pallas_call(
        kern,
        grid=(b, heads, l // tq, nkv),
        in_specs=[pl.BlockSpec(memory_space=pltpu.SMEM),
                  pl.BlockSpec((1, tq, HEAD_DIM), lambda bi, h, i, kv: (bi, i, q_blk + h)),
                  pl.BlockSpec((1, c, HEAD_DIM), lambda bi, h, i, kv: (bi, 0, h)),
                  pl.BlockSpec((1, c, HEAD_DIM), lambda bi, h, i, kv: (bi, 0, heads + h)),
                  pl.BlockSpec((1, tk, HEAD_DIM), lambda bi, h, i, kv: (bi, kv, k_blk + h)),
                  pl.BlockSpec((1, tk, HEAD_DIM), lambda bi, h, i, kv: (bi, kv, v_blk + h)),
                  pl.BlockSpec((1, HEAD_DIM), lambda bi, h, i, kv: (0, 0))],
        out_specs=pl.BlockSpec((1, tq, HEAD_DIM), lambda bi, h, i, kv: (bi, i, h)),
        out_shape=jax.ShapeDtypeStruct((b, l, heads * HEAD_DIM), BF16),
        scratch_shapes=[pltpu.VMEM((2 * tq, HEAD_DIM), BF16),
                        pltpu.VMEM((2 * tq, 1), F32),
                        pltpu.VMEM((2 * tq, 1), F32),
                        pltpu.VMEM((2 * tq, HEAD_DIM), F32)],
        compiler_params=_params("parallel", "parallel", "parallel", "arbitrary"),
        name="diff_attention",
    )(lam, p, pc, pc, p, p, sub_g.reshape(1, HEAD_DIM))


def _winattn_kernel(sink_ref, q_ref, kp_ref, km_ref, kn_ref, vp_ref, vm_ref, vn_ref, kc_ref, vc_ref,
                    o_ref, *, tq, l, group):
    g = pl.program_id(1)
    i = pl.program_id(2)
    kband = jnp.concatenate([kp_ref[0], km_ref[0], kn_ref[0]], axis=0)
    vband = jnp.concatenate([vp_ref[0], vm_ref[0], vn_ref[0]], axis=0)
    nk = tq + 2 * WINDOW
    qpos = i * tq + lax.broadcasted_iota(jnp.int32, (tq, nk), 0)
    kpos = i * tq - WINDOW + lax.broadcasted_iota(jnp.int32, (tq, nk), 1)
    valid = (jnp.abs(kpos - qpos) <= WINDOW) & (kpos >= 0) & (kpos < l)
    kc = kc_ref[0]
    vc = vc_ref[0]
    dn = (((1,), (1,)), ((), ()))
    for r in range(group):
        sl = slice(r * HEAD_DIM, (r + 1) * HEAD_DIM)
        q = q_ref[0, :, sl]
        s_lat = jnp.where(valid, lax.dot_general(q, kband, dn, preferred_element_type=F32), NEG)
        s_ctx = lax.dot_general(q, kc, dn, preferred_element_type=F32)
        sk = sink_ref[g, r]
        m = jnp.maximum(jnp.maximum(jnp.max(s_lat, axis=-1, keepdims=True),
                                    jnp.max(s_ctx, axis=-1, keepdims=True)), sk)
        p_lat = jnp.exp(s_lat - m)
        p_ctx = jnp.exp(s_ctx - m)
        denom = (jnp.sum(p_lat, axis=-1, keepdims=True) + jnp.sum(p_ctx, axis=-1, keepdims=True)
                 + jnp.exp(sk - m))
        o = (jnp.dot(p_lat.astype(BF16), vband, preferred_element_type=F32)
             + jnp.dot(p_ctx.astype(BF16), vc, preferred_element_type=F32))
        o_ref[0, :, sl] = (o / denom).astype(o_ref.dtype)


def _winattn(sink, p, pc, *, a_heads, b_heads, l, c):
    b = p.shape[0]
    group = b_heads // B_KV_HEADS
    tq = min(256, l)
    wpb = tq // WINDOW
    nwb = l // WINDOW
    gw = group * HEAD_DIM
    q_blk = a_heads * HEAD_DIM // gw
    k_blk = (a_heads + b_heads) + 2 * a_heads
    v_blk = k_blk + B_KV_HEADS
    kc_blk = 2 * a_heads
    vc_blk = kc_blk + B_KV_HEADS
    kern = functools.partial(_winattn_kernel, tq=tq, l=l, group=group)

    def prev_map(col):
        return lambda bi, g, i: (bi, jnp.maximum(i * wpb - 1, 0), col + g)

    def main_map(col):
        return lambda bi, g, i: (bi, i, col + g)

    def next_map(col):
        return lambda bi, g, i: (bi, jnp.minimum((i + 1) * wpb, nwb - 1), col + g)

    return pl.pallas_call(
        kern,
        grid=(b, B_KV_HEADS, l // tq),
        in_specs=[pl.BlockSpec(memory_space=pltpu.SMEM),
                  pl.BlockSpec((1, tq, gw), lambda bi, g, i: (bi, i, q_blk + g)),
                  pl.BlockSpec((1, WINDOW, HEAD_DIM), prev_map(k_blk)),
                  pl.BlockSpec((1, tq, HEAD_DIM), main_map(k_blk)),
                  pl.BlockSpec((1, WINDOW, HEAD_DIM), next_map(k_blk)),
                  pl.BlockSpec((1, WINDOW, HEAD_DIM), prev_map(v_blk)),
                  pl.BlockSpec((1, tq, HEAD_DIM), main_map(v_blk)),
                  pl.BlockSpec((1, WINDOW, HEAD_DIM), next_map(v_blk)),
                  pl.BlockSpec((1, c, HEAD_DIM), lambda bi, g, i: (bi, 0, kc_blk + g)),
                  pl.BlockSpec((1, c, HEAD_DIM), lambda bi, g, i: (bi, 0, vc_blk + g))],
        out_specs=pl.BlockSpec((1, tq, gw), lambda bi, g, i: (bi, i, g)),
        out_shape=jax.ShapeDtypeStruct((b, l, b_heads * HEAD_DIM), BF16),
        compiler_params=_params("parallel", "parallel", "parallel"),
        name="window_attention",
    )(sink, p, p, p, p, p, p, p, pc, pc)


def _mm_resid_kernel(*refs, nparts):
    a_refs = refs[0:nparts]
    w_refs = refs[nparts:2 * nparts]
    res_ref, gate_ref, o_ref = refs[2 * nparts:]
    y = jnp.dot(a_refs[0][0].astype(BF16), w_refs[0][...], preferred_element_type=F32)
    for a_ref, w_ref in zip(a_refs[1:], w_refs[1:]):
        y = y + jnp.dot(a_ref[0].astype(BF16), w_ref[...], preferred_element_type=F32)
    o_ref[0] = res_ref[0] + gate_ref[0] * y


def _mm_resid(parts, ws, res, gate):
    b, l, n = res.shape
    tm = min(512, l)
    tn = min(512, n)
    nparts = len(parts)
    in_specs = ([pl.BlockSpec((1, tm, a.shape[2]), lambda bi, i, j: (bi, i, 0)) for a in parts]
                + [pl.BlockSpec((w.shape[0], tn), lambda bi, i, j: (0, j)) for w in ws]
                + [pl.BlockSpec((1, tm, tn), lambda bi, i, j: (bi, i, j)),
                   pl.BlockSpec((1, 1, tn), lambda bi, i, j: (bi, 0, j))])
    return pl.pallas_call(
        functools.partial(_mm_resid_kernel, nparts=nparts),
        grid=(b, l // tm, n // tn),
        in_specs=in_specs,
        out_specs=pl.BlockSpec((1, tm, tn), lambda bi, i, j: (bi, i, j)),
        out_shape=jax.ShapeDtypeStruct((b, l, n), F32),
        compiler_params=_params("parallel", "parallel", "arbitrary"),
        name="out_projection_gated_residual",
    )(*parts, *ws, res, gate)


HALO = BF16_SUBLANES


def _fill_rows(hbuf, hp_ref, hm_ref, hn_ref, tm):
    hbuf[0:HALO, :] = hp_ref[0]
    hbuf[HALO:HALO + tm, :] = hm_ref[0]
    hbuf[HALO + tm:2 * HALO + tm, :] = hn_ref[0]


def _conv3(zbuf, cw_ref, cb_ref, tm, first, last):
    lo = slice(HALO - 1, HALO)
    hi = slice(HALO + tm, HALO + tm + 1)
    zbuf[lo, :] = jnp.where(first, 0.0, zbuf[lo, :])
    zbuf[hi, :] = jnp.where(last, 0.0, zbuf[hi, :])
    return (zbuf[HALO - 1:HALO - 1 + tm, :] * cw_ref[0:1, :] + zbuf[HALO:HALO + tm, :] * cw_ref[1:2, :]
            + zbuf[HALO + 1:HALO + 1 + tm, :] * cw_ref[2:3, :] + cb_ref[...])


def _gelu_tanh(x):
    return 0.5 * x * (1.0 + jnp.tanh(math.sqrt(2.0 / math.pi) * (x + 0.044715 * (x * x * x))))


def _ffn_epilogue(bufs, vec_refs, tm, first, last):
    cw_ref, cb_ref = vec_refs
    return _gelu_tanh(_conv3(bufs[0], cw_ref, cb_ref, tm, first, last)) * bufs[1][...]


def _hy_in_epilogue(bufs, vec_refs, tm, first, last):
    _, cw_ref, cb_ref = vec_refs
    return _conv3(bufs[0], cw_ref, cb_ref, tm, first, last)


def _mm_conv_kernel(*refs, tm, nt, nj, ntiles, nw, nv, has_bias, epilogue):
    hp_ref, hm_ref, hn_ref = refs[0:3]
    w_refs = refs[3:3 + nw]
    vec_refs = refs[3 + nw:3 + nw + nv]
    o_ref = refs[3 + nw + nv]
    hbuf = refs[4 + nw + nv]
    bufs = refs[5 + nw + nv:]
    t = pl.program_id(0)

    @pl.when((t % nj == 0) & (t < ntiles))
    def _():
        _fill_rows(hbuf, hp_ref, hm_ref, hn_ref, tm)

    def matmul(slot):
        z = jnp.dot(hbuf[...], w_refs[0][...], preferred_element_type=F32)
        bufs[slot * nw][...] = z + vec_refs[0][...] if has_bias else z
        for k in range(1, nw):
            bufs[slot * nw + k][...] = jnp.dot(hbuf[HALO:HALO + tm, :], w_refs[k][...],
                                               preferred_element_type=F32)

    def finish(slot):
        row_tile = ((t - 1) // nj) % nt
        o_ref[0] = epilogue(bufs[slot * nw:(slot + 1) * nw], vec_refs,
                            tm, row_tile == 0, row_tile == nt - 1).astype(o_ref.dtype)

    @pl.when(t == 0)
    def _():
        matmul(0)

    steady = (t > 0) & (t < ntiles)

    @pl.when(steady & (t % 2 == 1))
    def _():
        finish(0)
        matmul(1)

    @pl.when(steady & (t % 2 == 0))
    def _():
        finish(1)
        matmul(0)

    @pl.when(t == ntiles)
    def _():
        finish((ntiles - 1) % 2)


def _mm_conv(epilogue, h, ws, vecs, n, out_dtype, name, *, has_bias):
    b, l, k = h.shape
    tm = min(512, l)
    tn = min(512, n)
    nt, nj = l // tm, n // tn
    ntiles = b * nt * nj
    hpb = tm // HALO
    nhb = l // HALO
    nw, nv = len(ws), len(vecs)

    def tile(t):
        t = jnp.clip(t, 0, ntiles - 1)
        return t // (nt * nj), (t // nj) % nt, t % nj

    def h_map(which):
        def f(t):
            bi, i, _ = tile(t)
            row = {"prev": jnp.maximum(i * hpb - 1, 0), "main": i, "next": jnp.minimum((i + 1) * hpb, nhb - 1)}
            return bi, row[which], 0
        return f

    def out_map(t):
        return tile(t - 1)

    vec_specs = [pl.BlockSpec((v.shape[0], tn),
                              (lambda t: (0, tile(t)[2])) if (has_bias and vi == 0) else (lambda t: (0, tile(t - 1)[2])))
                 for vi, v in enumerate(vecs)]
    in_specs = ([pl.BlockSpec((1, HALO, k), h_map("prev")), pl.BlockSpec((1, tm, k), h_map("main")),
                 pl.BlockSpec((1, HALO, k), h_map("next"))]
                + [pl.BlockSpec((k, tn), lambda t: (0, tile(t)[2])) for _ in ws] + vec_specs)
    scratch = [pltpu.VMEM((tm + 2 * HALO, k), BF16)]
    for _ in range(2):
        scratch += [pltpu.VMEM((tm + 2 * HALO, tn), F32)] + [pltpu.VMEM((tm, tn), F32)] * (nw - 1)
    kern = functools.partial(_mm_conv_kernel, tm=tm, nt=nt, nj=nj, ntiles=ntiles, nw=nw, nv=nv,
                             has_bias=has_bias, epilogue=epilogue)
    return pl.pallas_call(
        kern,
        grid=(ntiles + 1,),
        in_specs=in_specs,
        out_specs=pl.BlockSpec((1, tm, tn), out_map),
        out_shape=jax.ShapeDtypeStruct((b, l, n), out_dtype),
        scratch_shapes=scratch,
        compiler_params=_params("arbitrary"),
        name=name,
    )(h, h, h, *ws, *vecs)


def _filter_kernel(z_ref, w0_ref, b0_ref, w1_ref, b1_ref, w2_ref, b2_ref, fr_ref, w3_ref, dl_ref,
                   o_ref, s_ref, *, tt):
    it = pl.program_id(1)
    z = z_ref[...]
    fr = fr_ref[...]

    def layer(a, w_ref, b_ref):
        return jnp.sin(fr * (jnp.dot(a, w_ref[...], preferred_element_type=F32, precision=HIGHEST)
                             + b_ref[...]))

    a = layer(layer(layer(z, w0_ref, b0_ref), w1_ref, b1_ref), w2_ref, b2_ref)
    decay = jnp.exp(-z[:, 0:1] * dl_ref[...])
    row = it * tt + lax.broadcasted_iota(jnp.int32, decay.shape, 0)

    @pl.when(it == 0)
    def _():
        s_ref[...] = jnp.zeros(s_ref.shape, F32)

    for s in range(2 * HY_ORDER):
        h = jnp.dot(a, w3_ref[s], preferred_element_type=F32, precision=HIGHEST) * decay
        if s % 2 == 1:
            h = jnp.where(row == 0, 0.0, h)
        o_ref[s] = h.astype(o_ref.dtype)
        s_ref[s] += jnp.sum(jnp.abs(h), axis=0, keepdims=True)


def _hyena_filters(l, d, f_w0, f_b0, f_w1, f_b1, f_w2, f_b2, f_freq, f_w3):
    emb, fw = f_w0.shape
    bands = (emb - 1) // 2
    t = np.linspace(0.0, 1.0, l)[:, None]
    w = 2.0 * math.pi * np.arange(l)[:, None] / l
    f = np.linspace(1e-4, bands - 1, bands)[None, :]
    emb_pad = -(-emb // 8) * 8
    z = np.concatenate([t, np.cos(f * w), -np.sin(f * w), np.zeros((l, emb_pad - emb))], axis=-1)
    z = jnp.asarray(z, F32)
    w0 = jnp.concatenate([f_w0, jnp.zeros((emb_pad - emb, fw), F32)], axis=0)
    max_decay = math.log(HY_DECAY_TARGET) / HY_MAX_DECAY_PCT
    min_decay = math.log(HY_DECAY_TARGET) / HY_MIN_DECAY_PCT
    dl = jnp.asarray(np.abs(np.linspace(min_decay, max_decay, d))[None, :], F32)
    nseg = 2 * HY_ORDER
    w3 = jnp.transpose(f_w3.reshape(fw, nseg, d), (1, 0, 2))
    tt = min(256, l)
    td = min(512, d)
    vec = lambda a: a.reshape(1, fw)
    small = lambda shape: pl.BlockSpec(shape, lambda jd, it: (0,) * len(shape))
    return pl.pallas_call(
        functools.partial(_filter_kernel, tt=tt),
        grid=(d // td, l // tt),
        in_specs=[pl.BlockSpec((tt, emb_pad), lambda jd, it: (it, 0)),
                  small((emb_pad, fw)), small((1, fw)), small((fw, fw)), small((1, fw)),
                  small((fw, fw)), small((1, fw)), small((1, fw)),
                  pl.BlockSpec((nseg, fw, td), lambda jd, it: (0, 0, jd)),
                  pl.BlockSpec((1, td), lambda jd, it: (0, jd))],
        out_specs=[pl.BlockSpec((nseg, tt, td), lambda jd, it: (0, it, jd)),
                   pl.BlockSpec((nseg, 1, td), lambda jd, it: (0, 0, jd))],
        out_shape=[jax.ShapeDtypeStruct((nseg, l, d), F32),
                   jax.ShapeDtypeStruct((nseg, 1, d), F32)],
        compiler_params=_params("parallel", "arbitrary"),
        name="hyena_filters",
    )(z, w0, vec(f_b0), f_w1, vec(f_b1), f_w2, vec(f_b2), vec(f_freq), w3, dl)


def _dft_tables(l):
    n1 = FFT_N1
    n = 2 * l
    n2 = n // n1
    k1 = np.arange(n1)[:, None]
    f1 = np.exp(-2j * np.pi * k1 * np.arange(n1 // 2)[None, :] / n1)
    tw = np.exp(-2j * np.pi * np.arange(n2)[:, None] * np.arange(n1)[None, :] / n)
    f1r, f1i = jnp.asarray(f1.real, F32), jnp.asarray(f1.imag, F32)
    twr, twi = jnp.asarray(tw.real, F32)[:, :, None], jnp.asarray(tw.imag, F32)[:, :, None]
    fr = twr * f1r - twi * f1i
    fi = twr * f1i + twi * f1r
    ta = jnp.concatenate([jnp.concatenate([fr, -fi], axis=2),
                          jnp.concatenate([fi, fr], axis=2)], axis=1)
    fb = np.exp(-2j * np.pi * np.arange(n2)[:, None] * np.arange(n2)[None, :] / n2)
    tb = np.block([[fb.real, -fb.imag], [fb.imag, fb.real]])
    return (ta.astype(BF16), jnp.swapaxes(ta, 1, 2).astype(BF16),
            jnp.asarray(tb, BF16), jnp.asarray(tb.T, BF16))


FFT_NB = 8


def _fft_a_kernel(x_ref, t_ref, o_ref):
    planes = x_ref.shape[1]
    for jj in range(FFT_NB):
        x = jnp.concatenate([x_ref[0, c, :, jj, :] for c in range(planes)], axis=0).astype(BF16)
        o_ref[0, :, jj, :] = jnp.dot(t_ref[jj], x, preferred_element_type=F32)


def _fft_a(x5, ta, *, n2, d, seg):
    s, p, half, _, _ = x5.shape
    kdim = p * half
    tn = min(512, d)
    sb, db = (seg * d) // tn, d // tn
    return pl.pallas_call(
        _fft_a_kernel,
        grid=(s, n2 // FFT_NB, db),
        in_specs=[pl.BlockSpec((1, p, half, FFT_NB, tn), lambda si, m, j: (si, 0, 0, m, sb + j)),
                  pl.BlockSpec((FFT_NB, 2 * FFT_N1, kdim), lambda si, m, j: (m, 0, 0))],
        out_specs=pl.BlockSpec((1, 2 * FFT_N1, FFT_NB, tn), lambda si, m, j: (si, 0, m, j)),
        out_shape=jax.ShapeDtypeStruct((s, 2 * FFT_N1, n2, d), F32),
        compiler_params=_params("parallel", "parallel", "parallel"),
        name="dft_stage_a",
    )(x5, ta)


def _fft_bfilt_kernel(af_ref, ab_ref, tb_ref, sc_ref, o_ref, *, n2):
    tb = tb_ref[...]
    hf = jnp.dot(tb, af_ref[0].reshape(2 * n2, -1).astype(BF16), preferred_element_type=F32)
    hb = jnp.dot(tb, ab_ref[0].reshape(2 * n2, -1).astype(BF16), preferred_element_type=F32)
    sc = sc_ref[0]
    o_ref[0, 0, 0] = (hf[0:n2] + hb[0:n2]) * sc
    o_ref[0, 1, 0] = (hf[n2:2 * n2] - hb[n2:2 * n2]) * sc


def _fft_bfilt(af, tb, scale, *, n2, d):
    tn = min(2048, d)
    blk = (1, 2, 1, n2, tn)
    return pl.pallas_call(
        functools.partial(_fft_bfilt_kernel, n2=n2),
        grid=(HY_ORDER, FFT_N1, d // tn),
        in_specs=[pl.BlockSpec(blk, lambda o, k, j: (2 * o, 0, k, 0, j)),
                  pl.BlockSpec(blk, lambda o, k, j: (2 * o + 1, 0, k, 0, j)),
                  pl.BlockSpec((2 * n2, 2 * n2), lambda o, k, j: (0, 0)),
                  pl.BlockSpec((1, 1, tn), lambda o, k, j: (o, 0, j))],
        out_specs=pl.BlockSpec(blk, lambda o, k, j: (o, 0, k, 0, j)),
        out_shape=jax.ShapeDtypeStruct((HY_ORDER, 2, FFT_N1, n2, d), F32),
        compiler_params=_params("parallel", "parallel", "parallel"),
        name="dft_stage_b_filters",
    )(af, af, tb, scale)


def _fft_b_kernel(a_ref, g_ref, tb_ref, tbt_ref, o_ref, *, n2):
    z = jnp.dot(tb_ref[...], a_ref[...].reshape(2 * n2, -1).astype(BF16), preferred_element_type=F32)
    zr, zi = z[0:n2], z[n2:2 * n2]
    gr, gi = g_ref[0, 0, 0], g_ref[0, 1, 0]
    prod = jnp.concatenate([zr * gr - zi * gi, zr * gi + zi * gr], axis=0).astype(BF16)
    y = jnp.dot(tbt_ref[...], prod, preferred_element_type=F32)
    o_ref[...] = y.reshape(o_ref.shape).astype(o_ref.dtype)


def _fft_b(a, g, tb, tbt, order, *, n2, d):
    tn = min(2048, d)
    blk = (2, 1, n2, tn)
    return pl.pallas_call(
        functools.partial(_fft_b_kernel, n2=n2),
        grid=(FFT_N1, d // tn),
        in_specs=[pl.BlockSpec(blk, lambda k, j: (0, k, 0, j)),
                  pl.BlockSpec((1,) + blk, lambda k, j: (order, 0, k, 0, j)),
                  pl.BlockSpec((2 * n2, 2 * n2), lambda k, j: (0, 0)),
                  pl.BlockSpec((2 * n2, 2 * n2), lambda k, j: (0, 0))],
        out_specs=pl.BlockSpec(blk, lambda k, j: (0, k, 0, j)),
        out_shape=jax.ShapeDtypeStruct((2, FFT_N1, n2, d), F32),
        compiler_params=_params("parallel", "parallel"),
        name="dft_stage_b_spectrum_product",
    )(a, g, tb, tbt)


def _fft_ainv_kernel(y_ref, t_ref, u_ref, gate_ref, d_ref, o_ref):
    half = o_ref.shape[1]
    for jj in range(FFT_NB):
        x = jnp.dot(t_ref[jj], y_ref[:, jj, :].astype(BF16), preferred_element_type=F32)
        for c in range(2):
            conv = x[c * half:(c + 1) * half]
            o_ref[c, :, jj, :] = gate_ref[c, :, jj, :] * (conv + d_ref[...] * u_ref[c, :, jj, :])


def _fft_ainv(y3, tat, u4, gate4, dskip, *, n2, d, useg, gseg):
    b, half, _, _ = u4.shape
    tn = min(256, d)
    db = d // tn
    usb, gsb = (useg * d) // tn, (gseg * d) // tn
    return pl.pallas_call(
        _fft_ainv_kernel,
        grid=(n2 // FFT_NB, db),
        in_specs=[pl.BlockSpec((2 * FFT_N1, FFT_NB, tn), lambda m, j: (0, m, j)),
                  pl.BlockSpec((FFT_NB, FFT_N1, 2 * FFT_N1), lambda m, j: (m, 0, 0)),
                  pl.BlockSpec((b, half, FFT_NB, tn), lambda m, j: (0, 0, m, usb + j)),
                  pl.BlockSpec((b, half, FFT_NB, tn), lambda m, j: (0, 0, m, gsb + j)),
                  pl.BlockSpec((1, tn), lambda m, j: (0, j))],
        out_specs=pl.BlockSpec((b, half, FFT_NB, tn), lambda m, j: (0, 0, m, j)),
        out_shape=jax.ShapeDtypeStruct((b, half, n2, d), F32),
        compiler_params=_params("parallel", "parallel"),
        name="dft_stage_a_inverse_gate",
    )(y3, tat, u4, gate4, dskip)


def _hyena_mixer(h, hy, l, d):
    (w_in, b_in, sconv_w, sconv_b, f_w0, f_b0, f_w1, f_b1, f_w2, f_b2, f_freq, f_w3, d_skip, w_out) = hy
    b = h.shape[0]
    assert b == 2, "the two batch rows ride the real / imaginary planes of one complex DFT"
    n1 = FFT_N1
    half = n1 // 2
    n2 = (2 * l) // n1
    assert half * n2 == l
    z3 = _mm_conv(_hy_in_epilogue, h, [w_in.astype(BF16)], [b_in.reshape(1, -1), sconv_w, sconv_b.reshape(1, -1)],
                  3 * d, F32, "hyena_in_proj_conv", has_bias=True)
    filt, fsum = _hyena_filters(l, d, f_w0, f_b0, f_w1, f_b1, f_w2, f_b2, f_freq, f_w3)
    ta, tat, tb, tbt = _dft_tables(l)
    nseg = 2 * HY_ORDER
    af = _fft_a(filt.reshape(nseg, 1, half, n2, d), ta, n2=n2, d=d, seg=0)
    norm = (fsum[0::2] + fsum[1::2])
    g = _fft_bfilt(af.reshape(nseg, 2, n1, n2, d), tb, 1.0 / (norm * (2 * l)), n2=n2, d=d)
    z4 = z3.reshape(b, half, n2, 3 * d)
    y, yseg = z4, 0
    for o in range(HY_ORDER):
        a = _fft_a(y[None], ta, n2=n2, d=d, seg=yseg)
        yb = _fft_b(a.reshape(2, n1, n2, d), g, tb, tbt, o, n2=n2, d=d)
        y = _fft_ainv(yb.reshape(2 * n1, n2, d), tat, y, z4, d_skip[o].reshape(1, d),
                      n2=n2, d=d, useg=yseg, gseg=o + 1)
        yseg = 0
    return y.reshape(b, l, d), w_out.astype(BF16)


def _rope_tables(l):
    t = jnp.arange(l)
    row = (t // GRID_W).astype(F32)[:, None]
    col = (t % GRID_W).astype(F32)[:, None]
    cos, sin = [], []
    for dim in (A_HALF, HEAD_DIM):
        nf = dim // 4
        inv = jnp.asarray(ROPE_BASE ** (-np.arange(nf) / nf), F32)[None, :]
        ar, ac = row * inv, col * inv
        c = jnp.concatenate([jnp.cos(ar), jnp.cos(ar), jnp.cos(ac), jnp.cos(ac)], axis=1)
        s = jnp.concatenate([-jnp.sin(ar), jnp.sin(ar), -jnp.sin(ac), jnp.sin(ac)], axis=1)
        reps = HEAD_DIM // dim
        cos.append(jnp.tile(c, (1, reps)))
        sin.append(jnp.tile(s, (1, reps)))
    return jnp.stack(cos), jnp.stack(sin)


def _attn_mixer(h, hc, lam_init, w_in, w_out, a_q_g, a_k_g, lq1, lk1, lq2, lk2, a_sub_g, b_q_g, b_k_g, b_sink):
    b, l, d = h.shape
    c = hc.shape[1]
    a_heads = d // (2 * HEAD_DIM)
    b_heads = d // (2 * HEAD_DIM)
    q_cols = (a_heads + b_heads) * HEAD_DIM
    in_cols = w_in.shape[1]
    lam = (jnp.exp(jnp.sum(lq1 * lk1)) - jnp.exp(jnp.sum(lq2 * lk2)) + lam_init).reshape(1, 1)
    sink = b_sink.reshape(B_KV_HEADS, b_heads // B_KV_HEADS)

    ones = lambda n: jnp.ones((n,), F32)
    tile = lambda v, n: jnp.tile(v, n)
    gain = jnp.concatenate([tile(a_q_g, 2 * a_heads), tile(b_q_g, b_heads), tile(a_k_g, 2 * a_heads),
                            ones(a_heads * HEAD_DIM), tile(b_k_g, B_KV_HEADS),
                            ones(B_KV_HEADS * HEAD_DIM)]).reshape(1, in_cols)
    scale = jnp.concatenate([jnp.full((a_heads * HEAD_DIM,), A_HALF ** -0.5 * math.log2(math.e), F32),
                             jnp.full((b_heads * HEAD_DIM,), HEAD_DIM ** -0.5, F32),
                             ones(in_cols - q_cols)]).reshape(1, in_cols)
    e = [0, a_heads // 2, (a_heads + b_heads) // 2, (2 * a_heads + b_heads) // 2,
         (3 * a_heads + b_heads) // 2, (3 * a_heads + b_heads) // 2 + 1, (3 * a_heads + b_heads) // 2 + 2]
    chunks = [A_HALF, HEAD_DIM, A_HALF, 0, HEAD_DIM, 0]
    modes = tuple((e[s], e[s + 1], chunks[s]) for s in range(6))
    cos_tab, sin_tab = _rope_tables(l)

    def type_of(j0):
        def f(j):
            jj = j + j0
            is_a = (jj < e[1]) | ((jj >= e[2]) & (jj < e[3]))
            return jnp.where(is_a, 0, 1)
        return f

    w_bf = w_in.astype(BF16)
    p = _inproj(h.reshape(b * l, d), w_bf, gain, scale, cos_tab, sin_tab, modes=modes,
                type_of_block=type_of(0), col0=0, ncols=in_cols, l=l, rope=True).reshape(b, l, in_cols)
    kv0 = e[2]
    modes_c = tuple((lo - kv0, hi - kv0, ch) for lo, hi, ch in modes[2:])
    pc = _inproj(hc.reshape(b * c, d), w_bf, gain, scale, cos_tab, sin_tab, modes=modes_c,
                 type_of_block=type_of(kv0), col0=kv0, ncols=in_cols - q_cols, l=c, rope=False
                 ).reshape(b, c, in_cols - q_cols)
    score_bound = (A_HALF ** 0.5 * math.log2(math.e)) * jnp.max(jnp.abs(a_q_g)) * jnp.max(jnp.abs(a_k_g))
    oa = _diffattn(lam, p, pc, a_sub_g, score_bound, heads=a_heads, l=l, c=c, out_scale=1.0 - lam_init)
    ob = _winattn(sink, p, pc, a_heads=a_heads, b_heads=b_heads, l=l, c=c)
    w_out_bf = w_out.astype(BF16)
    na = a_heads * HEAD_DIM
    return [oa, ob], [w_out_bf[:na], w_out_bf[na:]]


def kernel(x, c, ctx, c_ctx, ada_w, ada_b, norm1_g, norm2_g, attn_w_in, attn_w_out, a_q_g, a_k_g, a_lam_q1, a_lam_k1, a_lam_q2, a_lam_k2, a_sub_g, b_q_g, b_k_g, b_sink, hy_w_in, hy_b_in, hy_sconv_w, hy_sconv_b, hy_f_w0, hy_f_b0, hy_f_w1, hy_f_b1, hy_f_w2, hy_f_b2, hy_f_freq, hy_f_w3, hy_d, hy_w_out, ffn_w_gate, ffn_w_val, ffn_conv_w, ffn_conv_b, ffn_w_down):
    b, l, d = x.shape
    depth = ada_w.shape[0]
    cc = jnp.concatenate([c, c_ctx[None, :], jnp.zeros((8 - b - 1, d), F32)], axis=0)
    m = _ada(cc, ada_w, ada_b)
    xs = x
    for layer in range(depth):
        i = layer // 2
        lat = [m[layer, :b, k * d:(k + 1) * d].reshape(b, 1, d) for k in range(6)]
        sh1, sc1, g1, sh2, sc2, g2 = lat
        h = _normmod(xs, norm1_g[layer], sc1, sh1)
        if layer % 2 == 0:
            mc = [jnp.broadcast_to(m[layer, b, k * d:(k + 1) * d].reshape(1, 1, d), (b, 1, d)) for k in range(2)]
            hc = _normmod(ctx, norm1_g[layer], mc[1], mc[0])
            parts, ws = _attn_mixer(h, hc, 0.8 - 0.6 * math.exp(-0.3 * layer), attn_w_in[i], attn_w_out[i],
                                    a_q_g[i], a_k_g[i], a_lam_q1[i], a_lam_k1[i], a_lam_q2[i], a_lam_k2[i],
                                    a_sub_g[i], b_q_g[i], b_k_g[i], b_sink[i])
        else:
            hy = (hy_w_in[i], hy_b_in[i], hy_sconv_w[i], hy_sconv_b[i], hy_f_w0[i], hy_f_b0[i], hy_f_w1[i],
                  hy_f_b1[i], hy_f_w2[i], hy_f_b2[i], hy_f_freq[i], hy_f_w3[i], hy_d[i], hy_w_out[i])
            y, w_o = _hyena_mixer(h, hy, l, d)
            parts, ws = [y], [w_o]
        xs = _mm_resid(parts, ws, xs, g1)
        h2 = _normmod(xs, norm2_g[layer], sc2, sh2)
        d_ff = ffn_w_gate.shape[2]
        hid = _mm_conv(_ffn_epilogue, h2, [ffn_w_gate[layer].astype(BF16), ffn_w_val[layer].astype(BF16)],
                       [ffn_conv_w[layer], ffn_conv_b[layer].reshape(1, -1)], d_ff, BF16, "ffn_up_conv_glu",
                       has_bias=False)
        xs = _mm_resid([hid], [ffn_w_down[layer].astype(BF16)], xs, g2)
    return xs
```

```python
import functools
import math

import numpy as np
import jax
import jax.numpy as jnp
from jax import lax
from jax.experimental import pallas as pl
from jax.experimental.pallas import tpu as pltpu

F32 = jnp.float32
BF16 = jnp.bfloat16
HIGHEST = lax.Precision.HIGHEST

HEAD_DIM = 128
A_HALF = HEAD_DIM // 2
B_KV_HEADS = 2
GRID_W = 64
WINDOW = 128
ROPE_BASE = 10000.0
EPS = 1e-6
NEG = -1e30
HY_ORDER = 2
HY_DECAY_TARGET = 1e-2
HY_MAX_DECAY_PCT = 0.3
HY_MIN_DECAY_PCT = 1.5

LANES = 128
BF16_SUBLANES = 16
FFT_N1 = 256
VMEM_LIMIT_BYTES = 56 * 1024 * 1024


def _params(*sem):
    return pltpu.CompilerParams(dimension_semantics=sem, vmem_limit_bytes=VMEM_LIMIT_BYTES)


def _ada_kernel(c_ref, w_ref, b_ref, o_ref):
    c = c_ref[...]
    s = c * (1.0 / (1.0 + jnp.exp(-c)))
    o_ref[0] = jnp.dot(s, w_ref[0], preferred_element_type=F32, precision=HIGHEST) + b_ref[0]


def _ada(cc, ada_w, ada_b):
    depth, d, n = ada_w.shape
    tn = 1024
    return pl.pallas_call(
        _ada_kernel,
        grid=(depth, n // tn),
        in_specs=[pl.BlockSpec((8, d), lambda l, j: (0, 0)),
                  pl.BlockSpec((1, d, tn), lambda l, j: (l, 0, j)),
                  pl.BlockSpec((1, 1, tn), lambda l, j: (l, 0, j))],
        out_specs=pl.BlockSpec((1, 8, tn), lambda l, j: (l, 0, j)),
        out_shape=jax.ShapeDtypeStruct((depth, 8, n), F32),
        compiler_params=_params("parallel", "parallel"),
        name="ada_modulation",
    )(cc, ada_w, ada_b.reshape(depth, 1, n))


def _normmod_kernel(x_ref, g_ref, sc_ref, sh_ref, o_ref):
    x = x_ref[0]
    ms = jnp.mean(x * x, axis=-1, keepdims=True)
    y = x * lax.rsqrt(ms + EPS) * g_ref[...]
    o_ref[0] = (y * (1.0 + sc_ref[0]) + sh_ref[0]).astype(o_ref.dtype)


def _normmod(x, g, sc, sh):
    b, l, d = x.shape
    tm = min(512, l)
    return pl.pallas_call(
        _normmod_kernel,
        grid=(b, l // tm),
        in_specs=[pl.BlockSpec((1, tm, d), lambda bi, i: (bi, i, 0)),
                  pl.BlockSpec((1, d), lambda bi, i: (0, 0)),
                  pl.BlockSpec((1, 1, d), lambda bi, i: (bi, 0, 0)),
                  pl.BlockSpec((1, 1, d), lambda bi, i: (bi, 0, 0))],
        out_specs=pl.BlockSpec((1, tm, d), lambda bi, i: (bi, i, 0)),
        out_shape=jax.ShapeDtypeStruct((b, l, d), BF16),
        compiler_params=_params("parallel", "parallel"),
        name="rmsnorm_modulate",
    )(x, g.reshape(1, d), sc, sh)


def _head_epilogue(z, gain, scale, cos, sin, chunk, rope):
    lane = lax.broadcasted_iota(jnp.int32, z.shape, 1)
    zz = z * z
    s_all = jnp.sum(zz, axis=-1, keepdims=True)
    if chunk == HEAD_DIM:
        ms = s_all * (1.0 / HEAD_DIM)
    else:
        s_lo = jnp.sum(jnp.where(lane < A_HALF, zz, 0.0), axis=-1, keepdims=True)
        ms = jnp.where(lane < A_HALF, s_lo, s_all - s_lo) * (1.0 / A_HALF)
    y = z * lax.rsqrt(ms + EPS) * gain
    if rope:
        nf = chunk // 4
        first = (lane % (2 * nf)) < nf
        partner = jnp.where(first, pltpu.roll(y, HEAD_DIM - nf, 1), pltpu.roll(y, nf, 1))
        y = y * cos + partner * sin
    return y * scale


def _inproj_kernel(h_ref, w_ref, gain_ref, scale_ref, cos_ref, sin_ref, o_ref, *, modes, rope):
    j = pl.program_id(1)
    z = jnp.dot(h_ref[...], w_ref[...], preferred_element_type=F32)
    for lo, hi, chunk in modes:
        @pl.when((j >= lo) & (j < hi))
        def _(chunk=chunk):
            for half in range(2):
                sl = slice(half * HEAD_DIM, (half + 1) * HEAD_DIM)
                zh = z[:, sl]
                if chunk:
                    zh = _head_epilogue(zh, gain_ref[:, sl], scale_ref[:, sl],
                                        cos_ref[0], sin_ref[0], chunk, rope)
                o_ref[:, sl] = zh.astype(o_ref.dtype)


def _inproj(h2d, w, gain, scale, cos_tab, sin_tab, *, modes, type_of_block, col0, ncols, l, rope):
    m, d = h2d.shape
    tn = 2 * HEAD_DIM
    tm = min(512, l)
    pos_blocks = l // tm
    kern = functools.partial(_inproj_kernel, modes=modes, rope=rope)
    return pl.pallas_call(
        kern,
        grid=(m // tm, ncols // tn),
        in_specs=[pl.BlockSpec((tm, d), lambda i, j: (i, 0)),
                  pl.BlockSpec((d, tn), lambda i, j: (0, j + col0)),
                  pl.BlockSpec((1, tn), lambda i, j: (0, j + col0)),
                  pl.BlockSpec((1, tn), lambda i, j: (0, j + col0)),
                  pl.BlockSpec((1, tm, HEAD_DIM), lambda i, j: (type_of_block(j), i % pos_blocks, 0)),
                  pl.BlockSpec((1, tm, HEAD_DIM), lambda i, j: (type_of_block(j), i % pos_blocks, 0))],
        out_specs=pl.BlockSpec((tm, tn), lambda i, j: (i, j)),
        out_shape=jax.ShapeDtypeStruct((m, ncols), BF16),
        compiler_params=_params("parallel", "arbitrary"),
        name="attn_in_projection",
    )(h2d, w, gain, scale, cos_tab, sin_tab)


def _diffattn_kernel(lam_ref, q_ref, kc_ref, vc_ref, k_ref, v_ref, g_ref, o_ref,
                     q2_s, m_s, l_s, acc_s, *, tq, nkv, out_scale):
    kv = pl.program_id(3)

    def process(k, v):
        s = lax.dot_general(q2_s[...], k, (((1,), (1,)), ((), ())), preferred_element_type=F32)
        m_prev = m_s[...]
        m_new = jnp.maximum(m_prev, jnp.max(s, axis=-1, keepdims=True))
        alpha = jnp.exp2(m_prev - m_new)
        p = jnp.exp2(s - m_new)
        l_s[...] = alpha * l_s[...] + jnp.sum(p, axis=-1, keepdims=True)
        acc_s[...] = alpha * acc_s[...] + jnp.dot(p.astype(BF16), v, preferred_element_type=F32)
        m_s[...] = m_new

    @pl.when(kv == 0)
    def _():
        q = q_ref[0]
        lane = lax.broadcasted_iota(jnp.int32, q.shape, 1)
        zero = jnp.zeros_like(q)
        q2_s[0:tq, :] = jnp.where(lane < A_HALF, q, zero)
        q2_s[tq:2 * tq, :] = jnp.where(lane >= A_HALF, q, zero)
        m_s[...] = jnp.full(m_s.shape, -jnp.inf, F32)
        l_s[...] = jnp.zeros(l_s.shape, F32)
        acc_s[...] = jnp.zeros(acc_s.shape, F32)
        process(kc_ref[0], vc_ref[0])

    process(k_ref[0], v_ref[0])

    @pl.when(kv == nkv - 1)
    def _():
        o = acc_s[...] / l_s[...]
        d = o[0:tq] - lam_ref[0, 0] * o[tq:2 * tq]
        ms = jnp.mean(d * d, axis=-1, keepdims=True)
        o_ref[0] = (d * lax.rsqrt(ms + EPS) * g_ref[...] * out_scale).astype(o_ref.dtype)


def _diffattn_fast_kernel(lam_ref, q_ref, kc_ref, vct_ref, k_ref, vt_ref, g_ref, o_ref,
                          q2_s, l_s, acc_s, p_s, *, tq, tk, kc, nkv, out_scale):
    kv = pl.program_id(3)
    dn = (((1,), (1,)), ((), ()))

    def weights(k):
        n = k.shape[0]
        p = jnp.exp2(lax.dot_general(k, q2_s[...], dn, preferred_element_type=F32))
        return p.astype(BF16), jnp.sum(p.reshape(n // 8, 8, 2 * tq), axis=0)

    @pl.when(kv == 0)
    def _():
        q = q_ref[0]
        lane = lax.broadcasted_iota(jnp.int32, q.shape, 1)
        zero = jnp.zeros_like(q)
        q2_s[0:tq, :] = jnp.where(lane < A_HALF, q, zero)
        q2_s[tq:2 * tq, :] = jnp.where(lane >= A_HALF, q, zero)
        p, ls = weights(kc_ref[0])
        l_s[...] = ls
        acc_s[...] = jnp.dot(vct_ref[0, 0], p, preferred_element_type=F32)

    ls = l_s[...]
    for ci in range(tk // kc):
        p, lc = weights(k_ref[0, ci * kc:(ci + 1) * kc, :])
        p_s[ci * kc:(ci + 1) * kc, :] = p
        ls = ls + lc
    l_s[...] = ls
    acc_s[...] += jnp.dot(vt_ref[0, 0], p_s[...], preferred_element_type=F32)

    @pl.when(kv == nkv - 1)
    def _():
        l = jnp.sum(l_s[...], axis=0, keepdims=True)
        o = acc_s[...] / l
        d = o[:, 0:tq] - lam_ref[0, 0] * o[:, tq:2 * tq]
        ms = jnp.mean(d * d, axis=0, keepdims=True)
        y = d * lax.rsqrt(ms + EPS) * g_ref[...] * out_scale
        o_ref[0] = y.T.astype(o_ref.dtype)


def _diffattn_fast(lam, p, pc, sub_g, *, heads, l, c, out_scale):
    b = p.shape[0]
    tq = min(512, l)
    tk = min(4096, l)
    kc = min(512, tk)
    nkv = l // tk
    k_blk = 2 * heads
    hd = heads * HEAD_DIM
    vt = jnp.transpose(p[:, :, 3 * hd:4 * hd].reshape(b, l, heads, HEAD_DIM), (0, 2, 3, 1))
    vct = jnp.transpose(pc[:, :, hd:2 * hd].reshape(b, c, heads, HEAD_DIM), (0, 2, 3, 1))
    kern = functools.partial(_diffattn_fast_kernel, tq=tq, tk=tk, kc=kc, nkv=nkv, out_scale=out_scale)
    return pl.pallas_call(
        kern,
        grid=(b, heads, l // tq, nkv),
        in_specs=[pl.BlockSpec(memory_space=pltpu.SMEM),
                  pl.BlockSpec((1, tq, HEAD_DIM), lambda bi, h, i, kv: (bi, i, h)),
                  pl.BlockSpec((1, c, HEAD_DIM), lambda bi, h, i, kv: (bi, 0, h)),
                  pl.BlockSpec((1, 1, HEAD_DIM, c), lambda bi, h, i, kv: (bi, h, 0, 0)),
                  pl.BlockSpec((1, tk, HEAD_DIM), lambda bi, h, i, kv: (bi, kv, k_blk + h)),
                  pl.BlockSpec((1, 1, HEAD_DIM, tk), lambda bi, h, i, kv: (bi, h, 0, kv)),
                  pl.BlockSpec((HEAD_DIM, 1), lambda bi, h, i, kv: (0, 0))],
        out_specs=pl.BlockSpec((1, tq, HEAD_DIM), lambda bi, h, i, kv: (bi, i, h)),
        out_shape=jax.ShapeDtypeStruct((b, l, hd), BF16),
        scratch_shapes=[pltpu.VMEM((2 * tq, HEAD_DIM), BF16),
                        pltpu.VMEM((8, 2 * tq), F32),
                        pltpu.VMEM((HEAD_DIM, 2 * tq), F32),
                        pltpu.VMEM((tk, 2 * tq), BF16)],
        compiler_params=_params("parallel", "parallel", "parallel", "arbitrary"),
        name="diff_attention_bounded",
    )(lam, p, pc, vct, p, vt, sub_g.reshape(HEAD_DIM, 1))


SCORE_BOUND_LOG2 = 60.0


def _diffattn(lam, p, pc, sub_g, score_bound, *, heads, l, c, out_scale):
    kw = dict(heads=heads, l=l, c=c, out_scale=out_scale)
    return lax.cond(score_bound < SCORE_BOUND_LOG2,
                    lambda *a: _diffattn_fast(*a, **kw), lambda *a: _diffattn_safe(*a, **kw),
                    lam, p, pc, sub_g)


def _diffattn_safe(lam, p, pc, sub_g, *, heads, l, c, out_scale):
    b = p.shape[0]
    tq = min(256, l)
    tk = min(512, l)
    nkv = l // tk
    q_blk = 0
    k_blk = 2 * heads
    v_blk = 3 * heads
    kern = functools.partial(_diffattn_kernel, tq=tq, nkv=nkv, out_scale=out_scale)
    return pl.pallas_call(
        kern,
        grid=(b, heads, l // tq, nkv),
        in_specs=[pl.BlockSpec(memory_space=pltpu.SMEM),
                  pl.BlockSpec((1, tq, HEAD_DIM), lambda bi, h, i, kv: (bi, i, q_blk + h)),
                  pl.BlockSpec((1, c, HEAD_DIM), lambda bi, h, i, kv: (bi, 0, h)),
                  pl.BlockSpec((1, c, HEAD_DIM), lambda bi, h, i, kv: (bi, 0, heads + h)),
                  pl.BlockSpec((1, tk, HEAD_DIM), lambda bi, h, i, kv: (bi, kv, k_blk + h)),
                  pl.BlockSpec((1, tk, HEAD_DIM), lambda bi, h, i, kv: (bi, kv, v_blk + h)),
                  pl.BlockSpec((1, HEAD_DIM), lambda bi, h, i, kv: (0, 0))],
        out_specs=pl.BlockSpec((1, tq, HEAD_DIM), lambda bi, h, i, kv: (bi, i, h)),
        out_shape=jax.ShapeDtypeStruct((b, l, heads * HEAD_DIM), BF16),
        scratch_shapes=[pltpu.VMEM((2 * tq, HEAD_DIM), BF16),
                        pltpu.VMEM((2 * tq, 1), F32),
                        pltpu.VMEM((2 * tq, 1), F32),
                        pltpu.VMEM((2 * tq, HEAD_DIM), F32)],
        compiler_params=_params("parallel", "parallel", "parallel", "arbitrary"),
        name="diff_attention",
    )(lam, p, pc, pc, p, p, sub_g.reshape(1, HEAD_DIM))


def _winattn_kernel(sink_ref, q_ref, kp_ref, km_ref, kn_ref, vp_ref, vm_ref, vn_ref, kc_ref, vc_ref,
                    o_ref, *, tq, l, group):
    g = pl.program_id(1)
    i = pl.program_id(2)
    kband = jnp.concatenate([kp_ref[0], km_ref[0], kn_ref[0]], axis=0)
    vband = jnp.concatenate([vp_ref[0], vm_ref[0], vn_ref[0]], axis=0)
    nk = tq + 2 * WINDOW
    qpos = i * tq + lax.broadcasted_iota(jnp.int32, (tq, nk), 0)
    kpos = i * tq - WINDOW + lax.broadcasted_iota(jnp.int32, (tq, nk), 1)
    valid = (jnp.abs(kpos - qpos) <= WINDOW) & (kpos >= 0) & (kpos < l)
    kc = kc_ref[0]
    vc = vc_ref[0]
    dn = (((1,), (1,)), ((), ()))
    for r in range(group):
        sl = slice(r * HEAD_DIM, (r + 1) * HEAD_DIM)
        q = q_ref[0, :, sl]
        s_lat = jnp.where(valid, lax.dot_general(q, kband, dn, preferred_element_type=F32), NEG)
        s_ctx = lax.dot_general(q, kc, dn, preferred_element_type=F32)
        sk = sink_ref[g, r]
        m = jnp.maximum(jnp.maximum(jnp.max(s_lat, axis=-1, keepdims=True),
                                    jnp.max(s_ctx, axis=-1, keepdims=True)), sk)
        p_lat = jnp.exp(s_lat - m)
        p_ctx = jnp.exp(s_ctx - m)
        denom = (jnp.sum(p_lat, axis=-1, keepdims=True) + jnp.sum(p_ctx, axis=-1, keepdims=True)
                 + jnp.exp(sk - m))
        o = (jnp.dot(p_lat.astype(BF16), vband, preferred_element_type=F32)
             + jnp.dot(p_ctx.astype(BF16), vc, preferred_element_type=F32))
        o_ref[0, :, sl] = (o / denom).astype(o_ref.dtype)


def _winattn(sink, p, pc, *, a_heads, b_heads, l, c):
    b = p.shape[0]
    group = b_heads // B_KV_HEADS
    tq = min(256, l)
    wpb = tq // WINDOW
    nwb = l // WINDOW
    gw = group * HEAD_DIM
    q_blk = a_heads * HEAD_DIM // gw
    k_blk = (a_heads + b_heads) + 2 * a_heads
    v_blk = k_blk + B_KV_HEADS
    kc_blk = 2 * a_heads
    vc_blk = kc_blk + B_KV_HEADS
    kern = functools.partial(_winattn_kernel, tq=tq, l=l, group=group)

    def prev_map(col):
        return lambda bi, g, i: (bi, jnp.maximum(i * wpb - 1, 0), col + g)

    def main_map(col):
        return lambda bi, g, i: (bi, i, col + g)

    def next_map(col):
        return lambda bi, g, i: (bi, jnp.minimum((i + 1) * wpb, nwb - 1), col + g)

    return pl.pallas_call(
        kern,
        grid=(b, B_KV_HEADS, l // tq),
        in_specs=[pl.BlockSpec(memory_space=pltpu.SMEM),
                  pl.BlockSpec((1, tq, gw), lambda bi, g, i: (bi, i, q_blk + g)),
                  pl.BlockSpec((1, WINDOW, HEAD_DIM), prev_map(k_blk)),
                  pl.BlockSpec((1, tq, HEAD_DIM), main_map(k_blk)),
                  pl.BlockSpec((1, WINDOW, HEAD_DIM), next_map(k_blk)),
                  pl.BlockSpec((1, WINDOW, HEAD_DIM), prev_map(v_blk)),
                  pl.BlockSpec((1, tq, HEAD_DIM), main_map(v_blk)),
                  pl.BlockSpec((1, WINDOW, HEAD_DIM), next_map(v_blk)),
                  pl.BlockSpec((1, c, HEAD_DIM), lambda bi, g, i: (bi, 0, kc_blk + g)),
                  pl.BlockSpec((1, c, HEAD_DIM), lambda bi, g, i: (bi, 0, vc_blk + g))],
        out_specs=pl.BlockSpec((1, tq, gw), lambda bi, g, i: (bi, i, g)),
        out_shape=jax.ShapeDtypeStruct((b, l, b_heads * HEAD_DIM), BF16),
        compiler_params=_params("parallel", "parallel", "parallel"),
        name="window_attention",
    )(sink, p, p, p, p, p, p, p, pc, pc)


def _mm_resid_kernel(*refs, nparts):
    a_refs = refs[0:nparts]
    w_refs = refs[nparts:2 * nparts]
    res_ref, gate_ref, o_ref = refs[2 * nparts:]
    y = jnp.dot(a_refs[0][0].astype(BF16), w_refs[0][...], preferred_element_type=F32)
    for a_ref, w_ref in zip(a_refs[1:], w_refs[1:]):
        y = y + jnp.dot(a_ref[0].astype(BF16), w_ref[...], preferred_element_type=F32)
    o_ref[0] = res_ref[0] + gate_ref[0] * y


def _mm_resid(parts, ws, res, gate):
    b, l, n = res.shape
    tm = min(512, l)
    tn = min(512, n)
    nparts = len(parts)
    in_specs = ([pl.BlockSpec((1, tm, a.shape[2]), lambda bi, i, j: (bi, i, 0)) for a in parts]
                + [pl.BlockSpec((w.shape[0], tn), lambda bi, i, j: (0, j)) for w in ws]
                + [pl.BlockSpec((1, tm, tn), lambda bi, i, j: (bi, i, j)),
                   pl.BlockSpec((1, 1, tn), lambda bi, i, j: (bi, 0, j))])
    return pl.pallas_call(
        functools.partial(_mm_resid_kernel, nparts=nparts),
        grid=(b, l // tm, n // tn),
        in_specs=in_specs,
        out_specs=pl.BlockSpec((1, tm, tn), lambda bi, i, j: (bi, i, j)),
        out_shape=jax.ShapeDtypeStruct((b, l, n), F32),
        compiler_params=_params("parallel", "parallel", "arbitrary"),
        name="out_projection_gated_residual",
    )(*parts, *ws, res, gate)


HALO = BF16_SUBLANES


def _fill_rows(hbuf, hp_ref, hm_ref, hn_ref, tm):
    hbuf[0:HALO, :] = hp_ref[0]
    hbuf[HALO:HALO + tm, :] = hm_ref[0]
    hbuf[HALO + tm:2 * HALO + tm, :] = hn_ref[0]


def _conv3(zbuf, cw_ref, cb_ref, tm, first, last):
    @pl.when(first)
    def _():
        zbuf[HALO - 1:HALO, :] = jnp.zeros((1, zbuf.shape[1]), F32)

    @pl.when(last)
    def _():
        zbuf[HALO + tm:HALO + tm + 1, :] = jnp.zeros((1, zbuf.shape[1]), F32)

    return (zbuf[HALO - 1:HALO - 1 + tm, :] * cw_ref[0:1, :] + zbuf[HALO:HALO + tm, :] * cw_ref[1:2, :]
            + zbuf[HALO + 1:HALO + 1 + tm, :] * cw_ref[2:3, :] + cb_ref[...])


def _gelu_tanh(x):
    return 0.5 * x * (1.0 + jnp.tanh(math.sqrt(2.0 / math.pi) * (x + 0.044715 * (x * x * x))))


def _ffn_up_kernel(hp_ref, hm_ref, hn_ref, wg_ref, wv_ref, cw_ref, cb_ref, o_ref, hbuf, zbuf, *, tm, nt):
    i = pl.program_id(1)
    j = pl.program_id(2)

    @pl.when(j == 0)
    def _():
        _fill_rows(hbuf, hp_ref, hm_ref, hn_ref, tm)

    zbuf[...] = jnp.dot(hbuf[...], wg_ref[...], preferred_element_type=F32)
    g = _conv3(zbuf, cw_ref, cb_ref, tm, i == 0, i == nt - 1)
    v = jnp.dot(hbuf[HALO:HALO + tm, :], wv_ref[...], preferred_element_type=F32)
    o_ref[0] = (_gelu_tanh(g) * v).astype(o_ref.dtype)


def _hy_in_kernel(hp_ref, hm_ref, hn_ref, w_ref, b_ref, cw_ref, cb_ref, o_ref, hbuf, zbuf, *, tm, nt):
    i = pl.program_id(1)
    j = pl.program_id(2)

    @pl.when(j == 0)
    def _():
        _fill_rows(hbuf, hp_ref, hm_ref, hn_ref, tm)

    zbuf[...] = jnp.dot(hbuf[...], w_ref[...], preferred_element_type=F32) + b_ref[...]
    o_ref[0] = _conv3(zbuf, cw_ref, cb_ref, tm, i == 0, i == nt - 1).astype(o_ref.dtype)


def _mm_conv(kernel, h, ws, vecs, n, out_dtype, name):
    b, l, k = h.shape
    tm = min(512, l)
    tn = min(512, n)
    nt = l // tm
    hpb = tm // HALO
    nhb = l // HALO
    in_specs = ([pl.BlockSpec((1, HALO, k), lambda bi, i, j: (bi, jnp.maximum(i * hpb - 1, 0), 0)),
                 pl.BlockSpec((1, tm, k), lambda bi, i, j: (bi, i, 0)),
                 pl.BlockSpec((1, HALO, k), lambda bi, i, j: (bi, jnp.minimum((i + 1) * hpb, nhb - 1), 0))]
                + [pl.BlockSpec((k, tn), lambda bi, i, j: (0, j)) for _ in ws]
                + [pl.BlockSpec((v.shape[0], tn), lambda bi, i, j: (0, j)) for v in vecs])
    return pl.pallas_call(
        functools.partial(kernel, tm=tm, nt=nt),
        grid=(b, nt, n // tn),
        in_specs=in_specs,
        out_specs=pl.BlockSpec((1, tm, tn), lambda bi, i, j: (bi, i, j)),
        out_shape=jax.ShapeDtypeStruct((b, l, n), out_dtype),
        scratch_shapes=[pltpu.VMEM((tm + 2 * HALO, k), BF16),
                        pltpu.VMEM((tm + 2 * HALO, tn), F32)],
        compiler_params=_params("parallel", "parallel", "arbitrary"),
        name=name,
    )(h, h, h, *ws, *vecs)


def _filter_kernel(z_ref, w0_ref, b0_ref, w1_ref, b1_ref, w2_ref, b2_ref, fr_ref, w3_ref, dl_ref,
                   o_ref, s_ref, *, tt):
    it = pl.program_id(1)
    z = z_ref[...]
    fr = fr_ref[...]

    def layer(a, w_ref, b_ref):
        return jnp.sin(fr * (jnp.dot(a, w_ref[...], preferred_element_type=F32, precision=HIGHEST)
                             + b_ref[...]))

    a = layer(layer(layer(z, w0_ref, b0_ref), w1_ref, b1_ref), w2_ref, b2_ref)
    decay = jnp.exp(-z[:, 0:1] * dl_ref[...])
    row = it * tt + lax.broadcasted_iota(jnp.int32, decay.shape, 0)

    @pl.when(it == 0)
    def _():
        s_ref[...] = jnp.zeros(s_ref.shape, F32)

    a_hi = a.astype(BF16)
    a_lo = (a - a_hi.astype(F32)).astype(BF16)
    a3 = jnp.concatenate([a_hi, a_hi, a_lo], axis=1)
    for s in range(2 * HY_ORDER):
        h = jnp.dot(a3, w3_ref[s], preferred_element_type=F32) * decay
        if s % 2 == 1:
            h = jnp.where(row == 0, 0.0, h)
        o_ref[s] = h.astype(o_ref.dtype)
        s_ref[s] += jnp.sum(jnp.abs(h), axis=0, keepdims=True)


def _hyena_filters(l, d, f_w0, f_b0, f_w1, f_b1, f_w2, f_b2, f_freq, f_w3):
    emb, fw = f_w0.shape
    bands = (emb - 1) // 2
    t = np.linspace(0.0, 1.0, l)[:, None]
    w = 2.0 * math.pi * np.arange(l)[:, None] / l
    f = np.linspace(1e-4, bands - 1, bands)[None, :]
    emb_pad = -(-emb // 8) * 8
    z = np.concatenate([t, np.cos(f * w), -np.sin(f * w), np.zeros((l, emb_pad - emb))], axis=-1)
    z = jnp.asarray(z, F32)
    w0 = jnp.concatenate([f_w0, jnp.zeros((emb_pad - emb, fw), F32)], axis=0)
    max_decay = math.log(HY_DECAY_TARGET) / HY_MAX_DECAY_PCT
    min_decay = math.log(HY_DECAY_TARGET) / HY_MIN_DECAY_PCT
    dl = jnp.asarray(np.abs(np.linspace(min_decay, max_decay, d))[None, :], F32)
    nseg = 2 * HY_ORDER
    w3 = jnp.transpose(f_w3.reshape(fw, nseg, d), (1, 0, 2))
    w3_hi = w3.astype(BF16)
    w3_lo = (w3 - w3_hi.astype(F32)).astype(BF16)
    w3 = jnp.concatenate([w3_hi, w3_lo, w3_hi], axis=1)
    tt = min(256, l)
    td = min(512, d)
    vec = lambda a: a.reshape(1, fw)
    small = lambda shape: pl.BlockSpec(shape, lambda jd, it: (0,) * len(shape))
    return pl.pallas_call(
        functools.partial(_filter_kernel, tt=tt),
        grid=(d // td, l // tt),
        in_specs=[pl.BlockSpec((tt, emb_pad), lambda jd, it: (it, 0)),
                  small((emb_pad, fw)), small((1, fw)), small((fw, fw)), small((1, fw)),
                  small((fw, fw)), small((1, fw)), small((1, fw)),
                  pl.BlockSpec((nseg, 3 * fw, td), lambda jd, it: (0, 0, jd)),
                  pl.BlockSpec((1, td), lambda jd, it: (0, jd))],
        out_specs=[pl.BlockSpec((nseg, tt, td), lambda jd, it: (0, it, jd)),
                   pl.BlockSpec((nseg, 1, td), lambda jd, it: (0, 0, jd))],
        out_shape=[jax.ShapeDtypeStruct((nseg, l, d), F32),
                   jax.ShapeDtypeStruct((nseg, 1, d), F32)],
        compiler_params=_params("parallel", "arbitrary"),
        name="hyena_filters",
    )(z, w0, vec(f_b0), f_w1, vec(f_b1), f_w2, vec(f_b2), vec(f_freq), w3, dl)


def _dft_tables(l):
    n1 = FFT_N1
    n = 2 * l
    n2 = n // n1
    k1 = np.arange(n1)[:, None]
    f1 = np.exp(-2j * np.pi * k1 * np.arange(n1 // 2)[None, :] / n1)
    tw = np.exp(-2j * np.pi * np.arange(n2)[:, None] * np.arange(n1)[None, :] / n)
    f1r, f1i = jnp.asarray(f1.real, F32), jnp.asarray(f1.imag, F32)
    twr, twi = jnp.asarray(tw.real, F32)[:, :, None], jnp.asarray(tw.imag, F32)[:, :, None]
    fr = twr * f1r - twi * f1i
    fi = twr * f1i + twi * f1r
    ta = jnp.concatenate([jnp.concatenate([fr, -fi], axis=2),
                          jnp.concatenate([fi, fr], axis=2)], axis=1)
    fb = np.exp(-2j * np.pi * np.arange(n2)[:, None] * np.arange(n2)[None, :] / n2)
    tb = np.block([[fb.real, -fb.imag], [fb.imag, fb.real]])
    return (ta.astype(BF16), jnp.swapaxes(ta, 1, 2).astype(BF16),
            jnp.asarray(tb, BF16), jnp.asarray(tb.T, BF16))


FFT_NB = 8


def _fft_a_kernel(x_ref, t_ref, o_ref, r_s):
    planes = x_ref.shape[1]
    xt = [jnp.swapaxes(x_ref[0, c], 0, 1) for c in range(planes)]
    for jj in range(FFT_NB):
        x = jnp.concatenate([xt[c][jj] for c in range(planes)], axis=0).astype(BF16)
        r_s[jj] = jnp.dot(t_ref[jj], x, preferred_element_type=F32)
    o_ref[0] = jnp.swapaxes(r_s[...], 0, 1)


def _fft_a(x5, ta, *, n2, d, seg):
    s, p, half, _, _ = x5.shape
    kdim = p * half
    tn = min(256, d)
    sb, db = (seg * d) // tn, d // tn
    return pl.pallas_call(
        _fft_a_kernel,
        grid=(s, n2 // FFT_NB, db),
        in_specs=[pl.BlockSpec((1, p, half, FFT_NB, tn), lambda si, m, j: (si, 0, 0, m, sb + j)),
                  pl.BlockSpec((FFT_NB, 2 * FFT_N1, kdim), lambda si, m, j: (m, 0, 0))],
        out_specs=pl.BlockSpec((1, 2 * FFT_N1, FFT_NB, tn), lambda si, m, j: (si, 0, m, j)),
        out_shape=jax.ShapeDtypeStruct((s, 2 * FFT_N1, n2, d), F32),
        scratch_shapes=[pltpu.VMEM((FFT_NB, 2 * FFT_N1, tn), F32)],
        compiler_params=_params("parallel", "parallel", "parallel"),
        name="dft_stage_a",
    )(x5, ta)


def _fft_bfilt_kernel(af_ref, ab_ref, tb_ref, sc_ref, o_ref, *, n2):
    tb = tb_ref[...]
    hf = jnp.dot(tb, af_ref[0].reshape(2 * n2, -1).astype(BF16), preferred_element_type=F32)
    hb = jnp.dot(tb, ab_ref[0].reshape(2 * n2, -1).astype(BF16), preferred_element_type=F32)
    sc = sc_ref[0]
    o_ref[0, 0, 0] = (hf[0:n2] + hb[0:n2]) * sc
    o_ref[0, 1, 0] = (hf[n2:2 * n2] - hb[n2:2 * n2]) * sc


def _fft_bfilt(af, tb, scale, *, n2, d):
    tn = min(2048, d)
    blk = (1, 2, 1, n2, tn)
    return pl.pallas_call(
        functools.partial(_fft_bfilt_kernel, n2=n2),
        grid=(HY_ORDER, FFT_N1, d // tn),
        in_specs=[pl.BlockSpec(blk, lambda o, k, j: (2 * o, 0, k, 0, j)),
                  pl.BlockSpec(blk, lambda o, k, j: (2 * o + 1, 0, k, 0, j)),
                  pl.BlockSpec((2 * n2, 2 * n2), lambda o, k, j: (0, 0)),
                  pl.BlockSpec((1, 1, tn), lambda o, k, j: (o, 0, j))],
        out_specs=pl.BlockSpec(blk, lambda o, k, j: (o, 0, k, 0, j)),
        out_shape=jax.ShapeDtypeStruct((HY_ORDER, 2, FFT_N1, n2, d), F32),
        compiler_params=_params("parallel", "parallel", "parallel"),
        name="dft_stage_b_filters",
    )(af, af, tb, scale)


def _fft_b_kernel(a_ref, g_ref, tb_ref, tbt_ref, o_ref, *, n2):
    z = jnp.dot(tb_ref[...], a_ref[...].reshape(2 * n2, -1).astype(BF16), preferred_element_type=F32)
    zr, zi = z[0:n2], z[n2:2 * n2]
    gr, gi = g_ref[0, 0, 0], g_ref[0, 1, 0]
    prod = jnp.concatenate([zr * gr - zi * gi, zr * gi + zi * gr], axis=0).astype(BF16)
    y = jnp.dot(tbt_ref[...], prod, preferred_element_type=F32)
    o_ref[...] = y.reshape(o_ref.shape).astype(o_ref.dtype)


def _fft_b(a, g, tb, tbt, order, *, n2, d):
    tn = min(2048, d)
    blk = (2, 1, n2, tn)
    return pl.pallas_call(
        functools.partial(_fft_b_kernel, n2=n2),
        grid=(FFT_N1, d // tn),
        in_specs=[pl.BlockSpec(blk, lambda k, j: (0, k, 0, j)),
                  pl.BlockSpec((1,) + blk, lambda k, j: (order, 0, k, 0, j)),
                  pl.BlockSpec((2 * n2, 2 * n2), lambda k, j: (0, 0)),
                  pl.BlockSpec((2 * n2, 2 * n2), lambda k, j: (0, 0))],
        out_specs=pl.BlockSpec(blk, lambda k, j: (0, k, 0, j)),
        out_shape=jax.ShapeDtypeStruct((2, FFT_N1, n2, d), F32),
        compiler_params=_params("parallel", "parallel"),
        name="dft_stage_b_spectrum_product",
    )(a, g, tb, tbt)


def _fft_ainv_kernel(y_ref, t_ref, u_ref, gate_ref, d_ref, o_ref, r_s):
    half = o_ref.shape[1]
    yt = jnp.swapaxes(y_ref[...], 0, 1).astype(BF16)
    for c in range(2):
        r_s[c] = jnp.swapaxes(u_ref[c], 0, 1) * d_ref[...]
    for jj in range(FFT_NB):
        x = jnp.dot(t_ref[jj], yt[jj], preferred_element_type=F32)
        for c in range(2):
            r_s[c, jj] += x[c * half:(c + 1) * half]
    for c in range(2):
        o_ref[c] = gate_ref[c] * jnp.swapaxes(r_s[c], 0, 1)


def _fft_ainv(y3, tat, u4, gate4, dskip, *, n2, d, useg, gseg):
    b, half, _, _ = u4.shape
    tn = min(256, d)
    db = d // tn
    usb, gsb = (useg * d) // tn, (gseg * d) // tn
    return pl.pallas_call(
        _fft_ainv_kernel,
        grid=(n2 // FFT_NB, db),
        in_specs=[pl.BlockSpec((2 * FFT_N1, FFT_NB, tn), lambda m, j: (0, m, j)),
                  pl.BlockSpec((FFT_NB, FFT_N1, 2 * FFT_N1), lambda m, j: (m, 0, 0)),
                  pl.BlockSpec((b, half, FFT_NB, tn), lambda m, j: (0, 0, m, usb + j)),
                  pl.BlockSpec((b, half, FFT_NB, tn), lambda m, j: (0, 0, m, gsb + j)),
                  pl.BlockSpec((1, tn), lambda m, j: (0, j))],
        out_specs=pl.BlockSpec((b, half, FFT_NB, tn), lambda m, j: (0, 0, m, j)),
        out_shape=jax.ShapeDtypeStruct((b, half, n2, d), F32),
        scratch_shapes=[pltpu.VMEM((b, FFT_NB, half, tn), F32)],
        compiler_params=_params("parallel", "parallel"),
        name="dft_stage_a_inverse_gate",
    )(y3, tat, u4, gate4, dskip)


def _hyena_mixer(h, hy, l, d):
    (w_in, b_in, sconv_w, sconv_b, f_w0, f_b0, f_w1, f_b1, f_w2, f_b2, f_freq, f_w3, d_skip, w_out) = hy
    b = h.shape[0]
    assert b == 2, "the two batch rows ride the real / imaginary planes of one complex DFT"
    n1 = FFT_N1
    half = n1 // 2
    n2 = (2 * l) // n1
    assert half * n2 == l
    z3 = _mm_conv(_hy_in_kernel, h, [w_in.astype(BF16)],
                  [b_in.reshape(1, -1), sconv_w, sconv_b.reshape(1, -1)], 3 * d, F32, "hyena_in_proj_conv")
    filt, fsum = _hyena_filters(l, d, f_w0, f_b0, f_w1, f_b1, f_w2, f_b2, f_freq, f_w3)
    ta, tat, tb, tbt = _dft_tables(l)
    nseg = 2 * HY_ORDER
    af = _fft_a(filt.reshape(nseg, 1, half, n2, d), ta, n2=n2, d=d, seg=0)
    norm = (fsum[0::2] + fsum[1::2])
    g = _fft_bfilt(af.reshape(nseg, 2, n1, n2, d), tb, 1.0 / (norm * (2 * l)), n2=n2, d=d)
    z4 = z3.reshape(b, half, n2, 3 * d)
    y, yseg = z4, 0
    for o in range(HY_ORDER):
        a = _fft_a(y[None], ta, n2=n2, d=d, seg=yseg)
        yb = _fft_b(a.reshape(2, n1, n2, d), g, tb, tbt, o, n2=n2, d=d)
        y = _fft_ainv(yb.reshape(2 * n1, n2, d), tat, y, z4, d_skip[o].reshape(1, d),
                      n2=n2, d=d, useg=yseg, gseg=o + 1)
        yseg = 0
    return y.reshape(b, l, d), w_out.astype(BF16)


def _rope_tables(l):
    t = jnp.arange(l)
    row = (t // GRID_W).astype(F32)[:, None]
    col = (t % GRID_W).astype(F32)[:, None]
    cos, sin = [], []
    for dim in (A_HALF, HEAD_DIM):
        nf = dim // 4
        inv = jnp.asarray(ROPE_BASE ** (-np.arange(nf) / nf), F32)[None, :]
        ar, ac = row * inv, col * inv
        c = jnp.concatenate([jnp.cos(ar), jnp.cos(ar), jnp.cos(ac), jnp.cos(ac)], axis=1)
        s = jnp.concatenate([-jnp.sin(ar), jnp.sin(ar), -jnp.sin(ac), jnp.sin(ac)], axis=1)
        reps = HEAD_DIM // dim
        cos.append(jnp.tile(c, (1, reps)))
        sin.append(jnp.tile(s, (1, reps)))
    return jnp.stack(cos), jnp.stack(sin)


def _attn_mixer(h, hc, lam_init, w_in, w_out, a_q_g, a_k_g, lq1, lk1, lq2, lk2, a_sub_g, b_q_g, b_k_g, b_sink):
    b, l, d = h.shape
    c = hc.shape[1]
    a_heads = d // (2 * HEAD_DIM)
    b_heads = d // (2 * HEAD_DIM)
    q_cols = (a_heads + b_heads) * HEAD_DIM
    in_cols = w_in.shape[1]
    lam = (jnp.exp(jnp.sum(lq1 * lk1)) - jnp.exp(jnp.sum(lq2 * lk2)) + lam_init).reshape(1, 1)
    sink = b_sink.reshape(B_KV_HEADS, b_heads // B_KV_HEADS)

    ones = lambda n: jnp.ones((n,), F32)
    tile = lambda v, n: jnp.tile(v, n)
    gain = jnp.concatenate([tile(a_q_g, 2 * a_heads), tile(b_q_g, b_heads), tile(a_k_g, 2 * a_heads),
                            ones(a_heads * HEAD_DIM), tile(b_k_g, B_KV_HEADS),
                            ones(B_KV_HEADS * HEAD_DIM)]).reshape(1, in_cols)
    scale = jnp.concatenate([jnp.full((a_heads * HEAD_DIM,), A_HALF ** -0.5 * math.log2(math.e), F32),
                             jnp.full((b_heads * HEAD_DIM,), HEAD_DIM ** -0.5, F32),
                             ones(in_cols - q_cols)]).reshape(1, in_cols)
    e = [0, a_heads // 2, (a_heads + b_heads) // 2, (2 * a_heads + b_heads) // 2,
         (3 * a_heads + b_heads) // 2, (3 * a_heads + b_heads) // 2 + 1, (3 * a_heads + b_heads) // 2 + 2]
    chunks = [A_HALF, HEAD_DIM, A_HALF, 0, HEAD_DIM, 0]
    modes = tuple((e[s], e[s + 1], chunks[s]) for s in range(6))
    cos_tab, sin_tab = _rope_tables(l)

    def type_of(j0):
        def f(j):
            jj = j + j0
            is_a = (jj < e[1]) | ((jj >= e[2]) & (jj < e[3]))
            return jnp.where(is_a, 0, 1)
        return f

    w_bf = w_in.astype(BF16)
    p = _inproj(h.reshape(b * l, d), w_bf, gain, scale, cos_tab, sin_tab, modes=modes,
                type_of_block=type_of(0), col0=0, ncols=in_cols, l=l, rope=True).reshape(b, l, in_cols)
    kv0 = e[2]
    modes_c = tuple((lo - kv0, hi - kv0, ch) for lo, hi, ch in modes[2:])
    pc = _inproj(hc.reshape(b * c, d), w_bf, gain, scale, cos_tab, sin_tab, modes=modes_c,
                 type_of_block=type_of(kv0), col0=kv0, ncols=in_cols - q_cols, l=c, rope=False
                 ).reshape(b, c, in_cols - q_cols)
    score_bound = (A_HALF ** 0.5 * math.log2(math.e)) * jnp.max(jnp.abs(a_q_g)) * jnp.max(jnp.abs(a_k_g))
    oa = _diffattn(lam, p, pc, a_sub_g, score_bound, heads=a_heads, l=l, c=c, out_scale=1.0 - lam_init)
    ob = _winattn(sink, p, pc, a_heads=a_heads, b_heads=b_heads, l=l, c=c)
    w_out_bf = w_out.astype(BF16)
    na = a_heads * HEAD_DIM
    return [oa, ob], [w_out_bf[:na], w_out_bf[na:]]


def kernel(x, c, ctx, c_ctx, ada_w, ada_b, norm1_g, norm2_g, attn_w_in, attn_w_out, a_q_g, a_k_g, a_lam_q1, a_lam_k1, a_lam_q2, a_lam_k2, a_sub_g, b_q_g, b_k_g, b_sink, hy_w_in, hy_b_in, hy_sconv_w, hy_sconv_b, hy_f_w0, hy_f_b0, hy_f_w1, hy_f_b1, hy_f_w2, hy_f_b2, hy_f_freq, hy_f_w3, hy_d, hy_w_out, ffn_w_gate, ffn_w_val, ffn_conv_w, ffn_conv_b, ffn_w_down):
    b, l, d = x.shape
    depth = ada_w.shape[0]
    cc = jnp.concatenate([c, c_ctx[None, :], jnp.zeros((8 - b - 1, d), F32)], axis=0)
    m = _ada(cc, ada_w, ada_b)
    xs = x
    for layer in range(depth):
        i = layer // 2
        lat = [m[layer, :b, k * d:(k + 1) * d].reshape(b, 1, d) for k in range(6)]
        sh1, sc1, g1, sh2, sc2, g2 = lat
        h = _normmod(xs, norm1_g[layer], sc1, sh1)
        if layer % 2 == 0:
            mc = [jnp.broadcast_to(m[layer, b, k * d:(k + 1) * d].reshape(1, 1, d), (b, 1, d)) for k in range(2)]
            hc = _normmod(ctx, norm1_g[layer], mc[1], mc[0])
            parts, ws = _attn_mixer(h, hc, 0.8 - 0.6 * math.exp(-0.3 * layer), attn_w_in[i], attn_w_out[i],
                                    a_q_g[i], a_k_g[i], a_lam_q1[i], a_lam_k1[i], a_lam_q2[i], a_lam_k2[i],
                                    a_sub_g[i], b_q_g[i], b_k_g[i], b_sink[i])
        else:
            hy = (hy_w_in[i], hy_b_in[i], hy_sconv_w[i], hy_sconv_b[i], hy_f_w0[i], hy_f_b0[i], hy_f_w1[i],
                  hy_f_b1[i], hy_f_w2[i], hy_f_b2[i], hy_f_freq[i], hy_f_w3[i], hy_d[i], hy_w_out[i])
            y, w_o = _hyena_mixer(h, hy, l, d)
            parts, ws = [y], [w_o]
        xs = _mm_resid(parts, ws, xs, g1)
        h2 = _normmod(xs, norm2_g[layer], sc2, sh2)
        d_ff = ffn_w_gate.shape[2]
        hid = _mm_conv(_ffn_up_kernel, h2, [ffn_w_gate[layer].astype(BF16), ffn_w_val[layer].astype(BF16)],
                       [ffn_conv_w[layer], ffn_conv_b[layer].reshape(1, -1)], d_ff, BF16, "ffn_up_conv_glu")
        xs = _mm_resid([hid], [ffn_w_down[layer].astype(BF16)], xs, g2)
    return xs
```

```python
import functools
import math

import numpy as np
import jax
import jax.numpy as jnp
from jax import lax
from jax.experimental import pallas as pl
from jax.experimental.pallas import tpu as pltpu

F32 = jnp.float32
BF16 = jnp.bfloat16
HIGHEST = lax.Precision.HIGHEST

HEAD_DIM = 128
A_HALF = HEAD_DIM // 2
B_KV_HEADS = 2
GRID_W = 64
WINDOW = 128
ROPE_BASE = 10000.0
EPS = 1e-6
NEG = -1e30
HY_ORDER = 2
HY_DECAY_TARGET = 1e-2
HY_MAX_DECAY_PCT = 0.3
HY_MIN_DECAY_PCT = 1.5

LANES = 128
BF16_SUBLANES = 16
FFT_N1 = 256
VMEM_LIMIT_BYTES = 56 * 1024 * 1024


def _params(*sem):
    return pltpu.CompilerParams(dimension_semantics=sem, vmem_limit_bytes=VMEM_LIMIT_BYTES)


def _ada_kernel(c_ref, w_ref, b_ref, o_ref):
    c = c_ref[...]
    s = c * (1.0 / (1.0 + jnp.exp(-c)))
    o_ref[0] = jnp.dot(s, w_ref[0], preferred_element_type=F32, precision=HIGHEST) + b_ref[0]


def _ada(cc, ada_w, ada_b):
    depth, d, n = ada_w.shape
    tn = 1024
    return pl.pallas_call(
        _ada_kernel,
        grid=(depth, n // tn),
        in_specs=[pl.BlockSpec((8, d), lambda l, j: (0, 0)),
                  pl.BlockSpec((1, d, tn), lambda l, j: (l, 0, j)),
                  pl.BlockSpec((1, 1, tn), lambda l, j: (l, 0, j))],
        out_specs=pl.BlockSpec((1, 8, tn), lambda l, j: (l, 0, j)),
        out_shape=jax.ShapeDtypeStruct((depth, 8, n), F32),
        compiler_params=_params("parallel", "parallel"),
        name="ada_modulation",
    )(cc, ada_w, ada_b.reshape(depth, 1, n))


def _normmod_kernel(x_ref, g_ref, sc_ref, sh_ref, o_ref):
    x = x_ref[0]
    ms = jnp.mean(x * x, axis=-1, keepdims=True)
    y = x * lax.rsqrt(ms + EPS) * g_ref[...]
    o_ref[0] = (y * (1.0 + sc_ref[0]) + sh_ref[0]).astype(o_ref.dtype)


def _normmod(x, g, sc, sh):
    b, l, d = x.shape
    tm = min(512, l)
    return pl.pallas_call(
        _normmod_kernel,
        grid=(b, l // tm),
        in_specs=[pl.BlockSpec((1, tm, d), lambda bi, i: (bi, i, 0)),
                  pl.BlockSpec((1, d), lambda bi, i: (0, 0)),
                  pl.BlockSpec((1, 1, d), lambda bi, i: (bi, 0, 0)),
                  pl.BlockSpec((1, 1, d), lambda bi, i: (bi, 0, 0))],
        out_specs=pl.BlockSpec((1, tm, d), lambda bi, i: (bi, i, 0)),
        out_shape=jax.ShapeDtypeStruct((b, l, d), BF16),
        compiler_params=_params("parallel", "parallel"),
        name="rmsnorm_modulate",
    )(x, g.reshape(1, d), sc, sh)


def _head_epilogue(z, gain, scale, cos, sin, chunk, rope):
    lane = lax.broadcasted_iota(jnp.int32, z.shape, 1)
    zz = z * z
    s_all = jnp.sum(zz, axis=-1, keepdims=True)
    if chunk == HEAD_DIM:
        ms = s_all * (1.0 / HEAD_DIM)
    else:
        s_lo = jnp.sum(jnp.where(lane < A_HALF, zz, 0.0), axis=-1, keepdims=True)
        ms = jnp.where(lane < A_HALF, s_lo, s_all - s_lo) * (1.0 / A_HALF)
    y = z * lax.rsqrt(ms + EPS) * gain
    if rope:
        nf = chunk // 4
        first = (lane % (2 * nf)) < nf
        partner = jnp.where(first, pltpu.roll(y, HEAD_DIM - nf, 1), pltpu.roll(y, nf, 1))
        y = y * cos + partner * sin
    return y * scale


def _inproj_kernel(h_ref, w_ref, gain_ref, scale_ref, cos_ref, sin_ref, o_ref, *, modes, rope):
    j = pl.program_id(1)
    z = jnp.dot(h_ref[...], w_ref[...], preferred_element_type=F32)
    for lo, hi, chunk in modes:
        @pl.when((j >= lo) & (j < hi))
        def _(chunk=chunk):
            for half in range(2):
                sl = slice(half * HEAD_DIM, (half + 1) * HEAD_DIM)
                zh = z[:, sl]
                if chunk:
                    zh = _head_epilogue(zh, gain_ref[:, sl], scale_ref[:, sl],
                                        cos_ref[0], sin_ref[0], chunk, rope)
                o_ref[:, sl] = zh.astype(o_ref.dtype)


def _inproj(h2d, w, gain, scale, cos_tab, sin_tab, *, modes, type_of_block, col0, ncols, l, rope):
    m, d = h2d.shape
    tn = 2 * HEAD_DIM
    tm = min(512, l)
    pos_blocks = l // tm
    kern = functools.partial(_inproj_kernel, modes=modes, rope=rope)
    return pl.pallas_call(
        kern,
        grid=(m // tm, ncols // tn),
        in_specs=[pl.BlockSpec((tm, d), lambda i, j: (i, 0)),
                  pl.BlockSpec((d, tn), lambda i, j: (0, j + col0)),
                  pl.BlockSpec((1, tn), lambda i, j: (0, j + col0)),
                  pl.BlockSpec((1, tn), lambda i, j: (0, j + col0)),
                  pl.BlockSpec((1, tm, HEAD_DIM), lambda i, j: (type_of_block(j), i % pos_blocks, 0)),
                  pl.BlockSpec((1, tm, HEAD_DIM), lambda i, j: (type_of_block(j), i % pos_blocks, 0))],
        out_specs=pl.BlockSpec((tm, tn), lambda i, j: (i, j)),
        out_shape=jax.ShapeDtypeStruct((m, ncols), BF16),
        compiler_params=_params("parallel", "arbitrary"),
        name="attn_in_projection",
    )(h2d, w, gain, scale, cos_tab, sin_tab)


def _diffattn_kernel(lam_ref, q_ref, kc_ref, vc_ref, k_ref, v_ref, g_ref, o_ref,
                     q2_s, m_s, l_s, acc_s, *, tq, nkv, out_scale):
    kv = pl.program_id(3)

    def process(k, v):
        s = lax.dot_general(q2_s[...], k, (((1,), (1,)), ((), ())), preferred_element_type=F32)
        m_prev = m_s[...]
        m_new = jnp.maximum(m_prev, jnp.max(s, axis=-1, keepdims=True))
        alpha = jnp.exp2(m_prev - m_new)
        p = jnp.exp2(s - m_new)
        l_s[...] = alpha * l_s[...] + jnp.sum(p, axis=-1, keepdims=True)
        acc_s[...] = alpha * acc_s[...] + jnp.dot(p.astype(BF16), v, preferred_element_type=F32)
        m_s[...] = m_new

    @pl.when(kv == 0)
    def _():
        q = q_ref[0]
        lane = lax.broadcasted_iota(jnp.int32, q.shape, 1)
        zero = jnp.zeros_like(q)
        q2_s[0:tq, :] = jnp.where(lane < A_HALF, q, zero)
        q2_s[tq:2 * tq, :] = jnp.where(lane >= A_HALF, q, zero)
        m_s[...] = jnp.full(m_s.shape, -jnp.inf, F32)
        l_s[...] = jnp.zeros(l_s.shape, F32)
        acc_s[...] = jnp.zeros(acc_s.shape, F32)
        process(kc_ref[0], vc_ref[0])

    process(k_ref[0], v_ref[0])

    @pl.when(kv == nkv - 1)
    def _():
        o = acc_s[...] / l_s[...]
        d = o[0:tq] - lam_ref[0, 0] * o[tq:2 * tq]
        ms = jnp.mean(d * d, axis=-1, keepdims=True)
        o_ref[0] = (d * lax.rsqrt(ms + EPS) * g_ref[...] * out_scale).astype(o_ref.dtype)


def _diffattn_fast_kernel(lam_ref, q_ref, kc_ref, vct_ref, k_ref, vt_ref, g_ref, o_ref,
                          q2_s, l_s, acc_s, p_s, *, tq, tk, kc, nkv, out_scale):
    kv = pl.program_id(3)
    dn = (((1,), (1,)), ((), ()))

    def weights(k):
        n = k.shape[0]
        p = jnp.exp2(lax.dot_general(k, q2_s[...], dn, preferred_element_type=F32))
        return p.astype(BF16), jnp.sum(p.reshape(n // 8, 8, 2 * tq), axis=0)

    @pl.when(kv == 0)
    def _():
        q = q_ref[0]
        lane = lax.broadcasted_iota(jnp.int32, q.shape, 1)
        zero = jnp.zeros_like(q)
        q2_s[0:tq, :] = jnp.where(lane < A_HALF, q, zero)
        q2_s[tq:2 * tq, :] = jnp.where(lane >= A_HALF, q, zero)
        p, ls = weights(kc_ref[0])
        l_s[...] = ls
        acc_s[...] = jnp.dot(vct_ref[0, 0], p, preferred_element_type=F32)

    ls = l_s[...]
    for ci in range(tk // kc):
        p, lc = weights(k_ref[0, ci * kc:(ci + 1) * kc, :])
        p_s[ci * kc:(ci + 1) * kc, :] = p
        ls = ls + lc
    l_s[...] = ls
    acc_s[...] += jnp.dot(vt_ref[0, 0], p_s[...], preferred_element_type=F32)

    @pl.when(kv == nkv - 1)
    def _():
        l = jnp.sum(l_s[...], axis=0, keepdims=True)
        o = acc_s[...] / l
        d = o[:, 0:tq] - lam_ref[0, 0] * o[:, tq:2 * tq]
        ms = jnp.mean(d * d, axis=0, keepdims=True)
        y = d * lax.rsqrt(ms + EPS) * g_ref[...] * out_scale
        o_ref[0] = y.T.astype(o_ref.dtype)


def _diffattn_fast(lam, p, pc, sub_g, *, heads, l, c, out_scale):
    b = p.shape[0]
    tq = min(512, l)
    tk = min(4096, l)
    kc = min(512, tk)
    nkv = l // tk
    k_blk = 2 * heads
    hd = heads * HEAD_DIM
    vt = jnp.transpose(p[:, :, 3 * hd:4 * hd].reshape(b, l, heads, HEAD_DIM), (0, 2, 3, 1))
    vct = jnp.transpose(pc[:, :, hd:2 * hd].reshape(b, c, heads, HEAD_DIM), (0, 2, 3, 1))
    kern = functools.partial(_diffattn_fast_kernel, tq=tq, tk=tk, kc=kc, nkv=nkv, out_scale=out_scale)
    return pl.pallas_call(
        kern,
        grid=(b, heads, l // tq, nkv),
        in_specs=[pl.BlockSpec(memory_space=pltpu.SMEM),
                  pl.BlockSpec((1, tq, HEAD_DIM), lambda bi, h, i, kv: (bi, i, h)),
                  pl.BlockSpec((1, c, HEAD_DIM), lambda bi, h, i, kv: (bi, 0, h)),
                  pl.BlockSpec((1, 1, HEAD_DIM, c), lambda bi, h, i, kv: (bi, h, 0, 0)),
                  pl.BlockSpec((1, tk, HEAD_DIM), lambda bi, h, i, kv: (bi, kv, k_blk + h)),
                  pl.BlockSpec((1, 1, HEAD_DIM, tk), lambda bi, h, i, kv: (bi, h, 0, kv)),
                  pl.BlockSpec((HEAD_DIM, 1), lambda bi, h, i, kv: (0, 0))],
        out_specs=pl.BlockSpec((1, tq, HEAD_DIM), lambda bi, h, i, kv: (bi, i, h)),
        out_shape=jax.ShapeDtypeStruct((b, l, hd), BF16),
        scratch_shapes=[pltpu.VMEM((2 * tq, HEAD_DIM), BF16),
                        pltpu.VMEM((8, 2 * tq), F32),
                        pltpu.VMEM((HEAD_DIM, 2 * tq), F32),
                        pltpu.VMEM((tk, 2 * tq), BF16)],
        compiler_params=_params("parallel", "parallel", "parallel", "arbitrary"),
        name="diff_attention_bounded",
    )(lam, p, pc, vct, p, vt, sub_g.reshape(HEAD_DIM, 1))


SCORE_BOUND_LOG2 = 60.0


def _diffattn(lam, p, pc, sub_g, score_bound, *, heads, l, c, out_scale):
    kw = dict(heads=heads, l=l, c=c, out_scale=out_scale)
    return lax.cond(score_bound < SCORE_BOUND_LOG2,
                    lambda *a: _diffattn_fast(*a, **kw), lambda *a: _diffattn_safe(*a, **kw),
                    lam, p, pc, sub_g)


def _diffattn_safe(lam, p, pc, sub_g, *, heads, l, c, out_scale):
    b = p.shape[0]
    tq = min(256, l)
    tk = min(512, l)
    nkv = l // tk
    q_blk = 0
    k_blk = 2 * heads
    v_blk = 3 * heads
    kern = functools.partial(_diffattn_kernel, tq=tq, nkv=nkv, out_scale=out_scale)
    return pl.pallas_call(
        kern,
        grid=(b, heads, l // tq, nkv),
        in_specs=[pl.BlockSpec(memory_space=pltpu.SMEM),
                  pl.BlockSpec((1, tq, HEAD_DIM), lambda bi, h, i, kv: (bi, i, q_blk + h)),
                  pl.BlockSpec((1, c, HEAD_DIM), lambda bi, h, i, kv: (bi, 0, h)),
                  pl.BlockSpec((1, c, HEAD_DIM), lambda bi, h, i, kv: (bi, 0, heads + h)),
                  pl.BlockSpec((1, tk, HEAD_DIM), lambda bi, h, i, kv: (bi, kv, k_blk + h)),
                  pl.BlockSpec((1, tk, HEAD_DIM), lambda bi, h, i, kv: (bi, kv, v_blk + h)),
                  pl.BlockSpec((1, HEAD_DIM), lambda bi, h, i, kv: (0, 0))],
        out_specs=pl.BlockSpec((1, tq, HEAD_DIM), lambda bi, h, i, kv: (bi, i, h)),
        out_shape=jax.ShapeDtypeStruct((b, l, heads * HEAD_DIM), BF16),
        scratch_shapes=[pltpu.VMEM((2 * tq, HEAD_DIM), BF16),
                        pltpu.VMEM((2 * tq, 1), F32),
                        pltpu.VMEM((2 * tq, 1), F32),
                        pltpu.VMEM((2 * tq, HEAD_DIM), F32)],
        compiler_params=_params("parallel", "parallel", "parallel", "arbitrary"),
        name="diff_attention",
    )(lam, p, pc, pc, p, p, sub_g.reshape(1, HEAD_DIM))


def _winattn_kernel(sink_ref, q_ref, kp_ref, km_ref, kn_ref, vp_ref, vm_ref, vn_ref, kc_ref, vc_ref,
                    o_ref, *, tq, l, group):
    g = pl.program_id(1)
    i = pl.program_id(2)
    kband = jnp.concatenate([kp_ref[0], km_ref[0], kn_ref[0]], axis=0)
    vband = jnp.concatenate([vp_ref[0], vm_ref[0], vn_ref[0]], axis=0)
    nk = tq + 2 * WINDOW
    qpos = i * tq + lax.broadcasted_iota(jnp.int32, (tq, nk), 0)
    kpos = i * tq - WINDOW + lax.broadcasted_iota(jnp.int32, (tq, nk), 1)
    valid = (jnp.abs(kpos - qpos) <= WINDOW) & (kpos >= 0) & (kpos < l)
    kc = kc_ref[0]
    vc = vc_ref[0]
    dn = (((1,), (1,)), ((), ()))
    for r in range(group):
        sl = slice(r * HEAD_DIM, (r + 1) * HEAD_DIM)
        q = q_ref[0, :, sl]
        s_lat = jnp.where(valid, lax.dot_general(q, kband, dn, preferred_element_type=F32), NEG)
        s_ctx = lax.dot_general(q, kc, dn, preferred_element_type=F32)
        sk = sink_ref[g, r]
        m = jnp.maximum(jnp.maximum(jnp.max(s_lat, axis=-1, keepdims=True),
                                    jnp.max(s_ctx, axis=-1, keepdims=True)), sk)
        p_lat = jnp.exp(s_lat - m)
        p_ctx = jnp.exp(s_ctx - m)
        denom = (jnp.sum(p_lat, axis=-1, keepdims=True) + jnp.sum(p_ctx, axis=-1, keepdims=True)
                 + jnp.exp(sk - m))
        o = (jnp.dot(p_lat.astype(BF16), vband, preferred_element_type=F32)
             + jnp.dot(p_ctx.astype(BF16), vc, preferred_element_type=F32))
        o_ref[0, :, sl] = (o / denom).astype(o_ref.dtype)


def _winattn(sink, p, pc, *, a_heads, b_heads, l, c):
    b = p.shape[0]
    group = b_heads // B_KV_HEADS
    tq = min(256, l)
    wpb = tq // WINDOW
    nwb = l // WINDOW
    gw = group * HEAD_DIM
    q_blk = a_heads * HEAD_DIM // gw
    k_blk = (a_heads + b_heads) + 2 * a_heads
    v_blk = k_blk + B_KV_HEADS
    kc_blk = 2 * a_heads
    vc_blk = kc_blk + B_KV_HEADS
    kern = functools.partial(_winattn_kernel, tq=tq, l=l, group=group)

    def prev_map(col):
        return lambda bi, g, i: (bi, jnp.maximum(i * wpb - 1, 0), col + g)

    def main_map(col):
        return lambda bi, g, i: (bi, i, col + g)

    def next_map(col):
        return lambda bi, g, i: (bi, jnp.minimum((i + 1) * wpb, nwb - 1), col + g)

    return pl.pallas_call(
        kern,
        grid=(b, B_KV_HEADS, l // tq),
        in_specs=[pl.BlockSpec(memory_space=pltpu.SMEM),
                  pl.BlockSpec((1, tq, gw), lambda bi, g, i: (bi, i, q_blk + g)),
                  pl.BlockSpec((1, WINDOW, HEAD_DIM), prev_map(k_blk)),
                  pl.BlockSpec((1, tq, HEAD_DIM), main_map(k_blk)),
                  pl.BlockSpec((1, WINDOW, HEAD_DIM), next_map(k_blk)),
                  pl.BlockSpec((1, WINDOW, HEAD_DIM), prev_map(v_blk)),
                  pl.BlockSpec((1, tq, HEAD_DIM), main_map(v_blk)),
                  pl.BlockSpec((1, WINDOW, HEAD_DIM), next_map(v_blk)),
                  pl.BlockSpec((1, c, HEAD_DIM), lambda bi, g, i: (bi, 0, kc_blk + g)),
                  pl.BlockSpec((1, c, HEAD_DIM), lambda bi, g, i: (bi, 0, vc_blk + g))],
        out_specs=pl.BlockSpec((1, tq, gw), lambda bi, g, i: (bi, i, g)),
        out_shape=jax.ShapeDtypeStruct((b, l, b_heads * HEAD_DIM), BF16),
        compiler_params=_params("parallel", "parallel", "parallel"),
        name="window_attention",
    )(sink, p, p, p, p, p, p, p, pc, pc)


def _mm_resid_kernel(*refs, nparts):
    a_refs = refs[0:nparts]
    w_refs = refs[nparts:2 * nparts]
    res_ref, gate_ref, o_ref = refs[2 * nparts:]
    y = jnp.dot(a_refs[0][0].astype(BF16), w_refs[0][...], preferred_element_type=F32)
    for a_ref, w_ref in zip(a_refs[1:], w_refs[1:]):
        y = y + jnp.dot(a_ref[0].astype(BF16), w_ref[...], preferred_element_type=F32)
    o_ref[0] = res_ref[0] + gate_ref[0] * y


def _mm_resid(parts, ws, res, gate):
    b, l, n = res.shape
    tm = min(512, l)
    tn = min(512, n)
    nparts = len(parts)
    in_specs = ([pl.BlockSpec((1, tm, a.shape[2]), lambda bi, i, j: (bi, i, 0)) for a in parts]
                + [pl.BlockSpec((w.shape[0], tn), lambda bi, i, j: (0, j)) for w in ws]
                + [pl.BlockSpec((1, tm, tn), lambda bi, i, j: (bi, i, j)),
                   pl.BlockSpec((1, 1, tn), lambda bi, i, j: (bi, 0, j))])
    return pl.pallas_call(
        functools.partial(_mm_resid_kernel, nparts=nparts),
        grid=(b, l // tm, n // tn),
        in_specs=in_specs,
        out_specs=pl.BlockSpec((1, tm, tn), lambda bi, i, j: (bi, i, j)),
        out_shape=jax.ShapeDtypeStruct((b, l, n), F32),
        compiler_params=_params("parallel", "parallel", "arbitrary"),
        name="out_projection_gated_residual",
    )(*parts, *ws, res, gate)


HALO = BF16_SUBLANES


def _fill_rows(hbuf, hp_ref, hm_ref, hn_ref, tm):
    hbuf[0:HALO, :] = hp_ref[0]
    hbuf[HALO:HALO + tm, :] = hm_ref[0]
    hbuf[HALO + tm:2 * HALO + tm, :] = hn_ref[0]


def _conv3(zbuf, cw_ref, cb_ref, tm, first, last):
    @pl.when(first)
    def _():
        zbuf[HALO - 1:HALO, :] = jnp.zeros((1, zbuf.shape[1]), F32)

    @pl.when(last)
    def _():
        zbuf[HALO + tm:HALO + tm + 1, :] = jnp.zeros((1, zbuf.shape[1]), F32)

    return (zbuf[HALO - 1:HALO - 1 + tm, :] * cw_ref[0:1, :] + zbuf[HALO:HALO + tm, :] * cw_ref[1:2, :]
            + zbuf[HALO + 1:HALO + 1 + tm, :] * cw_ref[2:3, :] + cb_ref[...])


def _gelu_tanh(x):
    return 0.5 * x * (1.0 + jnp.tanh(math.sqrt(2.0 / math.pi) * (x + 0.044715 * (x * x * x))))


def _ffn_up_kernel(hp_ref, hm_ref, hn_ref, wg_ref, wv_ref, cw_ref, cb_ref, o_ref, hbuf, zbuf, *, tm, nt):
    i = pl.program_id(1)
    j = pl.program_id(2)

    @pl.when(j == 0)
    def _():
        _fill_rows(hbuf, hp_ref, hm_ref, hn_ref, tm)

    zbuf[...] = jnp.dot(hbuf[...], wg_ref[...], preferred_element_type=F32)
    g = _conv3(zbuf, cw_ref, cb_ref, tm, i == 0, i == nt - 1)
    v = jnp.dot(hbuf[HALO:HALO + tm, :], wv_ref[...], preferred_element_type=F32)
    o_ref[0] = (_gelu_tanh(g) * v).astype(o_ref.dtype)


def _hy_in_kernel(hp_ref, hm_ref, hn_ref, w_ref, b_ref, cw_ref, cb_ref, o_ref, hbuf, zbuf, *, tm, nt):
    i = pl.program_id(1)
    j = pl.program_id(2)

    @pl.when(j == 0)
    def _():
        _fill_rows(hbuf, hp_ref, hm_ref, hn_ref, tm)

    zbuf[...] = jnp.dot(hbuf[...], w_ref[...], preferred_element_type=F32) + b_ref[...]
    o_ref[0] = _conv3(zbuf, cw_ref, cb_ref, tm, i == 0, i == nt - 1).astype(o_ref.dtype)


def _mm_conv(kernel, h, ws, vecs, n, out_dtype, name):
    b, l, k = h.shape
    tm = min(512, l)
    tn = min(512, n)
    nt = l // tm
    hpb = tm // HALO
    nhb = l // HALO
    in_specs = ([pl.BlockSpec((1, HALO, k), lambda bi, i, j: (bi, jnp.maximum(i * hpb - 1, 0), 0)),
                 pl.BlockSpec((1, tm, k), lambda bi, i, j: (bi, i, 0)),
                 pl.BlockSpec((1, HALO, k), lambda bi, i, j: (bi, jnp.minimum((i + 1) * hpb, nhb - 1), 0))]
                + [pl.BlockSpec((k, tn), lambda bi, i, j: (0, j)) for _ in ws]
                + [pl.BlockSpec((v.shape[0], tn), lambda bi, i, j: (0, j)) for v in vecs])
    return pl.pallas_call(
        functools.partial(kernel, tm=tm, nt=nt),
        grid=(b, nt, n // tn),
        in_specs=in_specs,
        out_specs=pl.BlockSpec((1, tm, tn), lambda bi, i, j: (bi, i, j)),
        out_shape=jax.ShapeDtypeStruct((b, l, n), out_dtype),
        scratch_shapes=[pltpu.VMEM((tm + 2 * HALO, k), BF16),
                        pltpu.VMEM((tm + 2 * HALO, tn), F32)],
        compiler_params=_params("parallel", "parallel", "arbitrary"),
        name=name,
    )(h, h, h, *ws, *vecs)


FILT_NB = BF16_SUBLANES


def _filter_dft_kernel(z_ref, w0_ref, b0_ref, w1_ref, b1_ref, w2_ref, b2_ref, fr_ref, w3_ref, dl_ref, t_ref,
                       o_ref, s_ref, a3_s, r_s):
    m, seg, j = pl.program_id(0), pl.program_id(1), pl.program_id(2)
    tn = o_ref.shape[3]
    fr = fr_ref[...]

    def layer(a, w_ref, b_ref):
        return jnp.sin(fr * (jnp.dot(a, w_ref[...], preferred_element_type=F32, precision=HIGHEST)
                             + b_ref[...]))

    @pl.when((seg == 0) & (j == 0))
    def _():
        for jj in range(FILT_NB):
            a = layer(layer(layer(z_ref[jj], w0_ref, b0_ref), w1_ref, b1_ref), w2_ref, b2_ref)
            a_hi = a.astype(BF16)
            a_lo = (a - a_hi.astype(F32)).astype(BF16)
            a3_s[jj] = jnp.concatenate([a_hi, a_hi, a_lo], axis=1)

    @pl.when((m == 0) & (seg == 0) & (j == 0))
    def _():
        s_ref[...] = jnp.zeros(s_ref.shape, F32)

    dl = dl_ref[...]
    backward = seg % 2 == 1
    abs_sum = jnp.zeros((1, tn), F32)
    for jj in range(FILT_NB):
        h = (jnp.dot(a3_s[jj], w3_ref[0], preferred_element_type=F32)
             * jnp.exp(-z_ref[jj][:, 0:1] * dl))
        if jj == 0:
            row = lax.broadcasted_iota(jnp.int32, h.shape, 0)
            h = jnp.where(backward & (m == 0) & (row == 0), 0.0, h)
        abs_sum = abs_sum + jnp.sum(jnp.abs(h), axis=0, keepdims=True)
        r_s[jj] = jnp.dot(t_ref[jj], h.astype(BF16), preferred_element_type=F32)
    o_ref[0] = jnp.swapaxes(r_s[...], 0, 1).astype(o_ref.dtype)
    col = pl.multiple_of(j * tn, tn)
    s_ref[seg, :, pl.ds(col, tn)] += abs_sum


def _hyena_filter_spectra_a(l, d, ta, f_w0, f_b0, f_w1, f_b1, f_w2, f_b2, f_freq, f_w3):
    emb, fw = f_w0.shape
    bands = (emb - 1) // 2
    half = FFT_N1 // 2
    n2 = l // half
    t = np.linspace(0.0, 1.0, l)[:, None]
    w = 2.0 * math.pi * np.arange(l)[:, None] / l
    f = np.linspace(1e-4, bands - 1, bands)[None, :]
    emb_pad = -(-emb // 8) * 8
    z = np.concatenate([t, np.cos(f * w), -np.sin(f * w), np.zeros((l, emb_pad - emb))], axis=-1)
    z = jnp.asarray(z.reshape(half, n2, emb_pad).transpose(1, 0, 2), F32)
    w0 = jnp.concatenate([f_w0, jnp.zeros((emb_pad - emb, fw), F32)], axis=0)
    max_decay = math.log(HY_DECAY_TARGET) / HY_MAX_DECAY_PCT
    min_decay = math.log(HY_DECAY_TARGET) / HY_MIN_DECAY_PCT
    dl = jnp.asarray(np.abs(np.linspace(min_decay, max_decay, d))[None, :], F32)
    nseg = 2 * HY_ORDER
    w3 = jnp.transpose(f_w3.reshape(fw, nseg, d), (1, 0, 2))
    w3_hi = w3.astype(BF16)
    w3_lo = (w3 - w3_hi.astype(F32)).astype(BF16)
    w3 = jnp.concatenate([w3_hi, w3_lo, w3_hi], axis=1)
    nb = min(FILT_NB, n2)
    assert nb == FILT_NB
    tn = min(256, d)
    vec = lambda a: a.reshape(1, fw)
    small = lambda shape: pl.BlockSpec(shape, lambda m, sg, j: (0,) * len(shape))
    return pl.pallas_call(
        _filter_dft_kernel,
        grid=(n2 // nb, nseg, d // tn),
        in_specs=[pl.BlockSpec((nb, half, emb_pad), lambda m, sg, j: (m, 0, 0)),
                  small((emb_pad, fw)), small((1, fw)), small((fw, fw)), small((1, fw)),
                  small((fw, fw)), small((1, fw)), small((1, fw)),
                  pl.BlockSpec((1, 3 * fw, tn), lambda m, sg, j: (sg, 0, j)),
                  pl.BlockSpec((1, tn), lambda m, sg, j: (0, j)),
                  pl.BlockSpec((nb, 2 * FFT_N1, half), lambda m, sg, j: (m, 0, 0))],
        out_specs=[pl.BlockSpec((1, 2 * FFT_N1, nb, tn), lambda m, sg, j: (sg, 0, m, j)),
                   pl.BlockSpec((nseg, 1, d), lambda m, sg, j: (0, 0, 0))],
        out_shape=[jax.ShapeDtypeStruct((nseg, 2 * FFT_N1, n2, d), BF16),
                   jax.ShapeDtypeStruct((nseg, 1, d), F32)],
        scratch_shapes=[pltpu.VMEM((nb, half, 3 * fw), BF16),
                        pltpu.VMEM((nb, 2 * FFT_N1, tn), F32)],
        compiler_params=_params("arbitrary", "arbitrary", "arbitrary"),
        name="hyena_filter_dft_stage_a",
    )(z, w0, vec(f_b0), f_w1, vec(f_b1), f_w2, vec(f_b2), vec(f_freq), w3, dl, ta)


def _dft_tables(l):
    n1 = FFT_N1
    n = 2 * l
    n2 = n // n1
    k1 = np.arange(n1)[:, None]
    f1 = np.exp(-2j * np.pi * k1 * np.arange(n1 // 2)[None, :] / n1)
    tw = np.exp(-2j * np.pi * np.arange(n2)[:, None] * np.arange(n1)[None, :] / n)
    f1r, f1i = jnp.asarray(f1.real, F32), jnp.asarray(f1.imag, F32)
    twr, twi = jnp.asarray(tw.real, F32)[:, :, None], jnp.asarray(tw.imag, F32)[:, :, None]
    fr = twr * f1r - twi * f1i
    fi = twr * f1i + twi * f1r
    ta = jnp.concatenate([jnp.concatenate([fr, -fi], axis=2),
                          jnp.concatenate([fi, fr], axis=2)], axis=1)
    fb = np.exp(-2j * np.pi * np.arange(n2)[:, None] * np.arange(n2)[None, :] / n2)
    tb = np.block([[fb.real, -fb.imag], [fb.imag, fb.real]])
    return (ta.astype(BF16), jnp.swapaxes(ta, 1, 2).astype(BF16),
            jnp.asarray(tb, BF16), jnp.asarray(tb.T, BF16))


FFT_NB = BF16_SUBLANES


def _fft_a_kernel(x_ref, t_ref, o_ref, r_s):
    planes = x_ref.shape[1]
    xt = [jnp.swapaxes(x_ref[0, c], 0, 1) for c in range(planes)]
    for jj in range(FFT_NB):
        x = jnp.concatenate([xt[c][jj] for c in range(planes)], axis=0).astype(BF16)
        r_s[jj] = jnp.dot(t_ref[jj], x, preferred_element_type=F32)
    o_ref[0] = jnp.swapaxes(r_s[...], 0, 1).astype(o_ref.dtype)


def _fft_a(x5, ta, *, n2, d, seg):
    s, p, half, _, _ = x5.shape
    kdim = p * half
    tn = min(256, d)
    sb, db = (seg * d) // tn, d // tn
    return pl.pallas_call(
        _fft_a_kernel,
        grid=(s, n2 // FFT_NB, db),
        in_specs=[pl.BlockSpec((1, p, half, FFT_NB, tn), lambda si, m, j: (si, 0, 0, m, sb + j)),
                  pl.BlockSpec((FFT_NB, 2 * FFT_N1, kdim), lambda si, m, j: (m, 0, 0))],
        out_specs=pl.BlockSpec((1, 2 * FFT_N1, FFT_NB, tn), lambda si, m, j: (si, 0, m, j)),
        out_shape=jax.ShapeDtypeStruct((s, 2 * FFT_N1, n2, d), BF16),
        scratch_shapes=[pltpu.VMEM((FFT_NB, 2 * FFT_N1, tn), F32)],
        compiler_params=_params("parallel", "parallel", "parallel"),
        name="dft_stage_a",
    )(x5, ta)


def _fft_bfilt_kernel(af_ref, ab_ref, tb_ref, sc_ref, o_ref, *, n2):
    tb = tb_ref[...]
    hf = jnp.dot(tb, af_ref[0].reshape(2 * n2, -1).astype(BF16), preferred_element_type=F32)
    hb = jnp.dot(tb, ab_ref[0].reshape(2 * n2, -1).astype(BF16), preferred_element_type=F32)
    sc = sc_ref[0]
    o_ref[0, 0, 0] = (hf[0:n2] + hb[0:n2]) * sc
    o_ref[0, 1, 0] = (hf[n2:2 * n2] - hb[n2:2 * n2]) * sc


def _fft_bfilt(af, tb, scale, *, n2, d):
    tn = min(2048, d)
    blk = (1, 2, 1, n2, tn)
    return pl.pallas_call(
        functools.partial(_fft_bfilt_kernel, n2=n2),
        grid=(HY_ORDER, FFT_N1, d // tn),
        in_specs=[pl.BlockSpec(blk, lambda o, k, j: (2 * o, 0, k, 0, j)),
                  pl.BlockSpec(blk, lambda o, k, j: (2 * o + 1, 0, k, 0, j)),
                  pl.BlockSpec((2 * n2, 2 * n2), lambda o, k, j: (0, 0)),
                  pl.BlockSpec((1, 1, tn), lambda o, k, j: (o, 0, j))],
        out_specs=pl.BlockSpec(blk, lambda o, k, j: (o, 0, k, 0, j)),
        out_shape=jax.ShapeDtypeStruct((HY_ORDER, 2, FFT_N1, n2, d), F32),
        compiler_params=_params("parallel", "parallel", "parallel"),
        name="dft_stage_b_filters",
    )(af, af, tb, scale)


def _fft_b_kernel(a_ref, g_ref, tb_ref, tbt_ref, o_ref, *, n2):
    z = jnp.dot(tb_ref[...], a_ref[...].reshape(2 * n2, -1).astype(BF16), preferred_element_type=F32)
    zr, zi = z[0:n2], z[n2:2 * n2]
    gr, gi = g_ref[0, 0, 0], g_ref[0, 1, 0]
    prod = jnp.concatenate([zr * gr - zi * gi, zr * gi + zi * gr], axis=0).astype(BF16)
    y = jnp.dot(tbt_ref[...], prod, preferred_element_type=F32)
    o_ref[...] = y.reshape(o_ref.shape).astype(o_ref.dtype)


def _fft_b(a, g, tb, tbt, order, *, n2, d):
    tn = min(2048, d)
    blk = (2, 1, n2, tn)
    return pl.pallas_call(
        functools.partial(_fft_b_kernel, n2=n2),
        grid=(FFT_N1, d // tn),
        in_specs=[pl.BlockSpec(blk, lambda k, j: (0, k, 0, j)),
                  pl.BlockSpec((1,) + blk, lambda k, j: (order, 0, k, 0, j)),
                  pl.BlockSpec((2 * n2, 2 * n2), lambda k, j: (0, 0)),
                  pl.BlockSpec((2 * n2, 2 * n2), lambda k, j: (0, 0))],
        out_specs=pl.BlockSpec(blk, lambda k, j: (0, k, 0, j)),
        out_shape=jax.ShapeDtypeStruct((2, FFT_N1, n2, d), BF16),
        compiler_params=_params("parallel", "parallel"),
        name="dft_stage_b_spectrum_product",
    )(a, g, tb, tbt)


def _fft_ainv_kernel(y_ref, t_ref, u_ref, gate_ref, d_ref, o_ref, r_s):
    half = o_ref.shape[1]
    yt = jnp.swapaxes(y_ref[...].astype(F32), 0, 1).astype(BF16)
    for c in range(2):
        r_s[c] = jnp.swapaxes(u_ref[c], 0, 1) * d_ref[...]
    for jj in range(FFT_NB):
        x = jnp.dot(t_ref[jj], yt[jj], preferred_element_type=F32)
        for c in range(2):
            r_s[c, jj] += x[c * half:(c + 1) * half]
    for c in range(2):
        o_ref[c] = gate_ref[c] * jnp.swapaxes(r_s[c], 0, 1)


def _fft_ainv(y3, tat, u4, gate4, dskip, *, n2, d, useg, gseg):
    b, half, _, _ = u4.shape
    tn = min(128, d)
    db = d // tn
    usb, gsb = (useg * d) // tn, (gseg * d) // tn
    return pl.pallas_call(
        _fft_ainv_kernel,
        grid=(n2 // FFT_NB, db),
        in_specs=[pl.BlockSpec((2 * FFT_N1, FFT_NB, tn), lambda m, j: (0, m, j)),
                  pl.BlockSpec((FFT_NB, FFT_N1, 2 * FFT_N1), lambda m, j: (m, 0, 0)),
                  pl.BlockSpec((b, half, FFT_NB, tn), lambda m, j: (0, 0, m, usb + j)),
                  pl.BlockSpec((b, half, FFT_NB, tn), lambda m, j: (0, 0, m, gsb + j)),
                  pl.BlockSpec((1, tn), lambda m, j: (0, j))],
        out_specs=pl.BlockSpec((b, half, FFT_NB, tn), lambda m, j: (0, 0, m, j)),
        out_shape=jax.ShapeDtypeStruct((b, half, n2, d), F32),
        scratch_shapes=[pltpu.VMEM((b, FFT_NB, half, tn), F32)],
        compiler_params=_params("parallel", "parallel"),
        name="dft_stage_a_inverse_gate",
    )(y3, tat, u4, gate4, dskip)


def _hyena_mixer(h, hy, l, d):
    (w_in, b_in, sconv_w, sconv_b, f_w0, f_b0, f_w1, f_b1, f_w2, f_b2, f_freq, f_w3, d_skip, w_out) = hy
    b = h.shape[0]
    assert b == 2, "the two batch rows ride the real / imaginary planes of one complex DFT"
    n1 = FFT_N1
    half = n1 // 2
    n2 = (2 * l) // n1
    assert half * n2 == l
    z3 = _mm_conv(_hy_in_kernel, h, [w_in.astype(BF16)],
                  [b_in.reshape(1, -1), sconv_w, sconv_b.reshape(1, -1)], 3 * d, F32, "hyena_in_proj_conv")
    ta, tat, tb, tbt = _dft_tables(l)
    nseg = 2 * HY_ORDER
    af, fsum = _hyena_filter_spectra_a(l, d, ta, f_w0, f_b0, f_w1, f_b1, f_w2, f_b2, f_freq, f_w3)
    norm = (fsum[0::2] + fsum[1::2])
    g = _fft_bfilt(af.reshape(nseg, 2, n1, n2, d), tb, 1.0 / (norm * (2 * l)), n2=n2, d=d)
    z4 = z3.reshape(b, half, n2, 3 * d)
    y, yseg = z4, 0
    for o in range(HY_ORDER):
        a = _fft_a(y[None], ta, n2=n2, d=d, seg=yseg)
        yb = _fft_b(a.reshape(2, n1, n2, d), g, tb, tbt, o, n2=n2, d=d)
        y = _fft_ainv(yb.reshape(2 * n1, n2, d), tat, y, z4, d_skip[o].reshape(1, d),
                      n2=n2, d=d, useg=yseg, gseg=o + 1)
        yseg = 0
    return y.reshape(b, l, d), w_out.astype(BF16)


def _rope_tables(l):
    t = jnp.arange(l)
    row = (t // GRID_W).astype(F32)[:, None]
    col = (t % GRID_W).astype(F32)[:, None]
    cos, sin = [], []
    for dim in (A_HALF, HEAD_DIM):
        nf = dim // 4
        inv = jnp.asarray(ROPE_BASE ** (-np.arange(nf) / nf), F32)[None, :]
        ar, ac = row * inv, col * inv
        c = jnp.concatenate([jnp.cos(ar), jnp.cos(ar), jnp.cos(ac), jnp.cos(ac)], axis=1)
        s = jnp.concatenate([-jnp.sin(ar), jnp.sin(ar), -jnp.sin(ac), jnp.sin(ac)], axis=1)
        reps = HEAD_DIM // dim
        cos.append(jnp.tile(c, (1, reps)))
        sin.append(jnp.tile(s, (1, reps)))
    return jnp.stack(cos), jnp.stack(sin)


def _attn_mixer(h, hc, lam_init, w_in, w_out, a_q_g, a_k_g, lq1, lk1, lq2, lk2, a_sub_g, b_q_g, b_k_g, b_sink):
    b, l, d = h.shape
    c = hc.shape[1]
    a_heads = d // (2 * HEAD_DIM)
    b_heads = d // (2 * HEAD_DIM)
    q_cols = (a_heads + b_heads) * HEAD_DIM
    in_cols = w_in.shape[1]
    lam = (jnp.exp(jnp.sum(lq1 * lk1)) - jnp.exp(jnp.sum(lq2 * lk2)) + lam_init).reshape(1, 1)
    sink = b_sink.reshape(B_KV_HEADS, b_heads // B_KV_HEADS)

    ones = lambda n: jnp.ones((n,), F32)
    tile = lambda v, n: jnp.tile(v, n)
    gain = jnp.concatenate([tile(a_q_g, 2 * a_heads), tile(b_q_g, b_heads), tile(a_k_g, 2 * a_heads),
                            ones(a_heads * HEAD_DIM), tile(b_k_g, B_KV_HEADS),
                            ones(B_KV_HEADS * HEAD_DIM)]).reshape(1, in_cols)
    scale = jnp.concatenate([jnp.full((a_heads * HEAD_DIM,), A_HALF ** -0.5 * math.log2(math.e), F32),
                             jnp.full((b_heads * HEAD_DIM,), HEAD_DIM ** -0.5, F32),
                             ones(in_cols - q_cols)]).reshape(1, in_cols)
    e = [0, a_heads // 2, (a_heads + b_heads) // 2, (2 * a_heads + b_heads) // 2,
         (3 * a_heads + b_heads) // 2, (3 * a_heads + b_heads) // 2 + 1, (3 * a_heads + b_heads) // 2 + 2]
    chunks = [A_HALF, HEAD_DIM, A_HALF, 0, HEAD_DIM, 0]
    modes = tuple((e[s], e[s + 1], chunks[s]) for s in range(6))
    cos_tab, sin_tab = _rope_tables(l)

    def type_of(j0):
        def f(j):
            jj = j + j0
            is_a = (jj < e[1]) | ((jj >= e[2]) & (jj < e[3]))
            return jnp.where(is_a, 0, 1)
        return f

    w_bf = w_in.astype(BF16)
    p = _inproj(h.reshape(b * l, d), w_bf, gain, scale, cos_tab, sin_tab, modes=modes,
                type_of_block=type_of(0), col0=0, ncols=in_cols, l=l, rope=True).reshape(b, l, in_cols)
    kv0 = e[2]
    modes_c = tuple((lo - kv0, hi - kv0, ch) for lo, hi, ch in modes[2:])
    pc = _inproj(hc.reshape(b * c, d), w_bf, gain, scale, cos_tab, sin_tab, modes=modes_c,
                 type_of_block=type_of(kv0), col0=kv0, ncols=in_cols - q_cols, l=c, rope=False
                 ).reshape(b, c, in_cols - q_cols)
    score_bound = (A_HALF ** 0.5 * math.log2(math.e)) * jnp.max(jnp.abs(a_q_g)) * jnp.max(jnp.abs(a_k_g))
    oa = _diffattn(lam, p, pc, a_sub_g, score_bound, heads=a_heads, l=l, c=c, out_scale=1.0 - lam_init)
    ob = _winattn(sink, p, pc, a_heads=a_heads, b_heads=b_heads, l=l, c=c)
    w_out_bf = w_out.astype(BF16)
    na = a_heads * HEAD_DIM
    return [oa, ob], [w_out_bf[:na], w_out_bf[na:]]


def kernel(x, c, ctx, c_ctx, ada_w, ada_b, norm1_g, norm2_g, attn_w_in, attn_w_out, a_q_g, a_k_g, a_lam_q1, a_lam_k1, a_lam_q2, a_lam_k2, a_sub_g, b_q_g, b_k_g, b_sink, hy_w_in, hy_b_in, hy_sconv_w, hy_sconv_b, hy_f_w0, hy_f_b0, hy_f_w1, hy_f_b1, hy_f_w2, hy_f_b2, hy_f_freq, hy_f_w3, hy_d, hy_w_out, ffn_w_gate, ffn_w_val, ffn_conv_w, ffn_conv_b, ffn_w_down):
    b, l, d = x.shape
    depth = ada_w.shape[0]
    cc = jnp.concatenate([c, c_ctx[None, :], jnp.zeros((8 - b - 1, d), F32)], axis=0)
    m = _ada(cc, ada_w, ada_b)
    xs = x
    for layer in range(depth):
        i = layer // 2
        lat = [m[layer, :b, k * d:(k + 1) * d].reshape(b, 1, d) for k in range(6)]
        sh1, sc1, g1, sh2, sc2, g2 = lat
        h = _normmod(xs, norm1_g[layer], sc1, sh1)
        if layer % 2 == 0:
            mc = [jnp.broadcast_to(m[layer, b, k * d:(k + 1) * d].reshape(1, 1, d), (b, 1, d)) for k in range(2)]
            hc = _normmod(ctx, norm1_g[layer], mc[1], mc[0])
            parts, ws = _attn_mixer(h, hc, 0.8 - 0.6 * math.exp(-0.3 * layer), attn_w_in[i], attn_w_out[i],
                                    a_q_g[i], a_k_g[i], a_lam_q1[i], a_lam_k1[i], a_lam_q2[i], a_lam_k2[i],
                                    a_sub_g[i], b_q_g[i], b_k_g[i], b_sink[i])
        else:
            hy = (hy_w_in[i], hy_b_in[i], hy_sconv_w[i], hy_sconv_b[i], hy_f_w0[i], hy_f_b0[i], hy_f_w1[i],
                  hy_f_b1[i], hy_f_w2[i], hy_f_b2[i], hy_f_freq[i], hy_f_w3[i], hy_d[i], hy_w_out[i])
            y, w_o = _hyena_mixer(h, hy, l, d)
            parts, ws = [y], [w_o]
        xs = _mm_resid(parts, ws, xs, g1)
        h2 = _normmod(xs, norm2_g[layer], sc2, sh2)
        d_ff = ffn_w_gate.shape[2]
        hid = _mm_conv(_ffn_up_kernel, h2, [ffn_w_gate[layer].astype(BF16), ffn_w_val[layer].astype(BF16)],
                       [ffn_conv_w[layer], ffn_conv_b[layer].reshape(1, -1)], d_ff, BF16, "ffn_up_conv_glu")
        xs = _mm_resid([hid], [ffn_w_down[layer].astype(BF16)], xs, g2)
    return xs
```

```python
import functools
import math

import numpy as np
import jax
import jax.numpy as jnp
from jax import lax
from jax.experimental import pallas as pl
from jax.experimental.pallas import tpu as pltpu

F32 = jnp.float32
BF16 = jnp.bfloat16
HIGHEST = lax.Precision.HIGHEST

HEAD_DIM = 128
A_HALF = HEAD_DIM // 2
B_KV_HEADS = 2
GRID_W = 64
WINDOW = 128
ROPE_BASE = 10000.0
EPS = 1e-6
NEG = -1e30
HY_ORDER = 2
HY_DECAY_TARGET = 1e-2
HY_MAX_DECAY_PCT = 0.3
HY_MIN_DECAY_PCT = 1.5

LANES = 128
BF16_SUBLANES = 16
FFT_N1 = 256
VMEM_LIMIT_BYTES = 56 * 1024 * 1024


def _params(*sem):
    return pltpu.CompilerParams(dimension_semantics=sem, vmem_limit_bytes=VMEM_LIMIT_BYTES)


def _ada_kernel(c_ref, w_ref, b_ref, o_ref):
    c = c_ref[...]
    s = c * (1.0 / (1.0 + jnp.exp(-c)))
    o_ref[0] = jnp.dot(s, w_ref[0], preferred_element_type=F32, precision=HIGHEST) + b_ref[0]


def _ada(cc, ada_w, ada_b):
    depth, d, n = ada_w.shape
    tn = 1024
    return pl.pallas_call(
        _ada_kernel,
        grid=(depth, n // tn),
        in_specs=[pl.BlockSpec((8, d), lambda l, j: (0, 0)),
                  pl.BlockSpec((1, d, tn), lambda l, j: (l, 0, j)),
                  pl.BlockSpec((1, 1, tn), lambda l, j: (l, 0, j))],
        out_specs=pl.BlockSpec((1, 8, tn), lambda l, j: (l, 0, j)),
        out_shape=jax.ShapeDtypeStruct((depth, 8, n), F32),
        compiler_params=_params("parallel", "parallel"),
        name="ada_modulation",
    )(cc, ada_w, ada_b.reshape(depth, 1, n))


def _normmod_kernel(x_ref, g_ref, sc_ref, sh_ref, o_ref):
    x = x_ref[0]
    ms = jnp.mean(x * x, axis=-1, keepdims=True)
    y = x * lax.rsqrt(ms + EPS) * g_ref[...]
    o_ref[0] = (y * (1.0 + sc_ref[0]) + sh_ref[0]).astype(o_ref.dtype)


def _normmod(x, g, sc, sh):
    b, l, d = x.shape
    tm = min(512, l)
    return pl.pallas_call(
        _normmod_kernel,
        grid=(b, l // tm),
        in_specs=[pl.BlockSpec((1, tm, d), lambda bi, i: (bi, i, 0)),
                  pl.BlockSpec((1, d), lambda bi, i: (0, 0)),
                  pl.BlockSpec((1, 1, d), lambda bi, i: (bi, 0, 0)),
                  pl.BlockSpec((1, 1, d), lambda bi, i: (bi, 0, 0))],
        out_specs=pl.BlockSpec((1, tm, d), lambda bi, i: (bi, i, 0)),
        out_shape=jax.ShapeDtypeStruct((b, l, d), BF16),
        compiler_params=_params("parallel", "parallel"),
        name="rmsnorm_modulate",
    )(x, g.reshape(1, d), sc, sh)


def _dot_split(x, m):
    hi = x.astype(BF16)
    lo = (x - hi.astype(F32)).astype(BF16)
    return jnp.dot(hi, m, preferred_element_type=F32) + jnp.dot(lo, m, preferred_element_type=F32)


def _inproj_kernel(h_ref, w_ref, gain_ref, scale_ref, cos_ref, sin_ref, avg_ref, perm_ref, o_ref, *, plain, rope):
    j = pl.program_id(1)
    z = jnp.dot(h_ref[...], w_ref[...], preferred_element_type=F32)
    is_plain = functools.reduce(jnp.logical_or, [(j >= lo) & (j < hi) for lo, hi in plain])

    @pl.when(is_plain)
    def _():
        o_ref[...] = z.astype(o_ref.dtype)

    @pl.when(jnp.logical_not(is_plain))
    def _():
        ms = _dot_split(z * z, avg_ref[0])
        y = z * lax.rsqrt(ms + EPS) * gain_ref[...]
        if rope:
            y = y * cos_ref[0] + _dot_split(y, perm_ref[0]) * sin_ref[0]
        o_ref[...] = (y * scale_ref[...]).astype(o_ref.dtype)


def _inproj(h2d, w, gain, scale, cos_tab, sin_tab, avg, perm, *, plain, type_of_block, col0, ncols, l, rope):
    m, d = h2d.shape
    tn = 2 * HEAD_DIM
    tm = min(512, l)
    pos_blocks = l // tm
    kern = functools.partial(_inproj_kernel, plain=plain, rope=rope)
    by_type = lambda i, j: (type_of_block(j), 0, 0)
    return pl.pallas_call(
        kern,
        grid=(m // tm, ncols // tn),
        in_specs=[pl.BlockSpec((tm, d), lambda i, j: (i, 0)),
                  pl.BlockSpec((d, tn), lambda i, j: (0, j + col0)),
                  pl.BlockSpec((1, tn), lambda i, j: (0, j + col0)),
                  pl.BlockSpec((1, tn), lambda i, j: (0, j + col0)),
                  pl.BlockSpec((1, tm, tn), lambda i, j: (type_of_block(j), i % pos_blocks, 0)),
                  pl.BlockSpec((1, tm, tn), lambda i, j: (type_of_block(j), i % pos_blocks, 0)),
                  pl.BlockSpec((1, tn, tn), by_type),
                  pl.BlockSpec((1, tn, tn), by_type)],
        out_specs=pl.BlockSpec((tm, tn), lambda i, j: (i, j)),
        out_shape=jax.ShapeDtypeStruct((m, ncols), BF16),
        compiler_params=_params("parallel", "arbitrary"),
        name="attn_in_projection",
    )(h2d, w, gain, scale, cos_tab, sin_tab, avg, perm)


def _diffattn_kernel(lam_ref, q_ref, kc_ref, vc_ref, k_ref, v_ref, g_ref, o_ref,
                     q2_s, m_s, l_s, acc_s, *, tq, nkv, out_scale):
    kv = pl.program_id(3)

    def process(k, v):
        s = lax.dot_general(q2_s[...], k, (((1,), (1,)), ((), ())), preferred_element_type=F32)
        m_prev = m_s[...]
        m_new = jnp.maximum(m_prev, jnp.max(s, axis=-1, keepdims=True))
        alpha = jnp.exp2(m_prev - m_new)
        p = jnp.exp2(s - m_new)
        l_s[...] = alpha * l_s[...] + jnp.sum(p, axis=-1, keepdims=True)
        acc_s[...] = alpha * acc_s[...] + jnp.dot(p.astype(BF16), v, preferred_element_type=F32)
        m_s[...] = m_new

    @pl.when(kv == 0)
    def _():
        q = q_ref[0]
        lane = lax.broadcasted_iota(jnp.int32, q.shape, 1)
        zero = jnp.zeros_like(q)
        q2_s[0:tq, :] = jnp.where(lane < A_HALF, q, zero)
        q2_s[tq:2 * tq, :] = jnp.where(lane >= A_HALF, q, zero)
        m_s[...] = jnp.full(m_s.shape, -jnp.inf, F32)
        l_s[...] = jnp.zeros(l_s.shape, F32)
        acc_s[...] = jnp.zeros(acc_s.shape, F32)
        process(kc_ref[0], vc_ref[0])

    process(k_ref[0], v_ref[0])

    @pl.when(kv == nkv - 1)
    def _():
        o = acc_s[...] / l_s[...]
        d = o[0:tq] - lam_ref[0, 0] * o[tq:2 * tq]
        ms = jnp.mean(d * d, axis=-1, keepdims=True)
        o_ref[0] = (d * lax.rsqrt(ms + EPS) * g_ref[...] * out_scale).astype(o_ref.dtype)


def _diffattn_fast_kernel(lam_ref, q_ref, kc_ref, vct_ref, k_ref, vt_ref, g_ref, o_ref,
                          q2_s, l_s, acc_s, p_s, *, tq, tk, kc, nkv, out_scale):
    kv = pl.program_id(3)
    dn = (((1,), (1,)), ((), ()))

    def weights(k):
        n = k.shape[0]
        p = jnp.exp2(lax.dot_general(k, q2_s[...], dn, preferred_element_type=F32))
        return p.astype(BF16), jnp.sum(p.reshape(n // 8, 8, 2 * tq), axis=0)

    @pl.when(kv == 0)
    def _():
        q = q_ref[0]
        lane = lax.broadcasted_iota(jnp.int32, q.shape, 1)
        zero = jnp.zeros_like(q)
        q2_s[0:tq, :] = jnp.where(lane < A_HALF, q, zero)
        q2_s[tq:2 * tq, :] = jnp.where(lane >= A_HALF, q, zero)
        p, ls = weights(kc_ref[0])
        l_s[...] = ls
        acc_s[...] = jnp.dot(vct_ref[0, 0], p, preferred_element_type=F32)

    ls = l_s[...]
    for ci in range(tk // kc):
        p, lc = weights(k_ref[0, ci * kc:(ci + 1) * kc, :])
        p_s[ci * kc:(ci + 1) * kc, :] = p
        ls = ls + lc
    l_s[...] = ls
    acc_s[...] += jnp.dot(vt_ref[0, 0], p_s[...], preferred_element_type=F32)

    @pl.when(kv == nkv - 1)
    def _():
        l = jnp.sum(l_s[...], axis=0, keepdims=True)
        o = acc_s[...] / l
        d = o[:, 0:tq] - lam_ref[0, 0] * o[:, tq:2 * tq]
        ms = jnp.mean(d * d, axis=0, keepdims=True)
        y = d * lax.rsqrt(ms + EPS) * g_ref[...] * out_scale
        o_ref[0] = y.T.astype(o_ref.dtype)


def _diffattn_fast(lam, p, pc, sub_g, *, heads, l, c, out_scale):
    b = p.shape[0]
    tq = min(512, l)
    tk = min(4096, l)
    kc = min(512, tk)
    nkv = l // tk
    k_blk = 2 * heads
    hd = heads * HEAD_DIM
    vt = jnp.transpose(p[:, :, 3 * hd:4 * hd].reshape(b, l, heads, HEAD_DIM), (0, 2, 3, 1))
    vct = jnp.transpose(pc[:, :, hd:2 * hd].reshape(b, c, heads, HEAD_DIM), (0, 2, 3, 1))
    kern = functools.partial(_diffattn_fast_kernel, tq=tq, tk=tk, kc=kc, nkv=nkv, out_scale=out_scale)
    return pl.pallas_call(
        kern,
        grid=(b, heads, l // tq, nkv),
        in_specs=[pl.BlockSpec(memory_space=pltpu.SMEM),
                  pl.BlockSpec((1, tq, HEAD_DIM), lambda bi, h, i, kv: (bi, i, h)),
                  pl.BlockSpec((1, c, HEAD_DIM), lambda bi, h, i, kv: (bi, 0, h)),
                  pl.BlockSpec((1, 1, HEAD_DIM, c), lambda bi, h, i, kv: (bi, h, 0, 0)),
                  pl.BlockSpec((1, tk, HEAD_DIM), lambda bi, h, i, kv: (bi, kv, k_blk + h)),
                  pl.BlockSpec((1, 1, HEAD_DIM, tk), lambda bi, h, i, kv: (bi, h, 0, kv)),
                  pl.BlockSpec((HEAD_DIM, 1), lambda bi, h, i, kv: (0, 0))],
        out_specs=pl.BlockSpec((1, tq, HEAD_DIM), lambda bi, h, i, kv: (bi, i, h)),
        out_shape=jax.ShapeDtypeStruct((b, l, hd), BF16),
        scratch_shapes=[pltpu.VMEM((2 * tq, HEAD_DIM), BF16),
                        pltpu.VMEM((8, 2 * tq), F32),
                        pltpu.VMEM((HEAD_DIM, 2 * tq), F32),
                        pltpu.VMEM((tk, 2 * tq), BF16)],
        compiler_params=_params("parallel", "parallel", "parallel", "arbitrary"),
        name="diff_attention_bounded",
    )(lam, p, pc, vct, p, vt, sub_g.reshape(HEAD_DIM, 1))


SCORE_BOUND_LOG2 = 60.0


def _diffattn(lam, p, pc, sub_g, score_bound, *, heads, l, c, out_scale):
    kw = dict(heads=heads, l=l, c=c, out_scale=out_scale)
    return lax.cond(score_bound < SCORE_BOUND_LOG2,
                    lambda *a: _diffattn_fast(*a, **kw), lambda *a: _diffattn_safe(*a, **kw),
                    lam, p, pc, sub_g)


def _diffattn_safe(lam, p, pc, sub_g, *, heads, l, c, out_scale):
    b = p.shape[0]
    tq = min(256, l)
    tk = min(512, l)
    nkv = l // tk
    q_blk = 0
    k_blk = 2 * heads
    v_blk = 3 * heads
    kern = functools.partial(_diffattn_kernel, tq=tq, nkv=nkv, out_scale=out_scale)
    return pl.pallas_call(
        kern,
        grid=(b, heads, l // tq, nkv),
        in_specs=[pl.BlockSpec(memory_space=pltpu.SMEM),
                  pl.BlockSpec((1, tq, HEAD_DIM), lambda bi, h, i, kv: (bi, i, q_blk + h)),
                  pl.BlockSpec((1, c, HEAD_DIM), lambda bi, h, i, kv: (bi, 0, h)),
                  pl.BlockSpec((1, c, HEAD_DIM), lambda bi, h, i, kv: (bi, 0, heads + h)),
                  pl.BlockSpec((1, tk, HEAD_DIM), lambda bi, h, i, kv: (bi, kv, k_blk + h)),
                  pl.BlockSpec((1, tk, HEAD_DIM), lambda bi, h, i, kv: (bi, kv, v_blk + h)),
                  pl.BlockSpec((1, HEAD_DIM), lambda bi, h, i, kv: (0, 0))],
        out_specs=pl.BlockSpec((1, tq, HEAD_DIM), lambda bi, h, i, kv: (bi, i, h)),
        out_shape=jax.ShapeDtypeStruct((b, l, heads * HEAD_DIM), BF16),
        scratch_shapes=[pltpu.VMEM((2 * tq, HEAD_DIM), BF16),
                        pltpu.VMEM((2 * tq, 1), F32),
                        pltpu.VMEM((2 * tq, 1), F32),
                        pltpu.VMEM((2 * tq, HEAD_DIM), F32)],
        compiler_params=_params("parallel", "parallel", "parallel", "arbitrary"),
        name="diff_attention",
    )(lam, p, pc, pc, p, p, sub_g.reshape(1, HEAD_DIM))


def _winattn_kernel(sink_ref, q_ref, kp_ref, km_ref, kn_ref, vp_ref, vm_ref, vn_ref, kc_ref, vc_ref,
                    o_ref, *, tq, l, group):
    g = pl.program_id(1)
    i = pl.program_id(2)
    kband = jnp.concatenate([kp_ref[0], km_ref[0], kn_ref[0]], axis=0)
    vband = jnp.concatenate([vp_ref[0], vm_ref[0], vn_ref[0]], axis=0)
    nk = tq + 2 * WINDOW
    qpos = i * tq + lax.broadcasted_iota(jnp.int32, (tq, nk), 0)
    kpos = i * tq - WINDOW + lax.broadcasted_iota(jnp.int32, (tq, nk), 1)
    valid = (jnp.abs(kpos - qpos) <= WINDOW) & (kpos >= 0) & (kpos < l)
    kc = kc_ref[0]
    vc = vc_ref[0]
    dn = (((1,), (1,)), ((), ()))
    for r in range(group):
        sl = slice(r * HEAD_DIM, (r + 1) * HEAD_DIM)
        q = q_ref[0, :, sl]
        s_lat = jnp.where(valid, lax.dot_general(q, kband, dn, preferred_element_type=F32), NEG)
        s_ctx = lax.dot_general(q, kc, dn, preferred_element_type=F32)
        sk = sink_ref[g, r]
        m = jnp.maximum(jnp.maximum(jnp.max(s_lat, axis=-1, keepdims=True),
                                    jnp.max(s_ctx, axis=-1, keepdims=True)), sk)
        p_lat = jnp.exp(s_lat - m)
        p_ctx = jnp.exp(s_ctx - m)
        denom = (jnp.sum(p_lat, axis=-1, keepdims=True) + jnp.sum(p_ctx, axis=-1, keepdims=True)
                 + jnp.exp(sk - m))
        o = (jnp.dot(p_lat.astype(BF16), vband, preferred_element_type=F32)
             + jnp.dot(p_ctx.astype(BF16), vc, preferred_element_type=F32))
        o_ref[0, :, sl] = (o / denom).astype(o_ref.dtype)


def _winattn(sink, p, pc, *, a_heads, b_heads, l, c):
    b = p.shape[0]
    group = b_heads // B_KV_HEADS
    tq = min(256, l)
    wpb = tq // WINDOW
    nwb = l // WINDOW
    gw = group * HEAD_DIM
    q_blk = a_heads * HEAD_DIM // gw
    k_blk = (a_heads + b_heads) + 2 * a_heads
    v_blk = k_blk + B_KV_HEADS
    kc_blk = 2 * a_heads
    vc_blk = kc_blk + B_KV_HEADS
    kern = functools.partial(_winattn_kernel, tq=tq, l=l, group=group)

    def prev_map(col):
        return lambda bi, g, i: (bi, jnp.maximum(i * wpb - 1, 0), col + g)

    def main_map(col):
        return lambda bi, g, i: (bi, i, col + g)

    def next_map(col):
        return lambda bi, g, i: (bi, jnp.minimum((i + 1) * wpb, nwb - 1), col + g)

    return pl.pallas_call(
        kern,
        grid=(b, B_KV_HEADS, l // tq),
        in_specs=[pl.BlockSpec(memory_space=pltpu.SMEM),
                  pl.BlockSpec((1, tq, gw), lambda bi, g, i: (bi, i, q_blk + g)),
                  pl.BlockSpec((1, WINDOW, HEAD_DIM), prev_map(k_blk)),
                  pl.BlockSpec((1, tq, HEAD_DIM), main_map(k_blk)),
                  pl.BlockSpec((1, WINDOW, HEAD_DIM), next_map(k_blk)),
                  pl.BlockSpec((1, WINDOW, HEAD_DIM), prev_map(v_blk)),
                  pl.BlockSpec((1, tq, HEAD_DIM), main_map(v_blk)),
                  pl.BlockSpec((1, WINDOW, HEAD_DIM), next_map(v_blk)),
                  pl.BlockSpec((1, c, HEAD_DIM), lambda bi, g, i: (bi, 0, kc_blk + g)),
                  pl.BlockSpec((1, c, HEAD_DIM), lambda bi, g, i: (bi, 0, vc_blk + g))],
        out_specs=pl.BlockSpec((1, tq, gw), lambda bi, g, i: (bi, i, g)),
        out_shape=jax.ShapeDtypeStruct((b, l, b_heads * HEAD_DIM), BF16),
        compiler_params=_params("parallel", "parallel", "parallel"),
        name="window_attention",
    )(sink, p, p, p, p, p, p, p, pc, pc)


def _mm_resid_kernel(*refs, nparts):
    a_refs = refs[0:nparts]
    w_refs = refs[nparts:2 * nparts]
    res_ref, gate_ref, o_ref = refs[2 * nparts:]
    y = jnp.dot(a_refs[0][0].astype(BF16), w_refs[0][...], preferred_element_type=F32)
    for a_ref, w_ref in zip(a_refs[1:], w_refs[1:]):
        y = y + jnp.dot(a_ref[0].astype(BF16), w_ref[...], preferred_element_type=F32)
    o_ref[0] = res_ref[0] + gate_ref[0] * y


def _mm_resid(parts, ws, res, gate):
    b, l, n = res.shape
    tm = min(512, l)
    tn = min(512, n)
    nparts = len(parts)
    in_specs = ([pl.BlockSpec((1, tm, a.shape[2]), lambda bi, i, j: (bi, i, 0)) for a in parts]
                + [pl.BlockSpec((w.shape[0], tn), lambda bi, i, j: (0, j)) for w in ws]
                + [pl.BlockSpec((1, tm, tn), lambda bi, i, j: (bi, i, j)),
                   pl.BlockSpec((1, 1, tn), lambda bi, i, j: (bi, 0, j))])
    return pl.pallas_call(
        functools.partial(_mm_resid_kernel, nparts=nparts),
        grid=(b, l // tm, n // tn),
        in_specs=in_specs,
        out_specs=pl.BlockSpec((1, tm, tn), lambda bi, i, j: (bi, i, j)),
        out_shape=jax.ShapeDtypeStruct((b, l, n), F32),
        compiler_params=_params("parallel", "parallel", "arbitrary"),
        name="out_projection_gated_residual",
    )(*parts, *ws, res, gate)


HALO = BF16_SUBLANES


def _fill_rows(hbuf, hp_ref, hm_ref, hn_ref, tm):
    hbuf[0:HALO, :] = hp_ref[0]
    hbuf[HALO:HALO + tm, :] = hm_ref[0]
    hbuf[HALO + tm:2 * HALO + tm, :] = hn_ref[0]


def _conv3(zbuf, cw_ref, cb_ref, tm, first, last):
    lo = slice(HALO - 1, HALO)
    hi = slice(HALO + tm, HALO + tm + 1)
    zbuf[lo, :] = jnp.where(first, 0.0, zbuf[lo, :])
    zbuf[hi, :] = jnp.where(last, 0.0, zbuf[hi, :])
    z = zbuf[...]
    rows = z.shape[0]
    g = (z * cw_ref[1:2, :] + pltpu.roll(z * cw_ref[0:1, :], 1, 0) + pltpu.roll(z * cw_ref[2:3, :], rows - 1, 0))
    return g[HALO:HALO + tm] + cb_ref[...]


def _gelu_tanh(x):
    return 0.5 * x * (1.0 + jnp.tanh(math.sqrt(2.0 / math.pi) * (x + 0.044715 * (x * x * x))))


def _ffn_up_kernel(hp_ref, hm_ref, hn_ref, wg_ref, wv_ref, cw_ref, cb_ref, o_ref, hbuf, zbuf, zbuf2, *, tm, nt):
    i = pl.program_id(1)
    j = pl.program_id(2)

    @pl.when(j == 0)
    def _():
        _fill_rows(hbuf, hp_ref, hm_ref, hn_ref, tm)

    hw = zbuf.shape[1]
    for hh, zb in enumerate((zbuf, zbuf2)):
        cs = slice(hh * hw, (hh + 1) * hw)
        zb[...] = jnp.dot(hbuf[...], wg_ref[:, cs], preferred_element_type=F32)
        g = _conv3(zb, cw_ref.at[:, cs], cb_ref.at[:, cs], tm, i == 0, i == nt - 1)
        v = jnp.dot(hbuf[HALO:HALO + tm, :], wv_ref[:, cs], preferred_element_type=F32)
        o_ref[0, :, cs] = (_gelu_tanh(g) * v).astype(o_ref.dtype)


def _hy_in_kernel(hp_ref, hm_ref, hn_ref, w_ref, b_ref, cw_ref, cb_ref, o_ref, hbuf, zbuf, *, tm, nt):
    i = pl.program_id(1)
    j = pl.program_id(2)

    @pl.when(j == 0)
    def _():
        _fill_rows(hbuf, hp_ref, hm_ref, hn_ref, tm)

    zbuf[...] = jnp.dot(hbuf[...], w_ref[...], preferred_element_type=F32) + b_ref[...]
    o_ref[0] = _conv3(zbuf, cw_ref, cb_ref, tm, i == 0, i == nt - 1).astype(o_ref.dtype)


def _mm_conv(kernel, h, ws, vecs, n, out_dtype, name, nz=1):
    b, l, k = h.shape
    tm = min(512, l)
    tn = min(512, n)
    nt = l // tm
    hpb = tm // HALO
    nhb = l // HALO
    in_specs = ([pl.BlockSpec((1, HALO, k), lambda bi, i, j: (bi, jnp.maximum(i * hpb - 1, 0), 0)),
                 pl.BlockSpec((1, tm, k), lambda bi, i, j: (bi, i, 0)),
                 pl.BlockSpec((1, HALO, k), lambda bi, i, j: (bi, jnp.minimum((i + 1) * hpb, nhb - 1), 0))]
                + [pl.BlockSpec((k, tn), lambda bi, i, j: (0, j)) for _ in ws]
                + [pl.BlockSpec((v.shape[0], tn), lambda bi, i, j: (0, j)) for v in vecs])
    return pl.pallas_call(
        functools.partial(kernel, tm=tm, nt=nt),
        grid=(b, nt, n // tn),
        in_specs=in_specs,
        out_specs=pl.BlockSpec((1, tm, tn), lambda bi, i, j: (bi, i, j)),
        out_shape=jax.ShapeDtypeStruct((b, l, n), out_dtype),
        scratch_shapes=[pltpu.VMEM((tm + 2 * HALO, k), BF16)] + [pltpu.VMEM((tm + 2 * HALO, tn // nz), F32)] * nz,
        compiler_params=_params("parallel", "parallel", "arbitrary"),
        name=name,
    )(h, h, h, *ws, *vecs)


FILT_NB = BF16_SUBLANES


def _filter_dft_kernel(z_ref, w0_ref, b0_ref, w1_ref, b1_ref, w2_ref, b2_ref, fr_ref, w3_ref, dl_ref, t_ref,
                       o_ref, s_ref, a3_s, r_s):
    m, seg, j = pl.program_id(0), pl.program_id(1), pl.program_id(2)
    tn = o_ref.shape[3]
    fr = fr_ref[...]

    def layer(a, w_ref, b_ref):
        return jnp.sin(fr * (jnp.dot(a, w_ref[...], preferred_element_type=F32, precision=HIGHEST)
                             + b_ref[...]))

    @pl.when((seg == 0) & (j == 0))
    def _():
        for jj in range(FILT_NB):
            a = layer(layer(layer(z_ref[jj], w0_ref, b0_ref), w1_ref, b1_ref), w2_ref, b2_ref)
            a_hi = a.astype(BF16)
            a_lo = (a - a_hi.astype(F32)).astype(BF16)
            a3_s[jj] = jnp.concatenate([a_hi, a_hi, a_lo], axis=1)

    @pl.when((m == 0) & (seg == 0) & (j == 0))
    def _():
        s_ref[...] = jnp.zeros(s_ref.shape, F32)

    dl = dl_ref[...]
    backward = seg % 2 == 1
    abs_sum = jnp.zeros((1, tn), F32)
    for jj in range(FILT_NB):
        h = (jnp.dot(a3_s[jj], w3_ref[0], preferred_element_type=F32)
             * jnp.exp(-z_ref[jj][:, 0:1] * dl))
        if jj == 0:
            row = lax.broadcasted_iota(jnp.int32, h.shape, 0)
            h = jnp.where(backward & (m == 0) & (row == 0), 0.0, h)
        abs_sum = abs_sum + jnp.sum(jnp.abs(h), axis=0, keepdims=True)
        r_s[jj] = jnp.dot(t_ref[jj], h.astype(BF16), preferred_element_type=F32)
    o_ref[0] = jnp.swapaxes(r_s[...], 0, 1).astype(o_ref.dtype)
    col = pl.multiple_of(j * tn, tn)
    s_ref[seg, :, pl.ds(col, tn)] += abs_sum


def _hyena_filter_spectra_a(l, d, ta, f_w0, f_b0, f_w1, f_b1, f_w2, f_b2, f_freq, f_w3):
    emb, fw = f_w0.shape
    bands = (emb - 1) // 2
    half = FFT_N1 // 2
    n2 = l // half
    t = np.linspace(0.0, 1.0, l)[:, None]
    w = 2.0 * math.pi * np.arange(l)[:, None] / l
    f = np.linspace(1e-4, bands - 1, bands)[None, :]
    emb_pad = -(-emb // 8) * 8
    z = np.concatenate([t, np.cos(f * w), -np.sin(f * w), np.zeros((l, emb_pad - emb))], axis=-1)
    z = jnp.asarray(z.reshape(half, n2, emb_pad).transpose(1, 0, 2), F32)
    w0 = jnp.concatenate([f_w0, jnp.zeros((emb_pad - emb, fw), F32)], axis=0)
    max_decay = math.log(HY_DECAY_TARGET) / HY_MAX_DECAY_PCT
    min_decay = math.log(HY_DECAY_TARGET) / HY_MIN_DECAY_PCT
    dl = jnp.asarray(np.abs(np.linspace(min_decay, max_decay, d))[None, :], F32)
    nseg = 2 * HY_ORDER
    w3 = jnp.transpose(f_w3.reshape(fw, nseg, d), (1, 0, 2))
    w3_hi = w3.astype(BF16)
    w3_lo = (w3 - w3_hi.astype(F32)).astype(BF16)
    w3 = jnp.concatenate([w3_hi, w3_lo, w3_hi], axis=1)
    nb = min(FILT_NB, n2)
    assert nb == FILT_NB
    tn = min(256, d)
    vec = lambda a: a.reshape(1, fw)
    small = lambda shape: pl.BlockSpec(shape, lambda m, sg, j: (0,) * len(shape))
    return pl.pallas_call(
        _filter_dft_kernel,
        grid=(n2 // nb, nseg, d // tn),
        in_specs=[pl.BlockSpec((nb, half, emb_pad), lambda m, sg, j: (m, 0, 0)),
                  small((emb_pad, fw)), small((1, fw)), small((fw, fw)), small((1, fw)),
                  small((fw, fw)), small((1, fw)), small((1, fw)),
                  pl.BlockSpec((1, 3 * fw, tn), lambda m, sg, j: (sg, 0, j)),
                  pl.BlockSpec((1, tn), lambda m, sg, j: (0, j)),
                  pl.BlockSpec((nb, 2 * FFT_N1, half), lambda m, sg, j: (m, 0, 0))],
        out_specs=[pl.BlockSpec((1, 2 * FFT_N1, nb, tn), lambda m, sg, j: (sg, 0, m, j)),
                   pl.BlockSpec((nseg, 1, d), lambda m, sg, j: (0, 0, 0))],
        out_shape=[jax.ShapeDtypeStruct((nseg, 2 * FFT_N1, n2, d), BF16),
                   jax.ShapeDtypeStruct((nseg, 1, d), F32)],
        scratch_shapes=[pltpu.VMEM((nb, half, 3 * fw), BF16),
                        pltpu.VMEM((nb, 2 * FFT_N1, tn), F32)],
        compiler_params=_params("arbitrary", "arbitrary", "arbitrary"),
        name="hyena_filter_dft_stage_a",
    )(z, w0, vec(f_b0), f_w1, vec(f_b1), f_w2, vec(f_b2), vec(f_freq), w3, dl, ta)


def _dft_tables(l):
    n1 = FFT_N1
    n = 2 * l
    n2 = n // n1
    k1 = np.arange(n1)[:, None]
    f1 = np.exp(-2j * np.pi * k1 * np.arange(n1 // 2)[None, :] / n1)
    tw = np.exp(-2j * np.pi * np.arange(n2)[:, None] * np.arange(n1)[None, :] / n)
    f1r, f1i = jnp.asarray(f1.real, F32), jnp.asarray(f1.imag, F32)
    twr, twi = jnp.asarray(tw.real, F32)[:, :, None], jnp.asarray(tw.imag, F32)[:, :, None]
    fr = twr * f1r - twi * f1i
    fi = twr * f1i + twi * f1r
    ta = jnp.concatenate([jnp.concatenate([fr, -fi], axis=2),
                          jnp.concatenate([fi, fr], axis=2)], axis=1)
    fb = np.exp(-2j * np.pi * np.arange(n2)[:, None] * np.arange(n2)[None, :] / n2)
    tb = np.block([[fb.real, -fb.imag], [fb.imag, fb.real]])
    return (ta.astype(BF16), jnp.swapaxes(ta, 1, 2).astype(BF16),
            jnp.asarray(tb, BF16), jnp.asarray(tb.T, BF16))


FFT_NB = BF16_SUBLANES


def _fft_a_kernel(x_ref, t_ref, o_ref, r_s):
    planes = x_ref.shape[1]
    xt = [jnp.swapaxes(x_ref[0, c], 0, 1) for c in range(planes)]
    for jj in range(FFT_NB):
        x = jnp.concatenate([xt[c][jj] for c in range(planes)], axis=0).astype(BF16)
        r_s[jj] = jnp.dot(t_ref[jj], x, preferred_element_type=F32)
    o_ref[0] = jnp.swapaxes(r_s[...], 0, 1).astype(o_ref.dtype)


def _fft_a(x5, ta, *, n2, d, seg):
    s, p, half, _, _ = x5.shape
    kdim = p * half
    tn = min(256, d)
    sb, db = (seg * d) // tn, d // tn
    return pl.pallas_call(
        _fft_a_kernel,
        grid=(s, n2 // FFT_NB, db),
        in_specs=[pl.BlockSpec((1, p, half, FFT_NB, tn), lambda si, m, j: (si, 0, 0, m, sb + j)),
                  pl.BlockSpec((FFT_NB, 2 * FFT_N1, kdim), lambda si, m, j: (m, 0, 0))],
        out_specs=pl.BlockSpec((1, 2 * FFT_N1, FFT_NB, tn), lambda si, m, j: (si, 0, m, j)),
        out_shape=jax.ShapeDtypeStruct((s, 2 * FFT_N1, n2, d), BF16),
        scratch_shapes=[pltpu.VMEM((FFT_NB, 2 * FFT_N1, tn), F32)],
        compiler_params=_params("parallel", "parallel", "parallel"),
        name="dft_stage_a",
    )(x5, ta)


def _fft_bfilt_kernel(af_ref, ab_ref, tb_ref, sc_ref, o_ref, *, n2):
    tb = tb_ref[...]
    hf = jnp.dot(tb, af_ref[0].reshape(2 * n2, -1).astype(BF16), preferred_element_type=F32)
    hb = jnp.dot(tb, ab_ref[0].reshape(2 * n2, -1).astype(BF16), preferred_element_type=F32)
    sc = sc_ref[0]
    o_ref[0, 0, 0] = (hf[0:n2] + hb[0:n2]) * sc
    o_ref[0, 1, 0] = (hf[n2:2 * n2] - hb[n2:2 * n2]) * sc


def _fft_bfilt(af, tb, scale, *, n2, d):
    tn = min(2048, d)
    blk = (1, 2, 1, n2, tn)
    return pl.pallas_call(
        functools.partial(_fft_bfilt_kernel, n2=n2),
        grid=(HY_ORDER, FFT_N1, d // tn),
        in_specs=[pl.BlockSpec(blk, lambda o, k, j: (2 * o, 0, k, 0, j)),
                  pl.BlockSpec(blk, lambda o, k, j: (2 * o + 1, 0, k, 0, j)),
                  pl.BlockSpec((2 * n2, 2 * n2), lambda o, k, j: (0, 0)),
                  pl.BlockSpec((1, 1, tn), lambda o, k, j: (o, 0, j))],
        out_specs=pl.BlockSpec(blk, lambda o, k, j: (o, 0, k, 0, j)),
        out_shape=jax.ShapeDtypeStruct((HY_ORDER, 2, FFT_N1, n2, d), F32),
        compiler_params=_params("parallel", "parallel", "parallel"),
        name="dft_stage_b_filters",
    )(af, af, tb, scale)


def _fft_b_kernel(a_ref, g_ref, tb_ref, tbt_ref, o_ref, *, n2):
    z = jnp.dot(tb_ref[...], a_ref[...].reshape(2 * n2, -1).astype(BF16), preferred_element_type=F32)
    zr, zi = z[0:n2], z[n2:2 * n2]
    gr, gi = g_ref[0, 0, 0], g_ref[0, 1, 0]
    prod = jnp.concatenate([zr * gr - zi * gi, zr * gi + zi * gr], axis=0).astype(BF16)
    y = jnp.dot(tbt_ref[...], prod, preferred_element_type=F32)
    o_ref[...] = y.reshape(o_ref.shape).astype(o_ref.dtype)


def _fft_b(a, g, tb, tbt, order, *, n2, d):
    tn = min(2048, d)
    blk = (2, 1, n2, tn)
    return pl.pallas_call(
        functools.partial(_fft_b_kernel, n2=n2),
        grid=(FFT_N1, d // tn),
        in_specs=[pl.BlockSpec(blk, lambda k, j: (0, k, 0, j)),
                  pl.BlockSpec((1,) + blk, lambda k, j: (order, 0, k, 0, j)),
                  pl.BlockSpec((2 * n2, 2 * n2), lambda k, j: (0, 0)),
                  pl.BlockSpec((2 * n2, 2 * n2), lambda k, j: (0, 0))],
        out_specs=pl.BlockSpec(blk, lambda k, j: (0, k, 0, j)),
        out_shape=jax.ShapeDtypeStruct((2, FFT_N1, n2, d), BF16),
        compiler_params=_params("parallel", "parallel"),
        name="dft_stage_b_spectrum_product",
    )(a, g, tb, tbt)


def _fft_ainv_kernel(y_ref, t_ref, u_ref, gate_ref, d_ref, o_ref, r_s):
    half = o_ref.shape[1]
    yt = jnp.swapaxes(y_ref[...].astype(F32), 0, 1).astype(BF16)
    for c in range(2):
        r_s[c] = jnp.swapaxes(u_ref[c], 0, 1) * d_ref[...]
    for jj in range(FFT_NB):
        x = jnp.dot(t_ref[jj], yt[jj], preferred_element_type=F32)
        for c in range(2):
            r_s[c, jj] += x[c * half:(c + 1) * half]
    for c in range(2):
        o_ref[c] = gate_ref[c] * jnp.swapaxes(r_s[c], 0, 1)


def _fft_ainv(y3, tat, u4, gate4, dskip, *, n2, d, useg, gseg):
    b, half, _, _ = u4.shape
    tn = min(128, d)
    db = d // tn
    usb, gsb = (useg * d) // tn, (gseg * d) // tn
    return pl.pallas_call(
        _fft_ainv_kernel,
        grid=(n2 // FFT_NB, db),
        in_specs=[pl.BlockSpec((2 * FFT_N1, FFT_NB, tn), lambda m, j: (0, m, j)),
                  pl.BlockSpec((FFT_NB, FFT_N1, 2 * FFT_N1), lambda m, j: (m, 0, 0)),
                  pl.BlockSpec((b, half, FFT_NB, tn), lambda m, j: (0, 0, m, usb + j)),
                  pl.BlockSpec((b, half, FFT_NB, tn), lambda m, j: (0, 0, m, gsb + j)),
                  pl.BlockSpec((1, tn), lambda m, j: (0, j))],
        out_specs=pl.BlockSpec((b, half, FFT_NB, tn), lambda m, j: (0, 0, m, j)),
        out_shape=jax.ShapeDtypeStruct((b, half, n2, d), F32),
        scratch_shapes=[pltpu.VMEM((b, FFT_NB, half, tn), F32)],
        compiler_params=_params("parallel", "parallel"),
        name="dft_stage_a_inverse_gate",
    )(y3, tat, u4, gate4, dskip)


def _hyena_mixer(h, hy, l, d):
    (w_in, b_in, sconv_w, sconv_b, f_w0, f_b0, f_w1, f_b1, f_w2, f_b2, f_freq, f_w3, d_skip, w_out) = hy
    b = h.shape[0]
    assert b == 2, "the two batch rows ride the real / imaginary planes of one complex DFT"
    n1 = FFT_N1
    half = n1 // 2
    n2 = (2 * l) // n1
    assert half * n2 == l
    z3 = _mm_conv(_hy_in_kernel, h, [w_in.astype(BF16)],
                  [b_in.reshape(1, -1), sconv_w, sconv_b.reshape(1, -1)], 3 * d, F32, "hyena_in_proj_conv")
    ta, tat, tb, tbt = _dft_tables(l)
    nseg = 2 * HY_ORDER
    af, fsum = _hyena_filter_spectra_a(l, d, ta, f_w0, f_b0, f_w1, f_b1, f_w2, f_b2, f_freq, f_w3)
    norm = (fsum[0::2] + fsum[1::2])
    g = _fft_bfilt(af.reshape(nseg, 2, n1, n2, d), tb, 1.0 / (norm * (2 * l)), n2=n2, d=d)
    z4 = z3.reshape(b, half, n2, 3 * d)
    y, yseg = z4, 0
    for o in range(HY_ORDER):
        a = _fft_a(y[None], ta, n2=n2, d=d, seg=yseg)
        yb = _fft_b(a.reshape(2, n1, n2, d), g, tb, tbt, o, n2=n2, d=d)
        y = _fft_ainv(yb.reshape(2 * n1, n2, d), tat, y, z4, d_skip[o].reshape(1, d),
                      n2=n2, d=d, useg=yseg, gseg=o + 1)
        yseg = 0
    return y.reshape(b, l, d), w_out.astype(BF16)


def _rope_tables(l):
    t = jnp.arange(l)
    row = (t // GRID_W).astype(F32)[:, None]
    col = (t % GRID_W).astype(F32)[:, None]
    width = 2 * HEAD_DIM
    cos, sin, avg, perm = [], [], [], []
    for dim in (A_HALF, HEAD_DIM):
        nf = dim // 4
        inv = jnp.asarray(ROPE_BASE ** (-np.arange(nf) / nf), F32)[None, :]
        ar, ac = row * inv, col * inv
        c = jnp.concatenate([jnp.cos(ar), jnp.cos(ar), jnp.cos(ac), jnp.cos(ac)], axis=1)
        s = jnp.concatenate([-jnp.sin(ar), jnp.sin(ar), -jnp.sin(ac), jnp.sin(ac)], axis=1)
        reps = width // dim
        cos.append(jnp.tile(c, (1, reps)))
        sin.append(jnp.tile(s, (1, reps)))
        lane = np.arange(width)
        avg.append((lane[:, None] // dim == lane[None, :] // dim) / dim)
        partner = np.where(lane % (2 * nf) < nf, lane + nf, lane - nf)
        perm.append((lane[:, None] == partner[None, :]).astype(np.float64))
    return (jnp.stack(cos), jnp.stack(sin), jnp.asarray(np.stack(avg), BF16), jnp.asarray(np.stack(perm), BF16))


def _attn_mixer(h, hc, lam_init, w_in, w_out, a_q_g, a_k_g, lq1, lk1, lq2, lk2, a_sub_g, b_q_g, b_k_g, b_sink):
    b, l, d = h.shape
    c = hc.shape[1]
    a_heads = d // (2 * HEAD_DIM)
    b_heads = d // (2 * HEAD_DIM)
    q_cols = (a_heads + b_heads) * HEAD_DIM
    in_cols = w_in.shape[1]
    lam = (jnp.exp(jnp.sum(lq1 * lk1)) - jnp.exp(jnp.sum(lq2 * lk2)) + lam_init).reshape(1, 1)
    sink = b_sink.reshape(B_KV_HEADS, b_heads // B_KV_HEADS)

    ones = lambda n: jnp.ones((n,), F32)
    tile = lambda v, n: jnp.tile(v, n)
    gain = jnp.concatenate([tile(a_q_g, 2 * a_heads), tile(b_q_g, b_heads), tile(a_k_g, 2 * a_heads),
                            ones(a_heads * HEAD_DIM), tile(b_k_g, B_KV_HEADS),
                            ones(B_KV_HEADS * HEAD_DIM)]).reshape(1, in_cols)
    scale = jnp.concatenate([jnp.full((a_heads * HEAD_DIM,), A_HALF ** -0.5 * math.log2(math.e), F32),
                             jnp.full((b_heads * HEAD_DIM,), HEAD_DIM ** -0.5, F32),
                             ones(in_cols - q_cols)]).reshape(1, in_cols)
    e = [0, a_heads // 2, (a_heads + b_heads) // 2, (2 * a_heads + b_heads) // 2,
         (3 * a_heads + b_heads) // 2, (3 * a_heads + b_heads) // 2 + 1, (3 * a_heads + b_heads) // 2 + 2]
    plain = ((e[3], e[4]), (e[5], e[6]))
    cos_tab, sin_tab, avg, perm = _rope_tables(l)

    def type_of(j0):
        def f(j):
            jj = j + j0
            is_a = (jj < e[1]) | ((jj >= e[2]) & (jj < e[3]))
            return jnp.where(is_a, 0, 1)
        return f

    w_bf = w_in.astype(BF16)
    p = _inproj(h.reshape(b * l, d), w_bf, gain, scale, cos_tab, sin_tab, avg, perm, plain=plain,
                type_of_block=type_of(0), col0=0, ncols=in_cols, l=l, rope=True).reshape(b, l, in_cols)
    kv0 = e[2]
    plain_c = tuple((lo - kv0, hi - kv0) for lo, hi in plain)
    pc = _inproj(hc.reshape(b * c, d), w_bf, gain, scale, cos_tab, sin_tab, avg, perm, plain=plain_c,
                 type_of_block=type_of(kv0), col0=kv0, ncols=in_cols - q_cols, l=c, rope=False
                 ).reshape(b, c, in_cols - q_cols)
    score_bound = (A_HALF ** 0.5 * math.log2(math.e)) * jnp.max(jnp.abs(a_q_g)) * jnp.max(jnp.abs(a_k_g))
    oa = _diffattn(lam, p, pc, a_sub_g, score_bound, heads=a_heads, l=l, c=c, out_scale=1.0 - lam_init)
    ob = _winattn(sink, p, pc, a_heads=a_heads, b_heads=b_heads, l=l, c=c)
    w_out_bf = w_out.astype(BF16)
    na = a_heads * HEAD_DIM
    return [oa, ob], [w_out_bf[:na], w_out_bf[na:]]


def kernel(x, c, ctx, c_ctx, ada_w, ada_b, norm1_g, norm2_g, attn_w_in, attn_w_out, a_q_g, a_k_g, a_lam_q1, a_lam_k1, a_lam_q2, a_lam_k2, a_sub_g, b_q_g, b_k_g, b_sink, hy_w_in, hy_b_in, hy_sconv_w, hy_sconv_b, hy_f_w0, hy_f_b0, hy_f_w1, hy_f_b1, hy_f_w2, hy_f_b2, hy_f_freq, hy_f_w3, hy_d, hy_w_out, ffn_w_gate, ffn_w_val, ffn_conv_w, ffn_conv_b, ffn_w_down):
    b, l, d = x.shape
    depth = ada_w.shape[0]
    cc = jnp.concatenate([c, c_ctx[None, :], jnp.zeros((8 - b - 1, d), F32)], axis=0)
    m = _ada(cc, ada_w, ada_b)
    xs = x
    for layer in range(depth):
        i = layer // 2
        lat = [m[layer, :b, k * d:(k + 1) * d].reshape(b, 1, d) for k in range(6)]
        sh1, sc1, g1, sh2, sc2, g2 = lat
        h = _normmod(xs, norm1_g[layer], sc1, sh1)
        if layer % 2 == 0:
            mc = [jnp.broadcast_to(m[layer, b, k * d:(k + 1) * d].reshape(1, 1, d), (b, 1, d)) for k in range(2)]
            hc = _normmod(ctx, norm1_g[layer], mc[1], mc[0])
            parts, ws = _attn_mixer(h, hc, 0.8 - 0.6 * math.exp(-0.3 * layer), attn_w_in[i], attn_w_out[i],
                                    a_q_g[i], a_k_g[i], a_lam_q1[i], a_lam_k1[i], a_lam_q2[i], a_lam_k2[i],
                                    a_sub_g[i], b_q_g[i], b_k_g[i], b_sink[i])
        else:
            hy = (hy_w_in[i], hy_b_in[i], hy_sconv_w[i], hy_sconv_b[i], hy_f_w0[i], hy_f_b0[i], hy_f_w1[i],
                  hy_f_b1[i], hy_f_w2[i], hy_f_b2[i], hy_f_freq[i], hy_f_w3[i], hy_d[i], hy_w_out[i])
            y, w_o = _hyena_mixer(h, hy, l, d)
            parts, ws = [y], [w_o]
        xs = _mm_resid(parts, ws, xs, g1)
        h2 = _normmod(xs, norm2_g[layer], sc2, sh2)
        d_ff = ffn_w_gate.shape[2]
        hid = _mm_conv(_ffn_up_kernel, h2, [ffn_w_gate[layer].astype(BF16), ffn_w_val[layer].astype(BF16)],
                       [ffn_conv_w[layer], ffn_conv_b[layer].reshape(1, -1)], d_ff, BF16, "ffn_up_conv_glu", nz=2)
        xs = _mm_resid([hid], [ffn_w_down[layer].astype(BF16)], xs, g2)
    return xs
```

```python
import functools
import math

import numpy as np
import jax
import jax.numpy as jnp
from jax import lax
from jax.experimental import pallas as pl
from jax.experimental.pallas import tpu as pltpu

F32 = jnp.float32
BF16 = jnp.bfloat16
HIGHEST = lax.Precision.HIGHEST

HEAD_DIM = 128
A_HALF = HEAD_DIM // 2
B_KV_HEADS = 2
GRID_W = 64
WINDOW = 128
ROPE_BASE = 10000.0
EPS = 1e-6
NEG = -1e30
HY_ORDER = 2
HY_DECAY_TARGET = 1e-2
HY_MAX_DECAY_PCT = 0.3
HY_MIN_DECAY_PCT = 1.5

LANES = 128
BF16_SUBLANES = 16
FFT_N1 = 256
VMEM_LIMIT_BYTES = 56 * 1024 * 1024


def _params(*sem):
    return pltpu.CompilerParams(dimension_semantics=sem, vmem_limit_bytes=VMEM_LIMIT_BYTES)


def _col_tiles(w, tn):
    k, n = w.shape
    return jnp.transpose(w.reshape(k, n // tn, tn), (1, 0, 2))


def _ada_kernel(c_ref, w_ref, b_ref, o_ref):
    c = c_ref[...]
    s = c * (1.0 / (1.0 + jnp.exp(-c)))
    o_ref[0] = jnp.dot(s, w_ref[0], preferred_element_type=F32, precision=HIGHEST) + b_ref[0]


def _ada(cc, ada_w, ada_b):
    depth, d, n = ada_w.shape
    tn = 1024
    return pl.pallas_call(
        _ada_kernel,
        grid=(depth, n // tn),
        in_specs=[pl.BlockSpec((8, d), lambda l, j: (0, 0)),
                  pl.BlockSpec((1, d, tn), lambda l, j: (l, 0, j)),
                  pl.BlockSpec((1, 1, tn), lambda l, j: (l, 0, j))],
        out_specs=pl.BlockSpec((1, 8, tn), lambda l, j: (l, 0, j)),
        out_shape=jax.ShapeDtypeStruct((depth, 8, n), F32),
        compiler_params=_params("parallel", "parallel"),
        name="ada_modulation",
    )(cc, ada_w, ada_b.reshape(depth, 1, n))


def _normmod_kernel(x_ref, g_ref, sc_ref, sh_ref, o_ref):
    x = x_ref[0]
    ms = jnp.mean(x * x, axis=-1, keepdims=True)
    y = x * lax.rsqrt(ms + EPS) * g_ref[...]
    o_ref[0] = (y * (1.0 + sc_ref[0]) + sh_ref[0]).astype(o_ref.dtype)


def _normmod(x, g, sc, sh):
    b, l, d = x.shape
    tm = min(512, l)
    return pl.pallas_call(
        _normmod_kernel,
        grid=(b, l // tm),
        in_specs=[pl.BlockSpec((1, tm, d), lambda bi, i: (bi, i, 0)),
                  pl.BlockSpec((1, d), lambda bi, i: (0, 0)),
                  pl.BlockSpec((1, 1, d), lambda bi, i: (bi, 0, 0)),
                  pl.BlockSpec((1, 1, d), lambda bi, i: (bi, 0, 0))],
        out_specs=pl.BlockSpec((1, tm, d), lambda bi, i: (bi, i, 0)),
        out_shape=jax.ShapeDtypeStruct((b, l, d), BF16),
        compiler_params=_params("parallel", "parallel"),
        name="rmsnorm_modulate",
    )(x, g.reshape(1, d), sc, sh)


def _dot_split(x, m):
    hi = x.astype(BF16)
    lo = (x - hi.astype(F32)).astype(BF16)
    return jnp.dot(hi, m, preferred_element_type=F32) + jnp.dot(lo, m, preferred_element_type=F32)


def _inproj_kernel(h_ref, w_ref, gain_ref, scale_ref, cos_ref, sin_ref, avg_ref, perm_ref, o_ref, *, plain, rope):
    j = pl.program_id(1)
    z = jnp.dot(h_ref[...], w_ref[0], preferred_element_type=F32)
    is_plain = functools.reduce(jnp.logical_or, [(j >= lo) & (j < hi) for lo, hi in plain])

    @pl.when(is_plain)
    def _():
        o_ref[...] = z.astype(o_ref.dtype)

    @pl.when(jnp.logical_not(is_plain))
    def _():
        ms = _dot_split(z * z, avg_ref[0])
        y = z * lax.rsqrt(ms + EPS) * gain_ref[...]
        if rope:
            y = y * cos_ref[0] + _dot_split(y, perm_ref[0]) * sin_ref[0]
        o_ref[...] = (y * scale_ref[...]).astype(o_ref.dtype)


def _inproj(h2d, w, gain, scale, cos_tab, sin_tab, avg, perm, *, plain, type_of_block, col0, ncols, l, rope):
    m, d = h2d.shape
    tn = 2 * HEAD_DIM
    tm = min(1024, l)
    pos_blocks = l // tm
    kern = functools.partial(_inproj_kernel, plain=plain, rope=rope)
    by_type = lambda i, j: (type_of_block(j), 0, 0)
    return pl.pallas_call(
        kern,
        grid=(m // tm, ncols // tn),
        in_specs=[pl.BlockSpec((tm, d), lambda i, j: (i, 0)),
                  pl.BlockSpec((1, d, tn), lambda i, j: (j + col0, 0, 0)),
                  pl.BlockSpec((1, tn), lambda i, j: (0, j + col0)),
                  pl.BlockSpec((1, tn), lambda i, j: (0, j + col0)),
                  pl.BlockSpec((1, tm, tn), lambda i, j: (type_of_block(j), i % pos_blocks, 0)),
                  pl.BlockSpec((1, tm, tn), lambda i, j: (type_of_block(j), i % pos_blocks, 0)),
                  pl.BlockSpec((1, tn, tn), by_type),
                  pl.BlockSpec((1, tn, tn), by_type)],
        out_specs=pl.BlockSpec((tm, tn), lambda i, j: (i, j)),
        out_shape=jax.ShapeDtypeStruct((m, ncols), BF16),
        compiler_params=_params("parallel", "arbitrary"),
        name="attn_in_projection",
    )(h2d, w, gain, scale, cos_tab, sin_tab, avg, perm)


def _diffattn_kernel(lam_ref, q_ref, kc_ref, vc_ref, k_ref, v_ref, g_ref, o_ref,
                     q2_s, m_s, l_s, acc_s, *, tq, nkv, out_scale):
    kv = pl.program_id(3)

    def process(k, v):
        s = lax.dot_general(q2_s[...], k, (((1,), (1,)), ((), ())), preferred_element_type=F32)
        m_prev = m_s[...]
        m_new = jnp.maximum(m_prev, jnp.max(s, axis=-1, keepdims=True))
        alpha = jnp.exp2(m_prev - m_new)
        p = jnp.exp2(s - m_new)
        l_s[...] = alpha * l_s[...] + jnp.sum(p, axis=-1, keepdims=True)
        acc_s[...] = alpha * acc_s[...] + jnp.dot(p.astype(BF16), v, preferred_element_type=F32)
        m_s[...] = m_new

    @pl.when(kv == 0)
    def _():
        q = q_ref[0]
        lane = lax.broadcasted_iota(jnp.int32, q.shape, 1)
        zero = jnp.zeros_like(q)
        q2_s[0:tq, :] = jnp.where(lane < A_HALF, q, zero)
        q2_s[tq:2 * tq, :] = jnp.where(lane >= A_HALF, q, zero)
        m_s[...] = jnp.full(m_s.shape, -jnp.inf, F32)
        l_s[...] = jnp.zeros(l_s.shape, F32)
        acc_s[...] = jnp.zeros(acc_s.shape, F32)
        process(kc_ref[0], vc_ref[0])

    process(k_ref[0], v_ref[0])

    @pl.when(kv == nkv - 1)
    def _():
        o = acc_s[...] / l_s[...]
        d = o[0:tq] - lam_ref[0, 0] * o[tq:2 * tq]
        ms = jnp.mean(d * d, axis=-1, keepdims=True)
        o_ref[0] = (d * lax.rsqrt(ms + EPS) * g_ref[...] * out_scale).astype(o_ref.dtype)


def _diffattn_fast_kernel(lam_ref, q_ref, kc_ref, vct_ref, k_ref, vt_ref, g_ref, o_ref,
                          q2_s, l_s, acc_s, p_s, *, tq, tk, kc, nkv, out_scale):
    kv = pl.program_id(3)
    dn = (((1,), (1,)), ((), ()))

    def weights(k):
        n = k.shape[0]
        p = jnp.exp2(lax.dot_general(k, q2_s[...], dn, preferred_element_type=F32))
        return p.astype(BF16), jnp.sum(p.reshape(n // 8, 8, 2 * tq), axis=0)

    @pl.when(kv == 0)
    def _():
        q = q_ref[0]
        lane = lax.broadcasted_iota(jnp.int32, q.shape, 1)
        zero = jnp.zeros_like(q)
        q2_s[0:tq, :] = jnp.where(lane < A_HALF, q, zero)
        q2_s[tq:2 * tq, :] = jnp.where(lane >= A_HALF, q, zero)
        p, ls = weights(kc_ref[0])
        l_s[...] = ls
        acc_s[...] = jnp.dot(vct_ref[0, 0], p, preferred_element_type=F32)

    ls = l_s[...]
    for ci in range(tk // kc):
        p, lc = weights(k_ref[0, ci * kc:(ci + 1) * kc, :])
        p_s[ci * kc:(ci + 1) * kc, :] = p
        ls = ls + lc
    l_s[...] = ls
    acc_s[...] += jnp.dot(vt_ref[0, 0], p_s[...], preferred_element_type=F32)

    @pl.when(kv == nkv - 1)
    def _():
        l = jnp.sum(l_s[...], axis=0, keepdims=True)
        o = acc_s[...] / l
        d = o[:, 0:tq] - lam_ref[0, 0] * o[:, tq:2 * tq]
        ms = jnp.mean(d * d, axis=0, keepdims=True)
        y = d * lax.rsqrt(ms + EPS) * g_ref[...] * out_scale
        o_ref[0] = y.T.astype(o_ref.dtype)


def _diffattn_fast(lam, p, pc, sub_g, *, heads, l, c, out_scale):
    b = p.shape[0]
    tq = min(512, l)
    tk = min(4096, l)
    kc = min(512, tk)
    nkv = l // tk
    k_blk = 2 * heads
    hd = heads * HEAD_DIM
    vt = jnp.transpose(p[:, :, 3 * hd:4 * hd].reshape(b, l, heads, HEAD_DIM), (0, 2, 3, 1))
    vct = jnp.transpose(pc[:, :, hd:2 * hd].reshape(b, c, heads, HEAD_DIM), (0, 2, 3, 1))
    kern = functools.partial(_diffattn_fast_kernel, tq=tq, tk=tk, kc=kc, nkv=nkv, out_scale=out_scale)
    return pl.pallas_call(
        kern,
        grid=(b, heads, l // tq, nkv),
        in_specs=[pl.BlockSpec(memory_space=pltpu.SMEM),
                  pl.BlockSpec((1, tq, HEAD_DIM), lambda bi, h, i, kv: (bi, i, h)),
                  pl.BlockSpec((1, c, HEAD_DIM), lambda bi, h, i, kv: (bi, 0, h)),
                  pl.BlockSpec((1, 1, HEAD_DIM, c), lambda bi, h, i, kv: (bi, h, 0, 0)),
                  pl.BlockSpec((1, tk, HEAD_DIM), lambda bi, h, i, kv: (bi, kv, k_blk + h)),
                  pl.BlockSpec((1, 1, HEAD_DIM, tk), lambda bi, h, i, kv: (bi, h, 0, kv)),
                  pl.BlockSpec((HEAD_DIM, 1), lambda bi, h, i, kv: (0, 0))],
        out_specs=pl.BlockSpec((1, tq, HEAD_DIM), lambda bi, h, i, kv: (bi, i, h)),
        out_shape=jax.ShapeDtypeStruct((b, l, hd), BF16),
        scratch_shapes=[pltpu.VMEM((2 * tq, HEAD_DIM), BF16),
                        pltpu.VMEM((8, 2 * tq), F32),
                        pltpu.VMEM((HEAD_DIM, 2 * tq), F32),
                        pltpu.VMEM((tk, 2 * tq), BF16)],
        compiler_params=_params("parallel", "parallel", "parallel", "arbitrary"),
        name="diff_attention_bounded",
    )(lam, p, pc, vct, p, vt, sub_g.reshape(HEAD_DIM, 1))


SCORE_BOUND_LOG2 = 60.0


def _diffattn(lam, p, pc, sub_g, score_bound, *, heads, l, c, out_scale):
    kw = dict(heads=heads, l=l, c=c, out_scale=out_scale)
    return lax.cond(score_bound < SCORE_BOUND_LOG2,
                    lambda *a: _diffattn_fast(*a, **kw), lambda *a: _diffattn_safe(*a, **kw),
                    lam, p, pc, sub_g)


def _diffattn_safe(lam, p, pc, sub_g, *, heads, l, c, out_scale):
    b = p.shape[0]
    tq = min(256, l)
    tk = min(512, l)
    nkv = l // tk
    q_blk = 0
    k_blk = 2 * heads
    v_blk = 3 * heads
    kern = functools.partial(_diffattn_kernel, tq=tq, nkv=nkv, out_scale=out_scale)
    return pl.pallas_call(
        kern,
        grid=(b, heads, l // tq, nkv),
        in_specs=[pl.BlockSpec(memory_space=pltpu.SMEM),
                  pl.BlockSpec((1, tq, HEAD_DIM), lambda bi, h, i, kv: (bi, i, q_blk + h)),
                  pl.BlockSpec((1, c, HEAD_DIM), lambda bi, h, i, kv: (bi, 0, h)),
                  pl.BlockSpec((1, c, HEAD_DIM), lambda bi, h, i, kv: (bi, 0, heads + h)),
                  pl.BlockSpec((1, tk, HEAD_DIM), lambda bi, h, i, kv: (bi, kv, k_blk + h)),
                  pl.BlockSpec((1, tk, HEAD_DIM), lambda bi, h, i, kv: (bi, kv, v_blk + h)),
                  pl.BlockSpec((1, HEAD_DIM), lambda bi, h, i, kv: (0, 0))],
        out_specs=pl.BlockSpec((1, tq, HEAD_DIM), lambda bi, h, i, kv: (bi, i, h)),
        out_shape=jax.ShapeDtypeStruct((b, l, heads * HEAD_DIM), BF16),
        scratch_shapes=[pltpu.VMEM((2 * tq, HEAD_DIM), BF16),
                        pltpu.VMEM((2 * tq, 1), F32),
                        pltpu.VMEM((2 * tq, 1), F32),
                        pltpu.VMEM((2 * tq, HEAD_DIM), F32)],
        compiler_params=_params("parallel", "parallel", "parallel", "arbitrary"),
        name="diff_attention",
    )(lam, p, pc, pc, p, p, sub_g.reshape(1, HEAD_DIM))


def _winattn_kernel(sink_ref, q_ref, kp_ref, km_ref, kn_ref, vp_ref, vm_ref, vn_ref, kc_ref, vc_ref,
                    o_ref, *, tq, l, group):
    g = pl.program_id(1)
    i = pl.program_id(2)
    kband = jnp.concatenate([kp_ref[0], km_ref[0], kn_ref[0]], axis=0)
    vband = jnp.concatenate([vp_ref[0], vm_ref[0], vn_ref[0]], axis=0)
    nk = tq + 2 * WINDOW
    qpos = i * tq + lax.broadcasted_iota(jnp.int32, (tq, nk), 0)
    kpos = i * tq - WINDOW + lax.broadcasted_iota(jnp.int32, (tq, nk), 1)
    valid = (jnp.abs(kpos - qpos) <= WINDOW) & (kpos >= 0) & (kpos < l)
    kc = kc_ref[0]
    vc = vc_ref[0]
    dn = (((1,), (1,)), ((), ()))
    for r in range(group):
        sl = slice(r * HEAD_DIM, (r + 1) * HEAD_DIM)
        q = q_ref[0, :, sl]
        s_lat = jnp.where(valid, lax.dot_general(q, kband, dn, preferred_element_type=F32), NEG)
        s_ctx = lax.dot_general(q, kc, dn, preferred_element_type=F32)
        sk = sink_ref[g, r]
        m = jnp.maximum(jnp.maximum(jnp.max(s_lat, axis=-1, keepdims=True),
                                    jnp.max(s_ctx, axis=-1, keepdims=True)), sk)
        p_lat = jnp.exp(s_lat - m)
        p_ctx = jnp.exp(s_ctx - m)
        denom = (jnp.sum(p_lat, axis=-1, keepdims=True) + jnp.sum(p_ctx, axis=-1, keepdims=True)
                 + jnp.exp(sk - m))
        o = (jnp.dot(p_lat.astype(BF16), vband, preferred_element_type=F32)
             + jnp.dot(p_ctx.astype(BF16), vc, preferred_element_type=F32))
        o_ref[0, :, sl] = (o / denom).astype(o_ref.dtype)


def _winattn(sink, p, pc, *, a_heads, b_heads, l, c):
    b = p.shape[0]
    group = b_heads // B_KV_HEADS
    tq = min(256, l)
    wpb = tq // WINDOW
    nwb = l // WINDOW
    gw = group * HEAD_DIM
    q_blk = a_heads * HEAD_DIM // gw
    k_blk = (a_heads + b_heads) + 2 * a_heads
    v_blk = k_blk + B_KV_HEADS
    kc_blk = 2 * a_heads
    vc_blk = kc_blk + B_KV_HEADS
    kern = functools.partial(_winattn_kernel, tq=tq, l=l, group=group)

    def prev_map(col):
        return lambda bi, g, i: (bi, jnp.maximum(i * wpb - 1, 0), col + g)

    def main_map(col):
        return lambda bi, g, i: (bi, i, col + g)

    def next_map(col):
        return lambda bi, g, i: (bi, jnp.minimum((i + 1) * wpb, nwb - 1), col + g)

    return pl.pallas_call(
        kern,
        grid=(b, B_KV_HEADS, l // tq),
        in_specs=[pl.BlockSpec(memory_space=pltpu.SMEM),
                  pl.BlockSpec((1, tq, gw), lambda bi, g, i: (bi, i, q_blk + g)),
                  pl.BlockSpec((1, WINDOW, HEAD_DIM), prev_map(k_blk)),
                  pl.BlockSpec((1, tq, HEAD_DIM), main_map(k_blk)),
                  pl.BlockSpec((1, WINDOW, HEAD_DIM), next_map(k_blk)),
                  pl.BlockSpec((1, WINDOW, HEAD_DIM), prev_map(v_blk)),
                  pl.BlockSpec((1, tq, HEAD_DIM), main_map(v_blk)),
                  pl.BlockSpec((1, WINDOW, HEAD_DIM), next_map(v_blk)),
                  pl.BlockSpec((1, c, HEAD_DIM), lambda bi, g, i: (bi, 0, kc_blk + g)),
                  pl.BlockSpec((1, c, HEAD_DIM), lambda bi, g, i: (bi, 0, vc_blk + g))],
        out_specs=pl.BlockSpec((1, tq, gw), lambda bi, g, i: (bi, i, g)),
        out_shape=jax.ShapeDtypeStruct((b, l, b_heads * HEAD_DIM), BF16),
        compiler_params=_params("parallel", "parallel", "parallel"),
        name="window_attention",
    )(sink, p, p, p, p, p, p, p, pc, pc)


def _mm_resid_kernel(*refs, nparts):
    a_refs = refs[0:nparts]
    w_refs = refs[nparts:2 * nparts]
    res_ref, gate_ref, o_ref = refs[2 * nparts:]
    y = jnp.dot(a_refs[0][0].astype(BF16), w_refs[0][0], preferred_element_type=F32)
    for a_ref, w_ref in zip(a_refs[1:], w_refs[1:]):
        y = y + jnp.dot(a_ref[0].astype(BF16), w_ref[0], preferred_element_type=F32)
    o_ref[0] = res_ref[0] + gate_ref[0] * y


def _mm_resid(parts, ws, res, gate):
    b, l, n = res.shape
    tm = min(1024, l)
    tn = min(512, n)
    nparts = len(parts)
    ws = [_col_tiles(w, tn) for w in ws]
    in_specs = ([pl.BlockSpec((1, tm, a.shape[2]), lambda bi, i, j: (bi, i, 0)) for a in parts]
                + [pl.BlockSpec((1, w.shape[1], tn), lambda bi, i, j: (j, 0, 0)) for w in ws]
                + [pl.BlockSpec((1, tm, tn), lambda bi, i, j: (bi, i, j)),
                   pl.BlockSpec((1, 1, tn), lambda bi, i, j: (bi, 0, j))])
    return pl.pallas_call(
        functools.partial(_mm_resid_kernel, nparts=nparts),
        grid=(b, l // tm, n // tn),
        in_specs=in_specs,
        out_specs=pl.BlockSpec((1, tm, tn), lambda bi, i, j: (bi, i, j)),
        out_shape=jax.ShapeDtypeStruct((b, l, n), F32),
        compiler_params=_params("parallel", "parallel", "arbitrary"),
        name="out_projection_gated_residual",
    )(*parts, *ws, res, gate)


HALO = BF16_SUBLANES


def _fill_rows(hbuf, hp_ref, hm_ref, hn_ref, tm):
    hbuf[0:HALO, :] = hp_ref[0]
    hbuf[HALO:HALO + tm, :] = hm_ref[0]
    hbuf[HALO + tm:2 * HALO + tm, :] = hn_ref[0]


def _conv3(zbuf, cw_ref, cb_ref, tm, first, last):
    lo = slice(HALO - 1, HALO)
    hi = slice(HALO + tm, HALO + tm + 1)
    zbuf[lo, :] = jnp.where(first, 0.0, zbuf[lo, :])
    zbuf[hi, :] = jnp.where(last, 0.0, zbuf[hi, :])
    z = zbuf[...]
    rows = z.shape[0]
    g = (z * cw_ref[1:2, :] + pltpu.roll(z * cw_ref[0:1, :], 1, 0) + pltpu.roll(z * cw_ref[2:3, :], rows - 1, 0))
    return g[HALO:HALO + tm] + cb_ref[...]


def _gelu_tanh(x):
    return 0.5 * x * (1.0 + jnp.tanh(math.sqrt(2.0 / math.pi) * (x + 0.044715 * (x * x * x))))


def _ffn_up_kernel(hp_ref, hm_ref, hn_ref, wg_ref, wv_ref, cw_ref, cb_ref, o_ref, hbuf, zbuf, zbuf2, *, tm, nt):
    i = pl.program_id(1)
    j = pl.program_id(2)

    @pl.when(j == 0)
    def _():
        _fill_rows(hbuf, hp_ref, hm_ref, hn_ref, tm)

    hw = zbuf.shape[1]
    for hh, zb in enumerate((zbuf, zbuf2)):
        cs = slice(hh * hw, (hh + 1) * hw)
        zb[...] = jnp.dot(hbuf[...], wg_ref[0, :, cs], preferred_element_type=F32)
        g = _conv3(zb, cw_ref.at[:, cs], cb_ref.at[:, cs], tm, i == 0, i == nt - 1)
        v = jnp.dot(hbuf[HALO:HALO + tm, :], wv_ref[0, :, cs], preferred_element_type=F32)
        o_ref[0, :, cs] = (_gelu_tanh(g) * v).astype(o_ref.dtype)


def _hy_in_kernel(hp_ref, hm_ref, hn_ref, w_ref, b_ref, cw_ref, cb_ref, o_ref, hbuf, zbuf, *, tm, nt):
    i = pl.program_id(1)
    j = pl.program_id(2)

    @pl.when(j == 0)
    def _():
        _fill_rows(hbuf, hp_ref, hm_ref, hn_ref, tm)

    zbuf[...] = jnp.dot(hbuf[...], w_ref[0], preferred_element_type=F32) + b_ref[...]
    o_ref[0] = _conv3(zbuf, cw_ref, cb_ref, tm, i == 0, i == nt - 1).astype(o_ref.dtype)


def _mm_conv(kernel, h, ws, vecs, n, out_dtype, name, nz=1):
    b, l, k = h.shape
    tm = min(512, l)
    tn = min(512, n)
    nt = l // tm
    hpb = tm // HALO
    nhb = l // HALO
    ws = [_col_tiles(w, tn) for w in ws]
    in_specs = ([pl.BlockSpec((1, HALO, k), lambda bi, i, j: (bi, jnp.maximum(i * hpb - 1, 0), 0)),
                 pl.BlockSpec((1, tm, k), lambda bi, i, j: (bi, i, 0)),
                 pl.BlockSpec((1, HALO, k), lambda bi, i, j: (bi, jnp.minimum((i + 1) * hpb, nhb - 1), 0))]
                + [pl.BlockSpec((1, k, tn), lambda bi, i, j: (j, 0, 0)) for _ in ws]
                + [pl.BlockSpec((v.shape[0], tn), lambda bi, i, j: (0, j)) for v in vecs])
    return pl.pallas_call(
        functools.partial(kernel, tm=tm, nt=nt),
        grid=(b, nt, n // tn),
        in_specs=in_specs,
        out_specs=pl.BlockSpec((1, tm, tn), lambda bi, i, j: (bi, i, j)),
        out_shape=jax.ShapeDtypeStruct((b, l, n), out_dtype),
        scratch_shapes=[pltpu.VMEM((tm + 2 * HALO, k), BF16)] + [pltpu.VMEM((tm + 2 * HALO, tn // nz), F32)] * nz,
        compiler_params=_params("parallel", "parallel", "arbitrary"),
        name=name,
    )(h, h, h, *ws, *vecs)


FILT_NB = BF16_SUBLANES


def _filter_dft_kernel(z_ref, w0_ref, b0_ref, w1_ref, b1_ref, w2_ref, b2_ref, fr_ref, w3_ref, dl_ref, t_ref,
                       o_ref, s_ref, a3_s, r_s):
    m, seg, j = pl.program_id(0), pl.program_id(1), pl.program_id(2)
    tn = o_ref.shape[3]
    fr = fr_ref[...]

    def layer(a, w_ref, b_ref):
        return jnp.sin(fr * (jnp.dot(a, w_ref[...], preferred_element_type=F32, precision=HIGHEST)
                             + b_ref[...]))

    @pl.when((seg == 0) & (j == 0))
    def _():
        for jj in range(FILT_NB):
            a = layer(layer(layer(z_ref[jj], w0_ref, b0_ref), w1_ref, b1_ref), w2_ref, b2_ref)
            a_hi = a.astype(BF16)
            a_lo = (a - a_hi.astype(F32)).astype(BF16)
            a3_s[jj] = jnp.concatenate([a_hi, a_hi, a_lo], axis=1)

    @pl.when((m == 0) & (seg == 0) & (j == 0))
    def _():
        s_ref[...] = jnp.zeros(s_ref.shape, F32)

    dl = dl_ref[...]
    backward = seg % 2 == 1
    abs_sum = jnp.zeros((1, tn), F32)
    for jj in range(FILT_NB):
        h = (jnp.dot(a3_s[jj], w3_ref[0], preferred_element_type=F32)
             * jnp.exp(-z_ref[jj][:, 0:1] * dl))
        if jj == 0:
            row = lax.broadcasted_iota(jnp.int32, h.shape, 0)
            h = jnp.where(backward & (m == 0) & (row == 0), 0.0, h)
        abs_sum = abs_sum + jnp.sum(jnp.abs(h), axis=0, keepdims=True)
        r_s[jj] = jnp.dot(t_ref[jj], h.astype(BF16), preferred_element_type=F32)
    o_ref[0] = jnp.swapaxes(r_s[...], 0, 1).astype(o_ref.dtype)
    col = pl.multiple_of(j * tn, tn)
    s_ref[seg, :, pl.ds(col, tn)] += abs_sum


def _hyena_filter_spectra_a(l, d, ta, f_w0, f_b0, f_w1, f_b1, f_w2, f_b2, f_freq, f_w3):
    emb, fw = f_w0.shape
    bands = (emb - 1) // 2
    half = FFT_N1 // 2
    n2 = l // half
    t = np.linspace(0.0, 1.0, l)[:, None]
    w = 2.0 * math.pi * np.arange(l)[:, None] / l
    f = np.linspace(1e-4, bands - 1, bands)[None, :]
    emb_pad = -(-emb // 8) * 8
    z = np.concatenate([t, np.cos(f * w), -np.sin(f * w), np.zeros((l, emb_pad - emb))], axis=-1)
    z = jnp.asarray(z.reshape(half, n2, emb_pad).transpose(1, 0, 2), F32)
    w0 = jnp.concatenate([f_w0, jnp.zeros((emb_pad - emb, fw), F32)], axis=0)
    max_decay = math.log(HY_DECAY_TARGET) / HY_MAX_DECAY_PCT
    min_decay = math.log(HY_DECAY_TARGET) / HY_MIN_DECAY_PCT
    dl = jnp.asarray(np.abs(np.linspace(min_decay, max_decay, d))[None, :], F32)
    nseg = 2 * HY_ORDER
    w3 = jnp.transpose(f_w3.reshape(fw, nseg, d), (1, 0, 2))
    w3_hi = w3.astype(BF16)
    w3_lo = (w3 - w3_hi.astype(F32)).astype(BF16)
    w3 = jnp.concatenate([w3_hi, w3_lo, w3_hi], axis=1)
    nb = min(FILT_NB, n2)
    assert nb == FILT_NB
    tn = min(256, d)
    vec = lambda a: a.reshape(1, fw)
    small = lambda shape: pl.BlockSpec(shape, lambda m, sg, j: (0,) * len(shape))
    return pl.pallas_call(
        _filter_dft_kernel,
        grid=(n2 // nb, nseg, d // tn),
        in_specs=[pl.BlockSpec((nb, half, emb_pad), lambda m, sg, j: (m, 0, 0)),
                  small((emb_pad, fw)), small((1, fw)), small((fw, fw)), small((1, fw)),
                  small((fw, fw)), small((1, fw)), small((1, fw)),
                  pl.BlockSpec((1, 3 * fw, tn), lambda m, sg, j: (sg, 0, j)),
                  pl.BlockSpec((1, tn), lambda m, sg, j: (0, j)),
                  pl.BlockSpec((nb, 2 * FFT_N1, half), lambda m, sg, j: (m, 0, 0))],
        out_specs=[pl.BlockSpec((1, 2 * FFT_N1, nb, tn), lambda m, sg, j: (sg, 0, m, j)),
                   pl.BlockSpec((nseg, 1, d), lambda m, sg, j: (0, 0, 0))],
        out_shape=[jax.ShapeDtypeStruct((nseg, 2 * FFT_N1, n2, d), BF16),
                   jax.ShapeDtypeStruct((nseg, 1, d), F32)],
        scratch_shapes=[pltpu.VMEM((nb, half, 3 * fw), BF16),
                        pltpu.VMEM((nb, 2 * FFT_N1, tn), F32)],
        compiler_params=_params("arbitrary", "arbitrary", "arbitrary"),
        name="hyena_filter_dft_stage_a",
    )(z, w0, vec(f_b0), f_w1, vec(f_b1), f_w2, vec(f_b2), vec(f_freq), w3, dl, ta)


def _dft_tables(l):
    n1 = FFT_N1
    n = 2 * l
    n2 = n // n1
    k1 = np.arange(n1)[:, None]
    f1 = np.exp(-2j * np.pi * k1 * np.arange(n1 // 2)[None, :] / n1)
    tw = np.exp(-2j * np.pi * np.arange(n2)[:, None] * np.arange(n1)[None, :] / n)
    f1r, f1i = jnp.asarray(f1.real, F32), jnp.asarray(f1.imag, F32)
    twr, twi = jnp.asarray(tw.real, F32)[:, :, None], jnp.asarray(tw.imag, F32)[:, :, None]
    fr = twr * f1r - twi * f1i
    fi = twr * f1i + twi * f1r
    ta = jnp.concatenate([jnp.concatenate([fr, -fi], axis=2),
                          jnp.concatenate([fi, fr], axis=2)], axis=1)
    fb = np.exp(-2j * np.pi * np.arange(n2)[:, None] * np.arange(n2)[None, :] / n2)
    tb = np.block([[fb.real, -fb.imag], [fb.imag, fb.real]])
    return (ta.astype(BF16), jnp.swapaxes(ta, 1, 2).astype(BF16),
            jnp.asarray(tb, BF16), jnp.asarray(tb.T, BF16))


FFT_NB = BF16_SUBLANES


def _fft_a_kernel(x_ref, t_ref, o_ref, r_s):
    planes = x_ref.shape[1]
    xt = [jnp.swapaxes(x_ref[0, c], 0, 1) for c in range(planes)]
    for jj in range(FFT_NB):
        x = jnp.concatenate([xt[c][jj] for c in range(planes)], axis=0).astype(BF16)
        r_s[jj] = jnp.dot(t_ref[jj], x, preferred_element_type=F32)
    o_ref[0] = jnp.swapaxes(r_s[...], 0, 1).astype(o_ref.dtype)


def _fft_a(x5, ta, *, n2, d, seg):
    s, p, half, _, _ = x5.shape
    kdim = p * half
    tn = min(256, d)
    sb, db = (seg * d) // tn, d // tn
    return pl.pallas_call(
        _fft_a_kernel,
        grid=(s, n2 // FFT_NB, db),
        in_specs=[pl.BlockSpec((1, p, half, FFT_NB, tn), lambda si, m, j: (si, 0, 0, m, sb + j)),
                  pl.BlockSpec((FFT_NB, 2 * FFT_N1, kdim), lambda si, m, j: (m, 0, 0))],
        out_specs=pl.BlockSpec((1, 2 * FFT_N1, FFT_NB, tn), lambda si, m, j: (si, 0, m, j)),
        out_shape=jax.ShapeDtypeStruct((s, 2 * FFT_N1, n2, d), BF16),
        scratch_shapes=[pltpu.VMEM((FFT_NB, 2 * FFT_N1, tn), F32)],
        compiler_params=_params("parallel", "parallel", "parallel"),
        name="dft_stage_a",
    )(x5, ta)


def _fft_bfilt_kernel(af_ref, ab_ref, tb_ref, sc_ref, o_ref, *, n2):
    tb = tb_ref[...]
    hf = jnp.dot(tb, af_ref[0].reshape(2 * n2, -1).astype(BF16), preferred_element_type=F32)
    hb = jnp.dot(tb, ab_ref[0].reshape(2 * n2, -1).astype(BF16), preferred_element_type=F32)
    sc = sc_ref[0]
    o_ref[0, 0, 0] = (hf[0:n2] + hb[0:n2]) * sc
    o_ref[0, 1, 0] = (hf[n2:2 * n2] - hb[n2:2 * n2]) * sc


def _fft_bfilt(af, tb, scale, *, n2, d):
    tn = min(2048, d)
    blk = (1, 2, 1, n2, tn)
    return pl.pallas_call(
        functools.partial(_fft_bfilt_kernel, n2=n2),
        grid=(HY_ORDER, FFT_N1, d // tn),
        in_specs=[pl.BlockSpec(blk, lambda o, k, j: (2 * o, 0, k, 0, j)),
                  pl.BlockSpec(blk, lambda o, k, j: (2 * o + 1, 0, k, 0, j)),
                  pl.BlockSpec((2 * n2, 2 * n2), lambda o, k, j: (0, 0)),
                  pl.BlockSpec((1, 1, tn), lambda o, k, j: (o, 0, j))],
        out_specs=pl.BlockSpec(blk, lambda o, k, j: (o, 0, k, 0, j)),
        out_shape=jax.ShapeDtypeStruct((HY_ORDER, 2, FFT_N1, n2, d), F32),
        compiler_params=_params("parallel", "parallel", "parallel"),
        name="dft_stage_b_filters",
    )(af, af, tb, scale)


def _fft_b_kernel(a_ref, g_ref, tb_ref, tbt_ref, o_ref, *, n2):
    z = jnp.dot(tb_ref[...], a_ref[...].reshape(2 * n2, -1).astype(BF16), preferred_element_type=F32)
    zr, zi = z[0:n2], z[n2:2 * n2]
    gr, gi = g_ref[0, 0, 0], g_ref[0, 1, 0]
    prod = jnp.concatenate([zr * gr - zi * gi, zr * gi + zi * gr], axis=0).astype(BF16)
    y = jnp.dot(tbt_ref[...], prod, preferred_element_type=F32)
    o_ref[...] = y.reshape(o_ref.shape).astype(o_ref.dtype)


def _fft_b(a, g, tb, tbt, order, *, n2, d):
    tn = min(2048, d)
    blk = (2, 1, n2, tn)
    return pl.pallas_call(
        functools.partial(_fft_b_kernel, n2=n2),
        grid=(FFT_N1, d // tn),
        in_specs=[pl.BlockSpec(blk, lambda k, j: (0, k, 0, j)),
                  pl.BlockSpec((1,) + blk, lambda k, j: (order, 0, k, 0, j)),
                  pl.BlockSpec((2 * n2, 2 * n2), lambda k, j: (0, 0)),
                  pl.BlockSpec((2 * n2, 2 * n2), lambda k, j: (0, 0))],
        out_specs=pl.BlockSpec(blk, lambda k, j: (0, k, 0, j)),
        out_shape=jax.ShapeDtypeStruct((2, FFT_N1, n2, d), BF16),
        compiler_params=_params("parallel", "parallel"),
        name="dft_stage_b_spectrum_product",
    )(a, g, tb, tbt)


def _fft_ainv_kernel(y_ref, t_ref, u_ref, gate_ref, d_ref, o_ref, r_s):
    half = o_ref.shape[1]
    yt = jnp.swapaxes(y_ref[...].astype(F32), 0, 1).astype(BF16)
    for c in range(2):
        r_s[c] = jnp.swapaxes(u_ref[c], 0, 1) * d_ref[...]
    for jj in range(FFT_NB):
        x = jnp.dot(t_ref[jj], yt[jj], preferred_element_type=F32)
        for c in range(2):
            r_s[c, jj] += x[c * half:(c + 1) * half]
    for c in range(2):
        o_ref[c] = gate_ref[c] * jnp.swapaxes(r_s[c], 0, 1)


def _fft_ainv(y3, tat, u4, gate4, dskip, *, n2, d, useg, gseg):
    b, half, _, _ = u4.shape
    tn = min(128, d)
    db = d // tn
    usb, gsb = (useg * d) // tn, (gseg * d) // tn
    return pl.pallas_call(
        _fft_ainv_kernel,
        grid=(n2 // FFT_NB, db),
        in_specs=[pl.BlockSpec((2 * FFT_N1, FFT_NB, tn), lambda m, j: (0, m, j)),
                  pl.BlockSpec((FFT_NB, FFT_N1, 2 * FFT_N1), lambda m, j: (m, 0, 0)),
                  pl.BlockSpec((b, half, FFT_NB, tn), lambda m, j: (0, 0, m, usb + j)),
                  pl.BlockSpec((b, half, FFT_NB, tn), lambda m, j: (0, 0, m, gsb + j)),
                  pl.BlockSpec((1, tn), lambda m, j: (0, j))],
        out_specs=pl.BlockSpec((b, half, FFT_NB, tn), lambda m, j: (0, 0, m, j)),
        out_shape=jax.ShapeDtypeStruct((b, half, n2, d), F32),
        scratch_shapes=[pltpu.VMEM((b, FFT_NB, half, tn), F32)],
        compiler_params=_params("parallel", "parallel"),
        name="dft_stage_a_inverse_gate",
    )(y3, tat, u4, gate4, dskip)


def _hyena_mixer(h, hy, l, d):
    (w_in, b_in, sconv_w, sconv_b, f_w0, f_b0, f_w1, f_b1, f_w2, f_b2, f_freq, f_w3, d_skip, w_out) = hy
    b = h.shape[0]
    assert b == 2, "the two batch rows ride the real / imaginary planes of one complex DFT"
    n1 = FFT_N1
    half = n1 // 2
    n2 = (2 * l) // n1
    assert half * n2 == l
    z3 = _mm_conv(_hy_in_kernel, h, [w_in.astype(BF16)],
                  [b_in.reshape(1, -1), sconv_w, sconv_b.reshape(1, -1)], 3 * d, F32, "hyena_in_proj_conv")
    ta, tat, tb, tbt = _dft_tables(l)
    nseg = 2 * HY_ORDER
    af, fsum = _hyena_filter_spectra_a(l, d, ta, f_w0, f_b0, f_w1, f_b1, f_w2, f_b2, f_freq, f_w3)
    norm = (fsum[0::2] + fsum[1::2])
    g = _fft_bfilt(af.reshape(nseg, 2, n1, n2, d), tb, 1.0 / (norm * (2 * l)), n2=n2, d=d)
    z4 = z3.reshape(b, half, n2, 3 * d)
    y, yseg = z4, 0
    for o in range(HY_ORDER):
        a = _fft_a(y[None], ta, n2=n2, d=d, seg=yseg)
        yb = _fft_b(a.reshape(2, n1, n2, d), g, tb, tbt, o, n2=n2, d=d)
        y = _fft_ainv(yb.reshape(2 * n1, n2, d), tat, y, z4, d_skip[o].reshape(1, d),
                      n2=n2, d=d, useg=yseg, gseg=o + 1)
        yseg = 0
    return y.reshape(b, l, d), w_out.astype(BF16)


def _rope_tables(l):
    t = jnp.arange(l)
    row = (t // GRID_W).astype(F32)[:, None]
    col = (t % GRID_W).astype(F32)[:, None]
    width = 2 * HEAD_DIM
    cos, sin, avg, perm = [], [], [], []
    for dim in (A_HALF, HEAD_DIM):
        nf = dim // 4
        inv = jnp.asarray(ROPE_BASE ** (-np.arange(nf) / nf), F32)[None, :]
        ar, ac = row * inv, col * inv
        c = jnp.concatenate([jnp.cos(ar), jnp.cos(ar), jnp.cos(ac), jnp.cos(ac)], axis=1)
        s = jnp.concatenate([-jnp.sin(ar), jnp.sin(ar), -jnp.sin(ac), jnp.sin(ac)], axis=1)
        reps = width // dim
        cos.append(jnp.tile(c, (1, reps)))
        sin.append(jnp.tile(s, (1, reps)))
        lane = np.arange(width)
        avg.append((lane[:, None] // dim == lane[None, :] // dim) / dim)
        partner = np.where(lane % (2 * nf) < nf, lane + nf, lane - nf)
        perm.append((lane[:, None] == partner[None, :]).astype(np.float64))
    return (jnp.stack(cos), jnp.stack(sin), jnp.asarray(np.stack(avg), BF16), jnp.asarray(np.stack(perm), BF16))


def _attn_mixer(h, hc, lam_init, w_in, w_out, a_q_g, a_k_g, lq1, lk1, lq2, lk2, a_sub_g, b_q_g, b_k_g, b_sink):
    b, l, d = h.shape
    c = hc.shape[1]
    a_heads = d // (2 * HEAD_DIM)
    b_heads = d // (2 * HEAD_DIM)
    q_cols = (a_heads + b_heads) * HEAD_DIM
    in_cols = w_in.shape[1]
    lam = (jnp.exp(jnp.sum(lq1 * lk1)) - jnp.exp(jnp.sum(lq2 * lk2)) + lam_init).reshape(1, 1)
    sink = b_sink.reshape(B_KV_HEADS, b_heads // B_KV_HEADS)

    ones = lambda n: jnp.ones((n,), F32)
    tile = lambda v, n: jnp.tile(v, n)
    gain = jnp.concatenate([tile(a_q_g, 2 * a_heads), tile(b_q_g, b_heads), tile(a_k_g, 2 * a_heads),
                            ones(a_heads * HEAD_DIM), tile(b_k_g, B_KV_HEADS),
                            ones(B_KV_HEADS * HEAD_DIM)]).reshape(1, in_cols)
    scale = jnp.concatenate([jnp.full((a_heads * HEAD_DIM,), A_HALF ** -0.5 * math.log2(math.e), F32),
                             jnp.full((b_heads * HEAD_DIM,), HEAD_DIM ** -0.5, F32),
                             ones(in_cols - q_cols)]).reshape(1, in_cols)
    e = [0, a_heads // 2, (a_heads + b_heads) // 2, (2 * a_heads + b_heads) // 2,
         (3 * a_heads + b_heads) // 2, (3 * a_heads + b_heads) // 2 + 1, (3 * a_heads + b_heads) // 2 + 2]
    plain = ((e[3], e[4]), (e[5], e[6]))
    cos_tab, sin_tab, avg, perm = _rope_tables(l)

    def type_of(j0):
        def f(j):
            jj = j + j0
            is_a = (jj < e[1]) | ((jj >= e[2]) & (jj < e[3]))
            return jnp.where(is_a, 0, 1)
        return f

    w_bf = _col_tiles(w_in.astype(BF16), 2 * HEAD_DIM)
    p = _inproj(h.reshape(b * l, d), w_bf, gain, scale, cos_tab, sin_tab, avg, perm, plain=plain,
                type_of_block=type_of(0), col0=0, ncols=in_cols, l=l, rope=True).reshape(b, l, in_cols)
    kv0 = e[2]
    plain_c = tuple((lo - kv0, hi - kv0) for lo, hi in plain)
    pc = _inproj(hc.reshape(b * c, d), w_bf, gain, scale, cos_tab, sin_tab, avg, perm, plain=plain_c,
                 type_of_block=type_of(kv0), col0=kv0, ncols=in_cols - q_cols, l=c, rope=False
                 ).reshape(b, c, in_cols - q_cols)
    score_bound = (A_HALF ** 0.5 * math.log2(math.e)) * jnp.max(jnp.abs(a_q_g)) * jnp.max(jnp.abs(a_k_g))
    oa = _diffattn(lam, p, pc, a_sub_g, score_bound, heads=a_heads, l=l, c=c, out_scale=1.0 - lam_init)
    ob = _winattn(sink, p, pc, a_heads=a_heads, b_heads=b_heads, l=l, c=c)
    w_out_bf = w_out.astype(BF16)
    na = a_heads * HEAD_DIM
    return [oa, ob], [w_out_bf[:na], w_out_bf[na:]]


def kernel(x, c, ctx, c_ctx, ada_w, ada_b, norm1_g, norm2_g, attn_w_in, attn_w_out, a_q_g, a_k_g, a_lam_q1, a_lam_k1, a_lam_q2, a_lam_k2, a_sub_g, b_q_g, b_k_g, b_sink, hy_w_in, hy_b_in, hy_sconv_w, hy_sconv_b, hy_f_w0, hy_f_b0, hy_f_w1, hy_f_b1, hy_f_w2, hy_f_b2, hy_f_freq, hy_f_w3, hy_d, hy_w_out, ffn_w_gate, ffn_w_val, ffn_conv_w, ffn_conv_b, ffn_w_down):
    b, l, d = x.shape
    depth = ada_w.shape[0]
    cc = jnp.concatenate([c, c_ctx[None, :], jnp.zeros((8 - b - 1, d), F32)], axis=0)
    m = _ada(cc, ada_w, ada_b)
    xs = x
    for layer in range(depth):
        i = layer // 2
        lat = [m[layer, :b, k * d:(k + 1) * d].reshape(b, 1, d) for k in range(6)]
        sh1, sc1, g1, sh2, sc2, g2 = lat
        h = _normmod(xs, norm1_g[layer], sc1, sh1)
        if layer % 2 == 0:
            mc = [jnp.broadcast_to(m[layer, b, k * d:(k + 1) * d].reshape(1, 1, d), (b, 1, d)) for k in range(2)]
            hc = _normmod(ctx, norm1_g[layer], mc[1], mc[0])
            parts, ws = _attn_mixer(h, hc, 0.8 - 0.6 * math.exp(-0.3 * layer), attn_w_in[i], attn_w_out[i],
                                    a_q_g[i], a_k_g[i], a_lam_q1[i], a_lam_k1[i], a_lam_q2[i], a_lam_k2[i],
                                    a_sub_g[i], b_q_g[i], b_k_g[i], b_sink[i])
        else:
            hy = (hy_w_in[i], hy_b_in[i], hy_sconv_w[i], hy_sconv_b[i], hy_f_w0[i], hy_f_b0[i], hy_f_w1[i],
                  hy_f_b1[i], hy_f_w2[i], hy_f_b2[i], hy_f_freq[i], hy_f_w3[i], hy_d[i], hy_w_out[i])
            y, w_o = _hyena_mixer(h, hy, l, d)
            parts, ws = [y], [w_o]
        xs = _mm_resid(parts, ws, xs, g1)
        h2 = _normmod(xs, norm2_g[layer], sc2, sh2)
        d_ff = ffn_w_gate.shape[2]
        hid = _mm_conv(_ffn_up_kernel, h2, [ffn_w_gate[layer].astype(BF16), ffn_w_val[layer].astype(BF16)],
                       [ffn_conv_w[layer], ffn_conv_b[layer].reshape(1, -1)], d_ff, BF16, "ffn_up_conv_glu", nz=2)
        xs = _mm_resid([hid], [ffn_w_down[layer].astype(BF16)], xs, g2)
    return xs
```

```python
import functools
import math

import numpy as np
import jax
import jax.numpy as jnp
from jax import lax
from jax.experimental import pallas as pl
from jax.experimental.pallas import tpu as pltpu

F32 = jnp.float32
BF16 = jnp.bfloat16
HIGHEST = lax.Precision.HIGHEST

HEAD_DIM = 128
A_HALF = HEAD_DIM // 2
B_KV_HEADS = 2
GRID_W = 64
WINDOW = 128
ROPE_BASE = 10000.0
EPS = 1e-6
NEG = -1e30
HY_ORDER = 2
HY_DECAY_TARGET = 1e-2
HY_MAX_DECAY_PCT = 0.3
HY_MIN_DECAY_PCT = 1.5

LANES = 128
BF16_SUBLANES = 16
FFT_N1 = 256
VMEM_LIMIT_BYTES = 56 * 1024 * 1024


def _params(*sem):
    return pltpu.CompilerParams(dimension_semantics=sem, vmem_limit_bytes=VMEM_LIMIT_BYTES)


def _col_tiles(w, tn):
    k, n = w.shape
    return jnp.transpose(w.reshape(k, n // tn, tn), (1, 0, 2))


def _ada_kernel(c_ref, w_ref, b_ref, o_ref):
    c = c_ref[...]
    s = c * (1.0 / (1.0 + jnp.exp(-c)))
    o_ref[0] = jnp.dot(s, w_ref[0], preferred_element_type=F32, precision=HIGHEST) + b_ref[0]


def _ada(cc, ada_w, ada_b):
    depth, d, n = ada_w.shape
    tn = 1024
    return pl.pallas_call(
        _ada_kernel,
        grid=(depth, n // tn),
        in_specs=[pl.BlockSpec((8, d), lambda l, j: (0, 0)),
                  pl.BlockSpec((1, d, tn), lambda l, j: (l, 0, j)),
                  pl.BlockSpec((1, 1, tn), lambda l, j: (l, 0, j))],
        out_specs=pl.BlockSpec((1, 8, tn), lambda l, j: (l, 0, j)),
        out_shape=jax.ShapeDtypeStruct((depth, 8, n), F32),
        compiler_params=_params("parallel", "parallel"),
        name="ada_modulation",
    )(cc, ada_w, ada_b.reshape(depth, 1, n))


def _normmod_kernel(x_ref, g_ref, sc_ref, sh_ref, o_ref):
    x = x_ref[0]
    ms = jnp.mean(x * x, axis=-1, keepdims=True)
    y = x * lax.rsqrt(ms + EPS) * g_ref[...]
    o_ref[0] = (y * (1.0 + sc_ref[0]) + sh_ref[0]).astype(o_ref.dtype)


def _normmod(x, g, sc, sh):
    b, l, d = x.shape
    tm = min(512, l)
    return pl.pallas_call(
        _normmod_kernel,
        grid=(b, l // tm),
        in_specs=[pl.BlockSpec((1, tm, d), lambda bi, i: (bi, i, 0)),
                  pl.BlockSpec((1, d), lambda bi, i: (0, 0)),
                  pl.BlockSpec((1, 1, d), lambda bi, i: (bi, 0, 0)),
                  pl.BlockSpec((1, 1, d), lambda bi, i: (bi, 0, 0))],
        out_specs=pl.BlockSpec((1, tm, d), lambda bi, i: (bi, i, 0)),
        out_shape=jax.ShapeDtypeStruct((b, l, d), BF16),
        compiler_params=_params("parallel", "parallel"),
        name="rmsnorm_modulate",
    )(x, g.reshape(1, d), sc, sh)


def _dot_split(x, m):
    hi = x.astype(BF16)
    lo = (x - hi.astype(F32)).astype(BF16)
    return jnp.dot(hi, m, preferred_element_type=F32) + jnp.dot(lo, m, preferred_element_type=F32)


def _inproj_kernel(h_ref, w_ref, gain_ref, scale_ref, cos_ref, sin_ref, avg_ref, perm_ref, o_ref, *, plain, rope):
    j = pl.program_id(1)
    z = jnp.dot(h_ref[...], w_ref[0], preferred_element_type=F32)
    is_plain = functools.reduce(jnp.logical_or, [(j >= lo) & (j < hi) for lo, hi in plain])

    @pl.when(is_plain)
    def _():
        o_ref[...] = z.astype(o_ref.dtype)

    @pl.when(jnp.logical_not(is_plain))
    def _():
        ms = _dot_split(z * z, avg_ref[0])
        y = z * lax.rsqrt(ms + EPS) * gain_ref[...]
        if rope:
            y = y * cos_ref[0] + _dot_split(y, perm_ref[0]) * sin_ref[0]
        o_ref[...] = (y * scale_ref[...]).astype(o_ref.dtype)


def _inproj(h2d, w, gain, scale, cos_tab, sin_tab, avg, perm, *, plain, type_of_block, col0, ncols, l, rope):
    m, d = h2d.shape
    tn = 2 * HEAD_DIM
    tm = min(1024, l)
    pos_blocks = l // tm
    kern = functools.partial(_inproj_kernel, plain=plain, rope=rope)
    by_type = lambda i, j: (type_of_block(j), 0, 0)
    return pl.pallas_call(
        kern,
        grid=(m // tm, ncols // tn),
        in_specs=[pl.BlockSpec((tm, d), lambda i, j: (i, 0)),
                  pl.BlockSpec((1, d, tn), lambda i, j: (j + col0, 0, 0)),
                  pl.BlockSpec((1, tn), lambda i, j: (0, j + col0)),
                  pl.BlockSpec((1, tn), lambda i, j: (0, j + col0)),
                  pl.BlockSpec((1, tm, tn), lambda i, j: (type_of_block(j), i % pos_blocks, 0)),
                  pl.BlockSpec((1, tm, tn), lambda i, j: (type_of_block(j), i % pos_blocks, 0)),
                  pl.BlockSpec((1, tn, tn), by_type),
                  pl.BlockSpec((1, tn, tn), by_type)],
        out_specs=pl.BlockSpec((tm, tn), lambda i, j: (i, j)),
        out_shape=jax.ShapeDtypeStruct((m, ncols), BF16),
        compiler_params=_params("parallel", "arbitrary"),
        name="attn_in_projection",
    )(h2d, w, gain, scale, cos_tab, sin_tab, avg, perm)


def _diffattn_kernel(lam_ref, q_ref, kc_ref, vc_ref, k_ref, v_ref, g_ref, o_ref,
                     q2_s, m_s, l_s, acc_s, *, tq, nkv, out_scale):
    kv = pl.program_id(3)

    def process(k, v):
        s = lax.dot_general(q2_s[...], k, (((1,), (1,)), ((), ())), preferred_element_type=F32)
        m_prev = m_s[...]
        m_new = jnp.maximum(m_prev, jnp.max(s, axis=-1, keepdims=True))
        alpha = jnp.exp2(m_prev - m_new)
        p = jnp.exp2(s - m_new)
        l_s[...] = alpha * l_s[...] + jnp.sum(p, axis=-1, keepdims=True)
        acc_s[...] = alpha * acc_s[...] + jnp.dot(p.astype(BF16), v, preferred_element_type=F32)
        m_s[...] = m_new

    @pl.when(kv == 0)
    def _():
        q = q_ref[0]
        lane = lax.broadcasted_iota(jnp.int32, q.shape, 1)
        zero = jnp.zeros_like(q)
        q2_s[0:tq, :] = jnp.where(lane < A_HALF, q, zero)
        q2_s[tq:2 * tq, :] = jnp.where(lane >= A_HALF, q, zero)
        m_s[...] = jnp.full(m_s.shape, -jnp.inf, F32)
        l_s[...] = jnp.zeros(l_s.shape, F32)
        acc_s[...] = jnp.zeros(acc_s.shape, F32)
        process(kc_ref[0], vc_ref[0])

    process(k_ref[0], v_ref[0])

    @pl.when(kv == nkv - 1)
    def _():
        o = acc_s[...] / l_s[...]
        d = o[0:tq] - lam_ref[0, 0] * o[tq:2 * tq]
        ms = jnp.mean(d * d, axis=-1, keepdims=True)
        o_ref[0] = (d * lax.rsqrt(ms + EPS) * g_ref[...] * out_scale).astype(o_ref.dtype)


def _diffattn_fast_kernel(lam_ref, q_ref, kc_ref, vct_ref, k_ref, vt_ref, g_ref, o_ref,
                          q2_s, l_s, acc_s, p_s, *, tq, tk, kc, nkv, out_scale):
    kv = pl.program_id(3)
    dn = (((1,), (1,)), ((), ()))

    def weights(k):
        n = k.shape[0]
        p = jnp.exp2(lax.dot_general(k, q2_s[...], dn, preferred_element_type=F32))
        return p.astype(BF16), jnp.sum(p.reshape(n // 8, 8, 2 * tq), axis=0)

    @pl.when(kv == 0)
    def _():
        q = q_ref[0]
        lane = lax.broadcasted_iota(jnp.int32, q.shape, 1)
        zero = jnp.zeros_like(q)
        q2_s[0:tq, :] = jnp.where(lane < A_HALF, q, zero)
        q2_s[tq:2 * tq, :] = jnp.where(lane >= A_HALF, q, zero)
        p, ls = weights(kc_ref[0])
        l_s[...] = ls
        acc_s[...] = jnp.dot(vct_ref[0, 0], p, preferred_element_type=F32)

    ls = l_s[...]
    for ci in range(tk // kc):
        p, lc = weights(k_ref[0, ci * kc:(ci + 1) * kc, :])
        p_s[ci * kc:(ci + 1) * kc, :] = p
        ls = ls + lc
    l_s[...] = ls
    acc_s[...] += jnp.dot(vt_ref[0, 0], p_s[...], preferred_element_type=F32)

    @pl.when(kv == nkv - 1)
    def _():
        l = jnp.sum(l_s[...], axis=0, keepdims=True)
        o = acc_s[...] / l
        d = o[:, 0:tq] - lam_ref[0, 0] * o[:, tq:2 * tq]
        ms = jnp.mean(d * d, axis=0, keepdims=True)
        y = d * lax.rsqrt(ms + EPS) * g_ref[...] * out_scale
        o_ref[0] = y.T.astype(o_ref.dtype)


def _diffattn_fast(lam, p, pc, sub_g, *, heads, l, c, out_scale):
    b = p.shape[0]
    tq = min(512, l)
    tk = min(4096, l)
    kc = min(512, tk)
    nkv = l // tk
    k_blk = 2 * heads
    hd = heads * HEAD_DIM
    vt = jnp.transpose(p[:, :, 3 * hd:4 * hd].reshape(b, l, heads, HEAD_DIM), (0, 2, 3, 1))
    vct = jnp.transpose(pc[:, :, hd:2 * hd].reshape(b, c, heads, HEAD_DIM), (0, 2, 3, 1))
    kern = functools.partial(_diffattn_fast_kernel, tq=tq, tk=tk, kc=kc, nkv=nkv, out_scale=out_scale)
    return pl.pallas_call(
        kern,
        grid=(b, heads, l // tq, nkv),
        in_specs=[pl.BlockSpec(memory_space=pltpu.SMEM),
                  pl.BlockSpec((1, tq, HEAD_DIM), lambda bi, h, i, kv: (bi, i, h)),
                  pl.BlockSpec((1, c, HEAD_DIM), lambda bi, h, i, kv: (bi, 0, h)),
                  pl.BlockSpec((1, 1, HEAD_DIM, c), lambda bi, h, i, kv: (bi, h, 0, 0)),
                  pl.BlockSpec((1, tk, HEAD_DIM), lambda bi, h, i, kv: (bi, kv, k_blk + h)),
                  pl.BlockSpec((1, 1, HEAD_DIM, tk), lambda bi, h, i, kv: (bi, h, 0, kv)),
                  pl.BlockSpec((HEAD_DIM, 1), lambda bi, h, i, kv: (0, 0))],
        out_specs=pl.BlockSpec((1, tq, HEAD_DIM), lambda bi, h, i, kv: (bi, i, h)),
        out_shape=jax.ShapeDtypeStruct((b, l, hd), BF16),
        scratch_shapes=[pltpu.VMEM((2 * tq, HEAD_DIM), BF16),
                        pltpu.VMEM((8, 2 * tq), F32),
                        pltpu.VMEM((HEAD_DIM, 2 * tq), F32),
                        pltpu.VMEM((tk, 2 * tq), BF16)],
        compiler_params=_params("parallel", "parallel", "parallel", "arbitrary"),
        name="diff_attention_bounded",
    )(lam, p, pc, vct, p, vt, sub_g.reshape(HEAD_DIM, 1))


SCORE_BOUND_LOG2 = 60.0


def _diffattn(lam, p, pc, sub_g, score_bound, *, heads, l, c, out_scale):
    kw = dict(heads=heads, l=l, c=c, out_scale=out_scale)
    return lax.cond(score_bound < SCORE_BOUND_LOG2,
                    lambda *a: _diffattn_fast(*a, **kw), lambda *a: _diffattn_safe(*a, **kw),
                    lam, p, pc, sub_g)


def _diffattn_safe(lam, p, pc, sub_g, *, heads, l, c, out_scale):
    b = p.shape[0]
    tq = min(256, l)
    tk = min(512, l)
    nkv = l // tk
    q_blk = 0
    k_blk = 2 * heads
    v_blk = 3 * heads
    kern = functools.partial(_diffattn_kernel, tq=tq, nkv=nkv, out_scale=out_scale)
    return pl.pallas_call(
        kern,
        grid=(b, heads, l // tq, nkv),
        in_specs=[pl.BlockSpec(memory_space=pltpu.SMEM),
                  pl.BlockSpec((1, tq, HEAD_DIM), lambda bi, h, i, kv: (bi, i, q_blk + h)),
                  pl.BlockSpec((1, c, HEAD_DIM), lambda bi, h, i, kv: (bi, 0, h)),
                  pl.BlockSpec((1, c, HEAD_DIM), lambda bi, h, i, kv: (bi, 0, heads + h)),
                  pl.BlockSpec((1, tk, HEAD_DIM), lambda bi, h, i, kv: (bi, kv, k_blk + h)),
                  pl.BlockSpec((1, tk, HEAD_DIM), lambda bi, h, i, kv: (bi, kv, v_blk + h)),
                  pl.BlockSpec((1, HEAD_DIM), lambda bi, h, i, kv: (0, 0))],
        out_specs=pl.BlockSpec((1, tq, HEAD_DIM), lambda bi, h, i, kv: (bi, i, h)),
        out_shape=jax.ShapeDtypeStruct((b, l, heads * HEAD_DIM), BF16),
        scratch_shapes=[pltpu.VMEM((2 * tq, HEAD_DIM), BF16),
                        pltpu.VMEM((2 * tq, 1), F32),
                        pltpu.VMEM((2 * tq, 1), F32),
                        pltpu.VMEM((2 * tq, HEAD_DIM), F32)],
        compiler_params=_params("parallel", "parallel", "parallel", "arbitrary"),
        name="diff_attention",
    )(lam, p, pc, pc, p, p, sub_g.reshape(1, HEAD_DIM))


def _winattn_kernel(sink_ref, q_ref, kp_ref, km_ref, kn_ref, vp_ref, vm_ref, vn_ref, kc_ref, vc_ref,
                    o_ref, *, tq, l, group):
    g = pl.program_id(1)
    i = pl.program_id(2)
    kband = jnp.concatenate([kp_ref[0], km_ref[0], kn_ref[0]], axis=0)
    vband = jnp.concatenate([vp_ref[0], vm_ref[0], vn_ref[0]], axis=0)
    nk = tq + 2 * WINDOW
    qpos = i * tq + lax.broadcasted_iota(jnp.int32, (tq, nk), 0)
    kpos = i * tq - WINDOW + lax.broadcasted_iota(jnp.int32, (tq, nk), 1)
    valid = (jnp.abs(kpos - qpos) <= WINDOW) & (kpos >= 0) & (kpos < l)
    kc = kc_ref[0]
    vc = vc_ref[0]
    dn = (((1,), (1,)), ((), ()))
    for r in range(group):
        sl = slice(r * HEAD_DIM, (r + 1) * HEAD_DIM)
        q = q_ref[0, :, sl]
        s_lat = jnp.where(valid, lax.dot_general(q, kband, dn, preferred_element_type=F32), NEG)
        s_ctx = lax.dot_general(q, kc, dn, preferred_element_type=F32)
        sk = sink_ref[g, r]
        m = jnp.maximum(jnp.maximum(jnp.max(s_lat, axis=-1, keepdims=True),
                                    jnp.max(s_ctx, axis=-1, keepdims=True)), sk)
        p_lat = jnp.exp(s_lat - m)
        p_ctx = jnp.exp(s_ctx - m)
        denom = (jnp.sum(p_lat, axis=-1, keepdims=True) + jnp.sum(p_ctx, axis=-1, keepdims=True)
                 + jnp.exp(sk - m))
        o = (jnp.dot(p_lat.astype(BF16), vband, preferred_element_type=F32)
             + jnp.dot(p_ctx.astype(BF16), vc, preferred_element_type=F32))
        o_ref[0, :, sl] = (o / denom).astype(o_ref.dtype)


def _winattn(sink, p, pc, *, a_heads, b_heads, l, c):
    b = p.shape[0]
    group = b_heads // B_KV_HEADS
    tq = min(256, l)
    wpb = tq // WINDOW
    nwb = l // WINDOW
    gw = group * HEAD_DIM
    q_blk = a_heads * HEAD_DIM // gw
    k_blk = (a_heads + b_heads) + 2 * a_heads
    v_blk = k_blk + B_KV_HEADS
    kc_blk = 2 * a_heads
    vc_blk = kc_blk + B_KV_HEADS
    kern = functools.partial(_winattn_kernel, tq=tq, l=l, group=group)

    def prev_map(col):
        return lambda bi, g, i: (bi, jnp.maximum(i * wpb - 1, 0), col + g)

    def main_map(col):
        return lambda bi, g, i: (bi, i, col + g)

    def next_map(col):
        return lambda bi, g, i: (bi, jnp.minimum((i + 1) * wpb, nwb - 1), col + g)

    return pl.pallas_call(
        kern,
        grid=(b, B_KV_HEADS, l // tq),
        in_specs=[pl.BlockSpec(memory_space=pltpu.SMEM),
                  pl.BlockSpec((1, tq, gw), lambda bi, g, i: (bi, i, q_blk + g)),
                  pl.BlockSpec((1, WINDOW, HEAD_DIM), prev_map(k_blk)),
                  pl.BlockSpec((1, tq, HEAD_DIM), main_map(k_blk)),
                  pl.BlockSpec((1, WINDOW, HEAD_DIM), next_map(k_blk)),
                  pl.BlockSpec((1, WINDOW, HEAD_DIM), prev_map(v_blk)),
                  pl.BlockSpec((1, tq, HEAD_DIM), main_map(v_blk)),
                  pl.BlockSpec((1, WINDOW, HEAD_DIM), next_map(v_blk)),
                  pl.BlockSpec((1, c, HEAD_DIM), lambda bi, g, i: (bi, 0, kc_blk + g)),
                  pl.BlockSpec((1, c, HEAD_DIM), lambda bi, g, i: (bi, 0, vc_blk + g))],
        out_specs=pl.BlockSpec((1, tq, gw), lambda bi, g, i: (bi, i, g)),
        out_shape=jax.ShapeDtypeStruct((b, l, b_heads * HEAD_DIM), BF16),
        compiler_params=_params("parallel", "parallel", "parallel"),
        name="window_attention",
    )(sink, p, p, p, p, p, p, p, pc, pc)


def _mm_resid_kernel(*refs, nparts):
    a_refs = refs[0:nparts]
    w_refs = refs[nparts:2 * nparts]
    res_ref, gate_ref, o_ref = refs[2 * nparts:]
    y = jnp.dot(a_refs[0][0].astype(BF16), w_refs[0][0], preferred_element_type=F32)
    for a_ref, w_ref in zip(a_refs[1:], w_refs[1:]):
        y = y + jnp.dot(a_ref[0].astype(BF16), w_ref[0], preferred_element_type=F32)
    o_ref[0] = res_ref[0] + gate_ref[0] * y


def _mm_resid(parts, ws, res, gate):
    b, l, n = res.shape
    tm = min(1024, l)
    tn = min(512, n)
    nparts = len(parts)
    ws = [_col_tiles(w, tn) for w in ws]
    in_specs = ([pl.BlockSpec((1, tm, a.shape[2]), lambda bi, i, j: (bi, i, 0)) for a in parts]
                + [pl.BlockSpec((1, w.shape[1], tn), lambda bi, i, j: (j, 0, 0)) for w in ws]
                + [pl.BlockSpec((1, tm, tn), lambda bi, i, j: (bi, i, j)),
                   pl.BlockSpec((1, 1, tn), lambda bi, i, j: (bi, 0, j))])
    return pl.pallas_call(
        functools.partial(_mm_resid_kernel, nparts=nparts),
        grid=(b, l // tm, n // tn),
        in_specs=in_specs,
        out_specs=pl.BlockSpec((1, tm, tn), lambda bi, i, j: (bi, i, j)),
        out_shape=jax.ShapeDtypeStruct((b, l, n), F32),
        compiler_params=_params("parallel", "parallel", "arbitrary"),
        name="out_projection_gated_residual",
    )(*parts, *ws, res, gate)


HALO = BF16_SUBLANES


def _fill_rows(hbuf, hp_ref, hm_ref, hn_ref, tm):
    hbuf[0:HALO, :] = hp_ref[0]
    hbuf[HALO:HALO + tm, :] = hm_ref[0]
    hbuf[HALO + tm:2 * HALO + tm, :] = hn_ref[0]


def _conv3(zbuf, cw_ref, cb_ref, tm, first, last):
    lo = slice(HALO - 1, HALO)
    hi = slice(HALO + tm, HALO + tm + 1)
    zbuf[lo, :] = jnp.where(first, 0.0, zbuf[lo, :])
    zbuf[hi, :] = jnp.where(last, 0.0, zbuf[hi, :])
    z = zbuf[...]
    rows = z.shape[0]
    g = (z * cw_ref[1:2, :] + pltpu.roll(z * cw_ref[0:1, :], 1, 0) + pltpu.roll(z * cw_ref[2:3, :], rows - 1, 0))
    return g[HALO:HALO + tm] + cb_ref[...]


def _gelu_tanh(x):
    return 0.5 * x * (1.0 + jnp.tanh(math.sqrt(2.0 / math.pi) * (x + 0.044715 * (x * x * x))))


def _ffn_up_kernel(hp_ref, hm_ref, hn_ref, wg_ref, wv_ref, cw_ref, cb_ref, o_ref, hbuf, zbuf, zbuf2, *, tm, nt):
    i = pl.program_id(1)
    j = pl.program_id(2)

    @pl.when(j == 0)
    def _():
        _fill_rows(hbuf, hp_ref, hm_ref, hn_ref, tm)

    hw = zbuf.shape[1]
    for hh, zb in enumerate((zbuf, zbuf2)):
        cs = slice(hh * hw, (hh + 1) * hw)
        zb[...] = jnp.dot(hbuf[...], wg_ref[0, :, cs], preferred_element_type=F32)
        g = _conv3(zb, cw_ref.at[:, cs], cb_ref.at[:, cs], tm, i == 0, i == nt - 1)
        v = jnp.dot(hbuf[HALO:HALO + tm, :], wv_ref[0, :, cs], preferred_element_type=F32)
        o_ref[0, :, cs] = (_gelu_tanh(g) * v).astype(o_ref.dtype)


def _hy_in_kernel(hp_ref, hm_ref, hn_ref, w_ref, b_ref, cw_ref, cb_ref, o_ref, hbuf, zbuf, zbuf2, *, tm, nt):
    i = pl.program_id(1)
    j = pl.program_id(2)

    @pl.when(j == 0)
    def _():
        _fill_rows(hbuf, hp_ref, hm_ref, hn_ref, tm)

    hw = zbuf.shape[1]
    for hh, zb in enumerate((zbuf, zbuf2)):
        cs = slice(hh * hw, (hh + 1) * hw)
        zb[...] = jnp.dot(hbuf[...], w_ref[0, :, cs], preferred_element_type=F32) + b_ref[:, cs]
        o_ref[0, :, cs] = _conv3(zb, cw_ref.at[:, cs], cb_ref.at[:, cs], tm, i == 0, i == nt - 1
                                 ).astype(o_ref.dtype)


def _mm_conv(kernel, h, ws, vecs, n, out_dtype, name, nz=1):
    b, l, k = h.shape
    tm = min(512, l)
    tn = min(512, n)
    nt = l // tm
    hpb = tm // HALO
    nhb = l // HALO
    ws = [_col_tiles(w, tn) for w in ws]
    in_specs = ([pl.BlockSpec((1, HALO, k), lambda bi, i, j: (bi, jnp.maximum(i * hpb - 1, 0), 0)),
                 pl.BlockSpec((1, tm, k), lambda bi, i, j: (bi, i, 0)),
                 pl.BlockSpec((1, HALO, k), lambda bi, i, j: (bi, jnp.minimum((i + 1) * hpb, nhb - 1), 0))]
                + [pl.BlockSpec((1, k, tn), lambda bi, i, j: (j, 0, 0)) for _ in ws]
                + [pl.BlockSpec((v.shape[0], tn), lambda bi, i, j: (0, j)) for v in vecs])
    return pl.pallas_call(
        functools.partial(kernel, tm=tm, nt=nt),
        grid=(b, nt, n // tn),
        in_specs=in_specs,
        out_specs=pl.BlockSpec((1, tm, tn), lambda bi, i, j: (bi, i, j)),
        out_shape=jax.ShapeDtypeStruct((b, l, n), out_dtype),
        scratch_shapes=[pltpu.VMEM((tm + 2 * HALO, k), BF16)] + [pltpu.VMEM((tm + 2 * HALO, tn // nz), F32)] * nz,
        compiler_params=_params("parallel", "parallel", "arbitrary"),
        name=name,
    )(h, h, h, *ws, *vecs)


FILT_NB = BF16_SUBLANES


def _filter_dft_kernel(z_ref, w0_ref, b0_ref, w1_ref, b1_ref, w2_ref, b2_ref, fr_ref, w3_ref, dl_ref, t_ref,
                       o_ref, s_ref, a3_s, r_s):
    m, seg, j = pl.program_id(0), pl.program_id(1), pl.program_id(2)
    tn = o_ref.shape[3]
    fr = fr_ref[...]

    def layer(a, w_ref, b_ref):
        return jnp.sin(fr * (jnp.dot(a, w_ref[...], preferred_element_type=F32, precision=HIGHEST)
                             + b_ref[...]))

    @pl.when((seg == 0) & (j == 0))
    def _():
        for jj in range(FILT_NB):
            a = layer(layer(layer(z_ref[jj], w0_ref, b0_ref), w1_ref, b1_ref), w2_ref, b2_ref)
            a_hi = a.astype(BF16)
            a_lo = (a - a_hi.astype(F32)).astype(BF16)
            a3_s[jj] = jnp.concatenate([a_hi, a_hi, a_lo], axis=1)

    @pl.when((m == 0) & (seg == 0) & (j == 0))
    def _():
        s_ref[...] = jnp.zeros(s_ref.shape, F32)

    dl = dl_ref[...]
    backward = seg % 2 == 1
    abs_sum = jnp.zeros((1, tn), F32)
    for jj in range(FILT_NB):
        h = (jnp.dot(a3_s[jj], w3_ref[0], preferred_element_type=F32)
             * jnp.exp(-z_ref[jj][:, 0:1] * dl))
        if jj == 0:
            row = lax.broadcasted_iota(jnp.int32, h.shape, 0)
            h = jnp.where(backward & (m == 0) & (row == 0), 0.0, h)
        abs_sum = abs_sum + jnp.sum(jnp.abs(h), axis=0, keepdims=True)
        r_s[jj] = jnp.dot(t_ref[jj], h.astype(BF16), preferred_element_type=F32)
    o_ref[0] = jnp.swapaxes(r_s[...], 0, 1).astype(o_ref.dtype)
    col = pl.multiple_of(j * tn, tn)
    s_ref[seg, :, pl.ds(col, tn)] += abs_sum


def _hyena_filter_spectra_a(l, d, ta, f_w0, f_b0, f_w1, f_b1, f_w2, f_b2, f_freq, f_w3):
    emb, fw = f_w0.shape
    bands = (emb - 1) // 2
    half = FFT_N1 // 2
    n2 = l // half
    t = np.linspace(0.0, 1.0, l)[:, None]
    w = 2.0 * math.pi * np.arange(l)[:, None] / l
    f = np.linspace(1e-4, bands - 1, bands)[None, :]
    emb_pad = -(-emb // 8) * 8
    z = np.concatenate([t, np.cos(f * w), -np.sin(f * w), np.zeros((l, emb_pad - emb))], axis=-1)
    z = jnp.asarray(z.reshape(half, n2, emb_pad).transpose(1, 0, 2), F32)
    w0 = jnp.concatenate([f_w0, jnp.zeros((emb_pad - emb, fw), F32)], axis=0)
    max_decay = math.log(HY_DECAY_TARGET) / HY_MAX_DECAY_PCT
    min_decay = math.log(HY_DECAY_TARGET) / HY_MIN_DECAY_PCT
    dl = jnp.asarray(np.abs(np.linspace(min_decay, max_decay, d))[None, :], F32)
    nseg = 2 * HY_ORDER
    w3 = jnp.transpose(f_w3.reshape(fw, nseg, d), (1, 0, 2))
    w3_hi = w3.astype(BF16)
    w3_lo = (w3 - w3_hi.astype(F32)).astype(BF16)
    w3 = jnp.concatenate([w3_hi, w3_lo, w3_hi], axis=1)
    nb = min(FILT_NB, n2)
    assert nb == FILT_NB
    tn = min(256, d)
    vec = lambda a: a.reshape(1, fw)
    small = lambda shape: pl.BlockSpec(shape, lambda m, sg, j: (0,) * len(shape))
    return pl.pallas_call(
        _filter_dft_kernel,
        grid=(n2 // nb, nseg, d // tn),
        in_specs=[pl.BlockSpec((nb, half, emb_pad), lambda m, sg, j: (m, 0, 0)),
                  small((emb_pad, fw)), small((1, fw)), small((fw, fw)), small((1, fw)),
                  small((fw, fw)), small((1, fw)), small((1, fw)),
                  pl.BlockSpec((1, 3 * fw, tn), lambda m, sg, j: (sg, 0, j)),
                  pl.BlockSpec((1, tn), lambda m, sg, j: (0, j)),
                  pl.BlockSpec((nb, 2 * FFT_N1, half), lambda m, sg, j: (m, 0, 0))],
        out_specs=[pl.BlockSpec((1, 2 * FFT_N1, nb, tn), lambda m, sg, j: (sg, 0, m, j)),
                   pl.BlockSpec((nseg, 1, d), lambda m, sg, j: (0, 0, 0))],
        out_shape=[jax.ShapeDtypeStruct((nseg, 2 * FFT_N1, n2, d), BF16),
                   jax.ShapeDtypeStruct((nseg, 1, d), F32)],
        scratch_shapes=[pltpu.VMEM((nb, half, 3 * fw), BF16),
                        pltpu.VMEM((nb, 2 * FFT_N1, tn), F32)],
        compiler_params=_params("arbitrary", "arbitrary", "arbitrary"),
        name="hyena_filter_dft_stage_a",
    )(z, w0, vec(f_b0), f_w1, vec(f_b1), f_w2, vec(f_b2), vec(f_freq), w3, dl, ta)


def _dft_tables(l):
    n1 = FFT_N1
    n = 2 * l
    n2 = n // n1
    k1 = np.arange(n1)[:, None]
    f1 = np.exp(-2j * np.pi * k1 * np.arange(n1 // 2)[None, :] / n1)
    tw = np.exp(-2j * np.pi * np.arange(n2)[:, None] * np.arange(n1)[None, :] / n)
    f1r, f1i = jnp.asarray(f1.real, F32), jnp.asarray(f1.imag, F32)
    twr, twi = jnp.asarray(tw.real, F32)[:, :, None], jnp.asarray(tw.imag, F32)[:, :, None]
    fr = twr * f1r - twi * f1i
    fi = twr * f1i + twi * f1r
    ta = jnp.concatenate([jnp.concatenate([fr, -fi], axis=2),
                          jnp.concatenate([fi, fr], axis=2)], axis=1)
    fb = np.exp(-2j * np.pi * np.arange(n2)[:, None] * np.arange(n2)[None, :] / n2)
    tb = np.block([[fb.real, -fb.imag], [fb.imag, fb.real]])
    return (ta.astype(BF16), jnp.swapaxes(ta, 1, 2).astype(BF16),
            jnp.asarray(tb, BF16), jnp.asarray(tb.T, BF16))


FFT_NB = BF16_SUBLANES


def _fft_a_kernel(x_ref, t_ref, o_ref, r_s):
    planes = x_ref.shape[1]
    xt = [jnp.swapaxes(x_ref[0, c], 0, 1) for c in range(planes)]
    for jj in range(FFT_NB):
        x = jnp.concatenate([xt[c][jj] for c in range(planes)], axis=0).astype(BF16)
        r_s[jj] = jnp.dot(t_ref[jj], x, preferred_element_type=F32)
    o_ref[0] = jnp.swapaxes(r_s[...], 0, 1).astype(o_ref.dtype)


def _fft_a(x5, ta, *, n2, d, seg):
    s, p, half, _, _ = x5.shape
    kdim = p * half
    tn = min(256, d)
    sb, db = (seg * d) // tn, d // tn
    return pl.pallas_call(
        _fft_a_kernel,
        grid=(s, n2 // FFT_NB, db),
        in_specs=[pl.BlockSpec((1, p, half, FFT_NB, tn), lambda si, m, j: (si, 0, 0, m, sb + j)),
                  pl.BlockSpec((FFT_NB, 2 * FFT_N1, kdim), lambda si, m, j: (m, 0, 0))],
        out_specs=pl.BlockSpec((1, 2 * FFT_N1, FFT_NB, tn), lambda si, m, j: (si, 0, m, j)),
        out_shape=jax.ShapeDtypeStruct((s, 2 * FFT_N1, n2, d), BF16),
        scratch_shapes=[pltpu.VMEM((FFT_NB, 2 * FFT_N1, tn), F32)],
        compiler_params=_params("parallel", "parallel", "parallel"),
        name="dft_stage_a",
    )(x5, ta)


def _fft_b_kernel(a_ref, af_ref, ab_ref, sc_ref, tb_ref, tbt_ref, o_ref, *, n2):
    tb = tb_ref[...]
    z = jnp.dot(tb, a_ref[...].reshape(2 * n2, -1), preferred_element_type=F32)
    hf = jnp.dot(tb, af_ref[0].reshape(2 * n2, -1), preferred_element_type=F32)
    hb = jnp.dot(tb, ab_ref[0].reshape(2 * n2, -1), preferred_element_type=F32)
    sc = sc_ref[0]
    gr = (hf[0:n2] + hb[0:n2]) * sc
    gi = (hf[n2:2 * n2] - hb[n2:2 * n2]) * sc
    zr, zi = z[0:n2], z[n2:2 * n2]
    prod = jnp.concatenate([zr * gr - zi * gi, zr * gi + zi * gr], axis=0).astype(BF16)
    y = jnp.dot(tbt_ref[...], prod, preferred_element_type=F32)
    o_ref[...] = y.reshape(o_ref.shape).astype(o_ref.dtype)


def _fft_b(a, af, scale, tb, tbt, order, *, n2, d):
    tn = min(2048, d)
    blk = (2, 1, n2, tn)
    fblk = (1, 2, 1, n2, tn)
    return pl.pallas_call(
        functools.partial(_fft_b_kernel, n2=n2),
        grid=(FFT_N1, d // tn),
        in_specs=[pl.BlockSpec(blk, lambda k, j: (0, k, 0, j)),
                  pl.BlockSpec(fblk, lambda k, j: (2 * order, 0, k, 0, j)),
                  pl.BlockSpec(fblk, lambda k, j: (2 * order + 1, 0, k, 0, j)),
                  pl.BlockSpec((1, 1, tn), lambda k, j: (order, 0, j)),
                  pl.BlockSpec((2 * n2, 2 * n2), lambda k, j: (0, 0)),
                  pl.BlockSpec((2 * n2, 2 * n2), lambda k, j: (0, 0))],
        out_specs=pl.BlockSpec(blk, lambda k, j: (0, k, 0, j)),
        out_shape=jax.ShapeDtypeStruct((2, FFT_N1, n2, d), BF16),
        compiler_params=_params("parallel", "parallel"),
        name="dft_stage_b_spectrum_product",
    )(a, af, af, scale, tb, tbt)


def _fft_ainv_kernel(y_ref, t_ref, u_ref, gate_ref, d_ref, o_ref, r_s):
    half = o_ref.shape[1]
    yt = jnp.swapaxes(y_ref[...].astype(F32), 0, 1).astype(BF16)
    for c in range(2):
        r_s[c] = jnp.swapaxes(u_ref[c], 0, 1) * d_ref[...]
    for jj in range(FFT_NB):
        x = jnp.dot(t_ref[jj], yt[jj], preferred_element_type=F32)
        for c in range(2):
            r_s[c, jj] += x[c * half:(c + 1) * half]
    for c in range(2):
        o_ref[c] = gate_ref[c] * jnp.swapaxes(r_s[c], 0, 1)


def _fft_ainv(y3, tat, u4, gate4, dskip, *, n2, d, useg, gseg):
    b, half, _, _ = u4.shape
    tn = min(128, d)
    db = d // tn
    usb, gsb = (useg * d) // tn, (gseg * d) // tn
    return pl.pallas_call(
        _fft_ainv_kernel,
        grid=(n2 // FFT_NB, db),
        in_specs=[pl.BlockSpec((2 * FFT_N1, FFT_NB, tn), lambda m, j: (0, m, j)),
                  pl.BlockSpec((FFT_NB, FFT_N1, 2 * FFT_N1), lambda m, j: (m, 0, 0)),
                  pl.BlockSpec((b, half, FFT_NB, tn), lambda m, j: (0, 0, m, usb + j)),
                  pl.BlockSpec((b, half, FFT_NB, tn), lambda m, j: (0, 0, m, gsb + j)),
                  pl.BlockSpec((1, tn), lambda m, j: (0, j))],
        out_specs=pl.BlockSpec((b, half, FFT_NB, tn), lambda m, j: (0, 0, m, j)),
        out_shape=jax.ShapeDtypeStruct((b, half, n2, d), F32),
        scratch_shapes=[pltpu.VMEM((b, FFT_NB, half, tn), F32)],
        compiler_params=_params("parallel", "parallel"),
        name="dft_stage_a_inverse_gate",
    )(y3, tat, u4, gate4, dskip)


def _hyena_mixer(h, hy, l, d):
    (w_in, b_in, sconv_w, sconv_b, f_w0, f_b0, f_w1, f_b1, f_w2, f_b2, f_freq, f_w3, d_skip, w_out) = hy
    b = h.shape[0]
    assert b == 2, "the two batch rows ride the real / imaginary planes of one complex DFT"
    n1 = FFT_N1
    half = n1 // 2
    n2 = (2 * l) // n1
    assert half * n2 == l
    z3 = _mm_conv(_hy_in_kernel, h, [w_in.astype(BF16)],
                  [b_in.reshape(1, -1), sconv_w, sconv_b.reshape(1, -1)], 3 * d, F32, "hyena_in_proj_conv", nz=2)
    ta, tat, tb, tbt = _dft_tables(l)
    nseg = 2 * HY_ORDER
    af, fsum = _hyena_filter_spectra_a(l, d, ta, f_w0, f_b0, f_w1, f_b1, f_w2, f_b2, f_freq, f_w3)
    af = af.reshape(nseg, 2, n1, n2, d)
    scale = 1.0 / ((fsum[0::2] + fsum[1::2]) * (2 * l))
    z4 = z3.reshape(b, half, n2, 3 * d)
    y, yseg = z4, 0
    for o in range(HY_ORDER):
        a = _fft_a(y[None], ta, n2=n2, d=d, seg=yseg)
        yb = _fft_b(a.reshape(2, n1, n2, d), af, scale, tb, tbt, o, n2=n2, d=d)
        y = _fft_ainv(yb.reshape(2 * n1, n2, d), tat, y, z4, d_skip[o].reshape(1, d),
                      n2=n2, d=d, useg=yseg, gseg=o + 1)
        yseg = 0
    return y.reshape(b, l, d), w_out.astype(BF16)


def _rope_tables(l):
    t = jnp.arange(l)
    row = (t // GRID_W).astype(F32)[:, None]
    col = (t % GRID_W).astype(F32)[:, None]
    width = 2 * HEAD_DIM
    cos, sin, avg, perm = [], [], [], []
    for dim in (A_HALF, HEAD_DIM):
        nf = dim // 4
        inv = jnp.asarray(ROPE_BASE ** (-np.arange(nf) / nf), F32)[None, :]
        ar, ac = row * inv, col * inv
        c = jnp.concatenate([jnp.cos(ar), jnp.cos(ar), jnp.cos(ac), jnp.cos(ac)], axis=1)
        s = jnp.concatenate([-jnp.sin(ar), jnp.sin(ar), -jnp.sin(ac), jnp.sin(ac)], axis=1)
        reps = width // dim
        cos.append(jnp.tile(c, (1, reps)))
        sin.append(jnp.tile(s, (1, reps)))
        lane = np.arange(width)
        avg.append((lane[:, None] // dim == lane[None, :] // dim) / dim)
        partner = np.where(lane % (2 * nf) < nf, lane + nf, lane - nf)
        perm.append((lane[:, None] == partner[None, :]).astype(np.float64))
    return (jnp.stack(cos), jnp.stack(sin), jnp.asarray(np.stack(avg), BF16), jnp.asarray(np.stack(perm), BF16))


def _attn_mixer(h, hc, lam_init, w_in, w_out, a_q_g, a_k_g, lq1, lk1, lq2, lk2, a_sub_g, b_q_g, b_k_g, b_sink):
    b, l, d = h.shape
    c = hc.shape[1]
    a_heads = d // (2 * HEAD_DIM)
    b_heads = d // (2 * HEAD_DIM)
    q_cols = (a_heads + b_heads) * HEAD_DIM
    in_cols = w_in.shape[1]
    lam = (jnp.exp(jnp.sum(lq1 * lk1)) - jnp.exp(jnp.sum(lq2 * lk2)) + lam_init).reshape(1, 1)
    sink = b_sink.reshape(B_KV_HEADS, b_heads // B_KV_HEADS)

    ones = lambda n: jnp.ones((n,), F32)
    tile = lambda v, n: jnp.tile(v, n)
    gain = jnp.concatenate([tile(a_q_g, 2 * a_heads), tile(b_q_g, b_heads), tile(a_k_g, 2 * a_heads),
                            ones(a_heads * HEAD_DIM), tile(b_k_g, B_KV_HEADS),
                            ones(B_KV_HEADS * HEAD_DIM)]).reshape(1, in_cols)
    scale = jnp.concatenate([jnp.full((a_heads * HEAD_DIM,), A_HALF ** -0.5 * math.log2(math.e), F32),
                             jnp.full((b_heads * HEAD_DIM,), HEAD_DIM ** -0.5, F32),
                             ones(in_cols - q_cols)]).reshape(1, in_cols)
    e = [0, a_heads // 2, (a_heads + b_heads) // 2, (2 * a_heads + b_heads) // 2,
         (3 * a_heads + b_heads) // 2, (3 * a_heads + b_heads) // 2 + 1, (3 * a_heads + b_heads) // 2 + 2]
    plain = ((e[3], e[4]), (e[5], e[6]))
    cos_tab, sin_tab, avg, perm = _rope_tables(l)

    def type_of(j0):
        def f(j):
            jj = j + j0
            is_a = (jj < e[1]) | ((jj >= e[2]) & (jj < e[3]))
            return jnp.where(is_a, 0, 1)
        return f

    w_bf = _col_tiles(w_in.astype(BF16), 2 * HEAD_DIM)
    p = _inproj(h.reshape(b * l, d), w_bf, gain, scale, cos_tab, sin_tab, avg, perm, plain=plain,
                type_of_block=type_of(0), col0=0, ncols=in_cols, l=l, rope=True).reshape(b, l, in_cols)
    kv0 = e[2]
    plain_c = tuple((lo - kv0, hi - kv0) for lo, hi in plain)
    pc = _inproj(hc.reshape(b * c, d), w_bf, gain, scale, cos_tab, sin_tab, avg, perm, plain=plain_c,
                 type_of_block=type_of(kv0), col0=kv0, ncols=in_cols - q_cols, l=c, rope=False
                 ).reshape(b, c, in_cols - q_cols)
    score_bound = (A_HALF ** 0.5 * math.log2(math.e)) * jnp.max(jnp.abs(a_q_g)) * jnp.max(jnp.abs(a_k_g))
    oa = _diffattn(lam, p, pc, a_sub_g, score_bound, heads=a_heads, l=l, c=c, out_scale=1.0 - lam_init)
    ob = _winattn(sink, p, pc, a_heads=a_heads, b_heads=b_heads, l=l, c=c)
    w_out_bf = w_out.astype(BF16)
    na = a_heads * HEAD_DIM
    return [oa, ob], [w_out_bf[:na], w_out_bf[na:]]


def kernel(x, c, ctx, c_ctx, ada_w, ada_b, norm1_g, norm2_g, attn_w_in, attn_w_out, a_q_g, a_k_g, a_lam_q1, a_lam_k1, a_lam_q2, a_lam_k2, a_sub_g, b_q_g, b_k_g, b_sink, hy_w_in, hy_b_in, hy_sconv_w, hy_sconv_b, hy_f_w0, hy_f_b0, hy_f_w1, hy_f_b1, hy_f_w2, hy_f_b2, hy_f_freq, hy_f_w3, hy_d, hy_w_out, ffn_w_gate, ffn_w_val, ffn_conv_w, ffn_conv_b, ffn_w_down):
    b, l, d = x.shape
    depth = ada_w.shape[0]
    cc = jnp.concatenate([c, c_ctx[None, :], jnp.zeros((8 - b - 1, d), F32)], axis=0)
    m = _ada(cc, ada_w, ada_b)
    xs = x
    for layer in range(depth):
        i = layer // 2
        lat = [m[layer, :b, k * d:(k + 1) * d].reshape(b, 1, d) for k in range(6)]
        sh1, sc1, g1, sh2, sc2, g2 = lat
        h = _normmod(xs, norm1_g[layer], sc1, sh1)
        if layer % 2 == 0:
            mc = [jnp.broadcast_to(m[layer, b, k * d:(k + 1) * d].reshape(1, 1, d), (b, 1, d)) for k in range(2)]
            hc = _normmod(ctx, norm1_g[layer], mc[1], mc[0])
            parts, ws = _attn_mixer(h, hc, 0.8 - 0.6 * math.exp(-0.3 * layer), attn_w_in[i], attn_w_out[i],
                                    a_q_g[i], a_k_g[i], a_lam_q1[i], a_lam_k1[i], a_lam_q2[i], a_lam_k2[i],
                                    a_sub_g[i], b_q_g[i], b_k_g[i], b_sink[i])
        else:
            hy = (hy_w_in[i], hy_b_in[i], hy_sconv_w[i], hy_sconv_b[i], hy_f_w0[i], hy_f_b0[i], hy_f_w1[i],
                  hy_f_b1[i], hy_f_w2[i], hy_f_b2[i], hy_f_freq[i], hy_f_w3[i], hy_d[i], hy_w_out[i])
            y, w_o = _hyena_mixer(h, hy, l, d)
            parts, ws = [y], [w_o]
        xs = _mm_resid(parts, ws, xs, g1)
        h2 = _normmod(xs, norm2_g[layer], sc2, sh2)
        d_ff = ffn_w_gate.shape[2]
        hid = _mm_conv(_ffn_up_kernel, h2, [ffn_w_gate[layer].astype(BF16), ffn_w_val[layer].astype(BF16)],
                       [ffn_conv_w[layer], ffn_conv_b[layer].reshape(1, -1)], d_ff, BF16, "ffn_up_conv_glu", nz=2)
        xs = _mm_resid([hid], [ffn_w_down[layer].astype(BF16)], xs, g2)
    return xs
```

```python
import functools
import math

import numpy as np
import jax
import jax.numpy as jnp
from jax import lax
from jax.experimental import pallas as pl
from jax.experimental.pallas import tpu as pltpu

F32 = jnp.float32
BF16 = jnp.bfloat16
HIGHEST = lax.Precision.HIGHEST

HEAD_DIM = 128
A_HALF = HEAD_DIM // 2
B_KV_HEADS = 2
GRID_W = 64
WINDOW = 128
ROPE_BASE = 10000.0
EPS = 1e-6
NEG = -1e30
HY_ORDER = 2
HY_DECAY_TARGET = 1e-2
HY_MAX_DECAY_PCT = 0.3
HY_MIN_DECAY_PCT = 1.5

LANES = 128
BF16_SUBLANES = 16
FFT_N1 = 256
VMEM_LIMIT_BYTES = 56 * 1024 * 1024


def _params(*sem):
    return pltpu.CompilerParams(dimension_semantics=sem, vmem_limit_bytes=VMEM_LIMIT_BYTES)


def _col_tiles(w, tn):
    k, n = w.shape
    return jnp.transpose(w.reshape(k, n // tn, tn), (1, 0, 2))


def _ada_kernel(c_ref, w_ref, b_ref, o_ref):
    c = c_ref[...]
    s = c * (1.0 / (1.0 + jnp.exp(-c)))
    o_ref[0] = jnp.dot(s, w_ref[0], preferred_element_type=F32, precision=HIGHEST) + b_ref[0]


def _ada(cc, ada_w, ada_b):
    depth, d, n = ada_w.shape
    tn = 1024
    return pl.pallas_call(
        _ada_kernel,
        grid=(depth, n // tn),
        in_specs=[pl.BlockSpec((8, d), lambda l, j: (0, 0)),
                  pl.BlockSpec((1, d, tn), lambda l, j: (l, 0, j)),
                  pl.BlockSpec((1, 1, tn), lambda l, j: (l, 0, j))],
        out_specs=pl.BlockSpec((1, 8, tn), lambda l, j: (l, 0, j)),
        out_shape=jax.ShapeDtypeStruct((depth, 8, n), F32),
        compiler_params=_params("parallel", "parallel"),
        name="ada_modulation",
    )(cc, ada_w, ada_b.reshape(depth, 1, n))


def _normmod_kernel(x_ref, g_ref, sc_ref, sh_ref, o_ref):
    x = x_ref[0]
    ms = jnp.mean(x * x, axis=-1, keepdims=True)
    y = x * lax.rsqrt(ms + EPS) * g_ref[...]
    o_ref[0] = (y * (1.0 + sc_ref[0]) + sh_ref[0]).astype(o_ref.dtype)


def _normmod(x, g, sc, sh):
    b, l, d = x.shape
    tm = min(512, l)
    return pl.pallas_call(
        _normmod_kernel,
        grid=(b, l // tm),
        in_specs=[pl.BlockSpec((1, tm, d), lambda bi, i: (bi, i, 0)),
                  pl.BlockSpec((1, d), lambda bi, i: (0, 0)),
                  pl.BlockSpec((1, 1, d), lambda bi, i: (bi, 0, 0)),
                  pl.BlockSpec((1, 1, d), lambda bi, i: (bi, 0, 0))],
        out_specs=pl.BlockSpec((1, tm, d), lambda bi, i: (bi, i, 0)),
        out_shape=jax.ShapeDtypeStruct((b, l, d), BF16),
        compiler_params=_params("parallel", "parallel"),
        name="rmsnorm_modulate",
    )(x, g.reshape(1, d), sc, sh)


def _dot_split(x, m):
    hi = x.astype(BF16)
    lo = (x - hi.astype(F32)).astype(BF16)
    return jnp.dot(hi, m, preferred_element_type=F32) + jnp.dot(lo, m, preferred_element_type=F32)


def _inproj_kernel(h_ref, w_ref, gain_ref, scale_ref, cos_ref, sin_ref, avg_ref, perm_ref, o_ref, *, plain, rope):
    j = pl.program_id(1)
    z = jnp.dot(h_ref[...], w_ref[0], preferred_element_type=F32)
    is_plain = functools.reduce(jnp.logical_or, [(j >= lo) & (j < hi) for lo, hi in plain])

    @pl.when(is_plain)
    def _():
        o_ref[...] = z.astype(o_ref.dtype)

    @pl.when(jnp.logical_not(is_plain))
    def _():
        rows = min(256, z.shape[0])
        for r0 in range(0, z.shape[0], rows):
            rs = slice(r0, r0 + rows)
            zc = z[rs]
            ms = _dot_split(zc * zc, avg_ref[0])
            y = zc * lax.rsqrt(ms + EPS) * gain_ref[...]
            if rope:
                y = y * cos_ref[0, rs, :] + _dot_split(y, perm_ref[0]) * sin_ref[0, rs, :]
            o_ref[rs, :] = (y * scale_ref[...]).astype(o_ref.dtype)


def _inproj(h2d, w, gain, scale, cos_tab, sin_tab, avg, perm, *, plain, type_of_block, col0, ncols, l, rope):
    m, d = h2d.shape
    tn = 2 * HEAD_DIM
    tm = min(1024, l)
    pos_blocks = l // tm
    kern = functools.partial(_inproj_kernel, plain=plain, rope=rope)
    by_type = lambda i, j: (type_of_block(j), 0, 0)
    return pl.pallas_call(
        kern,
        grid=(m // tm, ncols // tn),
        in_specs=[pl.BlockSpec((tm, d), lambda i, j: (i, 0)),
                  pl.BlockSpec((1, d, tn), lambda i, j: (j + col0, 0, 0)),
                  pl.BlockSpec((1, tn), lambda i, j: (0, j + col0)),
                  pl.BlockSpec((1, tn), lambda i, j: (0, j + col0)),
                  pl.BlockSpec((1, tm, tn), lambda i, j: (type_of_block(j), i % pos_blocks, 0)),
                  pl.BlockSpec((1, tm, tn), lambda i, j: (type_of_block(j), i % pos_blocks, 0)),
                  pl.BlockSpec((1, tn, tn), by_type),
                  pl.BlockSpec((1, tn, tn), by_type)],
        out_specs=pl.BlockSpec((tm, tn), lambda i, j: (i, j)),
        out_shape=jax.ShapeDtypeStruct((m, ncols), BF16),
        compiler_params=_params("parallel", "arbitrary"),
        name="attn_in_projection",
    )(h2d, w, gain, scale, cos_tab, sin_tab, avg, perm)


def _diffattn_kernel(lam_ref, q_ref, kc_ref, vc_ref, k_ref, v_ref, g_ref, o_ref,
                     q2_s, m_s, l_s, acc_s, *, tq, nkv, out_scale):
    kv = pl.program_id(3)

    def process(k, v):
        s = lax.dot_general(q2_s[...], k, (((1,), (1,)), ((), ())), preferred_element_type=F32)
        m_prev = m_s[...]
        m_new = jnp.maximum(m_prev, jnp.max(s, axis=-1, keepdims=True))
        alpha = jnp.exp2(m_prev - m_new)
        p = jnp.exp2(s - m_new)
        l_s[...] = alpha * l_s[...] + jnp.sum(p, axis=-1, keepdims=True)
        acc_s[...] = alpha * acc_s[...] + jnp.dot(p.astype(BF16), v, preferred_element_type=F32)
        m_s[...] = m_new

    @pl.when(kv == 0)
    def _():
        q = q_ref[0]
        lane = lax.broadcasted_iota(jnp.int32, q.shape, 1)
        zero = jnp.zeros_like(q)
        q2_s[0:tq, :] = jnp.where(lane < A_HALF, q, zero)
        q2_s[tq:2 * tq, :] = jnp.where(lane >= A_HALF, q, zero)
        m_s[...] = jnp.full(m_s.shape, -jnp.inf, F32)
        l_s[...] = jnp.zeros(l_s.shape, F32)
        acc_s[...] = jnp.zeros(acc_s.shape, F32)
        process(kc_ref[0], vc_ref[0])

    process(k_ref[0], v_ref[0])

    @pl.when(kv == nkv - 1)
    def _():
        o = acc_s[...] / l_s[...]
        d = o[0:tq] - lam_ref[0, 0] * o[tq:2 * tq]
        ms = jnp.mean(d * d, axis=-1, keepdims=True)
        o_ref[0] = (d * lax.rsqrt(ms + EPS) * g_ref[...] * out_scale).astype(o_ref.dtype)


def _diffattn_fast_kernel(lam_ref, q_ref, kc_ref, vct_ref, k_ref, vt_ref, g_ref, o_ref,
                          q2_s, l_s, acc_s, p_s, *, tq, tk, kc, nkv, out_scale):
    kv = pl.program_id(3)
    dn = (((1,), (1,)), ((), ()))

    def weights(k):
        n = k.shape[0]
        p = jnp.exp2(lax.dot_general(k, q2_s[...], dn, preferred_element_type=F32))
        return p.astype(BF16), jnp.sum(p.reshape(n // 8, 8, 2 * tq), axis=0)

    @pl.when(kv == 0)
    def _():
        q = q_ref[0]
        lane = lax.broadcasted_iota(jnp.int32, q.shape, 1)
        zero = jnp.zeros_like(q)
        q2_s[0:tq, :] = jnp.where(lane < A_HALF, q, zero)
        q2_s[tq:2 * tq, :] = jnp.where(lane >= A_HALF, q, zero)
        p, ls = weights(kc_ref[0])
        l_s[...] = ls
        acc_s[...] = jnp.dot(vct_ref[0, 0], p, preferred_element_type=F32)

    ls = l_s[...]
    for ci in range(tk // kc):
        p, lc = weights(k_ref[0, ci * kc:(ci + 1) * kc, :])
        p_s[ci * kc:(ci + 1) * kc, :] = p
        ls = ls + lc
    l_s[...] = ls
    acc_s[...] += jnp.dot(vt_ref[0, 0], p_s[...], preferred_element_type=F32)

    @pl.when(kv == nkv - 1)
    def _():
        l = jnp.sum(l_s[...], axis=0, keepdims=True)
        o = acc_s[...] / l
        d = o[:, 0:tq] - lam_ref[0, 0] * o[:, tq:2 * tq]
        ms = jnp.mean(d * d, axis=0, keepdims=True)
        y = d * lax.rsqrt(ms + EPS) * g_ref[...] * out_scale
        o_ref[0] = y.T.astype(o_ref.dtype)


def _diffattn_fast(lam, p, pc, sub_g, *, heads, l, c, out_scale):
    b = p.shape[0]
    tq = min(1024, l)
    tk = min(4096, l)
    kc = min(512, tk)
    nkv = l // tk
    k_blk = 2 * heads
    hd = heads * HEAD_DIM
    vt = jnp.transpose(p[:, :, 3 * hd:4 * hd].reshape(b, l, heads, HEAD_DIM), (0, 2, 3, 1))
    vct = jnp.transpose(pc[:, :, hd:2 * hd].reshape(b, c, heads, HEAD_DIM), (0, 2, 3, 1))
    kern = functools.partial(_diffattn_fast_kernel, tq=tq, tk=tk, kc=kc, nkv=nkv, out_scale=out_scale)
    return pl.pallas_call(
        kern,
        grid=(b, heads, l // tq, nkv),
        in_specs=[pl.BlockSpec(memory_space=pltpu.SMEM),
                  pl.BlockSpec((1, tq, HEAD_DIM), lambda bi, h, i, kv: (bi, i, h)),
                  pl.BlockSpec((1, c, HEAD_DIM), lambda bi, h, i, kv: (bi, 0, h)),
                  pl.BlockSpec((1, 1, HEAD_DIM, c), lambda bi, h, i, kv: (bi, h, 0, 0)),
                  pl.BlockSpec((1, tk, HEAD_DIM), lambda bi, h, i, kv: (bi, kv, k_blk + h)),
                  pl.BlockSpec((1, 1, HEAD_DIM, tk), lambda bi, h, i, kv: (bi, h, 0, kv)),
                  pl.BlockSpec((HEAD_DIM, 1), lambda bi, h, i, kv: (0, 0))],
        out_specs=pl.BlockSpec((1, tq, HEAD_DIM), lambda bi, h, i, kv: (bi, i, h)),
        out_shape=jax.ShapeDtypeStruct((b, l, hd), BF16),
        scratch_shapes=[pltpu.VMEM((2 * tq, HEAD_DIM), BF16),
                        pltpu.VMEM((8, 2 * tq), F32),
                        pltpu.VMEM((HEAD_DIM, 2 * tq), F32),
                        pltpu.VMEM((tk, 2 * tq), BF16)],
        compiler_params=_params("parallel", "parallel", "parallel", "arbitrary"),
        name="diff_attention_bounded",
    )(lam, p, pc, vct, p, vt, sub_g.reshape(HEAD_DIM, 1))


SCORE_BOUND_LOG2 = 60.0


def _diffattn(lam, p, pc, sub_g, score_bound, *, heads, l, c, out_scale):
    kw = dict(heads=heads, l=l, c=c, out_scale=out_scale)
    return lax.cond(score_bound < SCORE_BOUND_LOG2,
                    lambda *a: _diffattn_fast(*a, **kw), lambda *a: _diffattn_safe(*a, **kw),
                    lam, p, pc, sub_g)


def _diffattn_safe(lam, p, pc, sub_g, *, heads, l, c, out_scale):
    b = p.shape[0]
    tq = min(256, l)
    tk = min(512, l)
    nkv = l // tk
    q_blk = 0
    k_blk = 2 * heads
    v_blk = 3 * heads
    kern = functools.partial(_diffattn_kernel, tq=tq, nkv=nkv, out_scale=out_scale)
    return pl.pallas_call(
        kern,
        grid=(b, heads, l // tq, nkv),
        in_specs=[pl.BlockSpec(memory_space=pltpu.SMEM),
                  pl.BlockSpec((1, tq, HEAD_DIM), lambda bi, h, i, kv: (bi, i, q_blk + h)),
                  pl.BlockSpec((1, c, HEAD_DIM), lambda bi, h, i, kv: (bi, 0, h)),
                  pl.BlockSpec((1, c, HEAD_DIM), lambda bi, h, i, kv: (bi, 0, heads + h)),
                  pl.BlockSpec((1, tk, HEAD_DIM), lambda bi, h, i, kv: (bi, kv, k_blk + h)),
                  pl.BlockSpec((1, tk, HEAD_DIM), lambda bi, h, i, kv: (bi, kv, v_blk + h)),
                  pl.BlockSpec((1, HEAD_DIM), lambda bi, h, i, kv: (0, 0))],
        out_specs=pl.BlockSpec((1, tq, HEAD_DIM), lambda bi, h, i, kv: (bi, i, h)),
        out_shape=jax.ShapeDtypeStruct((b, l, heads * HEAD_DIM), BF16),
        scratch_shapes=[pltpu.VMEM((2 * tq, HEAD_DIM), BF16),
                        pltpu.VMEM((2 * tq, 1), F32),
                        pltpu.VMEM((2 * tq, 1), F32),
                        pltpu.VMEM((2 * tq, HEAD_DIM), F32)],
        compiler_params=_params("parallel", "parallel", "parallel", "arbitrary"),
        name="diff_attention",
    )(lam, p, pc, pc, p, p, sub_g.reshape(1, HEAD_DIM))


def _winattn_kernel(sink_ref, q_ref, kp_ref, km_ref, kn_ref, vp_ref, vm_ref, vn_ref, kc_ref, vc_ref,
                    o_ref, *, tq, l, group):
    g = pl.program_id(1)
    i = pl.program_id(2)
    kband = jnp.concatenate([kp_ref[0], km_ref[0], kn_ref[0]], axis=0)
    vband = jnp.concatenate([vp_ref[0], vm_ref[0], vn_ref[0]], axis=0)
    nk = tq + 2 * WINDOW
    qpos = i * tq + lax.broadcasted_iota(jnp.int32, (tq, nk), 0)
    kpos = i * tq - WINDOW + lax.broadcasted_iota(jnp.int32, (tq, nk), 1)
    valid = (jnp.abs(kpos - qpos) <= WINDOW) & (kpos >= 0) & (kpos < l)
    kc = kc_ref[0]
    vc = vc_ref[0]
    dn = (((1,), (1,)), ((), ()))
    for r in range(group):
        sl = slice(r * HEAD_DIM, (r + 1) * HEAD_DIM)
        q = q_ref[0, :, sl]
        s_lat = jnp.where(valid, lax.dot_general(q, kband, dn, preferred_element_type=F32), NEG)
        s_ctx = lax.dot_general(q, kc, dn, preferred_element_type=F32)
        sk = sink_ref[g, r]
        m = jnp.maximum(jnp.maximum(jnp.max(s_lat, axis=-1, keepdims=True),
                                    jnp.max(s_ctx, axis=-1, keepdims=True)), sk)
        p_lat = jnp.exp(s_lat - m)
        p_ctx = jnp.exp(s_ctx - m)
        denom = (jnp.sum(p_lat, axis=-1, keepdims=True) + jnp.sum(p_ctx, axis=-1, keepdims=True)
                 + jnp.exp(sk - m))
        o = (jnp.dot(p_lat.astype(BF16), vband, preferred_element_type=F32)
             + jnp.dot(p_ctx.astype(BF16), vc, preferred_element_type=F32))
        o_ref[0, :, sl] = (o / denom).astype(o_ref.dtype)


def _winattn(sink, p, pc, *, a_heads, b_heads, l, c):
    b = p.shape[0]
    group = b_heads // B_KV_HEADS
    tq = min(256, l)
    wpb = tq // WINDOW
    nwb = l // WINDOW
    gw = group * HEAD_DIM
    q_blk = a_heads * HEAD_DIM // gw
    k_blk = (a_heads + b_heads) + 2 * a_heads
    v_blk = k_blk + B_KV_HEADS
    kc_blk = 2 * a_heads
    vc_blk = kc_blk + B_KV_HEADS
    kern = functools.partial(_winattn_kernel, tq=tq, l=l, group=group)

    def prev_map(col):
        return lambda bi, g, i: (bi, jnp.maximum(i * wpb - 1, 0), col + g)

    def main_map(col):
        return lambda bi, g, i: (bi, i, col + g)

    def next_map(col):
        return lambda bi, g, i: (bi, jnp.minimum((i + 1) * wpb, nwb - 1), col + g)

    return pl.pallas_call(
        kern,
        grid=(b, B_KV_HEADS, l // tq),
        in_specs=[pl.BlockSpec(memory_space=pltpu.SMEM),
                  pl.BlockSpec((1, tq, gw), lambda bi, g, i: (bi, i, q_blk + g)),
                  pl.BlockSpec((1, WINDOW, HEAD_DIM), prev_map(k_blk)),
                  pl.BlockSpec((1, tq, HEAD_DIM), main_map(k_blk)),
                  pl.BlockSpec((1, WINDOW, HEAD_DIM), next_map(k_blk)),
                  pl.BlockSpec((1, WINDOW, HEAD_DIM), prev_map(v_blk)),
                  pl.BlockSpec((1, tq, HEAD_DIM), main_map(v_blk)),
                  pl.BlockSpec((1, WINDOW, HEAD_DIM), next_map(v_blk)),
                  pl.BlockSpec((1, c, HEAD_DIM), lambda bi, g, i: (bi, 0, kc_blk + g)),
                  pl.BlockSpec((1, c, HEAD_DIM), lambda bi, g, i: (bi, 0, vc_blk + g))],
        out_specs=pl.BlockSpec((1, tq, gw), lambda bi, g, i: (bi, i, g)),
        out_shape=jax.ShapeDtypeStruct((b, l, b_heads * HEAD_DIM), BF16),
        compiler_params=_params("parallel", "parallel", "parallel"),
        name="window_attention",
    )(sink, p, p, p, p, p, p, p, pc, pc)


def _mm_resid_kernel(*refs, nparts):
    a_refs = refs[0:nparts]
    w_refs = refs[nparts:2 * nparts]
    res_ref, gate_ref, o_ref = refs[2 * nparts:]
    y = jnp.dot(a_refs[0][0].astype(BF16), w_refs[0][0], preferred_element_type=F32)
    for a_ref, w_ref in zip(a_refs[1:], w_refs[1:]):
        y = y + jnp.dot(a_ref[0].astype(BF16), w_ref[0], preferred_element_type=F32)
    o_ref[0] = res_ref[0] + gate_ref[0] * y


def _mm_resid(parts, ws, res, gate):
    b, l, n = res.shape
    tm = min(1024, l)
    tn = min(512, n)
    nparts = len(parts)
    ws = [_col_tiles(w, tn) for w in ws]
    in_specs = ([pl.BlockSpec((1, tm, a.shape[2]), lambda bi, i, j: (bi, i, 0)) for a in parts]
                + [pl.BlockSpec((1, w.shape[1], tn), lambda bi, i, j: (j, 0, 0)) for w in ws]
                + [pl.BlockSpec((1, tm, tn), lambda bi, i, j: (bi, i, j)),
                   pl.BlockSpec((1, 1, tn), lambda bi, i, j: (bi, 0, j))])
    return pl.pallas_call(
        functools.partial(_mm_resid_kernel, nparts=nparts),
        grid=(b, l // tm, n // tn),
        in_specs=in_specs,
        out_specs=pl.BlockSpec((1, tm, tn), lambda bi, i, j: (bi, i, j)),
        out_shape=jax.ShapeDtypeStruct((b, l, n), F32),
        compiler_params=_params("parallel", "parallel", "arbitrary"),
        name="out_projection_gated_residual",
    )(*parts, *ws, res, gate)


HALO = BF16_SUBLANES


def _fill_rows(hbuf, hp_ref, hm_ref, hn_ref, tm):
    hbuf[0:HALO, :] = hp_ref[0]
    hbuf[HALO:HALO + tm, :] = hm_ref[0]
    hbuf[HALO + tm:2 * HALO + tm, :] = hn_ref[0]


def _conv3(zbuf, cw_ref, cb_ref, tm, first, last):
    lo = slice(HALO - 1, HALO)
    hi = slice(HALO + tm, HALO + tm + 1)
    zbuf[lo, :] = jnp.where(first, 0.0, zbuf[lo, :])
    zbuf[hi, :] = jnp.where(last, 0.0, zbuf[hi, :])
    z = zbuf[...]
    rows = z.shape[0]
    g = (z * cw_ref[1:2, :] + pltpu.roll(z * cw_ref[0:1, :], 1, 0) + pltpu.roll(z * cw_ref[2:3, :], rows - 1, 0))
    return g[HALO:HALO + tm] + cb_ref[...]


def _gelu_tanh(x):
    return 0.5 * x * (1.0 + jnp.tanh(math.sqrt(2.0 / math.pi) * (x + 0.044715 * (x * x * x))))


def _ffn_up_kernel(hp_ref, hm_ref, hn_ref, wg_ref, wv_ref, cw_ref, cb_ref, o_ref, hbuf, zbuf, zbuf2, *, tm, nt):
    i = pl.program_id(1)
    j = pl.program_id(2)

    @pl.when(j == 0)
    def _():
        _fill_rows(hbuf, hp_ref, hm_ref, hn_ref, tm)

    hw = zbuf.shape[1]
    for hh, zb in enumerate((zbuf, zbuf2)):
        cs = slice(hh * hw, (hh + 1) * hw)
        zb[...] = jnp.dot(hbuf[...], wg_ref[0, :, cs], preferred_element_type=F32)
        g = _conv3(zb, cw_ref.at[:, cs], cb_ref.at[:, cs], tm, i == 0, i == nt - 1)
        v = jnp.dot(hbuf[HALO:HALO + tm, :], wv_ref[0, :, cs], preferred_element_type=F32)
        o_ref[0, :, cs] = (_gelu_tanh(g) * v).astype(o_ref.dtype)


def _hy_in_kernel(hp_ref, hm_ref, hn_ref, w_ref, b_ref, cw_ref, cb_ref, o_ref, hbuf, zbuf, zbuf2, *, tm, nt):
    i = pl.program_id(1)
    j = pl.program_id(2)

    @pl.when(j == 0)
    def _():
        _fill_rows(hbuf, hp_ref, hm_ref, hn_ref, tm)

    hw = zbuf.shape[1]
    for hh, zb in enumerate((zbuf, zbuf2)):
        cs = slice(hh * hw, (hh + 1) * hw)
        zb[...] = jnp.dot(hbuf[...], w_ref[0, :, cs], preferred_element_type=F32) + b_ref[:, cs]
        o_ref[0, :, cs] = _conv3(zb, cw_ref.at[:, cs], cb_ref.at[:, cs], tm, i == 0, i == nt - 1
                                 ).astype(o_ref.dtype)


def _mm_conv(kernel, h, ws, vecs, n, out_dtype, name, nz=1):
    b, l, k = h.shape
    tm = min(512, l)
    tn = min(512, n)
    nt = l // tm
    hpb = tm // HALO
    nhb = l // HALO
    ws = [_col_tiles(w, tn) for w in ws]
    in_specs = ([pl.BlockSpec((1, HALO, k), lambda bi, i, j: (bi, jnp.maximum(i * hpb - 1, 0), 0)),
                 pl.BlockSpec((1, tm, k), lambda bi, i, j: (bi, i, 0)),
                 pl.BlockSpec((1, HALO, k), lambda bi, i, j: (bi, jnp.minimum((i + 1) * hpb, nhb - 1), 0))]
                + [pl.BlockSpec((1, k, tn), lambda bi, i, j: (j, 0, 0)) for _ in ws]
                + [pl.BlockSpec((v.shape[0], tn), lambda bi, i, j: (0, j)) for v in vecs])
    return pl.pallas_call(
        functools.partial(kernel, tm=tm, nt=nt),
        grid=(b, nt, n // tn),
        in_specs=in_specs,
        out_specs=pl.BlockSpec((1, tm, tn), lambda bi, i, j: (bi, i, j)),
        out_shape=jax.ShapeDtypeStruct((b, l, n), out_dtype),
        scratch_shapes=[pltpu.VMEM((tm + 2 * HALO, k), BF16)] + [pltpu.VMEM((tm + 2 * HALO, tn // nz), F32)] * nz,
        compiler_params=_params("parallel", "parallel", "arbitrary"),
        name=name,
    )(h, h, h, *ws, *vecs)


FILT_NB = BF16_SUBLANES


def _filter_dft_kernel(z_ref, w0_ref, b0_ref, w1_ref, b1_ref, w2_ref, b2_ref, fr_ref, w3_ref, dl_ref, t_ref,
                       o_ref, s_ref, a3_s, r_s):
    m, seg, j = pl.program_id(0), pl.program_id(1), pl.program_id(2)
    tn = o_ref.shape[3]
    fr = fr_ref[...]

    def layer(a, w_ref, b_ref):
        return jnp.sin(fr * (jnp.dot(a, w_ref[...], preferred_element_type=F32, precision=HIGHEST)
                             + b_ref[...]))

    @pl.when((seg == 0) & (j == 0))
    def _():
        for jj in range(FILT_NB):
            a = layer(layer(layer(z_ref[jj], w0_ref, b0_ref), w1_ref, b1_ref), w2_ref, b2_ref)
            a_hi = a.astype(BF16)
            a_lo = (a - a_hi.astype(F32)).astype(BF16)
            a3_s[jj] = jnp.concatenate([a_hi, a_hi, a_lo], axis=1)

    @pl.when((m == 0) & (seg == 0) & (j == 0))
    def _():
        s_ref[...] = jnp.zeros(s_ref.shape, F32)

    dl = dl_ref[...]
    backward = seg % 2 == 1
    abs_sum = jnp.zeros((1, tn), F32)
    for jj in range(FILT_NB):
        h = (jnp.dot(a3_s[jj], w3_ref[0], preferred_element_type=F32)
             * jnp.exp(-z_ref[jj][:, 0:1] * dl))
        if jj == 0:
            row = lax.broadcasted_iota(jnp.int32, h.shape, 0)
            h = jnp.where(backward & (m == 0) & (row == 0), 0.0, h)
        abs_sum = abs_sum + jnp.sum(jnp.abs(h), axis=0, keepdims=True)
        r_s[jj] = jnp.dot(t_ref[jj], h.astype(BF16), preferred_element_type=F32)
    o_ref[0] = jnp.swapaxes(r_s[...], 0, 1).astype(o_ref.dtype)
    col = pl.multiple_of(j * tn, tn)
    s_ref[seg, :, pl.ds(col, tn)] += abs_sum


def _hyena_filter_spectra_a(l, d, ta, f_w0, f_b0, f_w1, f_b1, f_w2, f_b2, f_freq, f_w3):
    emb, fw = f_w0.shape
    bands = (emb - 1) // 2
    half = FFT_N1 // 2
    n2 = l // half
    t = np.linspace(0.0, 1.0, l)[:, None]
    w = 2.0 * math.pi * np.arange(l)[:, None] / l
    f = np.linspace(1e-4, bands - 1, bands)[None, :]
    emb_pad = -(-emb // 8) * 8
    z = np.concatenate([t, np.cos(f * w), -np.sin(f * w), np.zeros((l, emb_pad - emb))], axis=-1)
    z = jnp.asarray(z.reshape(half, n2, emb_pad).transpose(1, 0, 2), F32)
    w0 = jnp.concatenate([f_w0, jnp.zeros((emb_pad - emb, fw), F32)], axis=0)
    max_decay = math.log(HY_DECAY_TARGET) / HY_MAX_DECAY_PCT
    min_decay = math.log(HY_DECAY_TARGET) / HY_MIN_DECAY_PCT
    dl = jnp.asarray(np.abs(np.linspace(min_decay, max_decay, d))[None, :], F32)
    nseg = 2 * HY_ORDER
    w3 = jnp.transpose(f_w3.reshape(fw, nseg, d), (1, 0, 2))
    w3_hi = w3.astype(BF16)
    w3_lo = (w3 - w3_hi.astype(F32)).astype(BF16)
    w3 = jnp.concatenate([w3_hi, w3_lo, w3_hi], axis=1)
    nb = min(FILT_NB, n2)
    assert nb == FILT_NB
    tn = min(256, d)
    vec = lambda a: a.reshape(1, fw)
    small = lambda shape: pl.BlockSpec(shape, lambda m, sg, j: (0,) * len(shape))
    return pl.pallas_call(
        _filter_dft_kernel,
        grid=(n2 // nb, nseg, d // tn),
        in_specs=[pl.BlockSpec((nb, half, emb_pad), lambda m, sg, j: (m, 0, 0)),
                  small((emb_pad, fw)), small((1, fw)), small((fw, fw)), small((1, fw)),
                  small((fw, fw)), small((1, fw)), small((1, fw)),
                  pl.BlockSpec((1, 3 * fw, tn), lambda m, sg, j: (sg, 0, j)),
                  pl.BlockSpec((1, tn), lambda m, sg, j: (0, j)),
                  pl.BlockSpec((nb, 2 * FFT_N1, half), lambda m, sg, j: (m, 0, 0))],
        out_specs=[pl.BlockSpec((1, 2 * FFT_N1, nb, tn), lambda m, sg, j: (sg, 0, m, j)),
                   pl.BlockSpec((nseg, 1, d), lambda m, sg, j: (0, 0, 0))],
        out_shape=[jax.ShapeDtypeStruct((nseg, 2 * FFT_N1, n2, d), BF16),
                   jax.ShapeDtypeStruct((nseg, 1, d), F32)],
        scratch_shapes=[pltpu.VMEM((nb, half, 3 * fw), BF16),
                        pltpu.VMEM((nb, 2 * FFT_N1, tn), F32)],
        compiler_params=_params("arbitrary", "arbitrary", "arbitrary"),
        name="hyena_filter_dft_stage_a",
    )(z, w0, vec(f_b0), f_w1, vec(f_b1), f_w2, vec(f_b2), vec(f_freq), w3, dl, ta)


def _dft_tables(l):
    n1 = FFT_N1
    n = 2 * l
    n2 = n // n1
    k1 = np.arange(n1)[:, None]
    f1 = np.exp(-2j * np.pi * k1 * np.arange(n1 // 2)[None, :] / n1)
    tw = np.exp(-2j * np.pi * np.arange(n2)[:, None] * np.arange(n1)[None, :] / n)
    f1r, f1i = jnp.asarray(f1.real, F32), jnp.asarray(f1.imag, F32)
    twr, twi = jnp.asarray(tw.real, F32)[:, :, None], jnp.asarray(tw.imag, F32)[:, :, None]
    fr = twr * f1r - twi * f1i
    fi = twr * f1i + twi * f1r
    ta = jnp.concatenate([jnp.concatenate([fr, -fi], axis=2),
                          jnp.concatenate([fi, fr], axis=2)], axis=1)
    fb = np.exp(-2j * np.pi * np.arange(n2)[:, None] * np.arange(n2)[None, :] / n2)
    tb = np.block([[fb.real, -fb.imag], [fb.imag, fb.real]])
    return (ta.astype(BF16), jnp.swapaxes(ta, 1, 2).astype(BF16),
            jnp.asarray(tb, BF16), jnp.asarray(tb.T, BF16))


FFT_NB = BF16_SUBLANES


def _fft_a_kernel(x_ref, t_ref, o_ref, r_s):
    planes = x_ref.shape[1]
    xt = [jnp.swapaxes(x_ref[0, c], 0, 1) for c in range(planes)]
    for jj in range(FFT_NB):
        x = jnp.concatenate([xt[c][jj] for c in range(planes)], axis=0).astype(BF16)
        r_s[jj] = jnp.dot(t_ref[jj], x, preferred_element_type=F32)
    o_ref[0] = jnp.swapaxes(r_s[...], 0, 1).astype(o_ref.dtype)


def _fft_a(x5, ta, *, n2, d, seg):
    s, p, half, _, _ = x5.shape
    kdim = p * half
    tn = min(256, d)
    sb, db = (seg * d) // tn, d // tn
    return pl.pallas_call(
        _fft_a_kernel,
        grid=(s, n2 // FFT_NB, db),
        in_specs=[pl.BlockSpec((1, p, half, FFT_NB, tn), lambda si, m, j: (si, 0, 0, m, sb + j)),
                  pl.BlockSpec((FFT_NB, 2 * FFT_N1, kdim), lambda si, m, j: (m, 0, 0))],
        out_specs=pl.BlockSpec((1, 2 * FFT_N1, FFT_NB, tn), lambda si, m, j: (si, 0, m, j)),
        out_shape=jax.ShapeDtypeStruct((s, 2 * FFT_N1, n2, d), BF16),
        scratch_shapes=[pltpu.VMEM((FFT_NB, 2 * FFT_N1, tn), F32)],
        compiler_params=_params("parallel", "parallel", "parallel"),
        name="dft_stage_a",
    )(x5, ta)


def _fft_b_kernel(a_ref, af_ref, ab_ref, sc_ref, tb_ref, tbt_ref, o_ref, *, n2):
    tb = tb_ref[...]
    z = jnp.dot(tb, a_ref[...].reshape(2 * n2, -1), preferred_element_type=F32)
    hf = jnp.dot(tb, af_ref[0].reshape(2 * n2, -1), preferred_element_type=F32)
    hb = jnp.dot(tb, ab_ref[0].reshape(2 * n2, -1), preferred_element_type=F32)
    sc = sc_ref[0]
    gr = (hf[0:n2] + hb[0:n2]) * sc
    gi = (hf[n2:2 * n2] - hb[n2:2 * n2]) * sc
    zr, zi = z[0:n2], z[n2:2 * n2]
    prod = jnp.concatenate([zr * gr - zi * gi, zr * gi + zi * gr], axis=0).astype(BF16)
    y = jnp.dot(tbt_ref[...], prod, preferred_element_type=F32)
    o_ref[...] = y.reshape(o_ref.shape).astype(o_ref.dtype)


def _fft_b(a, af, scale, tb, tbt, order, *, n2, d):
    tn = min(2048, d)
    blk = (2, 1, n2, tn)
    fblk = (1, 2, 1, n2, tn)
    return pl.pallas_call(
        functools.partial(_fft_b_kernel, n2=n2),
        grid=(FFT_N1, d // tn),
        in_specs=[pl.BlockSpec(blk, lambda k, j: (0, k, 0, j)),
                  pl.BlockSpec(fblk, lambda k, j: (2 * order, 0, k, 0, j)),
                  pl.BlockSpec(fblk, lambda k, j: (2 * order + 1, 0, k, 0, j)),
                  pl.BlockSpec((1, 1, tn), lambda k, j: (order, 0, j)),
                  pl.BlockSpec((2 * n2, 2 * n2), lambda k, j: (0, 0)),
                  pl.BlockSpec((2 * n2, 2 * n2), lambda k, j: (0, 0))],
        out_specs=pl.BlockSpec(blk, lambda k, j: (0, k, 0, j)),
        out_shape=jax.ShapeDtypeStruct((2, FFT_N1, n2, d), BF16),
        compiler_params=_params("parallel", "parallel"),
        name="dft_stage_b_spectrum_product",
    )(a, af, af, scale, tb, tbt)


def _fft_ainv_kernel(y_ref, t_ref, u_ref, gate_ref, d_ref, o_ref, r_s):
    half = o_ref.shape[1]
    yt = jnp.swapaxes(y_ref[...].astype(F32), 0, 1).astype(BF16)
    for c in range(2):
        r_s[c] = jnp.swapaxes(u_ref[c], 0, 1) * d_ref[...]
    for jj in range(FFT_NB):
        x = jnp.dot(t_ref[jj], yt[jj], preferred_element_type=F32)
        for c in range(2):
            r_s[c, jj] += x[c * half:(c + 1) * half]
    for c in range(2):
        o_ref[c] = gate_ref[c] * jnp.swapaxes(r_s[c], 0, 1)


def _fft_ainv(y3, tat, u4, gate4, dskip, *, n2, d, useg, gseg):
    b, half, _, _ = u4.shape
    tn = min(128, d)
    db = d // tn
    usb, gsb = (useg * d) // tn, (gseg * d) // tn
    return pl.pallas_call(
        _fft_ainv_kernel,
        grid=(n2 // FFT_NB, db),
        in_specs=[pl.BlockSpec((2 * FFT_N1, FFT_NB, tn), lambda m, j: (0, m, j)),
                  pl.BlockSpec((FFT_NB, FFT_N1, 2 * FFT_N1), lambda m, j: (m, 0, 0)),
                  pl.BlockSpec((b, half, FFT_NB, tn), lambda m, j: (0, 0, m, usb + j)),
                  pl.BlockSpec((b, half, FFT_NB, tn), lambda m, j: (0, 0, m, gsb + j)),
                  pl.BlockSpec((1, tn), lambda m, j: (0, j))],
        out_specs=pl.BlockSpec((b, half, FFT_NB, tn), lambda m, j: (0, 0, m, j)),
        out_shape=jax.ShapeDtypeStruct((b, half, n2, d), F32),
        scratch_shapes=[pltpu.VMEM((b, FFT_NB, half, tn), F32)],
        compiler_params=_params("parallel", "parallel"),
        name="dft_stage_a_inverse_gate",
    )(y3, tat, u4, gate4, dskip)


def _hyena_mixer(h, hy, l, d):
    (w_in, b_in, sconv_w, sconv_b, f_w0, f_b0, f_w1, f_b1, f_w2, f_b2, f_freq, f_w3, d_skip, w_out) = hy
    b = h.shape[0]
    assert b == 2, "the two batch rows ride the real / imaginary planes of one complex DFT"
    n1 = FFT_N1
    half = n1 // 2
    n2 = (2 * l) // n1
    assert half * n2 == l
    z3 = _mm_conv(_hy_in_kernel, h, [w_in.astype(BF16)],
                  [b_in.reshape(1, -1), sconv_w, sconv_b.reshape(1, -1)], 3 * d, F32, "hyena_in_proj_conv", nz=2)
    ta, tat, tb, tbt = _dft_tables(l)
    nseg = 2 * HY_ORDER
    af, fsum = _hyena_filter_spectra_a(l, d, ta, f_w0, f_b0, f_w1, f_b1, f_w2, f_b2, f_freq, f_w3)
    af = af.reshape(nseg, 2, n1, n2, d)
    scale = 1.0 / ((fsum[0::2] + fsum[1::2]) * (2 * l))
    z4 = z3.reshape(b, half, n2, 3 * d)
    y, yseg = z4, 0
    for o in range(HY_ORDER):
        a = _fft_a(y[None], ta, n2=n2, d=d, seg=yseg)
        yb = _fft_b(a.reshape(2, n1, n2, d), af, scale, tb, tbt, o, n2=n2, d=d)
        y = _fft_ainv(yb.reshape(2 * n1, n2, d), tat, y, z4, d_skip[o].reshape(1, d),
                      n2=n2, d=d, useg=yseg, gseg=o + 1)
        yseg = 0
    return y.reshape(b, l, d), w_out.astype(BF16)


def _rope_tables(l):
    t = jnp.arange(l)
    row = (t // GRID_W).astype(F32)[:, None]
    col = (t % GRID_W).astype(F32)[:, None]
    width = 2 * HEAD_DIM
    cos, sin, avg, perm = [], [], [], []
    for dim in (A_HALF, HEAD_DIM):
        nf = dim // 4
        inv = jnp.asarray(ROPE_BASE ** (-np.arange(nf) / nf), F32)[None, :]
        ar, ac = row * inv, col * inv
        c = jnp.concatenate([jnp.cos(ar), jnp.cos(ar), jnp.cos(ac), jnp.cos(ac)], axis=1)
        s = jnp.concatenate([-jnp.sin(ar), jnp.sin(ar), -jnp.sin(ac), jnp.sin(ac)], axis=1)
        reps = width // dim
        cos.append(jnp.tile(c, (1, reps)))
        sin.append(jnp.tile(s, (1, reps)))
        lane = np.arange(width)
        avg.append((lane[:, None] // dim == lane[None, :] // dim) / dim)
        partner = np.where(lane % (2 * nf) < nf, lane + nf, lane - nf)
        perm.append((lane[:, None] == partner[None, :]).astype(np.float64))
    return (jnp.stack(cos), jnp.stack(sin), jnp.asarray(np.stack(avg), BF16), jnp.asarray(np.stack(perm), BF16))


def _attn_mixer(h, hc, lam_init, w_in, w_out, a_q_g, a_k_g, lq1, lk1, lq2, lk2, a_sub_g, b_q_g, b_k_g, b_sink):
    b, l, d = h.shape
    c = hc.shape[1]
    a_heads = d // (2 * HEAD_DIM)
    b_heads = d // (2 * HEAD_DIM)
    q_cols = (a_heads + b_heads) * HEAD_DIM
    in_cols = w_in.shape[1]
    lam = (jnp.exp(jnp.sum(lq1 * lk1)) - jnp.exp(jnp.sum(lq2 * lk2)) + lam_init).reshape(1, 1)
    sink = b_sink.reshape(B_KV_HEADS, b_heads // B_KV_HEADS)

    ones = lambda n: jnp.ones((n,), F32)
    tile = lambda v, n: jnp.tile(v, n)
    gain = jnp.concatenate([tile(a_q_g, 2 * a_heads), tile(b_q_g, b_heads), tile(a_k_g, 2 * a_heads),
                            ones(a_heads * HEAD_DIM), tile(b_k_g, B_KV_HEADS),
                            ones(B_KV_HEADS * HEAD_DIM)]).reshape(1, in_cols)
    scale = jnp.concatenate([jnp.full((a_heads * HEAD_DIM,), A_HALF ** -0.5 * math.log2(math.e), F32),
                             jnp.full((b_heads * HEAD_DIM,), HEAD_DIM ** -0.5, F32),
                             ones(in_cols - q_cols)]).reshape(1, in_cols)
    e = [0, a_heads // 2, (a_heads + b_heads) // 2, (2 * a_heads + b_heads) // 2,
         (3 * a_heads + b_heads) // 2, (3 * a_heads + b_heads) // 2 + 1, (3 * a_heads + b_heads) // 2 + 2]
    plain = ((e[3], e[4]), (e[5], e[6]))
    cos_tab, sin_tab, avg, perm = _rope_tables(l)

    def type_of(j0):
        def f(j):
            jj = j + j0
            is_a = (jj < e[1]) | ((jj >= e[2]) & (jj < e[3]))
            return jnp.where(is_a, 0, 1)
        return f

    w_bf = _col_tiles(w_in.astype(BF16), 2 * HEAD_DIM)
    p = _inproj(h.reshape(b * l, d), w_bf, gain, scale, cos_tab, sin_tab, avg, perm, plain=plain,
                type_of_block=type_of(0), col0=0, ncols=in_cols, l=l, rope=True).reshape(b, l, in_cols)
    kv0 = e[2]
    plain_c = tuple((lo - kv0, hi - kv0) for lo, hi in plain)
    pc = _inproj(hc.reshape(b * c, d), w_bf, gain, scale, cos_tab, sin_tab, avg, perm, plain=plain_c,
                 type_of_block=type_of(kv0), col0=kv0, ncols=in_cols - q_cols, l=c, rope=False
                 ).reshape(b, c, in_cols - q_cols)
    score_bound = (A_HALF ** 0.5 * math.log2(math.e)) * jnp.max(jnp.abs(a_q_g)) * jnp.max(jnp.abs(a_k_g))
    oa = _diffattn(lam, p, pc, a_sub_g, score_bound, heads=a_heads, l=l, c=c, out_scale=1.0 - lam_init)
    ob = _winattn(sink, p, pc, a_heads=a_heads, b_heads=b_heads, l=l, c=c)
    w_out_bf = w_out.astype(BF16)
    na = a_heads * HEAD_DIM
    return [oa, ob], [w_out_bf[:na], w_out_bf[na:]]


def kernel(x, c, ctx, c_ctx, ada_w, ada_b, norm1_g, norm2_g, attn_w_in, attn_w_out, a_q_g, a_k_g, a_lam_q1, a_lam_k1, a_lam_q2, a_lam_k2, a_sub_g, b_q_g, b_k_g, b_sink, hy_w_in, hy_b_in, hy_sconv_w, hy_sconv_b, hy_f_w0, hy_f_b0, hy_f_w1, hy_f_b1, hy_f_w2, hy_f_b2, hy_f_freq, hy_f_w3, hy_d, hy_w_out, ffn_w_gate, ffn_w_val, ffn_conv_w, ffn_conv_b, ffn_w_down):
    b, l, d = x.shape
    depth = ada_w.shape[0]
    cc = jnp.concatenate([c, c_ctx[None, :], jnp.zeros((8 - b - 1, d), F32)], axis=0)
    m = _ada(cc, ada_w, ada_b)
    xs = x
    for layer in range(depth):
        i = layer // 2
        lat = [m[layer, :b, k * d:(k + 1) * d].reshape(b, 1, d) for k in range(6)]
        sh1, sc1, g1, sh2, sc2, g2 = lat
        h = _normmod(xs, norm1_g[layer], sc1, sh1)
        if layer % 2 == 0:
            mc = [jnp.broadcast_to(m[layer, b, k * d:(k + 1) * d].reshape(1, 1, d), (b, 1, d)) for k in range(2)]
            hc = _normmod(ctx, norm1_g[layer], mc[1], mc[0])
            parts, ws = _attn_mixer(h, hc, 0.8 - 0.6 * math.exp(-0.3 * layer), attn_w_in[i], attn_w_out[i],
                                    a_q_g[i], a_k_g[i], a_lam_q1[i], a_lam_k1[i], a_lam_q2[i], a_lam_k2[i],
                                    a_sub_g[i], b_q_g[i], b_k_g[i], b_sink[i])
        else:
            hy = (hy_w_in[i], hy_b_in[i], hy_sconv_w[i], hy_sconv_b[i], hy_f_w0[i], hy_f_b0[i], hy_f_w1[i],
                  hy_f_b1[i], hy_f_w2[i], hy_f_b2[i], hy_f_freq[i], hy_f_w3[i], hy_d[i], hy_w_out[i])
            y, w_o = _hyena_mixer(h, hy, l, d)
            parts, ws = [y], [w_o]
        xs = _mm_resid(parts, ws, xs, g1)
        h2 = _normmod(xs, norm2_g[layer], sc2, sh2)
        d_ff = ffn_w_gate.shape[2]
        hid = _mm_conv(_ffn_up_kernel, h2, [ffn_w_gate[layer].astype(BF16), ffn_w_val[layer].astype(BF16)],
                       [ffn_conv_w[layer], ffn_conv_b[layer].reshape(1, -1)], d_ff, BF16, "ffn_up_conv_glu", nz=2)
        xs = _mm_resid([hid], [ffn_w_down[layer].astype(BF16)], xs, g2)
    return xs
```

```python
import functools
import math

import numpy as np
import jax
import jax.numpy as jnp
from jax import lax
from jax.experimental import pallas as pl
from jax.experimental.pallas import tpu as pltpu

F32 = jnp.float32
BF16 = jnp.bfloat16
HIGHEST = lax.Precision.HIGHEST

HEAD_DIM = 128
A_HALF = HEAD_DIM // 2
B_KV_HEADS = 2
GRID_W = 64
WINDOW = 128
ROPE_BASE = 10000.0
EPS = 1e-6
NEG = -1e30
HY_ORDER = 2
HY_DECAY_TARGET = 1e-2
HY_MAX_DECAY_PCT = 0.3
HY_MIN_DECAY_PCT = 1.5

LANES = 128
BF16_SUBLANES = 16
FFT_N1 = 256
VMEM_LIMIT_BYTES = 56 * 1024 * 1024


def _params(*sem):
    return pltpu.CompilerParams(dimension_semantics=sem, vmem_limit_bytes=VMEM_LIMIT_BYTES)


def _col_tiles(w, tn):
    k, n = w.shape
    return jnp.transpose(w.reshape(k, n // tn, tn), (1, 0, 2))


def _ada_kernel(c_ref, w_ref, b_ref, o_ref):
    c = c_ref[...]
    s = c * (1.0 / (1.0 + jnp.exp(-c)))
    o_ref[0] = jnp.dot(s, w_ref[0], preferred_element_type=F32, precision=HIGHEST) + b_ref[0]


def _ada(cc, ada_w, ada_b):
    depth, d, n = ada_w.shape
    tn = 1024
    return pl.pallas_call(
        _ada_kernel,
        grid=(depth, n // tn),
        in_specs=[pl.BlockSpec((8, d), lambda l, j: (0, 0)),
                  pl.BlockSpec((1, d, tn), lambda l, j: (l, 0, j)),
                  pl.BlockSpec((1, 1, tn), lambda l, j: (l, 0, j))],
        out_specs=pl.BlockSpec((1, 8, tn), lambda l, j: (l, 0, j)),
        out_shape=jax.ShapeDtypeStruct((depth, 8, n), F32),
        compiler_params=_params("parallel", "parallel"),
        name="ada_modulation",
    )(cc, ada_w, ada_b.reshape(depth, 1, n))


def _normmod_kernel(x_ref, g_ref, sc_ref, sh_ref, o_ref):
    x = x_ref[0]
    ms = jnp.mean(x * x, axis=-1, keepdims=True)
    y = x * lax.rsqrt(ms + EPS) * g_ref[...]
    o_ref[0] = (y * (1.0 + sc_ref[0]) + sh_ref[0]).astype(o_ref.dtype)


def _normmod(x, g, sc, sh):
    b, l, d = x.shape
    tm = min(512, l)
    return pl.pallas_call(
        _normmod_kernel,
        grid=(b, l // tm),
        in_specs=[pl.BlockSpec((1, tm, d), lambda bi, i: (bi, i, 0)),
                  pl.BlockSpec((1, d), lambda bi, i: (0, 0)),
                  pl.BlockSpec((1, 1, d), lambda bi, i: (bi, 0, 0)),
                  pl.BlockSpec((1, 1, d), lambda bi, i: (bi, 0, 0))],
        out_specs=pl.BlockSpec((1, tm, d), lambda bi, i: (bi, i, 0)),
        out_shape=jax.ShapeDtypeStruct((b, l, d), BF16),
        compiler_params=_params("parallel", "parallel"),
        name="rmsnorm_modulate",
    )(x, g.reshape(1, d), sc, sh)


def _dot_split(x, m):
    hi = x.astype(BF16)
    lo = (x - hi.astype(F32)).astype(BF16)
    return jnp.dot(hi, m, preferred_element_type=F32) + jnp.dot(lo, m, preferred_element_type=F32)


def _inproj_kernel(h_ref, w_ref, gain_ref, scale_ref, cos_ref, sin_ref, avg_ref, perm_ref, o_ref, *, plain, rope):
    j = pl.program_id(1)
    z = jnp.dot(h_ref[...], w_ref[0], preferred_element_type=F32)
    is_plain = functools.reduce(jnp.logical_or, [(j >= lo) & (j < hi) for lo, hi in plain])

    @pl.when(is_plain)
    def _():
        o_ref[...] = z.astype(o_ref.dtype)

    @pl.when(jnp.logical_not(is_plain))
    def _():
        rows = min(256, z.shape[0])
        for r0 in range(0, z.shape[0], rows):
            rs = slice(r0, r0 + rows)
            zc = z[rs]
            ms = _dot_split(zc * zc, avg_ref[0])
            y = zc * lax.rsqrt(ms + EPS) * gain_ref[...]
            if rope:
                y = y * cos_ref[0, rs, :] + _dot_split(y, perm_ref[0]) * sin_ref[0, rs, :]
            o_ref[rs, :] = (y * scale_ref[...]).astype(o_ref.dtype)


def _inproj(h2d, w, gain, scale, cos_tab, sin_tab, avg, perm, *, plain, type_of_block, col0, ncols, l, rope):
    m, d = h2d.shape
    tn = 2 * HEAD_DIM
    tm = min(1024, l)
    pos_blocks = l // tm
    kern = functools.partial(_inproj_kernel, plain=plain, rope=rope)
    by_type = lambda i, j: (type_of_block(j), 0, 0)
    return pl.pallas_call(
        kern,
        grid=(m // tm, ncols // tn),
        in_specs=[pl.BlockSpec((tm, d), lambda i, j: (i, 0)),
                  pl.BlockSpec((1, d, tn), lambda i, j: (j + col0, 0, 0)),
                  pl.BlockSpec((1, tn), lambda i, j: (0, j + col0)),
                  pl.BlockSpec((1, tn), lambda i, j: (0, j + col0)),
                  pl.BlockSpec((1, tm, tn), lambda i, j: (type_of_block(j), i % pos_blocks, 0)),
                  pl.BlockSpec((1, tm, tn), lambda i, j: (type_of_block(j), i % pos_blocks, 0)),
                  pl.BlockSpec((1, tn, tn), by_type),
                  pl.BlockSpec((1, tn, tn), by_type)],
        out_specs=pl.BlockSpec((tm, tn), lambda i, j: (i, j)),
        out_shape=jax.ShapeDtypeStruct((m, ncols), BF16),
        compiler_params=_params("parallel", "arbitrary"),
        name="attn_in_projection",
    )(h2d, w, gain, scale, cos_tab, sin_tab, avg, perm)


def _diffattn_kernel(lam_ref, q_ref, kc_ref, vc_ref, k_ref, v_ref, g_ref, o_ref,
                     q2_s, m_s, l_s, acc_s, *, tq, nkv, out_scale):
    kv = pl.program_id(3)

    def process(k, v):
        s = lax.dot_general(q2_s[...], k, (((1,), (1,)), ((), ())), preferred_element_type=F32)
        m_prev = m_s[...]
        m_new = jnp.maximum(m_prev, jnp.max(s, axis=-1, keepdims=True))
        alpha = jnp.exp2(m_prev - m_new)
        p = jnp.exp2(s - m_new)
        l_s[...] = alpha * l_s[...] + jnp.sum(p, axis=-1, keepdims=True)
        acc_s[...] = alpha * acc_s[...] + jnp.dot(p.astype(BF16), v, preferred_element_type=F32)
        m_s[...] = m_new

    @pl.when(kv == 0)
    def _():
        q = q_ref[0]
        lane = lax.broadcasted_iota(jnp.int32, q.shape, 1)
        zero = jnp.zeros_like(q)
        q2_s[0:tq, :] = jnp.where(lane < A_HALF, q, zero)
        q2_s[tq:2 * tq, :] = jnp.where(lane >= A_HALF, q, zero)
        m_s[...] = jnp.full(m_s.shape, -jnp.inf, F32)
        l_s[...] = jnp.zeros(l_s.shape, F32)
        acc_s[...] = jnp.zeros(acc_s.shape, F32)
        process(kc_ref[0], vc_ref[0])

    process(k_ref[0], v_ref[0])

    @pl.when(kv == nkv - 1)
    def _():
        o = acc_s[...] / l_s[...]
        d = o[0:tq] - lam_ref[0, 0] * o[tq:2 * tq]
        ms = jnp.mean(d * d, axis=-1, keepdims=True)
        o_ref[0] = (d * lax.rsqrt(ms + EPS) * g_ref[...] * out_scale).astype(o_ref.dtype)


def _diffattn_fast_kernel(lam_ref, q_ref, kc_ref, vct_ref, k_ref, vt_ref, g_ref, o_ref,
                          q2_s, l_s, acc_s, p_s, *, tq, tk, kc, nkv, out_scale):
    kv = pl.program_id(3)
    dn = (((1,), (1,)), ((), ()))

    def weights(k):
        n = k.shape[0]
        p = jnp.exp2(lax.dot_general(k, q2_s[...], dn, preferred_element_type=F32))
        return p.astype(BF16), jnp.sum(p.reshape(n // 8, 8, 2 * tq), axis=0)

    @pl.when(kv == 0)
    def _():
        q = q_ref[0]
        lane = lax.broadcasted_iota(jnp.int32, q.shape, 1)
        zero = jnp.zeros_like(q)
        q2_s[0:tq, :] = jnp.where(lane < A_HALF, q, zero)
        q2_s[tq:2 * tq, :] = jnp.where(lane >= A_HALF, q, zero)
        p, ls = weights(kc_ref[0])
        l_s[...] = ls
        acc_s[...] = jnp.dot(vct_ref[0, 0], p, preferred_element_type=F32)

    ls = l_s[...]
    for ci in range(tk // kc):
        p, lc = weights(k_ref[0, ci * kc:(ci + 1) * kc, :])
        p_s[ci * kc:(ci + 1) * kc, :] = p
        ls = ls + lc
    l_s[...] = ls
    acc_s[...] += jnp.dot(vt_ref[0, 0], p_s[...], preferred_element_type=F32)

    @pl.when(kv == nkv - 1)
    def _():
        l = jnp.sum(l_s[...], axis=0, keepdims=True)
        o = acc_s[...] / l
        d = o[:, 0:tq] - lam_ref[0, 0] * o[:, tq:2 * tq]
        ms = jnp.mean(d * d, axis=0, keepdims=True)
        y = d * lax.rsqrt(ms + EPS) * g_ref[...] * out_scale
        o_ref[0] = y.T.astype(o_ref.dtype)


def _diffattn_fast(lam, p, pc, sub_g, *, heads, l, c, out_scale):
    b = p.shape[0]
    tq = min(1024, l)
    tk = min(4096, l)
    kc = min(512, tk)
    nkv = l // tk
    k_blk = 2 * heads
    hd = heads * HEAD_DIM
    vt = jnp.transpose(p[:, :, 3 * hd:4 * hd].reshape(b, l, heads, HEAD_DIM), (0, 2, 3, 1))
    vct = jnp.transpose(pc[:, :, hd:2 * hd].reshape(b, c, heads, HEAD_DIM), (0, 2, 3, 1))
    kern = functools.partial(_diffattn_fast_kernel, tq=tq, tk=tk, kc=kc, nkv=nkv, out_scale=out_scale)
    return pl.pallas_call(
        kern,
        grid=(b, heads, l // tq, nkv),
        in_specs=[pl.BlockSpec(memory_space=pltpu.SMEM),
                  pl.BlockSpec((1, tq, HEAD_DIM), lambda bi, h, i, kv: (bi, i, h)),
                  pl.BlockSpec((1, c, HEAD_DIM), lambda bi, h, i, kv: (bi, 0, h)),
                  pl.BlockSpec((1, 1, HEAD_DIM, c), lambda bi, h, i, kv: (bi, h, 0, 0)),
                  pl.BlockSpec((1, tk, HEAD_DIM), lambda bi, h, i, kv: (bi, kv, k_blk + h)),
                  pl.BlockSpec((1, 1, HEAD_DIM, tk), lambda bi, h, i, kv: (bi, h, 0, kv)),
                  pl.BlockSpec((HEAD_DIM, 1), lambda bi, h, i, kv: (0, 0))],
        out_specs=pl.BlockSpec((1, tq, HEAD_DIM), lambda bi, h, i, kv: (bi, i, h)),
        out_shape=jax.ShapeDtypeStruct((b, l, hd), BF16),
        scratch_shapes=[pltpu.VMEM((2 * tq, HEAD_DIM), BF16),
                        pltpu.VMEM((8, 2 * tq), F32),
                        pltpu.VMEM((HEAD_DIM, 2 * tq), F32),
                        pltpu.VMEM((tk, 2 * tq), BF16)],
        compiler_params=_params("parallel", "parallel", "parallel", "arbitrary"),
        name="diff_attention_bounded",
    )(lam, p, pc, vct, p, vt, sub_g.reshape(HEAD_DIM, 1))


SCORE_BOUND_LOG2 = 60.0


def _diffattn(lam, p, pc, sub_g, score_bound, *, heads, l, c, out_scale):
    kw = dict(heads=heads, l=l, c=c, out_scale=out_scale)
    return lax.cond(score_bound < SCORE_BOUND_LOG2,
                    lambda *a: _diffattn_fast(*a, **kw), lambda *a: _diffattn_safe(*a, **kw),
                    lam, p, pc, sub_g)


def _diffattn_safe(lam, p, pc, sub_g, *, heads, l, c, out_scale):
    b = p.shape[0]
    tq = min(256, l)
    tk = min(512, l)
    nkv = l // tk
    q_blk = 0
    k_blk = 2 * heads
    v_blk = 3 * heads
    kern = functools.partial(_diffattn_kernel, tq=tq, nkv=nkv, out_scale=out_scale)
    return pl.pallas_call(
        kern,
        grid=(b, heads, l // tq, nkv),
        in_specs=[pl.BlockSpec(memory_space=pltpu.SMEM),
                  pl.BlockSpec((1, tq, HEAD_DIM), lambda bi, h, i, kv: (bi, i, q_blk + h)),
                  pl.BlockSpec((1, c, HEAD_DIM), lambda bi, h, i, kv: (bi, 0, h)),
                  pl.BlockSpec((1, c, HEAD_DIM), lambda bi, h, i, kv: (bi, 0, heads + h)),
                  pl.BlockSpec((1, tk, HEAD_DIM), lambda bi, h, i, kv: (bi, kv, k_blk + h)),
                  pl.BlockSpec((1, tk, HEAD_DIM), lambda bi, h, i, kv: (bi, kv, v_blk + h)),
                  pl.BlockSpec((1, HEAD_DIM), lambda bi, h, i, kv: (0, 0))],
        out_specs=pl.BlockSpec((1, tq, HEAD_DIM), lambda bi, h, i, kv: (bi, i, h)),
        out_shape=jax.ShapeDtypeStruct((b, l, heads * HEAD_DIM), BF16),
        scratch_shapes=[pltpu.VMEM((2 * tq, HEAD_DIM), BF16),
                        pltpu.VMEM((2 * tq, 1), F32),
                        pltpu.VMEM((2 * tq, 1), F32),
                        pltpu.VMEM((2 * tq, HEAD_DIM), F32)],
        compiler_params=_params("parallel", "parallel", "parallel", "arbitrary"),
        name="diff_attention",
    )(lam, p, pc, pc, p, p, sub_g.reshape(1, HEAD_DIM))


def _winattn_kernel(sink_ref, q_ref, kp_ref, km_ref, kn_ref, vp_ref, vm_ref, vn_ref, kc_ref, vc_ref,
                    o_ref, *, tq, l, group):
    g = pl.program_id(1)
    i = pl.program_id(2)
    kband = jnp.concatenate([kp_ref[0], km_ref[0], kn_ref[0]], axis=0)
    vband = jnp.concatenate([vp_ref[0], vm_ref[0], vn_ref[0]], axis=0)
    nk = tq + 2 * WINDOW
    qpos = i * tq + lax.broadcasted_iota(jnp.int32, (tq, nk), 0)
    kpos = i * tq - WINDOW + lax.broadcasted_iota(jnp.int32, (tq, nk), 1)
    valid = (jnp.abs(kpos - qpos) <= WINDOW) & (kpos >= 0) & (kpos < l)
    kc = kc_ref[0]
    vc = vc_ref[0]
    dn = (((1,), (1,)), ((), ()))
    for r in range(group):
        sl = slice(r * HEAD_DIM, (r + 1) * HEAD_DIM)
        q = q_ref[0, :, sl]
        s_lat = jnp.where(valid, lax.dot_general(q, kband, dn, preferred_element_type=F32), NEG)
        s_ctx = lax.dot_general(q, kc, dn, preferred_element_type=F32)
        sk = sink_ref[g, r]
        m = jnp.maximum(jnp.maximum(jnp.max(s_lat, axis=-1, keepdims=True),
                                    jnp.max(s_ctx, axis=-1, keepdims=True)), sk)
        p_lat = jnp.exp(s_lat - m)
        p_ctx = jnp.exp(s_ctx - m)
        denom = (jnp.sum(p_lat, axis=-1, keepdims=True) + jnp.sum(p_ctx, axis=-1, keepdims=True)
                 + jnp.exp(sk - m))
        o = (jnp.dot(p_lat.astype(BF16), vband, preferred_element_type=F32)
             + jnp.dot(p_ctx.astype(BF16), vc, preferred_element_type=F32))
        o_ref[0, :, sl] = (o / denom).astype(o_ref.dtype)


def _winattn(sink, p, pc, *, a_heads, b_heads, l, c):
    b = p.shape[0]
    group = b_heads // B_KV_HEADS
    tq = min(256, l)
    wpb = tq // WINDOW
    nwb = l // WINDOW
    gw = group * HEAD_DIM
    q_blk = a_heads * HEAD_DIM // gw
    k_blk = (a_heads + b_heads) + 2 * a_heads
    v_blk = k_blk + B_KV_HEADS
    kc_blk = 2 * a_heads
    vc_blk = kc_blk + B_KV_HEADS
    kern = functools.partial(_winattn_kernel, tq=tq, l=l, group=group)

    def prev_map(col):
        return lambda bi, g, i: (bi, jnp.maximum(i * wpb - 1, 0), col + g)

    def main_map(col):
        return lambda bi, g, i: (bi, i, col + g)

    def next_map(col):
        return lambda bi, g, i: (bi, jnp.minimum((i + 1) * wpb, nwb - 1), col + g)

    return pl.pallas_call(
        kern,
        grid=(b, B_KV_HEADS, l // tq),
        in_specs=[pl.BlockSpec(memory_space=pltpu.SMEM),
                  pl.BlockSpec((1, tq, gw), lambda bi, g, i: (bi, i, q_blk + g)),
                  pl.BlockSpec((1, WINDOW, HEAD_DIM), prev_map(k_blk)),
                  pl.BlockSpec((1, tq, HEAD_DIM), main_map(k_blk)),
                  pl.BlockSpec((1, WINDOW, HEAD_DIM), next_map(k_blk)),
                  pl.BlockSpec((1, WINDOW, HEAD_DIM), prev_map(v_blk)),
                  pl.BlockSpec((1, tq, HEAD_DIM), main_map(v_blk)),
                  pl.BlockSpec((1, WINDOW, HEAD_DIM), next_map(v_blk)),
                  pl.BlockSpec((1, c, HEAD_DIM), lambda bi, g, i: (bi, 0, kc_blk + g)),
                  pl.BlockSpec((1, c, HEAD_DIM), lambda bi, g, i: (bi, 0, vc_blk + g))],
        out_specs=pl.BlockSpec((1, tq, gw), lambda bi, g, i: (bi, i, g)),
        out_shape=jax.ShapeDtypeStruct((b, l, b_heads * HEAD_DIM), BF16),
        compiler_params=_params("parallel", "parallel", "parallel"),
        name="window_attention",
    )(sink, p, p, p, p, p, p, p, pc, pc)


def _mm_resid_kernel(*refs, nparts):
    a_refs = refs[0:nparts]
    w_refs = refs[nparts:2 * nparts]
    res_ref, gate_ref, o_ref = refs[2 * nparts:]
    y = jnp.dot(a_refs[0][0].astype(BF16), w_refs[0][0], preferred_element_type=F32)
    for a_ref, w_ref in zip(a_refs[1:], w_refs[1:]):
        y = y + jnp.dot(a_ref[0].astype(BF16), w_ref[0], preferred_element_type=F32)
    o_ref[0] = res_ref[0] + gate_ref[0] * y


def _mm_resid(parts, ws, res, gate):
    b, l, n = res.shape
    tm = min(1024, l)
    tn = min(512, n)
    nparts = len(parts)
    ws = [_col_tiles(w, tn) for w in ws]
    in_specs = ([pl.BlockSpec((1, tm, a.shape[2]), lambda bi, i, j: (bi, i, 0)) for a in parts]
                + [pl.BlockSpec((1, w.shape[1], tn), lambda bi, i, j: (j, 0, 0)) for w in ws]
                + [pl.BlockSpec((1, tm, tn), lambda bi, i, j: (bi, i, j)),
                   pl.BlockSpec((1, 1, tn), lambda bi, i, j: (bi, 0, j))])
    return pl.pallas_call(
        functools.partial(_mm_resid_kernel, nparts=nparts),
        grid=(b, l // tm, n // tn),
        in_specs=in_specs,
        out_specs=pl.BlockSpec((1, tm, tn), lambda bi, i, j: (bi, i, j)),
        out_shape=jax.ShapeDtypeStruct((b, l, n), F32),
        compiler_params=_params("parallel", "parallel", "arbitrary"),
        name="out_projection_gated_residual",
    )(*parts, *ws, res, gate)


HALO = BF16_SUBLANES


def _fill_rows(hbuf, hp_ref, hm_ref, hn_ref, tm):
    hbuf[0:HALO, :] = hp_ref[0]
    hbuf[HALO:HALO + tm, :] = hm_ref[0]
    hbuf[HALO + tm:2 * HALO + tm, :] = hn_ref[0]


def _conv3(zbuf, cw_ref, cb_ref, tm, first, last):
    lo = slice(HALO - 1, HALO)
    hi = slice(HALO + tm, HALO + tm + 1)
    zbuf[lo, :] = jnp.where(first, 0.0, zbuf[lo, :])
    zbuf[hi, :] = jnp.where(last, 0.0, zbuf[hi, :])
    z = zbuf[...]
    rows = z.shape[0]
    g = (z * cw_ref[1:2, :] + pltpu.roll(z * cw_ref[0:1, :], 1, 0) + pltpu.roll(z * cw_ref[2:3, :], rows - 1, 0))
    return g[HALO:HALO + tm] + cb_ref[...]


def _gelu_tanh(x):
    return 0.5 * x * (1.0 + jnp.tanh(math.sqrt(2.0 / math.pi) * (x + 0.044715 * (x * x * x))))


def _ffn_up_kernel(hp_ref, hm_ref, hn_ref, wg_ref, wv_ref, cw_ref, cb_ref, o_ref, hbuf, zbuf, zbuf2, *, tm, nt):
    i = pl.program_id(1)
    j = pl.program_id(2)

    @pl.when(j == 0)
    def _():
        _fill_rows(hbuf, hp_ref, hm_ref, hn_ref, tm)

    hw = zbuf.shape[1]
    for hh, zb in enumerate((zbuf, zbuf2)):
        cs = slice(hh * hw, (hh + 1) * hw)
        zb[...] = jnp.dot(hbuf[...], wg_ref[0, :, cs], preferred_element_type=F32)
        g = _conv3(zb, cw_ref.at[:, cs], cb_ref.at[:, cs], tm, i == 0, i == nt - 1)
        v = jnp.dot(hbuf[HALO:HALO + tm, :], wv_ref[0, :, cs], preferred_element_type=F32)
        o_ref[0, :, cs] = (_gelu_tanh(g) * v).astype(o_ref.dtype)


def _hy_in_kernel(hp_ref, hm_ref, hn_ref, w_ref, b_ref, cw_ref, cb_ref, o_ref, hbuf, zbuf, zbuf2, *, tm, nt):
    i = pl.program_id(1)
    j = pl.program_id(2)

    @pl.when(j == 0)
    def _():
        _fill_rows(hbuf, hp_ref, hm_ref, hn_ref, tm)

    hw = zbuf.shape[1]
    for hh, zb in enumerate((zbuf, zbuf2)):
        cs = slice(hh * hw, (hh + 1) * hw)
        zb[...] = jnp.dot(hbuf[...], w_ref[0, :, cs], preferred_element_type=F32) + b_ref[:, cs]
        o_ref[0, :, cs] = _conv3(zb, cw_ref.at[:, cs], cb_ref.at[:, cs], tm, i == 0, i == nt - 1
                                 ).astype(o_ref.dtype)


def _mm_conv(kernel, h, ws, vecs, n, out_dtype, name, nz=1):
    b, l, k = h.shape
    tm = min(512, l)
    tn = min(512, n)
    nt = l // tm
    hpb = tm // HALO
    nhb = l // HALO
    ws = [_col_tiles(w, tn) for w in ws]
    in_specs = ([pl.BlockSpec((1, HALO, k), lambda bi, i, j: (bi, jnp.maximum(i * hpb - 1, 0), 0)),
                 pl.BlockSpec((1, tm, k), lambda bi, i, j: (bi, i, 0)),
                 pl.BlockSpec((1, HALO, k), lambda bi, i, j: (bi, jnp.minimum((i + 1) * hpb, nhb - 1), 0))]
                + [pl.BlockSpec((1, k, tn), lambda bi, i, j: (j, 0, 0)) for _ in ws]
                + [pl.BlockSpec((v.shape[0], tn), lambda bi, i, j: (0, j)) for v in vecs])
    return pl.pallas_call(
        functools.partial(kernel, tm=tm, nt=nt),
        grid=(b, nt, n // tn),
        in_specs=in_specs,
        out_specs=pl.BlockSpec((1, tm, tn), lambda bi, i, j: (bi, i, j)),
        out_shape=jax.ShapeDtypeStruct((b, l, n), out_dtype),
        scratch_shapes=[pltpu.VMEM((tm + 2 * HALO, k), BF16)] + [pltpu.VMEM((tm + 2 * HALO, tn // nz), F32)] * nz,
        compiler_params=_params("parallel", "parallel", "arbitrary"),
        name=name,
    )(h, h, h, *ws, *vecs)


FILT_NB = BF16_SUBLANES
FILT_K1 = FFT_N1 // 2 + 8


def _filter_dft_kernel(z_ref, w0_ref, b0_ref, w1_ref, b1_ref, w2_ref, b2_ref, fr_ref, w3_ref, dl_ref, t_ref,
                       o_ref, s_ref, a3_s, r_s):
    m, seg, j = pl.program_id(0), pl.program_id(1), pl.program_id(2)
    tn = o_ref.shape[3]
    fr = fr_ref[...]

    def layer(a, w_ref, b_ref):
        return jnp.sin(fr * (jnp.dot(a, w_ref[...], preferred_element_type=F32, precision=HIGHEST)
                             + b_ref[...]))

    @pl.when((seg == 0) & (j == 0))
    def _():
        for jj in range(FILT_NB):
            a = layer(layer(layer(z_ref[jj], w0_ref, b0_ref), w1_ref, b1_ref), w2_ref, b2_ref)
            a_hi = a.astype(BF16)
            a_lo = (a - a_hi.astype(F32)).astype(BF16)
            a3_s[jj] = jnp.concatenate([a_hi, a_hi, a_lo], axis=1)

    @pl.when((m == 0) & (seg == 0) & (j == 0))
    def _():
        s_ref[...] = jnp.zeros(s_ref.shape, F32)

    dl = dl_ref[...]
    backward = seg % 2 == 1
    abs_sum = jnp.zeros((1, tn), F32)
    for jj in range(FILT_NB):
        h = (jnp.dot(a3_s[jj], w3_ref[0], preferred_element_type=F32)
             * jnp.exp(-z_ref[jj][:, 0:1] * dl))
        if jj == 0:
            row = lax.broadcasted_iota(jnp.int32, h.shape, 0)
            h = jnp.where(backward & (m == 0) & (row == 0), 0.0, h)
        abs_sum = abs_sum + jnp.sum(jnp.abs(h), axis=0, keepdims=True)
        r_s[jj] = jnp.dot(t_ref[jj], h.astype(BF16), preferred_element_type=F32)
    o_ref[0] = jnp.swapaxes(r_s[...], 0, 1).astype(o_ref.dtype)
    col = pl.multiple_of(j * tn, tn)
    s_ref[seg, :, pl.ds(col, tn)] += abs_sum


def _hyena_filter_spectra_a(l, d, taf, f_w0, f_b0, f_w1, f_b1, f_w2, f_b2, f_freq, f_w3):
    emb, fw = f_w0.shape
    bands = (emb - 1) // 2
    half = FFT_N1 // 2
    n2 = l // half
    t = np.linspace(0.0, 1.0, l)[:, None]
    w = 2.0 * math.pi * np.arange(l)[:, None] / l
    f = np.linspace(1e-4, bands - 1, bands)[None, :]
    emb_pad = -(-emb // 8) * 8
    z = np.concatenate([t, np.cos(f * w), -np.sin(f * w), np.zeros((l, emb_pad - emb))], axis=-1)
    z = jnp.asarray(z.reshape(half, n2, emb_pad).transpose(1, 0, 2), F32)
    w0 = jnp.concatenate([f_w0, jnp.zeros((emb_pad - emb, fw), F32)], axis=0)
    max_decay = math.log(HY_DECAY_TARGET) / HY_MAX_DECAY_PCT
    min_decay = math.log(HY_DECAY_TARGET) / HY_MIN_DECAY_PCT
    dl = jnp.asarray(np.abs(np.linspace(min_decay, max_decay, d))[None, :], F32)
    nseg = 2 * HY_ORDER
    w3 = jnp.transpose(f_w3.reshape(fw, nseg, d), (1, 0, 2))
    w3_hi = w3.astype(BF16)
    w3_lo = (w3 - w3_hi.astype(F32)).astype(BF16)
    w3 = jnp.concatenate([w3_hi, w3_lo, w3_hi], axis=1)
    nb = min(FILT_NB, n2)
    assert nb == FILT_NB
    tn = min(256, d)
    vec = lambda a: a.reshape(1, fw)
    small = lambda shape: pl.BlockSpec(shape, lambda m, sg, j: (0,) * len(shape))
    return pl.pallas_call(
        _filter_dft_kernel,
        grid=(n2 // nb, nseg, d // tn),
        in_specs=[pl.BlockSpec((nb, half, emb_pad), lambda m, sg, j: (m, 0, 0)),
                  small((emb_pad, fw)), small((1, fw)), small((fw, fw)), small((1, fw)),
                  small((fw, fw)), small((1, fw)), small((1, fw)),
                  pl.BlockSpec((1, 3 * fw, tn), lambda m, sg, j: (sg, 0, j)),
                  pl.BlockSpec((1, tn), lambda m, sg, j: (0, j)),
                  pl.BlockSpec((nb, 2 * FILT_K1, half), lambda m, sg, j: (m, 0, 0))],
        out_specs=[pl.BlockSpec((1, 2 * FILT_K1, nb, tn), lambda m, sg, j: (sg, 0, m, j)),
                   pl.BlockSpec((nseg, 1, d), lambda m, sg, j: (0, 0, 0))],
        out_shape=[jax.ShapeDtypeStruct((nseg, 2 * FILT_K1, n2, d), BF16),
                   jax.ShapeDtypeStruct((nseg, 1, d), F32)],
        scratch_shapes=[pltpu.VMEM((nb, half, 3 * fw), BF16),
                        pltpu.VMEM((nb, 2 * FILT_K1, tn), F32)],
        compiler_params=_params("arbitrary", "arbitrary", "arbitrary"),
        name="hyena_filter_dft_stage_a",
    )(z, w0, vec(f_b0), f_w1, vec(f_b1), f_w2, vec(f_b2), vec(f_freq), w3, dl, taf)


def _dft_tables(l):
    n1 = FFT_N1
    n = 2 * l
    n2 = n // n1
    k1 = np.arange(n1)[:, None]
    f1 = np.exp(-2j * np.pi * k1 * np.arange(n1 // 2)[None, :] / n1)
    tw = np.exp(-2j * np.pi * np.arange(n2)[:, None] * np.arange(n1)[None, :] / n)
    f1r, f1i = jnp.asarray(f1.real, F32), jnp.asarray(f1.imag, F32)
    twr, twi = jnp.asarray(tw.real, F32)[:, :, None], jnp.asarray(tw.imag, F32)[:, :, None]
    fr = twr * f1r - twi * f1i
    fi = twr * f1i + twi * f1r
    ta = jnp.concatenate([jnp.concatenate([fr, -fi], axis=2),
                          jnp.concatenate([fi, fr], axis=2)], axis=1)
    fb = np.exp(-2j * np.pi * np.arange(n2)[:, None] * np.arange(n2)[None, :] / n2)
    tb = np.block([[fb.real, -fb.imag], [fb.imag, fb.real]])
    taf = jnp.concatenate([ta[:, 0:FILT_K1, 0:n1 // 2], ta[:, n1:n1 + FILT_K1, 0:n1 // 2]], axis=1)
    fm = np.exp(-2j * np.pi * (np.arange(n2)[:, None] + 1) * np.arange(n2)[None, :] / n2)
    tb_mirror = np.block([[fm.real, fm.imag], [fm.imag, -fm.real]])
    return (ta.astype(BF16), jnp.swapaxes(ta, 1, 2).astype(BF16), jnp.asarray(tb, BF16), jnp.asarray(tb.T, BF16),
            taf.astype(BF16), jnp.asarray(np.stack([tb, tb_mirror]), BF16))


FFT_NB = BF16_SUBLANES


def _fft_a_kernel(x_ref, t_ref, o_ref, r_s):
    planes = x_ref.shape[1]
    xt = [jnp.swapaxes(x_ref[0, c], 0, 1) for c in range(planes)]
    for jj in range(FFT_NB):
        x = jnp.concatenate([xt[c][jj] for c in range(planes)], axis=0).astype(BF16)
        r_s[jj] = jnp.dot(t_ref[jj], x, preferred_element_type=F32)
    o_ref[0] = jnp.swapaxes(r_s[...], 0, 1).astype(o_ref.dtype)


def _fft_a(x5, ta, *, n2, d, seg):
    s, p, half, _, _ = x5.shape
    kdim = p * half
    tn = min(256, d)
    sb, db = (seg * d) // tn, d // tn
    return pl.pallas_call(
        _fft_a_kernel,
        grid=(s, n2 // FFT_NB, db),
        in_specs=[pl.BlockSpec((1, p, half, FFT_NB, tn), lambda si, m, j: (si, 0, 0, m, sb + j)),
                  pl.BlockSpec((FFT_NB, 2 * FFT_N1, kdim), lambda si, m, j: (m, 0, 0))],
        out_specs=pl.BlockSpec((1, 2 * FFT_N1, FFT_NB, tn), lambda si, m, j: (si, 0, m, j)),
        out_shape=jax.ShapeDtypeStruct((s, 2 * FFT_N1, n2, d), BF16),
        scratch_shapes=[pltpu.VMEM((FFT_NB, 2 * FFT_N1, tn), F32)],
        compiler_params=_params("parallel", "parallel", "parallel"),
        name="dft_stage_a",
    )(x5, ta)


def _fft_b_kernel(a_ref, af_ref, ab_ref, sc_ref, tb_ref, tbf_ref, tbt_ref, o_ref, *, n2):
    z = jnp.dot(tb_ref[...], a_ref[...].reshape(2 * n2, -1), preferred_element_type=F32)
    tbf = tbf_ref[0]
    hf = jnp.dot(tbf, af_ref[0].reshape(2 * n2, -1), preferred_element_type=F32)
    hb = jnp.dot(tbf, ab_ref[0].reshape(2 * n2, -1), preferred_element_type=F32)
    sc = sc_ref[0]
    gr = (hf[0:n2] + hb[0:n2]) * sc
    gi = (hf[n2:2 * n2] - hb[n2:2 * n2]) * sc
    zr, zi = z[0:n2], z[n2:2 * n2]
    prod = jnp.concatenate([zr * gr - zi * gi, zr * gi + zi * gr], axis=0).astype(BF16)
    y = jnp.dot(tbt_ref[...], prod, preferred_element_type=F32)
    o_ref[...] = y.reshape(o_ref.shape).astype(o_ref.dtype)


def _fft_b(a, af, scale, tb, tbf, tbt, order, *, n2, d):
    tn = min(2048, d)
    blk = (2, 1, n2, tn)
    fblk = (1, 2, 1, n2, tn)
    mirrored = lambda k: k > FFT_N1 // 2
    fk = lambda k: jnp.where(mirrored(k), FFT_N1 - k, k)
    return pl.pallas_call(
        functools.partial(_fft_b_kernel, n2=n2),
        grid=(FFT_N1, d // tn),
        in_specs=[pl.BlockSpec(blk, lambda k, j: (0, k, 0, j)),
                  pl.BlockSpec(fblk, lambda k, j: (2 * order, 0, fk(k), 0, j)),
                  pl.BlockSpec(fblk, lambda k, j: (2 * order + 1, 0, fk(k), 0, j)),
                  pl.BlockSpec((1, 1, tn), lambda k, j: (order, 0, j)),
                  pl.BlockSpec((2 * n2, 2 * n2), lambda k, j: (0, 0)),
                  pl.BlockSpec((1, 2 * n2, 2 * n2), lambda k, j: (jnp.where(mirrored(k), 1, 0), 0, 0)),
                  pl.BlockSpec((2 * n2, 2 * n2), lambda k, j: (0, 0))],
        out_specs=pl.BlockSpec(blk, lambda k, j: (0, k, 0, j)),
        out_shape=jax.ShapeDtypeStruct((2, FFT_N1, n2, d), BF16),
        compiler_params=_params("parallel", "parallel"),
        name="dft_stage_b_spectrum_product",
    )(a, af, af, scale, tb, tbf, tbt)


def _fft_ainv_kernel(y_ref, t_ref, u_ref, gate_ref, d_ref, o_ref, r_s):
    half = o_ref.shape[1]
    yt = jnp.swapaxes(y_ref[...].astype(F32), 0, 1).astype(BF16)
    for c in range(2):
        r_s[c] = jnp.swapaxes(u_ref[c], 0, 1) * d_ref[...]
    for jj in range(FFT_NB):
        x = jnp.dot(t_ref[jj], yt[jj], preferred_element_type=F32)
        for c in range(2):
            r_s[c, jj] += x[c * half:(c + 1) * half]
    for c in range(2):
        o_ref[c] = gate_ref[c] * jnp.swapaxes(r_s[c], 0, 1)


def _fft_ainv(y3, tat, u4, gate4, dskip, *, n2, d, useg, gseg):
    b, half, _, _ = u4.shape
    tn = min(128, d)
    db = d // tn
    usb, gsb = (useg * d) // tn, (gseg * d) // tn
    return pl.pallas_call(
        _fft_ainv_kernel,
        grid=(n2 // FFT_NB, db),
        in_specs=[pl.BlockSpec((2 * FFT_N1, FFT_NB, tn), lambda m, j: (0, m, j)),
                  pl.BlockSpec((FFT_NB, FFT_N1, 2 * FFT_N1), lambda m, j: (m, 0, 0)),
                  pl.BlockSpec((b, half, FFT_NB, tn), lambda m, j: (0, 0, m, usb + j)),
                  pl.BlockSpec((b, half, FFT_NB, tn), lambda m, j: (0, 0, m, gsb + j)),
                  pl.BlockSpec((1, tn), lambda m, j: (0, j))],
        out_specs=pl.BlockSpec((b, half, FFT_NB, tn), lambda m, j: (0, 0, m, j)),
        out_shape=jax.ShapeDtypeStruct((b, half, n2, d), F32),
        scratch_shapes=[pltpu.VMEM((b, FFT_NB, half, tn), F32)],
        compiler_params=_params("parallel", "parallel"),
        name="dft_stage_a_inverse_gate",
    )(y3, tat, u4, gate4, dskip)


def _hyena_mixer(h, hy, l, d):
    (w_in, b_in, sconv_w, sconv_b, f_w0, f_b0, f_w1, f_b1, f_w2, f_b2, f_freq, f_w3, d_skip, w_out) = hy
    b = h.shape[0]
    assert b == 2, "the two batch rows ride the real / imaginary planes of one complex DFT"
    n1 = FFT_N1
    half = n1 // 2
    n2 = (2 * l) // n1
    assert half * n2 == l
    z3 = _mm_conv(_hy_in_kernel, h, [w_in.astype(BF16)],
                  [b_in.reshape(1, -1), sconv_w, sconv_b.reshape(1, -1)], 3 * d, F32, "hyena_in_proj_conv", nz=2)
    ta, tat, tb, tbt, taf, tbf = _dft_tables(l)
    nseg = 2 * HY_ORDER
    af, fsum = _hyena_filter_spectra_a(l, d, taf, f_w0, f_b0, f_w1, f_b1, f_w2, f_b2, f_freq, f_w3)
    af = af.reshape(nseg, 2, FILT_K1, n2, d)
    scale = 1.0 / ((fsum[0::2] + fsum[1::2]) * (2 * l))
    z4 = z3.reshape(b, half, n2, 3 * d)
    y, yseg = z4, 0
    for o in range(HY_ORDER):
        a = _fft_a(y[None], ta, n2=n2, d=d, seg=yseg)
        yb = _fft_b(a.reshape(2, n1, n2, d), af, scale, tb, tbf, tbt, o, n2=n2, d=d)
        y = _fft_ainv(yb.reshape(2 * n1, n2, d), tat, y, z4, d_skip[o].reshape(1, d),
                      n2=n2, d=d, useg=yseg, gseg=o + 1)
        yseg = 0
    return y.reshape(b, l, d), w_out.astype(BF16)


def _rope_tables(l):
    t = jnp.arange(l)
    row = (t // GRID_W).astype(F32)[:, None]
    col = (t % GRID_W).astype(F32)[:, None]
    width = 2 * HEAD_DIM
    cos, sin, avg, perm = [], [], [], []
    for dim in (A_HALF, HEAD_DIM):
        nf = dim // 4
        inv = jnp.asarray(ROPE_BASE ** (-np.arange(nf) / nf), F32)[None, :]
        ar, ac = row * inv, col * inv
        c = jnp.concatenate([jnp.cos(ar), jnp.cos(ar), jnp.cos(ac), jnp.cos(ac)], axis=1)
        s = jnp.concatenate([-jnp.sin(ar), jnp.sin(ar), -jnp.sin(ac), jnp.sin(ac)], axis=1)
        reps = width // dim
        cos.append(jnp.tile(c, (1, reps)))
        sin.append(jnp.tile(s, (1, reps)))
        lane = np.arange(width)
        avg.append((lane[:, None] // dim == lane[None, :] // dim) / dim)
        partner = np.where(lane % (2 * nf) < nf, lane + nf, lane - nf)
        perm.append((lane[:, None] == partner[None, :]).astype(np.float64))
    return (jnp.stack(cos), jnp.stack(sin), jnp.asarray(np.stack(avg), BF16), jnp.asarray(np.stack(perm), BF16))


def _attn_mixer(h, hc, lam_init, w_in, w_out, a_q_g, a_k_g, lq1, lk1, lq2, lk2, a_sub_g, b_q_g, b_k_g, b_sink):
    b, l, d = h.shape
    c = hc.shape[1]
    a_heads = d // (2 * HEAD_DIM)
    b_heads = d // (2 * HEAD_DIM)
    q_cols = (a_heads + b_heads) * HEAD_DIM
    in_cols = w_in.shape[1]
    lam = (jnp.exp(jnp.sum(lq1 * lk1)) - jnp.exp(jnp.sum(lq2 * lk2)) + lam_init).reshape(1, 1)
    sink = b_sink.reshape(B_KV_HEADS, b_heads // B_KV_HEADS)

    ones = lambda n: jnp.ones((n,), F32)
    tile = lambda v, n: jnp.tile(v, n)
    gain = jnp.concatenate([tile(a_q_g, 2 * a_heads), tile(b_q_g, b_heads), tile(a_k_g, 2 * a_heads),
                            ones(a_heads * HEAD_DIM), tile(b_k_g, B_KV_HEADS),
                            ones(B_KV_HEADS * HEAD_DIM)]).reshape(1, in_cols)
    scale = jnp.concatenate([jnp.full((a_heads * HEAD_DIM,), A_HALF ** -0.5 * math.log2(math.e), F32),
                             jnp.full((b_heads * HEAD_DIM,), HEAD_DIM ** -0.5, F32),
                             ones(in_cols - q_cols)]).reshape(1, in_cols)
    e = [0, a_heads // 2, (a_heads + b_heads) // 2, (2 * a_heads + b_heads) // 2,
         (3 * a_heads + b_heads) // 2, (3 * a_heads + b_heads) // 2 + 1, (3 * a_heads + b_heads) // 2 + 2]
    plain = ((e[3], e[4]), (e[5], e[6]))
    cos_tab, sin_tab, avg, perm = _rope_tables(l)

    def type_of(j0):
        def f(j):
            jj = j + j0
            is_a = (jj < e[1]) | ((jj >= e[2]) & (jj < e[3]))
            return jnp.where(is_a, 0, 1)
        return f

    w_bf = _col_tiles(w_in.astype(BF16), 2 * HEAD_DIM)
    p = _inproj(h.reshape(b * l, d), w_bf, gain, scale, cos_tab, sin_tab, avg, perm, plain=plain,
                type_of_block=type_of(0), col0=0, ncols=in_cols, l=l, rope=True).reshape(b, l, in_cols)
    kv0 = e[2]
    plain_c = tuple((lo - kv0, hi - kv0) for lo, hi in plain)
    pc = _inproj(hc.reshape(b * c, d), w_bf, gain, scale, cos_tab, sin_tab, avg, perm, plain=plain_c,
                 type_of_block=type_of(kv0), col0=kv0, ncols=in_cols - q_cols, l=c, rope=False
                 ).reshape(b, c, in_cols - q_cols)
    score_bound = (A_HALF ** 0.5 * math.log2(math.e)) * jnp.max(jnp.abs(a_q_g)) * jnp.max(jnp.abs(a_k_g))
    oa = _diffattn(lam, p, pc, a_sub_g, score_bound, heads=a_heads, l=l, c=c, out_scale=1.0 - lam_init)
    ob = _winattn(sink, p, pc, a_heads=a_heads, b_heads=b_heads, l=l, c=c)
    w_out_bf = w_out.astype(BF16)
    na = a_heads * HEAD_DIM
    return [oa, ob], [w_out_bf[:na], w_out_bf[na:]]


def kernel(x, c, ctx, c_ctx, ada_w, ada_b, norm1_g, norm2_g, attn_w_in, attn_w_out, a_q_g, a_k_g, a_lam_q1, a_lam_k1, a_lam_q2, a_lam_k2, a_sub_g, b_q_g, b_k_g, b_sink, hy_w_in, hy_b_in, hy_sconv_w, hy_sconv_b, hy_f_w0, hy_f_b0, hy_f_w1, hy_f_b1, hy_f_w2, hy_f_b2, hy_f_freq, hy_f_w3, hy_d, hy_w_out, ffn_w_gate, ffn_w_val, ffn_conv_w, ffn_conv_b, ffn_w_down):
    b, l, d = x.shape
    depth = ada_w.shape[0]
    cc = jnp.concatenate([c, c_ctx[None, :], jnp.zeros((8 - b - 1, d), F32)], axis=0)
    m = _ada(cc, ada_w, ada_b)
    xs = x
    for layer in range(depth):
        i = layer // 2
        lat = [m[layer, :b, k * d:(k + 1) * d].reshape(b, 1, d) for k in range(6)]
        sh1, sc1, g1, sh2, sc2, g2 = lat
        h = _normmod(xs, norm1_g[layer], sc1, sh1)
        if layer % 2 == 0:
            mc = [jnp.broadcast_to(m[layer, b, k * d:(k + 1) * d].reshape(1, 1, d), (b, 1, d)) for k in range(2)]
            hc = _normmod(ctx, norm1_g[layer], mc[1], mc[0])
            parts, ws = _attn_mixer(h, hc, 0.8 - 0.6 * math.exp(-0.3 * layer), attn_w_in[i], attn_w_out[i],
                                    a_q_g[i], a_k_g[i], a_lam_q1[i], a_lam_k1[i], a_lam_q2[i], a_lam_k2[i],
                                    a_sub_g[i], b_q_g[i], b_k_g[i], b_sink[i])
        else:
            hy = (hy_w_in[i], hy_b_in[i], hy_sconv_w[i], hy_sconv_b[i], hy_f_w0[i], hy_f_b0[i], hy_f_w1[i],
                  hy_f_b1[i], hy_f_w2[i], hy_f_b2[i], hy_f_freq[i], hy_f_w3[i], hy_d[i], hy_w_out[i])
            y, w_o = _hyena_mixer(h, hy, l, d)
            parts, ws = [y], [w_o]
        xs = _mm_resid(parts, ws, xs, g1)
        h2 = _normmod(xs, norm2_g[layer], sc2, sh2)
        d_ff = ffn_w_gate.shape[2]
        hid = _mm_conv(_ffn_up_kernel, h2, [ffn_w_gate[layer].astype(BF16), ffn_w_val[layer].astype(BF16)],
                       [ffn_conv_w[layer], ffn_conv_b[layer].reshape(1, -1)], d_ff, BF16, "ffn_up_conv_glu", nz=2)
        xs = _mm_resid([hid], [ffn_w_down[layer].astype(BF16)], xs, g2)
    return xs
```

```python
import functools
import math

import numpy as np
import jax
import jax.numpy as jnp
from jax import lax
from jax.experimental import pallas as pl
from jax.experimental.pallas import tpu as pltpu

F32 = jnp.float32
BF16 = jnp.bfloat16
HIGHEST = lax.Precision.HIGHEST

HEAD_DIM = 128
A_HALF = HEAD_DIM // 2
B_KV_HEADS = 2
GRID_W = 64
WINDOW = 128
ROPE_BASE = 10000.0
EPS = 1e-6
NEG = -1e30
HY_ORDER = 2
HY_DECAY_TARGET = 1e-2
HY_MAX_DECAY_PCT = 0.3
HY_MIN_DECAY_PCT = 1.5

BF16_SUBLANES = 16
FFT_N1 = 256
VMEM_LIMIT_BYTES = 56 * 1024 * 1024


def _params(*sem):
    return pltpu.CompilerParams(dimension_semantics=sem, vmem_limit_bytes=VMEM_LIMIT_BYTES)


def _col_tiles(w, tn):
    k, n = w.shape
    return jnp.transpose(w.reshape(k, n // tn, tn), (1, 0, 2))


def _ada_kernel(c_ref, w_ref, b_ref, o_ref):
    c = c_ref[...]
    s = c * (1.0 / (1.0 + jnp.exp(-c)))
    o_ref[0] = jnp.dot(s, w_ref[0], preferred_element_type=F32, precision=HIGHEST) + b_ref[0]


def _ada(cc, ada_w, ada_b):
    depth, d, n = ada_w.shape
    tn = 1024
    return pl.pallas_call(
        _ada_kernel,
        grid=(depth, n // tn),
        in_specs=[pl.BlockSpec((8, d), lambda l, j: (0, 0)),
                  pl.BlockSpec((1, d, tn), lambda l, j: (l, 0, j)),
                  pl.BlockSpec((1, 1, tn), lambda l, j: (l, 0, j))],
        out_specs=pl.BlockSpec((1, 8, tn), lambda l, j: (l, 0, j)),
        out_shape=jax.ShapeDtypeStruct((depth, 8, n), F32),
        compiler_params=_params("parallel", "parallel"),
        name="ada_modulation",
    )(cc, ada_w, ada_b.reshape(depth, 1, n))


def _normmod_kernel(x_ref, g_ref, sc_ref, sh_ref, o_ref):
    x = x_ref[0]
    ms = jnp.mean(x * x, axis=-1, keepdims=True)
    y = x * lax.rsqrt(ms + EPS) * g_ref[...]
    o_ref[0] = (y * (1.0 + sc_ref[0]) + sh_ref[0]).astype(o_ref.dtype)


def _normmod(x, g, sc, sh):
    b, l, d = x.shape
    tm = min(512, l)
    return pl.pallas_call(
        _normmod_kernel,
        grid=(b, l // tm),
        in_specs=[pl.BlockSpec((1, tm, d), lambda bi, i: (bi, i, 0)),
                  pl.BlockSpec((1, d), lambda bi, i: (0, 0)),
                  pl.BlockSpec((1, 1, d), lambda bi, i: (bi, 0, 0)),
                  pl.BlockSpec((1, 1, d), lambda bi, i: (bi, 0, 0))],
        out_specs=pl.BlockSpec((1, tm, d), lambda bi, i: (bi, i, 0)),
        out_shape=jax.ShapeDtypeStruct((b, l, d), BF16),
        compiler_params=_params("parallel", "parallel"),
        name="rmsnorm_modulate",
    )(x, g.reshape(1, d), sc, sh)


def _dot_split(x, m):
    hi = x.astype(BF16)
    lo = (x - hi.astype(F32)).astype(BF16)
    return jnp.dot(hi, m, preferred_element_type=F32) + jnp.dot(lo, m, preferred_element_type=F32)


def _inproj_kernel(h_ref, w_ref, gain_ref, scale_ref, cos_ref, sin_ref, avg_ref, perm_ref, o_ref, *, plain, rope):
    j = pl.program_id(1)
    z = jnp.dot(h_ref[...], w_ref[0], preferred_element_type=F32)
    is_plain = functools.reduce(jnp.logical_or, [(j >= lo) & (j < hi) for lo, hi in plain])

    @pl.when(is_plain)
    def _():
        o_ref[...] = z.astype(o_ref.dtype)

    @pl.when(jnp.logical_not(is_plain))
    def _():
        rows = min(256, z.shape[0])
        for r0 in range(0, z.shape[0], rows):
            rs = slice(r0, r0 + rows)
            zc = z[rs]
            ms = _dot_split(zc * zc, avg_ref[0])
            y = zc * lax.rsqrt(ms + EPS) * gain_ref[...]
            if rope:
                y = y * cos_ref[0, rs, :] + _dot_split(y, perm_ref[0]) * sin_ref[0, rs, :]
            o_ref[rs, :] = (y * scale_ref[...]).astype(o_ref.dtype)


def _inproj(h2d, w, gain, scale, cos_tab, sin_tab, avg, perm, *, plain, type_of_block, col0, ncols, l, rope):
    m, d = h2d.shape
    tn = 2 * HEAD_DIM
    tm = min(1024, l)
    pos_blocks = l // tm
    kern = functools.partial(_inproj_kernel, plain=plain, rope=rope)
    by_type = lambda i, j: (type_of_block(j), 0, 0)
    return pl.pallas_call(
        kern,
        grid=(m // tm, ncols // tn),
        in_specs=[pl.BlockSpec((tm, d), lambda i, j: (i, 0)),
                  pl.BlockSpec((1, d, tn), lambda i, j: (j + col0, 0, 0)),
                  pl.BlockSpec((1, tn), lambda i, j: (0, j + col0)),
                  pl.BlockSpec((1, tn), lambda i, j: (0, j + col0)),
                  pl.BlockSpec((1, tm, tn), lambda i, j: (type_of_block(j), i % pos_blocks, 0)),
                  pl.BlockSpec((1, tm, tn), lambda i, j: (type_of_block(j), i % pos_blocks, 0)),
                  pl.BlockSpec((1, tn, tn), by_type),
                  pl.BlockSpec((1, tn, tn), by_type)],
        out_specs=pl.BlockSpec((tm, tn), lambda i, j: (i, j)),
        out_shape=jax.ShapeDtypeStruct((m, ncols), BF16),
        compiler_params=_params("parallel", "arbitrary"),
        name="attn_in_projection",
    )(h2d, w, gain, scale, cos_tab, sin_tab, avg, perm)


def _diffattn_kernel(lam_ref, q_ref, kc_ref, vc_ref, k_ref, v_ref, g_ref, o_ref,
                     q2_s, m_s, l_s, acc_s, *, tq, nkv, out_scale):
    kv = pl.program_id(3)

    def process(k, v):
        s = lax.dot_general(q2_s[...], k, (((1,), (1,)), ((), ())), preferred_element_type=F32)
        m_prev = m_s[...]
        m_new = jnp.maximum(m_prev, jnp.max(s, axis=-1, keepdims=True))
        alpha = jnp.exp2(m_prev - m_new)
        p = jnp.exp2(s - m_new)
        l_s[...] = alpha * l_s[...] + jnp.sum(p, axis=-1, keepdims=True)
        acc_s[...] = alpha * acc_s[...] + jnp.dot(p.astype(BF16), v, preferred_element_type=F32)
        m_s[...] = m_new

    @pl.when(kv == 0)
    def _():
        q = q_ref[0]
        lane = lax.broadcasted_iota(jnp.int32, q.shape, 1)
        zero = jnp.zeros_like(q)
        q2_s[0:tq, :] = jnp.where(lane < A_HALF, q, zero)
        q2_s[tq:2 * tq, :] = jnp.where(lane >= A_HALF, q, zero)
        m_s[...] = jnp.full(m_s.shape, -jnp.inf, F32)
        l_s[...] = jnp.zeros(l_s.shape, F32)
        acc_s[...] = jnp.zeros(acc_s.shape, F32)
        process(kc_ref[0], vc_ref[0])

    process(k_ref[0], v_ref[0])

    @pl.when(kv == nkv - 1)
    def _():
        o = acc_s[...] / l_s[...]
        d = o[0:tq] - lam_ref[0, 0] * o[tq:2 * tq]
        ms = jnp.mean(d * d, axis=-1, keepdims=True)
        o_ref[0] = (d * lax.rsqrt(ms + EPS) * g_ref[...] * out_scale).astype(o_ref.dtype)


def _diffattn_fast_kernel(lam_ref, q_ref, kc_ref, vct_ref, k_ref, vt_ref, g_ref, o_ref,
                          q2_s, l_s, acc_s, p_s, *, tq, tk, kc, nkv, out_scale):
    kv = pl.program_id(3)
    dn = (((1,), (1,)), ((), ()))

    def weights(k):
        n = k.shape[0]
        p = jnp.exp2(lax.dot_general(k, q2_s[...], dn, preferred_element_type=F32))
        return p.astype(BF16), jnp.sum(p.reshape(n // 8, 8, 2 * tq), axis=0)

    @pl.when(kv == 0)
    def _():
        q = q_ref[0]
        lane = lax.broadcasted_iota(jnp.int32, q.shape, 1)
        zero = jnp.zeros_like(q)
        q2_s[0:tq, :] = jnp.where(lane < A_HALF, q, zero)
        q2_s[tq:2 * tq, :] = jnp.where(lane >= A_HALF, q, zero)
        p, ls = weights(kc_ref[0])
        l_s[...] = ls
        acc_s[...] = jnp.dot(vct_ref[0, 0], p, preferred_element_type=F32)

    ls = l_s[...]
    for ci in range(tk // kc):
        p, lc = weights(k_ref[0, ci * kc:(ci + 1) * kc, :])
        p_s[ci * kc:(ci + 1) * kc, :] = p
        ls = ls + lc
    l_s[...] = ls
    acc_s[...] += jnp.dot(vt_ref[0, 0], p_s[...], preferred_element_type=F32)

    @pl.when(kv == nkv - 1)
    def _():
        l = jnp.sum(l_s[...], axis=0, keepdims=True)
        o = acc_s[...] / l
        d = o[:, 0:tq] - lam_ref[0, 0] * o[:, tq:2 * tq]
        ms = jnp.mean(d * d, axis=0, keepdims=True)
        y = d * lax.rsqrt(ms + EPS) * g_ref[...] * out_scale
        o_ref[0] = y.T.astype(o_ref.dtype)


def _diffattn_fast(lam, p, pc, sub_g, *, heads, l, c, out_scale):
    b = p.shape[0]
    tq = min(1024, l)
    tk = min(4096, l)
    kc = min(512, tk)
    nkv = l // tk
    k_blk = 2 * heads
    hd = heads * HEAD_DIM
    vt = jnp.transpose(p[:, :, 3 * hd:4 * hd].reshape(b, l, heads, HEAD_DIM), (0, 2, 3, 1))
    vct = jnp.transpose(pc[:, :, hd:2 * hd].reshape(b, c, heads, HEAD_DIM), (0, 2, 3, 1))
    kern = functools.partial(_diffattn_fast_kernel, tq=tq, tk=tk, kc=kc, nkv=nkv, out_scale=out_scale)
    return pl.pallas_call(
        kern,
        grid=(b, heads, l // tq, nkv),
        in_specs=[pl.BlockSpec(memory_space=pltpu.SMEM),
                  pl.BlockSpec((1, tq, HEAD_DIM), lambda bi, h, i, kv: (bi, i, h)),
                  pl.BlockSpec((1, c, HEAD_DIM), lambda bi, h, i, kv: (bi, 0, h)),
                  pl.BlockSpec((1, 1, HEAD_DIM, c), lambda bi, h, i, kv: (bi, h, 0, 0)),
                  pl.BlockSpec((1, tk, HEAD_DIM), lambda bi, h, i, kv: (bi, kv, k_blk + h)),
                  pl.BlockSpec((1, 1, HEAD_DIM, tk), lambda bi, h, i, kv: (bi, h, 0, kv)),
                  pl.BlockSpec((HEAD_DIM, 1), lambda bi, h, i, kv: (0, 0))],
        out_specs=pl.BlockSpec((1, tq, HEAD_DIM), lambda bi, h, i, kv: (bi, i, h)),
        out_shape=jax.ShapeDtypeStruct((b, l, hd), BF16),
        scratch_shapes=[pltpu.VMEM((2 * tq, HEAD_DIM), BF16),
                        pltpu.VMEM((8, 2 * tq), F32),
                        pltpu.VMEM((HEAD_DIM, 2 * tq), F32),
                        pltpu.VMEM((tk, 2 * tq), BF16)],
        compiler_params=_params("parallel", "parallel", "parallel", "arbitrary"),
        name="diff_attention_bounded",
    )(lam, p, pc, vct, p, vt, sub_g.reshape(HEAD_DIM, 1))


SCORE_BOUND_LOG2 = 60.0


def _diffattn(lam, p, pc, sub_g, score_bound, *, heads, l, c, out_scale):
    kw = dict(heads=heads, l=l, c=c, out_scale=out_scale)
    return lax.cond(score_bound < SCORE_BOUND_LOG2,
                    lambda *a: _diffattn_fast(*a, **kw), lambda *a: _diffattn_safe(*a, **kw),
                    lam, p, pc, sub_g)


def _diffattn_safe(lam, p, pc, sub_g, *, heads, l, c, out_scale):
    b = p.shape[0]
    tq = min(256, l)
    tk = min(512, l)
    nkv = l // tk
    q_blk = 0
    k_blk = 2 * heads
    v_blk = 3 * heads
    kern = functools.partial(_diffattn_kernel, tq=tq, nkv=nkv, out_scale=out_scale)
    return pl.pallas_call(
        kern,
        grid=(b, heads, l // tq, nkv),
        in_specs=[pl.BlockSpec(memory_space=pltpu.SMEM),
                  pl.BlockSpec((1, tq, HEAD_DIM), lambda bi, h, i, kv: (bi, i, q_blk + h)),
                  pl.BlockSpec((1, c, HEAD_DIM), lambda bi, h, i, kv: (bi, 0, h)),
                  pl.BlockSpec((1, c, HEAD_DIM), lambda bi, h, i, kv: (bi, 0, heads + h)),
                  pl.BlockSpec((1, tk, HEAD_DIM), lambda bi, h, i, kv: (bi, kv, k_blk + h)),
                  pl.BlockSpec((1, tk, HEAD_DIM), lambda bi, h, i, kv: (bi, kv, v_blk + h)),
                  pl.BlockSpec((1, HEAD_DIM), lambda bi, h, i, kv: (0, 0))],
        out_specs=pl.BlockSpec((1, tq, HEAD_DIM), lambda bi, h, i, kv: (bi, i, h)),
        out_shape=jax.ShapeDtypeStruct((b, l, heads * HEAD_DIM), BF16),
        scratch_shapes=[pltpu.VMEM((2 * tq, HEAD_DIM), BF16),
                        pltpu.VMEM((2 * tq, 1), F32),
                        pltpu.VMEM((2 * tq, 1), F32),
                        pltpu.VMEM((2 * tq, HEAD_DIM), F32)],
        compiler_params=_params("parallel", "parallel", "parallel", "arbitrary"),
        name="diff_attention",
    )(lam, p, pc, pc, p, p, sub_g.reshape(1, HEAD_DIM))


def _winattn_kernel(sink_ref, q_ref, kp_ref, km_ref, kn_ref, vp_ref, vm_ref, vn_ref, kc_ref, vc_ref,
                    o_ref, *, tq, l, group):
    g = pl.program_id(1)
    i = pl.program_id(2)
    kband = jnp.concatenate([kp_ref[0], km_ref[0], kn_ref[0]], axis=0)
    vband = jnp.concatenate([vp_ref[0], vm_ref[0], vn_ref[0]], axis=0)
    nk = tq + 2 * WINDOW
    qpos = i * tq + lax.broadcasted_iota(jnp.int32, (tq, nk), 0)
    kpos = i * tq - WINDOW + lax.broadcasted_iota(jnp.int32, (tq, nk), 1)
    valid = (jnp.abs(kpos - qpos) <= WINDOW) & (kpos >= 0) & (kpos < l)
    kc = kc_ref[0]
    vc = vc_ref[0]
    dn = (((1,), (1,)), ((), ()))
    for r in range(group):
        sl = slice(r * HEAD_DIM, (r + 1) * HEAD_DIM)
        q = q_ref[0, :, sl]
        s_lat = jnp.where(valid, lax.dot_general(q, kband, dn, preferred_element_type=F32), NEG)
        s_ctx = lax.dot_general(q, kc, dn, preferred_element_type=F32)
        sk = sink_ref[g, r]
        m = jnp.maximum(jnp.maximum(jnp.max(s_lat, axis=-1, keepdims=True),
                                    jnp.max(s_ctx, axis=-1, keepdims=True)), sk)
        p_lat = jnp.exp(s_lat - m)
        p_ctx = jnp.exp(s_ctx - m)
        denom = (jnp.sum(p_lat, axis=-1, keepdims=True) + jnp.sum(p_ctx, axis=-1, keepdims=True)
                 + jnp.exp(sk - m))
        o = (jnp.dot(p_lat.astype(BF16), vband, preferred_element_type=F32)
             + jnp.dot(p_ctx.astype(BF16), vc, preferred_element_type=F32))
        o_ref[0, :, sl] = (o / denom).astype(o_ref.dtype)


def _winattn(sink, p, pc, *, a_heads, b_heads, l, c):
    b = p.shape[0]
    group = b_heads // B_KV_HEADS
    tq = min(256, l)
    wpb = tq // WINDOW
    nwb = l // WINDOW
    gw = group * HEAD_DIM
    q_blk = a_heads * HEAD_DIM // gw
    k_blk = (a_heads + b_heads) + 2 * a_heads
    v_blk = k_blk + B_KV_HEADS
    kc_blk = 2 * a_heads
    vc_blk = kc_blk + B_KV_HEADS
    kern = functools.partial(_winattn_kernel, tq=tq, l=l, group=group)

    def prev_map(col):
        return lambda bi, g, i: (bi, jnp.maximum(i * wpb - 1, 0), col + g)

    def main_map(col):
        return lambda bi, g, i: (bi, i, col + g)

    def next_map(col):
        return lambda bi, g, i: (bi, jnp.minimum((i + 1) * wpb, nwb - 1), col + g)

    return pl.pallas_call(
        kern,
        grid=(b, B_KV_HEADS, l // tq),
        in_specs=[pl.BlockSpec(memory_space=pltpu.SMEM),
                  pl.BlockSpec((1, tq, gw), lambda bi, g, i: (bi, i, q_blk + g)),
                  pl.BlockSpec((1, WINDOW, HEAD_DIM), prev_map(k_blk)),
                  pl.BlockSpec((1, tq, HEAD_DIM), main_map(k_blk)),
                  pl.BlockSpec((1, WINDOW, HEAD_DIM), next_map(k_blk)),
                  pl.BlockSpec((1, WINDOW, HEAD_DIM), prev_map(v_blk)),
                  pl.BlockSpec((1, tq, HEAD_DIM), main_map(v_blk)),
                  pl.BlockSpec((1, WINDOW, HEAD_DIM), next_map(v_blk)),
                  pl.BlockSpec((1, c, HEAD_DIM), lambda bi, g, i: (bi, 0, kc_blk + g)),
                  pl.BlockSpec((1, c, HEAD_DIM), lambda bi, g, i: (bi, 0, vc_blk + g))],
        out_specs=pl.BlockSpec((1, tq, gw), lambda bi, g, i: (bi, i, g)),
        out_shape=jax.ShapeDtypeStruct((b, l, b_heads * HEAD_DIM), BF16),
        compiler_params=_params("parallel", "parallel", "parallel"),
        name="window_attention",
    )(sink, p, p, p, p, p, p, p, pc, pc)


def _mm_resid_kernel(*refs, nparts):
    a_refs = refs[0:nparts]
    w_refs = refs[nparts:2 * nparts]
    res_ref, gate_ref, o_ref = refs[2 * nparts:]
    y = jnp.dot(a_refs[0][0].astype(BF16), w_refs[0][0], preferred_element_type=F32)
    for a_ref, w_ref in zip(a_refs[1:], w_refs[1:]):
        y = y + jnp.dot(a_ref[0].astype(BF16), w_ref[0], preferred_element_type=F32)
    o_ref[0] = res_ref[0] + gate_ref[0] * y


def _mm_resid(parts, ws, res, gate):
    b, l, n = res.shape
    tm = min(1024, l)
    tn = min(512, n)
    nparts = len(parts)
    ws = [_col_tiles(w, tn) for w in ws]
    in_specs = ([pl.BlockSpec((1, tm, a.shape[2]), lambda bi, i, j: (bi, i, 0)) for a in parts]
                + [pl.BlockSpec((1, w.shape[1], tn), lambda bi, i, j: (j, 0, 0)) for w in ws]
                + [pl.BlockSpec((1, tm, tn), lambda bi, i, j: (bi, i, j)),
                   pl.BlockSpec((1, 1, tn), lambda bi, i, j: (bi, 0, j))])
    return pl.pallas_call(
        functools.partial(_mm_resid_kernel, nparts=nparts),
        grid=(b, l // tm, n // tn),
        in_specs=in_specs,
        out_specs=pl.BlockSpec((1, tm, tn), lambda bi, i, j: (bi, i, j)),
        out_shape=jax.ShapeDtypeStruct((b, l, n), F32),
        compiler_params=_params("parallel", "parallel", "arbitrary"),
        name="out_projection_gated_residual",
    )(*parts, *ws, res, gate)


HALO = BF16_SUBLANES


def _fill_rows(hbuf, hp_ref, hm_ref, hn_ref, tm):
    hbuf[0:HALO, :] = hp_ref[0]
    hbuf[HALO:HALO + tm, :] = hm_ref[0]
    hbuf[HALO + tm:2 * HALO + tm, :] = hn_ref[0]


def _conv3(zbuf, cw_ref, cb_ref, tm, first, last):
    lo = slice(HALO - 1, HALO)
    hi = slice(HALO + tm, HALO + tm + 1)
    zbuf[lo, :] = jnp.where(first, 0.0, zbuf[lo, :])
    zbuf[hi, :] = jnp.where(last, 0.0, zbuf[hi, :])
    z = zbuf[...]
    rows = z.shape[0]
    g = (z * cw_ref[1:2, :] + pltpu.roll(z * cw_ref[0:1, :], 1, 0) + pltpu.roll(z * cw_ref[2:3, :], rows - 1, 0))
    return g[HALO:HALO + tm] + cb_ref[...]


def _gelu_tanh(x):
    return 0.5 * x * (1.0 + jnp.tanh(math.sqrt(2.0 / math.pi) * (x + 0.044715 * (x * x * x))))


def _ffn_up_kernel(hp_ref, hm_ref, hn_ref, wg_ref, wv_ref, cw_ref, cb_ref, o_ref, hbuf, zbuf, zbuf2, *, tm, nt):
    i = pl.program_id(1)
    j = pl.program_id(2)

    @pl.when(j == 0)
    def _():
        _fill_rows(hbuf, hp_ref, hm_ref, hn_ref, tm)

    hw = zbuf.shape[1]
    for hh, zb in enumerate((zbuf, zbuf2)):
        cs = slice(hh * hw, (hh + 1) * hw)
        zb[...] = jnp.dot(hbuf[...], wg_ref[0, :, cs], preferred_element_type=F32)
        g = _conv3(zb, cw_ref.at[:, cs], cb_ref.at[:, cs], tm, i == 0, i == nt - 1)
        v = jnp.dot(hbuf[HALO:HALO + tm, :], wv_ref[0, :, cs], preferred_element_type=F32)
        o_ref[0, :, cs] = (_gelu_tanh(g) * v).astype(o_ref.dtype)


def _hy_in_kernel(hp_ref, hm_ref, hn_ref, w_ref, b_ref, cw_ref, cb_ref, o_ref, hbuf, zbuf, zbuf2, *, tm, nt):
    i = pl.program_id(1)
    j = pl.program_id(2)

    @pl.when(j == 0)
    def _():
        _fill_rows(hbuf, hp_ref, hm_ref, hn_ref, tm)

    hw = zbuf.shape[1]
    for hh, zb in enumerate((zbuf, zbuf2)):
        cs = slice(hh * hw, (hh + 1) * hw)
        zb[...] = jnp.dot(hbuf[...], w_ref[0, :, cs], preferred_element_type=F32) + b_ref[:, cs]
        o_ref[0, :, cs] = _conv3(zb, cw_ref.at[:, cs], cb_ref.at[:, cs], tm, i == 0, i == nt - 1
                                 ).astype(o_ref.dtype)


def _mm_conv(kernel, h, ws, vecs, n, out_dtype, name, nz=1):
    b, l, k = h.shape
    tm = min(512, l)
    tn = min(512, n)
    nt = l // tm
    hpb = tm // HALO
    nhb = l // HALO
    ws = [_col_tiles(w, tn) for w in ws]
    in_specs = ([pl.BlockSpec((1, HALO, k), lambda bi, i, j: (bi, jnp.maximum(i * hpb - 1, 0), 0)),
                 pl.BlockSpec((1, tm, k), lambda bi, i, j: (bi, i, 0)),
                 pl.BlockSpec((1, HALO, k), lambda bi, i, j: (bi, jnp.minimum((i + 1) * hpb, nhb - 1), 0))]
                + [pl.BlockSpec((1, k, tn), lambda bi, i, j: (j, 0, 0)) for _ in ws]
                + [pl.BlockSpec((v.shape[0], tn), lambda bi, i, j: (0, j)) for v in vecs])
    return pl.pallas_call(
        functools.partial(kernel, tm=tm, nt=nt),
        grid=(b, nt, n // tn),
        in_specs=in_specs,
        out_specs=pl.BlockSpec((1, tm, tn), lambda bi, i, j: (bi, i, j)),
        out_shape=jax.ShapeDtypeStruct((b, l, n), out_dtype),
        scratch_shapes=[pltpu.VMEM((tm + 2 * HALO, k), BF16)] + [pltpu.VMEM((tm + 2 * HALO, tn // nz), F32)] * nz,
        compiler_params=_params("parallel", "parallel", "arbitrary"),
        name=name,
    )(h, h, h, *ws, *vecs)


FILT_NB = BF16_SUBLANES
FILT_K1 = FFT_N1 // 2 + 8


def _filter_dft_kernel(z_ref, w0_ref, b0_ref, w1_ref, b1_ref, w2_ref, b2_ref, fr_ref, w3_ref, dl_ref, t_ref,
                       o_ref, s_ref, a3_s, r_s):
    m, seg, j = pl.program_id(0), pl.program_id(1), pl.program_id(2)
    tn = o_ref.shape[3]
    fr = fr_ref[...]

    def layer(a, w_ref, b_ref):
        return jnp.sin(fr * (jnp.dot(a, w_ref[...], preferred_element_type=F32, precision=HIGHEST)
                             + b_ref[...]))

    @pl.when((seg == 0) & (j == 0))
    def _():
        for jj in range(FILT_NB):
            a = layer(layer(layer(z_ref[jj], w0_ref, b0_ref), w1_ref, b1_ref), w2_ref, b2_ref)
            a_hi = a.astype(BF16)
            a_lo = (a - a_hi.astype(F32)).astype(BF16)
            a3_s[jj] = jnp.concatenate([a_hi, a_hi, a_lo], axis=1)

    @pl.when((m == 0) & (seg == 0) & (j == 0))
    def _():
        s_ref[...] = jnp.zeros(s_ref.shape, F32)

    dl = dl_ref[...]
    backward = seg % 2 == 1
    abs_sum = jnp.zeros((1, tn), F32)
    for jj in range(FILT_NB):
        h = (jnp.dot(a3_s[jj], w3_ref[0], preferred_element_type=F32)
             * jnp.exp(-z_ref[jj][:, 0:1] * dl))
        if jj == 0:
            row = lax.broadcasted_iota(jnp.int32, h.shape, 0)
            h = jnp.where(backward & (m == 0) & (row == 0), 0.0, h)
        abs_sum = abs_sum + jnp.sum(jnp.abs(h), axis=0, keepdims=True)
        r_s[jj] = jnp.dot(t_ref[jj], h.astype(BF16), preferred_element_type=F32)
    o_ref[0] = jnp.swapaxes(r_s[...], 0, 1).astype(o_ref.dtype)
    col = pl.multiple_of(j * tn, tn)
    s_ref[seg, :, pl.ds(col, tn)] += abs_sum


def _hyena_filter_spectra_a(l, d, taf, f_w0, f_b0, f_w1, f_b1, f_w2, f_b2, f_freq, f_w3):
    emb, fw = f_w0.shape
    bands = (emb - 1) // 2
    half = FFT_N1 // 2
    n2 = l // half
    t = np.linspace(0.0, 1.0, l)[:, None]
    w = 2.0 * math.pi * np.arange(l)[:, None] / l
    f = np.linspace(1e-4, bands - 1, bands)[None, :]
    emb_pad = -(-emb // 8) * 8
    z = np.concatenate([t, np.cos(f * w), -np.sin(f * w), np.zeros((l, emb_pad - emb))], axis=-1)
    z = jnp.asarray(z.reshape(half, n2, emb_pad).transpose(1, 0, 2), F32)
    w0 = jnp.concatenate([f_w0, jnp.zeros((emb_pad - emb, fw), F32)], axis=0)
    max_decay = math.log(HY_DECAY_TARGET) / HY_MAX_DECAY_PCT
    min_decay = math.log(HY_DECAY_TARGET) / HY_MIN_DECAY_PCT
    dl = jnp.asarray(np.abs(np.linspace(min_decay, max_decay, d))[None, :], F32)
    nseg = 2 * HY_ORDER
    w3 = jnp.transpose(f_w3.reshape(fw, nseg, d), (1, 0, 2))
    w3_hi = w3.astype(BF16)
    w3_lo = (w3 - w3_hi.astype(F32)).astype(BF16)
    w3 = jnp.concatenate([w3_hi, w3_lo, w3_hi], axis=1)
    nb = min(FILT_NB, n2)
    assert nb == FILT_NB
    tn = min(256, d)
    vec = lambda a: a.reshape(1, fw)
    small = lambda shape: pl.BlockSpec(shape, lambda m, sg, j: (0,) * len(shape))
    return pl.pallas_call(
        _filter_dft_kernel,
        grid=(n2 // nb, nseg, d // tn),
        in_specs=[pl.BlockSpec((nb, half, emb_pad), lambda m, sg, j: (m, 0, 0)),
                  small((emb_pad, fw)), small((1, fw)), small((fw, fw)), small((1, fw)),
                  small((fw, fw)), small((1, fw)), small((1, fw)),
                  pl.BlockSpec((1, 3 * fw, tn), lambda m, sg, j: (sg, 0, j)),
                  pl.BlockSpec((1, tn), lambda m, sg, j: (0, j)),
                  pl.BlockSpec((nb, 2 * FILT_K1, half), lambda m, sg, j: (m, 0, 0))],
        out_specs=[pl.BlockSpec((1, 2 * FILT_K1, nb, tn), lambda m, sg, j: (sg, 0, m, j)),
                   pl.BlockSpec((nseg, 1, d), lambda m, sg, j: (0, 0, 0))],
        out_shape=[jax.ShapeDtypeStruct((nseg, 2 * FILT_K1, n2, d), BF16),
                   jax.ShapeDtypeStruct((nseg, 1, d), F32)],
        scratch_shapes=[pltpu.VMEM((nb, half, 3 * fw), BF16),
                        pltpu.VMEM((nb, 2 * FILT_K1, tn), F32)],
        compiler_params=_params("arbitrary", "arbitrary", "arbitrary"),
        name="hyena_filter_dft_stage_a",
    )(z, w0, vec(f_b0), f_w1, vec(f_b1), f_w2, vec(f_b2), vec(f_freq), w3, dl, taf)


def _dft_tables(l):
    n1 = FFT_N1
    n = 2 * l
    n2 = n // n1
    k1 = np.arange(n1)[:, None]
    f1 = np.exp(-2j * np.pi * k1 * np.arange(n1 // 2)[None, :] / n1)
    tw = np.exp(-2j * np.pi * np.arange(n2)[:, None] * np.arange(n1)[None, :] / n)
    f1r, f1i = jnp.asarray(f1.real, F32), jnp.asarray(f1.imag, F32)
    twr, twi = jnp.asarray(tw.real, F32)[:, :, None], jnp.asarray(tw.imag, F32)[:, :, None]
    fr = twr * f1r - twi * f1i
    fi = twr * f1i + twi * f1r
    ta = jnp.concatenate([jnp.concatenate([fr, -fi], axis=2),
                          jnp.concatenate([fi, fr], axis=2)], axis=1)
    fb = np.exp(-2j * np.pi * np.arange(n2)[:, None] * np.arange(n2)[None, :] / n2)
    tb = np.block([[fb.real, -fb.imag], [fb.imag, fb.real]])
    taf = jnp.concatenate([ta[:, 0:FILT_K1, 0:n1 // 2], ta[:, n1:n1 + FILT_K1, 0:n1 // 2]], axis=1)
    fm = np.exp(-2j * np.pi * (np.arange(n2)[:, None] + 1) * np.arange(n2)[None, :] / n2)
    tb_mirror = np.block([[fm.real, fm.imag], [fm.imag, -fm.real]])
    return (ta.astype(BF16), jnp.swapaxes(ta, 1, 2).astype(BF16), jnp.asarray(tb, BF16), jnp.asarray(tb.T, BF16),
            taf.astype(BF16), jnp.asarray(np.stack([tb, tb_mirror]), BF16))


FFT_NB = BF16_SUBLANES


def _fft_a_kernel(x_ref, t_ref, o_ref, r_s):
    planes = x_ref.shape[1]
    xt = [jnp.swapaxes(x_ref[0, c], 0, 1) for c in range(planes)]
    for jj in range(FFT_NB):
        x = jnp.concatenate([xt[c][jj] for c in range(planes)], axis=0).astype(BF16)
        r_s[jj] = jnp.dot(t_ref[jj], x, preferred_element_type=F32)
    o_ref[0] = jnp.swapaxes(r_s[...], 0, 1).astype(o_ref.dtype)


def _fft_a(x5, ta, *, n2, d, seg):
    s, p, half, _, _ = x5.shape
    kdim = p * half
    tn = min(256, d)
    sb, db = (seg * d) // tn, d // tn
    return pl.pallas_call(
        _fft_a_kernel,
        grid=(s, n2 // FFT_NB, db),
        in_specs=[pl.BlockSpec((1, p, half, FFT_NB, tn), lambda si, m, j: (si, 0, 0, m, sb + j)),
                  pl.BlockSpec((FFT_NB, 2 * FFT_N1, kdim), lambda si, m, j: (m, 0, 0))],
        out_specs=pl.BlockSpec((1, 2 * FFT_N1, FFT_NB, tn), lambda si, m, j: (si, 0, m, j)),
        out_shape=jax.ShapeDtypeStruct((s, 2 * FFT_N1, n2, d), BF16),
        scratch_shapes=[pltpu.VMEM((FFT_NB, 2 * FFT_N1, tn), F32)],
        compiler_params=_params("parallel", "parallel", "parallel"),
        name="dft_stage_a",
    )(x5, ta)


def _fft_b_kernel(a_ref, af_ref, ab_ref, sc_ref, tb_ref, tbf_ref, tbt_ref, o_ref, *, n2):
    z = jnp.dot(tb_ref[...], a_ref[...].reshape(2 * n2, -1), preferred_element_type=F32)
    tbf = tbf_ref[0]
    hf = jnp.dot(tbf, af_ref[0].reshape(2 * n2, -1), preferred_element_type=F32)
    hb = jnp.dot(tbf, ab_ref[0].reshape(2 * n2, -1), preferred_element_type=F32)
    sc = sc_ref[0]
    gr = (hf[0:n2] + hb[0:n2]) * sc
    gi = (hf[n2:2 * n2] - hb[n2:2 * n2]) * sc
    zr, zi = z[0:n2], z[n2:2 * n2]
    prod = jnp.concatenate([zr * gr - zi * gi, zr * gi + zi * gr], axis=0).astype(BF16)
    y = jnp.dot(tbt_ref[...], prod, preferred_element_type=F32)
    o_ref[...] = y.reshape(o_ref.shape).astype(o_ref.dtype)


def _fft_b(a, af, scale, tb, tbf, tbt, order, *, n2, d):
    tn = min(2048, d)
    blk = (2, 1, n2, tn)
    fblk = (1, 2, 1, n2, tn)
    mirrored = lambda k: k > FFT_N1 // 2
    fk = lambda k: jnp.where(mirrored(k), FFT_N1 - k, k)
    return pl.pallas_call(
        functools.partial(_fft_b_kernel, n2=n2),
        grid=(FFT_N1, d // tn),
        in_specs=[pl.BlockSpec(blk, lambda k, j: (0, k, 0, j)),
                  pl.BlockSpec(fblk, lambda k, j: (2 * order, 0, fk(k), 0, j)),
                  pl.BlockSpec(fblk, lambda k, j: (2 * order + 1, 0, fk(k), 0, j)),
                  pl.BlockSpec((1, 1, tn), lambda k, j: (order, 0, j)),
                  pl.BlockSpec((2 * n2, 2 * n2), lambda k, j: (0, 0)),
                  pl.BlockSpec((1, 2 * n2, 2 * n2), lambda k, j: (jnp.where(mirrored(k), 1, 0), 0, 0)),
                  pl.BlockSpec((2 * n2, 2 * n2), lambda k, j: (0, 0))],
        out_specs=pl.BlockSpec(blk, lambda k, j: (0, k, 0, j)),
        out_shape=jax.ShapeDtypeStruct((2, FFT_N1, n2, d), BF16),
        compiler_params=_params("parallel", "parallel"),
        name="dft_stage_b_spectrum_product",
    )(a, af, af, scale, tb, tbf, tbt)


def _fft_ainv_kernel(y_ref, t_ref, u_ref, gate_ref, d_ref, o_ref, r_s):
    half = o_ref.shape[1]
    yt = jnp.swapaxes(y_ref[...].astype(F32), 0, 1).astype(BF16)
    for jj in range(FFT_NB):
        x = jnp.dot(t_ref[jj], yt[jj], preferred_element_type=F32)
        for c in range(2):
            r_s[c, jj] = x[c * half:(c + 1) * half]
    for c in range(2):
        o_ref[c] = gate_ref[c] * (jnp.swapaxes(r_s[c], 0, 1) + d_ref[...] * u_ref[c])


def _fft_ainv(y3, tat, u4, gate4, dskip, *, n2, d, useg, gseg):
    b, half, _, _ = u4.shape
    tn = min(128, d)
    db = d // tn
    usb, gsb = (useg * d) // tn, (gseg * d) // tn
    return pl.pallas_call(
        _fft_ainv_kernel,
        grid=(n2 // FFT_NB, db),
        in_specs=[pl.BlockSpec((2 * FFT_N1, FFT_NB, tn), lambda m, j: (0, m, j)),
                  pl.BlockSpec((FFT_NB, FFT_N1, 2 * FFT_N1), lambda m, j: (m, 0, 0)),
                  pl.BlockSpec((b, half, FFT_NB, tn), lambda m, j: (0, 0, m, usb + j)),
                  pl.BlockSpec((b, half, FFT_NB, tn), lambda m, j: (0, 0, m, gsb + j)),
                  pl.BlockSpec((1, tn), lambda m, j: (0, j))],
        out_specs=pl.BlockSpec((b, half, FFT_NB, tn), lambda m, j: (0, 0, m, j)),
        out_shape=jax.ShapeDtypeStruct((b, half, n2, d), F32),
        scratch_shapes=[pltpu.VMEM((b, FFT_NB, half, tn), F32)],
        compiler_params=_params("parallel", "parallel"),
        name="dft_stage_a_inverse_gate",
    )(y3, tat, u4, gate4, dskip)


def _hyena_mixer(h, hy, l, d):
    (w_in, b_in, sconv_w, sconv_b, f_w0, f_b0, f_w1, f_b1, f_w2, f_b2, f_freq, f_w3, d_skip, w_out) = hy
    b = h.shape[0]
    assert b == 2, "the two batch rows ride the real / imaginary planes of one complex DFT"
    n1 = FFT_N1
    half = n1 // 2
    n2 = (2 * l) // n1
    assert half * n2 == l
    z3 = _mm_conv(_hy_in_kernel, h, [w_in.astype(BF16)],
                  [b_in.reshape(1, -1), sconv_w, sconv_b.reshape(1, -1)], 3 * d, F32, "hyena_in_proj_conv", nz=2)
    ta, tat, tb, tbt, taf, tbf = _dft_tables(l)
    nseg = 2 * HY_ORDER
    af, fsum = _hyena_filter_spectra_a(l, d, taf, f_w0, f_b0, f_w1, f_b1, f_w2, f_b2, f_freq, f_w3)
    af = af.reshape(nseg, 2, FILT_K1, n2, d)
    scale = 1.0 / ((fsum[0::2] + fsum[1::2]) * (2 * l))
    z4 = z3.reshape(b, half, n2, 3 * d)
    y, yseg = z4, 0
    for o in range(HY_ORDER):
        a = _fft_a(y[None], ta, n2=n2, d=d, seg=yseg)
        yb = _fft_b(a.reshape(2, n1, n2, d), af, scale, tb, tbf, tbt, o, n2=n2, d=d)
        y = _fft_ainv(yb.reshape(2 * n1, n2, d), tat, y, z4, d_skip[o].reshape(1, d),
                      n2=n2, d=d, useg=yseg, gseg=o + 1)
        yseg = 0
    return y.reshape(b, l, d), w_out.astype(BF16)


def _rope_tables(l):
    t = jnp.arange(l)
    row = (t // GRID_W).astype(F32)[:, None]
    col = (t % GRID_W).astype(F32)[:, None]
    width = 2 * HEAD_DIM
    cos, sin, avg, perm = [], [], [], []
    for dim in (A_HALF, HEAD_DIM):
        nf = dim // 4
        inv = jnp.asarray(ROPE_BASE ** (-np.arange(nf) / nf), F32)[None, :]
        ar, ac = row * inv, col * inv
        c = jnp.concatenate([jnp.cos(ar), jnp.cos(ar), jnp.cos(ac), jnp.cos(ac)], axis=1)
        s = jnp.concatenate([-jnp.sin(ar), jnp.sin(ar), -jnp.sin(ac), jnp.sin(ac)], axis=1)
        reps = width // dim
        cos.append(jnp.tile(c, (1, reps)))
        sin.append(jnp.tile(s, (1, reps)))
        lane = np.arange(width)
        avg.append((lane[:, None] // dim == lane[None, :] // dim) / dim)
        partner = np.where(lane % (2 * nf) < nf, lane + nf, lane - nf)
        perm.append((lane[:, None] == partner[None, :]).astype(np.float64))
    return (jnp.stack(cos), jnp.stack(sin), jnp.asarray(np.stack(avg), BF16), jnp.asarray(np.stack(perm), BF16))


def _attn_mixer(h, hc, lam_init, w_in, w_out, a_q_g, a_k_g, lq1, lk1, lq2, lk2, a_sub_g, b_q_g, b_k_g, b_sink):
    b, l, d = h.shape
    c = hc.shape[1]
    a_heads = d // (2 * HEAD_DIM)
    b_heads = d // (2 * HEAD_DIM)
    q_cols = (a_heads + b_heads) * HEAD_DIM
    in_cols = w_in.shape[1]
    lam = (jnp.exp(jnp.sum(lq1 * lk1)) - jnp.exp(jnp.sum(lq2 * lk2)) + lam_init).reshape(1, 1)
    sink = b_sink.reshape(B_KV_HEADS, b_heads // B_KV_HEADS)

    ones = lambda n: jnp.ones((n,), F32)
    tile = lambda v, n: jnp.tile(v, n)
    gain = jnp.concatenate([tile(a_q_g, 2 * a_heads), tile(b_q_g, b_heads), tile(a_k_g, 2 * a_heads),
                            ones(a_heads * HEAD_DIM), tile(b_k_g, B_KV_HEADS),
                            ones(B_KV_HEADS * HEAD_DIM)]).reshape(1, in_cols)
    scale = jnp.concatenate([jnp.full((a_heads * HEAD_DIM,), A_HALF ** -0.5 * math.log2(math.e), F32),
                             jnp.full((b_heads * HEAD_DIM,), HEAD_DIM ** -0.5, F32),
                             ones(in_cols - q_cols)]).reshape(1, in_cols)
    e = [0, a_heads // 2, (a_heads + b_heads) // 2, (2 * a_heads + b_heads) // 2,
         (3 * a_heads + b_heads) // 2, (3 * a_heads + b_heads) // 2 + 1, (3 * a_heads + b_heads) // 2 + 2]
    plain = ((e[3], e[4]), (e[5], e[6]))
    cos_tab, sin_tab, avg, perm = _rope_tables(l)

    def type_of(j0):
        def f(j):
            jj = j + j0
            is_a = (jj < e[1]) | ((jj >= e[2]) & (jj < e[3]))
            return jnp.where(is_a, 0, 1)
        return f

    w_bf = _col_tiles(w_in.astype(BF16), 2 * HEAD_DIM)
    p = _inproj(h.reshape(b * l, d), w_bf, gain, scale, cos_tab, sin_tab, avg, perm, plain=plain,
                type_of_block=type_of(0), col0=0, ncols=in_cols, l=l, rope=True).reshape(b, l, in_cols)
    kv0 = e[2]
    plain_c = tuple((lo - kv0, hi - kv0) for lo, hi in plain)
    pc = _inproj(hc.reshape(b * c, d), w_bf, gain, scale, cos_tab, sin_tab, avg, perm, plain=plain_c,
                 type_of_block=type_of(kv0), col0=kv0, ncols=in_cols - q_cols, l=c, rope=False
                 ).reshape(b, c, in_cols - q_cols)
    score_bound = (A_HALF ** 0.5 * math.log2(math.e)) * jnp.max(jnp.abs(a_q_g)) * jnp.max(jnp.abs(a_k_g))
    oa = _diffattn(lam, p, pc, a_sub_g, score_bound, heads=a_heads, l=l, c=c, out_scale=1.0 - lam_init)
    ob = _winattn(sink, p, pc, a_heads=a_heads, b_heads=b_heads, l=l, c=c)
    w_out_bf = w_out.astype(BF16)
    na = a_heads * HEAD_DIM
    return [oa, ob], [w_out_bf[:na], w_out_bf[na:]]


def kernel(x, c, ctx, c_ctx, ada_w, ada_b, norm1_g, norm2_g, attn_w_in, attn_w_out, a_q_g, a_k_g, a_lam_q1, a_lam_k1, a_lam_q2, a_lam_k2, a_sub_g, b_q_g, b_k_g, b_sink, hy_w_in, hy_b_in, hy_sconv_w, hy_sconv_b, hy_f_w0, hy_f_b0, hy_f_w1, hy_f_b1, hy_f_w2, hy_f_b2, hy_f_freq, hy_f_w3, hy_d, hy_w_out, ffn_w_gate, ffn_w_val, ffn_conv_w, ffn_conv_b, ffn_w_down):
    b, l, d = x.shape
    depth = ada_w.shape[0]
    cc = jnp.concatenate([c, c_ctx[None, :], jnp.zeros((8 - b - 1, d), F32)], axis=0)
    m = _ada(cc, ada_w, ada_b)
    xs = x
    for layer in range(depth):
        i = layer // 2
        lat = [m[layer, :b, k * d:(k + 1) * d].reshape(b, 1, d) for k in range(6)]
        sh1, sc1, g1, sh2, sc2, g2 = lat
        h = _normmod(xs, norm1_g[layer], sc1, sh1)
        if layer % 2 == 0:
            mc = [jnp.broadcast_to(m[layer, b, k * d:(k + 1) * d].reshape(1, 1, d), (b, 1, d)) for k in range(2)]
            hc = _normmod(ctx, norm1_g[layer], mc[1], mc[0])
            parts, ws = _attn_mixer(h, hc, 0.8 - 0.6 * math.exp(-0.3 * layer), attn_w_in[i], attn_w_out[i],
                                    a_q_g[i], a_k_g[i], a_lam_q1[i], a_lam_k1[i], a_lam_q2[i], a_lam_k2[i],
                                    a_sub_g[i], b_q_g[i], b_k_g[i], b_sink[i])
        else:
            hy = (hy_w_in[i], hy_b_in[i], hy_sconv_w[i], hy_sconv_b[i], hy_f_w0[i], hy_f_b0[i], hy_f_w1[i],
                  hy_f_b1[i], hy_f_w2[i], hy_f_b2[i], hy_f_freq[i], hy_f_w3[i], hy_d[i], hy_w_out[i])
            y, w_o = _hyena_mixer(h, hy, l, d)
            parts, ws = [y], [w_o]
        xs = _mm_resid(parts, ws, xs, g1)
        h2 = _normmod(xs, norm2_g[layer], sc2, sh2)
        d_ff = ffn_w_gate.shape[2]
        hid = _mm_conv(_ffn_up_kernel, h2, [ffn_w_gate[layer].astype(BF16), ffn_w_val[layer].astype(BF16)],
                       [ffn_conv_w[layer], ffn_conv_b[layer].reshape(1, -1)], d_ff, BF16, "ffn_up_conv_glu", nz=2)
        xs = _mm_resid([hid], [ffn_w_down[layer].astype(BF16)], xs, g2)
    return xs
```

```python
import functools
import math

import numpy as np
import jax
import jax.numpy as jnp
from jax import lax
from jax.experimental import pallas as pl
from jax.experimental.pallas import tpu as pltpu

F32 = jnp.float32
BF16 = jnp.bfloat16
HIGHEST = lax.Precision.HIGHEST

HEAD_DIM = 128
A_HALF = HEAD_DIM // 2
B_KV_HEADS = 2
GRID_W = 64
WINDOW = 128
ROPE_BASE = 10000.0
EPS = 1e-6
NEG = -1e30
HY_ORDER = 2
HY_DECAY_TARGET = 1e-2
HY_MAX_DECAY_PCT = 0.3
HY_MIN_DECAY_PCT = 1.5

BF16_SUBLANES = 16
FFT_N1 = 256
VMEM_LIMIT_BYTES = 56 * 1024 * 1024


def _params(*sem):
    return pltpu.CompilerParams(dimension_semantics=sem, vmem_limit_bytes=VMEM_LIMIT_BYTES)


def _col_tiles(w, tn):
    k, n = w.shape
    return jnp.transpose(w.reshape(k, n // tn, tn), (1, 0, 2))


def _ada_kernel(c_ref, w_ref, b_ref, o_ref):
    c = c_ref[...]
    s = c * (1.0 / (1.0 + jnp.exp(-c)))
    o_ref[0] = jnp.dot(s, w_ref[0], preferred_element_type=F32, precision=HIGHEST) + b_ref[0]


def _ada(cc, ada_w, ada_b):
    depth, d, n = ada_w.shape
    tn = 1024
    return pl.pallas_call(
        _ada_kernel,
        grid=(depth, n // tn),
        in_specs=[pl.BlockSpec((8, d), lambda l, j: (0, 0)),
                  pl.BlockSpec((1, d, tn), lambda l, j: (l, 0, j)),
                  pl.BlockSpec((1, 1, tn), lambda l, j: (l, 0, j))],
        out_specs=pl.BlockSpec((1, 8, tn), lambda l, j: (l, 0, j)),
        out_shape=jax.ShapeDtypeStruct((depth, 8, n), F32),
        compiler_params=_params("parallel", "parallel"),
        name="ada_modulation",
    )(cc, ada_w, ada_b.reshape(depth, 1, n))


NORM_ROWS = 32


def _normmod_kernel(x_ref, g_ref, sc_ref, sh_ref, o_ref):
    scale = g_ref[...] * (1.0 + sc_ref[0])
    shift = sh_ref[0]
    rows = min(NORM_ROWS, x_ref.shape[1])

    def slab(r, carry):
        rs = pl.ds(pl.multiple_of(r * rows, rows), rows)
        x = x_ref[0, rs, :]
        ms = jnp.mean(x * x, axis=-1, keepdims=True)
        o_ref[0, rs, :] = (x * lax.rsqrt(ms + EPS) * scale + shift).astype(o_ref.dtype)
        return carry

    lax.fori_loop(0, x_ref.shape[1] // rows, slab, 0, unroll=4)


def _normmod(x, g, sc, sh):
    b, l, d = x.shape
    tm = min(512, l)
    return pl.pallas_call(
        _normmod_kernel,
        grid=(b, l // tm),
        in_specs=[pl.BlockSpec((1, tm, d), lambda bi, i: (bi, i, 0)),
                  pl.BlockSpec((1, d), lambda bi, i: (0, 0)),
                  pl.BlockSpec((1, 1, d), lambda bi, i: (bi, 0, 0)),
                  pl.BlockSpec((1, 1, d), lambda bi, i: (bi, 0, 0))],
        out_specs=pl.BlockSpec((1, tm, d), lambda bi, i: (bi, i, 0)),
        out_shape=jax.ShapeDtypeStruct((b, l, d), BF16),
        compiler_params=_params("parallel", "parallel"),
        name="rmsnorm_modulate",
    )(x, g.reshape(1, d), sc, sh)


def _dot_split(x, m):
    hi = x.astype(BF16)
    lo = (x - hi.astype(F32)).astype(BF16)
    return jnp.dot(hi, m, preferred_element_type=F32) + jnp.dot(lo, m, preferred_element_type=F32)


def _inproj_kernel(h_ref, w_ref, gain_ref, scale_ref, cos_ref, sin_ref, avg_ref, perm_ref, o_ref, *, plain, rope):
    j = pl.program_id(1)
    z = jnp.dot(h_ref[...], w_ref[0], preferred_element_type=F32)
    is_plain = functools.reduce(jnp.logical_or, [(j >= lo) & (j < hi) for lo, hi in plain])

    @pl.when(is_plain)
    def _():
        o_ref[...] = z.astype(o_ref.dtype)

    @pl.when(jnp.logical_not(is_plain))
    def _():
        rows = min(256, z.shape[0])
        for r0 in range(0, z.shape[0], rows):
            rs = slice(r0, r0 + rows)
            zc = z[rs]
            ms = _dot_split(zc * zc, avg_ref[0])
            y = zc * lax.rsqrt(ms + EPS) * gain_ref[...]
            if rope:
                y = y * cos_ref[0, rs, :] + _dot_split(y, perm_ref[0]) * sin_ref[0, rs, :]
            o_ref[rs, :] = (y * scale_ref[...]).astype(o_ref.dtype)


def _inproj(h2d, w, gain, scale, cos_tab, sin_tab, avg, perm, *, plain, type_of_block, col0, ncols, l, rope):
    m, d = h2d.shape
    tn = 2 * HEAD_DIM
    tm = min(1024, l)
    pos_blocks = l // tm
    kern = functools.partial(_inproj_kernel, plain=plain, rope=rope)
    by_type = lambda i, j: (type_of_block(j), 0, 0)
    return pl.pallas_call(
        kern,
        grid=(m // tm, ncols // tn),
        in_specs=[pl.BlockSpec((tm, d), lambda i, j: (i, 0)),
                  pl.BlockSpec((1, d, tn), lambda i, j: (j + col0, 0, 0)),
                  pl.BlockSpec((1, tn), lambda i, j: (0, j + col0)),
                  pl.BlockSpec((1, tn), lambda i, j: (0, j + col0)),
                  pl.BlockSpec((1, tm, tn), lambda i, j: (type_of_block(j), i % pos_blocks, 0)),
                  pl.BlockSpec((1, tm, tn), lambda i, j: (type_of_block(j), i % pos_blocks, 0)),
                  pl.BlockSpec((1, tn, tn), by_type),
                  pl.BlockSpec((1, tn, tn), by_type)],
        out_specs=pl.BlockSpec((tm, tn), lambda i, j: (i, j)),
        out_shape=jax.ShapeDtypeStruct((m, ncols), BF16),
        compiler_params=_params("parallel", "arbitrary"),
        name="attn_in_projection",
    )(h2d, w, gain, scale, cos_tab, sin_tab, avg, perm)


def _diffattn_kernel(lam_ref, q_ref, kc_ref, vc_ref, k_ref, v_ref, g_ref, o_ref,
                     q2_s, m_s, l_s, acc_s, *, tq, nkv, out_scale):
    kv = pl.program_id(3)

    def process(k, v):
        s = lax.dot_general(q2_s[...], k, (((1,), (1,)), ((), ())), preferred_element_type=F32)
        m_prev = m_s[...]
        m_new = jnp.maximum(m_prev, jnp.max(s, axis=-1, keepdims=True))
        alpha = jnp.exp2(m_prev - m_new)
        p = jnp.exp2(s - m_new)
        l_s[...] = alpha * l_s[...] + jnp.sum(p, axis=-1, keepdims=True)
        acc_s[...] = alpha * acc_s[...] + jnp.dot(p.astype(BF16), v, preferred_element_type=F32)
        m_s[...] = m_new

    @pl.when(kv == 0)
    def _():
        q = q_ref[0]
        lane = lax.broadcasted_iota(jnp.int32, q.shape, 1)
        zero = jnp.zeros_like(q)
        q2_s[0:tq, :] = jnp.where(lane < A_HALF, q, zero)
        q2_s[tq:2 * tq, :] = jnp.where(lane >= A_HALF, q, zero)
        m_s[...] = jnp.full(m_s.shape, -jnp.inf, F32)
        l_s[...] = jnp.zeros(l_s.shape, F32)
        acc_s[...] = jnp.zeros(acc_s.shape, F32)
        process(kc_ref[0], vc_ref[0])

    process(k_ref[0], v_ref[0])

    @pl.when(kv == nkv - 1)
    def _():
        o = acc_s[...] / l_s[...]
        d = o[0:tq] - lam_ref[0, 0] * o[tq:2 * tq]
        ms = jnp.mean(d * d, axis=-1, keepdims=True)
        o_ref[0] = (d * lax.rsqrt(ms + EPS) * g_ref[...] * out_scale).astype(o_ref.dtype)


def _diffattn_fast_kernel(lam_ref, q_ref, kc_ref, vct_ref, k_ref, vt_ref, g_ref, o_ref,
                          q2_s, l_s, acc_s, p_s, *, tq, tk, kc, nkv, out_scale):
    kv = pl.program_id(3)
    dn = (((1,), (1,)), ((), ()))

    def weights(k):
        n = k.shape[0]
        p = jnp.exp2(lax.dot_general(k, q2_s[...], dn, preferred_element_type=F32))
        return p.astype(BF16), jnp.sum(p.reshape(n // 8, 8, 2 * tq), axis=0)

    @pl.when(kv == 0)
    def _():
        q = q_ref[0]
        lane = lax.broadcasted_iota(jnp.int32, q.shape, 1)
        zero = jnp.zeros_like(q)
        q2_s[0:tq, :] = jnp.where(lane < A_HALF, q, zero)
        q2_s[tq:2 * tq, :] = jnp.where(lane >= A_HALF, q, zero)
        p, ls = weights(kc_ref[0])
        l_s[...] = ls
        acc_s[...] = jnp.dot(vct_ref[0, 0], p, preferred_element_type=F32)

    ls = l_s[...]
    for ci in range(tk // kc):
        p, lc = weights(k_ref[0, ci * kc:(ci + 1) * kc, :])
        p_s[ci * kc:(ci + 1) * kc, :] = p
        ls = ls + lc
    l_s[...] = ls
    acc_s[...] += jnp.dot(vt_ref[0, 0], p_s[...], preferred_element_type=F32)

    @pl.when(kv == nkv - 1)
    def _():
        l = jnp.sum(l_s[...], axis=0, keepdims=True)
        o = acc_s[...] / l
        d = o[:, 0:tq] - lam_ref[0, 0] * o[:, tq:2 * tq]
        ms = jnp.mean(d * d, axis=0, keepdims=True)
        y = d * lax.rsqrt(ms + EPS) * g_ref[...] * out_scale
        o_ref[0] = y.T.astype(o_ref.dtype)


def _diffattn_fast(lam, p, pc, sub_g, *, heads, l, c, out_scale):
    b = p.shape[0]
    tq = min(1024, l)
    tk = min(4096, l)
    kc = min(512, tk)
    nkv = l // tk
    k_blk = 2 * heads
    hd = heads * HEAD_DIM
    vt = jnp.transpose(p[:, :, 3 * hd:4 * hd].reshape(b, l, heads, HEAD_DIM), (0, 2, 3, 1))
    vct = jnp.transpose(pc[:, :, hd:2 * hd].reshape(b, c, heads, HEAD_DIM), (0, 2, 3, 1))
    kern = functools.partial(_diffattn_fast_kernel, tq=tq, tk=tk, kc=kc, nkv=nkv, out_scale=out_scale)
    return pl.pallas_call(
        kern,
        grid=(b, heads, l // tq, nkv),
        in_specs=[pl.BlockSpec(memory_space=pltpu.SMEM),
                  pl.BlockSpec((1, tq, HEAD_DIM), lambda bi, h, i, kv: (bi, i, h)),
                  pl.BlockSpec((1, c, HEAD_DIM), lambda bi, h, i, kv: (bi, 0, h)),
                  pl.BlockSpec((1, 1, HEAD_DIM, c), lambda bi, h, i, kv: (bi, h, 0, 0)),
                  pl.BlockSpec((1, tk, HEAD_DIM), lambda bi, h, i, kv: (bi, kv, k_blk + h)),
                  pl.BlockSpec((1, 1, HEAD_DIM, tk), lambda bi, h, i, kv: (bi, h, 0, kv)),
                  pl.BlockSpec((HEAD_DIM, 1), lambda bi, h, i, kv: (0, 0))],
        out_specs=pl.BlockSpec((1, tq, HEAD_DIM), lambda bi, h, i, kv: (bi, i, h)),
        out_shape=jax.ShapeDtypeStruct((b, l, hd), BF16),
        scratch_shapes=[pltpu.VMEM((2 * tq, HEAD_DIM), BF16),
                        pltpu.VMEM((8, 2 * tq), F32),
                        pltpu.VMEM((HEAD_DIM, 2 * tq), F32),
                        pltpu.VMEM((tk, 2 * tq), BF16)],
        compiler_params=_params("parallel", "parallel", "parallel", "arbitrary"),
        name="diff_attention_bounded",
    )(lam, p, pc, vct, p, vt, sub_g.reshape(HEAD_DIM, 1))


SCORE_BOUND_LOG2 = 60.0


def _diffattn(lam, p, pc, sub_g, score_bound, *, heads, l, c, out_scale):
    kw = dict(heads=heads, l=l, c=c, out_scale=out_scale)
    return lax.cond(score_bound < SCORE_BOUND_LOG2,
                    lambda *a: _diffattn_fast(*a, **kw), lambda *a: _diffattn_safe(*a, **kw),
                    lam, p, pc, sub_g)


def _diffattn_safe(lam, p, pc, sub_g, *, heads, l, c, out_scale):
    b = p.shape[0]
    tq = min(256, l)
    tk = min(512, l)
    nkv = l // tk
    q_blk = 0
    k_blk = 2 * heads
    v_blk = 3 * heads
    kern = functools.partial(_diffattn_kernel, tq=tq, nkv=nkv, out_scale=out_scale)
    return pl.pallas_call(
        kern,
        grid=(b, heads, l // tq, nkv),
        in_specs=[pl.BlockSpec(memory_space=pltpu.SMEM),
                  pl.BlockSpec((1, tq, HEAD_DIM), lambda bi, h, i, kv: (bi, i, q_blk + h)),
                  pl.BlockSpec((1, c, HEAD_DIM), lambda bi, h, i, kv: (bi, 0, h)),
                  pl.BlockSpec((1, c, HEAD_DIM), lambda bi, h, i, kv: (bi, 0, heads + h)),
                  pl.BlockSpec((1, tk, HEAD_DIM), lambda bi, h, i, kv: (bi, kv, k_blk + h)),
                  pl.BlockSpec((1, tk, HEAD_DIM), lambda bi, h, i, kv: (bi, kv, v_blk + h)),
                  pl.BlockSpec((1, HEAD_DIM), lambda bi, h, i, kv: (0, 0))],
        out_specs=pl.BlockSpec((1, tq, HEAD_DIM), lambda bi, h, i, kv: (bi, i, h)),
        out_shape=jax.ShapeDtypeStruct((b, l, heads * HEAD_DIM), BF16),
        scratch_shapes=[pltpu.VMEM((2 * tq, HEAD_DIM), BF16),
                        pltpu.VMEM((2 * tq, 1), F32),
                        pltpu.VMEM((2 * tq, 1), F32),
                        pltpu.VMEM((2 * tq, HEAD_DIM), F32)],
        compiler_params=_params("parallel", "parallel", "parallel", "arbitrary"),
        name="diff_attention",
    )(lam, p, pc, pc, p, p, sub_g.reshape(1, HEAD_DIM))


def _winattn_kernel(sink_ref, q_ref, kp_ref, km_ref, kn_ref, vp_ref, vm_ref, vn_ref, kc_ref, vc_ref,
                    o_ref, *, tq, l, group):
    g = pl.program_id(1)
    i = pl.program_id(2)
    kband = jnp.concatenate([kp_ref[0], km_ref[0], kn_ref[0]], axis=0)
    vband = jnp.concatenate([vp_ref[0], vm_ref[0], vn_ref[0]], axis=0)
    nk = tq + 2 * WINDOW
    qpos = i * tq + lax.broadcasted_iota(jnp.int32, (tq, nk), 0)
    kpos = i * tq - WINDOW + lax.broadcasted_iota(jnp.int32, (tq, nk), 1)
    valid = (jnp.abs(kpos - qpos) <= WINDOW) & (kpos >= 0) & (kpos < l)
    kc = kc_ref[0]
    vc = vc_ref[0]
    dn = (((1,), (1,)), ((), ()))
    for r in range(group):
        sl = slice(r * HEAD_DIM, (r + 1) * HEAD_DIM)
        q = q_ref[0, :, sl]
        s_lat = jnp.where(valid, lax.dot_general(q, kband, dn, preferred_element_type=F32), NEG)
        s_ctx = lax.dot_general(q, kc, dn, preferred_element_type=F32)
        sk = sink_ref[g, r]
        m = jnp.maximum(jnp.maximum(jnp.max(s_lat, axis=-1, keepdims=True),
                                    jnp.max(s_ctx, axis=-1, keepdims=True)), sk)
        p_lat = jnp.exp(s_lat - m)
        p_ctx = jnp.exp(s_ctx - m)
        denom = (jnp.sum(p_lat, axis=-1, keepdims=True) + jnp.sum(p_ctx, axis=-1, keepdims=True)
                 + jnp.exp(sk - m))
        o = (jnp.dot(p_lat.astype(BF16), vband, preferred_element_type=F32)
             + jnp.dot(p_ctx.astype(BF16), vc, preferred_element_type=F32))
        o_ref[0, :, sl] = (o / denom).astype(o_ref.dtype)


def _winattn(sink, p, pc, *, a_heads, b_heads, l, c):
    b = p.shape[0]
    group = b_heads // B_KV_HEADS
    tq = min(256, l)
    wpb = tq // WINDOW
    nwb = l // WINDOW
    gw = group * HEAD_DIM
    q_blk = a_heads * HEAD_DIM // gw
    k_blk = (a_heads + b_heads) + 2 * a_heads
    v_blk = k_blk + B_KV_HEADS
    kc_blk = 2 * a_heads
    vc_blk = kc_blk + B_KV_HEADS
    kern = functools.partial(_winattn_kernel, tq=tq, l=l, group=group)

    def prev_map(col):
        return lambda bi, g, i: (bi, jnp.maximum(i * wpb - 1, 0), col + g)

    def main_map(col):
        return lambda bi, g, i: (bi, i, col + g)

    def next_map(col):
        return lambda bi, g, i: (bi, jnp.minimum((i + 1) * wpb, nwb - 1), col + g)

    return pl.pallas_call(
        kern,
        grid=(b, B_KV_HEADS, l // tq),
        in_specs=[pl.BlockSpec(memory_space=pltpu.SMEM),
                  pl.BlockSpec((1, tq, gw), lambda bi, g, i: (bi, i, q_blk + g)),
                  pl.BlockSpec((1, WINDOW, HEAD_DIM), prev_map(k_blk)),
                  pl.BlockSpec((1, tq, HEAD_DIM), main_map(k_blk)),
                  pl.BlockSpec((1, WINDOW, HEAD_DIM), next_map(k_blk)),
                  pl.BlockSpec((1, WINDOW, HEAD_DIM), prev_map(v_blk)),
                  pl.BlockSpec((1, tq, HEAD_DIM), main_map(v_blk)),
                  pl.BlockSpec((1, WINDOW, HEAD_DIM), next_map(v_blk)),
                  pl.BlockSpec((1, c, HEAD_DIM), lambda bi, g, i: (bi, 0, kc_blk + g)),
                  pl.BlockSpec((1, c, HEAD_DIM), lambda bi, g, i: (bi, 0, vc_blk + g))],
        out_specs=pl.BlockSpec((1, tq, gw), lambda bi, g, i: (bi, i, g)),
        out_shape=jax.ShapeDtypeStruct((b, l, b_heads * HEAD_DIM), BF16),
        compiler_params=_params("parallel", "parallel", "parallel"),
        name="window_attention",
    )(sink, p, p, p, p, p, p, p, pc, pc)


def _mm_resid_kernel(*refs, nparts):
    a_refs = refs[0:nparts]
    w_refs = refs[nparts:2 * nparts]
    res_ref, gate_ref, o_ref = refs[2 * nparts:]
    y = jnp.dot(a_refs[0][0].astype(BF16), w_refs[0][0], preferred_element_type=F32)
    for a_ref, w_ref in zip(a_refs[1:], w_refs[1:]):
        y = y + jnp.dot(a_ref[0].astype(BF16), w_ref[0], preferred_element_type=F32)
    o_ref[0] = res_ref[0] + gate_ref[0] * y


def _mm_resid(parts, ws, res, gate):
    b, l, n = res.shape
    tm = min(1024, l)
    tn = min(512, n)
    nparts = len(parts)
    ws = [_col_tiles(w, tn) for w in ws]
    in_specs = ([pl.BlockSpec((1, tm, a.shape[2]), lambda bi, i, j: (bi, i, 0)) for a in parts]
                + [pl.BlockSpec((1, w.shape[1], tn), lambda bi, i, j: (j, 0, 0)) for w in ws]
                + [pl.BlockSpec((1, tm, tn), lambda bi, i, j: (bi, i, j)),
                   pl.BlockSpec((1, 1, tn), lambda bi, i, j: (bi, 0, j))])
    return pl.pallas_call(
        functools.partial(_mm_resid_kernel, nparts=nparts),
        grid=(b, l // tm, n // tn),
        in_specs=in_specs,
        out_specs=pl.BlockSpec((1, tm, tn), lambda bi, i, j: (bi, i, j)),
        out_shape=jax.ShapeDtypeStruct((b, l, n), F32),
        compiler_params=_params("parallel", "parallel", "arbitrary"),
        name="out_projection_gated_residual",
    )(*parts, *ws, res, gate)


HALO = BF16_SUBLANES


def _fill_rows(hbuf, hp_ref, hm_ref, hn_ref, tm):
    hbuf[0:HALO, :] = hp_ref[0]
    hbuf[HALO:HALO + tm, :] = hm_ref[0]
    hbuf[HALO + tm:2 * HALO + tm, :] = hn_ref[0]


def _conv3(zbuf, cw_ref, cb_ref, tm, first, last):
    lo = slice(HALO - 1, HALO)
    hi = slice(HALO + tm, HALO + tm + 1)
    zbuf[lo, :] = jnp.where(first, 0.0, zbuf[lo, :])
    zbuf[hi, :] = jnp.where(last, 0.0, zbuf[hi, :])
    z = zbuf[...]
    rows = z.shape[0]
    g = (z * cw_ref[1:2, :] + pltpu.roll(z * cw_ref[0:1, :], 1, 0) + pltpu.roll(z * cw_ref[2:3, :], rows - 1, 0))
    return g[HALO:HALO + tm] + cb_ref[...]


def _gelu_tanh(x):
    return 0.5 * x * (1.0 + jnp.tanh(math.sqrt(2.0 / math.pi) * (x + 0.044715 * (x * x * x))))


def _ffn_up_kernel(hp_ref, hm_ref, hn_ref, wg_ref, wv_ref, cw_ref, cb_ref, o_ref, hbuf, zbuf, zbuf2, *, tm, nt):
    i = pl.program_id(1)
    j = pl.program_id(2)

    @pl.when(j == 0)
    def _():
        _fill_rows(hbuf, hp_ref, hm_ref, hn_ref, tm)

    hw = zbuf.shape[1]
    for hh, zb in enumerate((zbuf, zbuf2)):
        cs = slice(hh * hw, (hh + 1) * hw)
        zb[...] = jnp.dot(hbuf[...], wg_ref[0, :, cs], preferred_element_type=F32)
        g = _conv3(zb, cw_ref.at[:, cs], cb_ref.at[:, cs], tm, i == 0, i == nt - 1)
        v = jnp.dot(hbuf[HALO:HALO + tm, :], wv_ref[0, :, cs], preferred_element_type=F32)
        o_ref[0, :, cs] = (_gelu_tanh(g) * v).astype(o_ref.dtype)


def _hy_in_kernel(hp_ref, hm_ref, hn_ref, w_ref, b_ref, cw_ref, cb_ref, o_ref, hbuf, zbuf, zbuf2, *, tm, nt):
    i = pl.program_id(1)
    j = pl.program_id(2)

    @pl.when(j == 0)
    def _():
        _fill_rows(hbuf, hp_ref, hm_ref, hn_ref, tm)

    hw = zbuf.shape[1]
    for hh, zb in enumerate((zbuf, zbuf2)):
        cs = slice(hh * hw, (hh + 1) * hw)
        zb[...] = jnp.dot(hbuf[...], w_ref[0, :, cs], preferred_element_type=F32) + b_ref[:, cs]
        o_ref[0, :, cs] = _conv3(zb, cw_ref.at[:, cs], cb_ref.at[:, cs], tm, i == 0, i == nt - 1
                                 ).astype(o_ref.dtype)


def _mm_conv(kernel, h, ws, vecs, n, out_dtype, name, nz=1):
    b, l, k = h.shape
    tm = min(512, l)
    tn = min(512, n)
    nt = l // tm
    hpb = tm // HALO
    nhb = l // HALO
    ws = [_col_tiles(w, tn) for w in ws]
    in_specs = ([pl.BlockSpec((1, HALO, k), lambda bi, i, j: (bi, jnp.maximum(i * hpb - 1, 0), 0)),
                 pl.BlockSpec((1, tm, k), lambda bi, i, j: (bi, i, 0)),
                 pl.BlockSpec((1, HALO, k), lambda bi, i, j: (bi, jnp.minimum((i + 1) * hpb, nhb - 1), 0))]
                + [pl.BlockSpec((1, k, tn), lambda bi, i, j: (j, 0, 0)) for _ in ws]
                + [pl.BlockSpec((v.shape[0], tn), lambda bi, i, j: (0, j)) for v in vecs])
    return pl.pallas_call(
        functools.partial(kernel, tm=tm, nt=nt),
        grid=(b, nt, n // tn),
        in_specs=in_specs,
        out_specs=pl.BlockSpec((1, tm, tn), lambda bi, i, j: (bi, i, j)),
        out_shape=jax.ShapeDtypeStruct((b, l, n), out_dtype),
        scratch_shapes=[pltpu.VMEM((tm + 2 * HALO, k), BF16)] + [pltpu.VMEM((tm + 2 * HALO, tn // nz), F32)] * nz,
        compiler_params=_params("parallel", "parallel", "arbitrary"),
        name=name,
    )(h, h, h, *ws, *vecs)


FILT_NB = BF16_SUBLANES
FILT_K1 = FFT_N1 // 2 + 8


def _filter_dft_kernel(z_ref, w0_ref, b0_ref, w1_ref, b1_ref, w2_ref, b2_ref, fr_ref, w3_ref, dl_ref, t_ref,
                       o_ref, s_ref, a3_s, r_s):
    m, seg, j = pl.program_id(0), pl.program_id(1), pl.program_id(2)
    tn = o_ref.shape[3]
    fr = fr_ref[...]

    def layer(a, w_ref, b_ref):
        return jnp.sin(fr * (jnp.dot(a, w_ref[...], preferred_element_type=F32, precision=HIGHEST)
                             + b_ref[...]))

    @pl.when((seg == 0) & (j == 0))
    def _():
        for jj in range(FILT_NB):
            a = layer(layer(layer(z_ref[jj], w0_ref, b0_ref), w1_ref, b1_ref), w2_ref, b2_ref)
            a_hi = a.astype(BF16)
            a_lo = (a - a_hi.astype(F32)).astype(BF16)
            a3_s[jj] = jnp.concatenate([a_hi, a_hi, a_lo], axis=1)

    @pl.when((m == 0) & (seg == 0) & (j == 0))
    def _():
        s_ref[...] = jnp.zeros(s_ref.shape, F32)

    dl = dl_ref[...]
    backward = seg % 2 == 1
    abs_sum = jnp.zeros((1, tn), F32)
    for jj in range(FILT_NB):
        h = (jnp.dot(a3_s[jj], w3_ref[0], preferred_element_type=F32)
             * jnp.exp(-z_ref[jj][:, 0:1] * dl))
        if jj == 0:
            row = lax.broadcasted_iota(jnp.int32, h.shape, 0)
            h = jnp.where(backward & (m == 0) & (row == 0), 0.0, h)
        abs_sum = abs_sum + jnp.sum(jnp.abs(h), axis=0, keepdims=True)
        r_s[jj] = jnp.dot(t_ref[jj], h.astype(BF16), preferred_element_type=F32)
    o_ref[0] = jnp.swapaxes(r_s[...], 0, 1).astype(o_ref.dtype)
    col = pl.multiple_of(j * tn, tn)
    s_ref[seg, :, pl.ds(col, tn)] += abs_sum


def _hyena_filter_spectra_a(l, d, taf, f_w0, f_b0, f_w1, f_b1, f_w2, f_b2, f_freq, f_w3):
    emb, fw = f_w0.shape
    bands = (emb - 1) // 2
    half = FFT_N1 // 2
    n2 = l // half
    t = np.linspace(0.0, 1.0, l)[:, None]
    w = 2.0 * math.pi * np.arange(l)[:, None] / l
    f = np.linspace(1e-4, bands - 1, bands)[None, :]
    emb_pad = -(-emb // 8) * 8
    z = np.concatenate([t, np.cos(f * w), -np.sin(f * w), np.zeros((l, emb_pad - emb))], axis=-1)
    z = jnp.asarray(z.reshape(half, n2, emb_pad).transpose(1, 0, 2), F32)
    w0 = jnp.concatenate([f_w0, jnp.zeros((emb_pad - emb, fw), F32)], axis=0)
    max_decay = math.log(HY_DECAY_TARGET) / HY_MAX_DECAY_PCT
    min_decay = math.log(HY_DECAY_TARGET) / HY_MIN_DECAY_PCT
    dl = jnp.asarray(np.abs(np.linspace(min_decay, max_decay, d))[None, :], F32)
    nseg = 2 * HY_ORDER
    w3 = jnp.transpose(f_w3.reshape(fw, nseg, d), (1, 0, 2))
    w3_hi = w3.astype(BF16)
    w3_lo = (w3 - w3_hi.astype(F32)).astype(BF16)
    w3 = jnp.concatenate([w3_hi, w3_lo, w3_hi], axis=1)
    nb = min(FILT_NB, n2)
    assert nb == FILT_NB
    tn = min(256, d)
    vec = lambda a: a.reshape(1, fw)
    small = lambda shape: pl.BlockSpec(shape, lambda m, sg, j: (0,) * len(shape))
    return pl.pallas_call(
        _filter_dft_kernel,
        grid=(n2 // nb, nseg, d // tn),
        in_specs=[pl.BlockSpec((nb, half, emb_pad), lambda m, sg, j: (m, 0, 0)),
                  small((emb_pad, fw)), small((1, fw)), small((fw, fw)), small((1, fw)),
                  small((fw, fw)), small((1, fw)), small((1, fw)),
                  pl.BlockSpec((1, 3 * fw, tn), lambda m, sg, j: (sg, 0, j)),
                  pl.BlockSpec((1, tn), lambda m, sg, j: (0, j)),
                  pl.BlockSpec((nb, 2 * FILT_K1, half), lambda m, sg, j: (m, 0, 0))],
        out_specs=[pl.BlockSpec((1, 2 * FILT_K1, nb, tn), lambda m, sg, j: (sg, 0, m, j)),
                   pl.BlockSpec((nseg, 1, d), lambda m, sg, j: (0, 0, 0))],
        out_shape=[jax.ShapeDtypeStruct((nseg, 2 * FILT_K1, n2, d), BF16),
                   jax.ShapeDtypeStruct((nseg, 1, d), F32)],
        scratch_shapes=[pltpu.VMEM((nb, half, 3 * fw), BF16),
                        pltpu.VMEM((nb, 2 * FILT_K1, tn), F32)],
        compiler_params=_params("arbitrary", "arbitrary", "arbitrary"),
        name="hyena_filter_dft_stage_a",
    )(z, w0, vec(f_b0), f_w1, vec(f_b1), f_w2, vec(f_b2), vec(f_freq), w3, dl, taf)


def _dft_tables(l):
    n1 = FFT_N1
    n = 2 * l
    n2 = n // n1
    k1 = np.arange(n1)[:, None]
    f1 = np.exp(-2j * np.pi * k1 * np.arange(n1 // 2)[None, :] / n1)
    tw = np.exp(-2j * np.pi * np.arange(n2)[:, None] * np.arange(n1)[None, :] / n)
    f1r, f1i = jnp.asarray(f1.real, F32), jnp.asarray(f1.imag, F32)
    twr, twi = jnp.asarray(tw.real, F32)[:, :, None], jnp.asarray(tw.imag, F32)[:, :, None]
    fr = twr * f1r - twi * f1i
    fi = twr * f1i + twi * f1r
    ta = jnp.concatenate([jnp.concatenate([fr, -fi], axis=2),
                          jnp.concatenate([fi, fr], axis=2)], axis=1)
    fb = np.exp(-2j * np.pi * np.arange(n2)[:, None] * np.arange(n2)[None, :] / n2)
    tb = np.block([[fb.real, -fb.imag], [fb.imag, fb.real]])
    taf = jnp.concatenate([ta[:, 0:FILT_K1, 0:n1 // 2], ta[:, n1:n1 + FILT_K1, 0:n1 // 2]], axis=1)
    fm = np.exp(-2j * np.pi * (np.arange(n2)[:, None] + 1) * np.arange(n2)[None, :] / n2)
    tb_mirror = np.block([[fm.real, fm.imag], [fm.imag, -fm.real]])
    return (ta.astype(BF16), jnp.swapaxes(ta, 1, 2).astype(BF16), jnp.asarray(tb, BF16), jnp.asarray(tb.T, BF16),
            taf.astype(BF16), jnp.asarray(np.stack([tb, tb_mirror]), BF16))


FFT_NB = BF16_SUBLANES


def _fft_a_kernel(x_ref, t_ref, o_ref, r_s):
    planes = x_ref.shape[1]
    xt = [jnp.swapaxes(x_ref[0, c], 0, 1) for c in range(planes)]
    for jj in range(FFT_NB):
        x = jnp.concatenate([xt[c][jj] for c in range(planes)], axis=0).astype(BF16)
        r_s[jj] = jnp.dot(t_ref[jj], x, preferred_element_type=F32)
    o_ref[0] = jnp.swapaxes(r_s[...], 0, 1).astype(o_ref.dtype)


def _fft_a(x5, ta, *, n2, d, seg):
    s, p, half, _, _ = x5.shape
    kdim = p * half
    tn = min(256, d)
    sb, db = (seg * d) // tn, d // tn
    return pl.pallas_call(
        _fft_a_kernel,
        grid=(s, n2 // FFT_NB, db),
        in_specs=[pl.BlockSpec((1, p, half, FFT_NB, tn), lambda si, m, j: (si, 0, 0, m, sb + j)),
                  pl.BlockSpec((FFT_NB, 2 * FFT_N1, kdim), lambda si, m, j: (m, 0, 0))],
        out_specs=pl.BlockSpec((1, 2 * FFT_N1, FFT_NB, tn), lambda si, m, j: (si, 0, m, j)),
        out_shape=jax.ShapeDtypeStruct((s, 2 * FFT_N1, n2, d), BF16),
        scratch_shapes=[pltpu.VMEM((FFT_NB, 2 * FFT_N1, tn), F32)],
        compiler_params=_params("parallel", "parallel", "parallel"),
        name="dft_stage_a",
    )(x5, ta)


def _fft_b_kernel(a_ref, af_ref, ab_ref, sc_ref, tb_ref, tbf_ref, tbt_ref, o_ref, *, n2):
    z = jnp.dot(tb_ref[...], a_ref[...].reshape(2 * n2, -1), preferred_element_type=F32)
    tbf = tbf_ref[0]
    hf = jnp.dot(tbf, af_ref[0].reshape(2 * n2, -1), preferred_element_type=F32)
    hb = jnp.dot(tbf, ab_ref[0].reshape(2 * n2, -1), preferred_element_type=F32)
    sc = sc_ref[0]
    gr = (hf[0:n2] + hb[0:n2]) * sc
    gi = (hf[n2:2 * n2] - hb[n2:2 * n2]) * sc
    zr, zi = z[0:n2], z[n2:2 * n2]
    prod = jnp.concatenate([zr * gr - zi * gi, zr * gi + zi * gr], axis=0).astype(BF16)
    y = jnp.dot(tbt_ref[...], prod, preferred_element_type=F32)
    o_ref[...] = y.reshape(o_ref.shape).astype(o_ref.dtype)


def _fft_b(a, af, scale, tb, tbf, tbt, order, *, n2, d):
    tn = min(2048, d)
    blk = (2, 1, n2, tn)
    fblk = (1, 2, 1, n2, tn)
    mirrored = lambda k: k > FFT_N1 // 2
    fk = lambda k: jnp.where(mirrored(k), FFT_N1 - k, k)
    return pl.pallas_call(
        functools.partial(_fft_b_kernel, n2=n2),
        grid=(FFT_N1, d // tn),
        in_specs=[pl.BlockSpec(blk, lambda k, j: (0, k, 0, j)),
                  pl.BlockSpec(fblk, lambda k, j: (2 * order, 0, fk(k), 0, j)),
                  pl.BlockSpec(fblk, lambda k, j: (2 * order + 1, 0, fk(k), 0, j)),
                  pl.BlockSpec((1, 1, tn), lambda k, j: (order, 0, j)),
                  pl.BlockSpec((2 * n2, 2 * n2), lambda k, j: (0, 0)),
                  pl.BlockSpec((1, 2 * n2, 2 * n2), lambda k, j: (jnp.where(mirrored(k), 1, 0), 0, 0)),
                  pl.BlockSpec((2 * n2, 2 * n2), lambda k, j: (0, 0))],
        out_specs=pl.BlockSpec(blk, lambda k, j: (0, k, 0, j)),
        out_shape=jax.ShapeDtypeStruct((2, FFT_N1, n2, d), BF16),
        compiler_params=_params("parallel", "parallel"),
        name="dft_stage_b_spectrum_product",
    )(a, af, af, scale, tb, tbf, tbt)


def _fft_ainv_kernel(y_ref, t_ref, u_ref, gate_ref, d_ref, o_ref, r_s):
    half = o_ref.shape[1]
    yt = jnp.swapaxes(y_ref[...].astype(F32), 0, 1).astype(BF16)
    for jj in range(FFT_NB):
        x = jnp.dot(t_ref[jj], yt[jj], preferred_element_type=F32)
        for c in range(2):
            r_s[c, jj] = x[c * half:(c + 1) * half]
    for c in range(2):
        o_ref[c] = gate_ref[c] * (jnp.swapaxes(r_s[c], 0, 1) + d_ref[...] * u_ref[c])


def _fft_ainv(y3, tat, u4, gate4, dskip, *, n2, d, useg, gseg):
    b, half, _, _ = u4.shape
    tn = min(128, d)
    db = d // tn
    usb, gsb = (useg * d) // tn, (gseg * d) // tn
    return pl.pallas_call(
        _fft_ainv_kernel,
        grid=(n2 // FFT_NB, db),
        in_specs=[pl.BlockSpec((2 * FFT_N1, FFT_NB, tn), lambda m, j: (0, m, j)),
                  pl.BlockSpec((FFT_NB, FFT_N1, 2 * FFT_N1), lambda m, j: (m, 0, 0)),
                  pl.BlockSpec((b, half, FFT_NB, tn), lambda m, j: (0, 0, m, usb + j)),
                  pl.BlockSpec((b, half, FFT_NB, tn), lambda m, j: (0, 0, m, gsb + j)),
                  pl.BlockSpec((1, tn), lambda m, j: (0, j))],
        out_specs=pl.BlockSpec((b, half, FFT_NB, tn), lambda m, j: (0, 0, m, j)),
        out_shape=jax.ShapeDtypeStruct((b, half, n2, d), F32),
        scratch_shapes=[pltpu.VMEM((b, FFT_NB, half, tn), F32)],
        compiler_params=_params("parallel", "parallel"),
        name="dft_stage_a_inverse_gate",
    )(y3, tat, u4, gate4, dskip)


def _hyena_mixer(h, hy, l, d):
    (w_in, b_in, sconv_w, sconv_b, f_w0, f_b0, f_w1, f_b1, f_w2, f_b2, f_freq, f_w3, d_skip, w_out) = hy
    b = h.shape[0]
    assert b == 2, "the two batch rows ride the real / imaginary planes of one complex DFT"
    n1 = FFT_N1
    half = n1 // 2
    n2 = (2 * l) // n1
    assert half * n2 == l
    z3 = _mm_conv(_hy_in_kernel, h, [w_in.astype(BF16)],
                  [b_in.reshape(1, -1), sconv_w, sconv_b.reshape(1, -1)], 3 * d, F32, "hyena_in_proj_conv", nz=2)
    ta, tat, tb, tbt, taf, tbf = _dft_tables(l)
    nseg = 2 * HY_ORDER
    af, fsum = _hyena_filter_spectra_a(l, d, taf, f_w0, f_b0, f_w1, f_b1, f_w2, f_b2, f_freq, f_w3)
    af = af.reshape(nseg, 2, FILT_K1, n2, d)
    scale = 1.0 / ((fsum[0::2] + fsum[1::2]) * (2 * l))
    z4 = z3.reshape(b, half, n2, 3 * d)
    y, yseg = z4, 0
    for o in range(HY_ORDER):
        a = _fft_a(y[None], ta, n2=n2, d=d, seg=yseg)
        yb = _fft_b(a.reshape(2, n1, n2, d), af, scale, tb, tbf, tbt, o, n2=n2, d=d)
        y = _fft_ainv(yb.reshape(2 * n1, n2, d), tat, y, z4, d_skip[o].reshape(1, d),
                      n2=n2, d=d, useg=yseg, gseg=o + 1)
        yseg = 0
    return y.reshape(b, l, d), w_out.astype(BF16)


def _rope_tables(l):
    t = jnp.arange(l)
    row = (t // GRID_W).astype(F32)[:, None]
    col = (t % GRID_W).astype(F32)[:, None]
    width = 2 * HEAD_DIM
    cos, sin, avg, perm = [], [], [], []
    for dim in (A_HALF, HEAD_DIM):
        nf = dim // 4
        inv = jnp.asarray(ROPE_BASE ** (-np.arange(nf) / nf), F32)[None, :]
        ar, ac = row * inv, col * inv
        c = jnp.concatenate([jnp.cos(ar), jnp.cos(ar), jnp.cos(ac), jnp.cos(ac)], axis=1)
        s = jnp.concatenate([-jnp.sin(ar), jnp.sin(ar), -jnp.sin(ac), jnp.sin(ac)], axis=1)
        reps = width // dim
        cos.append(jnp.tile(c, (1, reps)))
        sin.append(jnp.tile(s, (1, reps)))
        lane = np.arange(width)
        avg.append((lane[:, None] // dim == lane[None, :] // dim) / dim)
        partner = np.where(lane % (2 * nf) < nf, lane + nf, lane - nf)
        perm.append((lane[:, None] == partner[None, :]).astype(np.float64))
    return (jnp.stack(cos), jnp.stack(sin), jnp.asarray(np.stack(avg), BF16), jnp.asarray(np.stack(perm), BF16))


def _attn_mixer(h, hc, lam_init, w_in, w_out, a_q_g, a_k_g, lq1, lk1, lq2, lk2, a_sub_g, b_q_g, b_k_g, b_sink):
    b, l, d = h.shape
    c = hc.shape[1]
    a_heads = d // (2 * HEAD_DIM)
    b_heads = d // (2 * HEAD_DIM)
    q_cols = (a_heads + b_heads) * HEAD_DIM
    in_cols = w_in.shape[1]
    lam = (jnp.exp(jnp.sum(lq1 * lk1)) - jnp.exp(jnp.sum(lq2 * lk2)) + lam_init).reshape(1, 1)
    sink = b_sink.reshape(B_KV_HEADS, b_heads // B_KV_HEADS)

    ones = lambda n: jnp.ones((n,), F32)
    tile = lambda v, n: jnp.tile(v, n)
    gain = jnp.concatenate([tile(a_q_g, 2 * a_heads), tile(b_q_g, b_heads), tile(a_k_g, 2 * a_heads),
                            ones(a_heads * HEAD_DIM), tile(b_k_g, B_KV_HEADS),
                            ones(B_KV_HEADS * HEAD_DIM)]).reshape(1, in_cols)
    scale = jnp.concatenate([jnp.full((a_heads * HEAD_DIM,), A_HALF ** -0.5 * math.log2(math.e), F32),
                             jnp.full((b_heads * HEAD_DIM,), HEAD_DIM ** -0.5, F32),
                             ones(in_cols - q_cols)]).reshape(1, in_cols)
    e = [0, a_heads // 2, (a_heads + b_heads) // 2, (2 * a_heads + b_heads) // 2,
         (3 * a_heads + b_heads) // 2, (3 * a_heads + b_heads) // 2 + 1, (3 * a_heads + b_heads) // 2 + 2]
    plain = ((e[3], e[4]), (e[5], e[6]))
    cos_tab, sin_tab, avg, perm = _rope_tables(l)

    def type_of(j0):
        def f(j):
            jj = j + j0
            is_a = (jj < e[1]) | ((jj >= e[2]) & (jj < e[3]))
            return jnp.where(is_a, 0, 1)
        return f

    w_bf = _col_tiles(w_in.astype(BF16), 2 * HEAD_DIM)
    p = _inproj(h.reshape(b * l, d), w_bf, gain, scale, cos_tab, sin_tab, avg, perm, plain=plain,
                type_of_block=type_of(0), col0=0, ncols=in_cols, l=l, rope=True).reshape(b, l, in_cols)
    kv0 = e[2]
    plain_c = tuple((lo - kv0, hi - kv0) for lo, hi in plain)
    pc = _inproj(hc.reshape(b * c, d), w_bf, gain, scale, cos_tab, sin_tab, avg, perm, plain=plain_c,
                 type_of_block=type_of(kv0), col0=kv0, ncols=in_cols - q_cols, l=c, rope=False
                 ).reshape(b, c, in_cols - q_cols)
    score_bound = (A_HALF ** 0.5 * math.log2(math.e)) * jnp.max(jnp.abs(a_q_g)) * jnp.max(jnp.abs(a_k_g))
    oa = _diffattn(lam, p, pc, a_sub_g, score_bound, heads=a_heads, l=l, c=c, out_scale=1.0 - lam_init)
    ob = _winattn(sink, p, pc, a_heads=a_heads, b_heads=b_heads, l=l, c=c)
    w_out_bf = w_out.astype(BF16)
    na = a_heads * HEAD_DIM
    return [oa, ob], [w_out_bf[:na], w_out_bf[na:]]


def kernel(x, c, ctx, c_ctx, ada_w, ada_b, norm1_g, norm2_g, attn_w_in, attn_w_out, a_q_g, a_k_g, a_lam_q1, a_lam_k1, a_lam_q2, a_lam_k2, a_sub_g, b_q_g, b_k_g, b_sink, hy_w_in, hy_b_in, hy_sconv_w, hy_sconv_b, hy_f_w0, hy_f_b0, hy_f_w1, hy_f_b1, hy_f_w2, hy_f_b2, hy_f_freq, hy_f_w3, hy_d, hy_w_out, ffn_w_gate, ffn_w_val, ffn_conv_w, ffn_conv_b, ffn_w_down):
    b, l, d = x.shape
    depth = ada_w.shape[0]
    cc = jnp.concatenate([c, c_ctx[None, :], jnp.zeros((8 - b - 1, d), F32)], axis=0)
    m = _ada(cc, ada_w, ada_b)
    xs = x
    for layer in range(depth):
        i = layer // 2
        lat = [m[layer, :b, k * d:(k + 1) * d].reshape(b, 1, d) for k in range(6)]
        sh1, sc1, g1, sh2, sc2, g2 = lat
        h = _normmod(xs, norm1_g[layer], sc1, sh1)
        if layer % 2 == 0:
            mc = [jnp.broadcast_to(m[layer, b, k * d:(k + 1) * d].reshape(1, 1, d), (b, 1, d)) for k in range(2)]
            hc = _normmod(ctx, norm1_g[layer], mc[1], mc[0])
            parts, ws = _attn_mixer(h, hc, 0.8 - 0.6 * math.exp(-0.3 * layer), attn_w_in[i], attn_w_out[i],
                                    a_q_g[i], a_k_g[i], a_lam_q1[i], a_lam_k1[i], a_lam_q2[i], a_lam_k2[i],
                                    a_sub_g[i], b_q_g[i], b_k_g[i], b_sink[i])
        else:
            hy = (hy_w_in[i], hy_b_in[i], hy_sconv_w[i], hy_sconv_b[i], hy_f_w0[i], hy_f_b0[i], hy_f_w1[i],
                  hy_f_b1[i], hy_f_w2[i], hy_f_b2[i], hy_f_freq[i], hy_f_w3[i], hy_d[i], hy_w_out[i])
            y, w_o = _hyena_mixer(h, hy, l, d)
            parts, ws = [y], [w_o]
        xs = _mm_resid(parts, ws, xs, g1)
        h2 = _normmod(xs, norm2_g[layer], sc2, sh2)
        d_ff = ffn_w_gate.shape[2]
        hid = _mm_conv(_ffn_up_kernel, h2, [ffn_w_gate[layer].astype(BF16), ffn_w_val[layer].astype(BF16)],
                       [ffn_conv_w[layer], ffn_conv_b[layer].reshape(1, -1)], d_ff, BF16, "ffn_up_conv_glu", nz=2)
        xs = _mm_resid([hid], [ffn_w_down[layer].astype(BF16)], xs, g2)
    return xs
```

```python
import functools
import math

import numpy as np
import jax
import jax.numpy as jnp
from jax import lax
from jax.experimental import pallas as pl
from jax.experimental.pallas import tpu as pltpu

F32 = jnp.float32
BF16 = jnp.bfloat16
HIGHEST = lax.Precision.HIGHEST

HEAD_DIM = 128
A_HALF = HEAD_DIM // 2
B_KV_HEADS = 2
GRID_W = 64
WINDOW = 128
ROPE_BASE = 10000.0
EPS = 1e-6
NEG = -1e30
HY_ORDER = 2
HY_DECAY_TARGET = 1e-2
HY_MAX_DECAY_PCT = 0.3
HY_MIN_DECAY_PCT = 1.5

BF16_SUBLANES = 16
FFT_N1 = 256
VMEM_LIMIT_BYTES = 56 * 1024 * 1024


def _params(*sem):
    return pltpu.CompilerParams(dimension_semantics=sem, vmem_limit_bytes=VMEM_LIMIT_BYTES)


def _col_tiles(w, tn):
    k, n = w.shape
    return jnp.transpose(w.reshape(k, n // tn, tn), (1, 0, 2))


def _ada_kernel(c_ref, w_ref, b_ref, o_ref):
    c = c_ref[...]
    s = c * (1.0 / (1.0 + jnp.exp(-c)))
    o_ref[0] = jnp.dot(s, w_ref[0], preferred_element_type=F32, precision=HIGHEST) + b_ref[0]


def _ada(cc, ada_w, ada_b):
    depth, d, n = ada_w.shape
    tn = 1024
    return pl.pallas_call(
        _ada_kernel,
        grid=(depth, n // tn),
        in_specs=[pl.BlockSpec((8, d), lambda l, j: (0, 0)),
                  pl.BlockSpec((1, d, tn), lambda l, j: (l, 0, j)),
                  pl.BlockSpec((1, 1, tn), lambda l, j: (l, 0, j))],
        out_specs=pl.BlockSpec((1, 8, tn), lambda l, j: (l, 0, j)),
        out_shape=jax.ShapeDtypeStruct((depth, 8, n), F32),
        compiler_params=_params("parallel", "parallel"),
        name="ada_modulation",
    )(cc, ada_w, ada_b.reshape(depth, 1, n))


NORM_ROWS = 32


def _normmod_kernel(x_ref, g_ref, sc_ref, sh_ref, o_ref):
    scale = g_ref[...] * (1.0 + sc_ref[0])
    shift = sh_ref[0]
    rows = min(NORM_ROWS, x_ref.shape[1])

    def slab(r, carry):
        rs = pl.ds(pl.multiple_of(r * rows, rows), rows)
        x = x_ref[0, rs, :]
        ms = jnp.mean(x * x, axis=-1, keepdims=True)
        o_ref[0, rs, :] = (x * lax.rsqrt(ms + EPS) * scale + shift).astype(o_ref.dtype)
        return carry

    lax.fori_loop(0, x_ref.shape[1] // rows, slab, 0, unroll=4)


def _normmod(x, g, sc, sh):
    b, l, d = x.shape
    tm = min(512, l)
    return pl.pallas_call(
        _normmod_kernel,
        grid=(b, l // tm),
        in_specs=[pl.BlockSpec((1, tm, d), lambda bi, i: (bi, i, 0)),
                  pl.BlockSpec((1, d), lambda bi, i: (0, 0)),
                  pl.BlockSpec((1, 1, d), lambda bi, i: (bi, 0, 0)),
                  pl.BlockSpec((1, 1, d), lambda bi, i: (bi, 0, 0))],
        out_specs=pl.BlockSpec((1, tm, d), lambda bi, i: (bi, i, 0)),
        out_shape=jax.ShapeDtypeStruct((b, l, d), BF16),
        compiler_params=_params("parallel", "parallel"),
        name="rmsnorm_modulate",
    )(x, g.reshape(1, d), sc, sh)


def _dot_split(x, m):
    hi = x.astype(BF16)
    lo = (x - hi.astype(F32)).astype(BF16)
    return jnp.dot(hi, m, preferred_element_type=F32) + jnp.dot(lo, m, preferred_element_type=F32)


def _inproj_kernel(h_ref, w_ref, gain_ref, scale_ref, cos_ref, sin_ref, avg_ref, perm_ref, o_ref, *, plain, rope):
    j = pl.program_id(1)
    z = jnp.dot(h_ref[...], w_ref[0], preferred_element_type=F32)
    is_plain = functools.reduce(jnp.logical_or, [(j >= lo) & (j < hi) for lo, hi in plain])

    @pl.when(is_plain)
    def _():
        o_ref[...] = z.astype(o_ref.dtype)

    @pl.when(jnp.logical_not(is_plain))
    def _():
        rows = min(256, z.shape[0])
        for r0 in range(0, z.shape[0], rows):
            rs = slice(r0, r0 + rows)
            zc = z[rs]
            ms = _dot_split(zc * zc, avg_ref[0])
            y = zc * lax.rsqrt(ms + EPS) * gain_ref[...]
            if rope:
                y = y * cos_ref[0, rs, :] + _dot_split(y, perm_ref[0]) * sin_ref[0, rs, :]
            o_ref[rs, :] = (y * scale_ref[...]).astype(o_ref.dtype)


def _inproj(h2d, w, gain, scale, cos_tab, sin_tab, avg, perm, *, plain, type_of_block, col0, ncols, l, rope):
    m, d = h2d.shape
    tn = 2 * HEAD_DIM
    tm = min(1024, l)
    pos_blocks = l // tm
    kern = functools.partial(_inproj_kernel, plain=plain, rope=rope)
    by_type = lambda i, j: (type_of_block(j), 0, 0)
    return pl.pallas_call(
        kern,
        grid=(m // tm, ncols // tn),
        in_specs=[pl.BlockSpec((tm, d), lambda i, j: (i, 0)),
                  pl.BlockSpec((1, d, tn), lambda i, j: (j + col0, 0, 0)),
                  pl.BlockSpec((1, tn), lambda i, j: (0, j + col0)),
                  pl.BlockSpec((1, tn), lambda i, j: (0, j + col0)),
                  pl.BlockSpec((1, tm, tn), lambda i, j: (type_of_block(j), i % pos_blocks, 0)),
                  pl.BlockSpec((1, tm, tn), lambda i, j: (type_of_block(j), i % pos_blocks, 0)),
                  pl.BlockSpec((1, tn, tn), by_type),
                  pl.BlockSpec((1, tn, tn), by_type)],
        out_specs=pl.BlockSpec((tm, tn), lambda i, j: (i, j)),
        out_shape=jax.ShapeDtypeStruct((m, ncols), BF16),
        compiler_params=_params("parallel", "arbitrary"),
        name="attn_in_projection",
    )(h2d, w, gain, scale, cos_tab, sin_tab, avg, perm)


def _diffattn_kernel(lam_ref, q_ref, kc_ref, vc_ref, k_ref, v_ref, g_ref, o_ref,
                     q2_s, m_s, l_s, acc_s, *, tq, nkv, out_scale):
    kv = pl.program_id(3)

    def process(k, v):
        s = lax.dot_general(q2_s[...], k, (((1,), (1,)), ((), ())), preferred_element_type=F32)
        m_prev = m_s[...]
        m_new = jnp.maximum(m_prev, jnp.max(s, axis=-1, keepdims=True))
        alpha = jnp.exp2(m_prev - m_new)
        p = jnp.exp2(s - m_new)
        l_s[...] = alpha * l_s[...] + jnp.sum(p, axis=-1, keepdims=True)
        acc_s[...] = alpha * acc_s[...] + jnp.dot(p.astype(BF16), v, preferred_element_type=F32)
        m_s[...] = m_new

    @pl.when(kv == 0)
    def _():
        q = q_ref[0]
        lane = lax.broadcasted_iota(jnp.int32, q.shape, 1)
        zero = jnp.zeros_like(q)
        q2_s[0:tq, :] = jnp.where(lane < A_HALF, q, zero)
        q2_s[tq:2 * tq, :] = jnp.where(lane >= A_HALF, q, zero)
        m_s[...] = jnp.full(m_s.shape, -jnp.inf, F32)
        l_s[...] = jnp.zeros(l_s.shape, F32)
        acc_s[...] = jnp.zeros(acc_s.shape, F32)
        process(kc_ref[0], vc_ref[0])

    process(k_ref[0], v_ref[0])

    @pl.when(kv == nkv - 1)
    def _():
        o = acc_s[...] / l_s[...]
        d = o[0:tq] - lam_ref[0, 0] * o[tq:2 * tq]
        ms = jnp.mean(d * d, axis=-1, keepdims=True)
        o_ref[0] = (d * lax.rsqrt(ms + EPS) * g_ref[...] * out_scale).astype(o_ref.dtype)


def _diffattn_fast_kernel(lam_ref, q_ref, kc_ref, vct_ref, k_ref, vt_ref, g_ref, o_ref,
                          q2_s, l_s, acc_s, p_s, *, tq, tk, kc, nkv, out_scale):
    kv = pl.program_id(3)
    dn = (((1,), (1,)), ((), ()))

    def weights(k):
        n = k.shape[0]
        p = jnp.exp2(lax.dot_general(k, q2_s[...], dn, preferred_element_type=F32))
        return p.astype(BF16), jnp.sum(p.reshape(n // 8, 8, 2 * tq), axis=0)

    @pl.when(kv == 0)
    def _():
        q = q_ref[0]
        lane = lax.broadcasted_iota(jnp.int32, q.shape, 1)
        zero = jnp.zeros_like(q)
        q2_s[0:tq, :] = jnp.where(lane < A_HALF, q, zero)
        q2_s[tq:2 * tq, :] = jnp.where(lane >= A_HALF, q, zero)
        p, ls = weights(kc_ref[0])
        l_s[...] = ls
        acc_s[...] = jnp.dot(vct_ref[0, 0], p, preferred_element_type=F32)

    ls = l_s[...]
    for ci in range(tk // kc):
        p, lc = weights(k_ref[0, ci * kc:(ci + 1) * kc, :])
        p_s[ci * kc:(ci + 1) * kc, :] = p
        ls = ls + lc
    l_s[...] = ls
    acc_s[...] += jnp.dot(vt_ref[0, 0], p_s[...], preferred_element_type=F32)

    @pl.when(kv == nkv - 1)
    def _():
        l = jnp.sum(l_s[...], axis=0, keepdims=True)
        o = acc_s[...] / l
        d = o[:, 0:tq] - lam_ref[0, 0] * o[:, tq:2 * tq]
        ms = jnp.mean(d * d, axis=0, keepdims=True)
        y = d * lax.rsqrt(ms + EPS) * g_ref[...] * out_scale
        o_ref[0] = y.T.astype(o_ref.dtype)


def _diffattn_fast(lam, p, pc, sub_g, *, heads, l, c, out_scale):
    b = p.shape[0]
    tq = min(1024, l)
    tk = min(4096, l)
    kc = min(512, tk)
    nkv = l // tk
    k_blk = 2 * heads
    hd = heads * HEAD_DIM
    vt = jnp.transpose(p[:, :, 3 * hd:4 * hd].reshape(b, l, heads, HEAD_DIM), (0, 2, 3, 1))
    vct = jnp.transpose(pc[:, :, hd:2 * hd].reshape(b, c, heads, HEAD_DIM), (0, 2, 3, 1))
    kern = functools.partial(_diffattn_fast_kernel, tq=tq, tk=tk, kc=kc, nkv=nkv, out_scale=out_scale)
    return pl.pallas_call(
        kern,
        grid=(b, heads, l // tq, nkv),
        in_specs=[pl.BlockSpec(memory_space=pltpu.SMEM),
                  pl.BlockSpec((1, tq, HEAD_DIM), lambda bi, h, i, kv: (bi, i, h)),
                  pl.BlockSpec((1, c, HEAD_DIM), lambda bi, h, i, kv: (bi, 0, h)),
                  pl.BlockSpec((1, 1, HEAD_DIM, c), lambda bi, h, i, kv: (bi, h, 0, 0)),
                  pl.BlockSpec((1, tk, HEAD_DIM), lambda bi, h, i, kv: (bi, kv, k_blk + h)),
                  pl.BlockSpec((1, 1, HEAD_DIM, tk), lambda bi, h, i, kv: (bi, h, 0, kv)),
                  pl.BlockSpec((HEAD_DIM, 1), lambda bi, h, i, kv: (0, 0))],
        out_specs=pl.BlockSpec((1, tq, HEAD_DIM), lambda bi, h, i, kv: (bi, i, h)),
        out_shape=jax.ShapeDtypeStruct((b, l, hd), BF16),
        scratch_shapes=[pltpu.VMEM((2 * tq, HEAD_DIM), BF16),
                        pltpu.VMEM((8, 2 * tq), F32),
                        pltpu.VMEM((HEAD_DIM, 2 * tq), F32),
                        pltpu.VMEM((tk, 2 * tq), BF16)],
        compiler_params=_params("parallel", "parallel", "parallel", "arbitrary"),
        name="diff_attention_bounded",
    )(lam, p, pc, vct, p, vt, sub_g.reshape(HEAD_DIM, 1))


SCORE_BOUND_LOG2 = 60.0


def _diffattn(lam, p, pc, sub_g, score_bound, *, heads, l, c, out_scale):
    kw = dict(heads=heads, l=l, c=c, out_scale=out_scale)
    return lax.cond(score_bound < SCORE_BOUND_LOG2,
                    lambda *a: _diffattn_fast(*a, **kw), lambda *a: _diffattn_safe(*a, **kw),
                    lam, p, pc, sub_g)


def _diffattn_safe(lam, p, pc, sub_g, *, heads, l, c, out_scale):
    b = p.shape[0]
    tq = min(256, l)
    tk = min(512, l)
    nkv = l // tk
    q_blk = 0
    k_blk = 2 * heads
    v_blk = 3 * heads
    kern = functools.partial(_diffattn_kernel, tq=tq, nkv=nkv, out_scale=out_scale)
    return pl.pallas_call(
        kern,
        grid=(b, heads, l // tq, nkv),
        in_specs=[pl.BlockSpec(memory_space=pltpu.SMEM),
                  pl.BlockSpec((1, tq, HEAD_DIM), lambda bi, h, i, kv: (bi, i, q_blk + h)),
                  pl.BlockSpec((1, c, HEAD_DIM), lambda bi, h, i, kv: (bi, 0, h)),
                  pl.BlockSpec((1, c, HEAD_DIM), lambda bi, h, i, kv: (bi, 0, heads + h)),
                  pl.BlockSpec((1, tk, HEAD_DIM), lambda bi, h, i, kv: (bi, kv, k_blk + h)),
                  pl.BlockSpec((1, tk, HEAD_DIM), lambda bi, h, i, kv: (bi, kv, v_blk + h)),
                  pl.BlockSpec((1, HEAD_DIM), lambda bi, h, i, kv: (0, 0))],
        out_specs=pl.BlockSpec((1, tq, HEAD_DIM), lambda bi, h, i, kv: (bi, i, h)),
        out_shape=jax.ShapeDtypeStruct((b, l, heads * HEAD_DIM), BF16),
        scratch_shapes=[pltpu.VMEM((2 * tq, HEAD_DIM), BF16),
                        pltpu.VMEM((2 * tq, 1), F32),
                        pltpu.VMEM((2 * tq, 1), F32),
                        pltpu.VMEM((2 * tq, HEAD_DIM), F32)],
        compiler_params=_params("parallel", "parallel", "parallel", "arbitrary"),
        name="diff_attention",
    )(lam, p, pc, pc, p, p, sub_g.reshape(1, HEAD_DIM))


def _winattn_kernel(sink_ref, q_ref, kp_ref, km_ref, kn_ref, vp_ref, vm_ref, vn_ref, kc_ref, vc_ref,
                    o_ref, *, tq, l, group):
    g = pl.program_id(1)
    i = pl.program_id(2)
    kband = jnp.concatenate([kp_ref[0], km_ref[0], kn_ref[0]], axis=0)
    vband = jnp.concatenate([vp_ref[0], vm_ref[0], vn_ref[0]], axis=0)
    nk = tq + 2 * WINDOW
    qpos = i * tq + lax.broadcasted_iota(jnp.int32, (tq, nk), 0)
    kpos = i * tq - WINDOW + lax.broadcasted_iota(jnp.int32, (tq, nk), 1)
    valid = (jnp.abs(kpos - qpos) <= WINDOW) & (kpos >= 0) & (kpos < l)
    kc = kc_ref[0]
    vc = vc_ref[0]
    dn = (((1,), (1,)), ((), ()))
    for r in range(group):
        sl = slice(r * HEAD_DIM, (r + 1) * HEAD_DIM)
        q = q_ref[0, :, sl]
        s_lat = jnp.where(valid, lax.dot_general(q, kband, dn, preferred_element_type=F32), NEG)
        s_ctx = lax.dot_general(q, kc, dn, preferred_element_type=F32)
        sk = sink_ref[g, r]
        m = jnp.maximum(jnp.maximum(jnp.max(s_lat, axis=-1, keepdims=True),
                                    jnp.max(s_ctx, axis=-1, keepdims=True)), sk)
        p_lat = jnp.exp(s_lat - m)
        p_ctx = jnp.exp(s_ctx - m)
        denom = (jnp.sum(p_lat, axis=-1, keepdims=True) + jnp.sum(p_ctx, axis=-1, keepdims=True)
                 + jnp.exp(sk - m))
        o = (jnp.dot(p_lat.astype(BF16), vband, preferred_element_type=F32)
             + jnp.dot(p_ctx.astype(BF16), vc, preferred_element_type=F32))
        o_ref[0, :, sl] = (o / denom).astype(o_ref.dtype)


def _winattn(sink, p, pc, *, a_heads, b_heads, l, c):
    b = p.shape[0]
    group = b_heads // B_KV_HEADS
    tq = min(256, l)
    wpb = tq // WINDOW
    nwb = l // WINDOW
    gw = group * HEAD_DIM
    q_blk = a_heads * HEAD_DIM // gw
    k_blk = (a_heads + b_heads) + 2 * a_heads
    v_blk = k_blk + B_KV_HEADS
    kc_blk = 2 * a_heads
    vc_blk = kc_blk + B_KV_HEADS
    kern = functools.partial(_winattn_kernel, tq=tq, l=l, group=group)

    def prev_map(col):
        return lambda bi, g, i: (bi, jnp.maximum(i * wpb - 1, 0), col + g)

    def main_map(col):
        return lambda bi, g, i: (bi, i, col + g)

    def next_map(col):
        return lambda bi, g, i: (bi, jnp.minimum((i + 1) * wpb, nwb - 1), col + g)

    return pl.pallas_call(
        kern,
        grid=(b, B_KV_HEADS, l // tq),
        in_specs=[pl.BlockSpec(memory_space=pltpu.SMEM),
                  pl.BlockSpec((1, tq, gw), lambda bi, g, i: (bi, i, q_blk + g)),
                  pl.BlockSpec((1, WINDOW, HEAD_DIM), prev_map(k_blk)),
                  pl.BlockSpec((1, tq, HEAD_DIM), main_map(k_blk)),
                  pl.BlockSpec((1, WINDOW, HEAD_DIM), next_map(k_blk)),
                  pl.BlockSpec((1, WINDOW, HEAD_DIM), prev_map(v_blk)),
                  pl.BlockSpec((1, tq, HEAD_DIM), main_map(v_blk)),
                  pl.BlockSpec((1, WINDOW, HEAD_DIM), next_map(v_blk)),
                  pl.BlockSpec((1, c, HEAD_DIM), lambda bi, g, i: (bi, 0, kc_blk + g)),
                  pl.BlockSpec((1, c, HEAD_DIM), lambda bi, g, i: (bi, 0, vc_blk + g))],
        out_specs=pl.BlockSpec((1, tq, gw), lambda bi, g, i: (bi, i, g)),
        out_shape=jax.ShapeDtypeStruct((b, l, b_heads * HEAD_DIM), BF16),
        compiler_params=_params("parallel", "parallel", "parallel"),
        name="window_attention",
    )(sink, p, p, p, p, p, p, p, pc, pc)


def _mm_resid_kernel(*refs, nparts):
    a_refs = refs[0:nparts]
    w_refs = refs[nparts:2 * nparts]
    res_ref, gate_ref, o_ref = refs[2 * nparts:]
    y = jnp.dot(a_refs[0][0].astype(BF16), w_refs[0][0], preferred_element_type=F32)
    for a_ref, w_ref in zip(a_refs[1:], w_refs[1:]):
        y = y + jnp.dot(a_ref[0].astype(BF16), w_ref[0], preferred_element_type=F32)
    o_ref[0] = res_ref[0] + gate_ref[0] * y


def _mm_resid(parts, ws, res, gate):
    b, l, n = res.shape
    tm = min(1024, l)
    tn = min(512, n)
    nparts = len(parts)
    ws = [_col_tiles(w, tn) for w in ws]
    in_specs = ([pl.BlockSpec((1, tm, a.shape[2]), lambda bi, i, j: (bi, i, 0)) for a in parts]
                + [pl.BlockSpec((1, w.shape[1], tn), lambda bi, i, j: (j, 0, 0)) for w in ws]
                + [pl.BlockSpec((1, tm, tn), lambda bi, i, j: (bi, i, j)),
                   pl.BlockSpec((1, 1, tn), lambda bi, i, j: (bi, 0, j))])
    return pl.pallas_call(
        functools.partial(_mm_resid_kernel, nparts=nparts),
        grid=(b, l // tm, n // tn),
        in_specs=in_specs,
        out_specs=pl.BlockSpec((1, tm, tn), lambda bi, i, j: (bi, i, j)),
        out_shape=jax.ShapeDtypeStruct((b, l, n), F32),
        compiler_params=_params("parallel", "parallel", "arbitrary"),
        name="out_projection_gated_residual",
    )(*parts, *ws, res, gate)


HALO = BF16_SUBLANES


def _fill_rows(hbuf, hp_ref, hm_ref, hn_ref, tm):
    hbuf[0:HALO, :] = hp_ref[0]
    hbuf[HALO:HALO + tm, :] = hm_ref[0]
    hbuf[HALO + tm:2 * HALO + tm, :] = hn_ref[0]


def _conv3(zbuf, cw_ref, cb_ref, tm, first, last):
    lo = slice(HALO - 1, HALO)
    hi = slice(HALO + tm, HALO + tm + 1)
    zbuf[lo, :] = jnp.where(first, 0.0, zbuf[lo, :])
    zbuf[hi, :] = jnp.where(last, 0.0, zbuf[hi, :])
    z = zbuf[...]
    rows = z.shape[0]
    g = (z * cw_ref[1:2, :] + pltpu.roll(z * cw_ref[0:1, :], 1, 0) + pltpu.roll(z * cw_ref[2:3, :], rows - 1, 0))
    return g[HALO:HALO + tm] + cb_ref[...]


def _gelu_tanh(x):
    return 0.5 * x * (1.0 + jnp.tanh(math.sqrt(2.0 / math.pi) * (x + 0.044715 * (x * x * x))))


def _ffn_up_kernel(hp_ref, hm_ref, hn_ref, wg_ref, wv_ref, cw_ref, cb_ref, o_ref, hbuf, zbuf, zbuf2, *, tm, nt):
    i = pl.program_id(1)
    j = pl.program_id(2)

    @pl.when(j == 0)
    def _():
        _fill_rows(hbuf, hp_ref, hm_ref, hn_ref, tm)

    hw = zbuf.shape[1]
    for hh, zb in enumerate((zbuf, zbuf2)):
        cs = slice(hh * hw, (hh + 1) * hw)
        zb[...] = jnp.dot(hbuf[...], wg_ref[0, :, cs], preferred_element_type=F32)
        g = _conv3(zb, cw_ref.at[:, cs], cb_ref.at[:, cs], tm, i == 0, i == nt - 1)
        v = jnp.dot(hbuf[HALO:HALO + tm, :], wv_ref[0, :, cs], preferred_element_type=F32)
        o_ref[0, :, cs] = (_gelu_tanh(g) * v).astype(o_ref.dtype)


def _hy_in_kernel(hp_ref, hm_ref, hn_ref, w_ref, b_ref, cw_ref, cb_ref, o_ref, hbuf, zbuf, zbuf2, *, tm, nt):
    i = pl.program_id(1)
    j = pl.program_id(2)

    @pl.when(j == 0)
    def _():
        _fill_rows(hbuf, hp_ref, hm_ref, hn_ref, tm)

    hw = zbuf.shape[1]
    for hh, zb in enumerate((zbuf, zbuf2)):
        cs = slice(hh * hw, (hh + 1) * hw)
        zb[...] = jnp.dot(hbuf[...], w_ref[0, :, cs], preferred_element_type=F32) + b_ref[:, cs]
        o_ref[0, :, cs] = _conv3(zb, cw_ref.at[:, cs], cb_ref.at[:, cs], tm, i == 0, i == nt - 1
                                 ).astype(o_ref.dtype)


def _mm_conv(kernel, h, ws, vecs, n, out_dtype, name, nz=1):
    b, l, k = h.shape
    tm = min(512, l)
    tn = min(512, n)
    nt = l // tm
    hpb = tm // HALO
    nhb = l // HALO
    ws = [w[None] for w in ws]
    in_specs = ([pl.BlockSpec((1, HALO, k), lambda bi, i, j: (bi, jnp.maximum(i * hpb - 1, 0), 0)),
                 pl.BlockSpec((1, tm, k), lambda bi, i, j: (bi, i, 0)),
                 pl.BlockSpec((1, HALO, k), lambda bi, i, j: (bi, jnp.minimum((i + 1) * hpb, nhb - 1), 0))]
                + [pl.BlockSpec((1, k, tn), lambda bi, i, j: (0, 0, j)) for _ in ws]
                + [pl.BlockSpec((v.shape[0], tn), lambda bi, i, j: (0, j)) for v in vecs])
    return pl.pallas_call(
        functools.partial(kernel, tm=tm, nt=nt),
        grid=(b, nt, n // tn),
        in_specs=in_specs,
        out_specs=pl.BlockSpec((1, tm, tn), lambda bi, i, j: (bi, i, j)),
        out_shape=jax.ShapeDtypeStruct((b, l, n), out_dtype),
        scratch_shapes=[pltpu.VMEM((tm + 2 * HALO, k), BF16)] + [pltpu.VMEM((tm + 2 * HALO, tn // nz), F32)] * nz,
        compiler_params=_params("parallel", "parallel", "arbitrary"),
        name=name,
    )(h, h, h, *ws, *vecs)


FILT_NB = BF16_SUBLANES
FILT_K1 = FFT_N1 // 2 + 8


def _filter_dft_kernel(z_ref, w0_ref, b0_ref, w1_ref, b1_ref, w2_ref, b2_ref, fr_ref, w3_ref, dl_ref, t_ref,
                       o_ref, s_ref, a3_s, r_s):
    m, seg, j = pl.program_id(0), pl.program_id(1), pl.program_id(2)
    tn = o_ref.shape[3]
    fr = fr_ref[...]

    def layer(a, w_ref, b_ref):
        return jnp.sin(fr * (jnp.dot(a, w_ref[...], preferred_element_type=F32, precision=HIGHEST)
                             + b_ref[...]))

    @pl.when((seg == 0) & (j == 0))
    def _():
        for jj in range(FILT_NB):
            a = layer(layer(layer(z_ref[jj], w0_ref, b0_ref), w1_ref, b1_ref), w2_ref, b2_ref)
            a_hi = a.astype(BF16)
            a_lo = (a - a_hi.astype(F32)).astype(BF16)
            a3_s[jj] = jnp.concatenate([a_hi, a_hi, a_lo], axis=1)

    @pl.when((m == 0) & (seg == 0) & (j == 0))
    def _():
        s_ref[...] = jnp.zeros(s_ref.shape, F32)

    dl = dl_ref[...]
    backward = seg % 2 == 1
    abs_sum = jnp.zeros((1, tn), F32)
    for jj in range(FILT_NB):
        h = (jnp.dot(a3_s[jj], w3_ref[0], preferred_element_type=F32)
             * jnp.exp(-z_ref[jj][:, 0:1] * dl))
        if jj == 0:
            row = lax.broadcasted_iota(jnp.int32, h.shape, 0)
            h = jnp.where(backward & (m == 0) & (row == 0), 0.0, h)
        abs_sum = abs_sum + jnp.sum(jnp.abs(h), axis=0, keepdims=True)
        r_s[jj] = jnp.dot(t_ref[jj], h.astype(BF16), preferred_element_type=F32)
    o_ref[0] = jnp.swapaxes(r_s[...], 0, 1).astype(o_ref.dtype)
    col = pl.multiple_of(j * tn, tn)
    s_ref[seg, :, pl.ds(col, tn)] += abs_sum


def _hyena_filter_spectra_a(l, d, taf, f_w0, f_b0, f_w1, f_b1, f_w2, f_b2, f_freq, f_w3):
    emb, fw = f_w0.shape
    bands = (emb - 1) // 2
    half = FFT_N1 // 2
    n2 = l // half
    t = np.linspace(0.0, 1.0, l)[:, None]
    w = 2.0 * math.pi * np.arange(l)[:, None] / l
    f = np.linspace(1e-4, bands - 1, bands)[None, :]
    emb_pad = -(-emb // 8) * 8
    z = np.concatenate([t, np.cos(f * w), -np.sin(f * w), np.zeros((l, emb_pad - emb))], axis=-1)
    z = jnp.asarray(z.reshape(half, n2, emb_pad).transpose(1, 0, 2), F32)
    w0 = jnp.concatenate([f_w0, jnp.zeros((emb_pad - emb, fw), F32)], axis=0)
    max_decay = math.log(HY_DECAY_TARGET) / HY_MAX_DECAY_PCT
    min_decay = math.log(HY_DECAY_TARGET) / HY_MIN_DECAY_PCT
    dl = jnp.asarray(np.abs(np.linspace(min_decay, max_decay, d))[None, :], F32)
    nseg = 2 * HY_ORDER
    w3 = jnp.transpose(f_w3.reshape(fw, nseg, d), (1, 0, 2))
    w3_hi = w3.astype(BF16)
    w3_lo = (w3 - w3_hi.astype(F32)).astype(BF16)
    w3 = jnp.concatenate([w3_hi, w3_lo, w3_hi], axis=1)
    nb = min(FILT_NB, n2)
    assert nb == FILT_NB
    tn = min(256, d)
    vec = lambda a: a.reshape(1, fw)
    small = lambda shape: pl.BlockSpec(shape, lambda m, sg, j: (0,) * len(shape))
    return pl.pallas_call(
        _filter_dft_kernel,
        grid=(n2 // nb, nseg, d // tn),
        in_specs=[pl.BlockSpec((nb, half, emb_pad), lambda m, sg, j: (m, 0, 0)),
                  small((emb_pad, fw)), small((1, fw)), small((fw, fw)), small((1, fw)),
                  small((fw, fw)), small((1, fw)), small((1, fw)),
                  pl.BlockSpec((1, 3 * fw, tn), lambda m, sg, j: (sg, 0, j)),
                  pl.BlockSpec((1, tn), lambda m, sg, j: (0, j)),
                  pl.BlockSpec((nb, 2 * FILT_K1, half), lambda m, sg, j: (m, 0, 0))],
        out_specs=[pl.BlockSpec((1, 2 * FILT_K1, nb, tn), lambda m, sg, j: (sg, 0, m, j)),
                   pl.BlockSpec((nseg, 1, d), lambda m, sg, j: (0, 0, 0))],
        out_shape=[jax.ShapeDtypeStruct((nseg, 2 * FILT_K1, n2, d), BF16),
                   jax.ShapeDtypeStruct((nseg, 1, d), F32)],
        scratch_shapes=[pltpu.VMEM((nb, half, 3 * fw), BF16),
                        pltpu.VMEM((nb, 2 * FILT_K1, tn), F32)],
        compiler_params=_params("arbitrary", "arbitrary", "arbitrary"),
        name="hyena_filter_dft_stage_a",
    )(z, w0, vec(f_b0), f_w1, vec(f_b1), f_w2, vec(f_b2), vec(f_freq), w3, dl, taf)


def _dft_tables(l):
    n1 = FFT_N1
    n = 2 * l
    n2 = n // n1
    k1 = np.arange(n1)[:, None]
    f1 = np.exp(-2j * np.pi * k1 * np.arange(n1 // 2)[None, :] / n1)
    tw = np.exp(-2j * np.pi * np.arange(n2)[:, None] * np.arange(n1)[None, :] / n)
    f1r, f1i = jnp.asarray(f1.real, F32), jnp.asarray(f1.imag, F32)
    twr, twi = jnp.asarray(tw.real, F32)[:, :, None], jnp.asarray(tw.imag, F32)[:, :, None]
    fr = twr * f1r - twi * f1i
    fi = twr * f1i + twi * f1r
    ta = jnp.concatenate([jnp.concatenate([fr, -fi], axis=2),
                          jnp.concatenate([fi, fr], axis=2)], axis=1)
    fb = np.exp(-2j * np.pi * np.arange(n2)[:, None] * np.arange(n2)[None, :] / n2)
    tb = np.block([[fb.real, -fb.imag], [fb.imag, fb.real]])
    taf = jnp.concatenate([ta[:, 0:FILT_K1, 0:n1 // 2], ta[:, n1:n1 + FILT_K1, 0:n1 // 2]], axis=1)
    fm = np.exp(-2j * np.pi * (np.arange(n2)[:, None] + 1) * np.arange(n2)[None, :] / n2)
    tb_mirror = np.block([[fm.real, fm.imag], [fm.imag, -fm.real]])
    return (ta.astype(BF16), jnp.swapaxes(ta, 1, 2).astype(BF16), jnp.asarray(tb, BF16), jnp.asarray(tb.T, BF16),
            taf.astype(BF16), jnp.asarray(np.stack([tb, tb_mirror]), BF16))


FFT_NB = BF16_SUBLANES


def _fft_a_kernel(x_ref, t_ref, o_ref, r_s):
    planes = x_ref.shape[1]
    xt = [jnp.swapaxes(x_ref[0, c], 0, 1) for c in range(planes)]
    for jj in range(FFT_NB):
        x = jnp.concatenate([xt[c][jj] for c in range(planes)], axis=0).astype(BF16)
        r_s[jj] = jnp.dot(t_ref[jj], x, preferred_element_type=F32)
    o_ref[0] = jnp.swapaxes(r_s[...], 0, 1).astype(o_ref.dtype)


def _fft_a(x5, ta, *, n2, d, seg):
    s, p, half, _, _ = x5.shape
    kdim = p * half
    tn = min(256, d)
    sb, db = (seg * d) // tn, d // tn
    return pl.pallas_call(
        _fft_a_kernel,
        grid=(s, n2 // FFT_NB, db),
        in_specs=[pl.BlockSpec((1, p, half, FFT_NB, tn), lambda si, m, j: (si, 0, 0, m, sb + j)),
                  pl.BlockSpec((FFT_NB, 2 * FFT_N1, kdim), lambda si, m, j: (m, 0, 0))],
        out_specs=pl.BlockSpec((1, 2 * FFT_N1, FFT_NB, tn), lambda si, m, j: (si, 0, m, j)),
        out_shape=jax.ShapeDtypeStruct((s, 2 * FFT_N1, n2, d), BF16),
        scratch_shapes=[pltpu.VMEM((FFT_NB, 2 * FFT_N1, tn), F32)],
        compiler_params=_params("parallel", "parallel", "parallel"),
        name="dft_stage_a",
    )(x5, ta)


def _fft_b_kernel(a_ref, af_ref, ab_ref, sc_ref, tb_ref, tbf_ref, tbt_ref, o_ref, *, n2):
    z = jnp.dot(tb_ref[...], a_ref[...].reshape(2 * n2, -1), preferred_element_type=F32)
    tbf = tbf_ref[0]
    hf = jnp.dot(tbf, af_ref[0].reshape(2 * n2, -1), preferred_element_type=F32)
    hb = jnp.dot(tbf, ab_ref[0].reshape(2 * n2, -1), preferred_element_type=F32)
    sc = sc_ref[0]
    gr = (hf[0:n2] + hb[0:n2]) * sc
    gi = (hf[n2:2 * n2] - hb[n2:2 * n2]) * sc
    zr, zi = z[0:n2], z[n2:2 * n2]
    prod = jnp.concatenate([zr * gr - zi * gi, zr * gi + zi * gr], axis=0).astype(BF16)
    y = jnp.dot(tbt_ref[...], prod, preferred_element_type=F32)
    o_ref[...] = y.reshape(o_ref.shape).astype(o_ref.dtype)


def _fft_b(a, af, scale, tb, tbf, tbt, order, *, n2, d):
    tn = min(2048, d)
    blk = (2, 1, n2, tn)
    fblk = (1, 2, 1, n2, tn)
    mirrored = lambda k: k > FFT_N1 // 2
    fk = lambda k: jnp.where(mirrored(k), FFT_N1 - k, k)
    return pl.pallas_call(
        functools.partial(_fft_b_kernel, n2=n2),
        grid=(FFT_N1, d // tn),
        in_specs=[pl.BlockSpec(blk, lambda k, j: (0, k, 0, j)),
                  pl.BlockSpec(fblk, lambda k, j: (2 * order, 0, fk(k), 0, j)),
                  pl.BlockSpec(fblk, lambda k, j: (2 * order + 1, 0, fk(k), 0, j)),
                  pl.BlockSpec((1, 1, tn), lambda k, j: (order, 0, j)),
                  pl.BlockSpec((2 * n2, 2 * n2), lambda k, j: (0, 0)),
                  pl.BlockSpec((1, 2 * n2, 2 * n2), lambda k, j: (jnp.where(mirrored(k), 1, 0), 0, 0)),
                  pl.BlockSpec((2 * n2, 2 * n2), lambda k, j: (0, 0))],
        out_specs=pl.BlockSpec(blk, lambda k, j: (0, k, 0, j)),
        out_shape=jax.ShapeDtypeStruct((2, FFT_N1, n2, d), BF16),
        compiler_params=_params("parallel", "parallel"),
        name="dft_stage_b_spectrum_product",
    )(a, af, af, scale, tb, tbf, tbt)


def _fft_ainv_kernel(y_ref, t_ref, u_ref, gate_ref, d_ref, o_ref, r_s):
    half = o_ref.shape[1]
    yt = jnp.swapaxes(y_ref[...].astype(F32), 0, 1).astype(BF16)
    for jj in range(FFT_NB):
        x = jnp.dot(t_ref[jj], yt[jj], preferred_element_type=F32)
        for c in range(2):
            r_s[c, jj] = x[c * half:(c + 1) * half]
    for c in range(2):
        o_ref[c] = gate_ref[c] * (jnp.swapaxes(r_s[c], 0, 1) + d_ref[...] * u_ref[c])


def _fft_ainv(y3, tat, u4, gate4, dskip, *, n2, d, useg, gseg):
    b, half, _, _ = u4.shape
    tn = min(128, d)
    db = d // tn
    usb, gsb = (useg * d) // tn, (gseg * d) // tn
    return pl.pallas_call(
        _fft_ainv_kernel,
        grid=(n2 // FFT_NB, db),
        in_specs=[pl.BlockSpec((2 * FFT_N1, FFT_NB, tn), lambda m, j: (0, m, j)),
                  pl.BlockSpec((FFT_NB, FFT_N1, 2 * FFT_N1), lambda m, j: (m, 0, 0)),
                  pl.BlockSpec((b, half, FFT_NB, tn), lambda m, j: (0, 0, m, usb + j)),
                  pl.BlockSpec((b, half, FFT_NB, tn), lambda m, j: (0, 0, m, gsb + j)),
                  pl.BlockSpec((1, tn), lambda m, j: (0, j))],
        out_specs=pl.BlockSpec((b, half, FFT_NB, tn), lambda m, j: (0, 0, m, j)),
        out_shape=jax.ShapeDtypeStruct((b, half, n2, d), F32),
        scratch_shapes=[pltpu.VMEM((b, FFT_NB, half, tn), F32)],
        compiler_params=_params("parallel", "parallel"),
        name="dft_stage_a_inverse_gate",
    )(y3, tat, u4, gate4, dskip)


def _hyena_mixer(h, hy, l, d):
    (w_in, b_in, sconv_w, sconv_b, f_w0, f_b0, f_w1, f_b1, f_w2, f_b2, f_freq, f_w3, d_skip, w_out) = hy
    b = h.shape[0]
    assert b == 2, "the two batch rows ride the real / imaginary planes of one complex DFT"
    n1 = FFT_N1
    half = n1 // 2
    n2 = (2 * l) // n1
    assert half * n2 == l
    z3 = _mm_conv(_hy_in_kernel, h, [w_in.astype(BF16)],
                  [b_in.reshape(1, -1), sconv_w, sconv_b.reshape(1, -1)], 3 * d, F32, "hyena_in_proj_conv", nz=2)
    ta, tat, tb, tbt, taf, tbf = _dft_tables(l)
    nseg = 2 * HY_ORDER
    af, fsum = _hyena_filter_spectra_a(l, d, taf, f_w0, f_b0, f_w1, f_b1, f_w2, f_b2, f_freq, f_w3)
    af = af.reshape(nseg, 2, FILT_K1, n2, d)
    scale = 1.0 / ((fsum[0::2] + fsum[1::2]) * (2 * l))
    z4 = z3.reshape(b, half, n2, 3 * d)
    y, yseg = z4, 0
    for o in range(HY_ORDER):
        a = _fft_a(y[None], ta, n2=n2, d=d, seg=yseg)
        yb = _fft_b(a.reshape(2, n1, n2, d), af, scale, tb, tbf, tbt, o, n2=n2, d=d)
        y = _fft_ainv(yb.reshape(2 * n1, n2, d), tat, y, z4, d_skip[o].reshape(1, d),
                      n2=n2, d=d, useg=yseg, gseg=o + 1)
        yseg = 0
    return y.reshape(b, l, d), w_out.astype(BF16)


def _rope_tables(l):
    t = jnp.arange(l)
    row = (t // GRID_W).astype(F32)[:, None]
    col = (t % GRID_W).astype(F32)[:, None]
    width = 2 * HEAD_DIM
    cos, sin, avg, perm = [], [], [], []
    for dim in (A_HALF, HEAD_DIM):
        nf = dim // 4
        inv = jnp.asarray(ROPE_BASE ** (-np.arange(nf) / nf), F32)[None, :]
        ar, ac = row * inv, col * inv
        c = jnp.concatenate([jnp.cos(ar), jnp.cos(ar), jnp.cos(ac), jnp.cos(ac)], axis=1)
        s = jnp.concatenate([-jnp.sin(ar), jnp.sin(ar), -jnp.sin(ac), jnp.sin(ac)], axis=1)
        reps = width // dim
        cos.append(jnp.tile(c, (1, reps)))
        sin.append(jnp.tile(s, (1, reps)))
        lane = np.arange(width)
        avg.append((lane[:, None] // dim == lane[None, :] // dim) / dim)
        partner = np.where(lane % (2 * nf) < nf, lane + nf, lane - nf)
        perm.append((lane[:, None] == partner[None, :]).astype(np.float64))
    return (jnp.stack(cos), jnp.stack(sin), jnp.asarray(np.stack(avg), BF16), jnp.asarray(np.stack(perm), BF16))


def _attn_mixer(h, hc, lam_init, w_in, w_out, a_q_g, a_k_g, lq1, lk1, lq2, lk2, a_sub_g, b_q_g, b_k_g, b_sink):
    b, l, d = h.shape
    c = hc.shape[1]
    a_heads = d // (2 * HEAD_DIM)
    b_heads = d // (2 * HEAD_DIM)
    q_cols = (a_heads + b_heads) * HEAD_DIM
    in_cols = w_in.shape[1]
    lam = (jnp.exp(jnp.sum(lq1 * lk1)) - jnp.exp(jnp.sum(lq2 * lk2)) + lam_init).reshape(1, 1)
    sink = b_sink.reshape(B_KV_HEADS, b_heads // B_KV_HEADS)

    ones = lambda n: jnp.ones((n,), F32)
    tile = lambda v, n: jnp.tile(v, n)
    gain = jnp.concatenate([tile(a_q_g, 2 * a_heads), tile(b_q_g, b_heads), tile(a_k_g, 2 * a_heads),
                            ones(a_heads * HEAD_DIM), tile(b_k_g, B_KV_HEADS),
                            ones(B_KV_HEADS * HEAD_DIM)]).reshape(1, in_cols)
    scale = jnp.concatenate([jnp.full((a_heads * HEAD_DIM,), A_HALF ** -0.5 * math.log2(math.e), F32),
                             jnp.full((b_heads * HEAD_DIM,), HEAD_DIM ** -0.5, F32),
                             ones(in_cols - q_cols)]).reshape(1, in_cols)
    e = [0, a_heads // 2, (a_heads + b_heads) // 2, (2 * a_heads + b_heads) // 2,
         (3 * a_heads + b_heads) // 2, (3 * a_heads + b_heads) // 2 + 1, (3 * a_heads + b_heads) // 2 + 2]
    plain = ((e[3], e[4]), (e[5], e[6]))
    cos_tab, sin_tab, avg, perm = _rope_tables(l)

    def type_of(j0):
        def f(j):
            jj = j + j0
            is_a = (jj < e[1]) | ((jj >= e[2]) & (jj < e[3]))
            return jnp.where(is_a, 0, 1)
        return f

    w_bf = _col_tiles(w_in.astype(BF16), 2 * HEAD_DIM)
    p = _inproj(h.reshape(b * l, d), w_bf, gain, scale, cos_tab, sin_tab, avg, perm, plain=plain,
                type_of_block=type_of(0), col0=0, ncols=in_cols, l=l, rope=True).reshape(b, l, in_cols)
    kv0 = e[2]
    plain_c = tuple((lo - kv0, hi - kv0) for lo, hi in plain)
    pc = _inproj(hc.reshape(b * c, d), w_bf, gain, scale, cos_tab, sin_tab, avg, perm, plain=plain_c,
                 type_of_block=type_of(kv0), col0=kv0, ncols=in_cols - q_cols, l=c, rope=False
                 ).reshape(b, c, in_cols - q_cols)
    score_bound = (A_HALF ** 0.5 * math.log2(math.e)) * jnp.max(jnp.abs(a_q_g)) * jnp.max(jnp.abs(a_k_g))
    oa = _diffattn(lam, p, pc, a_sub_g, score_bound, heads=a_heads, l=l, c=c, out_scale=1.0 - lam_init)
    ob = _winattn(sink, p, pc, a_heads=a_heads, b_heads=b_heads, l=l, c=c)
    w_out_bf = w_out.astype(BF16)
    na = a_heads * HEAD_DIM
    return [oa, ob], [w_out_bf[:na], w_out_bf[na:]]


def kernel(x, c, ctx, c_ctx, ada_w, ada_b, norm1_g, norm2_g, attn_w_in, attn_w_out, a_q_g, a_k_g, a_lam_q1, a_lam_k1, a_lam_q2, a_lam_k2, a_sub_g, b_q_g, b_k_g, b_sink, hy_w_in, hy_b_in, hy_sconv_w, hy_sconv_b, hy_f_w0, hy_f_b0, hy_f_w1, hy_f_b1, hy_f_w2, hy_f_b2, hy_f_freq, hy_f_w3, hy_d, hy_w_out, ffn_w_gate, ffn_w_val, ffn_conv_w, ffn_conv_b, ffn_w_down):
    b, l, d = x.shape
    depth = ada_w.shape[0]
    cc = jnp.concatenate([c, c_ctx[None, :], jnp.zeros((8 - b - 1, d), F32)], axis=0)
    m = _ada(cc, ada_w, ada_b)
    xs = x
    for layer in range(depth):
        i = layer // 2
        lat = [m[layer, :b, k * d:(k + 1) * d].reshape(b, 1, d) for k in range(6)]
        sh1, sc1, g1, sh2, sc2, g2 = lat
        h = _normmod(xs, norm1_g[layer], sc1, sh1)
        if layer % 2 == 0:
            mc = [jnp.broadcast_to(m[layer, b, k * d:(k + 1) * d].reshape(1, 1, d), (b, 1, d)) for k in range(2)]
            hc = _normmod(ctx, norm1_g[layer], mc[1], mc[0])
            parts, ws = _attn_mixer(h, hc, 0.8 - 0.6 * math.exp(-0.3 * layer), attn_w_in[i], attn_w_out[i],
                                    a_q_g[i], a_k_g[i], a_lam_q1[i], a_lam_k1[i], a_lam_q2[i], a_lam_k2[i],
                                    a_sub_g[i], b_q_g[i], b_k_g[i], b_sink[i])
        else:
            hy = (hy_w_in[i], hy_b_in[i], hy_sconv_w[i], hy_sconv_b[i], hy_f_w0[i], hy_f_b0[i], hy_f_w1[i],
                  hy_f_b1[i], hy_f_w2[i], hy_f_b2[i], hy_f_freq[i], hy_f_w3[i], hy_d[i], hy_w_out[i])
            y, w_o = _hyena_mixer(h, hy, l, d)
            parts, ws = [y], [w_o]
        xs = _mm_resid(parts, ws, xs, g1)
        h2 = _normmod(xs, norm2_g[layer], sc2, sh2)
        d_ff = ffn_w_gate.shape[2]
        hid = _mm_conv(_ffn_up_kernel, h2, [ffn_w_gate[layer].astype(BF16), ffn_w_val[layer].astype(BF16)],
                       [ffn_conv_w[layer], ffn_conv_b[layer].reshape(1, -1)], d_ff, BF16, "ffn_up_conv_glu", nz=2)
        xs = _mm_resid([hid], [ffn_w_down[layer].astype(BF16)], xs, g2)
    return xs
```

```python
import functools
import math

import numpy as np
import jax
import jax.numpy as jnp
from jax import lax
from jax.experimental import pallas as pl
from jax.experimental.pallas import tpu as pltpu

F32 = jnp.float32
BF16 = jnp.bfloat16
HIGHEST = lax.Precision.HIGHEST

HEAD_DIM = 128
A_HALF = HEAD_DIM // 2
B_KV_HEADS = 2
GRID_W = 64
WINDOW = 128
ROPE_BASE = 10000.0
EPS = 1e-6
NEG = -1e30
HY_ORDER = 2
HY_DECAY_TARGET = 1e-2
HY_MAX_DECAY_PCT = 0.3
HY_MIN_DECAY_PCT = 1.5

BF16_SUBLANES = 16
FFT_N1 = 256
VMEM_LIMIT_BYTES = 56 * 1024 * 1024


def _params(*sem):
    return pltpu.CompilerParams(dimension_semantics=sem, vmem_limit_bytes=VMEM_LIMIT_BYTES)


def _col_tiles(w, tn):
    k, n = w.shape
    return jnp.transpose(w.reshape(k, n // tn, tn), (1, 0, 2))


def _ada_kernel(c_ref, w_ref, b_ref, o_ref):
    c = c_ref[...]
    s = c * (1.0 / (1.0 + jnp.exp(-c)))
    o_ref[0] = jnp.dot(s, w_ref[0], preferred_element_type=F32, precision=HIGHEST) + b_ref[0]


def _ada(cc, ada_w, ada_b):
    depth, d, n = ada_w.shape
    tn = 1024
    return pl.pallas_call(
        _ada_kernel,
        grid=(depth, n // tn),
        in_specs=[pl.BlockSpec((8, d), lambda l, j: (0, 0)),
                  pl.BlockSpec((1, d, tn), lambda l, j: (l, 0, j)),
                  pl.BlockSpec((1, 1, tn), lambda l, j: (l, 0, j))],
        out_specs=pl.BlockSpec((1, 8, tn), lambda l, j: (l, 0, j)),
        out_shape=jax.ShapeDtypeStruct((depth, 8, n), F32),
        compiler_params=_params("parallel", "parallel"),
        name="ada_modulation",
    )(cc, ada_w, ada_b.reshape(depth, 1, n))


NORM_ROWS = 32


def _normmod_kernel(x_ref, g_ref, sc_ref, sh_ref, o_ref):
    scale = g_ref[...] * (1.0 + sc_ref[0])
    shift = sh_ref[0]
    rows = min(NORM_ROWS, x_ref.shape[1])

    def slab(r, carry):
        rs = pl.ds(pl.multiple_of(r * rows, rows), rows)
        x = x_ref[0, rs, :]
        ms = jnp.mean(x * x, axis=-1, keepdims=True)
        o_ref[0, rs, :] = (x * lax.rsqrt(ms + EPS) * scale + shift).astype(o_ref.dtype)
        return carry

    lax.fori_loop(0, x_ref.shape[1] // rows, slab, 0, unroll=4)


def _normmod(x, g, sc, sh):
    b, l, d = x.shape
    tm = min(512, l)
    return pl.pallas_call(
        _normmod_kernel,
        grid=(b, l // tm),
        in_specs=[pl.BlockSpec((1, tm, d), lambda bi, i: (bi, i, 0)),
                  pl.BlockSpec((1, d), lambda bi, i: (0, 0)),
                  pl.BlockSpec((1, 1, d), lambda bi, i: (bi, 0, 0)),
                  pl.BlockSpec((1, 1, d), lambda bi, i: (bi, 0, 0))],
        out_specs=pl.BlockSpec((1, tm, d), lambda bi, i: (bi, i, 0)),
        out_shape=jax.ShapeDtypeStruct((b, l, d), BF16),
        compiler_params=_params("parallel", "parallel"),
        name="rmsnorm_modulate",
    )(x, g.reshape(1, d), sc, sh)


def _dot_split(x, m):
    hi = x.astype(BF16)
    lo = (x - hi.astype(F32)).astype(BF16)
    return jnp.dot(hi, m, preferred_element_type=F32) + jnp.dot(lo, m, preferred_element_type=F32)


def _inproj_kernel(h_ref, w_ref, gain_ref, scale_ref, cos_ref, sin_ref, avg_ref, perm_ref, o_ref, *, plain, rope):
    j = pl.program_id(1)
    z = jnp.dot(h_ref[...], w_ref[0], preferred_element_type=F32)
    is_plain = functools.reduce(jnp.logical_or, [(j >= lo) & (j < hi) for lo, hi in plain])

    @pl.when(is_plain)
    def _():
        o_ref[...] = z.astype(o_ref.dtype)

    @pl.when(jnp.logical_not(is_plain))
    def _():
        rows = min(256, z.shape[0])
        for r0 in range(0, z.shape[0], rows):
            rs = slice(r0, r0 + rows)
            zc = z[rs]
            ms = _dot_split(zc * zc, avg_ref[0])
            y = zc * lax.rsqrt(ms + EPS) * gain_ref[...]
            if rope:
                y = y * cos_ref[0, rs, :] + _dot_split(y, perm_ref[0]) * sin_ref[0, rs, :]
            o_ref[rs, :] = (y * scale_ref[...]).astype(o_ref.dtype)


def _inproj(h2d, w, gain, scale, cos_tab, sin_tab, avg, perm, *, plain, type_of_block, col0, ncols, l, rope):
    m, d = h2d.shape
    tn = 2 * HEAD_DIM
    tm = min(1024, l)
    pos_blocks = l // tm
    kern = functools.partial(_inproj_kernel, plain=plain, rope=rope)
    by_type = lambda i, j: (type_of_block(j), 0, 0)
    return pl.pallas_call(
        kern,
        grid=(m // tm, ncols // tn),
        in_specs=[pl.BlockSpec((tm, d), lambda i, j: (i, 0)),
                  pl.BlockSpec((1, d, tn), lambda i, j: (j + col0, 0, 0)),
                  pl.BlockSpec((1, tn), lambda i, j: (0, j + col0)),
                  pl.BlockSpec((1, tn), lambda i, j: (0, j + col0)),
                  pl.BlockSpec((1, tm, tn), lambda i, j: (type_of_block(j), i % pos_blocks, 0)),
                  pl.BlockSpec((1, tm, tn), lambda i, j: (type_of_block(j), i % pos_blocks, 0)),
                  pl.BlockSpec((1, tn, tn), by_type),
                  pl.BlockSpec((1, tn, tn), by_type)],
        out_specs=pl.BlockSpec((tm, tn), lambda i, j: (i, j)),
        out_shape=jax.ShapeDtypeStruct((m, ncols), BF16),
        compiler_params=_params("parallel", "arbitrary"),
        name="attn_in_projection",
    )(h2d, w, gain, scale, cos_tab, sin_tab, avg, perm)


def _diffattn_kernel(lam_ref, q_ref, kc_ref, vc_ref, k_ref, v_ref, g_ref, o_ref,
                     q2_s, m_s, l_s, acc_s, *, tq, nkv, out_scale):
    kv = pl.program_id(3)

    def process(k, v):
        s = lax.dot_general(q2_s[...], k, (((1,), (1,)), ((), ())), preferred_element_type=F32)
        m_prev = m_s[...]
        m_new = jnp.maximum(m_prev, jnp.max(s, axis=-1, keepdims=True))
        alpha = jnp.exp2(m_prev - m_new)
        p = jnp.exp2(s - m_new)
        l_s[...] = alpha * l_s[...] + jnp.sum(p, axis=-1, keepdims=True)
        acc_s[...] = alpha * acc_s[...] + jnp.dot(p.astype(BF16), v, preferred_element_type=F32)
        m_s[...] = m_new

    @pl.when(kv == 0)
    def _():
        q = q_ref[0]
        lane = lax.broadcasted_iota(jnp.int32, q.shape, 1)
        zero = jnp.zeros_like(q)
        q2_s[0:tq, :] = jnp.where(lane < A_HALF, q, zero)
        q2_s[tq:2 * tq, :] = jnp.where(lane >= A_HALF, q, zero)
        m_s[...] = jnp.full(m_s.shape, -jnp.inf, F32)
        l_s[...] = jnp.zeros(l_s.shape, F32)
        acc_s[...] = jnp.zeros(acc_s.shape, F32)
        process(kc_ref[0], vc_ref[0])

    process(k_ref[0], v_ref[0])

    @pl.when(kv == nkv - 1)
    def _():
        o = acc_s[...] / l_s[...]
        d = o[0:tq] - lam_ref[0, 0] * o[tq:2 * tq]
        ms = jnp.mean(d * d, axis=-1, keepdims=True)
        o_ref[0] = (d * lax.rsqrt(ms + EPS) * g_ref[...] * out_scale).astype(o_ref.dtype)


def _diffattn_fast_kernel(lam_ref, q_ref, kc_ref, vct_ref, k_ref, vt_ref, g_ref, o_ref,
                          q2_s, l_s, acc_s, p_s, *, tq, tk, kc, nkv, out_scale):
    kv = pl.program_id(3)
    dn = (((1,), (1,)), ((), ()))

    def weights(k):
        n = k.shape[0]
        p = jnp.exp2(lax.dot_general(k, q2_s[...], dn, preferred_element_type=F32))
        return p.astype(BF16), jnp.sum(p.reshape(n // 8, 8, 2 * tq), axis=0)

    @pl.when(kv == 0)
    def _():
        q = q_ref[0]
        lane = lax.broadcasted_iota(jnp.int32, q.shape, 1)
        zero = jnp.zeros_like(q)
        q2_s[0:tq, :] = jnp.where(lane < A_HALF, q, zero)
        q2_s[tq:2 * tq, :] = jnp.where(lane >= A_HALF, q, zero)
        p, ls = weights(kc_ref[0])
        l_s[...] = ls
        acc_s[...] = jnp.dot(vct_ref[0, 0], p, preferred_element_type=F32)

    ls = l_s[...]
    for ci in range(tk // kc):
        p, lc = weights(k_ref[0, ci * kc:(ci + 1) * kc, :])
        p_s[ci * kc:(ci + 1) * kc, :] = p
        ls = ls + lc
    l_s[...] = ls
    acc_s[...] += jnp.dot(vt_ref[0, 0], p_s[...], preferred_element_type=F32)

    @pl.when(kv == nkv - 1)
    def _():
        l = jnp.sum(l_s[...], axis=0, keepdims=True)
        o = acc_s[...] / l
        d = o[:, 0:tq] - lam_ref[0, 0] * o[:, tq:2 * tq]
        ms = jnp.mean(d * d, axis=0, keepdims=True)
        y = d * lax.rsqrt(ms + EPS) * g_ref[...] * out_scale
        o_ref[0] = y.T.astype(o_ref.dtype)


def _diffattn_fast(lam, p, pc, sub_g, *, heads, l, c, out_scale):
    b = p.shape[0]
    tq = min(1024, l)
    tk = min(4096, l)
    kc = min(512, tk)
    nkv = l // tk
    k_blk = 2 * heads
    hd = heads * HEAD_DIM
    vt = jnp.transpose(p[:, :, 3 * hd:4 * hd].reshape(b, l, heads, HEAD_DIM), (0, 2, 3, 1))
    vct = jnp.transpose(pc[:, :, hd:2 * hd].reshape(b, c, heads, HEAD_DIM), (0, 2, 3, 1))
    kern = functools.partial(_diffattn_fast_kernel, tq=tq, tk=tk, kc=kc, nkv=nkv, out_scale=out_scale)
    return pl.pallas_call(
        kern,
        grid=(b, heads, l // tq, nkv),
        in_specs=[pl.BlockSpec(memory_space=pltpu.SMEM),
                  pl.BlockSpec((1, tq, HEAD_DIM), lambda bi, h, i, kv: (bi, i, h)),
                  pl.BlockSpec((1, c, HEAD_DIM), lambda bi, h, i, kv: (bi, 0, h)),
                  pl.BlockSpec((1, 1, HEAD_DIM, c), lambda bi, h, i, kv: (bi, h, 0, 0)),
                  pl.BlockSpec((1, tk, HEAD_DIM), lambda bi, h, i, kv: (bi, kv, k_blk + h)),
                  pl.BlockSpec((1, 1, HEAD_DIM, tk), lambda bi, h, i, kv: (bi, h, 0, kv)),
                  pl.BlockSpec((HEAD_DIM, 1), lambda bi, h, i, kv: (0, 0))],
        out_specs=pl.BlockSpec((1, tq, HEAD_DIM), lambda bi, h, i, kv: (bi, i, h)),
        out_shape=jax.ShapeDtypeStruct((b, l, hd), BF16),
        scratch_shapes=[pltpu.VMEM((2 * tq, HEAD_DIM), BF16),
                        pltpu.VMEM((8, 2 * tq), F32),
                        pltpu.VMEM((HEAD_DIM, 2 * tq), F32),
                        pltpu.VMEM((tk, 2 * tq), BF16)],
        compiler_params=_params("parallel", "parallel", "parallel", "arbitrary"),
        name="diff_attention_bounded",
    )(lam, p, pc, vct, p, vt, sub_g.reshape(HEAD_DIM, 1))


SCORE_BOUND_LOG2 = 60.0


def _diffattn(lam, p, pc, sub_g, score_bound, *, heads, l, c, out_scale):
    kw = dict(heads=heads, l=l, c=c, out_scale=out_scale)
    return lax.cond(score_bound < SCORE_BOUND_LOG2,
                    lambda *a: _diffattn_fast(*a, **kw), lambda *a: _diffattn_safe(*a, **kw),
                    lam, p, pc, sub_g)


def _diffattn_safe(lam, p, pc, sub_g, *, heads, l, c, out_scale):
    b = p.shape[0]
    tq = min(256, l)
    tk = min(512, l)
    nkv = l // tk
    q_blk = 0
    k_blk = 2 * heads
    v_blk = 3 * heads
    kern = functools.partial(_diffattn_kernel, tq=tq, nkv=nkv, out_scale=out_scale)
    return pl.pallas_call(
        kern,
        grid=(b, heads, l // tq, nkv),
        in_specs=[pl.BlockSpec(memory_space=pltpu.SMEM),
                  pl.BlockSpec((1, tq, HEAD_DIM), lambda bi, h, i, kv: (bi, i, q_blk + h)),
                  pl.BlockSpec((1, c, HEAD_DIM), lambda bi, h, i, kv: (bi, 0, h)),
                  pl.BlockSpec((1, c, HEAD_DIM), lambda bi, h, i, kv: (bi, 0, heads + h)),
                  pl.BlockSpec((1, tk, HEAD_DIM), lambda bi, h, i, kv: (bi, kv, k_blk + h)),
                  pl.BlockSpec((1, tk, HEAD_DIM), lambda bi, h, i, kv: (bi, kv, v_blk + h)),
                  pl.BlockSpec((1, HEAD_DIM), lambda bi, h, i, kv: (0, 0))],
        out_specs=pl.BlockSpec((1, tq, HEAD_DIM), lambda bi, h, i, kv: (bi, i, h)),
        out_shape=jax.ShapeDtypeStruct((b, l, heads * HEAD_DIM), BF16),
        scratch_shapes=[pltpu.VMEM((2 * tq, HEAD_DIM), BF16),
                        pltpu.VMEM((2 * tq, 1), F32),
                        pltpu.VMEM((2 * tq, 1), F32),
                        pltpu.VMEM((2 * tq, HEAD_DIM), F32)],
        compiler_params=_params("parallel", "parallel", "parallel", "arbitrary"),
        name="diff_attention",
    )(lam, p, pc, pc, p, p, sub_g.reshape(1, HEAD_DIM))


def _winattn_kernel(sink_ref, q_ref, kp_ref, km_ref, kn_ref, vp_ref, vm_ref, vn_ref, kc_ref, vc_ref,
                    o_ref, *, tq, l, group):
    g = pl.program_id(1)
    i = pl.program_id(2)
    kband = jnp.concatenate([kp_ref[0], km_ref[0], kn_ref[0]], axis=0)
    vband = jnp.concatenate([vp_ref[0], vm_ref[0], vn_ref[0]], axis=0)
    nk = tq + 2 * WINDOW
    qpos = i * tq + lax.broadcasted_iota(jnp.int32, (tq, nk), 0)
    kpos = i * tq - WINDOW + lax.broadcasted_iota(jnp.int32, (tq, nk), 1)
    valid = (jnp.abs(kpos - qpos) <= WINDOW) & (kpos >= 0) & (kpos < l)
    kc = kc_ref[0]
    vc = vc_ref[0]
    dn = (((1,), (1,)), ((), ()))
    for r in range(group):
        sl = slice(r * HEAD_DIM, (r + 1) * HEAD_DIM)
        q = q_ref[0, :, sl]
        s_lat = jnp.where(valid, lax.dot_general(q, kband, dn, preferred_element_type=F32), NEG)
        s_ctx = lax.dot_general(q, kc, dn, preferred_element_type=F32)
        sk = sink_ref[g, r]
        m = jnp.maximum(jnp.maximum(jnp.max(s_lat, axis=-1, keepdims=True),
                                    jnp.max(s_ctx, axis=-1, keepdims=True)), sk)
        p_lat = jnp.exp(s_lat - m)
        p_ctx = jnp.exp(s_ctx - m)
        denom = (jnp.sum(p_lat, axis=-1, keepdims=True) + jnp.sum(p_ctx, axis=-1, keepdims=True)
                 + jnp.exp(sk - m))
        o = (jnp.dot(p_lat.astype(BF16), vband, preferred_element_type=F32)
             + jnp.dot(p_ctx.astype(BF16), vc, preferred_element_type=F32))
        o_ref[0, :, sl] = (o / denom).astype(o_ref.dtype)


def _winattn(sink, p, pc, *, a_heads, b_heads, l, c):
    b = p.shape[0]
    group = b_heads // B_KV_HEADS
    tq = min(256, l)
    wpb = tq // WINDOW
    nwb = l // WINDOW
    gw = group * HEAD_DIM
    q_blk = a_heads * HEAD_DIM // gw
    k_blk = (a_heads + b_heads) + 2 * a_heads
    v_blk = k_blk + B_KV_HEADS
    kc_blk = 2 * a_heads
    vc_blk = kc_blk + B_KV_HEADS
    kern = functools.partial(_winattn_kernel, tq=tq, l=l, group=group)

    def prev_map(col):
        return lambda bi, g, i: (bi, jnp.maximum(i * wpb - 1, 0), col + g)

    def main_map(col):
        return lambda bi, g, i: (bi, i, col + g)

    def next_map(col):
        return lambda bi, g, i: (bi, jnp.minimum((i + 1) * wpb, nwb - 1), col + g)

    return pl.pallas_call(
        kern,
        grid=(b, B_KV_HEADS, l // tq),
        in_specs=[pl.BlockSpec(memory_space=pltpu.SMEM),
                  pl.BlockSpec((1, tq, gw), lambda bi, g, i: (bi, i, q_blk + g)),
                  pl.BlockSpec((1, WINDOW, HEAD_DIM), prev_map(k_blk)),
                  pl.BlockSpec((1, tq, HEAD_DIM), main_map(k_blk)),
                  pl.BlockSpec((1, WINDOW, HEAD_DIM), next_map(k_blk)),
                  pl.BlockSpec((1, WINDOW, HEAD_DIM), prev_map(v_blk)),
                  pl.BlockSpec((1, tq, HEAD_DIM), main_map(v_blk)),
                  pl.BlockSpec((1, WINDOW, HEAD_DIM), next_map(v_blk)),
                  pl.BlockSpec((1, c, HEAD_DIM), lambda bi, g, i: (bi, 0, kc_blk + g)),
                  pl.BlockSpec((1, c, HEAD_DIM), lambda bi, g, i: (bi, 0, vc_blk + g))],
        out_specs=pl.BlockSpec((1, tq, gw), lambda bi, g, i: (bi, i, g)),
        out_shape=jax.ShapeDtypeStruct((b, l, b_heads * HEAD_DIM), BF16),
        compiler_params=_params("parallel", "parallel", "parallel"),
        name="window_attention",
    )(sink, p, p, p, p, p, p, p, pc, pc)


def _mm_resid_kernel(*refs, nparts):
    a_refs = refs[0:nparts]
    w_refs = refs[nparts:2 * nparts]
    res_ref, gate_ref, o_ref = refs[2 * nparts:]
    y = jnp.dot(a_refs[0][0].astype(BF16), w_refs[0][0], preferred_element_type=F32)
    for a_ref, w_ref in zip(a_refs[1:], w_refs[1:]):
        y = y + jnp.dot(a_ref[0].astype(BF16), w_ref[0], preferred_element_type=F32)
    o_ref[0] = res_ref[0] + gate_ref[0] * y


def _mm_resid(parts, ws, res, gate):
    b, l, n = res.shape
    tm = min(1024, l)
    tn = min(512, n)
    nparts = len(parts)
    ws = [w[None] for w in ws]
    in_specs = ([pl.BlockSpec((1, tm, a.shape[2]), lambda bi, i, j: (bi, i, 0)) for a in parts]
                + [pl.BlockSpec((1, w.shape[1], tn), lambda bi, i, j: (0, 0, j)) for w in ws]
                + [pl.BlockSpec((1, tm, tn), lambda bi, i, j: (bi, i, j)),
                   pl.BlockSpec((1, 1, tn), lambda bi, i, j: (bi, 0, j))])
    return pl.pallas_call(
        functools.partial(_mm_resid_kernel, nparts=nparts),
        grid=(b, l // tm, n // tn),
        in_specs=in_specs,
        out_specs=pl.BlockSpec((1, tm, tn), lambda bi, i, j: (bi, i, j)),
        out_shape=jax.ShapeDtypeStruct((b, l, n), F32),
        compiler_params=_params("parallel", "parallel", "arbitrary"),
        name="out_projection_gated_residual",
    )(*parts, *ws, res, gate)


HALO = BF16_SUBLANES


def _fill_rows(hbuf, hp_ref, hm_ref, hn_ref, tm):
    hbuf[0:HALO, :] = hp_ref[0]
    hbuf[HALO:HALO + tm, :] = hm_ref[0]
    hbuf[HALO + tm:2 * HALO + tm, :] = hn_ref[0]


def _conv3(zbuf, cw_ref, cb_ref, tm, first, last):
    lo = slice(HALO - 1, HALO)
    hi = slice(HALO + tm, HALO + tm + 1)
    zbuf[lo, :] = jnp.where(first, 0.0, zbuf[lo, :])
    zbuf[hi, :] = jnp.where(last, 0.0, zbuf[hi, :])
    z = zbuf[...]
    rows = z.shape[0]
    g = (z * cw_ref[1:2, :] + pltpu.roll(z * cw_ref[0:1, :], 1, 0) + pltpu.roll(z * cw_ref[2:3, :], rows - 1, 0))
    return g[HALO:HALO + tm] + cb_ref[...]


def _gelu_tanh(x):
    return 0.5 * x * (1.0 + jnp.tanh(math.sqrt(2.0 / math.pi) * (x + 0.044715 * (x * x * x))))


def _ffn_up_kernel(hp_ref, hm_ref, hn_ref, wg_ref, wv_ref, cw_ref, cb_ref, o_ref, hbuf, zbuf, zbuf2, *, tm, nt):
    i = pl.program_id(1)
    j = pl.program_id(2)

    @pl.when(j == 0)
    def _():
        _fill_rows(hbuf, hp_ref, hm_ref, hn_ref, tm)

    hw = zbuf.shape[1]
    for hh, zb in enumerate((zbuf, zbuf2)):
        cs = slice(hh * hw, (hh + 1) * hw)
        zb[...] = jnp.dot(hbuf[...], wg_ref[0, :, cs], preferred_element_type=F32)
        g = _conv3(zb, cw_ref.at[:, cs], cb_ref.at[:, cs], tm, i == 0, i == nt - 1)
        v = jnp.dot(hbuf[HALO:HALO + tm, :], wv_ref[0, :, cs], preferred_element_type=F32)
        o_ref[0, :, cs] = (_gelu_tanh(g) * v).astype(o_ref.dtype)


def _hy_in_kernel(hp_ref, hm_ref, hn_ref, w_ref, b_ref, cw_ref, cb_ref, o_ref, hbuf, zbuf, zbuf2, *, tm, nt):
    i = pl.program_id(1)
    j = pl.program_id(2)

    @pl.when(j == 0)
    def _():
        _fill_rows(hbuf, hp_ref, hm_ref, hn_ref, tm)

    hw = zbuf.shape[1]
    for hh, zb in enumerate((zbuf, zbuf2)):
        cs = slice(hh * hw, (hh + 1) * hw)
        zb[...] = jnp.dot(hbuf[...], w_ref[0, :, cs], preferred_element_type=F32) + b_ref[:, cs]
        o_ref[0, :, cs] = _conv3(zb, cw_ref.at[:, cs], cb_ref.at[:, cs], tm, i == 0, i == nt - 1
                                 ).astype(o_ref.dtype)


def _mm_conv(kernel, h, ws, vecs, n, out_dtype, name, nz=1):
    b, l, k = h.shape
    tm = min(512, l)
    tn = min(512, n)
    nt = l // tm
    hpb = tm // HALO
    nhb = l // HALO
    ws = [w[None] for w in ws]
    in_specs = ([pl.BlockSpec((1, HALO, k), lambda bi, i, j: (bi, jnp.maximum(i * hpb - 1, 0), 0)),
                 pl.BlockSpec((1, tm, k), lambda bi, i, j: (bi, i, 0)),
                 pl.BlockSpec((1, HALO, k), lambda bi, i, j: (bi, jnp.minimum((i + 1) * hpb, nhb - 1), 0))]
                + [pl.BlockSpec((1, k, tn), lambda bi, i, j: (0, 0, j)) for _ in ws]
                + [pl.BlockSpec((v.shape[0], tn), lambda bi, i, j: (0, j)) for v in vecs])
    return pl.pallas_call(
        functools.partial(kernel, tm=tm, nt=nt),
        grid=(b, nt, n // tn),
        in_specs=in_specs,
        out_specs=pl.BlockSpec((1, tm, tn), lambda bi, i, j: (bi, i, j)),
        out_shape=jax.ShapeDtypeStruct((b, l, n), out_dtype),
        scratch_shapes=[pltpu.VMEM((tm + 2 * HALO, k), BF16)] + [pltpu.VMEM((tm + 2 * HALO, tn // nz), F32)] * nz,
        compiler_params=_params("parallel", "parallel", "arbitrary"),
        name=name,
    )(h, h, h, *ws, *vecs)


FILT_NB = BF16_SUBLANES
FILT_K1 = FFT_N1 // 2 + 8


def _filter_dft_kernel(z_ref, w0_ref, b0_ref, w1_ref, b1_ref, w2_ref, b2_ref, fr_ref, w3_ref, dl_ref, t_ref,
                       o_ref, s_ref, a3_s, r_s):
    m, seg, j = pl.program_id(0), pl.program_id(1), pl.program_id(2)
    tn = o_ref.shape[3]
    fr = fr_ref[...]

    def layer(a, w_ref, b_ref):
        return jnp.sin(fr * (jnp.dot(a, w_ref[...], preferred_element_type=F32, precision=HIGHEST)
                             + b_ref[...]))

    @pl.when((seg == 0) & (j == 0))
    def _():
        for jj in range(FILT_NB):
            a = layer(layer(layer(z_ref[jj], w0_ref, b0_ref), w1_ref, b1_ref), w2_ref, b2_ref)
            a_hi = a.astype(BF16)
            a_lo = (a - a_hi.astype(F32)).astype(BF16)
            a3_s[jj] = jnp.concatenate([a_hi, a_hi, a_lo], axis=1)

    @pl.when((m == 0) & (seg == 0) & (j == 0))
    def _():
        s_ref[...] = jnp.zeros(s_ref.shape, F32)

    dl = dl_ref[...]
    backward = seg % 2 == 1
    abs_sum = jnp.zeros((1, tn), F32)
    for jj in range(FILT_NB):
        h = (jnp.dot(a3_s[jj], w3_ref[0], preferred_element_type=F32)
             * jnp.exp(-z_ref[jj][:, 0:1] * dl))
        if jj == 0:
            row = lax.broadcasted_iota(jnp.int32, h.shape, 0)
            h = jnp.where(backward & (m == 0) & (row == 0), 0.0, h)
        abs_sum = abs_sum + jnp.sum(jnp.abs(h), axis=0, keepdims=True)
        r_s[jj] = jnp.dot(t_ref[jj], h.astype(BF16), preferred_element_type=F32)
    o_ref[0] = jnp.swapaxes(r_s[...], 0, 1).astype(o_ref.dtype)
    col = pl.multiple_of(j * tn, tn)
    s_ref[seg, :, pl.ds(col, tn)] += abs_sum


def _hyena_filter_spectra_a(l, d, taf, f_w0, f_b0, f_w1, f_b1, f_w2, f_b2, f_freq, f_w3):
    emb, fw = f_w0.shape
    bands = (emb - 1) // 2
    half = FFT_N1 // 2
    n2 = l // half
    t = np.linspace(0.0, 1.0, l)[:, None]
    w = 2.0 * math.pi * np.arange(l)[:, None] / l
    f = np.linspace(1e-4, bands - 1, bands)[None, :]
    emb_pad = -(-emb // 8) * 8
    z = np.concatenate([t, np.cos(f * w), -np.sin(f * w), np.zeros((l, emb_pad - emb))], axis=-1)
    z = jnp.asarray(z.reshape(half, n2, emb_pad).transpose(1, 0, 2), F32)
    w0 = jnp.concatenate([f_w0, jnp.zeros((emb_pad - emb, fw), F32)], axis=0)
    max_decay = math.log(HY_DECAY_TARGET) / HY_MAX_DECAY_PCT
    min_decay = math.log(HY_DECAY_TARGET) / HY_MIN_DECAY_PCT
    dl = jnp.asarray(np.abs(np.linspace(min_decay, max_decay, d))[None, :], F32)
    nseg = 2 * HY_ORDER
    w3 = jnp.transpose(f_w3.reshape(fw, nseg, d), (1, 0, 2))
    w3_hi = w3.astype(BF16)
    w3_lo = (w3 - w3_hi.astype(F32)).astype(BF16)
    w3 = jnp.concatenate([w3_hi, w3_lo, w3_hi], axis=1)
    nb = min(FILT_NB, n2)
    assert nb == FILT_NB
    tn = min(256, d)
    vec = lambda a: a.reshape(1, fw)
    small = lambda shape: pl.BlockSpec(shape, lambda m, sg, j: (0,) * len(shape))
    return pl.pallas_call(
        _filter_dft_kernel,
        grid=(n2 // nb, nseg, d // tn),
        in_specs=[pl.BlockSpec((nb, half, emb_pad), lambda m, sg, j: (m, 0, 0)),
                  small((emb_pad, fw)), small((1, fw)), small((fw, fw)), small((1, fw)),
                  small((fw, fw)), small((1, fw)), small((1, fw)),
                  pl.BlockSpec((1, 3 * fw, tn), lambda m, sg, j: (sg, 0, j)),
                  pl.BlockSpec((1, tn), lambda m, sg, j: (0, j)),
                  pl.BlockSpec((nb, 2 * FILT_K1, half), lambda m, sg, j: (m, 0, 0))],
        out_specs=[pl.BlockSpec((1, 2 * FILT_K1, nb, tn), lambda m, sg, j: (sg, 0, m, j)),
                   pl.BlockSpec((nseg, 1, d), lambda m, sg, j: (0, 0, 0))],
        out_shape=[jax.ShapeDtypeStruct((nseg, 2 * FILT_K1, n2, d), BF16),
                   jax.ShapeDtypeStruct((nseg, 1, d), F32)],
        scratch_shapes=[pltpu.VMEM((nb, half, 3 * fw), BF16),
                        pltpu.VMEM((nb, 2 * FILT_K1, tn), F32)],
        compiler_params=_params("arbitrary", "arbitrary", "arbitrary"),
        name="hyena_filter_dft_stage_a",
    )(z, w0, vec(f_b0), f_w1, vec(f_b1), f_w2, vec(f_b2), vec(f_freq), w3, dl, taf)


def _dft_tables(l):
    n1 = FFT_N1
    n = 2 * l
    n2 = n // n1
    k1 = np.arange(n1)[:, None]
    f1 = np.exp(-2j * np.pi * k1 * np.arange(n1 // 2)[None, :] / n1)
    tw = np.exp(-2j * np.pi * np.arange(n2)[:, None] * np.arange(n1)[None, :] / n)
    f1r, f1i = jnp.asarray(f1.real, F32), jnp.asarray(f1.imag, F32)
    twr, twi = jnp.asarray(tw.real, F32)[:, :, None], jnp.asarray(tw.imag, F32)[:, :, None]
    fr = twr * f1r - twi * f1i
    fi = twr * f1i + twi * f1r
    ta = jnp.concatenate([jnp.concatenate([fr, -fi], axis=2),
                          jnp.concatenate([fi, fr], axis=2)], axis=1)
    fb = np.exp(-2j * np.pi * np.arange(n2)[:, None] * np.arange(n2)[None, :] / n2)
    tb = np.block([[fb.real, -fb.imag], [fb.imag, fb.real]])
    taf = jnp.concatenate([ta[:, 0:FILT_K1, 0:n1 // 2], ta[:, n1:n1 + FILT_K1, 0:n1 // 2]], axis=1)
    fm = np.exp(-2j * np.pi * (np.arange(n2)[:, None] + 1) * np.arange(n2)[None, :] / n2)
    tb_mirror = np.block([[fm.real, fm.imag], [fm.imag, -fm.real]])
    return (ta.astype(BF16), jnp.swapaxes(ta, 1, 2).astype(BF16), jnp.asarray(tb, BF16), jnp.asarray(tb.T, BF16),
            taf.astype(BF16), jnp.asarray(np.stack([tb, tb_mirror]), BF16))


FFT_NB = BF16_SUBLANES


def _fft_a_kernel(x_ref, t_ref, o_ref, r_s):
    planes = x_ref.shape[1]
    xt = [jnp.swapaxes(x_ref[0, c], 0, 1) for c in range(planes)]
    for jj in range(FFT_NB):
        x = jnp.concatenate([xt[c][jj] for c in range(planes)], axis=0).astype(BF16)
        r_s[jj] = jnp.dot(t_ref[jj], x, preferred_element_type=F32)
    o_ref[0] = jnp.swapaxes(r_s[...], 0, 1).astype(o_ref.dtype)


def _fft_a(x5, ta, *, n2, d, seg):
    s, p, half, _, _ = x5.shape
    kdim = p * half
    tn = min(256, d)
    sb, db = (seg * d) // tn, d // tn
    return pl.pallas_call(
        _fft_a_kernel,
        grid=(s, n2 // FFT_NB, db),
        in_specs=[pl.BlockSpec((1, p, half, FFT_NB, tn), lambda si, m, j: (si, 0, 0, m, sb + j)),
                  pl.BlockSpec((FFT_NB, 2 * FFT_N1, kdim), lambda si, m, j: (m, 0, 0))],
        out_specs=pl.BlockSpec((1, 2 * FFT_N1, FFT_NB, tn), lambda si, m, j: (si, 0, m, j)),
        out_shape=jax.ShapeDtypeStruct((s, 2 * FFT_N1, n2, d), BF16),
        scratch_shapes=[pltpu.VMEM((FFT_NB, 2 * FFT_N1, tn), F32)],
        compiler_params=_params("parallel", "parallel", "parallel"),
        name="dft_stage_a",
    )(x5, ta)


def _fft_b_kernel(a_ref, af_ref, ab_ref, sc_ref, tb_ref, tbf_ref, tbt_ref, o_ref, *, n2):
    z = jnp.dot(tb_ref[...], a_ref[...].reshape(2 * n2, -1), preferred_element_type=F32)
    tbf = tbf_ref[0]
    hf = jnp.dot(tbf, af_ref[0].reshape(2 * n2, -1), preferred_element_type=F32)
    hb = jnp.dot(tbf, ab_ref[0].reshape(2 * n2, -1), preferred_element_type=F32)
    sc = sc_ref[0]
    gr = (hf[0:n2] + hb[0:n2]) * sc
    gi = (hf[n2:2 * n2] - hb[n2:2 * n2]) * sc
    zr, zi = z[0:n2], z[n2:2 * n2]
    prod = jnp.concatenate([zr * gr - zi * gi, zr * gi + zi * gr], axis=0).astype(BF16)
    y = jnp.dot(tbt_ref[...], prod, preferred_element_type=F32)
    o_ref[...] = y.reshape(o_ref.shape).astype(o_ref.dtype)


def _fft_b(a, af, scale, tb, tbf, tbt, order, *, n2, d):
    tn = min(2048, d)
    blk = (2, 1, n2, tn)
    fblk = (1, 2, 1, n2, tn)
    mirrored = lambda k: k > FFT_N1 // 2
    fk = lambda k: jnp.where(mirrored(k), FFT_N1 - k, k)
    return pl.pallas_call(
        functools.partial(_fft_b_kernel, n2=n2),
        grid=(FFT_N1, d // tn),
        in_specs=[pl.BlockSpec(blk, lambda k, j: (0, k, 0, j)),
                  pl.BlockSpec(fblk, lambda k, j: (2 * order, 0, fk(k), 0, j)),
                  pl.BlockSpec(fblk, lambda k, j: (2 * order + 1, 0, fk(k), 0, j)),
                  pl.BlockSpec((1, 1, tn), lambda k, j: (order, 0, j)),
                  pl.BlockSpec((2 * n2, 2 * n2), lambda k, j: (0, 0)),
                  pl.BlockSpec((1, 2 * n2, 2 * n2), lambda k, j: (jnp.where(mirrored(k), 1, 0), 0, 0)),
                  pl.BlockSpec((2 * n2, 2 * n2), lambda k, j: (0, 0))],
        out_specs=pl.BlockSpec(blk, lambda k, j: (0, k, 0, j)),
        out_shape=jax.ShapeDtypeStruct((2, FFT_N1, n2, d), BF16),
        compiler_params=_params("parallel", "parallel"),
        name="dft_stage_b_spectrum_product",
    )(a, af, af, scale, tb, tbf, tbt)


def _fft_ainv_kernel(y_ref, t_ref, u_ref, gate_ref, d_ref, o_ref, r_s):
    half = o_ref.shape[1]
    yt = jnp.swapaxes(y_ref[...].astype(F32), 0, 1).astype(BF16)
    for jj in range(FFT_NB):
        x = jnp.dot(t_ref[jj], yt[jj], preferred_element_type=F32)
        for c in range(2):
            r_s[c, jj] = x[c * half:(c + 1) * half]
    for c in range(2):
        o_ref[c] = gate_ref[c] * (jnp.swapaxes(r_s[c], 0, 1) + d_ref[...] * u_ref[c])


def _fft_ainv(y3, tat, u4, gate4, dskip, *, n2, d, useg, gseg):
    b, half, _, _ = u4.shape
    tn = min(128, d)
    db = d // tn
    usb, gsb = (useg * d) // tn, (gseg * d) // tn
    return pl.pallas_call(
        _fft_ainv_kernel,
        grid=(n2 // FFT_NB, db),
        in_specs=[pl.BlockSpec((2 * FFT_N1, FFT_NB, tn), lambda m, j: (0, m, j)),
                  pl.BlockSpec((FFT_NB, FFT_N1, 2 * FFT_N1), lambda m, j: (m, 0, 0)),
                  pl.BlockSpec((b, half, FFT_NB, tn), lambda m, j: (0, 0, m, usb + j)),
                  pl.BlockSpec((b, half, FFT_NB, tn), lambda m, j: (0, 0, m, gsb + j)),
                  pl.BlockSpec((1, tn), lambda m, j: (0, j))],
        out_specs=pl.BlockSpec((b, half, FFT_NB, tn), lambda m, j: (0, 0, m, j)),
        out_shape=jax.ShapeDtypeStruct((b, half, n2, d), F32),
        scratch_shapes=[pltpu.VMEM((b, FFT_NB, half, tn), F32)],
        compiler_params=_params("parallel", "parallel"),
        name="dft_stage_a_inverse_gate",
    )(y3, tat, u4, gate4, dskip)


def _hyena_mixer(h, hy, l, d):
    (w_in, b_in, sconv_w, sconv_b, f_w0, f_b0, f_w1, f_b1, f_w2, f_b2, f_freq, f_w3, d_skip, w_out) = hy
    b = h.shape[0]
    assert b == 2, "the two batch rows ride the real / imaginary planes of one complex DFT"
    n1 = FFT_N1
    half = n1 // 2
    n2 = (2 * l) // n1
    assert half * n2 == l
    z3 = _mm_conv(_hy_in_kernel, h, [w_in.astype(BF16)],
                  [b_in.reshape(1, -1), sconv_w, sconv_b.reshape(1, -1)], 3 * d, F32, "hyena_in_proj_conv", nz=2)
    ta, tat, tb, tbt, taf, tbf = _dft_tables(l)
    nseg = 2 * HY_ORDER
    af, fsum = _hyena_filter_spectra_a(l, d, taf, f_w0, f_b0, f_w1, f_b1, f_w2, f_b2, f_freq, f_w3)
    af = af.reshape(nseg, 2, FILT_K1, n2, d)
    scale = 1.0 / ((fsum[0::2] + fsum[1::2]) * (2 * l))
    z4 = z3.reshape(b, half, n2, 3 * d)
    y, yseg = z4, 0
    for o in range(HY_ORDER):
        a = _fft_a(y[None], ta, n2=n2, d=d, seg=yseg)
        yb = _fft_b(a.reshape(2, n1, n2, d), af, scale, tb, tbf, tbt, o, n2=n2, d=d)
        y = _fft_ainv(yb.reshape(2 * n1, n2, d), tat, y, z4, d_skip[o].reshape(1, d),
                      n2=n2, d=d, useg=yseg, gseg=o + 1)
        yseg = 0
    return y.reshape(b, l, d), w_out.astype(BF16)


def _rope_tables(l):
    t = jnp.arange(l)
    row = (t // GRID_W).astype(F32)[:, None]
    col = (t % GRID_W).astype(F32)[:, None]
    width = 2 * HEAD_DIM
    cos, sin, avg, perm = [], [], [], []
    for dim in (A_HALF, HEAD_DIM):
        nf = dim // 4
        inv = jnp.asarray(ROPE_BASE ** (-np.arange(nf) / nf), F32)[None, :]
        ar, ac = row * inv, col * inv
        c = jnp.concatenate([jnp.cos(ar), jnp.cos(ar), jnp.cos(ac), jnp.cos(ac)], axis=1)
        s = jnp.concatenate([-jnp.sin(ar), jnp.sin(ar), -jnp.sin(ac), jnp.sin(ac)], axis=1)
        reps = width // dim
        cos.append(jnp.tile(c, (1, reps)))
        sin.append(jnp.tile(s, (1, reps)))
        lane = np.arange(width)
        avg.append((lane[:, None] // dim == lane[None, :] // dim) / dim)
        partner = np.where(lane % (2 * nf) < nf, lane + nf, lane - nf)
        perm.append((lane[:, None] == partner[None, :]).astype(np.float64))
    return (jnp.stack(cos), jnp.stack(sin), jnp.asarray(np.stack(avg), BF16), jnp.asarray(np.stack(perm), BF16))


def _attn_mixer(h, hc, lam_init, w_in, w_out, a_q_g, a_k_g, lq1, lk1, lq2, lk2, a_sub_g, b_q_g, b_k_g, b_sink):
    b, l, d = h.shape
    c = hc.shape[1]
    a_heads = d // (2 * HEAD_DIM)
    b_heads = d // (2 * HEAD_DIM)
    q_cols = (a_heads + b_heads) * HEAD_DIM
    in_cols = w_in.shape[1]
    lam = (jnp.exp(jnp.sum(lq1 * lk1)) - jnp.exp(jnp.sum(lq2 * lk2)) + lam_init).reshape(1, 1)
    sink = b_sink.reshape(B_KV_HEADS, b_heads // B_KV_HEADS)

    ones = lambda n: jnp.ones((n,), F32)
    tile = lambda v, n: jnp.tile(v, n)
    gain = jnp.concatenate([tile(a_q_g, 2 * a_heads), tile(b_q_g, b_heads), tile(a_k_g, 2 * a_heads),
                            ones(a_heads * HEAD_DIM), tile(b_k_g, B_KV_HEADS),
                            ones(B_KV_HEADS * HEAD_DIM)]).reshape(1, in_cols)
    scale = jnp.concatenate([jnp.full((a_heads * HEAD_DIM,), A_HALF ** -0.5 * math.log2(math.e), F32),
                             jnp.full((b_heads * HEAD_DIM,), HEAD_DIM ** -0.5, F32),
                             ones(in_cols - q_cols)]).reshape(1, in_cols)
    e = [0, a_heads // 2, (a_heads + b_heads) // 2, (2 * a_heads + b_heads) // 2,
         (3 * a_heads + b_heads) // 2, (3 * a_heads + b_heads) // 2 + 1, (3 * a_heads + b_heads) // 2 + 2]
    plain = ((e[3], e[4]), (e[5], e[6]))
    cos_tab, sin_tab, avg, perm = _rope_tables(l)

    def type_of(j0):
        def f(j):
            jj = j + j0
            is_a = (jj < e[1]) | ((jj >= e[2]) & (jj < e[3]))
            return jnp.where(is_a, 0, 1)
        return f

    w_bf = _col_tiles(w_in.astype(BF16), 2 * HEAD_DIM)
    p = _inproj(h.reshape(b * l, d), w_bf, gain, scale, cos_tab, sin_tab, avg, perm, plain=plain,
                type_of_block=type_of(0), col0=0, ncols=in_cols, l=l, rope=True).reshape(b, l, in_cols)
    kv0 = e[2]
    plain_c = tuple((lo - kv0, hi - kv0) for lo, hi in plain)
    pc = _inproj(hc.reshape(b * c, d), w_bf, gain, scale, cos_tab, sin_tab, avg, perm, plain=plain_c,
                 type_of_block=type_of(kv0), col0=kv0, ncols=in_cols - q_cols, l=c, rope=False
                 ).reshape(b, c, in_cols - q_cols)
    score_bound = (A_HALF ** 0.5 * math.log2(math.e)) * jnp.max(jnp.abs(a_q_g)) * jnp.max(jnp.abs(a_k_g))
    oa = _diffattn(lam, p, pc, a_sub_g, score_bound, heads=a_heads, l=l, c=c, out_scale=1.0 - lam_init)
    ob = _winattn(sink, p, pc, a_heads=a_heads, b_heads=b_heads, l=l, c=c)
    w_out_bf = w_out.astype(BF16)
    na = a_heads * HEAD_DIM
    return [oa, ob], [w_out_bf[:na], w_out_bf[na:]]


def kernel(x, c, ctx, c_ctx, ada_w, ada_b, norm1_g, norm2_g, attn_w_in, attn_w_out, a_q_g, a_k_g, a_lam_q1, a_lam_k1, a_lam_q2, a_lam_k2, a_sub_g, b_q_g, b_k_g, b_sink, hy_w_in, hy_b_in, hy_sconv_w, hy_sconv_b, hy_f_w0, hy_f_b0, hy_f_w1, hy_f_b1, hy_f_w2, hy_f_b2, hy_f_freq, hy_f_w3, hy_d, hy_w_out, ffn_w_gate, ffn_w_val, ffn_conv_w, ffn_conv_b, ffn_w_down):
    b, l, d = x.shape
    depth = ada_w.shape[0]
    cc = jnp.concatenate([c, c_ctx[None, :], jnp.zeros((8 - b - 1, d), F32)], axis=0)
    m = _ada(cc, ada_w, ada_b)
    xs = x
    for layer in range(depth):
        i = layer // 2
        lat = [m[layer, :b, k * d:(k + 1) * d].reshape(b, 1, d) for k in range(6)]
        sh1, sc1, g1, sh2, sc2, g2 = lat
        h = _normmod(xs, norm1_g[layer], sc1, sh1)
        if layer % 2 == 0:
            mc = [jnp.broadcast_to(m[layer, b, k * d:(k + 1) * d].reshape(1, 1, d), (b, 1, d)) for k in range(2)]
            hc = _normmod(ctx, norm1_g[layer], mc[1], mc[0])
            parts, ws = _attn_mixer(h, hc, 0.8 - 0.6 * math.exp(-0.3 * layer), attn_w_in[i], attn_w_out[i],
                                    a_q_g[i], a_k_g[i], a_lam_q1[i], a_lam_k1[i], a_lam_q2[i], a_lam_k2[i],
                                    a_sub_g[i], b_q_g[i], b_k_g[i], b_sink[i])
        else:
            hy = (hy_w_in[i], hy_b_in[i], hy_sconv_w[i], hy_sconv_b[i], hy_f_w0[i], hy_f_b0[i], hy_f_w1[i],
                  hy_f_b1[i], hy_f_w2[i], hy_f_b2[i], hy_f_freq[i], hy_f_w3[i], hy_d[i], hy_w_out[i])
            y, w_o = _hyena_mixer(h, hy, l, d)
            parts, ws = [y], [w_o]
        xs = _mm_resid(parts, ws, xs, g1)
        h2 = _normmod(xs, norm2_g[layer], sc2, sh2)
        d_ff = ffn_w_gate.shape[2]
        hid = _mm_conv(_ffn_up_kernel, h2, [ffn_w_gate[layer].astype(BF16), ffn_w_val[layer].astype(BF16)],
                       [ffn_conv_w[layer], ffn_conv_b[layer].reshape(1, -1)], d_ff, BF16, "ffn_up_conv_glu", nz=2)
        xs = _mm_resid([hid], [ffn_w_down[layer].astype(BF16)], xs, g2)
    return xs
```

```python
import functools
import math

import numpy as np
import jax
import jax.numpy as jnp
from jax import lax
from jax.experimental import pallas as pl
from jax.experimental.pallas import tpu as pltpu

F32 = jnp.float32
BF16 = jnp.bfloat16
HIGHEST = lax.Precision.HIGHEST

HEAD_DIM = 128
A_HALF = HEAD_DIM // 2
B_KV_HEADS = 2
GRID_W = 64
WINDOW = 128
ROPE_BASE = 10000.0
EPS = 1e-6
NEG = -1e30
HY_ORDER = 2
HY_DECAY_TARGET = 1e-2
HY_MAX_DECAY_PCT = 0.3
HY_MIN_DECAY_PCT = 1.5

BF16_SUBLANES = 16
FFT_N1 = 256
VMEM_LIMIT_BYTES = 56 * 1024 * 1024


def _params(*sem):
    return pltpu.CompilerParams(dimension_semantics=sem, vmem_limit_bytes=VMEM_LIMIT_BYTES)


def _ada_kernel(c_ref, w_ref, b_ref, o_ref):
    c = c_ref[...]
    s = c * (1.0 / (1.0 + jnp.exp(-c)))
    o_ref[0] = jnp.dot(s, w_ref[0], preferred_element_type=F32, precision=HIGHEST) + b_ref[0]


def _ada(cc, ada_w, ada_b):
    depth, d, n = ada_w.shape
    tn = 1024
    return pl.pallas_call(
        _ada_kernel,
        grid=(depth, n // tn),
        in_specs=[pl.BlockSpec((8, d), lambda l, j: (0, 0)),
                  pl.BlockSpec((1, d, tn), lambda l, j: (l, 0, j)),
                  pl.BlockSpec((1, 1, tn), lambda l, j: (l, 0, j))],
        out_specs=pl.BlockSpec((1, 8, tn), lambda l, j: (l, 0, j)),
        out_shape=jax.ShapeDtypeStruct((depth, 8, n), F32),
        compiler_params=_params("parallel", "parallel"),
        name="ada_modulation",
    )(cc, ada_w, ada_b.reshape(depth, 1, n))


NORM_ROWS = 32


def _normmod_kernel(x_ref, g_ref, sc_ref, sh_ref, o_ref):
    scale = g_ref[...] * (1.0 + sc_ref[0])
    shift = sh_ref[0]
    rows = min(NORM_ROWS, x_ref.shape[1])

    def slab(r, carry):
        rs = pl.ds(pl.multiple_of(r * rows, rows), rows)
        x = x_ref[0, rs, :]
        ms = jnp.mean(x * x, axis=-1, keepdims=True)
        o_ref[0, rs, :] = (x * lax.rsqrt(ms + EPS) * scale + shift).astype(o_ref.dtype)
        return carry

    lax.fori_loop(0, x_ref.shape[1] // rows, slab, 0, unroll=4)


def _normmod(x, g, sc, sh):
    b, l, d = x.shape
    tm = min(512, l)
    return pl.pallas_call(
        _normmod_kernel,
        grid=(b, l // tm),
        in_specs=[pl.BlockSpec((1, tm, d), lambda bi, i: (bi, i, 0)),
                  pl.BlockSpec((1, d), lambda bi, i: (0, 0)),
                  pl.BlockSpec((1, 1, d), lambda bi, i: (bi, 0, 0)),
                  pl.BlockSpec((1, 1, d), lambda bi, i: (bi, 0, 0))],
        out_specs=pl.BlockSpec((1, tm, d), lambda bi, i: (bi, i, 0)),
        out_shape=jax.ShapeDtypeStruct((b, l, d), BF16),
        compiler_params=_params("parallel", "parallel"),
        name="rmsnorm_modulate",
    )(x, g.reshape(1, d), sc, sh)


def _dot_split(x, m):
    hi = x.astype(BF16)
    lo = (x - hi.astype(F32)).astype(BF16)
    return jnp.dot(hi, m, preferred_element_type=F32) + jnp.dot(lo, m, preferred_element_type=F32)


def _inproj_kernel(h_ref, w_ref, gain_ref, scale_ref, cos_ref, sin_ref, avg_ref, perm_ref, o_ref, *, plain, rope):
    j = pl.program_id(1)
    z = jnp.dot(h_ref[...], w_ref[0], preferred_element_type=F32)
    is_plain = functools.reduce(jnp.logical_or, [(j >= lo) & (j < hi) for lo, hi in plain])

    @pl.when(is_plain)
    def _():
        o_ref[...] = z.astype(o_ref.dtype)

    @pl.when(jnp.logical_not(is_plain))
    def _():
        rows = min(256, z.shape[0])
        for r0 in range(0, z.shape[0], rows):
            rs = slice(r0, r0 + rows)
            zc = z[rs]
            ms = _dot_split(zc * zc, avg_ref[0])
            y = zc * lax.rsqrt(ms + EPS) * gain_ref[...]
            if rope:
                y = y * cos_ref[0, rs, :] + _dot_split(y, perm_ref[0]) * sin_ref[0, rs, :]
            o_ref[rs, :] = (y * scale_ref[...]).astype(o_ref.dtype)


def _inproj(h2d, w, gain, scale, cos_tab, sin_tab, avg, perm, *, plain, type_of_block, col0, ncols, l, rope):
    m, d = h2d.shape
    tn = 2 * HEAD_DIM
    tm = min(1024, l)
    pos_blocks = l // tm
    kern = functools.partial(_inproj_kernel, plain=plain, rope=rope)
    by_type = lambda i, j: (type_of_block(j), 0, 0)
    return pl.pallas_call(
        kern,
        grid=(m // tm, ncols // tn),
        in_specs=[pl.BlockSpec((tm, d), lambda i, j: (i, 0)),
                  pl.BlockSpec((1, d, tn), lambda i, j: (0, 0, j + col0)),
                  pl.BlockSpec((1, tn), lambda i, j: (0, j + col0)),
                  pl.BlockSpec((1, tn), lambda i, j: (0, j + col0)),
                  pl.BlockSpec((1, tm, tn), lambda i, j: (type_of_block(j), i % pos_blocks, 0)),
                  pl.BlockSpec((1, tm, tn), lambda i, j: (type_of_block(j), i % pos_blocks, 0)),
                  pl.BlockSpec((1, tn, tn), by_type),
                  pl.BlockSpec((1, tn, tn), by_type)],
        out_specs=pl.BlockSpec((tm, tn), lambda i, j: (i, j)),
        out_shape=jax.ShapeDtypeStruct((m, ncols), BF16),
        compiler_params=_params("parallel", "arbitrary"),
        name="attn_in_projection",
    )(h2d, w, gain, scale, cos_tab, sin_tab, avg, perm)


def _diffattn_kernel(lam_ref, q_ref, kc_ref, vc_ref, k_ref, v_ref, g_ref, o_ref,
                     q2_s, m_s, l_s, acc_s, *, tq, nkv, out_scale):
    kv = pl.program_id(3)

    def process(k, v):
        s = lax.dot_general(q2_s[...], k, (((1,), (1,)), ((), ())), preferred_element_type=F32)
        m_prev = m_s[...]
        m_new = jnp.maximum(m_prev, jnp.max(s, axis=-1, keepdims=True))
        alpha = jnp.exp2(m_prev - m_new)
        p = jnp.exp2(s - m_new)
        l_s[...] = alpha * l_s[...] + jnp.sum(p, axis=-1, keepdims=True)
        acc_s[...] = alpha * acc_s[...] + jnp.dot(p.astype(BF16), v, preferred_element_type=F32)
        m_s[...] = m_new

    @pl.when(kv == 0)
    def _():
        q = q_ref[0]
        lane = lax.broadcasted_iota(jnp.int32, q.shape, 1)
        zero = jnp.zeros_like(q)
        q2_s[0:tq, :] = jnp.where(lane < A_HALF, q, zero)
        q2_s[tq:2 * tq, :] = jnp.where(lane >= A_HALF, q, zero)
        m_s[...] = jnp.full(m_s.shape, -jnp.inf, F32)
        l_s[...] = jnp.zeros(l_s.shape, F32)
        acc_s[...] = jnp.zeros(acc_s.shape, F32)
        process(kc_ref[0], vc_ref[0])

    process(k_ref[0], v_ref[0])

    @pl.when(kv == nkv - 1)
    def _():
        o = acc_s[...] / l_s[...]
        d = o[0:tq] - lam_ref[0, 0] * o[tq:2 * tq]
        ms = jnp.mean(d * d, axis=-1, keepdims=True)
        o_ref[0] = (d * lax.rsqrt(ms + EPS) * g_ref[...] * out_scale).astype(o_ref.dtype)


def _diffattn_fast_kernel(lam_ref, q_ref, kc_ref, vct_ref, k_ref, vt_ref, g_ref, o_ref,
                          q2_s, l_s, acc_s, p_s, *, tq, tk, kc, nkv, out_scale):
    kv = pl.program_id(3)
    dn = (((1,), (1,)), ((), ()))

    def weights(k):
        n = k.shape[0]
        p = jnp.exp2(lax.dot_general(k, q2_s[...], dn, preferred_element_type=F32))
        return p.astype(BF16), jnp.sum(p.reshape(n // 8, 8, 2 * tq), axis=0)

    @pl.when(kv == 0)
    def _():
        q = q_ref[0]
        lane = lax.broadcasted_iota(jnp.int32, q.shape, 1)
        zero = jnp.zeros_like(q)
        q2_s[0:tq, :] = jnp.where(lane < A_HALF, q, zero)
        q2_s[tq:2 * tq, :] = jnp.where(lane >= A_HALF, q, zero)
        p, ls = weights(kc_ref[0])
        l_s[...] = ls
        acc_s[...] = jnp.dot(vct_ref[0, 0], p, preferred_element_type=F32)

    ls = l_s[...]
    for ci in range(tk // kc):
        p, lc = weights(k_ref[0, ci * kc:(ci + 1) * kc, :])
        p_s[ci * kc:(ci + 1) * kc, :] = p
        ls = ls + lc
    l_s[...] = ls
    acc_s[...] += jnp.dot(vt_ref[0, 0], p_s[...], preferred_element_type=F32)

    @pl.when(kv == nkv - 1)
    def _():
        l = jnp.sum(l_s[...], axis=0, keepdims=True)
        o = acc_s[...] / l
        d = o[:, 0:tq] - lam_ref[0, 0] * o[:, tq:2 * tq]
        ms = jnp.mean(d * d, axis=0, keepdims=True)
        y = d * lax.rsqrt(ms + EPS) * g_ref[...] * out_scale
        o_ref[0] = y.T.astype(o_ref.dtype)


def _diffattn_fast(lam, p, pc, sub_g, *, heads, l, c, out_scale):
    b = p.shape[0]
    tq = min(1024, l)
    tk = min(4096, l)
    kc = min(512, tk)
    nkv = l // tk
    k_blk = 2 * heads
    hd = heads * HEAD_DIM
    vt = jnp.transpose(p[:, :, 3 * hd:4 * hd].reshape(b, l, heads, HEAD_DIM), (0, 2, 3, 1))
    vct = jnp.transpose(pc[:, :, hd:2 * hd].reshape(b, c, heads, HEAD_DIM), (0, 2, 3, 1))
    kern = functools.partial(_diffattn_fast_kernel, tq=tq, tk=tk, kc=kc, nkv=nkv, out_scale=out_scale)
    return pl.pallas_call(
        kern,
        grid=(b, heads, l // tq, nkv),
        in_specs=[pl.BlockSpec(memory_space=pltpu.SMEM),
                  pl.BlockSpec((1, tq, HEAD_DIM), lambda bi, h, i, kv: (bi, i, h)),
                  pl.BlockSpec((1, c, HEAD_DIM), lambda bi, h, i, kv: (bi, 0, h)),
                  pl.BlockSpec((1, 1, HEAD_DIM, c), lambda bi, h, i, kv: (bi, h, 0, 0)),
                  pl.BlockSpec((1, tk, HEAD_DIM), lambda bi, h, i, kv: (bi, kv, k_blk + h)),
                  pl.BlockSpec((1, 1, HEAD_DIM, tk), lambda bi, h, i, kv: (bi, h, 0, kv)),
                  pl.BlockSpec((HEAD_DIM, 1), lambda bi, h, i, kv: (0, 0))],
        out_specs=pl.BlockSpec((1, tq, HEAD_DIM), lambda bi, h, i, kv: (bi, i, h)),
        out_shape=jax.ShapeDtypeStruct((b, l, hd), BF16),
        scratch_shapes=[pltpu.VMEM((2 * tq, HEAD_DIM), BF16),
                        pltpu.VMEM((8, 2 * tq), F32),
                        pltpu.VMEM((HEAD_DIM, 2 * tq), F32),
                        pltpu.VMEM((tk, 2 * tq), BF16)],
        compiler_params=_params("parallel", "parallel", "parallel", "arbitrary"),
        name="diff_attention_bounded",
    )(lam, p, pc, vct, p, vt, sub_g.reshape(HEAD_DIM, 1))


SCORE_BOUND_LOG2 = 60.0


def _diffattn(lam, p, pc, sub_g, score_bound, *, heads, l, c, out_scale):
    kw = dict(heads=heads, l=l, c=c, out_scale=out_scale)
    return lax.cond(score_bound < SCORE_BOUND_LOG2,
                    lambda *a: _diffattn_fast(*a, **kw), lambda *a: _diffattn_safe(*a, **kw),
                    lam, p, pc, sub_g)


def _diffattn_safe(lam, p, pc, sub_g, *, heads, l, c, out_scale):
    b = p.shape[0]
    tq = min(256, l)
    tk = min(512, l)
    nkv = l // tk
    q_blk = 0
    k_blk = 2 * heads
    v_blk = 3 * heads
    kern = functools.partial(_diffattn_kernel, tq=tq, nkv=nkv, out_scale=out_scale)
    return pl.pallas_call(
        kern,
        grid=(b, heads, l // tq, nkv),
        in_specs=[pl.BlockSpec(memory_space=pltpu.SMEM),
                  pl.BlockSpec((1, tq, HEAD_DIM), lambda bi, h, i, kv: (bi, i, q_blk + h)),
                  pl.BlockSpec((1, c, HEAD_DIM), lambda bi, h, i, kv: (bi, 0, h)),
                  pl.BlockSpec((1, c, HEAD_DIM), lambda bi, h, i, kv: (bi, 0, heads + h)),
                  pl.BlockSpec((1, tk, HEAD_DIM), lambda bi, h, i, kv: (bi, kv, k_blk + h)),
                  pl.BlockSpec((1, tk, HEAD_DIM), lambda bi, h, i, kv: (bi, kv, v_blk + h)),
                  pl.BlockSpec((1, HEAD_DIM), lambda bi, h, i, kv: (0, 0))],
        out_specs=pl.BlockSpec((1, tq, HEAD_DIM), lambda bi, h, i, kv: (bi, i, h)),
        out_shape=jax.ShapeDtypeStruct((b, l, heads * HEAD_DIM), BF16),
        scratch_shapes=[pltpu.VMEM((2 * tq, HEAD_DIM), BF16),
                        pltpu.VMEM((2 * tq, 1), F32),
                        pltpu.VMEM((2 * tq, 1), F32),
                        pltpu.VMEM((2 * tq, HEAD_DIM), F32)],
        compiler_params=_params("parallel", "parallel", "parallel", "arbitrary"),
        name="diff_attention",
    )(lam, p, pc, pc, p, p, sub_g.reshape(1, HEAD_DIM))


def _winattn_kernel(sink_ref, q_ref, kp_ref, km_ref, kn_ref, vp_ref, vm_ref, vn_ref, kc_ref, vc_ref,
                    o_ref, *, tq, l, group):
    g = pl.program_id(1)
    i = pl.program_id(2)
    kband = jnp.concatenate([kp_ref[0], km_ref[0], kn_ref[0]], axis=0)
    vband = jnp.concatenate([vp_ref[0], vm_ref[0], vn_ref[0]], axis=0)
    nk = tq + 2 * WINDOW
    qpos = i * tq + lax.broadcasted_iota(jnp.int32, (tq, nk), 0)
    kpos = i * tq - WINDOW + lax.broadcasted_iota(jnp.int32, (tq, nk), 1)
    valid = (jnp.abs(kpos - qpos) <= WINDOW) & (kpos >= 0) & (kpos < l)
    kc = kc_ref[0]
    vc = vc_ref[0]
    dn = (((1,), (1,)), ((), ()))
    for r in range(group):
        sl = slice(r * HEAD_DIM, (r + 1) * HEAD_DIM)
        q = q_ref[0, :, sl]
        s_lat = jnp.where(valid, lax.dot_general(q, kband, dn, preferred_element_type=F32), NEG)
        s_ctx = lax.dot_general(q, kc, dn, preferred_element_type=F32)
        sk = sink_ref[g, r]
        m = jnp.maximum(jnp.maximum(jnp.max(s_lat, axis=-1, keepdims=True),
                                    jnp.max(s_ctx, axis=-1, keepdims=True)), sk)
        p_lat = jnp.exp(s_lat - m)
        p_ctx = jnp.exp(s_ctx - m)
        denom = (jnp.sum(p_lat, axis=-1, keepdims=True) + jnp.sum(p_ctx, axis=-1, keepdims=True)
                 + jnp.exp(sk - m))
        o = (jnp.dot(p_lat.astype(BF16), vband, preferred_element_type=F32)
             + jnp.dot(p_ctx.astype(BF16), vc, preferred_element_type=F32))
        o_ref[0, :, sl] = (o / denom).astype(o_ref.dtype)


def _winattn(sink, p, pc, *, a_heads, b_heads, l, c):
    b = p.shape[0]
    group = b_heads // B_KV_HEADS
    tq = min(256, l)
    wpb = tq // WINDOW
    nwb = l // WINDOW
    gw = group * HEAD_DIM
    q_blk = a_heads * HEAD_DIM // gw
    k_blk = (a_heads + b_heads) + 2 * a_heads
    v_blk = k_blk + B_KV_HEADS
    kc_blk = 2 * a_heads
    vc_blk = kc_blk + B_KV_HEADS
    kern = functools.partial(_winattn_kernel, tq=tq, l=l, group=group)

    def prev_map(col):
        return lambda bi, g, i: (bi, jnp.maximum(i * wpb - 1, 0), col + g)

    def main_map(col):
        return lambda bi, g, i: (bi, i, col + g)

    def next_map(col):
        return lambda bi, g, i: (bi, jnp.minimum((i + 1) * wpb, nwb - 1), col + g)

    return pl.pallas_call(
        kern,
        grid=(b, B_KV_HEADS, l // tq),
        in_specs=[pl.BlockSpec(memory_space=pltpu.SMEM),
                  pl.BlockSpec((1, tq, gw), lambda bi, g, i: (bi, i, q_blk + g)),
                  pl.BlockSpec((1, WINDOW, HEAD_DIM), prev_map(k_blk)),
                  pl.BlockSpec((1, tq, HEAD_DIM), main_map(k_blk)),
                  pl.BlockSpec((1, WINDOW, HEAD_DIM), next_map(k_blk)),
                  pl.BlockSpec((1, WINDOW, HEAD_DIM), prev_map(v_blk)),
                  pl.BlockSpec((1, tq, HEAD_DIM), main_map(v_blk)),
                  pl.BlockSpec((1, WINDOW, HEAD_DIM), next_map(v_blk)),
                  pl.BlockSpec((1, c, HEAD_DIM), lambda bi, g, i: (bi, 0, kc_blk + g)),
                  pl.BlockSpec((1, c, HEAD_DIM), lambda bi, g, i: (bi, 0, vc_blk + g))],
        out_specs=pl.BlockSpec((1, tq, gw), lambda bi, g, i: (bi, i, g)),
        out_shape=jax.ShapeDtypeStruct((b, l, b_heads * HEAD_DIM), BF16),
        compiler_params=_params("parallel", "parallel", "parallel"),
        name="window_attention",
    )(sink, p, p, p, p, p, p, p, pc, pc)


def _mm_resid_kernel(*refs, nparts):
    a_refs = refs[0:nparts]
    w_refs = refs[nparts:2 * nparts]
    res_ref, gate_ref, o_ref = refs[2 * nparts:]
    y = jnp.dot(a_refs[0][0].astype(BF16), w_refs[0][0], preferred_element_type=F32)
    for a_ref, w_ref in zip(a_refs[1:], w_refs[1:]):
        y = y + jnp.dot(a_ref[0].astype(BF16), w_ref[0], preferred_element_type=F32)
    o_ref[0] = res_ref[0] + gate_ref[0] * y


def _mm_resid(parts, ws, res, gate):
    b, l, n = res.shape
    tm = min(1024, l)
    tn = min(512, n)
    nparts = len(parts)
    ws = [w[None] for w in ws]
    in_specs = ([pl.BlockSpec((1, tm, a.shape[2]), lambda bi, i, j: (bi, i, 0)) for a in parts]
                + [pl.BlockSpec((1, w.shape[1], tn), lambda bi, i, j: (0, 0, j)) for w in ws]
                + [pl.BlockSpec((1, tm, tn), lambda bi, i, j: (bi, i, j)),
                   pl.BlockSpec((1, 1, tn), lambda bi, i, j: (bi, 0, j))])
    return pl.pallas_call(
        functools.partial(_mm_resid_kernel, nparts=nparts),
        grid=(b, l // tm, n // tn),
        in_specs=in_specs,
        out_specs=pl.BlockSpec((1, tm, tn), lambda bi, i, j: (bi, i, j)),
        out_shape=jax.ShapeDtypeStruct((b, l, n), F32),
        compiler_params=_params("parallel", "parallel", "arbitrary"),
        name="out_projection_gated_residual",
    )(*parts, *ws, res, gate)


HALO = BF16_SUBLANES


def _fill_rows(hbuf, hp_ref, hm_ref, hn_ref, tm):
    hbuf[0:HALO, :] = hp_ref[0]
    hbuf[HALO:HALO + tm, :] = hm_ref[0]
    hbuf[HALO + tm:2 * HALO + tm, :] = hn_ref[0]


def _conv3(zbuf, cw_ref, cb_ref, tm, first, last):
    lo = slice(HALO - 1, HALO)
    hi = slice(HALO + tm, HALO + tm + 1)
    zbuf[lo, :] = jnp.where(first, 0.0, zbuf[lo, :])
    zbuf[hi, :] = jnp.where(last, 0.0, zbuf[hi, :])
    z = zbuf[...]
    rows = z.shape[0]
    g = (z * cw_ref[1:2, :] + pltpu.roll(z * cw_ref[0:1, :], 1, 0) + pltpu.roll(z * cw_ref[2:3, :], rows - 1, 0))
    return g[HALO:HALO + tm] + cb_ref[...]


def _gelu_tanh(x):
    return 0.5 * x * (1.0 + jnp.tanh(math.sqrt(2.0 / math.pi) * (x + 0.044715 * (x * x * x))))


def _ffn_up_kernel(hp_ref, hm_ref, hn_ref, wg_ref, wv_ref, cw_ref, cb_ref, o_ref, hbuf, zbuf, zbuf2, *, tm, nt):
    i = pl.program_id(1)
    j = pl.program_id(2)

    @pl.when(j == 0)
    def _():
        _fill_rows(hbuf, hp_ref, hm_ref, hn_ref, tm)

    hw = zbuf.shape[1]
    for hh, zb in enumerate((zbuf, zbuf2)):
        cs = slice(hh * hw, (hh + 1) * hw)
        zb[...] = jnp.dot(hbuf[...], wg_ref[0, :, cs], preferred_element_type=F32)
        g = _conv3(zb, cw_ref.at[:, cs], cb_ref.at[:, cs], tm, i == 0, i == nt - 1)
        v = jnp.dot(hbuf[HALO:HALO + tm, :], wv_ref[0, :, cs], preferred_element_type=F32)
        o_ref[0, :, cs] = (_gelu_tanh(g) * v).astype(o_ref.dtype)


def _hy_in_kernel(hp_ref, hm_ref, hn_ref, w_ref, b_ref, cw_ref, cb_ref, o_ref, hbuf, zbuf, zbuf2, *, tm, nt):
    i = pl.program_id(1)
    j = pl.program_id(2)

    @pl.when(j == 0)
    def _():
        _fill_rows(hbuf, hp_ref, hm_ref, hn_ref, tm)

    hw = zbuf.shape[1]
    for hh, zb in enumerate((zbuf, zbuf2)):
        cs = slice(hh * hw, (hh + 1) * hw)
        zb[...] = jnp.dot(hbuf[...], w_ref[0, :, cs], preferred_element_type=F32) + b_ref[:, cs]
        o_ref[0, :, cs] = _conv3(zb, cw_ref.at[:, cs], cb_ref.at[:, cs], tm, i == 0, i == nt - 1
                                 ).astype(o_ref.dtype)


def _mm_conv(kernel, h, ws, vecs, n, out_dtype, name, nz=1):
    b, l, k = h.shape
    tm = min(512, l)
    tn = min(512, n)
    nt = l // tm
    hpb = tm // HALO
    nhb = l // HALO
    ws = [w[None] for w in ws]
    in_specs = ([pl.BlockSpec((1, HALO, k), lambda bi, i, j: (bi, jnp.maximum(i * hpb - 1, 0), 0)),
                 pl.BlockSpec((1, tm, k), lambda bi, i, j: (bi, i, 0)),
                 pl.BlockSpec((1, HALO, k), lambda bi, i, j: (bi, jnp.minimum((i + 1) * hpb, nhb - 1), 0))]
                + [pl.BlockSpec((1, k, tn), lambda bi, i, j: (0, 0, j)) for _ in ws]
                + [pl.BlockSpec((v.shape[0], tn), lambda bi, i, j: (0, j)) for v in vecs])
    return pl.pallas_call(
        functools.partial(kernel, tm=tm, nt=nt),
        grid=(b, nt, n // tn),
        in_specs=in_specs,
        out_specs=pl.BlockSpec((1, tm, tn), lambda bi, i, j: (bi, i, j)),
        out_shape=jax.ShapeDtypeStruct((b, l, n), out_dtype),
        scratch_shapes=[pltpu.VMEM((tm + 2 * HALO, k), BF16)] + [pltpu.VMEM((tm + 2 * HALO, tn // nz), F32)] * nz,
        compiler_params=_params("parallel", "parallel", "arbitrary"),
        name=name,
    )(h, h, h, *ws, *vecs)


FILT_NB = BF16_SUBLANES
FILT_K1 = FFT_N1 // 2 + 8


def _filter_dft_kernel(z_ref, w0_ref, b0_ref, w1_ref, b1_ref, w2_ref, b2_ref, fr_ref, w3_ref, dl_ref, t_ref,
                       o_ref, s_ref, a3_s, r_s):
    m, seg, j = pl.program_id(0), pl.program_id(1), pl.program_id(2)
    tn = o_ref.shape[3]
    fr = fr_ref[...]

    def layer(a, w_ref, b_ref):
        return jnp.sin(fr * (jnp.dot(a, w_ref[...], preferred_element_type=F32, precision=HIGHEST)
                             + b_ref[...]))

    @pl.when((seg == 0) & (j == 0))
    def _():
        for jj in range(FILT_NB):
            a = layer(layer(layer(z_ref[jj], w0_ref, b0_ref), w1_ref, b1_ref), w2_ref, b2_ref)
            a_hi = a.astype(BF16)
            a_lo = (a - a_hi.astype(F32)).astype(BF16)
            a3_s[jj] = jnp.concatenate([a_hi, a_hi, a_lo], axis=1)

    @pl.when((m == 0) & (seg == 0) & (j == 0))
    def _():
        s_ref[...] = jnp.zeros(s_ref.shape, F32)

    dl = dl_ref[...]
    backward = seg % 2 == 1
    abs_sum = jnp.zeros((1, tn), F32)
    for jj in range(FILT_NB):
        h = (jnp.dot(a3_s[jj], w3_ref[0], preferred_element_type=F32)
             * jnp.exp(-z_ref[jj][:, 0:1] * dl))
        if jj == 0:
            row = lax.broadcasted_iota(jnp.int32, h.shape, 0)
            h = jnp.where(backward & (m == 0) & (row == 0), 0.0, h)
        abs_sum = abs_sum + jnp.sum(jnp.abs(h), axis=0, keepdims=True)
        r_s[jj] = jnp.dot(t_ref[jj], h.astype(BF16), preferred_element_type=F32)
    o_ref[0] = jnp.swapaxes(r_s[...], 0, 1).astype(o_ref.dtype)
    col = pl.multiple_of(j * tn, tn)
    s_ref[seg, :, pl.ds(col, tn)] += abs_sum


def _hyena_filter_spectra_a(l, d, taf, f_w0, f_b0, f_w1, f_b1, f_w2, f_b2, f_freq, f_w3):
    emb, fw = f_w0.shape
    bands = (emb - 1) // 2
    half = FFT_N1 // 2
    n2 = l // half
    t = np.linspace(0.0, 1.0, l)[:, None]
    w = 2.0 * math.pi * np.arange(l)[:, None] / l
    f = np.linspace(1e-4, bands - 1, bands)[None, :]
    emb_pad = -(-emb // 8) * 8
    z = np.concatenate([t, np.cos(f * w), -np.sin(f * w), np.zeros((l, emb_pad - emb))], axis=-1)
    z = jnp.asarray(z.reshape(half, n2, emb_pad).transpose(1, 0, 2), F32)
    w0 = jnp.concatenate([f_w0, jnp.zeros((emb_pad - emb, fw), F32)], axis=0)
    max_decay = math.log(HY_DECAY_TARGET) / HY_MAX_DECAY_PCT
    min_decay = math.log(HY_DECAY_TARGET) / HY_MIN_DECAY_PCT
    dl = jnp.asarray(np.abs(np.linspace(min_decay, max_decay, d))[None, :], F32)
    nseg = 2 * HY_ORDER
    w3 = jnp.transpose(f_w3.reshape(fw, nseg, d), (1, 0, 2))
    w3_hi = w3.astype(BF16)
    w3_lo = (w3 - w3_hi.astype(F32)).astype(BF16)
    w3 = jnp.concatenate([w3_hi, w3_lo, w3_hi], axis=1)
    nb = min(FILT_NB, n2)
    assert nb == FILT_NB
    tn = min(256, d)
    vec = lambda a: a.reshape(1, fw)
    small = lambda shape: pl.BlockSpec(shape, lambda m, sg, j: (0,) * len(shape))
    return pl.pallas_call(
        _filter_dft_kernel,
        grid=(n2 // nb, nseg, d // tn),
        in_specs=[pl.BlockSpec((nb, half, emb_pad), lambda m, sg, j: (m, 0, 0)),
                  small((emb_pad, fw)), small((1, fw)), small((fw, fw)), small((1, fw)),
                  small((fw, fw)), small((1, fw)), small((1, fw)),
                  pl.BlockSpec((1, 3 * fw, tn), lambda m, sg, j: (sg, 0, j)),
                  pl.BlockSpec((1, tn), lambda m, sg, j: (0, j)),
                  pl.BlockSpec((nb, 2 * FILT_K1, half), lambda m, sg, j: (m, 0, 0))],
        out_specs=[pl.BlockSpec((1, 2 * FILT_K1, nb, tn), lambda m, sg, j: (sg, 0, m, j)),
                   pl.BlockSpec((nseg, 1, d), lambda m, sg, j: (0, 0, 0))],
        out_shape=[jax.ShapeDtypeStruct((nseg, 2 * FILT_K1, n2, d), BF16),
                   jax.ShapeDtypeStruct((nseg, 1, d), F32)],
        scratch_shapes=[pltpu.VMEM((nb, half, 3 * fw), BF16),
                        pltpu.VMEM((nb, 2 * FILT_K1, tn), F32)],
        compiler_params=_params("arbitrary", "arbitrary", "arbitrary"),
        name="hyena_filter_dft_stage_a",
    )(z, w0, vec(f_b0), f_w1, vec(f_b1), f_w2, vec(f_b2), vec(f_freq), w3, dl, taf)


def _dft_tables(l):
    n1 = FFT_N1
    n = 2 * l
    n2 = n // n1
    k1 = np.arange(n1)[:, None]
    f1 = np.exp(-2j * np.pi * k1 * np.arange(n1 // 2)[None, :] / n1)
    tw = np.exp(-2j * np.pi * np.arange(n2)[:, None] * np.arange(n1)[None, :] / n)
    f1r, f1i = jnp.asarray(f1.real, F32), jnp.asarray(f1.imag, F32)
    twr, twi = jnp.asarray(tw.real, F32)[:, :, None], jnp.asarray(tw.imag, F32)[:, :, None]
    fr = twr * f1r - twi * f1i
    fi = twr * f1i + twi * f1r
    ta = jnp.concatenate([jnp.concatenate([fr, -fi], axis=2),
                          jnp.concatenate([fi, fr], axis=2)], axis=1)
    fb = np.exp(-2j * np.pi * np.arange(n2)[:, None] * np.arange(n2)[None, :] / n2)
    tb = np.block([[fb.real, -fb.imag], [fb.imag, fb.real]])
    taf = jnp.concatenate([ta[:, 0:FILT_K1, 0:n1 // 2], ta[:, n1:n1 + FILT_K1, 0:n1 // 2]], axis=1)
    fm = np.exp(-2j * np.pi * (np.arange(n2)[:, None] + 1) * np.arange(n2)[None, :] / n2)
    tb_mirror = np.block([[fm.real, fm.imag], [fm.imag, -fm.real]])
    return (ta.astype(BF16), jnp.swapaxes(ta, 1, 2).astype(BF16), jnp.asarray(tb, BF16), jnp.asarray(tb.T, BF16),
            taf.astype(BF16), jnp.asarray(np.stack([tb, tb_mirror]), BF16))


FFT_NB = BF16_SUBLANES


def _fft_a_kernel(x_ref, t_ref, o_ref, r_s):
    planes = x_ref.shape[1]
    xt = [jnp.swapaxes(x_ref[0, c], 0, 1) for c in range(planes)]
    for jj in range(FFT_NB):
        x = jnp.concatenate([xt[c][jj] for c in range(planes)], axis=0).astype(BF16)
        r_s[jj] = jnp.dot(t_ref[jj], x, preferred_element_type=F32)
    o_ref[0] = jnp.swapaxes(r_s[...], 0, 1).astype(o_ref.dtype)


def _fft_a(x5, ta, *, n2, d, seg):
    s, p, half, _, _ = x5.shape
    kdim = p * half
    tn = min(256, d)
    sb, db = (seg * d) // tn, d // tn
    return pl.pallas_call(
        _fft_a_kernel,
        grid=(s, n2 // FFT_NB, db),
        in_specs=[pl.BlockSpec((1, p, half, FFT_NB, tn), lambda si, m, j: (si, 0, 0, m, sb + j)),
                  pl.BlockSpec((FFT_NB, 2 * FFT_N1, kdim), lambda si, m, j: (m, 0, 0))],
        out_specs=pl.BlockSpec((1, 2 * FFT_N1, FFT_NB, tn), lambda si, m, j: (si, 0, m, j)),
        out_shape=jax.ShapeDtypeStruct((s, 2 * FFT_N1, n2, d), BF16),
        scratch_shapes=[pltpu.VMEM((FFT_NB, 2 * FFT_N1, tn), F32)],
        compiler_params=_params("parallel", "parallel", "parallel"),
        name="dft_stage_a",
    )(x5, ta)


def _fft_b_kernel(a_ref, af_ref, ab_ref, sc_ref, tb_ref, tbf_ref, tbt_ref, o_ref, *, n2):
    z = jnp.dot(tb_ref[...], a_ref[...].reshape(2 * n2, -1), preferred_element_type=F32)
    tbf = tbf_ref[0]
    hf = jnp.dot(tbf, af_ref[0].reshape(2 * n2, -1), preferred_element_type=F32)
    hb = jnp.dot(tbf, ab_ref[0].reshape(2 * n2, -1), preferred_element_type=F32)
    sc = sc_ref[0]
    gr = (hf[0:n2] + hb[0:n2]) * sc
    gi = (hf[n2:2 * n2] - hb[n2:2 * n2]) * sc
    zr, zi = z[0:n2], z[n2:2 * n2]
    prod = jnp.concatenate([zr * gr - zi * gi, zr * gi + zi * gr], axis=0).astype(BF16)
    y = jnp.dot(tbt_ref[...], prod, preferred_element_type=F32)
    o_ref[...] = y.reshape(o_ref.shape).astype(o_ref.dtype)


def _fft_b(a, af, scale, tb, tbf, tbt, order, *, n2, d):
    tn = min(2048, d)
    blk = (2, 1, n2, tn)
    fblk = (1, 2, 1, n2, tn)
    mirrored = lambda k: k > FFT_N1 // 2
    fk = lambda k: jnp.where(mirrored(k), FFT_N1 - k, k)
    return pl.pallas_call(
        functools.partial(_fft_b_kernel, n2=n2),
        grid=(FFT_N1, d // tn),
        in_specs=[pl.BlockSpec(blk, lambda k, j: (0, k, 0, j)),
                  pl.BlockSpec(fblk, lambda k, j: (2 * order, 0, fk(k), 0, j)),
                  pl.BlockSpec(fblk, lambda k, j: (2 * order + 1, 0, fk(k), 0, j)),
                  pl.BlockSpec((1, 1, tn), lambda k, j: (order, 0, j)),
                  pl.BlockSpec((2 * n2, 2 * n2), lambda k, j: (0, 0)),
                  pl.BlockSpec((1, 2 * n2, 2 * n2), lambda k, j: (jnp.where(mirrored(k), 1, 0), 0, 0)),
                  pl.BlockSpec((2 * n2, 2 * n2), lambda k, j: (0, 0))],
        out_specs=pl.BlockSpec(blk, lambda k, j: (0, k, 0, j)),
        out_shape=jax.ShapeDtypeStruct((2, FFT_N1, n2, d), BF16),
        compiler_params=_params("parallel", "parallel"),
        name="dft_stage_b_spectrum_product",
    )(a, af, af, scale, tb, tbf, tbt)


def _fft_ainv_kernel(y_ref, t_ref, u_ref, gate_ref, d_ref, o_ref, r_s):
    half = o_ref.shape[1]
    yt = jnp.swapaxes(y_ref[...].astype(F32), 0, 1).astype(BF16)
    for jj in range(FFT_NB):
        x = jnp.dot(t_ref[jj], yt[jj], preferred_element_type=F32)
        for c in range(2):
            r_s[c, jj] = x[c * half:(c + 1) * half]
    for c in range(2):
        o_ref[c] = gate_ref[c] * (jnp.swapaxes(r_s[c], 0, 1) + d_ref[...] * u_ref[c])


def _fft_ainv(y3, tat, u4, gate4, dskip, *, n2, d, useg, gseg):
    b, half, _, _ = u4.shape
    tn = min(128, d)
    db = d // tn
    usb, gsb = (useg * d) // tn, (gseg * d) // tn
    return pl.pallas_call(
        _fft_ainv_kernel,
        grid=(n2 // FFT_NB, db),
        in_specs=[pl.BlockSpec((2 * FFT_N1, FFT_NB, tn), lambda m, j: (0, m, j)),
                  pl.BlockSpec((FFT_NB, FFT_N1, 2 * FFT_N1), lambda m, j: (m, 0, 0)),
                  pl.BlockSpec((b, half, FFT_NB, tn), lambda m, j: (0, 0, m, usb + j)),
                  pl.BlockSpec((b, half, FFT_NB, tn), lambda m, j: (0, 0, m, gsb + j)),
                  pl.BlockSpec((1, tn), lambda m, j: (0, j))],
        out_specs=pl.BlockSpec((b, half, FFT_NB, tn), lambda m, j: (0, 0, m, j)),
        out_shape=jax.ShapeDtypeStruct((b, half, n2, d), F32),
        scratch_shapes=[pltpu.VMEM((b, FFT_NB, half, tn), F32)],
        compiler_params=_params("parallel", "parallel"),
        name="dft_stage_a_inverse_gate",
    )(y3, tat, u4, gate4, dskip)


def _hyena_mixer(h, hy, l, d):
    (w_in, b_in, sconv_w, sconv_b, f_w0, f_b0, f_w1, f_b1, f_w2, f_b2, f_freq, f_w3, d_skip, w_out) = hy
    b = h.shape[0]
    assert b == 2, "the two batch rows ride the real / imaginary planes of one complex DFT"
    n1 = FFT_N1
    half = n1 // 2
    n2 = (2 * l) // n1
    assert half * n2 == l
    z3 = _mm_conv(_hy_in_kernel, h, [w_in.astype(BF16)],
                  [b_in.reshape(1, -1), sconv_w, sconv_b.reshape(1, -1)], 3 * d, F32, "hyena_in_proj_conv", nz=2)
    ta, tat, tb, tbt, taf, tbf = _dft_tables(l)
    nseg = 2 * HY_ORDER
    af, fsum = _hyena_filter_spectra_a(l, d, taf, f_w0, f_b0, f_w1, f_b1, f_w2, f_b2, f_freq, f_w3)
    af = af.reshape(nseg, 2, FILT_K1, n2, d)
    scale = 1.0 / ((fsum[0::2] + fsum[1::2]) * (2 * l))
    z4 = z3.reshape(b, half, n2, 3 * d)
    y, yseg = z4, 0
    for o in range(HY_ORDER):
        a = _fft_a(y[None], ta, n2=n2, d=d, seg=yseg)
        yb = _fft_b(a.reshape(2, n1, n2, d), af, scale, tb, tbf, tbt, o, n2=n2, d=d)
        y = _fft_ainv(yb.reshape(2 * n1, n2, d), tat, y, z4, d_skip[o].reshape(1, d),
                      n2=n2, d=d, useg=yseg, gseg=o + 1)
        yseg = 0
    return y.reshape(b, l, d), w_out.astype(BF16)


def _rope_tables(l):
    t = jnp.arange(l)
    row = (t // GRID_W).astype(F32)[:, None]
    col = (t % GRID_W).astype(F32)[:, None]
    width = 2 * HEAD_DIM
    cos, sin, avg, perm = [], [], [], []
    for dim in (A_HALF, HEAD_DIM):
        nf = dim // 4
        inv = jnp.asarray(ROPE_BASE ** (-np.arange(nf) / nf), F32)[None, :]
        ar, ac = row * inv, col * inv
        c = jnp.concatenate([jnp.cos(ar), jnp.cos(ar), jnp.cos(ac), jnp.cos(ac)], axis=1)
        s = jnp.concatenate([-jnp.sin(ar), jnp.sin(ar), -jnp.sin(ac), jnp.sin(ac)], axis=1)
        reps = width // dim
        cos.append(jnp.tile(c, (1, reps)))
        sin.append(jnp.tile(s, (1, reps)))
        lane = np.arange(width)
        avg.append((lane[:, None] // dim == lane[None, :] // dim) / dim)
        partner = np.where(lane % (2 * nf) < nf, lane + nf, lane - nf)
        perm.append((lane[:, None] == partner[None, :]).astype(np.float64))
    return (jnp.stack(cos), jnp.stack(sin), jnp.asarray(np.stack(avg), BF16), jnp.asarray(np.stack(perm), BF16))


def _attn_mixer(h, hc, lam_init, w_in, w_out, a_q_g, a_k_g, lq1, lk1, lq2, lk2, a_sub_g, b_q_g, b_k_g, b_sink):
    b, l, d = h.shape
    c = hc.shape[1]
    a_heads = d // (2 * HEAD_DIM)
    b_heads = d // (2 * HEAD_DIM)
    q_cols = (a_heads + b_heads) * HEAD_DIM
    in_cols = w_in.shape[1]
    lam = (jnp.exp(jnp.sum(lq1 * lk1)) - jnp.exp(jnp.sum(lq2 * lk2)) + lam_init).reshape(1, 1)
    sink = b_sink.reshape(B_KV_HEADS, b_heads // B_KV_HEADS)

    ones = lambda n: jnp.ones((n,), F32)
    tile = lambda v, n: jnp.tile(v, n)
    gain = jnp.concatenate([tile(a_q_g, 2 * a_heads), tile(b_q_g, b_heads), tile(a_k_g, 2 * a_heads),
                            ones(a_heads * HEAD_DIM), tile(b_k_g, B_KV_HEADS),
                            ones(B_KV_HEADS * HEAD_DIM)]).reshape(1, in_cols)
    scale = jnp.concatenate([jnp.full((a_heads * HEAD_DIM,), A_HALF ** -0.5 * math.log2(math.e), F32),
                             jnp.full((b_heads * HEAD_DIM,), HEAD_DIM ** -0.5, F32),
                             ones(in_cols - q_cols)]).reshape(1, in_cols)
    e = [0, a_heads // 2, (a_heads + b_heads) // 2, (2 * a_heads + b_heads) // 2,
         (3 * a_heads + b_heads) // 2, (3 * a_heads + b_heads) // 2 + 1, (3 * a_heads + b_heads) // 2 + 2]
    plain = ((e[3], e[4]), (e[5], e[6]))
    cos_tab, sin_tab, avg, perm = _rope_tables(l)

    def type_of(j0):
        def f(j):
            jj = j + j0
            is_a = (jj < e[1]) | ((jj >= e[2]) & (jj < e[3]))
            return jnp.where(is_a, 0, 1)
        return f

    w_bf = w_in.astype(BF16)[None]
    p = _inproj(h.reshape(b * l, d), w_bf, gain, scale, cos_tab, sin_tab, avg, perm, plain=plain,
                type_of_block=type_of(0), col0=0, ncols=in_cols, l=l, rope=True).reshape(b, l, in_cols)
    kv0 = e[2]
    plain_c = tuple((lo - kv0, hi - kv0) for lo, hi in plain)
    pc = _inproj(hc.reshape(b * c, d), w_bf, gain, scale, cos_tab, sin_tab, avg, perm, plain=plain_c,
                 type_of_block=type_of(kv0), col0=kv0, ncols=in_cols - q_cols, l=c, rope=False
                 ).reshape(b, c, in_cols - q_cols)
    score_bound = (A_HALF ** 0.5 * math.log2(math.e)) * jnp.max(jnp.abs(a_q_g)) * jnp.max(jnp.abs(a_k_g))
    oa = _diffattn(lam, p, pc, a_sub_g, score_bound, heads=a_heads, l=l, c=c, out_scale=1.0 - lam_init)
    ob = _winattn(sink, p, pc, a_heads=a_heads, b_heads=b_heads, l=l, c=c)
    w_out_bf = w_out.astype(BF16)
    na = a_heads * HEAD_DIM
    return [oa, ob], [w_out_bf[:na], w_out_bf[na:]]


def kernel(x, c, ctx, c_ctx, ada_w, ada_b, norm1_g, norm2_g, attn_w_in, attn_w_out, a_q_g, a_k_g, a_lam_q1, a_lam_k1, a_lam_q2, a_lam_k2, a_sub_g, b_q_g, b_k_g, b_sink, hy_w_in, hy_b_in, hy_sconv_w, hy_sconv_b, hy_f_w0, hy_f_b0, hy_f_w1, hy_f_b1, hy_f_w2, hy_f_b2, hy_f_freq, hy_f_w3, hy_d, hy_w_out, ffn_w_gate, ffn_w_val, ffn_conv_w, ffn_conv_b, ffn_w_down):
    b, l, d = x.shape
    depth = ada_w.shape[0]
    cc = jnp.concatenate([c, c_ctx[None, :], jnp.zeros((8 - b - 1, d), F32)], axis=0)
    m = _ada(cc, ada_w, ada_b)
    xs = x
    for layer in range(depth):
        i = layer // 2
        lat = [m[layer, :b, k * d:(k + 1) * d].reshape(b, 1, d) for k in range(6)]
        sh1, sc1, g1, sh2, sc2, g2 = lat
        h = _normmod(xs, norm1_g[layer], sc1, sh1)
        if layer % 2 == 0:
            mc = [jnp.broadcast_to(m[layer, b, k * d:(k + 1) * d].reshape(1, 1, d), (b, 1, d)) for k in range(2)]
            hc = _normmod(ctx, norm1_g[layer], mc[1], mc[0])
            parts, ws = _attn_mixer(h, hc, 0.8 - 0.6 * math.exp(-0.3 * layer), attn_w_in[i], attn_w_out[i],
                                    a_q_g[i], a_k_g[i], a_lam_q1[i], a_lam_k1[i], a_lam_q2[i], a_lam_k2[i],
                                    a_sub_g[i], b_q_g[i], b_k_g[i], b_sink[i])
        else:
            hy = (hy_w_in[i], hy_b_in[i], hy_sconv_w[i], hy_sconv_b[i], hy_f_w0[i], hy_f_b0[i], hy_f_w1[i],
                  hy_f_b1[i], hy_f_w2[i], hy_f_b2[i], hy_f_freq[i], hy_f_w3[i], hy_d[i], hy_w_out[i])
            y, w_o = _hyena_mixer(h, hy, l, d)
            parts, ws = [y], [w_o]
        xs = _mm_resid(parts, ws, xs, g1)
        h2 = _normmod(xs, norm2_g[layer], sc2, sh2)
        d_ff = ffn_w_gate.shape[2]
        hid = _mm_conv(_ffn_up_kernel, h2, [ffn_w_gate[layer].astype(BF16), ffn_w_val[layer].astype(BF16)],
                       [ffn_conv_w[layer], ffn_conv_b[layer].reshape(1, -1)], d_ff, BF16, "ffn_up_conv_glu", nz=2)
        xs = _mm_resid([hid], [ffn_w_down[layer].astype(BF16)], xs, g2)
    return xs
```

```python
import functools
import math

import numpy as np
import jax
import jax.numpy as jnp
from jax import lax
from jax.experimental import pallas as pl
from jax.experimental.pallas import tpu as pltpu

F32 = jnp.float32
BF16 = jnp.bfloat16
HIGHEST = lax.Precision.HIGHEST

HEAD_DIM = 128
A_HALF = HEAD_DIM // 2
B_KV_HEADS = 2
GRID_W = 64
WINDOW = 128
ROPE_BASE = 10000.0
EPS = 1e-6
NEG = -1e30
HY_ORDER = 2
HY_DECAY_TARGET = 1e-2
HY_MAX_DECAY_PCT = 0.3
HY_MIN_DECAY_PCT = 1.5

BF16_SUBLANES = 16
FFT_N1 = 256
VMEM_LIMIT_BYTES = 56 * 1024 * 1024


def _params(*sem):
    return pltpu.CompilerParams(dimension_semantics=sem, vmem_limit_bytes=VMEM_LIMIT_BYTES)


def _ada_kernel(c_ref, w_ref, b_ref, o_ref):
    c = c_ref[...]
    s = c * (1.0 / (1.0 + jnp.exp(-c)))
    o_ref[0] = jnp.dot(s, w_ref[0], preferred_element_type=F32, precision=HIGHEST) + b_ref[0]


def _ada(cc, ada_w, ada_b):
    depth, d, n = ada_w.shape
    tn = 1024
    return pl.pallas_call(
        _ada_kernel,
        grid=(depth, n // tn),
        in_specs=[pl.BlockSpec((8, d), lambda l, j: (0, 0)),
                  pl.BlockSpec((1, d, tn), lambda l, j: (l, 0, j)),
                  pl.BlockSpec((1, 1, tn), lambda l, j: (l, 0, j))],
        out_specs=pl.BlockSpec((1, 8, tn), lambda l, j: (l, 0, j)),
        out_shape=jax.ShapeDtypeStruct((depth, 8, n), F32),
        compiler_params=_params("parallel", "parallel"),
        name="ada_modulation",
    )(cc, ada_w, ada_b.reshape(depth, 1, n))


NORM_ROWS = 32


def _normmod_kernel(x_ref, g_ref, sc_ref, sh_ref, o_ref):
    scale = g_ref[...] * (1.0 + sc_ref[0])
    shift = sh_ref[0]
    rows = min(NORM_ROWS, x_ref.shape[1])

    def slab(r, carry):
        rs = pl.ds(pl.multiple_of(r * rows, rows), rows)
        x = x_ref[0, rs, :]
        ms = jnp.mean(x * x, axis=-1, keepdims=True)
        o_ref[0, rs, :] = (x * lax.rsqrt(ms + EPS) * scale + shift).astype(o_ref.dtype)
        return carry

    lax.fori_loop(0, x_ref.shape[1] // rows, slab, 0, unroll=4)


def _normmod(x, g, sc, sh):
    b, l, d = x.shape
    tm = min(512, l)
    return pl.pallas_call(
        _normmod_kernel,
        grid=(b, l // tm),
        in_specs=[pl.BlockSpec((1, tm, d), lambda bi, i: (bi, i, 0)),
                  pl.BlockSpec((1, d), lambda bi, i: (0, 0)),
                  pl.BlockSpec((1, 1, d), lambda bi, i: (bi, 0, 0)),
                  pl.BlockSpec((1, 1, d), lambda bi, i: (bi, 0, 0))],
        out_specs=pl.BlockSpec((1, tm, d), lambda bi, i: (bi, i, 0)),
        out_shape=jax.ShapeDtypeStruct((b, l, d), BF16),
        compiler_params=_params("parallel", "parallel"),
        name="rmsnorm_modulate",
    )(x, g.reshape(1, d), sc, sh)


def _dot_split(x, m):
    hi = x.astype(BF16)
    lo = (x - hi.astype(F32)).astype(BF16)
    return jnp.dot(hi, m, preferred_element_type=F32) + jnp.dot(lo, m, preferred_element_type=F32)


def _inproj_kernel(h_ref, w_ref, gain_ref, scale_ref, cos_ref, sin_ref, avg_ref, perm_ref, o_ref, *, plain, rope):
    j = pl.program_id(1)
    z = jnp.dot(h_ref[...], w_ref[0], preferred_element_type=F32)
    is_plain = functools.reduce(jnp.logical_or, [(j >= lo) & (j < hi) for lo, hi in plain])

    @pl.when(is_plain)
    def _():
        o_ref[...] = z.astype(o_ref.dtype)

    @pl.when(jnp.logical_not(is_plain))
    def _():
        rows = min(256, z.shape[0])
        for r0 in range(0, z.shape[0], rows):
            rs = slice(r0, r0 + rows)
            zc = z[rs]
            ms = _dot_split(zc * zc, avg_ref[0])
            y = zc * lax.rsqrt(ms + EPS) * gain_ref[...]
            if rope:
                y = y * cos_ref[0, rs, :] + _dot_split(y, perm_ref[0]) * sin_ref[0, rs, :]
            o_ref[rs, :] = (y * scale_ref[...]).astype(o_ref.dtype)


def _inproj(h2d, w, gain, scale, cos_tab, sin_tab, avg, perm, *, plain, type_of_block, col0, ncols, l, rope):
    m, d = h2d.shape
    tn = 2 * HEAD_DIM
    tm = min(2048, l)
    pos_blocks = l // tm
    kern = functools.partial(_inproj_kernel, plain=plain, rope=rope)
    by_type = lambda i, j: (type_of_block(j), 0, 0)
    return pl.pallas_call(
        kern,
        grid=(m // tm, ncols // tn),
        in_specs=[pl.BlockSpec((tm, d), lambda i, j: (i, 0)),
                  pl.BlockSpec((1, d, tn), lambda i, j: (0, 0, j + col0)),
                  pl.BlockSpec((1, tn), lambda i, j: (0, j + col0)),
                  pl.BlockSpec((1, tn), lambda i, j: (0, j + col0)),
                  pl.BlockSpec((1, tm, tn), lambda i, j: (type_of_block(j), i % pos_blocks, 0)),
                  pl.BlockSpec((1, tm, tn), lambda i, j: (type_of_block(j), i % pos_blocks, 0)),
                  pl.BlockSpec((1, tn, tn), by_type),
                  pl.BlockSpec((1, tn, tn), by_type)],
        out_specs=pl.BlockSpec((tm, tn), lambda i, j: (i, j)),
        out_shape=jax.ShapeDtypeStruct((m, ncols), BF16),
        compiler_params=_params("parallel", "arbitrary"),
        name="attn_in_projection",
    )(h2d, w, gain, scale, cos_tab, sin_tab, avg, perm)


def _diffattn_kernel(lam_ref, q_ref, kc_ref, vc_ref, k_ref, v_ref, g_ref, o_ref,
                     q2_s, m_s, l_s, acc_s, *, tq, nkv, out_scale):
    kv = pl.program_id(3)

    def process(k, v):
        s = lax.dot_general(q2_s[...], k, (((1,), (1,)), ((), ())), preferred_element_type=F32)
        m_prev = m_s[...]
        m_new = jnp.maximum(m_prev, jnp.max(s, axis=-1, keepdims=True))
        alpha = jnp.exp2(m_prev - m_new)
        p = jnp.exp2(s - m_new)
        l_s[...] = alpha * l_s[...] + jnp.sum(p, axis=-1, keepdims=True)
        acc_s[...] = alpha * acc_s[...] + jnp.dot(p.astype(BF16), v, preferred_element_type=F32)
        m_s[...] = m_new

    @pl.when(kv == 0)
    def _():
        q = q_ref[0]
        lane = lax.broadcasted_iota(jnp.int32, q.shape, 1)
        zero = jnp.zeros_like(q)
        q2_s[0:tq, :] = jnp.where(lane < A_HALF, q, zero)
        q2_s[tq:2 * tq, :] = jnp.where(lane >= A_HALF, q, zero)
        m_s[...] = jnp.full(m_s.shape, -jnp.inf, F32)
        l_s[...] = jnp.zeros(l_s.shape, F32)
        acc_s[...] = jnp.zeros(acc_s.shape, F32)
        process(kc_ref[0], vc_ref[0])

    process(k_ref[0], v_ref[0])

    @pl.when(kv == nkv - 1)
    def _():
        o = acc_s[...] / l_s[...]
        d = o[0:tq] - lam_ref[0, 0] * o[tq:2 * tq]
        ms = jnp.mean(d * d, axis=-1, keepdims=True)
        o_ref[0] = (d * lax.rsqrt(ms + EPS) * g_ref[...] * out_scale).astype(o_ref.dtype)


def _diffattn_fast_kernel(lam_ref, q_ref, kc_ref, vct_ref, k_ref, vt_ref, g_ref, o_ref,
                          q2_s, l_s, acc_s, p_s, *, tq, tk, kc, nkv, out_scale):
    kv = pl.program_id(3)
    dn = (((1,), (1,)), ((), ()))

    def weights(k):
        n = k.shape[0]
        p = jnp.exp2(lax.dot_general(k, q2_s[...], dn, preferred_element_type=F32))
        return p.astype(BF16), jnp.sum(p.reshape(n // 8, 8, 2 * tq), axis=0)

    @pl.when(kv == 0)
    def _():
        q = q_ref[0]
        lane = lax.broadcasted_iota(jnp.int32, q.shape, 1)
        zero = jnp.zeros_like(q)
        q2_s[0:tq, :] = jnp.where(lane < A_HALF, q, zero)
        q2_s[tq:2 * tq, :] = jnp.where(lane >= A_HALF, q, zero)
        p, ls = weights(kc_ref[0])
        l_s[...] = ls
        acc_s[...] = jnp.dot(vct_ref[0, 0], p, preferred_element_type=F32)

    ls = l_s[...]
    for ci in range(tk // kc):
        p, lc = weights(k_ref[0, ci * kc:(ci + 1) * kc, :])
        p_s[ci * kc:(ci + 1) * kc, :] = p
        ls = ls + lc
    l_s[...] = ls
    acc_s[...] += jnp.dot(vt_ref[0, 0], p_s[...], preferred_element_type=F32)

    @pl.when(kv == nkv - 1)
    def _():
        l = jnp.sum(l_s[...], axis=0, keepdims=True)
        o = acc_s[...] / l
        d = o[:, 0:tq] - lam_ref[0, 0] * o[:, tq:2 * tq]
        ms = jnp.mean(d * d, axis=0, keepdims=True)
        y = d * lax.rsqrt(ms + EPS) * g_ref[...] * out_scale
        o_ref[0] = y.T.astype(o_ref.dtype)


def _diffattn_fast(lam, p, pc, sub_g, *, heads, l, c, out_scale):
    b = p.shape[0]
    tq = min(1024, l)
    tk = min(4096, l)
    kc = min(512, tk)
    nkv = l // tk
    k_blk = 2 * heads
    hd = heads * HEAD_DIM
    vt = jnp.transpose(p[:, :, 3 * hd:4 * hd].reshape(b, l, heads, HEAD_DIM), (0, 2, 3, 1))
    vct = jnp.transpose(pc[:, :, hd:2 * hd].reshape(b, c, heads, HEAD_DIM), (0, 2, 3, 1))
    kern = functools.partial(_diffattn_fast_kernel, tq=tq, tk=tk, kc=kc, nkv=nkv, out_scale=out_scale)
    return pl.pallas_call(
        kern,
        grid=(b, heads, l // tq, nkv),
        in_specs=[pl.BlockSpec(memory_space=pltpu.SMEM),
                  pl.BlockSpec((1, tq, HEAD_DIM), lambda bi, h, i, kv: (bi, i, h)),
                  pl.BlockSpec((1, c, HEAD_DIM), lambda bi, h, i, kv: (bi, 0, h)),
                  pl.BlockSpec((1, 1, HEAD_DIM, c), lambda bi, h, i, kv: (bi, h, 0, 0)),
                  pl.BlockSpec((1, tk, HEAD_DIM), lambda bi, h, i, kv: (bi, kv, k_blk + h)),
                  pl.BlockSpec((1, 1, HEAD_DIM, tk), lambda bi, h, i, kv: (bi, h, 0, kv)),
                  pl.BlockSpec((HEAD_DIM, 1), lambda bi, h, i, kv: (0, 0))],
        out_specs=pl.BlockSpec((1, tq, HEAD_DIM), lambda bi, h, i, kv: (bi, i, h)),
        out_shape=jax.ShapeDtypeStruct((b, l, hd), BF16),
        scratch_shapes=[pltpu.VMEM((2 * tq, HEAD_DIM), BF16),
                        pltpu.VMEM((8, 2 * tq), F32),
                        pltpu.VMEM((HEAD_DIM, 2 * tq), F32),
                        pltpu.VMEM((tk, 2 * tq), BF16)],
        compiler_params=_params("parallel", "parallel", "parallel", "arbitrary"),
        name="diff_attention_bounded",
    )(lam, p, pc, vct, p, vt, sub_g.reshape(HEAD_DIM, 1))


SCORE_BOUND_LOG2 = 60.0


def _diffattn(lam, p, pc, sub_g, score_bound, *, heads, l, c, out_scale):
    kw = dict(heads=heads, l=l, c=c, out_scale=out_scale)
    return lax.cond(score_bound < SCORE_BOUND_LOG2,
                    lambda *a: _diffattn_fast(*a, **kw), lambda *a: _diffattn_safe(*a, **kw),
                    lam, p, pc, sub_g)


def _diffattn_safe(lam, p, pc, sub_g, *, heads, l, c, out_scale):
    b = p.shape[0]
    tq = min(256, l)
    tk = min(512, l)
    nkv = l // tk
    q_blk = 0
    k_blk = 2 * heads
    v_blk = 3 * heads
    kern = functools.partial(_diffattn_kernel, tq=tq, nkv=nkv, out_scale=out_scale)
    return pl.pallas_call(
        kern,
        grid=(b, heads, l // tq, nkv),
        in_specs=[pl.BlockSpec(memory_space=pltpu.SMEM),
                  pl.BlockSpec((1, tq, HEAD_DIM), lambda bi, h, i, kv: (bi, i, q_blk + h)),
                  pl.BlockSpec((1, c, HEAD_DIM), lambda bi, h, i, kv: (bi, 0, h)),
                  pl.BlockSpec((1, c, HEAD_DIM), lambda bi, h, i, kv: (bi, 0, heads + h)),
                  pl.BlockSpec((1, tk, HEAD_DIM), lambda bi, h, i, kv: (bi, kv, k_blk + h)),
                  pl.BlockSpec((1, tk, HEAD_DIM), lambda bi, h, i, kv: (bi, kv, v_blk + h)),
                  pl.BlockSpec((1, HEAD_DIM), lambda bi, h, i, kv: (0, 0))],
        out_specs=pl.BlockSpec((1, tq, HEAD_DIM), lambda bi, h, i, kv: (bi, i, h)),
        out_shape=jax.ShapeDtypeStruct((b, l, heads * HEAD_DIM), BF16),
        scratch_shapes=[pltpu.VMEM((2 * tq, HEAD_DIM), BF16),
                        pltpu.VMEM((2 * tq, 1), F32),
                        pltpu.VMEM((2 * tq, 1), F32),
                        pltpu.VMEM((2 * tq, HEAD_DIM), F32)],
        compiler_params=_params("parallel", "parallel", "parallel", "arbitrary"),
        name="diff_attention",
    )(lam, p, pc, pc, p, p, sub_g.reshape(1, HEAD_DIM))


def _winattn_kernel(sink_ref, q_ref, kp_ref, km_ref, kn_ref, vp_ref, vm_ref, vn_ref, kc_ref, vc_ref,
                    o_ref, *, tq, l, group):
    g = pl.program_id(1)
    i = pl.program_id(2)
    kband = jnp.concatenate([kp_ref[0], km_ref[0], kn_ref[0]], axis=0)
    vband = jnp.concatenate([vp_ref[0], vm_ref[0], vn_ref[0]], axis=0)
    nk = tq + 2 * WINDOW
    qpos = i * tq + lax.broadcasted_iota(jnp.int32, (tq, nk), 0)
    kpos = i * tq - WINDOW + lax.broadcasted_iota(jnp.int32, (tq, nk), 1)
    valid = (jnp.abs(kpos - qpos) <= WINDOW) & (kpos >= 0) & (kpos < l)
    kc = kc_ref[0]
    vc = vc_ref[0]
    dn = (((1,), (1,)), ((), ()))
    for r in range(group):
        sl = slice(r * HEAD_DIM, (r + 1) * HEAD_DIM)
        q = q_ref[0, :, sl]
        s_lat = jnp.where(valid, lax.dot_general(q, kband, dn, preferred_element_type=F32), NEG)
        s_ctx = lax.dot_general(q, kc, dn, preferred_element_type=F32)
        sk = sink_ref[g, r]
        m = jnp.maximum(jnp.maximum(jnp.max(s_lat, axis=-1, keepdims=True),
                                    jnp.max(s_ctx, axis=-1, keepdims=True)), sk)
        p_lat = jnp.exp(s_lat - m)
        p_ctx = jnp.exp(s_ctx - m)
        denom = (jnp.sum(p_lat, axis=-1, keepdims=True) + jnp.sum(p_ctx, axis=-1, keepdims=True)
                 + jnp.exp(sk - m))
        o = (jnp.dot(p_lat.astype(BF16), vband, preferred_element_type=F32)
             + jnp.dot(p_ctx.astype(BF16), vc, preferred_element_type=F32))
        o_ref[0, :, sl] = (o / denom).astype(o_ref.dtype)


def _winattn(sink, p, pc, *, a_heads, b_heads, l, c):
    b = p.shape[0]
    group = b_heads // B_KV_HEADS
    tq = min(256, l)
    wpb = tq // WINDOW
    nwb = l // WINDOW
    gw = group * HEAD_DIM
    q_blk = a_heads * HEAD_DIM // gw
    k_blk = (a_heads + b_heads) + 2 * a_heads
    v_blk = k_blk + B_KV_HEADS
    kc_blk = 2 * a_heads
    vc_blk = kc_blk + B_KV_HEADS
    kern = functools.partial(_winattn_kernel, tq=tq, l=l, group=group)

    def prev_map(col):
        return lambda bi, g, i: (bi, jnp.maximum(i * wpb - 1, 0), col + g)

    def main_map(col):
        return lambda bi, g, i: (bi, i, col + g)

    def next_map(col):
        return lambda bi, g, i: (bi, jnp.minimum((i + 1) * wpb, nwb - 1), col + g)

    return pl.pallas_call(
        kern,
        grid=(b, B_KV_HEADS, l // tq),
        in_specs=[pl.BlockSpec(memory_space=pltpu.SMEM),
                  pl.BlockSpec((1, tq, gw), lambda bi, g, i: (bi, i, q_blk + g)),
                  pl.BlockSpec((1, WINDOW, HEAD_DIM), prev_map(k_blk)),
                  pl.BlockSpec((1, tq, HEAD_DIM), main_map(k_blk)),
                  pl.BlockSpec((1, WINDOW, HEAD_DIM), next_map(k_blk)),
                  pl.BlockSpec((1, WINDOW, HEAD_DIM), prev_map(v_blk)),
                  pl.BlockSpec((1, tq, HEAD_DIM), main_map(v_blk)),
                  pl.BlockSpec((1, WINDOW, HEAD_DIM), next_map(v_blk)),
                  pl.BlockSpec((1, c, HEAD_DIM), lambda bi, g, i: (bi, 0, kc_blk + g)),
                  pl.BlockSpec((1, c, HEAD_DIM), lambda bi, g, i: (bi, 0, vc_blk + g))],
        out_specs=pl.BlockSpec((1, tq, gw), lambda bi, g, i: (bi, i, g)),
        out_shape=jax.ShapeDtypeStruct((b, l, b_heads * HEAD_DIM), BF16),
        compiler_params=_params("parallel", "parallel", "parallel"),
        name="window_attention",
    )(sink, p, p, p, p, p, p, p, pc, pc)


def _mm_resid_kernel(*refs, nparts):
    a_refs = refs[0:nparts]
    w_refs = refs[nparts:2 * nparts]
    res_ref, gate_ref, o_ref = refs[2 * nparts:]
    y = jnp.dot(a_refs[0][0].astype(BF16), w_refs[0][0], preferred_element_type=F32)
    for a_ref, w_ref in zip(a_refs[1:], w_refs[1:]):
        y = y + jnp.dot(a_ref[0].astype(BF16), w_ref[0], preferred_element_type=F32)
    o_ref[0] = res_ref[0] + gate_ref[0] * y


def _mm_resid(parts, ws, res, gate):
    b, l, n = res.shape
    tm = min(1024, l)
    tn = min(512, n)
    nparts = len(parts)
    ws = [w[None] for w in ws]
    in_specs = ([pl.BlockSpec((1, tm, a.shape[2]), lambda bi, i, j: (bi, i, 0)) for a in parts]
                + [pl.BlockSpec((1, w.shape[1], tn), lambda bi, i, j: (0, 0, j)) for w in ws]
                + [pl.BlockSpec((1, tm, tn), lambda bi, i, j: (bi, i, j)),
                   pl.BlockSpec((1, 1, tn), lambda bi, i, j: (bi, 0, j))])
    return pl.pallas_call(
        functools.partial(_mm_resid_kernel, nparts=nparts),
        grid=(b, l // tm, n // tn),
        in_specs=in_specs,
        out_specs=pl.BlockSpec((1, tm, tn), lambda bi, i, j: (bi, i, j)),
        out_shape=jax.ShapeDtypeStruct((b, l, n), F32),
        compiler_params=_params("parallel", "parallel", "arbitrary"),
        name="out_projection_gated_residual",
    )(*parts, *ws, res, gate)


HALO = BF16_SUBLANES


def _fill_rows(hbuf, hp_ref, hm_ref, hn_ref, tm):
    hbuf[0:HALO, :] = hp_ref[0]
    hbuf[HALO:HALO + tm, :] = hm_ref[0]
    hbuf[HALO + tm:2 * HALO + tm, :] = hn_ref[0]


def _conv3(zbuf, cw_ref, cb_ref, tm, first, last):
    lo = slice(HALO - 1, HALO)
    hi = slice(HALO + tm, HALO + tm + 1)
    zbuf[lo, :] = jnp.where(first, 0.0, zbuf[lo, :])
    zbuf[hi, :] = jnp.where(last, 0.0, zbuf[hi, :])
    z = zbuf[...]
    rows = z.shape[0]
    g = (z * cw_ref[1:2, :] + pltpu.roll(z * cw_ref[0:1, :], 1, 0) + pltpu.roll(z * cw_ref[2:3, :], rows - 1, 0))
    return g[HALO:HALO + tm] + cb_ref[...]


def _gelu_tanh(x):
    return 0.5 * x * (1.0 + jnp.tanh(math.sqrt(2.0 / math.pi) * (x + 0.044715 * (x * x * x))))


def _ffn_up_kernel(hp_ref, hm_ref, hn_ref, wg_ref, wv_ref, cw_ref, cb_ref, o_ref, hbuf, zbuf, zbuf2, *, tm, nt):
    i = pl.program_id(1)
    j = pl.program_id(2)

    @pl.when(j == 0)
    def _():
        _fill_rows(hbuf, hp_ref, hm_ref, hn_ref, tm)

    hw = zbuf.shape[1]
    for hh, zb in enumerate((zbuf, zbuf2)):
        cs = slice(hh * hw, (hh + 1) * hw)
        zb[...] = jnp.dot(hbuf[...], wg_ref[0, :, cs], preferred_element_type=F32)
        g = _conv3(zb, cw_ref.at[:, cs], cb_ref.at[:, cs], tm, i == 0, i == nt - 1)
        v = jnp.dot(hbuf[HALO:HALO + tm, :], wv_ref[0, :, cs], preferred_element_type=F32)
        o_ref[0, :, cs] = (_gelu_tanh(g) * v).astype(o_ref.dtype)


def _hy_in_kernel(hp_ref, hm_ref, hn_ref, w_ref, b_ref, cw_ref, cb_ref, o_ref, hbuf, zbuf, zbuf2, *, tm, nt):
    i = pl.program_id(1)
    j = pl.program_id(2)

    @pl.when(j == 0)
    def _():
        _fill_rows(hbuf, hp_ref, hm_ref, hn_ref, tm)

    hw = zbuf.shape[1]
    for hh, zb in enumerate((zbuf, zbuf2)):
        cs = slice(hh * hw, (hh + 1) * hw)
        zb[...] = jnp.dot(hbuf[...], w_ref[0, :, cs], preferred_element_type=F32) + b_ref[:, cs]
        o_ref[0, :, cs] = _conv3(zb, cw_ref.at[:, cs], cb_ref.at[:, cs], tm, i == 0, i == nt - 1
                                 ).astype(o_ref.dtype)


def _mm_conv(kernel, h, ws, vecs, n, out_dtype, name, nz=1):
    b, l, k = h.shape
    tm = min(1024, l)
    tn = min(512, n)
    nt = l // tm
    hpb = tm // HALO
    nhb = l // HALO
    ws = [w[None] for w in ws]
    in_specs = ([pl.BlockSpec((1, HALO, k), lambda bi, i, j: (bi, jnp.maximum(i * hpb - 1, 0), 0)),
                 pl.BlockSpec((1, tm, k), lambda bi, i, j: (bi, i, 0)),
                 pl.BlockSpec((1, HALO, k), lambda bi, i, j: (bi, jnp.minimum((i + 1) * hpb, nhb - 1), 0))]
                + [pl.BlockSpec((1, k, tn), lambda bi, i, j: (0, 0, j)) for _ in ws]
                + [pl.BlockSpec((v.shape[0], tn), lambda bi, i, j: (0, j)) for v in vecs])
    return pl.pallas_call(
        functools.partial(kernel, tm=tm, nt=nt),
        grid=(b, nt, n // tn),
        in_specs=in_specs,
        out_specs=pl.BlockSpec((1, tm, tn), lambda bi, i, j: (bi, i, j)),
        out_shape=jax.ShapeDtypeStruct((b, l, n), out_dtype),
        scratch_shapes=[pltpu.VMEM((tm + 2 * HALO, k), BF16)] + [pltpu.VMEM((tm + 2 * HALO, tn // nz), F32)] * nz,
        compiler_params=_params("parallel", "parallel", "arbitrary"),
        name=name,
    )(h, h, h, *ws, *vecs)


FILT_NB = BF16_SUBLANES
FILT_K1 = FFT_N1 // 2 + 8


def _filter_dft_kernel(z_ref, w0_ref, b0_ref, w1_ref, b1_ref, w2_ref, b2_ref, fr_ref, w3_ref, dl_ref, t_ref,
                       o_ref, s_ref, a3_s, r_s):
    m, seg, j = pl.program_id(0), pl.program_id(1), pl.program_id(2)
    tn = o_ref.shape[3]
    fr = fr_ref[...]

    def layer(a, w_ref, b_ref):
        return jnp.sin(fr * (jnp.dot(a, w_ref[...], preferred_element_type=F32, precision=HIGHEST)
                             + b_ref[...]))

    @pl.when((seg == 0) & (j == 0))
    def _():
        for jj in range(FILT_NB):
            a = layer(layer(layer(z_ref[jj], w0_ref, b0_ref), w1_ref, b1_ref), w2_ref, b2_ref)
            a_hi = a.astype(BF16)
            a_lo = (a - a_hi.astype(F32)).astype(BF16)
            a3_s[jj] = jnp.concatenate([a_hi, a_hi, a_lo], axis=1)

    @pl.when((m == 0) & (seg == 0) & (j == 0))
    def _():
        s_ref[...] = jnp.zeros(s_ref.shape, F32)

    dl = dl_ref[...]
    backward = seg % 2 == 1
    abs_sum = jnp.zeros((1, tn), F32)
    for jj in range(FILT_NB):
        h = (jnp.dot(a3_s[jj], w3_ref[0], preferred_element_type=F32)
             * jnp.exp(-z_ref[jj][:, 0:1] * dl))
        if jj == 0:
            row = lax.broadcasted_iota(jnp.int32, h.shape, 0)
            h = jnp.where(backward & (m == 0) & (row == 0), 0.0, h)
        abs_sum = abs_sum + jnp.sum(jnp.abs(h), axis=0, keepdims=True)
        r_s[jj] = jnp.dot(t_ref[jj], h.astype(BF16), preferred_element_type=F32)
    o_ref[0] = jnp.swapaxes(r_s[...], 0, 1).astype(o_ref.dtype)
    col = pl.multiple_of(j * tn, tn)
    s_ref[seg, :, pl.ds(col, tn)] += abs_sum


def _hyena_filter_spectra_a(l, d, taf, f_w0, f_b0, f_w1, f_b1, f_w2, f_b2, f_freq, f_w3):
    emb, fw = f_w0.shape
    bands = (emb - 1) // 2
    half = FFT_N1 // 2
    n2 = l // half
    t = np.linspace(0.0, 1.0, l)[:, None]
    w = 2.0 * math.pi * np.arange(l)[:, None] / l
    f = np.linspace(1e-4, bands - 1, bands)[None, :]
    emb_pad = -(-emb // 8) * 8
    z = np.concatenate([t, np.cos(f * w), -np.sin(f * w), np.zeros((l, emb_pad - emb))], axis=-1)
    z = jnp.asarray(z.reshape(half, n2, emb_pad).transpose(1, 0, 2), F32)
    w0 = jnp.concatenate([f_w0, jnp.zeros((emb_pad - emb, fw), F32)], axis=0)
    max_decay = math.log(HY_DECAY_TARGET) / HY_MAX_DECAY_PCT
    min_decay = math.log(HY_DECAY_TARGET) / HY_MIN_DECAY_PCT
    dl = jnp.asarray(np.abs(np.linspace(min_decay, max_decay, d))[None, :], F32)
    nseg = 2 * HY_ORDER
    w3 = jnp.transpose(f_w3.reshape(fw, nseg, d), (1, 0, 2))
    w3_hi = w3.astype(BF16)
    w3_lo = (w3 - w3_hi.astype(F32)).astype(BF16)
    w3 = jnp.concatenate([w3_hi, w3_lo, w3_hi], axis=1)
    nb = min(FILT_NB, n2)
    assert nb == FILT_NB
    tn = min(256, d)
    vec = lambda a: a.reshape(1, fw)
    small = lambda shape: pl.BlockSpec(shape, lambda m, sg, j: (0,) * len(shape))
    return pl.pallas_call(
        _filter_dft_kernel,
        grid=(n2 // nb, nseg, d // tn),
        in_specs=[pl.BlockSpec((nb, half, emb_pad), lambda m, sg, j: (m, 0, 0)),
                  small((emb_pad, fw)), small((1, fw)), small((fw, fw)), small((1, fw)),
                  small((fw, fw)), small((1, fw)), small((1, fw)),
                  pl.BlockSpec((1, 3 * fw, tn), lambda m, sg, j: (sg, 0, j)),
                  pl.BlockSpec((1, tn), lambda m, sg, j: (0, j)),
                  pl.BlockSpec((nb, 2 * FILT_K1, half), lambda m, sg, j: (m, 0, 0))],
        out_specs=[pl.BlockSpec((1, 2 * FILT_K1, nb, tn), lambda m, sg, j: (sg, 0, m, j)),
                   pl.BlockSpec((nseg, 1, d), lambda m, sg, j: (0, 0, 0))],
        out_shape=[jax.ShapeDtypeStruct((nseg, 2 * FILT_K1, n2, d), BF16),
                   jax.ShapeDtypeStruct((nseg, 1, d), F32)],
        scratch_shapes=[pltpu.VMEM((nb, half, 3 * fw), BF16),
                        pltpu.VMEM((nb, 2 * FILT_K1, tn), F32)],
        compiler_params=_params("arbitrary", "arbitrary", "arbitrary"),
        name="hyena_filter_dft_stage_a",
    )(z, w0, vec(f_b0), f_w1, vec(f_b1), f_w2, vec(f_b2), vec(f_freq), w3, dl, taf)


def _dft_tables(l):
    n1 = FFT_N1
    n = 2 * l
    n2 = n // n1
    k1 = np.arange(n1)[:, None]
    f1 = np.exp(-2j * np.pi * k1 * np.arange(n1 // 2)[None, :] / n1)
    tw = np.exp(-2j * np.pi * np.arange(n2)[:, None] * np.arange(n1)[None, :] / n)
    f1r, f1i = jnp.asarray(f1.real, F32), jnp.asarray(f1.imag, F32)
    twr, twi = jnp.asarray(tw.real, F32)[:, :, None], jnp.asarray(tw.imag, F32)[:, :, None]
    fr = twr * f1r - twi * f1i
    fi = twr * f1i + twi * f1r
    ta = jnp.concatenate([jnp.concatenate([fr, -fi], axis=2),
                          jnp.concatenate([fi, fr], axis=2)], axis=1)
    fb = np.exp(-2j * np.pi * np.arange(n2)[:, None] * np.arange(n2)[None, :] / n2)
    tb = np.block([[fb.real, -fb.imag], [fb.imag, fb.real]])
    taf = jnp.concatenate([ta[:, 0:FILT_K1, 0:n1 // 2], ta[:, n1:n1 + FILT_K1, 0:n1 // 2]], axis=1)
    fm = np.exp(-2j * np.pi * (np.arange(n2)[:, None] + 1) * np.arange(n2)[None, :] / n2)
    tb_mirror = np.block([[fm.real, fm.imag], [fm.imag, -fm.real]])
    return (ta.astype(BF16), jnp.swapaxes(ta, 1, 2).astype(BF16), jnp.asarray(tb, BF16), jnp.asarray(tb.T, BF16),
            taf.astype(BF16), jnp.asarray(np.stack([tb, tb_mirror]), BF16))


FFT_NB = BF16_SUBLANES


def _fft_a_kernel(x_ref, t_ref, o_ref, r_s):
    planes = x_ref.shape[1]
    xt = [jnp.swapaxes(x_ref[0, c], 0, 1) for c in range(planes)]
    for jj in range(FFT_NB):
        x = jnp.concatenate([xt[c][jj] for c in range(planes)], axis=0).astype(BF16)
        r_s[jj] = jnp.dot(t_ref[jj], x, preferred_element_type=F32)
    o_ref[0] = jnp.swapaxes(r_s[...], 0, 1).astype(o_ref.dtype)


def _fft_a(x5, ta, *, n2, d, seg):
    s, p, half, _, _ = x5.shape
    kdim = p * half
    tn = min(256, d)
    sb, db = (seg * d) // tn, d // tn
    return pl.pallas_call(
        _fft_a_kernel,
        grid=(s, n2 // FFT_NB, db),
        in_specs=[pl.BlockSpec((1, p, half, FFT_NB, tn), lambda si, m, j: (si, 0, 0, m, sb + j)),
                  pl.BlockSpec((FFT_NB, 2 * FFT_N1, kdim), lambda si, m, j: (m, 0, 0))],
        out_specs=pl.BlockSpec((1, 2 * FFT_N1, FFT_NB, tn), lambda si, m, j: (si, 0, m, j)),
        out_shape=jax.ShapeDtypeStruct((s, 2 * FFT_N1, n2, d), BF16),
        scratch_shapes=[pltpu.VMEM((FFT_NB, 2 * FFT_N1, tn), F32)],
        compiler_params=_params("parallel", "parallel", "parallel"),
        name="dft_stage_a",
    )(x5, ta)


def _fft_b_kernel(a_ref, af_ref, ab_ref, sc_ref, tb_ref, tbf_ref, tbt_ref, o_ref, *, n2):
    z = jnp.dot(tb_ref[...], a_ref[...].reshape(2 * n2, -1), preferred_element_type=F32)
    tbf = tbf_ref[0]
    hf = jnp.dot(tbf, af_ref[0].reshape(2 * n2, -1), preferred_element_type=F32)
    hb = jnp.dot(tbf, ab_ref[0].reshape(2 * n2, -1), preferred_element_type=F32)
    sc = sc_ref[0]
    gr = (hf[0:n2] + hb[0:n2]) * sc
    gi = (hf[n2:2 * n2] - hb[n2:2 * n2]) * sc
    zr, zi = z[0:n2], z[n2:2 * n2]
    prod = jnp.concatenate([zr * gr - zi * gi, zr * gi + zi * gr], axis=0).astype(BF16)
    y = jnp.dot(tbt_ref[...], prod, preferred_element_type=F32)
    o_ref[...] = y.reshape(o_ref.shape).astype(o_ref.dtype)


def _fft_b(a, af, scale, tb, tbf, tbt, order, *, n2, d):
    tn = min(2048, d)
    blk = (2, 1, n2, tn)
    fblk = (1, 2, 1, n2, tn)
    mirrored = lambda k: k > FFT_N1 // 2
    fk = lambda k: jnp.where(mirrored(k), FFT_N1 - k, k)
    return pl.pallas_call(
        functools.partial(_fft_b_kernel, n2=n2),
        grid=(FFT_N1, d // tn),
        in_specs=[pl.BlockSpec(blk, lambda k, j: (0, k, 0, j)),
                  pl.BlockSpec(fblk, lambda k, j: (2 * order, 0, fk(k), 0, j)),
                  pl.BlockSpec(fblk, lambda k, j: (2 * order + 1, 0, fk(k), 0, j)),
                  pl.BlockSpec((1, 1, tn), lambda k, j: (order, 0, j)),
                  pl.BlockSpec((2 * n2, 2 * n2), lambda k, j: (0, 0)),
                  pl.BlockSpec((1, 2 * n2, 2 * n2), lambda k, j: (jnp.where(mirrored(k), 1, 0), 0, 0)),
                  pl.BlockSpec((2 * n2, 2 * n2), lambda k, j: (0, 0))],
        out_specs=pl.BlockSpec(blk, lambda k, j: (0, k, 0, j)),
        out_shape=jax.ShapeDtypeStruct((2, FFT_N1, n2, d), BF16),
        compiler_params=_params("parallel", "parallel"),
        name="dft_stage_b_spectrum_product",
    )(a, af, af, scale, tb, tbf, tbt)


def _fft_ainv_kernel(y_ref, t_ref, u_ref, gate_ref, d_ref, o_ref, r_s):
    half = o_ref.shape[1]
    yt = jnp.swapaxes(y_ref[...].astype(F32), 0, 1).astype(BF16)
    for jj in range(FFT_NB):
        x = jnp.dot(t_ref[jj], yt[jj], preferred_element_type=F32)
        for c in range(2):
            r_s[c, jj] = x[c * half:(c + 1) * half]
    for c in range(2):
        o_ref[c] = gate_ref[c] * (jnp.swapaxes(r_s[c], 0, 1) + d_ref[...] * u_ref[c])


def _fft_ainv(y3, tat, u4, gate4, dskip, *, n2, d, useg, gseg):
    b, half, _, _ = u4.shape
    tn = min(128, d)
    db = d // tn
    usb, gsb = (useg * d) // tn, (gseg * d) // tn
    return pl.pallas_call(
        _fft_ainv_kernel,
        grid=(n2 // FFT_NB, db),
        in_specs=[pl.BlockSpec((2 * FFT_N1, FFT_NB, tn), lambda m, j: (0, m, j)),
                  pl.BlockSpec((FFT_NB, FFT_N1, 2 * FFT_N1), lambda m, j: (m, 0, 0)),
                  pl.BlockSpec((b, half, FFT_NB, tn), lambda m, j: (0, 0, m, usb + j)),
                  pl.BlockSpec((b, half, FFT_NB, tn), lambda m, j: (0, 0, m, gsb + j)),
                  pl.BlockSpec((1, tn), lambda m, j: (0, j))],
        out_specs=pl.BlockSpec((b, half, FFT_NB, tn), lambda m, j: (0, 0, m, j)),
        out_shape=jax.ShapeDtypeStruct((b, half, n2, d), F32),
        scratch_shapes=[pltpu.VMEM((b, FFT_NB, half, tn), F32)],
        compiler_params=_params("parallel", "parallel"),
        name="dft_stage_a_inverse_gate",
    )(y3, tat, u4, gate4, dskip)


def _hyena_mixer(h, hy, l, d):
    (w_in, b_in, sconv_w, sconv_b, f_w0, f_b0, f_w1, f_b1, f_w2, f_b2, f_freq, f_w3, d_skip, w_out) = hy
    b = h.shape[0]
    assert b == 2, "the two batch rows ride the real / imaginary planes of one complex DFT"
    n1 = FFT_N1
    half = n1 // 2
    n2 = (2 * l) // n1
    assert half * n2 == l
    z3 = _mm_conv(_hy_in_kernel, h, [w_in.astype(BF16)],
                  [b_in.reshape(1, -1), sconv_w, sconv_b.reshape(1, -1)], 3 * d, F32, "hyena_in_proj_conv", nz=2)
    ta, tat, tb, tbt, taf, tbf = _dft_tables(l)
    nseg = 2 * HY_ORDER
    af, fsum = _hyena_filter_spectra_a(l, d, taf, f_w0, f_b0, f_w1, f_b1, f_w2, f_b2, f_freq, f_w3)
    af = af.reshape(nseg, 2, FILT_K1, n2, d)
    scale = 1.0 / ((fsum[0::2] + fsum[1::2]) * (2 * l))
    z4 = z3.reshape(b, half, n2, 3 * d)
    y, yseg = z4, 0
    for o in range(HY_ORDER):
        a = _fft_a(y[None], ta, n2=n2, d=d, seg=yseg)
        yb = _fft_b(a.reshape(2, n1, n2, d), af, scale, tb, tbf, tbt, o, n2=n2, d=d)
        y = _fft_ainv(yb.reshape(2 * n1, n2, d), tat, y, z4, d_skip[o].reshape(1, d),
                      n2=n2, d=d, useg=yseg, gseg=o + 1)
        yseg = 0
    return y.reshape(b, l, d), w_out.astype(BF16)


def _rope_tables(l):
    t = jnp.arange(l)
    row = (t // GRID_W).astype(F32)[:, None]
    col = (t % GRID_W).astype(F32)[:, None]
    width = 2 * HEAD_DIM
    cos, sin, avg, perm = [], [], [], []
    for dim in (A_HALF, HEAD_DIM):
        nf = dim // 4
        inv = jnp.asarray(ROPE_BASE ** (-np.arange(nf) / nf), F32)[None, :]
        ar, ac = row * inv, col * inv
        c = jnp.concatenate([jnp.cos(ar), jnp.cos(ar), jnp.cos(ac), jnp.cos(ac)], axis=1)
        s = jnp.concatenate([-jnp.sin(ar), jnp.sin(ar), -jnp.sin(ac), jnp.sin(ac)], axis=1)
        reps = width // dim
        cos.append(jnp.tile(c, (1, reps)))
        sin.append(jnp.tile(s, (1, reps)))
        lane = np.arange(width)
        avg.append((lane[:, None] // dim == lane[None, :] // dim) / dim)
        partner = np.where(lane % (2 * nf) < nf, lane + nf, lane - nf)
        perm.append((lane[:, None] == partner[None, :]).astype(np.float64))
    return (jnp.stack(cos), jnp.stack(sin), jnp.asarray(np.stack(avg), BF16), jnp.asarray(np.stack(perm), BF16))


def _attn_mixer(h, hc, lam_init, w_in, w_out, a_q_g, a_k_g, lq1, lk1, lq2, lk2, a_sub_g, b_q_g, b_k_g, b_sink):
    b, l, d = h.shape
    c = hc.shape[1]
    a_heads = d // (2 * HEAD_DIM)
    b_heads = d // (2 * HEAD_DIM)
    q_cols = (a_heads + b_heads) * HEAD_DIM
    in_cols = w_in.shape[1]
    lam = (jnp.exp(jnp.sum(lq1 * lk1)) - jnp.exp(jnp.sum(lq2 * lk2)) + lam_init).reshape(1, 1)
    sink = b_sink.reshape(B_KV_HEADS, b_heads // B_KV_HEADS)

    ones = lambda n: jnp.ones((n,), F32)
    tile = lambda v, n: jnp.tile(v, n)
    gain = jnp.concatenate([tile(a_q_g, 2 * a_heads), tile(b_q_g, b_heads), tile(a_k_g, 2 * a_heads),
                            ones(a_heads * HEAD_DIM), tile(b_k_g, B_KV_HEADS),
                            ones(B_KV_HEADS * HEAD_DIM)]).reshape(1, in_cols)
    scale = jnp.concatenate([jnp.full((a_heads * HEAD_DIM,), A_HALF ** -0.5 * math.log2(math.e), F32),
                             jnp.full((b_heads * HEAD_DIM,), HEAD_DIM ** -0.5, F32),
                             ones(in_cols - q_cols)]).reshape(1, in_cols)
    e = [0, a_heads // 2, (a_heads + b_heads) // 2, (2 * a_heads + b_heads) // 2,
         (3 * a_heads + b_heads) // 2, (3 * a_heads + b_heads) // 2 + 1, (3 * a_heads + b_heads) // 2 + 2]
    plain = ((e[3], e[4]), (e[5], e[6]))
    cos_tab, sin_tab, avg, perm = _rope_tables(l)

    def type_of(j0):
        def f(j):
            jj = j + j0
            is_a = (jj < e[1]) | ((jj >= e[2]) & (jj < e[3]))
            return jnp.where(is_a, 0, 1)
        return f

    w_bf = w_in.astype(BF16)[None]
    p = _inproj(h.reshape(b * l, d), w_bf, gain, scale, cos_tab, sin_tab, avg, perm, plain=plain,
                type_of_block=type_of(0), col0=0, ncols=in_cols, l=l, rope=True).reshape(b, l, in_cols)
    kv0 = e[2]
    plain_c = tuple((lo - kv0, hi - kv0) for lo, hi in plain)
    pc = _inproj(hc.reshape(b * c, d), w_bf, gain, scale, cos_tab, sin_tab, avg, perm, plain=plain_c,
                 type_of_block=type_of(kv0), col0=kv0, ncols=in_cols - q_cols, l=c, rope=False
                 ).reshape(b, c, in_cols - q_cols)
    score_bound = (A_HALF ** 0.5 * math.log2(math.e)) * jnp.max(jnp.abs(a_q_g)) * jnp.max(jnp.abs(a_k_g))
    oa = _diffattn(lam, p, pc, a_sub_g, score_bound, heads=a_heads, l=l, c=c, out_scale=1.0 - lam_init)
    ob = _winattn(sink, p, pc, a_heads=a_heads, b_heads=b_heads, l=l, c=c)
    w_out_bf = w_out.astype(BF16)
    na = a_heads * HEAD_DIM
    return [oa, ob], [w_out_bf[:na], w_out_bf[na:]]


def kernel(x, c, ctx, c_ctx, ada_w, ada_b, norm1_g, norm2_g, attn_w_in, attn_w_out, a_q_g, a_k_g, a_lam_q1, a_lam_k1, a_lam_q2, a_lam_k2, a_sub_g, b_q_g, b_k_g, b_sink, hy_w_in, hy_b_in, hy_sconv_w, hy_sconv_b, hy_f_w0, hy_f_b0, hy_f_w1, hy_f_b1, hy_f_w2, hy_f_b2, hy_f_freq, hy_f_w3, hy_d, hy_w_out, ffn_w_gate, ffn_w_val, ffn_conv_w, ffn_conv_b, ffn_w_down):
    b, l, d = x.shape
    depth = ada_w.shape[0]
    cc = jnp.concatenate([c, c_ctx[None, :], jnp.zeros((8 - b - 1, d), F32)], axis=0)
    m = _ada(cc, ada_w, ada_b)
    xs = x
    for layer in range(depth):
        i = layer // 2
        lat = [m[layer, :b, k * d:(k + 1) * d].reshape(b, 1, d) for k in range(6)]
        sh1, sc1, g1, sh2, sc2, g2 = lat
        h = _normmod(xs, norm1_g[layer], sc1, sh1)
        if layer % 2 == 0:
            mc = [jnp.broadcast_to(m[layer, b, k * d:(k + 1) * d].reshape(1, 1, d), (b, 1, d)) for k in range(2)]
            hc = _normmod(ctx, norm1_g[layer], mc[1], mc[0])
            parts, ws = _attn_mixer(h, hc, 0.8 - 0.6 * math.exp(-0.3 * layer), attn_w_in[i], attn_w_out[i],
                                    a_q_g[i], a_k_g[i], a_lam_q1[i], a_lam_k1[i], a_lam_q2[i], a_lam_k2[i],
                                    a_sub_g[i], b_q_g[i], b_k_g[i], b_sink[i])
        else:
            hy = (hy_w_in[i], hy_b_in[i], hy_sconv_w[i], hy_sconv_b[i], hy_f_w0[i], hy_f_b0[i], hy_f_w1[i],
                  hy_f_b1[i], hy_f_w2[i], hy_f_b2[i], hy_f_freq[i], hy_f_w3[i], hy_d[i], hy_w_out[i])
            y, w_o = _hyena_mixer(h, hy, l, d)
            parts, ws = [y], [w_o]
        xs = _mm_resid(parts, ws, xs, g1)
        h2 = _normmod(xs, norm2_g[layer], sc2, sh2)
        d_ff = ffn_w_gate.shape[2]
        hid = _mm_conv(_ffn_up_kernel, h2, [ffn_w_gate[layer].astype(BF16), ffn_w_val[layer].astype(BF16)],
                       [ffn_conv_w[layer], ffn_conv_b[layer].reshape(1, -1)], d_ff, BF16, "ffn_up_conv_glu", nz=2)
        xs = _mm_resid([hid], [ffn_w_down[layer].astype(BF16)], xs, g2)
    return xs
```

```python
import functools
import math

import numpy as np
import jax
import jax.numpy as jnp
from jax import lax
from jax.experimental import pallas as pl
from jax.experimental.pallas import tpu as pltpu

F32 = jnp.float32
BF16 = jnp.bfloat16
HIGHEST = lax.Precision.HIGHEST

HEAD_DIM = 128
A_HALF = HEAD_DIM // 2
B_KV_HEADS = 2
GRID_W = 64
WINDOW = 128
ROPE_BASE = 10000.0
EPS = 1e-6
NEG = -1e30
HY_ORDER = 2
HY_DECAY_TARGET = 1e-2
HY_MAX_DECAY_PCT = 0.3
HY_MIN_DECAY_PCT = 1.5

BF16_SUBLANES = 16
FFT_N1 = 256
VMEM_LIMIT_BYTES = 56 * 1024 * 1024


def _params(*sem):
    return pltpu.CompilerParams(dimension_semantics=sem, vmem_limit_bytes=VMEM_LIMIT_BYTES)


def _ada_kernel(c_ref, w_ref, b_ref, o_ref):
    c = c_ref[...]
    s = c * (1.0 / (1.0 + jnp.exp(-c)))
    o_ref[0] = jnp.dot(s, w_ref[0], preferred_element_type=F32, precision=HIGHEST) + b_ref[0]


def _ada(cc, ada_w, ada_b):
    depth, d, n = ada_w.shape
    tn = 1024
    return pl.pallas_call(
        _ada_kernel,
        grid=(depth, n // tn),
        in_specs=[pl.BlockSpec((8, d), lambda l, j: (0, 0)),
                  pl.BlockSpec((1, d, tn), lambda l, j: (l, 0, j)),
                  pl.BlockSpec((1, 1, tn), lambda l, j: (l, 0, j))],
        out_specs=pl.BlockSpec((1, 8, tn), lambda l, j: (l, 0, j)),
        out_shape=jax.ShapeDtypeStruct((depth, 8, n), F32),
        compiler_params=_params("parallel", "parallel"),
        name="ada_modulation",
    )(cc, ada_w, ada_b.reshape(depth, 1, n))


NORM_ROWS = 32


def _normmod_kernel(x_ref, g_ref, sc_ref, sh_ref, o_ref):
    scale = g_ref[...] * (1.0 + sc_ref[0])
    shift = sh_ref[0]
    rows = min(NORM_ROWS, x_ref.shape[1])

    def slab(r, carry):
        rs = pl.ds(pl.multiple_of(r * rows, rows), rows)
        x = x_ref[0, rs, :]
        ms = jnp.mean(x * x, axis=-1, keepdims=True)
        o_ref[0, rs, :] = (x * lax.rsqrt(ms + EPS) * scale + shift).astype(o_ref.dtype)
        return carry

    lax.fori_loop(0, x_ref.shape[1] // rows, slab, 0, unroll=4)


def _normmod(x, g, sc, sh):
    b, l, d = x.shape
    tm = min(512, l)
    return pl.pallas_call(
        _normmod_kernel,
        grid=(b, l // tm),
        in_specs=[pl.BlockSpec((1, tm, d), lambda bi, i: (bi, i, 0)),
                  pl.BlockSpec((1, d), lambda bi, i: (0, 0)),
                  pl.BlockSpec((1, 1, d), lambda bi, i: (bi, 0, 0)),
                  pl.BlockSpec((1, 1, d), lambda bi, i: (bi, 0, 0))],
        out_specs=pl.BlockSpec((1, tm, d), lambda bi, i: (bi, i, 0)),
        out_shape=jax.ShapeDtypeStruct((b, l, d), BF16),
        compiler_params=_params("parallel", "parallel"),
        name="rmsnorm_modulate",
    )(x, g.reshape(1, d), sc, sh)


def _dot_split(x, m):
    hi = x.astype(BF16)
    lo = (x - hi.astype(F32)).astype(BF16)
    return jnp.dot(hi, m, preferred_element_type=F32) + jnp.dot(lo, m, preferred_element_type=F32)


def _inproj_kernel(h_ref, w_ref, gain_ref, scale_ref, cos_ref, sin_ref, avg_ref, perm_ref, o_ref, *, plain, rope):
    j = pl.program_id(1)
    z = jnp.dot(h_ref[...], w_ref[0], preferred_element_type=F32)
    is_plain = functools.reduce(jnp.logical_or, [(j >= lo) & (j < hi) for lo, hi in plain])

    @pl.when(is_plain)
    def _():
        o_ref[...] = z.astype(o_ref.dtype)

    @pl.when(jnp.logical_not(is_plain))
    def _():
        rows = min(256, z.shape[0])
        for r0 in range(0, z.shape[0], rows):
            rs = slice(r0, r0 + rows)
            zc = z[rs]
            ms = _dot_split(zc * zc, avg_ref[0])
            y = zc * lax.rsqrt(ms + EPS) * gain_ref[...]
            if rope:
                y = y * cos_ref[0, rs, :] + _dot_split(y, perm_ref[0]) * sin_ref[0, rs, :]
            o_ref[rs, :] = (y * scale_ref[...]).astype(o_ref.dtype)


def _inproj(h2d, w, gain, scale, cos_tab, sin_tab, avg, perm, *, plain, type_of_block, col0, ncols, l, rope):
    m, d = h2d.shape
    tn = 2 * HEAD_DIM
    tm = min(2048, l)
    pos_blocks = l // tm
    kern = functools.partial(_inproj_kernel, plain=plain, rope=rope)
    by_type = lambda i, j: (type_of_block(j), 0, 0)
    return pl.pallas_call(
        kern,
        grid=(m // tm, ncols // tn),
        in_specs=[pl.BlockSpec((tm, d), lambda i, j: (i, 0)),
                  pl.BlockSpec((1, d, tn), lambda i, j: (0, 0, j + col0)),
                  pl.BlockSpec((1, tn), lambda i, j: (0, j + col0)),
                  pl.BlockSpec((1, tn), lambda i, j: (0, j + col0)),
                  pl.BlockSpec((1, tm, tn), lambda i, j: (type_of_block(j), i % pos_blocks, 0)),
                  pl.BlockSpec((1, tm, tn), lambda i, j: (type_of_block(j), i % pos_blocks, 0)),
                  pl.BlockSpec((1, tn, tn), by_type),
                  pl.BlockSpec((1, tn, tn), by_type)],
        out_specs=pl.BlockSpec((tm, tn), lambda i, j: (i, j)),
        out_shape=jax.ShapeDtypeStruct((m, ncols), BF16),
        compiler_params=_params("parallel", "arbitrary"),
        name="attn_in_projection",
    )(h2d, w, gain, scale, cos_tab, sin_tab, avg, perm)


def _diffattn_kernel(lam_ref, q_ref, kc_ref, vc_ref, k_ref, v_ref, g_ref, o_ref,
                     q2_s, m_s, l_s, acc_s, *, tq, nkv, out_scale):
    kv = pl.program_id(3)

    def process(k, v):
        s = lax.dot_general(q2_s[...], k, (((1,), (1,)), ((), ())), preferred_element_type=F32)
        m_prev = m_s[...]
        m_new = jnp.maximum(m_prev, jnp.max(s, axis=-1, keepdims=True))
        alpha = jnp.exp2(m_prev - m_new)
        p = jnp.exp2(s - m_new)
        l_s[...] = alpha * l_s[...] + jnp.sum(p, axis=-1, keepdims=True)
        acc_s[...] = alpha * acc_s[...] + jnp.dot(p.astype(BF16), v, preferred_element_type=F32)
        m_s[...] = m_new

    @pl.when(kv == 0)
    def _():
        q = q_ref[0]
        lane = lax.broadcasted_iota(jnp.int32, q.shape, 1)
        zero = jnp.zeros_like(q)
        q2_s[0:tq, :] = jnp.where(lane < A_HALF, q, zero)
        q2_s[tq:2 * tq, :] = jnp.where(lane >= A_HALF, q, zero)
        m_s[...] = jnp.full(m_s.shape, -jnp.inf, F32)
        l_s[...] = jnp.zeros(l_s.shape, F32)
        acc_s[...] = jnp.zeros(acc_s.shape, F32)
        process(kc_ref[0], vc_ref[0])

    process(k_ref[0], v_ref[0])

    @pl.when(kv == nkv - 1)
    def _():
        o = acc_s[...] / l_s[...]
        d = o[0:tq] - lam_ref[0, 0] * o[tq:2 * tq]
        ms = jnp.mean(d * d, axis=-1, keepdims=True)
        o_ref[0] = (d * lax.rsqrt(ms + EPS) * g_ref[...] * out_scale).astype(o_ref.dtype)


def _diffattn_fast_kernel(lam_ref, q_ref, kc_ref, vct_ref, k_ref, vt_ref, g_ref, o_ref,
                          q2_s, l_s, acc_s, p_s, *, tq, tk, kc, nkv, out_scale):
    kv = pl.program_id(3)
    dn = (((1,), (1,)), ((), ()))

    def weights(k):
        n = k.shape[0]
        p = jnp.exp2(lax.dot_general(k, q2_s[...], dn, preferred_element_type=F32))
        return p.astype(BF16), jnp.sum(p.reshape(n // 8, 8, 2 * tq), axis=0)

    @pl.when(kv == 0)
    def _():
        q = q_ref[0]
        lane = lax.broadcasted_iota(jnp.int32, q.shape, 1)
        zero = jnp.zeros_like(q)
        q2_s[0:tq, :] = jnp.where(lane < A_HALF, q, zero)
        q2_s[tq:2 * tq, :] = jnp.where(lane >= A_HALF, q, zero)
        p, ls = weights(kc_ref[0])
        l_s[...] = ls
        acc_s[...] = jnp.dot(vct_ref[0, 0], p, preferred_element_type=F32)

    ls = l_s[...]
    for ci in range(tk // kc):
        p, lc = weights(k_ref[0, ci * kc:(ci + 1) * kc, :])
        p_s[ci * kc:(ci + 1) * kc, :] = p
        ls = ls + lc
    l_s[...] = ls
    acc_s[...] += jnp.dot(vt_ref[0, 0], p_s[...], preferred_element_type=F32)

    @pl.when(kv == nkv - 1)
    def _():
        l = jnp.sum(l_s[...], axis=0, keepdims=True)
        o = acc_s[...] / l
        d = o[:, 0:tq] - lam_ref[0, 0] * o[:, tq:2 * tq]
        ms = jnp.mean(d * d, axis=0, keepdims=True)
        y = d * lax.rsqrt(ms + EPS) * g_ref[...] * out_scale
        o_ref[0] = y.T.astype(o_ref.dtype)


def _diffattn_fast(lam, p, pc, sub_g, *, heads, l, c, out_scale):
    b = p.shape[0]
    tq = min(1024, l)
    tk = min(4096, l)
    kc = min(512, tk)
    nkv = l // tk
    k_blk = 2 * heads
    hd = heads * HEAD_DIM
    vt = jnp.transpose(p[:, :, 3 * hd:4 * hd].reshape(b, l, heads, HEAD_DIM), (0, 2, 3, 1))
    vct = jnp.transpose(pc[:, :, hd:2 * hd].reshape(b, c, heads, HEAD_DIM), (0, 2, 3, 1))
    kern = functools.partial(_diffattn_fast_kernel, tq=tq, tk=tk, kc=kc, nkv=nkv, out_scale=out_scale)
    return pl.pallas_call(
        kern,
        grid=(b, heads, l // tq, nkv),
        in_specs=[pl.BlockSpec(memory_space=pltpu.SMEM),
                  pl.BlockSpec((1, tq, HEAD_DIM), lambda bi, h, i, kv: (bi, i, h)),
                  pl.BlockSpec((1, c, HEAD_DIM), lambda bi, h, i, kv: (bi, 0, h)),
                  pl.BlockSpec((1, 1, HEAD_DIM, c), lambda bi, h, i, kv: (bi, h, 0, 0)),
                  pl.BlockSpec((1, tk, HEAD_DIM), lambda bi, h, i, kv: (bi, kv, k_blk + h)),
                  pl.BlockSpec((1, 1, HEAD_DIM, tk), lambda bi, h, i, kv: (bi, h, 0, kv)),
                  pl.BlockSpec((HEAD_DIM, 1), lambda bi, h, i, kv: (0, 0))],
        out_specs=pl.BlockSpec((1, tq, HEAD_DIM), lambda bi, h, i, kv: (bi, i, h)),
        out_shape=jax.ShapeDtypeStruct((b, l, hd), BF16),
        scratch_shapes=[pltpu.VMEM((2 * tq, HEAD_DIM), BF16),
                        pltpu.VMEM((8, 2 * tq), F32),
                        pltpu.VMEM((HEAD_DIM, 2 * tq), F32),
                        pltpu.VMEM((tk, 2 * tq), BF16)],
        compiler_params=_params("parallel", "parallel", "parallel", "arbitrary"),
        name="diff_attention_bounded",
    )(lam, p, pc, vct, p, vt, sub_g.reshape(HEAD_DIM, 1))


SCORE_BOUND_LOG2 = 60.0


def _diffattn(lam, p, pc, sub_g, score_bound, *, heads, l, c, out_scale):
    kw = dict(heads=heads, l=l, c=c, out_scale=out_scale)
    return lax.cond(score_bound < SCORE_BOUND_LOG2,
                    lambda *a: _diffattn_fast(*a, **kw), lambda *a: _diffattn_safe(*a, **kw),
                    lam, p, pc, sub_g)


def _diffattn_safe(lam, p, pc, sub_g, *, heads, l, c, out_scale):
    b = p.shape[0]
    tq = min(256, l)
    tk = min(512, l)
    nkv = l // tk
    q_blk = 0
    k_blk = 2 * heads
    v_blk = 3 * heads
    kern = functools.partial(_diffattn_kernel, tq=tq, nkv=nkv, out_scale=out_scale)
    return pl.pallas_call(
        kern,
        grid=(b, heads, l // tq, nkv),
        in_specs=[pl.BlockSpec(memory_space=pltpu.SMEM),
                  pl.BlockSpec((1, tq, HEAD_DIM), lambda bi, h, i, kv: (bi, i, q_blk + h)),
                  pl.BlockSpec((1, c, HEAD_DIM), lambda bi, h, i, kv: (bi, 0, h)),
                  pl.BlockSpec((1, c, HEAD_DIM), lambda bi, h, i, kv: (bi, 0, heads + h)),
                  pl.BlockSpec((1, tk, HEAD_DIM), lambda bi, h, i, kv: (bi, kv, k_blk + h)),
                  pl.BlockSpec((1, tk, HEAD_DIM), lambda bi, h, i, kv: (bi, kv, v_blk + h)),
                  pl.BlockSpec((1, HEAD_DIM), lambda bi, h, i, kv: (0, 0))],
        out_specs=pl.BlockSpec((1, tq, HEAD_DIM), lambda bi, h, i, kv: (bi, i, h)),
        out_shape=jax.ShapeDtypeStruct((b, l, heads * HEAD_DIM), BF16),
        scratch_shapes=[pltpu.VMEM((2 * tq, HEAD_DIM), BF16),
                        pltpu.VMEM((2 * tq, 1), F32),
                        pltpu.VMEM((2 * tq, 1), F32),
                        pltpu.VMEM((2 * tq, HEAD_DIM), F32)],
        compiler_params=_params("parallel", "parallel", "parallel", "arbitrary"),
        name="diff_attention",
    )(lam, p, pc, pc, p, p, sub_g.reshape(1, HEAD_DIM))


def _winattn_kernel(sink_ref, q_ref, kp_ref, km_ref, kn_ref, vp_ref, vm_ref, vn_ref, kc_ref, vc_ref,
                    o_ref, *, tq, l, group):
    g = pl.program_id(1)
    i = pl.program_id(2)
    kband = jnp.concatenate([kp_ref[0], km_ref[0], kn_ref[0]], axis=0)
    vband = jnp.concatenate([vp_ref[0], vm_ref[0], vn_ref[0]], axis=0)
    nk = tq + 2 * WINDOW
    qpos = i * tq + lax.broadcasted_iota(jnp.int32, (tq, nk), 0)
    kpos = i * tq - WINDOW + lax.broadcasted_iota(jnp.int32, (tq, nk), 1)
    valid = (jnp.abs(kpos - qpos) <= WINDOW) & (kpos >= 0) & (kpos < l)
    kc = kc_ref[0]
    vc = vc_ref[0]
    dn = (((1,), (1,)), ((), ()))
    for r in range(group):
        sl = slice(r * HEAD_DIM, (r + 1) * HEAD_DIM)
        q = q_ref[0, :, sl]
        s_lat = jnp.where(valid, lax.dot_general(q, kband, dn, preferred_element_type=F32), NEG)
        s_ctx = lax.dot_general(q, kc, dn, preferred_element_type=F32)
        sk = sink_ref[g, r]
        m = jnp.maximum(jnp.maximum(jnp.max(s_lat, axis=-1, keepdims=True),
                                    jnp.max(s_ctx, axis=-1, keepdims=True)), sk)
        p_lat = jnp.exp(s_lat - m)
        p_ctx = jnp.exp(s_ctx - m)
        denom = (jnp.sum(p_lat, axis=-1, keepdims=True) + jnp.sum(p_ctx, axis=-1, keepdims=True)
                 + jnp.exp(sk - m))
        o = (jnp.dot(p_lat.astype(BF16), vband, preferred_element_type=F32)
             + jnp.dot(p_ctx.astype(BF16), vc, preferred_element_type=F32))
        o_ref[0, :, sl] = (o / denom).astype(o_ref.dtype)


def _winattn(sink, p, pc, *, a_heads, b_heads, l, c):
    b = p.shape[0]
    group = b_heads // B_KV_HEADS
    tq = min(256, l)
    wpb = tq // WINDOW
    nwb = l // WINDOW
    gw = group * HEAD_DIM
    q_blk = a_heads * HEAD_DIM // gw
    k_blk = (a_heads + b_heads) + 2 * a_heads
    v_blk = k_blk + B_KV_HEADS
    kc_blk = 2 * a_heads
    vc_blk = kc_blk + B_KV_HEADS
    kern = functools.partial(_winattn_kernel, tq=tq, l=l, group=group)

    def prev_map(col):
        return lambda bi, g, i: (bi, jnp.maximum(i * wpb - 1, 0), col + g)

    def main_map(col):
        return lambda bi, g, i: (bi, i, col + g)

    def next_map(col):
        return lambda bi, g, i: (bi, jnp.minimum((i + 1) * wpb, nwb - 1), col + g)

    return pl.pallas_call(
        kern,
        grid=(b, B_KV_HEADS, l // tq),
        in_specs=[pl.BlockSpec(memory_space=pltpu.SMEM),
                  pl.BlockSpec((1, tq, gw), lambda bi, g, i: (bi, i, q_blk + g)),
                  pl.BlockSpec((1, WINDOW, HEAD_DIM), prev_map(k_blk)),
                  pl.BlockSpec((1, tq, HEAD_DIM), main_map(k_blk)),
                  pl.BlockSpec((1, WINDOW, HEAD_DIM), next_map(k_blk)),
                  pl.BlockSpec((1, WINDOW, HEAD_DIM), prev_map(v_blk)),
                  pl.BlockSpec((1, tq, HEAD_DIM), main_map(v_blk)),
                  pl.BlockSpec((1, WINDOW, HEAD_DIM), next_map(v_blk)),
                  pl.BlockSpec((1, c, HEAD_DIM), lambda bi, g, i: (bi, 0, kc_blk + g)),
                  pl.BlockSpec((1, c, HEAD_DIM), lambda bi, g, i: (bi, 0, vc_blk + g))],
        out_specs=pl.BlockSpec((1, tq, gw), lambda bi, g, i: (bi, i, g)),
        out_shape=jax.ShapeDtypeStruct((b, l, b_heads * HEAD_DIM), BF16),
        compiler_params=_params("parallel", "parallel", "parallel"),
        name="window_attention",
    )(sink, p, p, p, p, p, p, p, pc, pc)


def _mm_resid_kernel(*refs, nparts):
    a_refs = refs[0:nparts]
    w_refs = refs[nparts:2 * nparts]
    res_ref, gate_ref, o_ref = refs[2 * nparts:]
    y = jnp.dot(a_refs[0][0].astype(BF16), w_refs[0][0], preferred_element_type=F32)
    for a_ref, w_ref in zip(a_refs[1:], w_refs[1:]):
        y = y + jnp.dot(a_ref[0].astype(BF16), w_ref[0], preferred_element_type=F32)
    o_ref[0] = res_ref[0] + gate_ref[0] * y


def _mm_resid(parts, ws, res, gate):
    b, l, n = res.shape
    tm = min(1024, l)
    tn = min(512, n)
    nparts = len(parts)
    ws = [w[None] for w in ws]
    in_specs = ([pl.BlockSpec((1, tm, a.shape[2]), lambda bi, i, j: (bi, i, 0)) for a in parts]
                + [pl.BlockSpec((1, w.shape[1], tn), lambda bi, i, j: (0, 0, j)) for w in ws]
                + [pl.BlockSpec((1, tm, tn), lambda bi, i, j: (bi, i, j)),
                   pl.BlockSpec((1, 1, tn), lambda bi, i, j: (bi, 0, j))])
    return pl.pallas_call(
        functools.partial(_mm_resid_kernel, nparts=nparts),
        grid=(b, l // tm, n // tn),
        in_specs=in_specs,
        out_specs=pl.BlockSpec((1, tm, tn), lambda bi, i, j: (bi, i, j)),
        out_shape=jax.ShapeDtypeStruct((b, l, n), F32),
        compiler_params=_params("parallel", "parallel", "arbitrary"),
        name="out_projection_gated_residual",
    )(*parts, *ws, res, gate)


HALO = BF16_SUBLANES


def _fill_rows(hbuf, hp_ref, hm_ref, hn_ref, tm):
    hbuf[0:HALO, :] = hp_ref[0]
    hbuf[HALO:HALO + tm, :] = hm_ref[0]
    hbuf[HALO + tm:2 * HALO + tm, :] = hn_ref[0]


def _conv3(zbuf, cw_ref, cb_ref, tm, first, last):
    lo = slice(HALO - 1, HALO)
    hi = slice(HALO + tm, HALO + tm + 1)
    zbuf[lo, :] = jnp.where(first, 0.0, zbuf[lo, :])
    zbuf[hi, :] = jnp.where(last, 0.0, zbuf[hi, :])
    z = zbuf[...]
    rows = z.shape[0]
    g = (z * cw_ref[1:2, :] + pltpu.roll(z * cw_ref[0:1, :], 1, 0) + pltpu.roll(z * cw_ref[2:3, :], rows - 1, 0))
    return g[HALO:HALO + tm] + cb_ref[...]


def _gelu_tanh(x):
    return 0.5 * x * (1.0 + jnp.tanh(math.sqrt(2.0 / math.pi) * (x + 0.044715 * (x * x * x))))


def _ffn_up_kernel(hp_ref, hm_ref, hn_ref, wg_ref, wv_ref, cw_ref, cb_ref, o_ref, hbuf, zbuf, zbuf2, *, tm, nt):
    i = pl.program_id(1)
    j = pl.program_id(2)

    @pl.when(j == 0)
    def _():
        _fill_rows(hbuf, hp_ref, hm_ref, hn_ref, tm)

    hw = zbuf.shape[1]
    for hh, zb in enumerate((zbuf, zbuf2)):
        cs = slice(hh * hw, (hh + 1) * hw)
        zb[...] = jnp.dot(hbuf[...], wg_ref[0, :, cs], preferred_element_type=F32)
        g = _conv3(zb, cw_ref.at[:, cs], cb_ref.at[:, cs], tm, i == 0, i == nt - 1)
        v = jnp.dot(hbuf[HALO:HALO + tm, :], wv_ref[0, :, cs], preferred_element_type=F32)
        o_ref[0, :, cs] = (_gelu_tanh(g) * v).astype(o_ref.dtype)


def _hy_in_kernel(hp_ref, hm_ref, hn_ref, w_ref, b_ref, cw_ref, cb_ref, o_ref, hbuf, zbuf, zbuf2, *, tm, nt):
    i = pl.program_id(1)
    j = pl.program_id(2)

    @pl.when(j == 0)
    def _():
        _fill_rows(hbuf, hp_ref, hm_ref, hn_ref, tm)

    hw = zbuf.shape[1]
    for hh, zb in enumerate((zbuf, zbuf2)):
        cs = slice(hh * hw, (hh + 1) * hw)
        zb[...] = jnp.dot(hbuf[...], w_ref[0, :, cs], preferred_element_type=F32) + b_ref[:, cs]
        o_ref[0, :, cs] = _conv3(zb, cw_ref.at[:, cs], cb_ref.at[:, cs], tm, i == 0, i == nt - 1
                                 ).astype(o_ref.dtype)


def _mm_conv(kernel, h, ws, vecs, n, out_dtype, name, nz=1, row_tile=1024):
    b, l, k = h.shape
    tm = min(row_tile, l)
    tn = min(512, n)
    nt = l // tm
    hpb = tm // HALO
    nhb = l // HALO
    ws = [w[None] for w in ws]
    in_specs = ([pl.BlockSpec((1, HALO, k), lambda bi, i, j: (bi, jnp.maximum(i * hpb - 1, 0), 0)),
                 pl.BlockSpec((1, tm, k), lambda bi, i, j: (bi, i, 0)),
                 pl.BlockSpec((1, HALO, k), lambda bi, i, j: (bi, jnp.minimum((i + 1) * hpb, nhb - 1), 0))]
                + [pl.BlockSpec((1, k, tn), lambda bi, i, j: (0, 0, j)) for _ in ws]
                + [pl.BlockSpec((v.shape[0], tn), lambda bi, i, j: (0, j)) for v in vecs])
    return pl.pallas_call(
        functools.partial(kernel, tm=tm, nt=nt),
        grid=(b, nt, n // tn),
        in_specs=in_specs,
        out_specs=pl.BlockSpec((1, tm, tn), lambda bi, i, j: (bi, i, j)),
        out_shape=jax.ShapeDtypeStruct((b, l, n), out_dtype),
        scratch_shapes=[pltpu.VMEM((tm + 2 * HALO, k), BF16)] + [pltpu.VMEM((tm + 2 * HALO, tn // nz), F32)] * nz,
        compiler_params=_params("parallel", "parallel", "arbitrary"),
        name=name,
    )(h, h, h, *ws, *vecs)


FILT_NB = BF16_SUBLANES
FILT_K1 = FFT_N1 // 2 + 8


def _filter_dft_kernel(z_ref, w0_ref, b0_ref, w1_ref, b1_ref, w2_ref, b2_ref, fr_ref, w3_ref, dl_ref, t_ref,
                       o_ref, s_ref, a3_s, r_s):
    m, seg, j = pl.program_id(0), pl.program_id(1), pl.program_id(2)
    tn = o_ref.shape[3]
    fr = fr_ref[...]

    def layer(a, w_ref, b_ref):
        return jnp.sin(fr * (jnp.dot(a, w_ref[...], preferred_element_type=F32, precision=HIGHEST)
                             + b_ref[...]))

    @pl.when((seg == 0) & (j == 0))
    def _():
        for jj in range(FILT_NB):
            a = layer(layer(layer(z_ref[jj], w0_ref, b0_ref), w1_ref, b1_ref), w2_ref, b2_ref)
            a_hi = a.astype(BF16)
            a_lo = (a - a_hi.astype(F32)).astype(BF16)
            a3_s[jj] = jnp.concatenate([a_hi, a_hi, a_lo], axis=1)

    @pl.when((m == 0) & (seg == 0) & (j == 0))
    def _():
        s_ref[...] = jnp.zeros(s_ref.shape, F32)

    dl = dl_ref[...]
    backward = seg % 2 == 1
    abs_sum = jnp.zeros((1, tn), F32)
    for jj in range(FILT_NB):
        h = (jnp.dot(a3_s[jj], w3_ref[0], preferred_element_type=F32)
             * jnp.exp(-z_ref[jj][:, 0:1] * dl))
        if jj == 0:
            row = lax.broadcasted_iota(jnp.int32, h.shape, 0)
            h = jnp.where(backward & (m == 0) & (row == 0), 0.0, h)
        abs_sum = abs_sum + jnp.sum(jnp.abs(h), axis=0, keepdims=True)
        r_s[jj] = jnp.dot(t_ref[jj], h.astype(BF16), preferred_element_type=F32)
    o_ref[0] = jnp.swapaxes(r_s[...], 0, 1).astype(o_ref.dtype)
    col = pl.multiple_of(j * tn, tn)
    s_ref[seg, :, pl.ds(col, tn)] += abs_sum


def _hyena_filter_spectra_a(l, d, taf, f_w0, f_b0, f_w1, f_b1, f_w2, f_b2, f_freq, f_w3):
    emb, fw = f_w0.shape
    bands = (emb - 1) // 2
    half = FFT_N1 // 2
    n2 = l // half
    t = np.linspace(0.0, 1.0, l)[:, None]
    w = 2.0 * math.pi * np.arange(l)[:, None] / l
    f = np.linspace(1e-4, bands - 1, bands)[None, :]
    emb_pad = -(-emb // 8) * 8
    z = np.concatenate([t, np.cos(f * w), -np.sin(f * w), np.zeros((l, emb_pad - emb))], axis=-1)
    z = jnp.asarray(z.reshape(half, n2, emb_pad).transpose(1, 0, 2), F32)
    w0 = jnp.concatenate([f_w0, jnp.zeros((emb_pad - emb, fw), F32)], axis=0)
    max_decay = math.log(HY_DECAY_TARGET) / HY_MAX_DECAY_PCT
    min_decay = math.log(HY_DECAY_TARGET) / HY_MIN_DECAY_PCT
    dl = jnp.asarray(np.abs(np.linspace(min_decay, max_decay, d))[None, :], F32)
    nseg = 2 * HY_ORDER
    w3 = jnp.transpose(f_w3.reshape(fw, nseg, d), (1, 0, 2))
    w3_hi = w3.astype(BF16)
    w3_lo = (w3 - w3_hi.astype(F32)).astype(BF16)
    w3 = jnp.concatenate([w3_hi, w3_lo, w3_hi], axis=1)
    nb = min(FILT_NB, n2)
    assert nb == FILT_NB
    tn = min(256, d)
    vec = lambda a: a.reshape(1, fw)
    small = lambda shape: pl.BlockSpec(shape, lambda m, sg, j: (0,) * len(shape))
    return pl.pallas_call(
        _filter_dft_kernel,
        grid=(n2 // nb, nseg, d // tn),
        in_specs=[pl.BlockSpec((nb, half, emb_pad), lambda m, sg, j: (m, 0, 0)),
                  small((emb_pad, fw)), small((1, fw)), small((fw, fw)), small((1, fw)),
                  small((fw, fw)), small((1, fw)), small((1, fw)),
                  pl.BlockSpec((1, 3 * fw, tn), lambda m, sg, j: (sg, 0, j)),
                  pl.BlockSpec((1, tn), lambda m, sg, j: (0, j)),
                  pl.BlockSpec((nb, 2 * FILT_K1, half), lambda m, sg, j: (m, 0, 0))],
        out_specs=[pl.BlockSpec((1, 2 * FILT_K1, nb, tn), lambda m, sg, j: (sg, 0, m, j)),
                   pl.BlockSpec((nseg, 1, d), lambda m, sg, j: (0, 0, 0))],
        out_shape=[jax.ShapeDtypeStruct((nseg, 2 * FILT_K1, n2, d), BF16),
                   jax.ShapeDtypeStruct((nseg, 1, d), F32)],
        scratch_shapes=[pltpu.VMEM((nb, half, 3 * fw), BF16),
                        pltpu.VMEM((nb, 2 * FILT_K1, tn), F32)],
        compiler_params=_params("arbitrary", "arbitrary", "arbitrary"),
        name="hyena_filter_dft_stage_a",
    )(z, w0, vec(f_b0), f_w1, vec(f_b1), f_w2, vec(f_b2), vec(f_freq), w3, dl, taf)


def _dft_tables(l):
    n1 = FFT_N1
    n = 2 * l
    n2 = n // n1
    k1 = np.arange(n1)[:, None]
    f1 = np.exp(-2j * np.pi * k1 * np.arange(n1 // 2)[None, :] / n1)
    tw = np.exp(-2j * np.pi * np.arange(n2)[:, None] * np.arange(n1)[None, :] / n)
    f1r, f1i = jnp.asarray(f1.real, F32), jnp.asarray(f1.imag, F32)
    twr, twi = jnp.asarray(tw.real, F32)[:, :, None], jnp.asarray(tw.imag, F32)[:, :, None]
    fr = twr * f1r - twi * f1i
    fi = twr * f1i + twi * f1r
    ta = jnp.concatenate([jnp.concatenate([fr, -fi], axis=2),
                          jnp.concatenate([fi, fr], axis=2)], axis=1)
    fb = np.exp(-2j * np.pi * np.arange(n2)[:, None] * np.arange(n2)[None, :] / n2)
    tb = np.block([[fb.real, -fb.imag], [fb.imag, fb.real]])
    taf = jnp.concatenate([ta[:, 0:FILT_K1, 0:n1 // 2], ta[:, n1:n1 + FILT_K1, 0:n1 // 2]], axis=1)
    fm = np.exp(-2j * np.pi * (np.arange(n2)[:, None] + 1) * np.arange(n2)[None, :] / n2)
    tb_mirror = np.block([[fm.real, fm.imag], [fm.imag, -fm.real]])
    return (ta.astype(BF16), jnp.swapaxes(ta, 1, 2).astype(BF16), jnp.asarray(tb, BF16), jnp.asarray(tb.T, BF16),
            taf.astype(BF16), jnp.asarray(np.stack([tb, tb_mirror]), BF16))


FFT_NB = BF16_SUBLANES


def _fft_a_kernel(x_ref, t_ref, o_ref, r_s):
    planes = x_ref.shape[1]
    xt = [jnp.swapaxes(x_ref[0, c], 0, 1) for c in range(planes)]
    for jj in range(FFT_NB):
        x = jnp.concatenate([xt[c][jj] for c in range(planes)], axis=0).astype(BF16)
        r_s[jj] = jnp.dot(t_ref[jj], x, preferred_element_type=F32)
    o_ref[0] = jnp.swapaxes(r_s[...], 0, 1).astype(o_ref.dtype)


def _fft_a(x5, ta, *, n2, d, seg):
    s, p, half, _, _ = x5.shape
    kdim = p * half
    tn = min(256, d)
    sb, db = (seg * d) // tn, d // tn
    return pl.pallas_call(
        _fft_a_kernel,
        grid=(s, n2 // FFT_NB, db),
        in_specs=[pl.BlockSpec((1, p, half, FFT_NB, tn), lambda si, m, j: (si, 0, 0, m, sb + j)),
                  pl.BlockSpec((FFT_NB, 2 * FFT_N1, kdim), lambda si, m, j: (m, 0, 0))],
        out_specs=pl.BlockSpec((1, 2 * FFT_N1, FFT_NB, tn), lambda si, m, j: (si, 0, m, j)),
        out_shape=jax.ShapeDtypeStruct((s, 2 * FFT_N1, n2, d), BF16),
        scratch_shapes=[pltpu.VMEM((FFT_NB, 2 * FFT_N1, tn), F32)],
        compiler_params=_params("parallel", "parallel", "parallel"),
        name="dft_stage_a",
    )(x5, ta)


def _fft_b_kernel(a_ref, af_ref, ab_ref, sc_ref, tb_ref, tbf_ref, tbt_ref, o_ref, *, n2):
    z = jnp.dot(tb_ref[...], a_ref[...].reshape(2 * n2, -1), preferred_element_type=F32)
    tbf = tbf_ref[0]
    hf = jnp.dot(tbf, af_ref[0].reshape(2 * n2, -1), preferred_element_type=F32)
    hb = jnp.dot(tbf, ab_ref[0].reshape(2 * n2, -1), preferred_element_type=F32)
    sc = sc_ref[0]
    gr = (hf[0:n2] + hb[0:n2]) * sc
    gi = (hf[n2:2 * n2] - hb[n2:2 * n2]) * sc
    zr, zi = z[0:n2], z[n2:2 * n2]
    prod = jnp.concatenate([zr * gr - zi * gi, zr * gi + zi * gr], axis=0).astype(BF16)
    y = jnp.dot(tbt_ref[...], prod, preferred_element_type=F32)
    o_ref[...] = y.reshape(o_ref.shape).astype(o_ref.dtype)


def _fft_b(a, af, scale, tb, tbf, tbt, order, *, n2, d):
    tn = min(2048, d)
    blk = (2, 1, n2, tn)
    fblk = (1, 2, 1, n2, tn)
    mirrored = lambda k: k > FFT_N1 // 2
    fk = lambda k: jnp.where(mirrored(k), FFT_N1 - k, k)
    return pl.pallas_call(
        functools.partial(_fft_b_kernel, n2=n2),
        grid=(FFT_N1, d // tn),
        in_specs=[pl.BlockSpec(blk, lambda k, j: (0, k, 0, j)),
                  pl.BlockSpec(fblk, lambda k, j: (2 * order, 0, fk(k), 0, j)),
                  pl.BlockSpec(fblk, lambda k, j: (2 * order + 1, 0, fk(k), 0, j)),
                  pl.BlockSpec((1, 1, tn), lambda k, j: (order, 0, j)),
                  pl.BlockSpec((2 * n2, 2 * n2), lambda k, j: (0, 0)),
                  pl.BlockSpec((1, 2 * n2, 2 * n2), lambda k, j: (jnp.where(mirrored(k), 1, 0), 0, 0)),
                  pl.BlockSpec((2 * n2, 2 * n2), lambda k, j: (0, 0))],
        out_specs=pl.BlockSpec(blk, lambda k, j: (0, k, 0, j)),
        out_shape=jax.ShapeDtypeStruct((2, FFT_N1, n2, d), BF16),
        compiler_params=_params("parallel", "parallel"),
        name="dft_stage_b_spectrum_product",
    )(a, af, af, scale, tb, tbf, tbt)


def _fft_ainv_kernel(y_ref, t_ref, u_ref, gate_ref, d_ref, o_ref, r_s):
    half = o_ref.shape[1]
    yt = jnp.swapaxes(y_ref[...].astype(F32), 0, 1).astype(BF16)
    for jj in range(FFT_NB):
        x = jnp.dot(t_ref[jj], yt[jj], preferred_element_type=F32)
        for c in range(2):
            r_s[c, jj] = x[c * half:(c + 1) * half]
    for c in range(2):
        o_ref[c] = gate_ref[c] * (jnp.swapaxes(r_s[c], 0, 1) + d_ref[...] * u_ref[c])


def _fft_ainv(y3, tat, u4, gate4, dskip, *, n2, d, useg, gseg):
    b, half, _, _ = u4.shape
    tn = min(128, d)
    db = d // tn
    usb, gsb = (useg * d) // tn, (gseg * d) // tn
    return pl.pallas_call(
        _fft_ainv_kernel,
        grid=(n2 // FFT_NB, db),
        in_specs=[pl.BlockSpec((2 * FFT_N1, FFT_NB, tn), lambda m, j: (0, m, j)),
                  pl.BlockSpec((FFT_NB, FFT_N1, 2 * FFT_N1), lambda m, j: (m, 0, 0)),
                  pl.BlockSpec((b, half, FFT_NB, tn), lambda m, j: (0, 0, m, usb + j)),
                  pl.BlockSpec((b, half, FFT_NB, tn), lambda m, j: (0, 0, m, gsb + j)),
                  pl.BlockSpec((1, tn), lambda m, j: (0, j))],
        out_specs=pl.BlockSpec((b, half, FFT_NB, tn), lambda m, j: (0, 0, m, j)),
        out_shape=jax.ShapeDtypeStruct((b, half, n2, d), F32),
        scratch_shapes=[pltpu.VMEM((b, FFT_NB, half, tn), F32)],
        compiler_params=_params("parallel", "parallel"),
        name="dft_stage_a_inverse_gate",
    )(y3, tat, u4, gate4, dskip)


def _hyena_mixer(h, hy, l, d):
    (w_in, b_in, sconv_w, sconv_b, f_w0, f_b0, f_w1, f_b1, f_w2, f_b2, f_freq, f_w3, d_skip, w_out) = hy
    b = h.shape[0]
    assert b == 2, "the two batch rows ride the real / imaginary planes of one complex DFT"
    n1 = FFT_N1
    half = n1 // 2
    n2 = (2 * l) // n1
    assert half * n2 == l
    z3 = _mm_conv(_hy_in_kernel, h, [w_in.astype(BF16)],
                  [b_in.reshape(1, -1), sconv_w, sconv_b.reshape(1, -1)], 3 * d, F32, "hyena_in_proj_conv", nz=2,
                  row_tile=2048)
    ta, tat, tb, tbt, taf, tbf = _dft_tables(l)
    nseg = 2 * HY_ORDER
    af, fsum = _hyena_filter_spectra_a(l, d, taf, f_w0, f_b0, f_w1, f_b1, f_w2, f_b2, f_freq, f_w3)
    af = af.reshape(nseg, 2, FILT_K1, n2, d)
    scale = 1.0 / ((fsum[0::2] + fsum[1::2]) * (2 * l))
    z4 = z3.reshape(b, half, n2, 3 * d)
    y, yseg = z4, 0
    for o in range(HY_ORDER):
        a = _fft_a(y[None], ta, n2=n2, d=d, seg=yseg)
        yb = _fft_b(a.reshape(2, n1, n2, d), af, scale, tb, tbf, tbt, o, n2=n2, d=d)
        y = _fft_ainv(yb.reshape(2 * n1, n2, d), tat, y, z4, d_skip[o].reshape(1, d),
                      n2=n2, d=d, useg=yseg, gseg=o + 1)
        yseg = 0
    return y.reshape(b, l, d), w_out.astype(BF16)


def _rope_tables(l):
    t = jnp.arange(l)
    row = (t // GRID_W).astype(F32)[:, None]
    col = (t % GRID_W).astype(F32)[:, None]
    width = 2 * HEAD_DIM
    cos, sin, avg, perm = [], [], [], []
    for dim in (A_HALF, HEAD_DIM):
        nf = dim // 4
        inv = jnp.asarray(ROPE_BASE ** (-np.arange(nf) / nf), F32)[None, :]
        ar, ac = row * inv, col * inv
        c = jnp.concatenate([jnp.cos(ar), jnp.cos(ar), jnp.cos(ac), jnp.cos(ac)], axis=1)
        s = jnp.concatenate([-jnp.sin(ar), jnp.sin(ar), -jnp.sin(ac), jnp.sin(ac)], axis=1)
        reps = width // dim
        cos.append(jnp.tile(c, (1, reps)))
        sin.append(jnp.tile(s, (1, reps)))
        lane = np.arange(width)
        avg.append((lane[:, None] // dim == lane[None, :] // dim) / dim)
        partner = np.where(lane % (2 * nf) < nf, lane + nf, lane - nf)
        perm.append((lane[:, None] == partner[None, :]).astype(np.float64))
    return (jnp.stack(cos), jnp.stack(sin), jnp.asarray(np.stack(avg), BF16), jnp.asarray(np.stack(perm), BF16))


def _attn_mixer(h, hc, lam_init, w_in, w_out, a_q_g, a_k_g, lq1, lk1, lq2, lk2, a_sub_g, b_q_g, b_k_g, b_sink):
    b, l, d = h.shape
    c = hc.shape[1]
    a_heads = d // (2 * HEAD_DIM)
    b_heads = d // (2 * HEAD_DIM)
    q_cols = (a_heads + b_heads) * HEAD_DIM
    in_cols = w_in.shape[1]
    lam = (jnp.exp(jnp.sum(lq1 * lk1)) - jnp.exp(jnp.sum(lq2 * lk2)) + lam_init).reshape(1, 1)
    sink = b_sink.reshape(B_KV_HEADS, b_heads // B_KV_HEADS)

    ones = lambda n: jnp.ones((n,), F32)
    tile = lambda v, n: jnp.tile(v, n)
    gain = jnp.concatenate([tile(a_q_g, 2 * a_heads), tile(b_q_g, b_heads), tile(a_k_g, 2 * a_heads),
                            ones(a_heads * HEAD_DIM), tile(b_k_g, B_KV_HEADS),
                            ones(B_KV_HEADS * HEAD_DIM)]).reshape(1, in_cols)
    scale = jnp.concatenate([jnp.full((a_heads * HEAD_DIM,), A_HALF ** -0.5 * math.log2(math.e), F32),
                             jnp.full((b_heads * HEAD_DIM,), HEAD_DIM ** -0.5, F32),
                             ones(in_cols - q_cols)]).reshape(1, in_cols)
    e = [0, a_heads // 2, (a_heads + b_heads) // 2, (2 * a_heads + b_heads) // 2,
         (3 * a_heads + b_heads) // 2, (3 * a_heads + b_heads) // 2 + 1, (3 * a_heads + b_heads) // 2 + 2]
    plain = ((e[3], e[4]), (e[5], e[6]))
    cos_tab, sin_tab, avg, perm = _rope_tables(l)

    def type_of(j0):
        def f(j):
            jj = j + j0
            is_a = (jj < e[1]) | ((jj >= e[2]) & (jj < e[3]))
            return jnp.where(is_a, 0, 1)
        return f

    w_bf = w_in.astype(BF16)[None]
    p = _inproj(h.reshape(b * l, d), w_bf, gain, scale, cos_tab, sin_tab, avg, perm, plain=plain,
                type_of_block=type_of(0), col0=0, ncols=in_cols, l=l, rope=True).reshape(b, l, in_cols)
    kv0 = e[2]
    plain_c = tuple((lo - kv0, hi - kv0) for lo, hi in plain)
    pc = _inproj(hc.reshape(b * c, d), w_bf, gain, scale, cos_tab, sin_tab, avg, perm, plain=plain_c,
                 type_of_block=type_of(kv0), col0=kv0, ncols=in_cols - q_cols, l=c, rope=False
                 ).reshape(b, c, in_cols - q_cols)
    score_bound = (A_HALF ** 0.5 * math.log2(math.e)) * jnp.max(jnp.abs(a_q_g)) * jnp.max(jnp.abs(a_k_g))
    oa = _diffattn(lam, p, pc, a_sub_g, score_bound, heads=a_heads, l=l, c=c, out_scale=1.0 - lam_init)
    ob = _winattn(sink, p, pc, a_heads=a_heads, b_heads=b_heads, l=l, c=c)
    w_out_bf = w_out.astype(BF16)
    na = a_heads * HEAD_DIM
    return [oa, ob], [w_out_bf[:na], w_out_bf[na:]]


def kernel(x, c, ctx, c_ctx, ada_w, ada_b, norm1_g, norm2_g, attn_w_in, attn_w_out, a_q_g, a_k_g, a_lam_q1, a_lam_k1, a_lam_q2, a_lam_k2, a_sub_g, b_q_g, b_k_g, b_sink, hy_w_in, hy_b_in, hy_sconv_w, hy_sconv_b, hy_f_w0, hy_f_b0, hy_f_w1, hy_f_b1, hy_f_w2, hy_f_b2, hy_f_freq, hy_f_w3, hy_d, hy_w_out, ffn_w_gate, ffn_w_val, ffn_conv_w, ffn_conv_b, ffn_w_down):
    b, l, d = x.shape
    depth = ada_w.shape[0]
    cc = jnp.concatenate([c, c_ctx[None, :], jnp.zeros((8 - b - 1, d), F32)], axis=0)
    m = _ada(cc, ada_w, ada_b)
    xs = x
    for layer in range(depth):
        i = layer // 2
        lat = [m[layer, :b, k * d:(k + 1) * d].reshape(b, 1, d) for k in range(6)]
        sh1, sc1, g1, sh2, sc2, g2 = lat
        h = _normmod(xs, norm1_g[layer], sc1, sh1)
        if layer % 2 == 0:
            mc = [jnp.broadcast_to(m[layer, b, k * d:(k + 1) * d].reshape(1, 1, d), (b, 1, d)) for k in range(2)]
            hc = _normmod(ctx, norm1_g[layer], mc[1], mc[0])
            parts, ws = _attn_mixer(h, hc, 0.8 - 0.6 * math.exp(-0.3 * layer), attn_w_in[i], attn_w_out[i],
                                    a_q_g[i], a_k_g[i], a_lam_q1[i], a_lam_k1[i], a_lam_q2[i], a_lam_k2[i],
                                    a_sub_g[i], b_q_g[i], b_k_g[i], b_sink[i])
        else:
            hy = (hy_w_in[i], hy_b_in[i], hy_sconv_w[i], hy_sconv_b[i], hy_f_w0[i], hy_f_b0[i], hy_f_w1[i],
                  hy_f_b1[i], hy_f_w2[i], hy_f_b2[i], hy_f_freq[i], hy_f_w3[i], hy_d[i], hy_w_out[i])
            y, w_o = _hyena_mixer(h, hy, l, d)
            parts, ws = [y], [w_o]
        xs = _mm_resid(parts, ws, xs, g1)
        h2 = _normmod(xs, norm2_g[layer], sc2, sh2)
        d_ff = ffn_w_gate.shape[2]
        hid = _mm_conv(_ffn_up_kernel, h2, [ffn_w_gate[layer].astype(BF16), ffn_w_val[layer].astype(BF16)],
                       [ffn_conv_w[layer], ffn_conv_b[layer].reshape(1, -1)], d_ff, BF16, "ffn_up_conv_glu", nz=2)
        xs = _mm_resid([hid], [ffn_w_down[layer].astype(BF16)], xs, g2)
    return xs
```
